```python
import jax, jax.numpy as jnp
from jax import lax
import numpy as np

D_MODEL = 1024
BATCH = 8
SEQ = 2048
DEPTH = 2

D_MIX = D_MODEL
HEAD_DIM = 64
D_CONV = D_MIX // 4
CONV_GROUPS = D_CONV // HEAD_DIM
D_SGU = D_MIX // 4
SGU_HEADS = D_SGU // HEAD_DIM
D_SB = D_MIX - D_CONV - D_SGU
SB_HEADS = D_SB // HEAD_DIM
D_IN = 2 * D_CONV + 2 * D_SGU + 3 * D_SB
CONV_K = 31
CHUNK = 128
Q_BLOCK = 128
FFN_CONV_K = 3
D_FF = ((8 * D_MODEL // 3 + 127) // 128) * 128
EPS = 1e-6

kernel_name = "hybrid_conv_sgu_stickbreak_block"


def _rms(x):
    xf = x.astype(jnp.float32)
    return xf * lax.rsqrt(jnp.mean(xf * xf, axis=-1, keepdims=True) + EPS)


def rmsnorm(x, g):
    return (_rms(x) * g.astype(jnp.float32)).astype(x.dtype)


def layernorm(x, g, b):
    xf = x.astype(jnp.float32)
    mu = jnp.mean(xf, axis=-1, keepdims=True)
    var = jnp.mean(jnp.square(xf - mu), axis=-1, keepdims=True)
    y = (xf - mu) * lax.rsqrt(var + EPS)
    return (y * g.astype(jnp.float32) + b.astype(jnp.float32)).astype(x.dtype)


def causal_dwconv(x, w, b):
    c = x.shape[-1]
    k = w.shape[0]
    y = lax.conv_general_dilated(
        x, w[:, None, :].astype(x.dtype), window_strides=(1,), padding=[(k - 1, 0)],
        dimension_numbers=("NWC", "WIO", "NWC"), feature_group_count=c)
    return y + b.astype(x.dtype)


def conformer_conv(a_val, a_gate, conv_w, conv_b, ln_g, ln_b):
    h = a_val * jax.nn.sigmoid(a_gate)
    h = causal_dwconv(h, conv_w, conv_b)
    h = layernorm(h, ln_g, ln_b)
    return jax.nn.silu(h)


def chunked_sgu(u, v, ln_g, ln_b, w_s, b_s):
    bsz, s, c = u.shape
    u = jax.nn.gelu(u, approximate=False)
    v = layernorm(jax.nn.gelu(v, approximate=False), ln_g, ln_b)
    v = v.reshape(bsz, s // CHUNK, CHUNK, SGU_HEADS, HEAD_DIM)
    mask = jnp.tril(jnp.ones((CHUNK, CHUNK), dtype=bool))
    ws = jnp.where(mask[None], w_s, 0.0).astype(v.dtype)
    mixed = jnp.einsum("hts,bnshd->bnthd", ws, v)
    mixed = mixed + b_s.T.astype(v.dtype)[None, None, :, :, None]
    return u * mixed.reshape(bsz, s, c)


def stick_breaking_attention(q, k, v):
    bsz, s, h, d = q.shape
    scale = d ** -0.5
    q = q.transpose(0, 2, 1, 3)
    k = k.transpose(0, 2, 1, 3)
    v = v.transpose(0, 2, 1, 3)
    outs = []
    for i in range(s // Q_BLOCK):
        q0 = i * Q_BLOCK
        n = q0 + Q_BLOCK
        qb = q[:, :, q0:n]
        kb = k[:, :, :n]
        vb = v[:, :, :n]
        z = jnp.einsum("bhqd,bhkd->bhqk", qb, kb).astype(jnp.float32) * scale
        t_pos = q0 + jnp.arange(Q_BLOCK)[:, None]
        s_pos = jnp.arange(n)[None, :]
        mask = s_pos < t_pos
        log_1m_beta = jnp.where(mask, -jax.nn.softplus(z), 0.0)
        log_beta = -jax.nn.softplus(-z)
        log_a = log_beta + lax.cumsum(log_1m_beta, axis=3, reverse=True) - log_1m_beta
        a = jnp.where(mask, jnp.exp(log_a), 0.0)
        outs.append(jnp.einsum("bhqk,bhkd->bhqd", a.astype(vb.dtype), vb))
    o = jnp.concatenate(outs, axis=2).transpose(0, 2, 1, 3)
    return o.reshape(bsz, s, h * d)


def _fwd_setup_inputs(seed: int = 0) -> dict:
    key = jax.random.key(seed)
    ks = jax.random.split(key, 20)
    f32 = jnp.float32

    def nrm(k, shape, scale):
        return jax.random.normal(k, shape, f32) * scale

    return {
        "x": nrm(ks[0], (BATCH, SEQ, D_MODEL), 1.0),
        "g_mix": 1.0 + nrm(ks[1], (DEPTH, D_MODEL), 0.02),
        "w_in": nrm(ks[2], (DEPTH, D_MODEL, D_IN), D_MODEL ** -0.5),
        "conv_w": nrm(ks[3], (DEPTH, CONV_K, D_CONV), CONV_K ** -0.5),
        "conv_b": nrm(ks[4], (DEPTH, D_CONV), 0.02),
        "conv_ln_g": 1.0 + nrm(ks[5], (DEPTH, D_CONV), 0.02),
        "conv_ln_b": nrm(ks[6], (DEPTH, D_CONV), 0.02),
        "sgu_ln_g": 1.0 + nrm(ks[7], (DEPTH, D_SGU), 0.02),
        "sgu_ln_b": nrm(ks[8], (DEPTH, D_SGU), 0.02),
        "sgu_w": nrm(ks[9], (DEPTH, SGU_HEADS, CHUNK, CHUNK), 0.5 * CHUNK ** -0.5),
        "sgu_b": 1.0 + nrm(ks[10], (DEPTH, SGU_HEADS, CHUNK), 0.02),
        "g_out": 1.0 + nrm(ks[11], (DEPTH, D_MIX), 0.02),
        "w_out": nrm(ks[12], (DEPTH, D_MIX, D_MODEL), D_MIX ** -0.5),
        "g_ffn": 1.0 + nrm(ks[13], (DEPTH, D_MODEL), 0.02),
        "w_up": nrm(ks[14], (DEPTH, D_MODEL, 2 * D_FF), D_MODEL ** -0.5),
        "ffn_conv_w": nrm(ks[15], (DEPTH, FFN_CONV_K, 2 * D_FF), FFN_CONV_K ** -0.5),
        "ffn_conv_b": nrm(ks[16], (DEPTH, 2 * D_FF), 0.02),
        "w_down": nrm(ks[17], (DEPTH, D_FF, D_MODEL), D_FF ** -0.5),
        "g_final": 1.0 + nrm(ks[18], (D_MODEL,), 0.02),
    }


def _fwd_reference(x, g_mix, w_in, conv_w, conv_b, conv_ln_g, conv_ln_b, sgu_ln_g, sgu_ln_b,
              sgu_w, sgu_b, g_out, w_out, g_ffn, w_up, ffn_conv_w, ffn_conv_b, w_down,
              g_final):
    bsz, s, _ = x.shape
    splits = np.cumsum([D_CONV, D_CONV, D_SGU, D_SGU, D_SB, D_SB]).tolist()
    for l in range(DEPTH):
        h = rmsnorm(x, g_mix[l])
        p = h @ w_in[l]
        a_val, a_gate, b_u, b_v, c_q, c_k, c_v = jnp.split(p, splits, axis=-1)
        y_a = conformer_conv(a_val, a_gate, conv_w[l], conv_b[l], conv_ln_g[l], conv_ln_b[l])
        y_b = chunked_sgu(b_u, b_v, sgu_ln_g[l], sgu_ln_b[l], sgu_w[l], sgu_b[l])
        hs = (bsz, s, SB_HEADS, HEAD_DIM)
        y_c = stick_breaking_attention(c_q.reshape(hs), c_k.reshape(hs), c_v.reshape(hs))
        y = jnp.concatenate([_rms(y_a), _rms(y_b), _rms(y_c)], axis=-1)
        y = (y * g_out[l].astype(jnp.float32)).astype(x.dtype)
        x = x + y @ w_out[l]
        h = rmsnorm(x, g_ffn[l])
        up = causal_dwconv(h @ w_up[l], ffn_conv_w[l], ffn_conv_b[l])
        gate, val = jnp.split(up, 2, axis=-1)
        x = x + (jax.nn.silu(gate) * val) @ w_down[l]
    return rmsnorm(x, g_final)


import jax as _jax
import jax.numpy as _jnp

TWIN_FORMAT = 'train_step'
FWD_PARAMS = ['x', 'g_mix', 'w_in', 'conv_w', 'conv_b', 'conv_ln_g', 'conv_ln_b', 'sgu_ln_g', 'sgu_ln_b', 'sgu_w', 'sgu_b', 'g_out', 'w_out', 'g_ffn', 'w_up', 'ffn_conv_w', 'ffn_conv_b', 'w_down', 'g_final']
TWIN_WEIGHTS = ['g_mix', 'w_in', 'conv_w', 'conv_b', 'conv_ln_g', 'conv_ln_b', 'sgu_ln_g', 'sgu_ln_b', 'sgu_w', 'sgu_b', 'g_out', 'w_out', 'g_ffn', 'w_up', 'ffn_conv_w', 'ffn_conv_b', 'w_down', 'g_final']
TWIN_DIFF_INPUT = 'x'
TWIN_INPUTS = ['x', 'g_mix', 'w_in', 'conv_w', 'conv_b', 'conv_ln_g', 'conv_ln_b', 'sgu_ln_g', 'sgu_ln_b', 'sgu_w', 'sgu_b', 'g_out', 'w_out', 'g_ffn', 'w_up', 'ffn_conv_w', 'ffn_conv_b', 'w_down', 'g_final', 'loss_target', 'm_g_mix', 'm_w_in', 'm_conv_w', 'm_conv_b', 'm_conv_ln_g', 'm_conv_ln_b', 'm_sgu_ln_g', 'm_sgu_ln_b', 'm_sgu_w', 'm_sgu_b', 'm_g_out', 'm_w_out', 'm_g_ffn', 'm_w_up', 'm_ffn_conv_w', 'm_ffn_conv_b', 'm_w_down', 'm_g_final', 'v_g_mix', 'v_w_in', 'v_conv_w', 'v_conv_b', 'v_conv_ln_g', 'v_conv_ln_b', 'v_sgu_ln_g', 'v_sgu_ln_b', 'v_sgu_w', 'v_sgu_b', 'v_g_out', 'v_w_out', 'v_g_ffn', 'v_w_up', 'v_ffn_conv_w', 'v_ffn_conv_b', 'v_w_down', 'v_g_final']
TWIN_OUTPUTS = ['loss', 'grad_x', 'grad_g_mix', 'grad_w_in', 'grad_conv_w', 'grad_conv_b', 'grad_conv_ln_g', 'grad_conv_ln_b', 'grad_sgu_ln_g', 'grad_sgu_ln_b', 'grad_sgu_w', 'grad_sgu_b', 'grad_g_out', 'grad_w_out', 'grad_g_ffn', 'grad_w_up', 'grad_ffn_conv_w', 'grad_ffn_conv_b', 'grad_w_down', 'grad_g_final', 'delta_g_mix', 'delta_w_in', 'delta_conv_w', 'delta_conv_b', 'delta_conv_ln_g', 'delta_conv_ln_b', 'delta_sgu_ln_g', 'delta_sgu_ln_b', 'delta_sgu_w', 'delta_sgu_b', 'delta_g_out', 'delta_w_out', 'delta_g_ffn', 'delta_w_up', 'delta_ffn_conv_w', 'delta_ffn_conv_b', 'delta_w_down', 'delta_g_final', 'new_m_g_mix', 'new_m_w_in', 'new_m_conv_w', 'new_m_conv_b', 'new_m_conv_ln_g', 'new_m_conv_ln_b', 'new_m_sgu_ln_g', 'new_m_sgu_ln_b', 'new_m_sgu_w', 'new_m_sgu_b', 'new_m_g_out', 'new_m_w_out', 'new_m_g_ffn', 'new_m_w_up', 'new_m_ffn_conv_w', 'new_m_ffn_conv_b', 'new_m_w_down', 'new_m_g_final', 'new_v_g_mix', 'new_v_w_in', 'new_v_conv_w', 'new_v_conv_b', 'new_v_conv_ln_g', 'new_v_conv_ln_b', 'new_v_sgu_ln_g', 'new_v_sgu_ln_b', 'new_v_sgu_w', 'new_v_sgu_b', 'new_v_g_out', 'new_v_w_out', 'new_v_g_ffn', 'new_v_w_up', 'new_v_ffn_conv_w', 'new_v_ffn_conv_b', 'new_v_w_down', 'new_v_g_final']
TWIN_LEAF_KINDS = {'loss': 'loss', 'grad_x': 'grad_x', 'grad_g_mix': 'grad_w', 'grad_w_in': 'grad_w', 'grad_conv_w': 'grad_w', 'grad_conv_b': 'grad_w', 'grad_conv_ln_g': 'grad_w', 'grad_conv_ln_b': 'grad_w', 'grad_sgu_ln_g': 'grad_w', 'grad_sgu_ln_b': 'grad_w', 'grad_sgu_w': 'grad_w', 'grad_sgu_b': 'grad_w', 'grad_g_out': 'grad_w', 'grad_w_out': 'grad_w', 'grad_g_ffn': 'grad_w', 'grad_w_up': 'grad_w', 'grad_ffn_conv_w': 'grad_w', 'grad_ffn_conv_b': 'grad_w', 'grad_w_down': 'grad_w', 'grad_g_final': 'grad_w', 'delta_g_mix': 'delta_w', 'delta_w_in': 'delta_w', 'delta_conv_w': 'delta_w', 'delta_conv_b': 'delta_w', 'delta_conv_ln_g': 'delta_w', 'delta_conv_ln_b': 'delta_w', 'delta_sgu_ln_g': 'delta_w', 'delta_sgu_ln_b': 'delta_w', 'delta_sgu_w': 'delta_w', 'delta_sgu_b': 'delta_w', 'delta_g_out': 'delta_w', 'delta_w_out': 'delta_w', 'delta_g_ffn': 'delta_w', 'delta_w_up': 'delta_w', 'delta_ffn_conv_w': 'delta_w', 'delta_ffn_conv_b': 'delta_w', 'delta_w_down': 'delta_w', 'delta_g_final': 'delta_w', 'new_m_g_mix': 'new_m', 'new_m_w_in': 'new_m', 'new_m_conv_w': 'new_m', 'new_m_conv_b': 'new_m', 'new_m_conv_ln_g': 'new_m', 'new_m_conv_ln_b': 'new_m', 'new_m_sgu_ln_g': 'new_m', 'new_m_sgu_ln_b': 'new_m', 'new_m_sgu_w': 'new_m', 'new_m_sgu_b': 'new_m', 'new_m_g_out': 'new_m', 'new_m_w_out': 'new_m', 'new_m_g_ffn': 'new_m', 'new_m_w_up': 'new_m', 'new_m_ffn_conv_w': 'new_m', 'new_m_ffn_conv_b': 'new_m', 'new_m_w_down': 'new_m', 'new_m_g_final': 'new_m', 'new_v_g_mix': 'new_v', 'new_v_w_in': 'new_v', 'new_v_conv_w': 'new_v', 'new_v_conv_b': 'new_v', 'new_v_conv_ln_g': 'new_v', 'new_v_conv_ln_b': 'new_v', 'new_v_sgu_ln_g': 'new_v', 'new_v_sgu_ln_b': 'new_v', 'new_v_sgu_w': 'new_v', 'new_v_sgu_b': 'new_v', 'new_v_g_out': 'new_v', 'new_v_w_out': 'new_v', 'new_v_g_ffn': 'new_v', 'new_v_w_up': 'new_v', 'new_v_ffn_conv_w': 'new_v', 'new_v_ffn_conv_b': 'new_v', 'new_v_w_down': 'new_v', 'new_v_g_final': 'new_v'}


def _forward(args):
    return _fwd_reference(*[args[k] for k in FWD_PARAMS])


def _output_shape():
    out = _jax.eval_shape(lambda: _forward(_fwd_setup_inputs(0)))
    return out.shape, out.dtype

N_MICROBATCH = 1
ADAM_LR = 0.001
ADAM_B1 = 0.9
ADAM_B2 = 0.999
ADAM_EPS = 1e-08
ADAM_WD = 0.01
ADAM_STEP = 10
PER_EXAMPLE_BATCH_AXIS = {'x': 0, 'loss_target': 0}
SHARED_INPUTS = []
_WEIGHT_DTYPES = {'g_mix': _jnp.float32, 'w_in': _jnp.float32, 'conv_w': _jnp.float32, 'conv_b': _jnp.float32, 'conv_ln_g': _jnp.float32, 'conv_ln_b': _jnp.float32, 'sgu_ln_g': _jnp.float32, 'sgu_ln_b': _jnp.float32, 'sgu_w': _jnp.float32, 'sgu_b': _jnp.float32, 'g_out': _jnp.float32, 'w_out': _jnp.float32, 'g_ffn': _jnp.float32, 'w_up': _jnp.float32, 'ffn_conv_w': _jnp.float32, 'ffn_conv_b': _jnp.float32, 'w_down': _jnp.float32, 'g_final': _jnp.float32}
MOMENT_SCALE = {'g_mix': 1.046674e-01, 'w_in': 6.547627e-02, 'conv_w': 9.346928e-02, 'conv_b': 2.128260e-01, 'conv_ln_g': 1.221350e-01, 'conv_ln_b': 1.277864e-01, 'sgu_ln_g': 2.927674e-02, 'sgu_ln_b': 3.140267e-02, 'sgu_w': 4.147734e-02, 'sgu_b': 5.124494e-02, 'g_out': 9.307274e-02, 'w_out': 9.215356e-02, 'g_ffn': 6.914765e-02, 'w_up': 2.939360e-02, 'ffn_conv_w': 2.914550e-02, 'ffn_conv_b': 2.976776e-02, 'w_down': 4.809966e-02, 'g_final': 1.605582e+01}


def _to_microbatches(a, axis):
    t = _jnp.moveaxis(a, axis, 0)
    t = t.reshape((N_MICROBATCH, t.shape[0] // N_MICROBATCH) + t.shape[1:])
    return _jnp.moveaxis(t, 1, axis + 1)


def setup_inputs(seed: int = 0) -> dict:
    inp = _fwd_setup_inputs(seed)
    key = _jax.random.fold_in(_jax.random.key(seed), 7919)
    shape, _ = _output_shape()
    out = dict(inp)
    out["loss_target"] = _jax.random.normal(_jax.random.fold_in(key, 0), shape, _jnp.float32)
    for i, name in enumerate(TWIN_WEIGHTS):
        w = inp[name].astype(_jnp.float32)
        if MOMENT_SCALE is None:
            s = _jnp.sqrt(_jnp.mean(_jnp.square(w)) + 1e-30)
        else:
            s = MOMENT_SCALE[name]
        km, kv = _jax.random.split(_jax.random.fold_in(key, i + 1))
        out[name] = w
        out["m_" + name] = s * _jax.random.normal(km, w.shape, _jnp.float32)
        out["v_" + name] = (s * s) * _jax.random.uniform(kv, w.shape, _jnp.float32, 0.5, 1.5)
    if N_MICROBATCH > 1:
        for name, axis in PER_EXAMPLE_BATCH_AXIS.items():
            out[name] = _to_microbatches(out[name], axis)
    return {'x': out['x'], 'g_mix': out['g_mix'], 'w_in': out['w_in'], 'conv_w': out['conv_w'], 'conv_b': out['conv_b'], 'conv_ln_g': out['conv_ln_g'], 'conv_ln_b': out['conv_ln_b'], 'sgu_ln_g': out['sgu_ln_g'], 'sgu_ln_b': out['sgu_ln_b'], 'sgu_w': out['sgu_w'], 'sgu_b': out['sgu_b'], 'g_out': out['g_out'], 'w_out': out['w_out'], 'g_ffn': out['g_ffn'], 'w_up': out['w_up'], 'ffn_conv_w': out['ffn_conv_w'], 'ffn_conv_b': out['ffn_conv_b'], 'w_down': out['w_down'], 'g_final': out['g_final'], 'loss_target': out['loss_target'], 'm_g_mix': out['m_g_mix'], 'm_w_in': out['m_w_in'], 'm_conv_w': out['m_conv_w'], 'm_conv_b': out['m_conv_b'], 'm_conv_ln_g': out['m_conv_ln_g'], 'm_conv_ln_b': out['m_conv_ln_b'], 'm_sgu_ln_g': out['m_sgu_ln_g'], 'm_sgu_ln_b': out['m_sgu_ln_b'], 'm_sgu_w': out['m_sgu_w'], 'm_sgu_b': out['m_sgu_b'], 'm_g_out': out['m_g_out'], 'm_w_out': out['m_w_out'], 'm_g_ffn': out['m_g_ffn'], 'm_w_up': out['m_w_up'], 'm_ffn_conv_w': out['m_ffn_conv_w'], 'm_ffn_conv_b': out['m_ffn_conv_b'], 'm_w_down': out['m_w_down'], 'm_g_final': out['m_g_final'], 'v_g_mix': out['v_g_mix'], 'v_w_in': out['v_w_in'], 'v_conv_w': out['v_conv_w'], 'v_conv_b': out['v_conv_b'], 'v_conv_ln_g': out['v_conv_ln_g'], 'v_conv_ln_b': out['v_conv_ln_b'], 'v_sgu_ln_g': out['v_sgu_ln_g'], 'v_sgu_ln_b': out['v_sgu_ln_b'], 'v_sgu_w': out['v_sgu_w'], 'v_sgu_b': out['v_sgu_b'], 'v_g_out': out['v_g_out'], 'v_w_out': out['v_w_out'], 'v_g_ffn': out['v_g_ffn'], 'v_w_up': out['v_w_up'], 'v_ffn_conv_w': out['v_ffn_conv_w'], 'v_ffn_conv_b': out['v_ffn_conv_b'], 'v_w_down': out['v_w_down'], 'v_g_final': out['v_g_final']}


def _loss(weights, diff, rest, loss_target):
    with _jax.named_scope("forward"):
        args = {**rest, TWIN_DIFF_INPUT: diff, **{k: w.astype(_WEIGHT_DTYPES[k]) for k, w in weights.items()}}
        y = _forward(args)
    with _jax.named_scope("loss_head"):
        err = _jnp.square(y.astype(_jnp.float32) - loss_target)
        return 0.5 * _jnp.sum(_jnp.mean(err, axis=-1)) if err.ndim else 0.5 * err


def _adamw(w, g, m, v):
    m = ADAM_B1 * m + (1.0 - ADAM_B1) * g
    v = ADAM_B2 * v + (1.0 - ADAM_B2) * _jnp.square(g)
    m_hat = m / (1.0 - ADAM_B1 ** ADAM_STEP)
    v_hat = v / (1.0 - ADAM_B2 ** ADAM_STEP)
    delta = -ADAM_LR * (m_hat / (_jnp.sqrt(v_hat) + ADAM_EPS) + ADAM_WD * w)
    return delta, m, v


def reference(x, g_mix, w_in, conv_w, conv_b, conv_ln_g, conv_ln_b, sgu_ln_g, sgu_ln_b, sgu_w, sgu_b, g_out, w_out, g_ffn, w_up, ffn_conv_w, ffn_conv_b, w_down, g_final, loss_target, m_g_mix, m_w_in, m_conv_w, m_conv_b, m_conv_ln_g, m_conv_ln_b, m_sgu_ln_g, m_sgu_ln_b, m_sgu_w, m_sgu_b, m_g_out, m_w_out, m_g_ffn, m_w_up, m_ffn_conv_w, m_ffn_conv_b, m_w_down, m_g_final, v_g_mix, v_w_in, v_conv_w, v_conv_b, v_conv_ln_g, v_conv_ln_b, v_sgu_ln_g, v_sgu_ln_b, v_sgu_w, v_sgu_b, v_g_out, v_w_out, v_g_ffn, v_w_up, v_ffn_conv_w, v_ffn_conv_b, v_w_down, v_g_final):
    given = dict(x=x, g_mix=g_mix, w_in=w_in, conv_w=conv_w, conv_b=conv_b, conv_ln_g=conv_ln_g, conv_ln_b=conv_ln_b, sgu_ln_g=sgu_ln_g, sgu_ln_b=sgu_ln_b, sgu_w=sgu_w, sgu_b=sgu_b, g_out=g_out, w_out=w_out, g_ffn=g_ffn, w_up=w_up, ffn_conv_w=ffn_conv_w, ffn_conv_b=ffn_conv_b, w_down=w_down, g_final=g_final, loss_target=loss_target, m_g_mix=m_g_mix, m_w_in=m_w_in, m_conv_w=m_conv_w, m_conv_b=m_conv_b, m_conv_ln_g=m_conv_ln_g, m_conv_ln_b=m_conv_ln_b, m_sgu_ln_g=m_sgu_ln_g, m_sgu_ln_b=m_sgu_ln_b, m_sgu_w=m_sgu_w, m_sgu_b=m_sgu_b, m_g_out=m_g_out, m_w_out=m_w_out, m_g_ffn=m_g_ffn, m_w_up=m_w_up, m_ffn_conv_w=m_ffn_conv_w, m_ffn_conv_b=m_ffn_conv_b, m_w_down=m_w_down, m_g_final=m_g_final, v_g_mix=v_g_mix, v_w_in=v_w_in, v_conv_w=v_conv_w, v_conv_b=v_conv_b, v_conv_ln_g=v_conv_ln_g, v_conv_ln_b=v_conv_ln_b, v_sgu_ln_g=v_sgu_ln_g, v_sgu_ln_b=v_sgu_ln_b, v_sgu_w=v_sgu_w, v_sgu_b=v_sgu_b, v_g_out=v_g_out, v_w_out=v_w_out, v_g_ffn=v_g_ffn, v_w_up=v_w_up, v_ffn_conv_w=v_ffn_conv_w, v_ffn_conv_b=v_ffn_conv_b, v_w_down=v_w_down, v_g_final=v_g_final)
    weights = {n: given[n] for n in TWIN_WEIGHTS}
    shared = {n: given[n] for n in SHARED_INPUTS}
    per_example = {n: given[n] for n in ['x']}
    grad_fn = _jax.value_and_grad(_loss, argnums=(0, 1))

    def one_microbatch(ex, loss_target):
        ex = dict(ex)
        diff = ex.pop(TWIN_DIFF_INPUT)
        return grad_fn(weights, diff, {**shared, **ex}, loss_target)

    if N_MICROBATCH == 1:
        loss, (grad_w, grad_x) = one_microbatch(per_example, given["loss_target"])
    else:
        def body(carry, xs):
            loss_sum, grad_sum = carry
            l_k, (gw_k, gx_k) = one_microbatch(xs[0], xs[1])
            with _jax.named_scope("update"):
                return (loss_sum + l_k, _jax.tree.map(_jnp.add, grad_sum, gw_k)), gx_k

        init = (_jnp.zeros((), _jnp.float32), _jax.tree.map(_jnp.zeros_like, weights))
        (loss, grad_w), grad_x = _jax.lax.scan(body, init, (per_example, given["loss_target"]))
    with _jax.named_scope("update"):
        delta_w, new_m, new_v = {}, {}, {}
        for n in TWIN_WEIGHTS:
            delta_w[n], new_m[n], new_v[n] = _adamw(weights[n], grad_w[n], given["m_" + n], given["v_" + n])
    return (loss, grad_x, *[grad_w[n] for n in TWIN_WEIGHTS], *[delta_w[n] for n in TWIN_WEIGHTS],
            *[new_m[n] for n in TWIN_WEIGHTS], *[new_v[n] for n in TWIN_WEIGHTS])
```

```python
import functools
import math

import jax
import jax.numpy as jnp
from jax import lax
from jax.experimental import pallas as pl
from jax.experimental.pallas import tpu as pltpu

F32 = jnp.float32
BF16 = jnp.bfloat16

N_DEV = 8
D_MODEL = 1024
HEAD_DIM = 64
D_CONV = 256
D_SGU = 256
D_SB = 512
D_AB = 2 * D_CONV + 2 * D_SGU
D_QKV = 3 * D_SB
D_IN = D_AB + D_QKV
CONV_K = 31
CONV_HALO = 32
FFN_K = 3
FFN_HALO = 8
D_FF = 2816
CHUNK = 128
EPS = 1e-6
LANES = 128

ADAM_LR = 0.001
ADAM_B1 = 0.9
ADAM_B2 = 0.999
ADAM_EPS = 1e-08
ADAM_WD = 0.01
ADAM_STEP = 10

VMEM_LIMIT = 56 * 1024 * 1024


def _cparams(sem=None):
    return pltpu.CompilerParams(dimension_semantics=sem, vmem_limit_bytes=VMEM_LIMIT)


def _tile(n, prefs=(512, 256, 128)):
    for t in prefs:
        if n % t == 0:
            return t
    return n


def _sigmoid(x):
    return 1.0 / (1.0 + jnp.exp(-x))


def _softplus(x):
    return jnp.maximum(x, 0.0) + jnp.log1p(jnp.exp(-jnp.abs(x)))


_INV_SQRT2 = 1.0 / math.sqrt(2.0)
_INV_SQRT2PI = 1.0 / math.sqrt(2.0 * math.pi)


def _gelu(x):
    return 0.5 * x * (1.0 + lax.erf(x * _INV_SQRT2))


def _gelu_grad(x):
    return 0.5 * (1.0 + lax.erf(x * _INV_SQRT2)) + x * jnp.exp(-0.5 * x * x) * _INV_SQRT2PI


def _dot(a, b, dims):
    return lax.dot_general(a, b, (dims, ((), ())), preferred_element_type=F32)


_NN = ((1,), (0,))
_NT = ((1,), (1,))
_TN = ((0,), (0,))


def _split_bf16(x):
    hi = x.astype(BF16)
    lo = (x - hi.astype(F32)).astype(BF16)
    return jnp.concatenate([hi, lo], axis=1)


def _matmul(a, b, mode, *, name, out_dtype=F32, residual=None, n_cols=None, b_col0=0):
    if mode == "nn":
        (m, k), n = a.shape, (n_cols or b.shape[1])
    elif mode == "nt":
        (m, k), n = a.shape, b.shape[0]
    else:
        (k, m), n = a.shape, b.shape[1]
    tm, tn = _tile(m), _tile(n)
    tk = k if k <= 1024 else _tile(k, (512, 256))
    nk = k // tk
    assert b_col0 % tn == 0
    j0 = b_col0 // tn

    if mode == "nn":
        a_spec = pl.BlockSpec((tm, tk), lambda i, j, kk: (i, kk))
        b_spec = pl.BlockSpec((tk, tn), lambda i, j, kk: (kk, j + j0))
        dims = _NN
    elif mode == "nt":
        a_spec = pl.BlockSpec((tm, tk), lambda i, j, kk: (i, kk))
        b_spec = pl.BlockSpec((tn, tk), lambda i, j, kk: (j, kk))
        dims = _NT
    else:
        a_spec = pl.BlockSpec((tk, tm), lambda i, j, kk: (kk, i))
        b_spec = pl.BlockSpec((tk, tn), lambda i, j, kk: (kk, j))
        dims = _TN
    o_spec = pl.BlockSpec((tm, tn), lambda i, j, kk: (i, j))
    has_res = residual is not None

    def body(*refs):
        if has_res:
            a_ref, b_ref, r_ref, o_ref, acc_ref = refs
        else:
            a_ref, b_ref, o_ref, acc_ref = refs
        kk = pl.program_id(2)
        part = _dot(a_ref[...].astype(BF16), b_ref[...].astype(BF16), dims)

        @pl.when(kk == 0)
        def _():
            acc_ref[...] = part

        @pl.when(kk > 0)
        def _():
            acc_ref[...] += part

        @pl.when(kk == nk - 1)
        def _():
            acc = acc_ref[...]
            if has_res:
                acc = acc + r_ref[...]
            o_ref[...] = acc.astype(out_dtype)

    in_specs = [a_spec, b_spec] + ([o_spec] if has_res else [])
    args = (a, b) + ((residual,) if has_res else ())
    return pl.pallas_call(
        body,
        name=name,
        grid=(m // tm, n // tn, nk),
        in_specs=in_specs,
        out_specs=o_spec,
        out_shape=jax.ShapeDtypeStruct((m, n), out_dtype),
        scratch_shapes=[pltpu.VMEM((tm, tn), F32)],
        compiler_params=_cparams(("parallel", "parallel", "arbitrary")),
    )(*args)


ROW_TILE = 256


def _rmsnorm_fwd(x, g, *, name):
    s, d = x.shape

    def body(x_ref, g_ref, h_ref):
        xv = x_ref[...]
        r = lax.rsqrt(jnp.mean(xv * xv, axis=-1, keepdims=True) + EPS)
        h_ref[...] = (xv * r * g_ref[...]).astype(BF16)

    return pl.pallas_call(
        body,
        name=name,
        grid=(s // ROW_TILE,),
        in_specs=[pl.BlockSpec((ROW_TILE, d), lambda i: (i, 0)), pl.BlockSpec((1, d), lambda i: (0, 0))],
        out_specs=pl.BlockSpec((ROW_TILE, d), lambda i: (i, 0)),
        out_shape=jax.ShapeDtypeStruct((s, d), BF16),
        compiler_params=_cparams(("parallel",)),
    )(x, g)


def _rmsnorm_bwd(x, g, dh, dres, *, name):
    s, d = x.shape

    def body(x_ref, g_ref, dh_ref, dres_ref, dx_ref, dg_ref):
        xv = x_ref[...]
        r = lax.rsqrt(jnp.mean(xv * xv, axis=-1, keepdims=True) + EPS)
        xhat = xv * r
        dhv = dh_ref[...]
        dxhat = dhv * g_ref[...]
        dx_ref[...] = dres_ref[...] + r * (dxhat - xhat * jnp.mean(dxhat * xhat, axis=-1, keepdims=True))
        part = jnp.sum(dhv * xhat, axis=0, keepdims=True)

        @pl.when(pl.program_id(0) == 0)
        def _():
            dg_ref[...] = part

        @pl.when(pl.program_id(0) > 0)
        def _():
            dg_ref[...] += part

    row = pl.BlockSpec((ROW_TILE, d), lambda i: (i, 0))
    vec = pl.BlockSpec((1, d), lambda i: (0, 0))
    return pl.pallas_call(
        body,
        name=name,
        grid=(s // ROW_TILE,),
        in_specs=[row, vec, row, row],
        out_specs=[row, vec],
        out_shape=[jax.ShapeDtypeStruct((s, d), F32), jax.ShapeDtypeStruct((1, d), F32)],
        compiler_params=_cparams(("arbitrary",)),
    )(x, g, dh, dres)


def _loss_head(x, g, target, *, name):
    s, d = x.shape

    def body(x_ref, g_ref, t_ref, loss_ref, dx_ref, dg_ref):
        xv = x_ref[...]
        gv = g_ref[...]
        r = lax.rsqrt(jnp.mean(xv * xv, axis=-1, keepdims=True) + EPS)
        xhat = xv * r
        diff = xhat * gv - t_ref[...]
        dy = diff * (1.0 / d)
        dxhat = dy * gv
        dx_ref[...] = r * (dxhat - xhat * jnp.mean(dxhat * xhat, axis=-1, keepdims=True))
        dg_part = jnp.sum(dy * xhat, axis=0, keepdims=True)
        row_loss = jnp.sum(diff * diff, axis=-1, keepdims=True)
        loss_part = jnp.sum(row_loss, axis=0, keepdims=True) * (0.5 / d)

        @pl.when(pl.program_id(0) == 0)
        def _():
            dg_ref[...] = dg_part
            loss_ref[...] = jnp.broadcast_to(loss_part, loss_ref.shape)

        @pl.when(pl.program_id(0) > 0)
        def _():
            dg_ref[...] += dg_part
            loss_ref[...] += jnp.broadcast_to(loss_part, loss_ref.shape)

    row = pl.BlockSpec((ROW_TILE, d), lambda i: (i, 0))
    vec = pl.BlockSpec((1, d), lambda i: (0, 0))
    tile = pl.BlockSpec((8, LANES), lambda i: (0, 0))
    return pl.pallas_call(
        body,
        name=name,
        grid=(s // ROW_TILE,),
        in_specs=[row, vec, row],
        out_specs=[tile, row, vec],
        out_shape=[jax.ShapeDtypeStruct((8, LANES), F32), jax.ShapeDtypeStruct((s, d), F32),
                   jax.ShapeDtypeStruct((1, d), F32)],
        compiler_params=_cparams(("arbitrary",)),
    )(x, g, target)


_BRANCHES = ((0, D_CONV), (D_CONV, D_SGU), (D_CONV + D_SGU, D_SB))


def _combine_fwd(ya, yb, yc, g, *, name):
    s = ya.shape[0]

    def body(ya_ref, yb_ref, yc_ref, g_ref, y_ref):
        for ref, (off, w) in zip((ya_ref, yb_ref, yc_ref), _BRANCHES):
            v = ref[...]
            r = lax.rsqrt(jnp.mean(v * v, axis=-1, keepdims=True) + EPS)
            y_ref[:, off:off + w] = (v * r * g_ref[:, off:off + w]).astype(BF16)

    def row(w):
        return pl.BlockSpec((ROW_TILE, w), lambda i: (i, 0))

    return pl.pallas_call(
        body,
        name=name,
        grid=(s // ROW_TILE,),
        in_specs=[row(D_CONV), row(D_SGU), row(D_SB), pl.BlockSpec((1, D_MODEL), lambda i: (0, 0))],
        out_specs=row(D_MODEL),
        out_shape=jax.ShapeDtypeStruct((s, D_MODEL), BF16),
        compiler_params=_cparams(("parallel",)),
    )(ya, yb, yc, g)


def _combine_bwd(dy, ya, yb, yc, g, *, name):
    s = ya.shape[0]

    def body(dy_ref, ya_ref, yb_ref, yc_ref, g_ref, dya_ref, dyb_ref, dyc_ref, dg_ref):
        first = pl.program_id(0) == 0
        for ref, dref, (off, w) in zip((ya_ref, yb_ref, yc_ref), (dya_ref, dyb_ref, dyc_ref), _BRANCHES):
            v = ref[...]
            r = lax.rsqrt(jnp.mean(v * v, axis=-1, keepdims=True) + EPS)
            n = v * r
            dout = dy_ref[:, off:off + w]
            dn = dout * g_ref[:, off:off + w]
            dref[...] = r * (dn - n * jnp.mean(dn * n, axis=-1, keepdims=True))
            part = jnp.sum(dout * n, axis=0, keepdims=True)

            @pl.when(first)
            def _():
                dg_ref[:, off:off + w] = part

            @pl.when(jnp.logical_not(first))
            def _():
                dg_ref[:, off:off + w] += part

    def row(w):
        return pl.BlockSpec((ROW_TILE, w), lambda i: (i, 0))

    vec = pl.BlockSpec((1, D_MODEL), lambda i: (0, 0))
    return pl.pallas_call(
        body,
        name=name,
        grid=(s // ROW_TILE,),
        in_specs=[row(D_MODEL), row(D_CONV), row(D_SGU), row(D_SB), vec],
        out_specs=[row(D_CONV), row(D_SGU), row(D_SB), vec],
        out_shape=[jax.ShapeDtypeStruct((s, D_CONV), F32), jax.ShapeDtypeStruct((s, D_SGU), F32),
                   jax.ShapeDtypeStruct((s, D_SB), F32), jax.ShapeDtypeStruct((1, D_MODEL), F32)],
        compiler_params=_cparams(("arbitrary",)),
    )(dy, ya, yb, yc, g)


CONV_TILE = 128


def _shift_down(window, j, halo):
    return pltpu.roll(window, j, 0)[halo:, :] if j else window[halo:, :]


def _shift_up(window, j, n_out):
    n = window.shape[0]
    return pltpu.roll(window, n - j, 0)[:n_out, :] if j else window[:n_out, :]


def _mixer_a_fwd(p_ab, conv_w, conv_b, ln_g, ln_b, *, name):
    s = p_ab.shape[0]
    nt = s // CONV_TILE

    def body(p_ref, w_ref, b_ref, g_ref, beta_ref, y_ref, h_ref):
        h_ref[0:CONV_HALO, :] = jnp.zeros((CONV_HALO, D_CONV), F32)

        def glu(i, c):
            t0 = pl.multiple_of(i * CONV_TILE, CONV_TILE)
            a = p_ref[pl.ds(t0, CONV_TILE), 0:D_CONV]
            gate = p_ref[pl.ds(t0, CONV_TILE), D_CONV:2 * D_CONV]
            h_ref[pl.ds(t0 + CONV_HALO, CONV_TILE), :] = a * _sigmoid(gate)
            return c

        lax.fori_loop(0, nt, glu, 0)

        def conv(i, c):
            t0 = pl.multiple_of(i * CONV_TILE, CONV_TILE)
            window = h_ref[pl.ds(t0, CONV_TILE + CONV_HALO), :]
            acc = jnp.zeros((CONV_TILE, D_CONV), F32) + b_ref[...]
            for k in range(CONV_K):
                acc = acc + w_ref[k:k + 1, :] * _shift_down(window, CONV_K - 1 - k, CONV_HALO)
            mu = jnp.mean(acc, axis=-1, keepdims=True)
            xc = acc - mu
            rstd = lax.rsqrt(jnp.mean(xc * xc, axis=-1, keepdims=True) + EPS)
            z = xc * rstd * g_ref[...] + beta_ref[...]
            y_ref[pl.ds(t0, CONV_TILE), :] = z * _sigmoid(z)
            return c

        lax.fori_loop(0, nt, conv, 0)

    full = lambda shape: pl.BlockSpec(shape, lambda i: (0, 0))
    return pl.pallas_call(
        body,
        name=name,
        grid=(1,),
        in_specs=[full((s, 2 * D_CONV)), full((CONV_K, D_CONV)), full((1, D_CONV)), full((1, D_CONV)),
                  full((1, D_CONV))],
        out_specs=full((s, D_CONV)),
        out_shape=jax.ShapeDtypeStruct((s, D_CONV), F32),
        scratch_shapes=[pltpu.VMEM((s + CONV_HALO, D_CONV), F32)],
        compiler_params=_cparams(("arbitrary",)),
    )(p_ab, conv_w, conv_b, ln_g, ln_b)


def _mixer_a_bwd(p_ab, dya, conv_w, conv_b, ln_g, ln_b, *, name):
    s = p_ab.shape[0]
    nt = s // CONV_TILE

    def body(p_ref, dy_ref, w_ref, b_ref, g_ref, beta_ref, dp_ref, dw_ref, db_ref, dg_ref, dbeta_ref, h_ref, dc_ref):
        h_ref[0:CONV_HALO, :] = jnp.zeros((CONV_HALO, D_CONV), F32)
        dc_ref[s:s + CONV_HALO, :] = jnp.zeros((CONV_HALO, D_CONV), F32)
        dw_ref[...] = jnp.zeros_like(dw_ref)
        db_ref[...] = jnp.zeros_like(db_ref)
        dg_ref[...] = jnp.zeros_like(dg_ref)
        dbeta_ref[...] = jnp.zeros_like(dbeta_ref)

        def glu(i, c):
            t0 = pl.multiple_of(i * CONV_TILE, CONV_TILE)
            a = p_ref[pl.ds(t0, CONV_TILE), 0:D_CONV]
            gate = p_ref[pl.ds(t0, CONV_TILE), D_CONV:2 * D_CONV]
            h_ref[pl.ds(t0 + CONV_HALO, CONV_TILE), :] = a * _sigmoid(gate)
            return c

        lax.fori_loop(0, nt, glu, 0)

        def conv_bwd(i, c):
            t0 = pl.multiple_of(i * CONV_TILE, CONV_TILE)
            window = h_ref[pl.ds(t0, CONV_TILE + CONV_HALO), :]
            taps = [_shift_down(window, CONV_K - 1 - k, CONV_HALO) for k in range(CONV_K)]
            acc = jnp.zeros((CONV_TILE, D_CONV), F32) + b_ref[...]
            for k in range(CONV_K):
                acc = acc + w_ref[k:k + 1, :] * taps[k]
            mu = jnp.mean(acc, axis=-1, keepdims=True)
            xc = acc - mu
            rstd = lax.rsqrt(jnp.mean(xc * xc, axis=-1, keepdims=True) + EPS)
            xhat = xc * rstd
            z = xhat * g_ref[...] + beta_ref[...]
            sg = _sigmoid(z)
            dz = dy_ref[pl.ds(t0, CONV_TILE), :] * (sg * (1.0 + z * (1.0 - sg)))
            dg_ref[...] += jnp.sum(dz * xhat, axis=0, keepdims=True)
            dbeta_ref[...] += jnp.sum(dz, axis=0, keepdims=True)
            dxhat = dz * g_ref[...]
            dc = rstd * (dxhat - jnp.mean(dxhat, axis=-1, keepdims=True)
                         - xhat * jnp.mean(dxhat * xhat, axis=-1, keepdims=True))
            dc_ref[pl.ds(t0, CONV_TILE), :] = dc
            db_ref[...] += jnp.sum(dc, axis=0, keepdims=True)
            for k in range(CONV_K):
                dw_ref[k:k + 1, :] += jnp.sum(dc * taps[k], axis=0, keepdims=True)
            return c

        lax.fori_loop(0, nt, conv_bwd, 0)

        def glu_bwd(i, c):
            t0 = pl.multiple_of(i * CONV_TILE, CONV_TILE)
            window = dc_ref[pl.ds(t0, CONV_TILE + CONV_HALO), :]
            dh = jnp.zeros((CONV_TILE, D_CONV), F32)
            for j in range(CONV_K):
                dh = dh + w_ref[CONV_K - 1 - j:CONV_K - j, :] * _shift_up(window, j, CONV_TILE)
            a = p_ref[pl.ds(t0, CONV_TILE), 0:D_CONV]
            sg = _sigmoid(p_ref[pl.ds(t0, CONV_TILE), D_CONV:2 * D_CONV])
            dp_ref[pl.ds(t0, CONV_TILE), 0:D_CONV] = dh * sg
            dp_ref[pl.ds(t0, CONV_TILE), D_CONV:2 * D_CONV] = dh * a * sg * (1.0 - sg)
            return c

        lax.fori_loop(0, nt, glu_bwd, 0)

    full = lambda shape: pl.BlockSpec(shape, lambda i: (0, 0))
    vec = jax.ShapeDtypeStruct((1, D_CONV), F32)
    return pl.pallas_call(
        body,
        name=name,
        grid=(1,),
        in_specs=[full((s, 2 * D_CONV)), full((s, D_CONV)), full((CONV_K, D_CONV)), full((1, D_CONV)),
                  full((1, D_CONV)), full((1, D_CONV))],
        out_specs=[full((s, 2 * D_CONV)), full((CONV_K, D_CONV)), full((1, D_CONV)), full((1, D_CONV)),
                   full((1, D_CONV))],
        out_shape=[jax.ShapeDtypeStruct((s, 2 * D_CONV), F32), jax.ShapeDtypeStruct((CONV_K, D_CONV), F32),
                   vec, vec, vec],
        scratch_shapes=[pltpu.VMEM((s + CONV_HALO, D_CONV), F32), pltpu.VMEM((s + CONV_HALO, D_CONV), F32)],
        compiler_params=_cparams(("arbitrary",)),
    )(p_ab, dya, conv_w, conv_b, ln_g, ln_b)


N_SGU_HEADS = D_SGU // HEAD_DIM


def _head_masks(width):
    lane = lax.broadcasted_iota(jnp.int32, (1, width), 1)
    return [(lane >= h * HEAD_DIM) & (lane < (h + 1) * HEAD_DIM) for h in range(width // HEAD_DIM)]


def _tril_mask():
    r = lax.broadcasted_iota(jnp.int32, (CHUNK, CHUNK), 0)
    c = lax.broadcasted_iota(jnp.int32, (CHUNK, CHUNK), 1)
    return c <= r


def _sgu_norm(bv, g, beta):
    vg = _gelu(bv)
    mu = jnp.mean(vg, axis=-1, keepdims=True)
    xc = vg - mu
    rstd = lax.rsqrt(jnp.mean(xc * xc, axis=-1, keepdims=True) + EPS)
    xhat = xc * rstd
    return xhat, rstd, xhat * g + beta


def _sgu_fwd(p_ab, ln_g, ln_b, w_s, bias, *, name):
    s = p_ab.shape[0]

    def body(p_ref, g_ref, beta_ref, w_ref, bias_ref, y_ref):
        u = _gelu(p_ref[:, 0:D_SGU])
        _, _, vn = _sgu_norm(p_ref[:, D_SGU:2 * D_SGU], g_ref[...], beta_ref[...])
        vb = vn.astype(BF16)
        tril = _tril_mask()
        mixed = bias_ref[...]
        for h, m in enumerate(_head_masks(D_SGU)):
            wh = jnp.where(tril, w_ref[h], 0.0).astype(BF16)
            mixed = mixed + _dot(wh, jnp.where(m, vb, jnp.zeros_like(vb)), _NN)
        y_ref[...] = u * mixed

    return pl.pallas_call(
        body,
        name=name,
        grid=(s // CHUNK,),
        in_specs=[pl.BlockSpec((CHUNK, 2 * D_SGU), lambda i: (i, 1)),
                  pl.BlockSpec((1, D_SGU), lambda i: (0, 0)), pl.BlockSpec((1, D_SGU), lambda i: (0, 0)),
                  pl.BlockSpec((N_SGU_HEADS, CHUNK, CHUNK), lambda i: (0, 0, 0)),
                  pl.BlockSpec((CHUNK, D_SGU), lambda i: (0, 0))],
        out_specs=pl.BlockSpec((CHUNK, D_SGU), lambda i: (i, 0)),
        out_shape=jax.ShapeDtypeStruct((s, D_SGU), F32),
        compiler_params=_cparams(("parallel",)),
    )(p_ab, ln_g, ln_b, w_s, bias)


def _sgu_bwd(p_ab, dyb, ln_g, ln_b, w_s, bias, *, name):
    s = p_ab.shape[0]
    n_chunks = s // CHUNK

    def body(p_ref, dy_ref, g_ref, beta_ref, w_ref, bias_ref, dp_ref, dw_ref, db_ref, dg_ref, dbeta_ref, dbias_ref):
        @pl.when(pl.program_id(0) == 0)
        def _():
            dw_ref[...] = jnp.zeros_like(dw_ref)
            dbias_ref[...] = jnp.zeros_like(dbias_ref)
            dg_ref[...] = jnp.zeros_like(dg_ref)
            dbeta_ref[...] = jnp.zeros_like(dbeta_ref)

        bu = p_ref[:, 0:D_SGU]
        bv = p_ref[:, D_SGU:2 * D_SGU]
        u = _gelu(bu)
        gv = g_ref[...]
        xhat, rstd, vn = _sgu_norm(bv, gv, beta_ref[...])
        vb = vn.astype(BF16)
        tril = _tril_mask()
        masks = _head_masks(D_SGU)
        whs = [jnp.where(tril, w_ref[h], 0.0).astype(BF16) for h in range(N_SGU_HEADS)]
        mixed = bias_ref[...]
        for h, m in enumerate(masks):
            mixed = mixed + _dot(whs[h], jnp.where(m, vb, jnp.zeros_like(vb)), _NN)
        dy = dy_ref[...]
        dp_ref[:, 0:D_SGU] = dy * mixed * _gelu_grad(bu)
        dmixed = dy * u
        dbias_ref[...] += dmixed
        dmb = dmixed.astype(BF16)
        dvn = jnp.zeros((CHUNK, D_SGU), F32)
        for h, m in enumerate(masks):
            dmh = jnp.where(m, dmb, jnp.zeros_like(dmb))
            dvn = dvn + _dot(whs[h], dmh, _TN)
            dw_ref[h] += jnp.where(tril, _dot(dmh, vb, _NT), 0.0)
        dg_ref[...] += jnp.sum(dvn * xhat, axis=0, keepdims=True)
        dbeta_ref[...] += jnp.sum(dvn, axis=0, keepdims=True)
        dxhat = dvn * gv
        dvg = rstd * (dxhat - jnp.mean(dxhat, axis=-1, keepdims=True)
                      - xhat * jnp.mean(dxhat * xhat, axis=-1, keepdims=True))
        dp_ref[:, D_SGU:2 * D_SGU] = dvg * _gelu_grad(bv)

        @pl.when(pl.program_id(0) == n_chunks - 1)
        def _():
            chan = lax.broadcasted_iota(jnp.int32, (D_SGU, LANES), 0)
            head = lax.broadcasted_iota(jnp.int32, (D_SGU, LANES), 1)
            to_head = jnp.where(chan // HEAD_DIM == head, 1.0, 0.0).astype(BF16)
            db_ref[...] = _dot(_split_bf16(dbias_ref[...]), jnp.concatenate([to_head, to_head], axis=0), _NN)

    vec = pl.BlockSpec((1, D_SGU), lambda i: (0, 0))
    wspec = pl.BlockSpec((N_SGU_HEADS, CHUNK, CHUNK), lambda i: (0, 0, 0))
    bspec = pl.BlockSpec((CHUNK, D_SGU), lambda i: (0, 0))
    return pl.pallas_call(
        body,
        name=name,
        grid=(n_chunks,),
        in_specs=[pl.BlockSpec((CHUNK, 2 * D_SGU), lambda i: (i, 1)), pl.BlockSpec((CHUNK, D_SGU), lambda i: (i, 0)),
                  vec, vec, wspec, bspec],
        out_specs=[pl.BlockSpec((CHUNK, 2 * D_SGU), lambda i: (i, 0)), wspec,
                   pl.BlockSpec((CHUNK, LANES), lambda i: (0, 0)), vec, vec],
        out_shape=[jax.ShapeDtypeStruct((s, 2 * D_SGU), F32),
                   jax.ShapeDtypeStruct((N_SGU_HEADS, CHUNK, CHUNK), F32),
                   jax.ShapeDtypeStruct((CHUNK, LANES), F32),
                   jax.ShapeDtypeStruct((1, D_SGU), F32), jax.ShapeDtypeStruct((1, D_SGU), F32)],
        scratch_shapes=[pltpu.VMEM((CHUNK, D_SGU), F32)],
        compiler_params=_cparams(("arbitrary",)),
    )(p_ab, dyb, ln_g, ln_b, w_s, bias)


N_PAIRS = D_SB // LANES
SB_SCALE = HEAD_DIM ** -0.5


def _sb_logits(qh, kb, valid):
    z = _dot(qh, kb, _NT) * SB_SCALE
    sp = _softplus(z)
    l1 = -sp if valid is None else jnp.where(valid, -sp, 0.0)
    return l1, z - sp


def _attn_fwd(qkv, *, name):
    s = qkv.shape[0]

    def body(q_ref, k_ref, v_ref, o_ref, t_ref):
        qi = pl.program_id(1)
        q = q_ref[...]
        zero = jnp.zeros_like(q)
        masks = _head_masks(LANES)
        qs = [jnp.where(m, q, zero) for m in masks]
        row = lax.broadcasted_iota(jnp.int32, (CHUNK, CHUNK), 0)
        col = lax.broadcasted_iota(jnp.int32, (CHUNK, CHUNK), 1)
        upper = (row > col).astype(BF16)
        upper2 = jnp.concatenate([upper, upper], axis=0)

        def block(j, carry, valid):
            o, c0, c1 = carry
            k0 = pl.multiple_of(j * CHUNK, CHUNK)
            kb = k_ref[pl.ds(k0, CHUNK), :]
            vb = v_ref[pl.ds(k0, CHUNK), :]
            cs = []
            for h, c in enumerate((c0, c1)):
                l1, lb = _sb_logits(qs[h], kb, valid)
                after = _dot(_split_bf16(l1), upper2, _NN) + c
                a = jnp.exp(lb + after)
                if valid is not None:
                    a = jnp.where(valid, a, 0.0)
                o = o + _dot(a.astype(BF16), jnp.where(masks[h], vb, zero), _NN)
                cs.append(c + jnp.sum(l1, axis=1, keepdims=True))
            return o, cs[0], cs[1]

        zc = jnp.zeros((CHUNK, 1), F32)
        carry = block(qi, (jnp.zeros((CHUNK, LANES), F32), zc, zc), col < row)
        o, c0, c1 = lax.fori_loop(0, qi, lambda jj, c: block(qi - 1 - jj, c, None), carry)
        o_ref[...] = o
        t_ref[:, 0:LANES] = jnp.broadcast_to(c0, (CHUNK, LANES))
        t_ref[:, LANES:2 * LANES] = jnp.broadcast_to(c1, (CHUNK, LANES))

    return pl.pallas_call(
        body,
        name=name,
        grid=(N_PAIRS, s // CHUNK),
        in_specs=[pl.BlockSpec((CHUNK, LANES), lambda p, i: (i, p)),
                  pl.BlockSpec((s, LANES), lambda p, i: (0, N_PAIRS + p)),
                  pl.BlockSpec((s, LANES), lambda p, i: (0, 2 * N_PAIRS + p))],
        out_specs=[pl.BlockSpec((CHUNK, LANES), lambda p, i: (i, p)),
                   pl.BlockSpec((CHUNK, 2 * LANES), lambda p, i: (i, p))],
        out_shape=[jax.ShapeDtypeStruct((s, D_SB), F32), jax.ShapeDtypeStruct((s, 2 * D_SB), F32)],
        compiler_params=_cparams(("parallel", "parallel")),
    )(qkv, qkv, qkv)


def _attn_bwd(qkv, t_tot, do, *, name):
    s = qkv.shape[0]

    def body(q_ref, k_ref, v_ref, t_ref, do_ref, dq_ref, dk_ref, dv_ref):
        qi = pl.program_id(1)

        @pl.when(qi == 0)
        def _():
            dk_ref[...] = jnp.zeros_like(dk_ref)
            dv_ref[...] = jnp.zeros_like(dv_ref)

        q = q_ref[...]
        dob = do_ref[...].astype(BF16)
        zero = jnp.zeros_like(q)
        masks = _head_masks(LANES)
        qs = [jnp.where(m, q, zero) for m in masks]
        dos = [jnp.where(m, dob, zero) for m in masks]
        tots = [t_ref[:, 0:LANES], t_ref[:, LANES:2 * LANES]]
        row = lax.broadcasted_iota(jnp.int32, (CHUNK, CHUNK), 0)
        col = lax.broadcasted_iota(jnp.int32, (CHUNK, CHUNK), 1)
        upto = (row <= col).astype(BF16)
        before = (row < col).astype(BF16)
        upto2 = jnp.concatenate([upto, upto], axis=0)
        before2 = jnp.concatenate([before, before], axis=0)

        def block(j, carry, valid):
            dq, cl0, cl1, cp0, cp1 = carry
            k0 = pl.multiple_of(j * CHUNK, CHUNK)
            kb = k_ref[pl.ds(k0, CHUNK), :]
            vb = v_ref[pl.ds(k0, CHUNK), :]
            dk_blk = jnp.zeros((CHUNK, LANES), F32)
            dv_blk = jnp.zeros((CHUNK, LANES), F32)
            cls, cps = [], []
            for h, (cl, cp) in enumerate(((cl0, cp0), (cl1, cp1))):
                l1, lb = _sb_logits(qs[h], kb, valid)
                prefix = _dot(_split_bf16(l1), upto2, _NN) + cl
                a = jnp.exp(lb + (tots[h] - prefix))
                if valid is not None:
                    a = jnp.where(valid, a, 0.0)
                g = a * _dot(dos[h], vb, _NT)
                pre_g = _dot(_split_bf16(g), before2, _NN) + cp
                dz = (g - (g + pre_g) * jnp.exp(lb)) * SB_SCALE
                if valid is not None:
                    dz = jnp.where(valid, dz, 0.0)
                dzb = dz.astype(BF16)
                dq = dq + _dot(dzb, jnp.where(masks[h], kb, zero), _NN)
                dk_blk = dk_blk + _dot(dzb, qs[h], _TN)
                dv_blk = dv_blk + _dot(a.astype(BF16), dos[h], _TN)
                cls.append(cl + jnp.sum(l1, axis=1, keepdims=True))
                cps.append(cp + jnp.sum(g, axis=1, keepdims=True))
            dk_ref[pl.ds(k0, CHUNK), :] += dk_blk
            dv_ref[pl.ds(k0, CHUNK), :] += dv_blk
            return dq, cls[0], cls[1], cps[0], cps[1]

        zc = jnp.zeros((CHUNK, 1), F32)
        carry = (jnp.zeros((CHUNK, LANES), F32), zc, zc, zc, zc)
        carry = lax.fori_loop(0, qi, lambda j, c: block(j, c, None), carry)
        carry = block(qi, carry, col < row)
        dq_ref[...] = carry[0]

    blk = pl.BlockSpec((CHUNK, LANES), lambda p, i: (i, p))
    col_blk = pl.BlockSpec((s, LANES), lambda p, i: (0, p))
    out = jax.ShapeDtypeStruct((s, D_SB), F32)
    return pl.pallas_call(
        body,
        name=name,
        grid=(N_PAIRS, s // CHUNK),
        in_specs=[blk,
                  pl.BlockSpec((s, LANES), lambda p, i: (0, N_PAIRS + p)),
                  pl.BlockSpec((s, LANES), lambda p, i: (0, 2 * N_PAIRS + p)),
                  pl.BlockSpec((CHUNK, 2 * LANES), lambda p, i: (i, p)),
                  blk],
        out_specs=[blk, col_blk, col_blk],
        out_shape=[out, out, out],
        compiler_params=_cparams(("parallel", "arbitrary")),
    )(qkv, qkv, qkv, t_tot, do)


FFN_TILE = 256
FFN_COLS = 256
N_FF_BLOCKS = D_FF // FFN_COLS


def _ffn_act_fwd(up0, conv_w, conv_b, *, name):
    s = up0.shape[0]
    nt = s // FFN_TILE

    def body(xg_ref, xv_ref, wg_ref, wv_ref, bg_ref, bv_ref, act_ref, pg_ref, pv_ref):
        pg_ref[0:FFN_HALO, :] = jnp.zeros((FFN_HALO, FFN_COLS), F32)
        pv_ref[0:FFN_HALO, :] = jnp.zeros((FFN_HALO, FFN_COLS), F32)
        pg_ref[FFN_HALO:, :] = xg_ref[...]
        pv_ref[FFN_HALO:, :] = xv_ref[...]

        def tile(i, c):
            t0 = pl.multiple_of(i * FFN_TILE, FFN_TILE)
            outs = []
            for p_ref, w_ref, b_ref in ((pg_ref, wg_ref, bg_ref), (pv_ref, wv_ref, bv_ref)):
                window = p_ref[pl.ds(t0, FFN_TILE + FFN_HALO), :]
                acc = b_ref[...] + w_ref[2:3, :] * window[FFN_HALO:, :]
                for j in range(1, FFN_K):
                    acc = acc + w_ref[FFN_K - 1 - j:FFN_K - j, :] * _shift_down(window, j, FFN_HALO)
                outs.append(acc)
            gate, val = outs
            act_ref[pl.ds(t0, FFN_TILE), :] = (gate * _sigmoid(gate) * val).astype(BF16)
            return c

        lax.fori_loop(0, nt, tile, 0)

    gcol = lambda rows: pl.BlockSpec((rows, FFN_COLS), lambda j: (0, j))
    vcol = lambda rows: pl.BlockSpec((rows, FFN_COLS), lambda j: (0, j + N_FF_BLOCKS))
    return pl.pallas_call(
        body,
        name=name,
        grid=(N_FF_BLOCKS,),
        in_specs=[gcol(s), vcol(s), gcol(FFN_K), vcol(FFN_K), gcol(1), vcol(1)],
        out_specs=gcol(s),
        out_shape=jax.ShapeDtypeStruct((s, D_FF), BF16),
        scratch_shapes=[pltpu.VMEM((s + FFN_HALO, FFN_COLS), F32), pltpu.VMEM((s + FFN_HALO, FFN_COLS), F32)],
        compiler_params=_cparams(("parallel",)),
    )(up0, up0, conv_w, conv_w, conv_b, conv_b)


def _ffn_act_bwd(up0, dact, conv_w, conv_b, *, name):
    s = up0.shape[0]
    nt = s // FFN_TILE

    def body(xm_ref, xp_ref, da_ref, wm_ref, wp_ref, bm_ref, bp_ref, dx_ref, dw_ref, db_ref, pm_ref, pp_ref, dm_ref):
        is_gate = pl.program_id(0) < N_FF_BLOCKS
        zeros = jnp.zeros((FFN_HALO, FFN_COLS), F32)
        pm_ref[0:FFN_HALO, :] = zeros
        pp_ref[0:FFN_HALO, :] = zeros
        pm_ref[FFN_HALO:, :] = xm_ref[...]
        pp_ref[FFN_HALO:, :] = xp_ref[...]
        dm_ref[s:s + FFN_HALO, :] = zeros
        dw_ref[...] = jnp.zeros_like(dw_ref)
        db_ref[...] = jnp.zeros_like(db_ref)

        def tile(i, c):
            t0 = pl.multiple_of(i * FFN_TILE, FFN_TILE)
            wm = pm_ref[pl.ds(t0, FFN_TILE + FFN_HALO), :]
            wp = pp_ref[pl.ds(t0, FFN_TILE + FFN_HALO), :]
            taps = [_shift_down(wm, j, FFN_HALO) for j in range(FFN_K)]
            mine = bm_ref[...]
            partner = bp_ref[...]
            for j in range(FFN_K):
                mine = mine + wm_ref[FFN_K - 1 - j:FFN_K - j, :] * taps[j]
                partner = partner + wp_ref[FFN_K - 1 - j:FFN_K - j, :] * _shift_down(wp, j, FFN_HALO)
            da = da_ref[pl.ds(t0, FFN_TILE), :]
            sg_m = _sigmoid(mine)
            sg_p = _sigmoid(partner)
            d_as_gate = da * partner * (sg_m * (1.0 + mine * (1.0 - sg_m)))
            d_as_val = da * partner * sg_p
            dm = jnp.where(is_gate, d_as_gate, d_as_val)
            dm_ref[pl.ds(t0, FFN_TILE), :] = dm
            db_ref[...] += jnp.sum(dm, axis=0, keepdims=True)
            for j in range(FFN_K):
                dw_ref[FFN_K - 1 - j:FFN_K - j, :] += jnp.sum(dm * taps[j], axis=0, keepdims=True)
            return c

        lax.fori_loop(0, nt, tile, 0)

        def tile_dx(i, c):
            t0 = pl.multiple_of(i * FFN_TILE, FFN_TILE)
            window = dm_ref[pl.ds(t0, FFN_TILE + FFN_HALO), :]
            dx = jnp.zeros((FFN_TILE, FFN_COLS), F32)
            for j in range(FFN_K):
                dx = dx + wm_ref[FFN_K - 1 - j:FFN_K - j, :] * _shift_up(window, j, FFN_TILE)
            dx_ref[pl.ds(t0, FFN_TILE), :] = dx
            return c

        lax.fori_loop(0, nt, tile_dx, 0)

    nb = 2 * N_FF_BLOCKS
    mine = lambda rows: pl.BlockSpec((rows, FFN_COLS), lambda j: (0, j))
    partner = lambda rows: pl.BlockSpec((rows, FFN_COLS), lambda j: (0, (j + N_FF_BLOCKS) % nb))
    return pl.pallas_call(
        body,
        name=name,
        grid=(nb,),
        in_specs=[mine(s), partner(s), pl.BlockSpec((s, FFN_COLS), lambda j: (0, j % N_FF_BLOCKS)),
                  mine(FFN_K), partner(FFN_K), mine(1), partner(1)],
        out_specs=[mine(s), mine(FFN_K), mine(1)],
        out_shape=[jax.ShapeDtypeStruct((s, 2 * D_FF), F32), jax.ShapeDtypeStruct((FFN_K, 2 * D_FF), F32),
                   jax.ShapeDtypeStruct((1, 2 * D_FF), F32)],
        scratch_shapes=[pltpu.VMEM((s + FFN_HALO, FFN_COLS), F32), pltpu.VMEM((s + FFN_HALO, FFN_COLS), F32),
                        pltpu.VMEM((s + FFN_HALO, FFN_COLS), F32)],
        compiler_params=_cparams(("parallel",)),
    )(up0, up0, dact, conv_w, conv_w, conv_b, conv_b)


MESH = pl.DeviceIdType.MESH


def _position():
    x, y, c = lax.axis_index("x"), lax.axis_index("y"), lax.axis_index("c")
    return x, y, c, 4 * x + 2 * y + c


def _peer(k):
    x, y, c, _ = _position()
    px = 1 - x if k & 4 else x
    py = 1 - y if k & 2 else y
    pc = 1 - c if k & 1 else c
    return (px, py, pc), 4 * px + 2 * py + pc


def _exchange(src, *, scatter, name):
    rows = src.shape[-2]

    def body(src_ref, out_ref, send_sems, recv_sems, local_sem):
        me = _position()[3]
        part = (lambda idx: src_ref.at[idx]) if scatter else (lambda idx: src_ref)
        local = pltpu.make_async_copy(part(me), out_ref.at[me], local_sem)
        local.start()
        sends = []
        for k in range(1, N_DEV):
            peer, pidx = _peer(k)
            cp = pltpu.make_async_remote_copy(src_ref=part(pidx), dst_ref=out_ref.at[me], send_sem=send_sems.at[k - 1],
                                              recv_sem=recv_sems.at[k - 1], device_id=peer, device_id_type=MESH)
            cp.start()
            sends.append(cp)
        for k in range(1, N_DEV):
            peer, pidx = _peer(k)
            pltpu.make_async_remote_copy(src_ref=part(pidx), dst_ref=out_ref.at[pidx], send_sem=send_sems.at[k - 1],
                                         recv_sem=recv_sems.at[k - 1], device_id=peer, device_id_type=MESH).wait_recv()
        for cp in sends:
            cp.wait_send()
        local.wait()

    return pl.pallas_call(
        body,
        name=name,
        in_specs=[pl.BlockSpec(memory_space=pl.ANY)],
        out_specs=pl.BlockSpec(memory_space=pl.ANY),
        out_shape=jax.ShapeDtypeStruct((N_DEV, rows, LANES), src.dtype),
        scratch_shapes=[pltpu.SemaphoreType.DMA((N_DEV - 1,)), pltpu.SemaphoreType.DMA((N_DEV - 1,)),
                        pltpu.SemaphoreType.DMA],
    )(src)


ADAM_ROWS = 256


def _adamw(parts, w, m, v, *, name):
    rows = w.shape[0]

    def body(p_ref, w_ref, m_ref, v_ref, g_ref, d_ref, m2_ref, v2_ref):
        g = p_ref[0].astype(F32)
        for k in range(1, N_DEV):
            g = g + p_ref[k].astype(F32)
        m2 = ADAM_B1 * m_ref[...] + (1.0 - ADAM_B1) * g
        v2 = ADAM_B2 * v_ref[...] + (1.0 - ADAM_B2) * (g * g)
        m_hat = m2 / (1.0 - ADAM_B1 ** ADAM_STEP)
        v_hat = v2 / (1.0 - ADAM_B2 ** ADAM_STEP)
        g_ref[...] = g
        d_ref[...] = -ADAM_LR * (m_hat / (jnp.sqrt(v_hat) + ADAM_EPS) + ADAM_WD * w_ref[...])
        m2_ref[...] = m2
        v2_ref[...] = v2

    slab = pl.BlockSpec((ADAM_ROWS, LANES), lambda i: (i, 0))
    out = jax.ShapeDtypeStruct((rows, LANES), F32)
    return pl.pallas_call(
        body,
        name=name,
        grid=(rows // ADAM_ROWS,),
        in_specs=[pl.BlockSpec((N_DEV, ADAM_ROWS, LANES), lambda i: (0, i, 0)), slab, slab, slab],
        out_specs=[slab, slab, slab, slab],
        out_shape=[out, out, out, out],
        compiler_params=_cparams(("parallel",)),
    )(parts, w, m, v)


_SHARDED = (("w_in", (2, 1024, 320), 2), ("w_out", (2, 128, 1024), 1), ("w_up", (2, 1024, 704), 2),
            ("w_down", (2, 352, 1024), 1), ("conv_w", (2, 31, 32), 2), ("ffn_conv_w", (2, 3, 704), 2))
_F32_ON_WIRE = ("conv_w", "ffn_conv_w")
_REPLICATED = (("g_mix", (2, 1024)), ("conv_b", (2, 256)), ("conv_ln_g", (2, 256)), ("conv_ln_b", (2, 256)),
               ("sgu_ln_g", (2, 256)), ("sgu_ln_b", (2, 256)), ("sgu_w", (2, 4, 128, 128)), ("sgu_b", (2, 4, 128)),
               ("g_out", (2, 1024)), ("g_ffn", (2, 1024)), ("ffn_conv_b", (2, 5632)), ("g_final", (1024,)))


def _seg_rows(n_elems):
    return -(-n_elems // LANES)


def _pack(arrays, dtype, lead=()):
    segs = []
    for a in arrays:
        flat = a.reshape(lead + (-1,)).astype(dtype)
        pad = _seg_rows(flat.shape[-1]) * LANES - flat.shape[-1]
        if pad:
            flat = jnp.pad(flat, [(0, 0)] * len(lead) + [(0, pad)])
        segs.append(flat)
    flat = jnp.concatenate(segs, axis=-1)
    rows = flat.shape[-1] // LANES
    pad_rows = -rows % ADAM_ROWS
    if pad_rows:
        flat = jnp.pad(flat, [(0, 0)] * len(lead) + [(0, pad_rows * LANES)])
    return flat.reshape(lead + (rows + pad_rows, LANES))


def _unpack(slab, sizes, lead=()):
    flat = slab.reshape(lead + (-1,))
    out, off = [], 0
    for n in sizes:
        out.append(flat[..., off:off + n])
        off += _seg_rows(n) * LANES
    return out


def _split_blocks(full, axis):
    shape = full.shape
    split = shape[:axis] + (N_DEV, shape[axis] // N_DEV) + shape[axis + 1:]
    return jnp.moveaxis(full.reshape(split), axis, 0)


def _join_blocks(blocks, axis):
    moved = jnp.moveaxis(blocks, 0, axis)
    shape = moved.shape
    return moved.reshape(shape[:axis] + (shape[axis] * shape[axis + 1],) + shape[axis + 2:])


def _layer_fwd(l, x, wt, small):
    tag = f"l{l}"
    h = _rmsnorm_fwd(x, small["g_mix"][l][None], name=f"{tag}_norm_mix")
    p_ab = _matmul(h, wt["w_in"][l], "nn", name=f"{tag}_proj_ab", n_cols=D_AB)
    qkv = _matmul(h, wt["w_in"][l], "nn", name=f"{tag}_proj_qkv", n_cols=D_QKV, b_col0=D_AB, out_dtype=BF16)
    ya = _mixer_a_fwd(p_ab, wt["conv_w"][l], small["conv_b"][l][None], small["conv_ln_g"][l][None],
                      small["conv_ln_b"][l][None], name=f"{tag}_mixer_a")
    bias = jnp.repeat(small["sgu_b"][l].T, HEAD_DIM, axis=1)
    yb = _sgu_fwd(p_ab, small["sgu_ln_g"][l][None], small["sgu_ln_b"][l][None], small["sgu_w"][l], bias,
                  name=f"{tag}_sgu")
    yc, t_tot = _attn_fwd(qkv, name=f"{tag}_attn")
    y = _combine_fwd(ya, yb, yc, small["g_out"][l][None], name=f"{tag}_combine")
    x1 = _matmul(y, wt["w_out"][l], "nn", name=f"{tag}_out_proj", residual=x)
    h2 = _rmsnorm_fwd(x1, small["g_ffn"][l][None], name=f"{tag}_norm_ffn")
    up0 = _matmul(h2, wt["w_up"][l], "nn", name=f"{tag}_up")
    act = _ffn_act_fwd(up0, wt["ffn_conv_w"][l], small["ffn_conv_b"][l][None], name=f"{tag}_ffn_act")
    x2 = _matmul(act, wt["w_down"][l], "nn", name=f"{tag}_down", residual=x1)
    saved = dict(x=x, h=h, p_ab=p_ab, qkv=qkv, ya=ya, yb=yb, yc=yc, t_tot=t_tot, y=y, x1=x1, h2=h2, up0=up0,
                 act=act, bias=bias)
    return x2, saved


def _layer_bwd(l, dx2, sv, wt, small):
    tag = f"l{l}b"
    g = {}
    dact = _matmul(dx2, wt["w_down"][l], "nt", name=f"{tag}_dact")
    g["w_down"] = _matmul(sv["act"], dx2, "tn", name=f"{tag}_dw_down")
    dup0, g["ffn_conv_w"], dfb = _ffn_act_bwd(sv["up0"], dact, wt["ffn_conv_w"][l], small["ffn_conv_b"][l][None],
                                              name=f"{tag}_ffn_act")
    g["ffn_conv_b"] = dfb[0]
    dh2 = _matmul(dup0, wt["w_up"][l], "nt", name=f"{tag}_dh2")
    g["w_up"] = _matmul(sv["h2"], dup0, "tn", name=f"{tag}_dw_up")
    dx1, dg = _rmsnorm_bwd(sv["x1"], small["g_ffn"][l][None], dh2, dx2, name=f"{tag}_norm_ffn")
    g["g_ffn"] = dg[0]
    dy = _matmul(dx1, wt["w_out"][l], "nt", name=f"{tag}_dy")
    g["w_out"] = _matmul(sv["y"], dx1, "tn", name=f"{tag}_dw_out")
    dya, dyb, dyc, dg = _combine_bwd(dy, sv["ya"], sv["yb"], sv["yc"], small["g_out"][l][None], name=f"{tag}_combine")
    g["g_out"] = dg[0]
    dq, dk, dv = _attn_bwd(sv["qkv"], sv["t_tot"], dyc, name=f"{tag}_attn")
    dp_b, g["sgu_w"], db, dg, dbeta = _sgu_bwd(sv["p_ab"], dyb, small["sgu_ln_g"][l][None], small["sgu_ln_b"][l][None],
                                               small["sgu_w"][l], sv["bias"], name=f"{tag}_sgu")
    g["sgu_b"] = db[:, :N_SGU_HEADS].T
    g["sgu_ln_g"], g["sgu_ln_b"] = dg[0], dbeta[0]
    dp_a, g["conv_w"], dcb, dg, dbeta = _mixer_a_bwd(sv["p_ab"], dya, wt["conv_w"][l], small["conv_b"][l][None],
                                                     small["conv_ln_g"][l][None], small["conv_ln_b"][l][None],
                                                     name=f"{tag}_mixer_a")
    g["conv_b"], g["conv_ln_g"], g["conv_ln_b"] = dcb[0], dg[0], dbeta[0]
    dp = jnp.concatenate([dp_a.astype(BF16), dp_b.astype(BF16), dq.astype(BF16), dk.astype(BF16), dv.astype(BF16)],
                         axis=1)
    dh = _matmul(dp, wt["w_in"][l], "nt", name=f"{tag}_dh")
    g["w_in"] = _matmul(sv["h"], dp, "tn", name=f"{tag}_dw_in")
    dx, dg = _rmsnorm_bwd(sv["x"], small["g_mix"][l][None], dh, dx1, name=f"{tag}_norm_mix")
    g["g_mix"] = dg[0]
    return dx, g


def kernel(x, g_mix, w_in, conv_w, conv_b, conv_ln_g, conv_ln_b, sgu_ln_g, sgu_ln_b, sgu_w, sgu_b, g_out, w_out, g_ffn, w_up, ffn_conv_w, ffn_conv_b, w_down, g_final, loss_target, m_g_mix, m_w_in, m_conv_w, m_conv_b, m_conv_ln_g, m_conv_ln_b, m_sgu_ln_g, m_sgu_ln_b, m_sgu_w, m_sgu_b, m_g_out, m_w_out, m_g_ffn, m_w_up, m_ffn_conv_w, m_ffn_conv_b, m_w_down, m_g_final, v_g_mix, v_w_in, v_conv_w, v_conv_b, v_conv_ln_g, v_conv_ln_b, v_sgu_ln_g, v_sgu_ln_b, v_sgu_w, v_sgu_b, v_g_out, v_w_out, v_g_ffn, v_w_up, v_ffn_conv_w, v_ffn_conv_b, v_w_down, v_g_final):
    given = dict(locals())
    n_layers = g_mix.shape[0]
    sharded = [n for n, _, _ in _SHARDED]
    replicated = [n for n, _ in _REPLICATED]
    small = {n: given[n] for n in replicated}

    on_wire = [lax.bitcast_convert_type(given[n], BF16) if n in _F32_ON_WIRE else given[n] for n in sharded]
    gathered = _exchange(_pack(on_wire, BF16), scatter=False, name="gather_weights")
    sizes = [math.prod(shape) * (2 if n in _F32_ON_WIRE else 1) for n, shape, _ in _SHARDED]
    wt = {}
    for (n, shape, axis), seg in zip(_SHARDED, _unpack(gathered, sizes, lead=(N_DEV,))):
        if n in _F32_ON_WIRE:
            seg = lax.bitcast_convert_type(seg.reshape(N_DEV, -1, 2), F32)
        wt[n] = _join_blocks(seg.reshape((N_DEV,) + shape), axis)

    xs = x[0]
    saved = []
    for l in range(n_layers):
        xs, sv = _layer_fwd(l, xs, wt, small)
        saved.append(sv)
    loss_tile, dx, dgf = _loss_head(xs, g_final[None], loss_target[0], name="loss_head")
    grads = []
    for l in reversed(range(n_layers)):
        dx, g = _layer_bwd(l, dx, saved[l], wt, small)
        grads.append(g)
    grads.reverse()
    partial = {n: jnp.stack([g[n] for g in grads]) for n in grads[0]}
    partial["g_final"] = dgf[0]

    big = _pack([_split_blocks(partial[n], axis) for n, _, axis in _SHARDED], BF16, lead=(N_DEV,))
    big = _exchange(big, scatter=True, name="scatter_grads")
    little = _exchange(_pack([partial[n] for n in replicated], F32), scatter=False, name="gather_small_grads")

    out = {}
    for names, parts in ((sharded, big), (replicated, little)):
        slabs = [_pack([given[pre + n] for n in names], F32) for pre in ("", "m_", "v_")]
        results = _adamw(parts, *slabs, name="adamw_sharded" if names is sharded else "adamw_replicated")
        sizes = [math.prod(given[n].shape) for n in names]
        for pre, slab in zip(("grad_", "delta_", "new_m_", "new_v_"), results):
            for n, flat in zip(names, _unpack(slab, sizes)):
                out[pre + n] = flat.reshape(given[n].shape)

    loss = lax.psum(loss_tile[0, 0], ("x", "y", "c"))
    order = list(_WEIGHT_ORDER)
    return (loss, dx[None], *[out["grad_" + n] for n in order], *[out["delta_" + n] for n in order],
            *[out["new_m_" + n] for n in order], *[out["new_v_" + n] for n in order])


_WEIGHT_ORDER = ("g_mix", "w_in", "conv_w", "conv_b", "conv_ln_g", "conv_ln_b", "sgu_ln_g", "sgu_ln_b", "sgu_w", "sgu_b",
                 "g_out", "w_out", "g_ffn", "w_up", "ffn_conv_w", "ffn_conv_b", "w_down", "g_final")
```

```python
import functools
import math

import jax
import jax.numpy as jnp
from jax import lax
from jax.experimental import pallas as pl
from jax.experimental.pallas import tpu as pltpu

F32 = jnp.float32
BF16 = jnp.bfloat16

N_DEV = 8
D_MODEL = 1024
HEAD_DIM = 64
D_CONV = 256
D_SGU = 256
D_SB = 512
D_AB = 2 * D_CONV + 2 * D_SGU
D_QKV = 3 * D_SB
D_IN = D_AB + D_QKV
CONV_K = 31
CONV_HALO = 32
FFN_K = 3
FFN_HALO = 8
D_FF = 2816
CHUNK = 128
EPS = 1e-6
LANES = 128

ADAM_LR = 0.001
ADAM_B1 = 0.9
ADAM_B2 = 0.999
ADAM_EPS = 1e-08
ADAM_WD = 0.01
ADAM_STEP = 10

VMEM_LIMIT = 56 * 1024 * 1024


def _cparams(sem=None):
    return pltpu.CompilerParams(dimension_semantics=sem, vmem_limit_bytes=VMEM_LIMIT)


def _tile(n, prefs=(512, 256, 128)):
    for t in prefs:
        if n % t == 0:
            return t
    return n


def _sigmoid(x):
    return 1.0 / (1.0 + jnp.exp(-x))


def _softplus(x):
    return jnp.maximum(x, 0.0) + jnp.log1p(jnp.exp(-jnp.abs(x)))


_INV_SQRT2 = 1.0 / math.sqrt(2.0)
_INV_SQRT2PI = 1.0 / math.sqrt(2.0 * math.pi)


def _gelu(x):
    return 0.5 * x * (1.0 + lax.erf(x * _INV_SQRT2))


def _gelu_grad(x):
    return 0.5 * (1.0 + lax.erf(x * _INV_SQRT2)) + x * jnp.exp(-0.5 * x * x) * _INV_SQRT2PI


def _dot(a, b, dims):
    return lax.dot_general(a, b, (dims, ((), ())), preferred_element_type=F32)


_NN = ((1,), (0,))
_NT = ((1,), (1,))
_TN = ((0,), (0,))


def _split_bf16(x):
    hi = x.astype(BF16)
    lo = (x - hi.astype(F32)).astype(BF16)
    return jnp.concatenate([hi, lo], axis=1)


def _matmul(a, b, mode, *, name, out_dtype=F32, residual=None, n=None, b_n0=0):
    if mode == "nn":
        (m, k), n = a.shape, (n or b.shape[1])
    elif mode == "nt":
        (m, k), n = a.shape, (n or b.shape[0])
    else:
        (k, m), n = a.shape, b.shape[1]
    tm, tn = _tile(m), _tile(n)
    tk = k if k <= 1024 else _tile(k, (512, 256))
    nk = k // tk
    assert b_n0 % tn == 0
    j0 = b_n0 // tn

    if mode == "nn":
        a_spec = pl.BlockSpec((tm, tk), lambda i, j, kk: (i, kk))
        b_spec = pl.BlockSpec((tk, tn), lambda i, j, kk: (kk, j + j0))
        dims = _NN
    elif mode == "nt":
        a_spec = pl.BlockSpec((tm, tk), lambda i, j, kk: (i, kk))
        b_spec = pl.BlockSpec((tn, tk), lambda i, j, kk: (j + j0, kk))
        dims = _NT
    else:
        a_spec = pl.BlockSpec((tk, tm), lambda i, j, kk: (kk, i))
        b_spec = pl.BlockSpec((tk, tn), lambda i, j, kk: (kk, j))
        dims = _TN
    o_spec = pl.BlockSpec((tm, tn), lambda i, j, kk: (i, j))
    has_res = residual is not None

    def body(*refs):
        if has_res:
            a_ref, b_ref, r_ref, o_ref, acc_ref = refs
        else:
            a_ref, b_ref, o_ref, acc_ref = refs
        kk = pl.program_id(2)
        part = _dot(a_ref[...].astype(BF16), b_ref[...].astype(BF16), dims)

        @pl.when(kk == 0)
        def _():
            acc_ref[...] = part

        @pl.when(kk > 0)
        def _():
            acc_ref[...] += part

        @pl.when(kk == nk - 1)
        def _():
            acc = acc_ref[...]
            if has_res:
                acc = acc + r_ref[...]
            o_ref[...] = acc.astype(out_dtype)

    in_specs = [a_spec, b_spec] + ([o_spec] if has_res else [])
    args = (a, b) + ((residual,) if has_res else ())
    return pl.pallas_call(
        body,
        name=name,
        grid=(m // tm, n // tn, nk),
        in_specs=in_specs,
        out_specs=o_spec,
        out_shape=jax.ShapeDtypeStruct((m, n), out_dtype),
        scratch_shapes=[pltpu.VMEM((tm, tn), F32)],
        compiler_params=_cparams(("parallel", "parallel", "arbitrary")),
    )(*args)


ROW_TILE = 256


def _rmsnorm_fwd(x, g, *, name):
    s, d = x.shape

    def body(x_ref, g_ref, h_ref):
        xv = x_ref[...]
        r = lax.rsqrt(jnp.mean(xv * xv, axis=-1, keepdims=True) + EPS)
        h_ref[...] = (xv * r * g_ref[...]).astype(BF16)

    return pl.pallas_call(
        body,
        name=name,
        grid=(s // ROW_TILE,),
        in_specs=[pl.BlockSpec((ROW_TILE, d), lambda i: (i, 0)), pl.BlockSpec((1, d), lambda i: (0, 0))],
        out_specs=pl.BlockSpec((ROW_TILE, d), lambda i: (i, 0)),
        out_shape=jax.ShapeDtypeStruct((s, d), BF16),
        compiler_params=_cparams(("parallel",)),
    )(x, g)


def _rmsnorm_bwd(x, g, dh, dres, *, name):
    s, d = x.shape

    def body(x_ref, g_ref, dh_ref, dres_ref, dx_ref, dg_ref):
        xv = x_ref[...]
        r = lax.rsqrt(jnp.mean(xv * xv, axis=-1, keepdims=True) + EPS)
        xhat = xv * r
        dhv = dh_ref[...]
        dxhat = dhv * g_ref[...]
        dx_ref[...] = dres_ref[...] + r * (dxhat - xhat * jnp.mean(dxhat * xhat, axis=-1, keepdims=True))
        part = jnp.sum(dhv * xhat, axis=0, keepdims=True)

        @pl.when(pl.program_id(0) == 0)
        def _():
            dg_ref[...] = part

        @pl.when(pl.program_id(0) > 0)
        def _():
            dg_ref[...] += part

    row = pl.BlockSpec((ROW_TILE, d), lambda i: (i, 0))
    vec = pl.BlockSpec((1, d), lambda i: (0, 0))
    return pl.pallas_call(
        body,
        name=name,
        grid=(s // ROW_TILE,),
        in_specs=[row, vec, row, row],
        out_specs=[row, vec],
        out_shape=[jax.ShapeDtypeStruct((s, d), F32), jax.ShapeDtypeStruct((1, d), F32)],
        compiler_params=_cparams(("arbitrary",)),
    )(x, g, dh, dres)


def _loss_head(x, g, target, *, name):
    s, d = x.shape

    def body(x_ref, g_ref, t_ref, loss_ref, dx_ref, dg_ref):
        xv = x_ref[...]
        gv = g_ref[...]
        r = lax.rsqrt(jnp.mean(xv * xv, axis=-1, keepdims=True) + EPS)
        xhat = xv * r
        diff = xhat * gv - t_ref[...]
        dy = diff * (1.0 / d)
        dxhat = dy * gv
        dx_ref[...] = r * (dxhat - xhat * jnp.mean(dxhat * xhat, axis=-1, keepdims=True))
        dg_part = jnp.sum(dy * xhat, axis=0, keepdims=True)
        row_loss = jnp.sum(diff * diff, axis=-1, keepdims=True)
        loss_part = jnp.sum(row_loss, axis=0, keepdims=True) * (0.5 / d)

        @pl.when(pl.program_id(0) == 0)
        def _():
            dg_ref[...] = dg_part
            loss_ref[...] = jnp.broadcast_to(loss_part, loss_ref.shape)

        @pl.when(pl.program_id(0) > 0)
        def _():
            dg_ref[...] += dg_part
            loss_ref[...] += jnp.broadcast_to(loss_part, loss_ref.shape)

    row = pl.BlockSpec((ROW_TILE, d), lambda i: (i, 0))
    vec = pl.BlockSpec((1, d), lambda i: (0, 0))
    tile = pl.BlockSpec((8, LANES), lambda i: (0, 0))
    return pl.pallas_call(
        body,
        name=name,
        grid=(s // ROW_TILE,),
        in_specs=[row, vec, row],
        out_specs=[tile, row, vec],
        out_shape=[jax.ShapeDtypeStruct((8, LANES), F32), jax.ShapeDtypeStruct((s, d), F32),
                   jax.ShapeDtypeStruct((1, d), F32)],
        compiler_params=_cparams(("arbitrary",)),
    )(x, g, target)


_BRANCHES = ((0, D_CONV), (D_CONV, D_SGU), (D_CONV + D_SGU, D_SB))


def _combine_fwd(ya, yb, yc, g, *, name):
    s = ya.shape[0]

    def body(ya_ref, yb_ref, yc_ref, g_ref, y_ref):
        for ref, (off, w) in zip((ya_ref, yb_ref, yc_ref), _BRANCHES):
            v = ref[...]
            r = lax.rsqrt(jnp.mean(v * v, axis=-1, keepdims=True) + EPS)
            y_ref[:, off:off + w] = (v * r * g_ref[:, off:off + w]).astype(BF16)

    def row(w):
        return pl.BlockSpec((ROW_TILE, w), lambda i: (i, 0))

    return pl.pallas_call(
        body,
        name=name,
        grid=(s // ROW_TILE,),
        in_specs=[row(D_CONV), row(D_SGU), row(D_SB), pl.BlockSpec((1, D_MODEL), lambda i: (0, 0))],
        out_specs=row(D_MODEL),
        out_shape=jax.ShapeDtypeStruct((s, D_MODEL), BF16),
        compiler_params=_cparams(("parallel",)),
    )(ya, yb, yc, g)


def _combine_bwd(dy, ya, yb, yc, g, *, name):
    s = ya.shape[0]

    def body(dy_ref, ya_ref, yb_ref, yc_ref, g_ref, dya_ref, dyb_ref, dyc_ref, dg_ref):
        first = pl.program_id(0) == 0
        for ref, dref, (off, w) in zip((ya_ref, yb_ref, yc_ref), (dya_ref, dyb_ref, dyc_ref), _BRANCHES):
            v = ref[...]
            r = lax.rsqrt(jnp.mean(v * v, axis=-1, keepdims=True) + EPS)
            n = v * r
            dout = dy_ref[:, off:off + w]
            dn = dout * g_ref[:, off:off + w]
            dref[...] = r * (dn - n * jnp.mean(dn * n, axis=-1, keepdims=True))
            part = jnp.sum(dout * n, axis=0, keepdims=True)

            @pl.when(first)
            def _():
                dg_ref[:, off:off + w] = part

            @pl.when(jnp.logical_not(first))
            def _():
                dg_ref[:, off:off + w] += part

    def row(w):
        return pl.BlockSpec((ROW_TILE, w), lambda i: (i, 0))

    vec = pl.BlockSpec((1, D_MODEL), lambda i: (0, 0))
    return pl.pallas_call(
        body,
        name=name,
        grid=(s // ROW_TILE,),
        in_specs=[row(D_MODEL), row(D_CONV), row(D_SGU), row(D_SB), vec],
        out_specs=[row(D_CONV), row(D_SGU), row(D_SB), vec],
        out_shape=[jax.ShapeDtypeStruct((s, D_CONV), F32), jax.ShapeDtypeStruct((s, D_SGU), F32),
                   jax.ShapeDtypeStruct((s, D_SB), F32), jax.ShapeDtypeStruct((1, D_MODEL), F32)],
        compiler_params=_cparams(("arbitrary",)),
    )(dy, ya, yb, yc, g)


CONV_TILE = 128


def _shift_down(window, j, halo):
    return pltpu.roll(window, j, 0)[halo:, :] if j else window[halo:, :]


def _shift_up(window, j, n_out):
    n = window.shape[0]
    return pltpu.roll(window, n - j, 0)[:n_out, :] if j else window[:n_out, :]


def _mixer_a_fwd(p_ab, conv_w, conv_b, ln_g, ln_b, *, name):
    s = p_ab.shape[0]
    nt = s // CONV_TILE

    def body(p_ref, w_ref, b_ref, g_ref, beta_ref, y_ref, h_ref):
        h_ref[0:CONV_HALO, :] = jnp.zeros((CONV_HALO, D_CONV), F32)

        def glu(i, c):
            t0 = pl.multiple_of(i * CONV_TILE, CONV_TILE)
            a = p_ref[pl.ds(t0, CONV_TILE), 0:D_CONV]
            gate = p_ref[pl.ds(t0, CONV_TILE), D_CONV:2 * D_CONV]
            h_ref[pl.ds(t0 + CONV_HALO, CONV_TILE), :] = a * _sigmoid(gate)
            return c

        lax.fori_loop(0, nt, glu, 0)

        def conv(i, c):
            t0 = pl.multiple_of(i * CONV_TILE, CONV_TILE)
            window = h_ref[pl.ds(t0, CONV_TILE + CONV_HALO), :]
            acc = jnp.zeros((CONV_TILE, D_CONV), F32) + b_ref[...]
            for k in range(CONV_K):
                acc = acc + w_ref[k:k + 1, :] * _shift_down(window, CONV_K - 1 - k, CONV_HALO)
            mu = jnp.mean(acc, axis=-1, keepdims=True)
            xc = acc - mu
            rstd = lax.rsqrt(jnp.mean(xc * xc, axis=-1, keepdims=True) + EPS)
            z = xc * rstd * g_ref[...] + beta_ref[...]
            y_ref[pl.ds(t0, CONV_TILE), :] = z * _sigmoid(z)
            return c

        lax.fori_loop(0, nt, conv, 0)

    full = lambda shape: pl.BlockSpec(shape, lambda i: (0, 0))
    return pl.pallas_call(
        body,
        name=name,
        grid=(1,),
        in_specs=[full((s, 2 * D_CONV)), full((CONV_K, D_CONV)), full((1, D_CONV)), full((1, D_CONV)),
                  full((1, D_CONV))],
        out_specs=full((s, D_CONV)),
        out_shape=jax.ShapeDtypeStruct((s, D_CONV), F32),
        scratch_shapes=[pltpu.VMEM((s + CONV_HALO, D_CONV), F32)],
        compiler_params=_cparams(("arbitrary",)),
    )(p_ab, conv_w, conv_b, ln_g, ln_b)


def _mixer_a_bwd(p_ab, dya, conv_w, conv_b, ln_g, ln_b, *, name):
    s = p_ab.shape[0]
    nt = s // CONV_TILE

    def body(p_ref, dy_ref, w_ref, b_ref, g_ref, beta_ref, dp_ref, dw_ref, db_ref, dg_ref, dbeta_ref, h_ref, dc_ref):
        h_ref[0:CONV_HALO, :] = jnp.zeros((CONV_HALO, D_CONV), F32)
        dc_ref[s:s + CONV_HALO, :] = jnp.zeros((CONV_HALO, D_CONV), F32)
        dw_ref[...] = jnp.zeros_like(dw_ref)
        db_ref[...] = jnp.zeros_like(db_ref)
        dg_ref[...] = jnp.zeros_like(dg_ref)
        dbeta_ref[...] = jnp.zeros_like(dbeta_ref)

        def glu(i, c):
            t0 = pl.multiple_of(i * CONV_TILE, CONV_TILE)
            a = p_ref[pl.ds(t0, CONV_TILE), 0:D_CONV]
            gate = p_ref[pl.ds(t0, CONV_TILE), D_CONV:2 * D_CONV]
            h_ref[pl.ds(t0 + CONV_HALO, CONV_TILE), :] = a * _sigmoid(gate)
            return c

        lax.fori_loop(0, nt, glu, 0)

        def conv_bwd(i, c):
            t0 = pl.multiple_of(i * CONV_TILE, CONV_TILE)
            window = h_ref[pl.ds(t0, CONV_TILE + CONV_HALO), :]
            taps = [_shift_down(window, CONV_K - 1 - k, CONV_HALO) for k in range(CONV_K)]
            acc = jnp.zeros((CONV_TILE, D_CONV), F32) + b_ref[...]
            for k in range(CONV_K):
                acc = acc + w_ref[k:k + 1, :] * taps[k]
            mu = jnp.mean(acc, axis=-1, keepdims=True)
            xc = acc - mu
            rstd = lax.rsqrt(jnp.mean(xc * xc, axis=-1, keepdims=True) + EPS)
            xhat = xc * rstd
            z = xhat * g_ref[...] + beta_ref[...]
            sg = _sigmoid(z)
            dz = dy_ref[pl.ds(t0, CONV_TILE), :] * (sg * (1.0 + z * (1.0 - sg)))
            dg_ref[...] += jnp.sum(dz * xhat, axis=0, keepdims=True)
            dbeta_ref[...] += jnp.sum(dz, axis=0, keepdims=True)
            dxhat = dz * g_ref[...]
            dc = rstd * (dxhat - jnp.mean(dxhat, axis=-1, keepdims=True)
                         - xhat * jnp.mean(dxhat * xhat, axis=-1, keepdims=True))
            dc_ref[pl.ds(t0, CONV_TILE), :] = dc
            db_ref[...] += jnp.sum(dc, axis=0, keepdims=True)
            for k in range(CONV_K):
                dw_ref[k:k + 1, :] += jnp.sum(dc * taps[k], axis=0, keepdims=True)
            return c

        lax.fori_loop(0, nt, conv_bwd, 0)

        def glu_bwd(i, c):
            t0 = pl.multiple_of(i * CONV_TILE, CONV_TILE)
            window = dc_ref[pl.ds(t0, CONV_TILE + CONV_HALO), :]
            dh = jnp.zeros((CONV_TILE, D_CONV), F32)
            for j in range(CONV_K):
                dh = dh + w_ref[CONV_K - 1 - j:CONV_K - j, :] * _shift_up(window, j, CONV_TILE)
            a = p_ref[pl.ds(t0, CONV_TILE), 0:D_CONV]
            sg = _sigmoid(p_ref[pl.ds(t0, CONV_TILE), D_CONV:2 * D_CONV])
            dp_ref[pl.ds(t0, CONV_TILE), 0:D_CONV] = dh * sg
            dp_ref[pl.ds(t0, CONV_TILE), D_CONV:2 * D_CONV] = dh * a * sg * (1.0 - sg)
            return c

        lax.fori_loop(0, nt, glu_bwd, 0)

    full = lambda shape: pl.BlockSpec(shape, lambda i: (0, 0))
    vec = jax.ShapeDtypeStruct((1, D_CONV), F32)
    return pl.pallas_call(
        body,
        name=name,
        grid=(1,),
        in_specs=[full((s, 2 * D_CONV)), full((s, D_CONV)), full((CONV_K, D_CONV)), full((1, D_CONV)),
                  full((1, D_CONV)), full((1, D_CONV))],
        out_specs=[full((s, 2 * D_CONV)), full((CONV_K, D_CONV)), full((1, D_CONV)), full((1, D_CONV)),
                   full((1, D_CONV))],
        out_shape=[jax.ShapeDtypeStruct((s, 2 * D_CONV), F32), jax.ShapeDtypeStruct((CONV_K, D_CONV), F32),
                   vec, vec, vec],
        scratch_shapes=[pltpu.VMEM((s + CONV_HALO, D_CONV), F32), pltpu.VMEM((s + CONV_HALO, D_CONV), F32)],
        compiler_params=_cparams(("arbitrary",)),
    )(p_ab, dya, conv_w, conv_b, ln_g, ln_b)


N_SGU_HEADS = D_SGU // HEAD_DIM


def _head_masks(width):
    lane = lax.broadcasted_iota(jnp.int32, (1, width), 1)
    return [(lane >= h * HEAD_DIM) & (lane < (h + 1) * HEAD_DIM) for h in range(width // HEAD_DIM)]


def _tril_mask():
    r = lax.broadcasted_iota(jnp.int32, (CHUNK, CHUNK), 0)
    c = lax.broadcasted_iota(jnp.int32, (CHUNK, CHUNK), 1)
    return c <= r


def _sgu_norm(bv, g, beta):
    vg = _gelu(bv)
    mu = jnp.mean(vg, axis=-1, keepdims=True)
    xc = vg - mu
    rstd = lax.rsqrt(jnp.mean(xc * xc, axis=-1, keepdims=True) + EPS)
    xhat = xc * rstd
    return xhat, rstd, xhat * g + beta


def _sgu_fwd(p_ab, ln_g, ln_b, w_s, bias, *, name):
    s = p_ab.shape[0]

    def body(p_ref, g_ref, beta_ref, w_ref, bias_ref, y_ref):
        u = _gelu(p_ref[:, 0:D_SGU])
        _, _, vn = _sgu_norm(p_ref[:, D_SGU:2 * D_SGU], g_ref[...], beta_ref[...])
        vb = vn.astype(BF16)
        tril = _tril_mask()
        mixed = bias_ref[...]
        for h, m in enumerate(_head_masks(D_SGU)):
            wh = jnp.where(tril, w_ref[h], 0.0).astype(BF16)
            mixed = mixed + _dot(wh, jnp.where(m, vb, jnp.zeros_like(vb)), _NN)
        y_ref[...] = u * mixed

    return pl.pallas_call(
        body,
        name=name,
        grid=(s // CHUNK,),
        in_specs=[pl.BlockSpec((CHUNK, 2 * D_SGU), lambda i: (i, 1)),
                  pl.BlockSpec((1, D_SGU), lambda i: (0, 0)), pl.BlockSpec((1, D_SGU), lambda i: (0, 0)),
                  pl.BlockSpec((N_SGU_HEADS, CHUNK, CHUNK), lambda i: (0, 0, 0)),
                  pl.BlockSpec((CHUNK, D_SGU), lambda i: (0, 0))],
        out_specs=pl.BlockSpec((CHUNK, D_SGU), lambda i: (i, 0)),
        out_shape=jax.ShapeDtypeStruct((s, D_SGU), F32),
        compiler_params=_cparams(("parallel",)),
    )(p_ab, ln_g, ln_b, w_s, bias)


def _sgu_bwd(p_ab, dyb, ln_g, ln_b, w_s, bias, *, name):
    s = p_ab.shape[0]
    n_chunks = s // CHUNK

    def body(p_ref, dy_ref, g_ref, beta_ref, w_ref, bias_ref, dp_ref, dw_ref, db_ref, dg_ref, dbeta_ref, dbias_ref):
        @pl.when(pl.program_id(0) == 0)
        def _():
            dw_ref[...] = jnp.zeros_like(dw_ref)
            dbias_ref[...] = jnp.zeros_like(dbias_ref)
            dg_ref[...] = jnp.zeros_like(dg_ref)
            dbeta_ref[...] = jnp.zeros_like(dbeta_ref)

        bu = p_ref[:, 0:D_SGU]
        bv = p_ref[:, D_SGU:2 * D_SGU]
        u = _gelu(bu)
        gv = g_ref[...]
        xhat, rstd, vn = _sgu_norm(bv, gv, beta_ref[...])
        vb = vn.astype(BF16)
        tril = _tril_mask()
        masks = _head_masks(D_SGU)
        whs = [jnp.where(tril, w_ref[h], 0.0).astype(BF16) for h in range(N_SGU_HEADS)]
        mixed = bias_ref[...]
        for h, m in enumerate(masks):
            mixed = mixed + _dot(whs[h], jnp.where(m, vb, jnp.zeros_like(vb)), _NN)
        dy = dy_ref[...]
        dp_ref[:, 0:D_SGU] = dy * mixed * _gelu_grad(bu)
        dmixed = dy * u
        dbias_ref[...] += dmixed
        dmb = dmixed.astype(BF16)
        dvn = jnp.zeros((CHUNK, D_SGU), F32)
        for h, m in enumerate(masks):
            dmh = jnp.where(m, dmb, jnp.zeros_like(dmb))
            dvn = dvn + _dot(whs[h], dmh, _TN)
            dw_ref[h] += jnp.where(tril, _dot(dmh, vb, _NT), 0.0)
        dg_ref[...] += jnp.sum(dvn * xhat, axis=0, keepdims=True)
        dbeta_ref[...] += jnp.sum(dvn, axis=0, keepdims=True)
        dxhat = dvn * gv
        dvg = rstd * (dxhat - jnp.mean(dxhat, axis=-1, keepdims=True)
                      - xhat * jnp.mean(dxhat * xhat, axis=-1, keepdims=True))
        dp_ref[:, D_SGU:2 * D_SGU] = dvg * _gelu_grad(bv)

        @pl.when(pl.program_id(0) == n_chunks - 1)
        def _():
            chan = lax.broadcasted_iota(jnp.int32, (D_SGU, LANES), 0)
            head = lax.broadcasted_iota(jnp.int32, (D_SGU, LANES), 1)
            to_head = jnp.where(chan // HEAD_DIM == head, 1.0, 0.0).astype(BF16)
            db_ref[...] = _dot(_split_bf16(dbias_ref[...]), jnp.concatenate([to_head, to_head], axis=0), _NN)

    vec = pl.BlockSpec((1, D_SGU), lambda i: (0, 0))
    wspec = pl.BlockSpec((N_SGU_HEADS, CHUNK, CHUNK), lambda i: (0, 0, 0))
    bspec = pl.BlockSpec((CHUNK, D_SGU), lambda i: (0, 0))
    return pl.pallas_call(
        body,
        name=name,
        grid=(n_chunks,),
        in_specs=[pl.BlockSpec((CHUNK, 2 * D_SGU), lambda i: (i, 1)), pl.BlockSpec((CHUNK, D_SGU), lambda i: (i, 0)),
                  vec, vec, wspec, bspec],
        out_specs=[pl.BlockSpec((CHUNK, 2 * D_SGU), lambda i: (i, 0)), wspec,
                   pl.BlockSpec((CHUNK, LANES), lambda i: (0, 0)), vec, vec],
        out_shape=[jax.ShapeDtypeStruct((s, 2 * D_SGU), F32),
                   jax.ShapeDtypeStruct((N_SGU_HEADS, CHUNK, CHUNK), F32),
                   jax.ShapeDtypeStruct((CHUNK, LANES), F32),
                   jax.ShapeDtypeStruct((1, D_SGU), F32), jax.ShapeDtypeStruct((1, D_SGU), F32)],
        scratch_shapes=[pltpu.VMEM((CHUNK, D_SGU), F32)],
        compiler_params=_cparams(("arbitrary",)),
    )(p_ab, dyb, ln_g, ln_b, w_s, bias)


N_PAIRS = D_SB // LANES
SB_SCALE = HEAD_DIM ** -0.5


def _sb_logits(qh, kb, valid):
    z = _dot(qh, kb, _NT) * SB_SCALE
    sp = _softplus(z)
    l1 = -sp if valid is None else jnp.where(valid, -sp, 0.0)
    return l1, z - sp


def _attn_fwd(qkv, *, name):
    s = qkv.shape[0]

    def body(q_ref, k_ref, v_ref, o_ref, t_ref):
        qi = pl.program_id(1)
        q = q_ref[...]
        zero = jnp.zeros_like(q)
        masks = _head_masks(LANES)
        qs = [jnp.where(m, q, zero) for m in masks]
        row = lax.broadcasted_iota(jnp.int32, (CHUNK, CHUNK), 0)
        col = lax.broadcasted_iota(jnp.int32, (CHUNK, CHUNK), 1)
        upper = (row > col).astype(BF16)
        upper2 = jnp.concatenate([upper, upper], axis=0)

        def block(j, carry, valid):
            o, c0, c1 = carry
            k0 = pl.multiple_of(j * CHUNK, CHUNK)
            kb = k_ref[pl.ds(k0, CHUNK), :]
            vb = v_ref[pl.ds(k0, CHUNK), :]
            cs = []
            for h, c in enumerate((c0, c1)):
                l1, lb = _sb_logits(qs[h], kb, valid)
                after = _dot(_split_bf16(l1), upper2, _NN) + c
                a = jnp.exp(lb + after)
                if valid is not None:
                    a = jnp.where(valid, a, 0.0)
                o = o + _dot(a.astype(BF16), jnp.where(masks[h], vb, zero), _NN)
                cs.append(c + jnp.sum(l1, axis=1, keepdims=True))
            return o, cs[0], cs[1]

        zc = jnp.zeros((CHUNK, 1), F32)
        carry = block(qi, (jnp.zeros((CHUNK, LANES), F32), zc, zc), col < row)
        o, c0, c1 = lax.fori_loop(0, qi, lambda jj, c: block(qi - 1 - jj, c, None), carry)
        o_ref[...] = o
        t_ref[:, 0:LANES] = jnp.broadcast_to(c0, (CHUNK, LANES))
        t_ref[:, LANES:2 * LANES] = jnp.broadcast_to(c1, (CHUNK, LANES))

    return pl.pallas_call(
        body,
        name=name,
        grid=(N_PAIRS, s // CHUNK),
        in_specs=[pl.BlockSpec((CHUNK, LANES), lambda p, i: (i, p)),
                  pl.BlockSpec((s, LANES), lambda p, i: (0, N_PAIRS + p)),
                  pl.BlockSpec((s, LANES), lambda p, i: (0, 2 * N_PAIRS + p))],
        out_specs=[pl.BlockSpec((CHUNK, LANES), lambda p, i: (i, p)),
                   pl.BlockSpec((CHUNK, 2 * LANES), lambda p, i: (i, p))],
        out_shape=[jax.ShapeDtypeStruct((s, D_SB), F32), jax.ShapeDtypeStruct((s, 2 * D_SB), F32)],
        compiler_params=_cparams(("parallel", "parallel")),
    )(qkv, qkv, qkv)


def _attn_bwd(qkv, t_tot, do, *, name):
    s = qkv.shape[0]

    def body(q_ref, k_ref, v_ref, t_ref, do_ref, dq_ref, dk_ref, dv_ref):
        qi = pl.program_id(1)

        @pl.when(qi == 0)
        def _():
            dk_ref[...] = jnp.zeros_like(dk_ref)
            dv_ref[...] = jnp.zeros_like(dv_ref)

        q = q_ref[...]
        dob = do_ref[...].astype(BF16)
        zero = jnp.zeros_like(q)
        masks = _head_masks(LANES)
        qs = [jnp.where(m, q, zero) for m in masks]
        dos = [jnp.where(m, dob, zero) for m in masks]
        tots = [t_ref[:, 0:LANES], t_ref[:, LANES:2 * LANES]]
        row = lax.broadcasted_iota(jnp.int32, (CHUNK, CHUNK), 0)
        col = lax.broadcasted_iota(jnp.int32, (CHUNK, CHUNK), 1)
        upto = (row <= col).astype(BF16)
        before = (row < col).astype(BF16)
        upto2 = jnp.concatenate([upto, upto], axis=0)
        before2 = jnp.concatenate([before, before], axis=0)

        def block(j, carry, valid):
            dq, cl0, cl1, cp0, cp1 = carry
            k0 = pl.multiple_of(j * CHUNK, CHUNK)
            kb = k_ref[pl.ds(k0, CHUNK), :]
            vb = v_ref[pl.ds(k0, CHUNK), :]
            dk_blk = jnp.zeros((CHUNK, LANES), F32)
            dv_blk = jnp.zeros((CHUNK, LANES), F32)
            cls, cps = [], []
            for h, (cl, cp) in enumerate(((cl0, cp0), (cl1, cp1))):
                l1, lb = _sb_logits(qs[h], kb, valid)
                prefix = _dot(_split_bf16(l1), upto2, _NN) + cl
                a = jnp.exp(lb + (tots[h] - prefix))
                if valid is not None:
                    a = jnp.where(valid, a, 0.0)
                g = a * _dot(dos[h], vb, _NT)
                pre_g = _dot(_split_bf16(g), before2, _NN) + cp
                dz = (g - (g + pre_g) * jnp.exp(lb)) * SB_SCALE
                if valid is not None:
                    dz = jnp.where(valid, dz, 0.0)
                dzb = dz.astype(BF16)
                dq = dq + _dot(dzb, jnp.where(masks[h], kb, zero), _NN)
                dk_blk = dk_blk + _dot(dzb, qs[h], _TN)
                dv_blk = dv_blk + _dot(a.astype(BF16), dos[h], _TN)
                cls.append(cl + jnp.sum(l1, axis=1, keepdims=True))
                cps.append(cp + jnp.sum(g, axis=1, keepdims=True))
            dk_ref[pl.ds(k0, CHUNK), :] += dk_blk
            dv_ref[pl.ds(k0, CHUNK), :] += dv_blk
            return dq, cls[0], cls[1], cps[0], cps[1]

        zc = jnp.zeros((CHUNK, 1), F32)
        carry = (jnp.zeros((CHUNK, LANES), F32), zc, zc, zc, zc)
        carry = lax.fori_loop(0, qi, lambda j, c: block(j, c, None), carry)
        carry = block(qi, carry, col < row)
        dq_ref[...] = carry[0]

    blk = pl.BlockSpec((CHUNK, LANES), lambda p, i: (i, p))
    col_blk = pl.BlockSpec((s, LANES), lambda p, i: (0, p))
    out = jax.ShapeDtypeStruct((s, D_SB), F32)
    return pl.pallas_call(
        body,
        name=name,
        grid=(N_PAIRS, s // CHUNK),
        in_specs=[blk,
                  pl.BlockSpec((s, LANES), lambda p, i: (0, N_PAIRS + p)),
                  pl.BlockSpec((s, LANES), lambda p, i: (0, 2 * N_PAIRS + p)),
                  pl.BlockSpec((CHUNK, 2 * LANES), lambda p, i: (i, p)),
                  blk],
        out_specs=[blk, col_blk, col_blk],
        out_shape=[out, out, out],
        compiler_params=_cparams(("parallel", "arbitrary")),
    )(qkv, qkv, qkv, t_tot, do)


FFN_TILE = 256
FFN_COLS = 256
N_FF_BLOCKS = D_FF // FFN_COLS


def _ffn_act_fwd(up0, conv_w, conv_b, *, name):
    s = up0.shape[0]
    nt = s // FFN_TILE

    def body(xg_ref, xv_ref, wg_ref, wv_ref, bg_ref, bv_ref, act_ref, pg_ref, pv_ref):
        pg_ref[0:FFN_HALO, :] = jnp.zeros((FFN_HALO, FFN_COLS), F32)
        pv_ref[0:FFN_HALO, :] = jnp.zeros((FFN_HALO, FFN_COLS), F32)
        pg_ref[FFN_HALO:, :] = xg_ref[...]
        pv_ref[FFN_HALO:, :] = xv_ref[...]

        def tile(i, c):
            t0 = pl.multiple_of(i * FFN_TILE, FFN_TILE)
            outs = []
            for p_ref, w_ref, b_ref in ((pg_ref, wg_ref, bg_ref), (pv_ref, wv_ref, bv_ref)):
                window = p_ref[pl.ds(t0, FFN_TILE + FFN_HALO), :]
                acc = b_ref[...] + w_ref[2:3, :] * window[FFN_HALO:, :]
                for j in range(1, FFN_K):
                    acc = acc + w_ref[FFN_K - 1 - j:FFN_K - j, :] * _shift_down(window, j, FFN_HALO)
                outs.append(acc)
            gate, val = outs
            act_ref[pl.ds(t0, FFN_TILE), :] = (gate * _sigmoid(gate) * val).astype(BF16)
            return c

        lax.fori_loop(0, nt, tile, 0)

    gcol = lambda rows: pl.BlockSpec((rows, FFN_COLS), lambda j: (0, j))
    vcol = lambda rows: pl.BlockSpec((rows, FFN_COLS), lambda j: (0, j + N_FF_BLOCKS))
    return pl.pallas_call(
        body,
        name=name,
        grid=(N_FF_BLOCKS,),
        in_specs=[gcol(s), vcol(s), gcol(FFN_K), vcol(FFN_K), gcol(1), vcol(1)],
        out_specs=gcol(s),
        out_shape=jax.ShapeDtypeStruct((s, D_FF), BF16),
        scratch_shapes=[pltpu.VMEM((s + FFN_HALO, FFN_COLS), F32), pltpu.VMEM((s + FFN_HALO, FFN_COLS), F32)],
        compiler_params=_cparams(("parallel",)),
    )(up0, up0, conv_w, conv_w, conv_b, conv_b)


def _ffn_act_bwd(up0, dact, conv_w, conv_b, *, name):
    s = up0.shape[0]
    nt = s // FFN_TILE

    def body(xm_ref, xp_ref, da_ref, wm_ref, wp_ref, bm_ref, bp_ref, dx_ref, dw_ref, db_ref, pm_ref, pp_ref, dm_ref):
        is_gate = pl.program_id(0) < N_FF_BLOCKS
        zeros = jnp.zeros((FFN_HALO, FFN_COLS), F32)
        pm_ref[0:FFN_HALO, :] = zeros
        pp_ref[0:FFN_HALO, :] = zeros
        pm_ref[FFN_HALO:, :] = xm_ref[...]
        pp_ref[FFN_HALO:, :] = xp_ref[...]
        dm_ref[s:s + FFN_HALO, :] = zeros
        dw_ref[...] = jnp.zeros_like(dw_ref)
        db_ref[...] = jnp.zeros_like(db_ref)

        def tile(i, c):
            t0 = pl.multiple_of(i * FFN_TILE, FFN_TILE)
            wm = pm_ref[pl.ds(t0, FFN_TILE + FFN_HALO), :]
            wp = pp_ref[pl.ds(t0, FFN_TILE + FFN_HALO), :]
            taps = [_shift_down(wm, j, FFN_HALO) for j in range(FFN_K)]
            mine = bm_ref[...]
            partner = bp_ref[...]
            for j in range(FFN_K):
                mine = mine + wm_ref[FFN_K - 1 - j:FFN_K - j, :] * taps[j]
                partner = partner + wp_ref[FFN_K - 1 - j:FFN_K - j, :] * _shift_down(wp, j, FFN_HALO)
            da = da_ref[pl.ds(t0, FFN_TILE), :]
            sg_m = _sigmoid(mine)
            sg_p = _sigmoid(partner)
            d_as_gate = da * partner * (sg_m * (1.0 + mine * (1.0 - sg_m)))
            d_as_val = da * partner * sg_p
            dm = jnp.where(is_gate, d_as_gate, d_as_val)
            dm_ref[pl.ds(t0, FFN_TILE), :] = dm
            db_ref[...] += jnp.sum(dm, axis=0, keepdims=True)
            for j in range(FFN_K):
                dw_ref[FFN_K - 1 - j:FFN_K - j, :] += jnp.sum(dm * taps[j], axis=0, keepdims=True)
            return c

        lax.fori_loop(0, nt, tile, 0)

        def tile_dx(i, c):
            t0 = pl.multiple_of(i * FFN_TILE, FFN_TILE)
            window = dm_ref[pl.ds(t0, FFN_TILE + FFN_HALO), :]
            dx = jnp.zeros((FFN_TILE, FFN_COLS), F32)
            for j in range(FFN_K):
                dx = dx + wm_ref[FFN_K - 1 - j:FFN_K - j, :] * _shift_up(window, j, FFN_TILE)
            dx_ref[pl.ds(t0, FFN_TILE), :] = dx
            return c

        lax.fori_loop(0, nt, tile_dx, 0)

    nb = 2 * N_FF_BLOCKS
    mine = lambda rows: pl.BlockSpec((rows, FFN_COLS), lambda j: (0, j))
    partner = lambda rows: pl.BlockSpec((rows, FFN_COLS), lambda j: (0, (j + N_FF_BLOCKS) % nb))
    return pl.pallas_call(
        body,
        name=name,
        grid=(nb,),
        in_specs=[mine(s), partner(s), pl.BlockSpec((s, FFN_COLS), lambda j: (0, j % N_FF_BLOCKS)),
                  mine(FFN_K), partner(FFN_K), mine(1), partner(1)],
        out_specs=[mine(s), mine(FFN_K), mine(1)],
        out_shape=[jax.ShapeDtypeStruct((s, 2 * D_FF), F32), jax.ShapeDtypeStruct((FFN_K, 2 * D_FF), F32),
                   jax.ShapeDtypeStruct((1, 2 * D_FF), F32)],
        scratch_shapes=[pltpu.VMEM((s + FFN_HALO, FFN_COLS), F32), pltpu.VMEM((s + FFN_HALO, FFN_COLS), F32),
                        pltpu.VMEM((s + FFN_HALO, FFN_COLS), F32)],
        compiler_params=_cparams(("parallel",)),
    )(up0, up0, dact, conv_w, conv_w, conv_b, conv_b)


MESH = pl.DeviceIdType.MESH


def _position():
    x, y, c = lax.axis_index("x"), lax.axis_index("y"), lax.axis_index("c")
    return x, y, c, 4 * x + 2 * y + c


def _peer(k):
    x, y, c, _ = _position()
    px = 1 - x if k & 4 else x
    py = 1 - y if k & 2 else y
    pc = 1 - c if k & 1 else c
    return (px, py, pc), 4 * px + 2 * py + pc


def _exchange(src, *, kind, name):
    scatter = kind.startswith("scatter")
    by_rows = kind.endswith("rows")
    if by_rows:
        r = src.shape[0] // N_DEV if scatter else src.shape[0]
        out_shape = (N_DEV, r, src.shape[1]) if scatter else (N_DEV * r, src.shape[1])
    else:
        r = None
        out_shape = src.shape if scatter else (N_DEV,) + src.shape

    def body(src_ref, out_ref, send_sems, recv_sems, local_sem):
        me = _position()[3]

        def rows(ref, idx):
            return ref.at[pl.ds(pl.multiple_of(idx * r, r), r), :]

        def outgoing(idx):
            if not scatter:
                return src_ref
            return rows(src_ref, idx) if by_rows else src_ref.at[idx]

        def slot(idx):
            return rows(out_ref, idx) if (by_rows and not scatter) else out_ref.at[idx]

        local = pltpu.make_async_copy(outgoing(me), slot(me), local_sem)
        local.start()
        sends = []
        for k in range(1, N_DEV):
            peer, pidx = _peer(k)
            cp = pltpu.make_async_remote_copy(src_ref=outgoing(pidx), dst_ref=slot(me), send_sem=send_sems.at[k - 1],
                                              recv_sem=recv_sems.at[k - 1], device_id=peer, device_id_type=MESH)
            cp.start()
            sends.append(cp)
        for k in range(1, N_DEV):
            peer, pidx = _peer(k)
            pltpu.make_async_remote_copy(src_ref=outgoing(pidx), dst_ref=slot(pidx), send_sem=send_sems.at[k - 1],
                                         recv_sem=recv_sems.at[k - 1], device_id=peer, device_id_type=MESH).wait_recv()
        for cp in sends:
            cp.wait_send()
        local.wait()

    return pl.pallas_call(
        body,
        name=name,
        in_specs=[pl.BlockSpec(memory_space=pl.ANY)],
        out_specs=pl.BlockSpec(memory_space=pl.ANY),
        out_shape=jax.ShapeDtypeStruct(out_shape, src.dtype),
        scratch_shapes=[pltpu.SemaphoreType.DMA((N_DEV - 1,)), pltpu.SemaphoreType.DMA((N_DEV - 1,)),
                        pltpu.SemaphoreType.DMA],
    )(src)


def _row_tile(rows):
    return _tile(rows, (256, 128, 64, 32, 16, 8))


def _layer_parts_specs(n_layers, n_parts, tr, cols):
    return [pl.BlockSpec((n_parts, tr, cols), lambda l, i, j=j: (0, jnp.where(l == j, i, 0), 0)) for j in range(n_layers)]


def _select_layer_sum(p_refs):
    l = pl.program_id(0)
    g = None
    for j, p_ref in enumerate(p_refs):
        gj = p_ref[0].astype(F32)
        for k in range(1, p_ref.shape[0]):
            gj = gj + p_ref[k].astype(F32)
        g = gj if g is None else jnp.where(l == j, gj, g)
    return g


def _sum_parts(parts, *, name):
    n_layers = len(parts)
    n_parts, rows, cols = parts[0].shape
    tr = _row_tile(rows)

    def body(*refs):
        refs[-1][...] = _select_layer_sum(refs[:n_layers])

    return pl.pallas_call(
        body,
        name=name,
        grid=(n_layers, rows // tr),
        in_specs=_layer_parts_specs(n_layers, n_parts, tr, cols),
        out_specs=pl.BlockSpec((None, tr, cols), lambda l, i: (l, i, 0)),
        out_shape=jax.ShapeDtypeStruct((n_layers, rows, cols), F32),
        compiler_params=_cparams(("arbitrary", "arbitrary")),
    )(*parts)


def _adamw(parts, w, m, v, *, name):
    n_layers, rows, cols = w.shape
    summed = not isinstance(parts, (list, tuple))
    tr = _row_tile(rows)
    n_in = 1 if summed else n_layers

    def body(*refs):
        w_ref, m_ref, v_ref, g_ref, d_ref, m2_ref, v2_ref = refs[n_in:]
        g = refs[0][...] if summed else _select_layer_sum(refs[:n_in])
        m2 = ADAM_B1 * m_ref[...] + (1.0 - ADAM_B1) * g
        v2 = ADAM_B2 * v_ref[...] + (1.0 - ADAM_B2) * (g * g)
        m_hat = m2 / (1.0 - ADAM_B1 ** ADAM_STEP)
        v_hat = v2 / (1.0 - ADAM_B2 ** ADAM_STEP)
        g_ref[...] = g
        d_ref[...] = -ADAM_LR * (m_hat / (jnp.sqrt(v_hat) + ADAM_EPS) + ADAM_WD * w_ref[...])
        m2_ref[...] = m2
        v2_ref[...] = v2

    slab = pl.BlockSpec((None, tr, cols), lambda l, i: (l, i, 0))
    out = jax.ShapeDtypeStruct((n_layers, rows, cols), F32)
    p_specs = [slab] if summed else _layer_parts_specs(n_layers, parts[0].shape[0], tr, cols)
    return pl.pallas_call(
        body,
        name=name,
        grid=(n_layers, rows // tr),
        in_specs=p_specs + [slab, slab, slab],
        out_specs=[slab, slab, slab, slab],
        out_shape=[out, out, out, out],
        compiler_params=_cparams(("arbitrary", "arbitrary")),
    )(*((parts,) if summed else tuple(parts)), w, m, v)


SLAB_ROWS = 256
_SMALL_SHARDED = (("conv_w", (2, 31, 32)), ("ffn_conv_w", (2, 3, 704)))
_REPLICATED = (("g_mix", (2, 1024)), ("conv_b", (2, 256)), ("conv_ln_g", (2, 256)), ("conv_ln_b", (2, 256)),
               ("sgu_ln_g", (2, 256)), ("sgu_ln_b", (2, 256)), ("sgu_w", (2, 4, 128, 128)), ("sgu_b", (2, 4, 128)),
               ("g_out", (2, 1024)), ("g_ffn", (2, 1024)), ("ffn_conv_b", (2, 5632)), ("g_final", (1024,)))


def _seg_rows(n_elems):
    return -(-n_elems // LANES)


def _pack(arrays, lead=()):
    segs = []
    for a in arrays:
        flat = a.reshape(lead + (-1,)).astype(F32)
        pad = _seg_rows(flat.shape[-1]) * LANES - flat.shape[-1]
        if pad:
            flat = jnp.pad(flat, [(0, 0)] * len(lead) + [(0, pad)])
        segs.append(flat)
    flat = jnp.concatenate(segs, axis=-1)
    rows = flat.shape[-1] // LANES
    pad_rows = -rows % SLAB_ROWS
    if pad_rows:
        flat = jnp.pad(flat, [(0, 0)] * len(lead) + [(0, pad_rows * LANES)])
    return flat.reshape(lead + (rows + pad_rows, LANES))


def _unpack(slab, shapes, lead=()):
    flat = slab.reshape(lead + (-1,))
    out, off = [], 0
    for shape in shapes:
        n = math.prod(shape)
        out.append(flat[..., off:off + n].reshape(lead + tuple(shape)))
        off += _seg_rows(n) * LANES
    return out


def _split_last(full):
    split = full.shape[:-1] + (N_DEV, full.shape[-1] // N_DEV)
    return jnp.moveaxis(full.reshape(split), -2, 0)


def _join_last(blocks):
    moved = jnp.moveaxis(blocks, 0, -2)
    return moved.reshape(moved.shape[:-2] + (moved.shape[-2] * moved.shape[-1],))


def _layer_fwd(l, x, wt, small):
    tag = f"l{l}"
    h = _rmsnorm_fwd(x, small["g_mix"][l][None], name=f"{tag}_norm_mix")
    p_ab = _matmul(h, wt["w_in_t"][l], "nt", name=f"{tag}_proj_ab", n=D_AB)
    qkv = _matmul(h, wt["w_in_t"][l], "nt", name=f"{tag}_proj_qkv", n=D_QKV, b_n0=D_AB, out_dtype=BF16)
    ya = _mixer_a_fwd(p_ab, wt["conv_w"][l], small["conv_b"][l][None], small["conv_ln_g"][l][None],
                      small["conv_ln_b"][l][None], name=f"{tag}_mixer_a")
    bias = jnp.repeat(small["sgu_b"][l].T, HEAD_DIM, axis=1)
    yb = _sgu_fwd(p_ab, small["sgu_ln_g"][l][None], small["sgu_ln_b"][l][None], small["sgu_w"][l], bias,
                  name=f"{tag}_sgu")
    yc, t_tot = _attn_fwd(qkv, name=f"{tag}_attn")
    y = _combine_fwd(ya, yb, yc, small["g_out"][l][None], name=f"{tag}_combine")
    x1 = _matmul(y, wt["w_out"][l], "nn", name=f"{tag}_out_proj", residual=x)
    h2 = _rmsnorm_fwd(x1, small["g_ffn"][l][None], name=f"{tag}_norm_ffn")
    up0 = _matmul(h2, wt["w_up_t"][l], "nt", name=f"{tag}_up")
    act = _ffn_act_fwd(up0, wt["ffn_conv_w"][l], small["ffn_conv_b"][l][None], name=f"{tag}_ffn_act")
    x2 = _matmul(act, wt["w_down"][l], "nn", name=f"{tag}_down", residual=x1)
    saved = dict(x=x, h=h, p_ab=p_ab, qkv=qkv, ya=ya, yb=yb, yc=yc, t_tot=t_tot, y=y, x1=x1, h2=h2, up0=up0,
                 act=act, bias=bias)
    return x2, saved


def _layer_bwd(l, dx2, sv, wt, small):
    tag = f"l{l}b"
    big, g = {}, {}
    dact = _matmul(dx2, wt["w_down"][l], "nt", name=f"{tag}_dact")
    big["w_down"] = _matmul(sv["act"], dx2, "tn", name=f"{tag}_dw_down", out_dtype=BF16)
    dup0, g["ffn_conv_w"], dfb = _ffn_act_bwd(sv["up0"], dact, wt["ffn_conv_w"][l], small["ffn_conv_b"][l][None],
                                              name=f"{tag}_ffn_act")
    g["ffn_conv_b"] = dfb[0]
    dh2 = _matmul(dup0, wt["w_up_t"][l], "nn", name=f"{tag}_dh2")
    big["w_up_t"] = _matmul(dup0, sv["h2"], "tn", name=f"{tag}_dw_up", out_dtype=BF16)
    dx1, dg = _rmsnorm_bwd(sv["x1"], small["g_ffn"][l][None], dh2, dx2, name=f"{tag}_norm_ffn")
    g["g_ffn"] = dg[0]
    dy = _matmul(dx1, wt["w_out"][l], "nt", name=f"{tag}_dy")
    big["w_out"] = _matmul(sv["y"], dx1, "tn", name=f"{tag}_dw_out", out_dtype=BF16)
    dya, dyb, dyc, dg = _combine_bwd(dy, sv["ya"], sv["yb"], sv["yc"], small["g_out"][l][None], name=f"{tag}_combine")
    g["g_out"] = dg[0]
    dq, dk, dv = _attn_bwd(sv["qkv"], sv["t_tot"], dyc, name=f"{tag}_attn")
    dp_b, g["sgu_w"], db, dg, dbeta = _sgu_bwd(sv["p_ab"], dyb, small["sgu_ln_g"][l][None], small["sgu_ln_b"][l][None],
                                               small["sgu_w"][l], sv["bias"], name=f"{tag}_sgu")
    g["sgu_b"] = db[:, :N_SGU_HEADS].T
    g["sgu_ln_g"], g["sgu_ln_b"] = dg[0], dbeta[0]
    dp_a, g["conv_w"], dcb, dg, dbeta = _mixer_a_bwd(sv["p_ab"], dya, wt["conv_w"][l], small["conv_b"][l][None],
                                                     small["conv_ln_g"][l][None], small["conv_ln_b"][l][None],
                                                     name=f"{tag}_mixer_a")
    g["conv_b"], g["conv_ln_g"], g["conv_ln_b"] = dcb[0], dg[0], dbeta[0]
    dp = jnp.concatenate([dp_a.astype(BF16), dp_b.astype(BF16), dq.astype(BF16), dk.astype(BF16), dv.astype(BF16)],
                         axis=1)
    dh = _matmul(dp, wt["w_in_t"][l], "nn", name=f"{tag}_dh")
    big["w_in_t"] = _matmul(dp, sv["h"], "tn", name=f"{tag}_dw_in", out_dtype=BF16)
    dx, dg = _rmsnorm_bwd(sv["x"], small["g_mix"][l][None], dh, dx1, name=f"{tag}_norm_mix")
    g["g_mix"] = dg[0]
    return dx, big, g


_BIG = ("w_in_t", "w_out", "w_up_t", "w_down")


def kernel(x, g_mix, w_in, conv_w, conv_b, conv_ln_g, conv_ln_b, sgu_ln_g, sgu_ln_b, sgu_w, sgu_b, g_out, w_out, g_ffn, w_up, ffn_conv_w, ffn_conv_b, w_down, g_final, loss_target, m_g_mix, m_w_in, m_conv_w, m_conv_b, m_conv_ln_g, m_conv_ln_b, m_sgu_ln_g, m_sgu_ln_b, m_sgu_w, m_sgu_b, m_g_out, m_w_out, m_g_ffn, m_w_up, m_ffn_conv_w, m_ffn_conv_b, m_w_down, m_g_final, v_g_mix, v_w_in, v_conv_w, v_conv_b, v_conv_ln_g, v_conv_ln_b, v_sgu_ln_g, v_sgu_ln_b, v_sgu_w, v_sgu_b, v_g_out, v_w_out, v_g_ffn, v_w_up, v_ffn_conv_w, v_ffn_conv_b, v_w_down, v_g_final):
    given = dict(locals())
    n_layers = g_mix.shape[0]
    layers = range(n_layers)
    small_sharded = [n for n, _ in _SMALL_SHARDED]
    replicated = [n for n, _ in _REPLICATED]
    small = {n: given[n] for n in replicated}

    shard = {"w_in_t": [w_in[l].T.astype(BF16) for l in layers], "w_out": [w_out[l].astype(BF16) for l in layers],
             "w_up_t": [w_up[l].T.astype(BF16) for l in layers], "w_down": [w_down[l].astype(BF16) for l in layers]}
    wt = {n: [_exchange(shard[n][l], kind="gather_rows", name=f"gather_{n}_l{l}") for l in layers] for n in _BIG}
    filters = _exchange(_pack([given[n] for n in small_sharded]), kind="gather_blocks", name="gather_filters")
    for n, blocks in zip(small_sharded, _unpack(filters, [s for _, s in _SMALL_SHARDED], lead=(N_DEV,))):
        wt[n] = _join_last(blocks)

    xs = x[0]
    saved = []
    for l in layers:
        xs, sv = _layer_fwd(l, xs, wt, small)
        saved.append(sv)
    loss_tile, dx, dgf = _loss_head(xs, g_final[None], loss_target[0], name="loss_head")
    received = {n: [None] * n_layers for n in _BIG}
    grads = [None] * n_layers
    for l in reversed(layers):
        dx, big, grads[l] = _layer_bwd(l, dx, saved[l], wt, small)
        for n in _BIG:
            received[n][l] = _exchange(big[n], kind="scatter_rows", name=f"scatter_{n}_l{l}")
    partial = {n: jnp.stack([g[n] for g in grads]) for n in grads[0]}
    partial["g_final"] = dgf[0]

    out = {}

    def update(n, parts):
        results = _adamw(parts, given[n], given["m_" + n], given["v_" + n], name=f"adamw_{n}")
        for pre, res in zip(("grad_", "delta_", "new_m_", "new_v_"), results):
            out[pre + n] = res

    update("w_out", received["w_out"])
    update("w_down", received["w_down"])
    for n in ("w_in", "w_up"):
        update(n, jnp.swapaxes(_sum_parts(received[n + "_t"], name=f"sum_{n}"), 1, 2))

    own = _pack([_split_last(partial[n]) for n in small_sharded], lead=(N_DEV,))
    shared = _pack([partial[n] for n in replicated])
    slab = jnp.concatenate([own, jnp.broadcast_to(shared[None], (N_DEV,) + shared.shape)], axis=1)
    slab = _exchange(slab, kind="scatter_blocks", name="scatter_small_grads")
    stacks = [jnp.concatenate([_pack([given[pre + n] for n in small_sharded]),
                               _pack([given[pre + n] for n in replicated])])[None] for pre in ("", "m_", "v_")]
    results = _adamw([slab], *stacks, name="adamw_small")
    n_own = own.shape[1]
    for pre, res in zip(("grad_", "delta_", "new_m_", "new_v_"), results):
        unpacked = (_unpack(res[0, :n_own], [s for _, s in _SMALL_SHARDED])
                    + _unpack(res[0, n_own:], [s for _, s in _REPLICATED]))
        for n, a in zip(small_sharded + replicated, unpacked):
            out[pre + n] = a

    loss = lax.psum(loss_tile[0, 0], ("x", "y", "c"))
    order = list(_WEIGHT_ORDER)
    return (loss, dx[None], *[out["grad_" + n] for n in order], *[out["delta_" + n] for n in order],
            *[out["new_m_" + n] for n in order], *[out["new_v_" + n] for n in order])


_WEIGHT_ORDER = ("g_mix", "w_in", "conv_w", "conv_b", "conv_ln_g", "conv_ln_b", "sgu_ln_g", "sgu_ln_b", "sgu_w", "sgu_b",
                 "g_out", "w_out", "g_ffn", "w_up", "ffn_conv_w", "ffn_conv_b", "w_down", "g_final")
```

```python
import functools
import math

import jax
import jax.numpy as jnp
from jax import lax
from jax.experimental import pallas as pl
from jax.experimental.pallas import tpu as pltpu

F32 = jnp.float32
BF16 = jnp.bfloat16

N_DEV = 8
D_MODEL = 1024
HEAD_DIM = 64
D_CONV = 256
D_SGU = 256
D_SB = 512
D_AB = 2 * D_CONV + 2 * D_SGU
D_QKV = 3 * D_SB
D_IN = D_AB + D_QKV
CONV_K = 31
CONV_HALO = 32
FFN_K = 3
FFN_HALO = 8
D_FF = 2816
CHUNK = 128
EPS = 1e-6
LANES = 128

ADAM_LR = 0.001
ADAM_B1 = 0.9
ADAM_B2 = 0.999
ADAM_EPS = 1e-08
ADAM_WD = 0.01
ADAM_STEP = 10

VMEM_LIMIT = 56 * 1024 * 1024


def _cparams(sem=None):
    return pltpu.CompilerParams(dimension_semantics=sem, vmem_limit_bytes=VMEM_LIMIT)


def _tile(n, prefs=(512, 256, 128)):
    for t in prefs:
        if n % t == 0:
            return t
    return n


def _sigmoid(x):
    return 1.0 / (1.0 + jnp.exp(-x))


def _softplus(x):
    return jnp.maximum(x, 0.0) + jnp.log1p(jnp.exp(-jnp.abs(x)))


_INV_SQRT2 = 1.0 / math.sqrt(2.0)
_INV_SQRT2PI = 1.0 / math.sqrt(2.0 * math.pi)


def _gelu(x):
    return 0.5 * x * (1.0 + lax.erf(x * _INV_SQRT2))


def _gelu_grad(x):
    return 0.5 * (1.0 + lax.erf(x * _INV_SQRT2)) + x * jnp.exp(-0.5 * x * x) * _INV_SQRT2PI


def _dot(a, b, dims):
    return lax.dot_general(a, b, (dims, ((), ())), preferred_element_type=F32)


_NN = ((1,), (0,))
_NT = ((1,), (1,))
_TN = ((0,), (0,))


def _split_bf16(x):
    hi = x.astype(BF16)
    lo = (x - hi.astype(F32)).astype(BF16)
    return jnp.concatenate([hi, lo], axis=1)


def _matmul(a, b, mode, *, name, out_dtype=F32, residual=None, n=None, b_n0=0):
    if mode == "nn":
        (m, k), n = a.shape, (n or b.shape[1])
    elif mode == "nt":
        (m, k), n = a.shape, (n or b.shape[0])
    else:
        (k, m), n = a.shape, b.shape[1]
    tm, tn = _tile(m), _tile(n)
    tk = k if k <= 1024 else _tile(k, (512, 256))
    nk = k // tk
    assert b_n0 % tn == 0
    j0 = b_n0 // tn

    if mode == "nn":
        a_spec = pl.BlockSpec((tm, tk), lambda i, j, kk: (i, kk))
        b_spec = pl.BlockSpec((tk, tn), lambda i, j, kk: (kk, j + j0))
        dims = _NN
    elif mode == "nt":
        a_spec = pl.BlockSpec((tm, tk), lambda i, j, kk: (i, kk))
        b_spec = pl.BlockSpec((tn, tk), lambda i, j, kk: (j + j0, kk))
        dims = _NT
    else:
        a_spec = pl.BlockSpec((tk, tm), lambda i, j, kk: (kk, i))
        b_spec = pl.BlockSpec((tk, tn), lambda i, j, kk: (kk, j))
        dims = _TN
    o_spec = pl.BlockSpec((tm, tn), lambda i, j, kk: (i, j))
    has_res = residual is not None

    def body(*refs):
        if has_res:
            a_ref, b_ref, r_ref, o_ref, acc_ref = refs
        else:
            a_ref, b_ref, o_ref, acc_ref = refs
        kk = pl.program_id(2)
        part = _dot(a_ref[...].astype(BF16), b_ref[...].astype(BF16), dims)

        @pl.when(kk == 0)
        def _():
            acc_ref[...] = part

        @pl.when(kk > 0)
        def _():
            acc_ref[...] += part

        @pl.when(kk == nk - 1)
        def _():
            acc = acc_ref[...]
            if has_res:
                acc = acc + r_ref[...]
            o_ref[...] = acc.astype(out_dtype)

    in_specs = [a_spec, b_spec] + ([o_spec] if has_res else [])
    args = (a, b) + ((residual,) if has_res else ())
    return pl.pallas_call(
        body,
        name=name,
        grid=(m // tm, n // tn, nk),
        in_specs=in_specs,
        out_specs=o_spec,
        out_shape=jax.ShapeDtypeStruct((m, n), out_dtype),
        scratch_shapes=[pltpu.VMEM((tm, tn), F32)],
        compiler_params=_cparams(("parallel", "parallel", "arbitrary")),
    )(*args)


ROW_TILE = 256


def _rmsnorm_fwd(x, g, *, name):
    s, d = x.shape

    def body(x_ref, g_ref, h_ref):
        xv = x_ref[...]
        r = lax.rsqrt(jnp.mean(xv * xv, axis=-1, keepdims=True) + EPS)
        h_ref[...] = (xv * r * g_ref[...]).astype(BF16)

    return pl.pallas_call(
        body,
        name=name,
        grid=(s // ROW_TILE,),
        in_specs=[pl.BlockSpec((ROW_TILE, d), lambda i: (i, 0)), pl.BlockSpec((1, d), lambda i: (0, 0))],
        out_specs=pl.BlockSpec((ROW_TILE, d), lambda i: (i, 0)),
        out_shape=jax.ShapeDtypeStruct((s, d), BF16),
        compiler_params=_cparams(("parallel",)),
    )(x, g)


def _rmsnorm_bwd(x, g, dh, dres, *, name):
    s, d = x.shape

    def body(x_ref, g_ref, dh_ref, dres_ref, dx_ref, dg_ref):
        xv = x_ref[...]
        r = lax.rsqrt(jnp.mean(xv * xv, axis=-1, keepdims=True) + EPS)
        xhat = xv * r
        dhv = dh_ref[...]
        dxhat = dhv * g_ref[...]
        dx_ref[...] = dres_ref[...] + r * (dxhat - xhat * jnp.mean(dxhat * xhat, axis=-1, keepdims=True))
        part = jnp.sum(dhv * xhat, axis=0, keepdims=True)

        @pl.when(pl.program_id(0) == 0)
        def _():
            dg_ref[...] = part

        @pl.when(pl.program_id(0) > 0)
        def _():
            dg_ref[...] += part

    row = pl.BlockSpec((ROW_TILE, d), lambda i: (i, 0))
    vec = pl.BlockSpec((1, d), lambda i: (0, 0))
    return pl.pallas_call(
        body,
        name=name,
        grid=(s // ROW_TILE,),
        in_specs=[row, vec, row, row],
        out_specs=[row, vec],
        out_shape=[jax.ShapeDtypeStruct((s, d), F32), jax.ShapeDtypeStruct((1, d), F32)],
        compiler_params=_cparams(("arbitrary",)),
    )(x, g, dh, dres)


def _loss_head(x, g, target, *, name):
    s, d = x.shape

    def body(x_ref, g_ref, t_ref, loss_ref, dx_ref, dg_ref):
        xv = x_ref[...]
        gv = g_ref[...]
        r = lax.rsqrt(jnp.mean(xv * xv, axis=-1, keepdims=True) + EPS)
        xhat = xv * r
        diff = xhat * gv - t_ref[...]
        dy = diff * (1.0 / d)
        dxhat = dy * gv
        dx_ref[...] = r * (dxhat - xhat * jnp.mean(dxhat * xhat, axis=-1, keepdims=True))
        dg_part = jnp.sum(dy * xhat, axis=0, keepdims=True)
        row_loss = jnp.sum(diff * diff, axis=-1, keepdims=True)
        loss_part = jnp.sum(row_loss, axis=0, keepdims=True) * (0.5 / d)

        @pl.when(pl.program_id(0) == 0)
        def _():
            dg_ref[...] = dg_part
            loss_ref[...] = jnp.broadcast_to(loss_part, loss_ref.shape)

        @pl.when(pl.program_id(0) > 0)
        def _():
            dg_ref[...] += dg_part
            loss_ref[...] += jnp.broadcast_to(loss_part, loss_ref.shape)

    row = pl.BlockSpec((ROW_TILE, d), lambda i: (i, 0))
    vec = pl.BlockSpec((1, d), lambda i: (0, 0))
    tile = pl.BlockSpec((8, LANES), lambda i: (0, 0))
    return pl.pallas_call(
        body,
        name=name,
        grid=(s // ROW_TILE,),
        in_specs=[row, vec, row],
        out_specs=[tile, row, vec],
        out_shape=[jax.ShapeDtypeStruct((8, LANES), F32), jax.ShapeDtypeStruct((s, d), F32),
                   jax.ShapeDtypeStruct((1, d), F32)],
        compiler_params=_cparams(("arbitrary",)),
    )(x, g, target)


_BRANCHES = ((0, D_CONV), (D_CONV, D_SGU), (D_CONV + D_SGU, D_SB))


def _combine_fwd(ya, yb, yc, g, *, name):
    s = ya.shape[0]

    def body(ya_ref, yb_ref, yc_ref, g_ref, y_ref):
        for ref, (off, w) in zip((ya_ref, yb_ref, yc_ref), _BRANCHES):
            v = ref[...]
            r = lax.rsqrt(jnp.mean(v * v, axis=-1, keepdims=True) + EPS)
            y_ref[:, off:off + w] = (v * r * g_ref[:, off:off + w]).astype(BF16)

    def row(w):
        return pl.BlockSpec((ROW_TILE, w), lambda i: (i, 0))

    return pl.pallas_call(
        body,
        name=name,
        grid=(s // ROW_TILE,),
        in_specs=[row(D_CONV), row(D_SGU), row(D_SB), pl.BlockSpec((1, D_MODEL), lambda i: (0, 0))],
        out_specs=row(D_MODEL),
        out_shape=jax.ShapeDtypeStruct((s, D_MODEL), BF16),
        compiler_params=_cparams(("parallel",)),
    )(ya, yb, yc, g)


def _combine_bwd(dy, ya, yb, yc, g, *, name):
    s = ya.shape[0]

    def body(dy_ref, ya_ref, yb_ref, yc_ref, g_ref, dya_ref, dyb_ref, dyc_ref, dg_ref):
        first = pl.program_id(0) == 0
        for ref, dref, (off, w) in zip((ya_ref, yb_ref, yc_ref), (dya_ref, dyb_ref, dyc_ref), _BRANCHES):
            v = ref[...]
            r = lax.rsqrt(jnp.mean(v * v, axis=-1, keepdims=True) + EPS)
            n = v * r
            dout = dy_ref[:, off:off + w]
            dn = dout * g_ref[:, off:off + w]
            dref[...] = r * (dn - n * jnp.mean(dn * n, axis=-1, keepdims=True))
            part = jnp.sum(dout * n, axis=0, keepdims=True)

            @pl.when(first)
            def _():
                dg_ref[:, off:off + w] = part

            @pl.when(jnp.logical_not(first))
            def _():
                dg_ref[:, off:off + w] += part

    def row(w):
        return pl.BlockSpec((ROW_TILE, w), lambda i: (i, 0))

    vec = pl.BlockSpec((1, D_MODEL), lambda i: (0, 0))
    return pl.pallas_call(
        body,
        name=name,
        grid=(s // ROW_TILE,),
        in_specs=[row(D_MODEL), row(D_CONV), row(D_SGU), row(D_SB), vec],
        out_specs=[row(D_CONV), row(D_SGU), row(D_SB), vec],
        out_shape=[jax.ShapeDtypeStruct((s, D_CONV), F32), jax.ShapeDtypeStruct((s, D_SGU), F32),
                   jax.ShapeDtypeStruct((s, D_SB), F32), jax.ShapeDtypeStruct((1, D_MODEL), F32)],
        compiler_params=_cparams(("arbitrary",)),
    )(dy, ya, yb, yc, g)


CONV_TILE = 128


def _shift_down(window, j, halo):
    return pltpu.roll(window, j, 0)[halo:, :] if j else window[halo:, :]


def _shift_up(window, j, n_out):
    n = window.shape[0]
    return pltpu.roll(window, n - j, 0)[:n_out, :] if j else window[:n_out, :]


def _mixer_a_fwd(p_ab, conv_w, conv_b, ln_g, ln_b, *, name):
    s = p_ab.shape[0]
    nt = s // CONV_TILE

    def body(p_ref, w_ref, b_ref, g_ref, beta_ref, y_ref, h_ref):
        h_ref[0:CONV_HALO, :] = jnp.zeros((CONV_HALO, D_CONV), F32)

        def glu(i, c):
            t0 = pl.multiple_of(i * CONV_TILE, CONV_TILE)
            a = p_ref[pl.ds(t0, CONV_TILE), 0:D_CONV]
            gate = p_ref[pl.ds(t0, CONV_TILE), D_CONV:2 * D_CONV]
            h_ref[pl.ds(t0 + CONV_HALO, CONV_TILE), :] = a * _sigmoid(gate)
            return c

        lax.fori_loop(0, nt, glu, 0)

        def conv(i, c):
            t0 = pl.multiple_of(i * CONV_TILE, CONV_TILE)
            window = h_ref[pl.ds(t0, CONV_TILE + CONV_HALO), :]
            acc = jnp.zeros((CONV_TILE, D_CONV), F32) + b_ref[...]
            for k in range(CONV_K):
                acc = acc + w_ref[k:k + 1, :] * _shift_down(window, CONV_K - 1 - k, CONV_HALO)
            mu = jnp.mean(acc, axis=-1, keepdims=True)
            xc = acc - mu
            rstd = lax.rsqrt(jnp.mean(xc * xc, axis=-1, keepdims=True) + EPS)
            z = xc * rstd * g_ref[...] + beta_ref[...]
            y_ref[pl.ds(t0, CONV_TILE), :] = z * _sigmoid(z)
            return c

        lax.fori_loop(0, nt, conv, 0)

    full = lambda shape: pl.BlockSpec(shape, lambda i: (0, 0))
    return pl.pallas_call(
        body,
        name=name,
        grid=(1,),
        in_specs=[full((s, 2 * D_CONV)), full((CONV_K, D_CONV)), full((1, D_CONV)), full((1, D_CONV)),
                  full((1, D_CONV))],
        out_specs=full((s, D_CONV)),
        out_shape=jax.ShapeDtypeStruct((s, D_CONV), F32),
        scratch_shapes=[pltpu.VMEM((s + CONV_HALO, D_CONV), F32)],
        compiler_params=_cparams(("arbitrary",)),
    )(p_ab, conv_w, conv_b, ln_g, ln_b)


def _mixer_a_bwd(p_ab, dya, conv_w, conv_b, ln_g, ln_b, *, name):
    s = p_ab.shape[0]
    nt = s // CONV_TILE

    def body(p_ref, dy_ref, w_ref, b_ref, g_ref, beta_ref, dp_ref, dw_ref, db_ref, dg_ref, dbeta_ref, h_ref, dc_ref):
        h_ref[0:CONV_HALO, :] = jnp.zeros((CONV_HALO, D_CONV), F32)
        dc_ref[s:s + CONV_HALO, :] = jnp.zeros((CONV_HALO, D_CONV), F32)
        dw_ref[...] = jnp.zeros_like(dw_ref)
        db_ref[...] = jnp.zeros_like(db_ref)
        dg_ref[...] = jnp.zeros_like(dg_ref)
        dbeta_ref[...] = jnp.zeros_like(dbeta_ref)

        def glu(i, c):
            t0 = pl.multiple_of(i * CONV_TILE, CONV_TILE)
            a = p_ref[pl.ds(t0, CONV_TILE), 0:D_CONV]
            gate = p_ref[pl.ds(t0, CONV_TILE), D_CONV:2 * D_CONV]
            h_ref[pl.ds(t0 + CONV_HALO, CONV_TILE), :] = a * _sigmoid(gate)
            return c

        lax.fori_loop(0, nt, glu, 0)

        def conv_bwd(i, c):
            t0 = pl.multiple_of(i * CONV_TILE, CONV_TILE)
            window = h_ref[pl.ds(t0, CONV_TILE + CONV_HALO), :]
            taps = [_shift_down(window, CONV_K - 1 - k, CONV_HALO) for k in range(CONV_K)]
            acc = jnp.zeros((CONV_TILE, D_CONV), F32) + b_ref[...]
            for k in range(CONV_K):
                acc = acc + w_ref[k:k + 1, :] * taps[k]
            mu = jnp.mean(acc, axis=-1, keepdims=True)
            xc = acc - mu
            rstd = lax.rsqrt(jnp.mean(xc * xc, axis=-1, keepdims=True) + EPS)
            xhat = xc * rstd
            z = xhat * g_ref[...] + beta_ref[...]
            sg = _sigmoid(z)
            dz = dy_ref[pl.ds(t0, CONV_TILE), :] * (sg * (1.0 + z * (1.0 - sg)))
            dg_ref[...] += jnp.sum(dz * xhat, axis=0, keepdims=True)
            dbeta_ref[...] += jnp.sum(dz, axis=0, keepdims=True)
            dxhat = dz * g_ref[...]
            dc = rstd * (dxhat - jnp.mean(dxhat, axis=-1, keepdims=True)
                         - xhat * jnp.mean(dxhat * xhat, axis=-1, keepdims=True))
            dc_ref[pl.ds(t0, CONV_TILE), :] = dc
            db_ref[...] += jnp.sum(dc, axis=0, keepdims=True)
            for k in range(CONV_K):
                dw_ref[k:k + 1, :] += jnp.sum(dc * taps[k], axis=0, keepdims=True)
            return c

        lax.fori_loop(0, nt, conv_bwd, 0)

        def glu_bwd(i, c):
            t0 = pl.multiple_of(i * CONV_TILE, CONV_TILE)
            window = dc_ref[pl.ds(t0, CONV_TILE + CONV_HALO), :]
            dh = jnp.zeros((CONV_TILE, D_CONV), F32)
            for j in range(CONV_K):
                dh = dh + w_ref[CONV_K - 1 - j:CONV_K - j, :] * _shift_up(window, j, CONV_TILE)
            a = p_ref[pl.ds(t0, CONV_TILE), 0:D_CONV]
            sg = _sigmoid(p_ref[pl.ds(t0, CONV_TILE), D_CONV:2 * D_CONV])
            dp_ref[pl.ds(t0, CONV_TILE), 0:D_CONV] = dh * sg
            dp_ref[pl.ds(t0, CONV_TILE), D_CONV:2 * D_CONV] = dh * a * sg * (1.0 - sg)
            return c

        lax.fori_loop(0, nt, glu_bwd, 0)

    full = lambda shape: pl.BlockSpec(shape, lambda i: (0, 0))
    vec = jax.ShapeDtypeStruct((1, D_CONV), F32)
    return pl.pallas_call(
        body,
        name=name,
        grid=(1,),
        in_specs=[full((s, 2 * D_CONV)), full((s, D_CONV)), full((CONV_K, D_CONV)), full((1, D_CONV)),
                  full((1, D_CONV)), full((1, D_CONV))],
        out_specs=[full((s, 2 * D_CONV)), full((CONV_K, D_CONV)), full((1, D_CONV)), full((1, D_CONV)),
                   full((1, D_CONV))],
        out_shape=[jax.ShapeDtypeStruct((s, 2 * D_CONV), F32), jax.ShapeDtypeStruct((CONV_K, D_CONV), F32),
                   vec, vec, vec],
        scratch_shapes=[pltpu.VMEM((s + CONV_HALO, D_CONV), F32), pltpu.VMEM((s + CONV_HALO, D_CONV), F32)],
        compiler_params=_cparams(("arbitrary",)),
    )(p_ab, dya, conv_w, conv_b, ln_g, ln_b)


N_SGU_HEADS = D_SGU // HEAD_DIM


def _head_masks(width):
    lane = lax.broadcasted_iota(jnp.int32, (1, width), 1)
    return [(lane >= h * HEAD_DIM) & (lane < (h + 1) * HEAD_DIM) for h in range(width // HEAD_DIM)]


def _tril_mask():
    r = lax.broadcasted_iota(jnp.int32, (CHUNK, CHUNK), 0)
    c = lax.broadcasted_iota(jnp.int32, (CHUNK, CHUNK), 1)
    return c <= r


def _sgu_norm(bv, g, beta):
    vg = _gelu(bv)
    mu = jnp.mean(vg, axis=-1, keepdims=True)
    xc = vg - mu
    rstd = lax.rsqrt(jnp.mean(xc * xc, axis=-1, keepdims=True) + EPS)
    xhat = xc * rstd
    return xhat, rstd, xhat * g + beta


def _sgu_fwd(p_ab, ln_g, ln_b, w_s, bias, *, name):
    s = p_ab.shape[0]

    def body(p_ref, g_ref, beta_ref, w_ref, bias_ref, y_ref):
        u = _gelu(p_ref[:, 0:D_SGU])
        _, _, vn = _sgu_norm(p_ref[:, D_SGU:2 * D_SGU], g_ref[...], beta_ref[...])
        vb = vn.astype(BF16)
        tril = _tril_mask()
        mixed = bias_ref[...]
        for h, m in enumerate(_head_masks(D_SGU)):
            wh = jnp.where(tril, w_ref[h], 0.0).astype(BF16)
            mixed = mixed + _dot(wh, jnp.where(m, vb, jnp.zeros_like(vb)), _NN)
        y_ref[...] = u * mixed

    return pl.pallas_call(
        body,
        name=name,
        grid=(s // CHUNK,),
        in_specs=[pl.BlockSpec((CHUNK, 2 * D_SGU), lambda i: (i, 1)),
                  pl.BlockSpec((1, D_SGU), lambda i: (0, 0)), pl.BlockSpec((1, D_SGU), lambda i: (0, 0)),
                  pl.BlockSpec((N_SGU_HEADS, CHUNK, CHUNK), lambda i: (0, 0, 0)),
                  pl.BlockSpec((CHUNK, D_SGU), lambda i: (0, 0))],
        out_specs=pl.BlockSpec((CHUNK, D_SGU), lambda i: (i, 0)),
        out_shape=jax.ShapeDtypeStruct((s, D_SGU), F32),
        compiler_params=_cparams(("parallel",)),
    )(p_ab, ln_g, ln_b, w_s, bias)


def _sgu_bwd(p_ab, dyb, ln_g, ln_b, w_s, bias, *, name):
    s = p_ab.shape[0]
    n_chunks = s // CHUNK

    def body(p_ref, dy_ref, g_ref, beta_ref, w_ref, bias_ref, dp_ref, dw_ref, db_ref, dg_ref, dbeta_ref, dbias_ref):
        @pl.when(pl.program_id(0) == 0)
        def _():
            dw_ref[...] = jnp.zeros_like(dw_ref)
            dbias_ref[...] = jnp.zeros_like(dbias_ref)
            dg_ref[...] = jnp.zeros_like(dg_ref)
            dbeta_ref[...] = jnp.zeros_like(dbeta_ref)

        bu = p_ref[:, 0:D_SGU]
        bv = p_ref[:, D_SGU:2 * D_SGU]
        u = _gelu(bu)
        gv = g_ref[...]
        xhat, rstd, vn = _sgu_norm(bv, gv, beta_ref[...])
        vb = vn.astype(BF16)
        tril = _tril_mask()
        masks = _head_masks(D_SGU)
        whs = [jnp.where(tril, w_ref[h], 0.0).astype(BF16) for h in range(N_SGU_HEADS)]
        mixed = bias_ref[...]
        for h, m in enumerate(masks):
            mixed = mixed + _dot(whs[h], jnp.where(m, vb, jnp.zeros_like(vb)), _NN)
        dy = dy_ref[...]
        dp_ref[:, 0:D_SGU] = dy * mixed * _gelu_grad(bu)
        dmixed = dy * u
        dbias_ref[...] += dmixed
        dmb = dmixed.astype(BF16)
        dvn = jnp.zeros((CHUNK, D_SGU), F32)
        for h, m in enumerate(masks):
            dmh = jnp.where(m, dmb, jnp.zeros_like(dmb))
            dvn = dvn + _dot(whs[h], dmh, _TN)
            dw_ref[h] += jnp.where(tril, _dot(dmh, vb, _NT), 0.0)
        dg_ref[...] += jnp.sum(dvn * xhat, axis=0, keepdims=True)
        dbeta_ref[...] += jnp.sum(dvn, axis=0, keepdims=True)
        dxhat = dvn * gv
        dvg = rstd * (dxhat - jnp.mean(dxhat, axis=-1, keepdims=True)
                      - xhat * jnp.mean(dxhat * xhat, axis=-1, keepdims=True))
        dp_ref[:, D_SGU:2 * D_SGU] = dvg * _gelu_grad(bv)

        @pl.when(pl.program_id(0) == n_chunks - 1)
        def _():
            chan = lax.broadcasted_iota(jnp.int32, (D_SGU, LANES), 0)
            head = lax.broadcasted_iota(jnp.int32, (D_SGU, LANES), 1)
            to_head = jnp.where(chan // HEAD_DIM == head, 1.0, 0.0).astype(BF16)
            db_ref[...] = _dot(_split_bf16(dbias_ref[...]), jnp.concatenate([to_head, to_head], axis=0), _NN)

    vec = pl.BlockSpec((1, D_SGU), lambda i: (0, 0))
    wspec = pl.BlockSpec((N_SGU_HEADS, CHUNK, CHUNK), lambda i: (0, 0, 0))
    bspec = pl.BlockSpec((CHUNK, D_SGU), lambda i: (0, 0))
    return pl.pallas_call(
        body,
        name=name,
        grid=(n_chunks,),
        in_specs=[pl.BlockSpec((CHUNK, 2 * D_SGU), lambda i: (i, 1)), pl.BlockSpec((CHUNK, D_SGU), lambda i: (i, 0)),
                  vec, vec, wspec, bspec],
        out_specs=[pl.BlockSpec((CHUNK, 2 * D_SGU), lambda i: (i, 0)), wspec,
                   pl.BlockSpec((CHUNK, LANES), lambda i: (0, 0)), vec, vec],
        out_shape=[jax.ShapeDtypeStruct((s, 2 * D_SGU), F32),
                   jax.ShapeDtypeStruct((N_SGU_HEADS, CHUNK, CHUNK), F32),
                   jax.ShapeDtypeStruct((CHUNK, LANES), F32),
                   jax.ShapeDtypeStruct((1, D_SGU), F32), jax.ShapeDtypeStruct((1, D_SGU), F32)],
        scratch_shapes=[pltpu.VMEM((CHUNK, D_SGU), F32)],
        compiler_params=_cparams(("arbitrary",)),
    )(p_ab, dyb, ln_g, ln_b, w_s, bias)


N_PAIRS = D_SB // LANES
SB_SCALE = HEAD_DIM ** -0.5


def _sb_logits(qh, kb, valid):
    z = _dot(qh, kb, _NT) * SB_SCALE
    sp = _softplus(z)
    l1 = -sp if valid is None else jnp.where(valid, -sp, 0.0)
    return l1, z - sp


def _attn_fwd(qkv, *, name, exchanges=None):
    s = qkv.shape[0]
    nq = s // CHUNK
    ex = exchanges or _Exchanges([])
    n_ex = len(ex.arrays)

    def body(*refs):
        q_ref, k_ref, v_ref = refs[:3]
        o_ref, t_ref = refs[3 + n_ex:5 + n_ex]
        ex_refs = (refs[3:3 + n_ex], refs[5 + n_ex:5 + 2 * n_ex]) + refs[5 + 2 * n_ex:]
        qi = pl.program_id(1)
        if n_ex:
            @pl.when((pl.program_id(0) == 0) & (qi == 0))
            def _():
                ex.start(*ex_refs)

        q = q_ref[...]
        zero = jnp.zeros_like(q)
        masks = _head_masks(LANES)
        qs = [jnp.where(m, q, zero) for m in masks]
        row = lax.broadcasted_iota(jnp.int32, (CHUNK, CHUNK), 0)
        col = lax.broadcasted_iota(jnp.int32, (CHUNK, CHUNK), 1)
        upper = (row > col).astype(BF16)
        upper2 = jnp.concatenate([upper, upper], axis=0)

        def block(j, carry, valid):
            o, c0, c1 = carry
            k0 = pl.multiple_of(j * CHUNK, CHUNK)
            kb = k_ref[pl.ds(k0, CHUNK), :]
            vb = v_ref[pl.ds(k0, CHUNK), :]
            cs = []
            for h, c in enumerate((c0, c1)):
                l1, lb = _sb_logits(qs[h], kb, valid)
                after = _dot(_split_bf16(l1), upper2, _NN) + c
                a = jnp.exp(lb + after)
                if valid is not None:
                    a = jnp.where(valid, a, 0.0)
                o = o + _dot(a.astype(BF16), jnp.where(masks[h], vb, zero), _NN)
                cs.append(c + jnp.sum(l1, axis=1, keepdims=True))
            return o, cs[0], cs[1]

        zc = jnp.zeros((CHUNK, 1), F32)
        carry = block(qi, (jnp.zeros((CHUNK, LANES), F32), zc, zc), col < row)
        o, c0, c1 = lax.fori_loop(0, qi, lambda jj, c: block(qi - 1 - jj, c, None), carry)
        o_ref[...] = o
        t_ref[:, 0:LANES] = jnp.broadcast_to(c0, (CHUNK, LANES))
        t_ref[:, LANES:2 * LANES] = jnp.broadcast_to(c1, (CHUNK, LANES))
        if n_ex:
            @pl.when((pl.program_id(0) == N_PAIRS - 1) & (qi == nq - 1))
            def _():
                ex.wait(*ex_refs)

    return pl.pallas_call(
        body,
        name=name,
        grid=(N_PAIRS, nq),
        in_specs=[pl.BlockSpec((CHUNK, LANES), lambda p, i: (i, p)),
                  pl.BlockSpec((s, LANES), lambda p, i: (0, N_PAIRS + p)),
                  pl.BlockSpec((s, LANES), lambda p, i: (0, 2 * N_PAIRS + p))] + ex.in_specs,
        out_specs=[pl.BlockSpec((CHUNK, LANES), lambda p, i: (i, p)),
                   pl.BlockSpec((CHUNK, 2 * LANES), lambda p, i: (i, p))] + ex.out_specs,
        out_shape=[jax.ShapeDtypeStruct((s, D_SB), F32), jax.ShapeDtypeStruct((s, 2 * D_SB), F32)] + ex.out_shapes,
        scratch_shapes=ex.scratch_shapes if n_ex else [],
        compiler_params=_cparams(("arbitrary", "arbitrary")),
    )(qkv, qkv, qkv, *ex.arrays)


def _attn_bwd(qkv, t_tot, do, *, name, exchanges=None):
    s = qkv.shape[0]
    nq = s // CHUNK
    ex = exchanges or _Exchanges([])
    n_ex = len(ex.arrays)

    def body(*refs):
        q_ref, k_ref, v_ref, t_ref, do_ref = refs[:5]
        dq_ref, dk_ref, dv_ref = refs[5 + n_ex:8 + n_ex]
        ex_refs = (refs[5:5 + n_ex], refs[8 + n_ex:8 + 2 * n_ex]) + refs[8 + 2 * n_ex:]
        qi = pl.program_id(1)
        if n_ex:
            @pl.when((pl.program_id(0) == 0) & (qi == 0))
            def _():
                ex.start(*ex_refs)

        @pl.when(qi == 0)
        def _():
            dk_ref[...] = jnp.zeros_like(dk_ref)
            dv_ref[...] = jnp.zeros_like(dv_ref)

        q = q_ref[...]
        dob = do_ref[...].astype(BF16)
        zero = jnp.zeros_like(q)
        masks = _head_masks(LANES)
        qs = [jnp.where(m, q, zero) for m in masks]
        dos = [jnp.where(m, dob, zero) for m in masks]
        tots = [t_ref[:, 0:LANES], t_ref[:, LANES:2 * LANES]]
        row = lax.broadcasted_iota(jnp.int32, (CHUNK, CHUNK), 0)
        col = lax.broadcasted_iota(jnp.int32, (CHUNK, CHUNK), 1)
        upto = (row <= col).astype(BF16)
        before = (row < col).astype(BF16)
        upto2 = jnp.concatenate([upto, upto], axis=0)
        before2 = jnp.concatenate([before, before], axis=0)

        def block(j, carry, valid):
            dq, cl0, cl1, cp0, cp1 = carry
            k0 = pl.multiple_of(j * CHUNK, CHUNK)
            kb = k_ref[pl.ds(k0, CHUNK), :]
            vb = v_ref[pl.ds(k0, CHUNK), :]
            dk_blk = jnp.zeros((CHUNK, LANES), F32)
            dv_blk = jnp.zeros((CHUNK, LANES), F32)
            cls, cps = [], []
            for h, (cl, cp) in enumerate(((cl0, cp0), (cl1, cp1))):
                l1, lb = _sb_logits(qs[h], kb, valid)
                prefix = _dot(_split_bf16(l1), upto2, _NN) + cl
                a = jnp.exp(lb + (tots[h] - prefix))
                if valid is not None:
                    a = jnp.where(valid, a, 0.0)
                g = a * _dot(dos[h], vb, _NT)
                pre_g = _dot(_split_bf16(g), before2, _NN) + cp
                dz = (g - (g + pre_g) * jnp.exp(lb)) * SB_SCALE
                if valid is not None:
                    dz = jnp.where(valid, dz, 0.0)
                dzb = dz.astype(BF16)
                dq = dq + _dot(dzb, jnp.where(masks[h], kb, zero), _NN)
                dk_blk = dk_blk + _dot(dzb, qs[h], _TN)
                dv_blk = dv_blk + _dot(a.astype(BF16), dos[h], _TN)
                cls.append(cl + jnp.sum(l1, axis=1, keepdims=True))
                cps.append(cp + jnp.sum(g, axis=1, keepdims=True))
            dk_ref[pl.ds(k0, CHUNK), :] += dk_blk
            dv_ref[pl.ds(k0, CHUNK), :] += dv_blk
            return dq, cls[0], cls[1], cps[0], cps[1]

        zc = jnp.zeros((CHUNK, 1), F32)
        carry = (jnp.zeros((CHUNK, LANES), F32), zc, zc, zc, zc)
        carry = lax.fori_loop(0, qi, lambda j, c: block(j, c, None), carry)
        carry = block(qi, carry, col < row)
        dq_ref[...] = carry[0]
        if n_ex:
            @pl.when((pl.program_id(0) == N_PAIRS - 1) & (qi == nq - 1))
            def _():
                ex.wait(*ex_refs)

    blk = pl.BlockSpec((CHUNK, LANES), lambda p, i: (i, p))
    col_blk = pl.BlockSpec((s, LANES), lambda p, i: (0, p))
    out = jax.ShapeDtypeStruct((s, D_SB), F32)
    return pl.pallas_call(
        body,
        name=name,
        grid=(N_PAIRS, nq),
        in_specs=[blk,
                  pl.BlockSpec((s, LANES), lambda p, i: (0, N_PAIRS + p)),
                  pl.BlockSpec((s, LANES), lambda p, i: (0, 2 * N_PAIRS + p)),
                  pl.BlockSpec((CHUNK, 2 * LANES), lambda p, i: (i, p)),
                  blk] + ex.in_specs,
        out_specs=[blk, col_blk, col_blk] + ex.out_specs,
        out_shape=[out, out, out] + ex.out_shapes,
        scratch_shapes=ex.scratch_shapes if n_ex else [],
        compiler_params=_cparams(("arbitrary", "arbitrary")),
    )(qkv, qkv, qkv, t_tot, do, *ex.arrays)


FFN_TILE = 256
FFN_COLS = 256
N_FF_BLOCKS = D_FF // FFN_COLS


def _ffn_act_fwd(up0, conv_w, conv_b, *, name):
    s = up0.shape[0]
    nt = s // FFN_TILE

    def body(xg_ref, xv_ref, wg_ref, wv_ref, bg_ref, bv_ref, act_ref, pg_ref, pv_ref):
        pg_ref[0:FFN_HALO, :] = jnp.zeros((FFN_HALO, FFN_COLS), F32)
        pv_ref[0:FFN_HALO, :] = jnp.zeros((FFN_HALO, FFN_COLS), F32)
        pg_ref[FFN_HALO:, :] = xg_ref[...]
        pv_ref[FFN_HALO:, :] = xv_ref[...]

        def tile(i, c):
            t0 = pl.multiple_of(i * FFN_TILE, FFN_TILE)
            outs = []
            for p_ref, w_ref, b_ref in ((pg_ref, wg_ref, bg_ref), (pv_ref, wv_ref, bv_ref)):
                window = p_ref[pl.ds(t0, FFN_TILE + FFN_HALO), :]
                acc = b_ref[...] + w_ref[2:3, :] * window[FFN_HALO:, :]
                for j in range(1, FFN_K):
                    acc = acc + w_ref[FFN_K - 1 - j:FFN_K - j, :] * _shift_down(window, j, FFN_HALO)
                outs.append(acc)
            gate, val = outs
            act_ref[pl.ds(t0, FFN_TILE), :] = (gate * _sigmoid(gate) * val).astype(BF16)
            return c

        lax.fori_loop(0, nt, tile, 0)

    gcol = lambda rows: pl.BlockSpec((rows, FFN_COLS), lambda j: (0, j))
    vcol = lambda rows: pl.BlockSpec((rows, FFN_COLS), lambda j: (0, j + N_FF_BLOCKS))
    return pl.pallas_call(
        body,
        name=name,
        grid=(N_FF_BLOCKS,),
        in_specs=[gcol(s), vcol(s), gcol(FFN_K), vcol(FFN_K), gcol(1), vcol(1)],
        out_specs=gcol(s),
        out_shape=jax.ShapeDtypeStruct((s, D_FF), BF16),
        scratch_shapes=[pltpu.VMEM((s + FFN_HALO, FFN_COLS), F32), pltpu.VMEM((s + FFN_HALO, FFN_COLS), F32)],
        compiler_params=_cparams(("parallel",)),
    )(up0, up0, conv_w, conv_w, conv_b, conv_b)


def _ffn_act_bwd(up0, dact, conv_w, conv_b, *, name):
    s = up0.shape[0]
    nt = s // FFN_TILE

    def body(xm_ref, xp_ref, da_ref, wm_ref, wp_ref, bm_ref, bp_ref, dx_ref, dw_ref, db_ref, pm_ref, pp_ref, dm_ref):
        is_gate = pl.program_id(0) < N_FF_BLOCKS
        zeros = jnp.zeros((FFN_HALO, FFN_COLS), F32)
        pm_ref[0:FFN_HALO, :] = zeros
        pp_ref[0:FFN_HALO, :] = zeros
        pm_ref[FFN_HALO:, :] = xm_ref[...]
        pp_ref[FFN_HALO:, :] = xp_ref[...]
        dm_ref[s:s + FFN_HALO, :] = zeros
        dw_ref[...] = jnp.zeros_like(dw_ref)
        db_ref[...] = jnp.zeros_like(db_ref)

        def tile(i, c):
            t0 = pl.multiple_of(i * FFN_TILE, FFN_TILE)
            wm = pm_ref[pl.ds(t0, FFN_TILE + FFN_HALO), :]
            wp = pp_ref[pl.ds(t0, FFN_TILE + FFN_HALO), :]
            taps = [_shift_down(wm, j, FFN_HALO) for j in range(FFN_K)]
            mine = bm_ref[...]
            partner = bp_ref[...]
            for j in range(FFN_K):
                mine = mine + wm_ref[FFN_K - 1 - j:FFN_K - j, :] * taps[j]
                partner = partner + wp_ref[FFN_K - 1 - j:FFN_K - j, :] * _shift_down(wp, j, FFN_HALO)
            da = da_ref[pl.ds(t0, FFN_TILE), :]
            sg_m = _sigmoid(mine)
            sg_p = _sigmoid(partner)
            d_as_gate = da * partner * (sg_m * (1.0 + mine * (1.0 - sg_m)))
            d_as_val = da * partner * sg_p
            dm = jnp.where(is_gate, d_as_gate, d_as_val)
            dm_ref[pl.ds(t0, FFN_TILE), :] = dm
            db_ref[...] += jnp.sum(dm, axis=0, keepdims=True)
            for j in range(FFN_K):
                dw_ref[FFN_K - 1 - j:FFN_K - j, :] += jnp.sum(dm * taps[j], axis=0, keepdims=True)
            return c

        lax.fori_loop(0, nt, tile, 0)

        def tile_dx(i, c):
            t0 = pl.multiple_of(i * FFN_TILE, FFN_TILE)
            window = dm_ref[pl.ds(t0, FFN_TILE + FFN_HALO), :]
            dx = jnp.zeros((FFN_TILE, FFN_COLS), F32)
            for j in range(FFN_K):
                dx = dx + wm_ref[FFN_K - 1 - j:FFN_K - j, :] * _shift_up(window, j, FFN_TILE)
            dx_ref[pl.ds(t0, FFN_TILE), :] = dx
            return c

        lax.fori_loop(0, nt, tile_dx, 0)

    nb = 2 * N_FF_BLOCKS
    mine = lambda rows: pl.BlockSpec((rows, FFN_COLS), lambda j: (0, j))
    partner = lambda rows: pl.BlockSpec((rows, FFN_COLS), lambda j: (0, (j + N_FF_BLOCKS) % nb))
    return pl.pallas_call(
        body,
        name=name,
        grid=(nb,),
        in_specs=[mine(s), partner(s), pl.BlockSpec((s, FFN_COLS), lambda j: (0, j % N_FF_BLOCKS)),
                  mine(FFN_K), partner(FFN_K), mine(1), partner(1)],
        out_specs=[mine(s), mine(FFN_K), mine(1)],
        out_shape=[jax.ShapeDtypeStruct((s, 2 * D_FF), F32), jax.ShapeDtypeStruct((FFN_K, 2 * D_FF), F32),
                   jax.ShapeDtypeStruct((1, 2 * D_FF), F32)],
        scratch_shapes=[pltpu.VMEM((s + FFN_HALO, FFN_COLS), F32), pltpu.VMEM((s + FFN_HALO, FFN_COLS), F32),
                        pltpu.VMEM((s + FFN_HALO, FFN_COLS), F32)],
        compiler_params=_cparams(("parallel",)),
    )(up0, up0, dact, conv_w, conv_w, conv_b, conv_b)


MESH = pl.DeviceIdType.MESH


def _position():
    x, y, c = lax.axis_index("x"), lax.axis_index("y"), lax.axis_index("c")
    return x, y, c, 4 * x + 2 * y + c


def _peer(k):
    x, y, c, _ = _position()
    px = 1 - x if k & 4 else x
    py = 1 - y if k & 2 else y
    pc = 1 - c if k & 1 else c
    return (px, py, pc), 4 * px + 2 * py + pc


def _exchange(src, *, kind, name):
    ex = _Exchanges([(kind, src)])

    def body(src_ref, out_ref, send_sems, recv_sems, local_sems):
        ex.start([src_ref], [out_ref], send_sems, recv_sems, local_sems)
        ex.wait([src_ref], [out_ref], send_sems, recv_sems, local_sems)

    return pl.pallas_call(
        body,
        name=name,
        in_specs=ex.in_specs,
        out_specs=ex.out_specs[0],
        out_shape=ex.out_shapes[0],
        scratch_shapes=ex.scratch_shapes,
    )(src)


class _Exchanges:
    def __init__(self, items):
        self.kinds = [kind for kind, _ in items]
        self.arrays = [src for _, src in items]
        self.out_shapes = []
        self.block_rows = []
        for kind, src in items:
            scatter, by_rows = kind.startswith("scatter"), kind.endswith("rows")
            if by_rows:
                r = src.shape[0] // N_DEV if scatter else src.shape[0]
                shape = (N_DEV, r, src.shape[1]) if scatter else (N_DEV * r, src.shape[1])
            else:
                r = None
                shape = src.shape if scatter else (N_DEV,) + src.shape
            self.block_rows.append(r)
            self.out_shapes.append(jax.ShapeDtypeStruct(shape, src.dtype))
        n = len(items)
        self.in_specs = [pl.BlockSpec(memory_space=pl.ANY)] * n
        self.out_specs = [pl.BlockSpec(memory_space=pl.ANY)] * n
        self.scratch_shapes = [pltpu.SemaphoreType.DMA((n * (N_DEV - 1),)), pltpu.SemaphoreType.DMA((n * (N_DEV - 1),)),
                               pltpu.SemaphoreType.DMA((n,))]

    def _copies(self, i, src_ref, out_ref, send_sems, recv_sems, local_sems):
        kind, r = self.kinds[i], self.block_rows[i]
        scatter, by_rows = kind.startswith("scatter"), kind.endswith("rows")
        me = _position()[3]

        def rows(ref, idx):
            return ref.at[pl.ds(pl.multiple_of(idx * r, r), r), :]

        def outgoing(idx):
            if not scatter:
                return src_ref
            return rows(src_ref, idx) if by_rows else src_ref.at[idx]

        def slot(idx):
            return rows(out_ref, idx) if (by_rows and not scatter) else out_ref.at[idx]

        local = pltpu.make_async_copy(outgoing(me), slot(me), local_sems.at[i])
        sends, recvs = [], []
        for k in range(1, N_DEV):
            peer, pidx = _peer(k)
            sem = i * (N_DEV - 1) + k - 1
            sends.append(pltpu.make_async_remote_copy(src_ref=outgoing(pidx), dst_ref=slot(me), send_sem=send_sems.at[sem],
                                                      recv_sem=recv_sems.at[sem], device_id=peer, device_id_type=MESH))
            recvs.append(pltpu.make_async_remote_copy(src_ref=outgoing(pidx), dst_ref=slot(pidx), send_sem=send_sems.at[sem],
                                                      recv_sem=recv_sems.at[sem], device_id=peer, device_id_type=MESH))
        return local, sends, recvs

    def start(self, src_refs, out_refs, *sems):
        for i, (src_ref, out_ref) in enumerate(zip(src_refs, out_refs)):
            local, sends, _ = self._copies(i, src_ref, out_ref, *sems)
            local.start()
            for cp in sends:
                cp.start()

    def wait(self, src_refs, out_refs, *sems):
        for i, (src_ref, out_ref) in enumerate(zip(src_refs, out_refs)):
            local, sends, recvs = self._copies(i, src_ref, out_ref, *sems)
            for cp in recvs:
                cp.wait_recv()
            for cp in sends:
                cp.wait_send()
            local.wait()


def _row_tile(rows):
    return _tile(rows, (256, 128, 64, 32, 16, 8))


def _layer_parts_specs(n_layers, n_parts, tr, cols):
    return [pl.BlockSpec((n_parts, tr, cols), lambda l, i, j=j: (0, jnp.where(l == j, i, 0), 0)) for j in range(n_layers)]


def _select_layer_sum(p_refs):
    l = pl.program_id(0)
    g = None
    for j, p_ref in enumerate(p_refs):
        gj = p_ref[0].astype(F32)
        for k in range(1, p_ref.shape[0]):
            gj = gj + p_ref[k].astype(F32)
        g = gj if g is None else jnp.where(l == j, gj, g)
    return g


def _sum_parts(parts, *, name):
    n_layers = len(parts)
    n_parts, rows, cols = parts[0].shape
    tr = _row_tile(rows)

    def body(*refs):
        refs[-1][...] = _select_layer_sum(refs[:n_layers])

    return pl.pallas_call(
        body,
        name=name,
        grid=(n_layers, rows // tr),
        in_specs=_layer_parts_specs(n_layers, n_parts, tr, cols),
        out_specs=pl.BlockSpec((None, tr, cols), lambda l, i: (l, i, 0)),
        out_shape=jax.ShapeDtypeStruct((n_layers, rows, cols), F32),
        compiler_params=_cparams(("arbitrary", "arbitrary")),
    )(*parts)


def _adamw(parts, w, m, v, *, name):
    n_layers, rows, cols = w.shape
    summed = not isinstance(parts, (list, tuple))
    tr = _row_tile(rows)
    n_in = 1 if summed else n_layers

    def body(*refs):
        w_ref, m_ref, v_ref, g_ref, d_ref, m2_ref, v2_ref = refs[n_in:]
        g = refs[0][...] if summed else _select_layer_sum(refs[:n_in])
        m2 = ADAM_B1 * m_ref[...] + (1.0 - ADAM_B1) * g
        v2 = ADAM_B2 * v_ref[...] + (1.0 - ADAM_B2) * (g * g)
        m_hat = m2 / (1.0 - ADAM_B1 ** ADAM_STEP)
        v_hat = v2 / (1.0 - ADAM_B2 ** ADAM_STEP)
        g_ref[...] = g
        d_ref[...] = -ADAM_LR * (m_hat / (jnp.sqrt(v_hat) + ADAM_EPS) + ADAM_WD * w_ref[...])
        m2_ref[...] = m2
        v2_ref[...] = v2

    slab = pl.BlockSpec((None, tr, cols), lambda l, i: (l, i, 0))
    out = jax.ShapeDtypeStruct((n_layers, rows, cols), F32)
    p_specs = [slab] if summed else _layer_parts_specs(n_layers, parts[0].shape[0], tr, cols)
    return pl.pallas_call(
        body,
        name=name,
        grid=(n_layers, rows // tr),
        in_specs=p_specs + [slab, slab, slab],
        out_specs=[slab, slab, slab, slab],
        out_shape=[out, out, out, out],
        compiler_params=_cparams(("arbitrary", "arbitrary")),
    )(*((parts,) if summed else tuple(parts)), w, m, v)


SLAB_ROWS = 256
_SMALL_SHARDED = (("conv_w", (2, 31, 32)), ("ffn_conv_w", (2, 3, 704)))
_REPLICATED = (("g_mix", (2, 1024)), ("conv_b", (2, 256)), ("conv_ln_g", (2, 256)), ("conv_ln_b", (2, 256)),
               ("sgu_ln_g", (2, 256)), ("sgu_ln_b", (2, 256)), ("sgu_w", (2, 4, 128, 128)), ("sgu_b", (2, 4, 128)),
               ("g_out", (2, 1024)), ("g_ffn", (2, 1024)), ("ffn_conv_b", (2, 5632)), ("g_final", (1024,)))


def _seg_rows(n_elems):
    return -(-n_elems // LANES)


def _pack(arrays, lead=()):
    segs = []
    for a in arrays:
        flat = a.reshape(lead + (-1,)).astype(F32)
        pad = _seg_rows(flat.shape[-1]) * LANES - flat.shape[-1]
        if pad:
            flat = jnp.pad(flat, [(0, 0)] * len(lead) + [(0, pad)])
        segs.append(flat)
    flat = jnp.concatenate(segs, axis=-1)
    rows = flat.shape[-1] // LANES
    pad_rows = -rows % SLAB_ROWS
    if pad_rows:
        flat = jnp.pad(flat, [(0, 0)] * len(lead) + [(0, pad_rows * LANES)])
    return flat.reshape(lead + (rows + pad_rows, LANES))


def _unpack(slab, shapes, lead=()):
    flat = slab.reshape(lead + (-1,))
    out, off = [], 0
    for shape in shapes:
        n = math.prod(shape)
        out.append(flat[..., off:off + n].reshape(lead + tuple(shape)))
        off += _seg_rows(n) * LANES
    return out


def _split_last(full):
    split = full.shape[:-1] + (N_DEV, full.shape[-1] // N_DEV)
    return jnp.moveaxis(full.reshape(split), -2, 0)


def _join_last(blocks):
    moved = jnp.moveaxis(blocks, 0, -2)
    return moved.reshape(moved.shape[:-2] + (moved.shape[-2] * moved.shape[-1],))


def _layer_fwd(l, x, wt, small, gathers):
    tag = f"l{l}"
    h = _rmsnorm_fwd(x, small["g_mix"][l][None], name=f"{tag}_norm_mix")
    p_ab = _matmul(h, wt["w_in_t"][l], "nt", name=f"{tag}_proj_ab", n=D_AB)
    qkv = _matmul(h, wt["w_in_t"][l], "nt", name=f"{tag}_proj_qkv", n=D_QKV, b_n0=D_AB, out_dtype=BF16)
    ya = _mixer_a_fwd(p_ab, wt["conv_w"][l], small["conv_b"][l][None], small["conv_ln_g"][l][None],
                      small["conv_ln_b"][l][None], name=f"{tag}_mixer_a")
    bias = jnp.repeat(small["sgu_b"][l].T, HEAD_DIM, axis=1)
    yb = _sgu_fwd(p_ab, small["sgu_ln_g"][l][None], small["sgu_ln_b"][l][None], small["sgu_w"][l], bias,
                  name=f"{tag}_sgu")
    yc, t_tot, *gathered = _attn_fwd(qkv, name=f"{tag}_attn",
                                     exchanges=_Exchanges([("gather_rows", a) for _, _, a in gathers]))
    for (n, j, _), full in zip(gathers, gathered):
        wt[n][j] = full
    y = _combine_fwd(ya, yb, yc, small["g_out"][l][None], name=f"{tag}_combine")
    x1 = _matmul(y, wt["w_out"][l], "nn", name=f"{tag}_out_proj", residual=x)
    h2 = _rmsnorm_fwd(x1, small["g_ffn"][l][None], name=f"{tag}_norm_ffn")
    up0 = _matmul(h2, wt["w_up_t"][l], "nt", name=f"{tag}_up")
    act = _ffn_act_fwd(up0, wt["ffn_conv_w"][l], small["ffn_conv_b"][l][None], name=f"{tag}_ffn_act")
    x2 = _matmul(act, wt["w_down"][l], "nn", name=f"{tag}_down", residual=x1)
    saved = dict(x=x, h=h, p_ab=p_ab, qkv=qkv, ya=ya, yb=yb, yc=yc, t_tot=t_tot, y=y, x1=x1, h2=h2, up0=up0,
                 act=act, bias=bias)
    return x2, saved


def _layer_bwd(l, dx2, sv, wt, small, pending, received):
    tag = f"l{l}b"
    big, g = {}, {}
    dact = _matmul(dx2, wt["w_down"][l], "nt", name=f"{tag}_dact")
    big["w_down"] = _matmul(sv["act"], dx2, "tn", name=f"{tag}_dw_down", out_dtype=BF16)
    dup0, g["ffn_conv_w"], dfb = _ffn_act_bwd(sv["up0"], dact, wt["ffn_conv_w"][l], small["ffn_conv_b"][l][None],
                                              name=f"{tag}_ffn_act")
    g["ffn_conv_b"] = dfb[0]
    dh2 = _matmul(dup0, wt["w_up_t"][l], "nn", name=f"{tag}_dh2")
    big["w_up_t"] = _matmul(dup0, sv["h2"], "tn", name=f"{tag}_dw_up", out_dtype=BF16)
    dx1, dg = _rmsnorm_bwd(sv["x1"], small["g_ffn"][l][None], dh2, dx2, name=f"{tag}_norm_ffn")
    g["g_ffn"] = dg[0]
    dy = _matmul(dx1, wt["w_out"][l], "nt", name=f"{tag}_dy")
    big["w_out"] = _matmul(sv["y"], dx1, "tn", name=f"{tag}_dw_out", out_dtype=BF16)
    dya, dyb, dyc, dg = _combine_bwd(dy, sv["ya"], sv["yb"], sv["yc"], small["g_out"][l][None], name=f"{tag}_combine")
    g["g_out"] = dg[0]
    scatters = pending + [("w_down", l, big["w_down"]), ("w_up_t", l, big["w_up_t"])]
    dq, dk, dv, *landed = _attn_bwd(sv["qkv"], sv["t_tot"], dyc, name=f"{tag}_attn",
                                    exchanges=_Exchanges([("scatter_rows", a) for _, _, a in scatters]))
    for (n, j, _), got in zip(scatters, landed):
        received[n][j] = got
    dp_b, g["sgu_w"], db, dg, dbeta = _sgu_bwd(sv["p_ab"], dyb, small["sgu_ln_g"][l][None], small["sgu_ln_b"][l][None],
                                               small["sgu_w"][l], sv["bias"], name=f"{tag}_sgu")
    g["sgu_b"] = db[:, :N_SGU_HEADS].T
    g["sgu_ln_g"], g["sgu_ln_b"] = dg[0], dbeta[0]
    dp_a, g["conv_w"], dcb, dg, dbeta = _mixer_a_bwd(sv["p_ab"], dya, wt["conv_w"][l], small["conv_b"][l][None],
                                                     small["conv_ln_g"][l][None], small["conv_ln_b"][l][None],
                                                     name=f"{tag}_mixer_a")
    g["conv_b"], g["conv_ln_g"], g["conv_ln_b"] = dcb[0], dg[0], dbeta[0]
    dp = jnp.concatenate([dp_a.astype(BF16), dp_b.astype(BF16), dq.astype(BF16), dk.astype(BF16), dv.astype(BF16)],
                         axis=1)
    dh = _matmul(dp, wt["w_in_t"][l], "nn", name=f"{tag}_dh")
    big["w_in_t"] = _matmul(dp, sv["h"], "tn", name=f"{tag}_dw_in", out_dtype=BF16)
    dx, dg = _rmsnorm_bwd(sv["x"], small["g_mix"][l][None], dh, dx1, name=f"{tag}_norm_mix")
    g["g_mix"] = dg[0]
    return dx, [("w_out", l, big["w_out"]), ("w_in_t", l, big["w_in_t"])], g


_BIG = ("w_in_t", "w_out", "w_up_t", "w_down")


def kernel(x, g_mix, w_in, conv_w, conv_b, conv_ln_g, conv_ln_b, sgu_ln_g, sgu_ln_b, sgu_w, sgu_b, g_out, w_out, g_ffn, w_up, ffn_conv_w, ffn_conv_b, w_down, g_final, loss_target, m_g_mix, m_w_in, m_conv_w, m_conv_b, m_conv_ln_g, m_conv_ln_b, m_sgu_ln_g, m_sgu_ln_b, m_sgu_w, m_sgu_b, m_g_out, m_w_out, m_g_ffn, m_w_up, m_ffn_conv_w, m_ffn_conv_b, m_w_down, m_g_final, v_g_mix, v_w_in, v_conv_w, v_conv_b, v_conv_ln_g, v_conv_ln_b, v_sgu_ln_g, v_sgu_ln_b, v_sgu_w, v_sgu_b, v_g_out, v_w_out, v_g_ffn, v_w_up, v_ffn_conv_w, v_ffn_conv_b, v_w_down, v_g_final):
    given = dict(locals())
    n_layers = g_mix.shape[0]
    layers = range(n_layers)
    small_sharded = [n for n, _ in _SMALL_SHARDED]
    replicated = [n for n, _ in _REPLICATED]
    small = {n: given[n] for n in replicated}

    shard = {"w_in_t": [w_in[l].T.astype(BF16) for l in layers], "w_out": [w_out[l].astype(BF16) for l in layers],
             "w_up_t": [w_up[l].T.astype(BF16) for l in layers], "w_down": [w_down[l].astype(BF16) for l in layers]}
    wt = {n: [None] * n_layers for n in _BIG}
    wt["w_in_t"][0] = _exchange(shard["w_in_t"][0], kind="gather_rows", name="gather_w_in_t_l0")
    filters = _exchange(_pack([given[n] for n in small_sharded]), kind="gather_blocks", name="gather_filters")
    for n, blocks in zip(small_sharded, _unpack(filters, [s for _, s in _SMALL_SHARDED], lead=(N_DEV,))):
        wt[n] = _join_last(blocks)

    xs = x[0]
    saved = []
    for l in layers:
        gathers = [(n, l, shard[n][l]) for n in ("w_out", "w_up_t", "w_down")]
        if l + 1 < n_layers:
            gathers.append(("w_in_t", l + 1, shard["w_in_t"][l + 1]))
        xs, sv = _layer_fwd(l, xs, wt, small, gathers)
        saved.append(sv)
    loss_tile, dx, dgf = _loss_head(xs, g_final[None], loss_target[0], name="loss_head")
    received = {n: [None] * n_layers for n in _BIG}
    grads = [None] * n_layers
    pending = []
    for l in reversed(layers):
        dx, pending, grads[l] = _layer_bwd(l, dx, saved[l], wt, small, pending, received)
    for n, l, part in pending:
        received[n][l] = _exchange(part, kind="scatter_rows", name=f"scatter_{n}_l{l}")
    partial = {n: jnp.stack([g[n] for g in grads]) for n in grads[0]}
    partial["g_final"] = dgf[0]

    out = {}

    def update(n, parts):
        results = _adamw(parts, given[n], given["m_" + n], given["v_" + n], name=f"adamw_{n}")
        for pre, res in zip(("grad_", "delta_", "new_m_", "new_v_"), results):
            out[pre + n] = res

    update("w_out", received["w_out"])
    update("w_down", received["w_down"])
    for n in ("w_in", "w_up"):
        update(n, jnp.swapaxes(_sum_parts(received[n + "_t"], name=f"sum_{n}"), 1, 2))

    own = _pack([_split_last(partial[n]) for n in small_sharded], lead=(N_DEV,))
    shared = _pack([partial[n] for n in replicated])
    slab = jnp.concatenate([own, jnp.broadcast_to(shared[None], (N_DEV,) + shared.shape)], axis=1)
    slab = _exchange(slab, kind="scatter_blocks", name="scatter_small_grads")
    stacks = [jnp.concatenate([_pack([given[pre + n] for n in small_sharded]),
                               _pack([given[pre + n] for n in replicated])])[None] for pre in ("", "m_", "v_")]
    results = _adamw([slab], *stacks, name="adamw_small")
    n_own = own.shape[1]
    for pre, res in zip(("grad_", "delta_", "new_m_", "new_v_"), results):
        unpacked = (_unpack(res[0, :n_own], [s for _, s in _SMALL_SHARDED])
                    + _unpack(res[0, n_own:], [s for _, s in _REPLICATED]))
        for n, a in zip(small_sharded + replicated, unpacked):
            out[pre + n] = a

    loss = lax.psum(loss_tile[0, 0], ("x", "y", "c"))
    order = list(_WEIGHT_ORDER)
    return (loss, dx[None], *[out["grad_" + n] for n in order], *[out["delta_" + n] for n in order],
            *[out["new_m_" + n] for n in order], *[out["new_v_" + n] for n in order])


_WEIGHT_ORDER = ("g_mix", "w_in", "conv_w", "conv_b", "conv_ln_g", "conv_ln_b", "sgu_ln_g", "sgu_ln_b", "sgu_w", "sgu_b",
                 "g_out", "w_out", "g_ffn", "w_up", "ffn_conv_w", "ffn_conv_b", "w_down", "g_final")
```

```python
import functools
import math

import jax
import jax.numpy as jnp
from jax import lax
from jax.experimental import pallas as pl
from jax.experimental.pallas import tpu as pltpu

F32 = jnp.float32
BF16 = jnp.bfloat16

N_DEV = 8
D_MODEL = 1024
HEAD_DIM = 64
D_CONV = 256
D_SGU = 256
D_SB = 512
D_AB = 2 * D_CONV + 2 * D_SGU
D_QKV = 3 * D_SB
D_IN = D_AB + D_QKV
CONV_K = 31
CONV_HALO = 32
FFN_K = 3
FFN_HALO = 8
D_FF = 2816
CHUNK = 128
EPS = 1e-6
LANES = 128

ADAM_LR = 0.001
ADAM_B1 = 0.9
ADAM_B2 = 0.999
ADAM_EPS = 1e-08
ADAM_WD = 0.01
ADAM_STEP = 10

VMEM_LIMIT = 56 * 1024 * 1024


def _cparams(sem=None):
    return pltpu.CompilerParams(dimension_semantics=sem, vmem_limit_bytes=VMEM_LIMIT)


def _tile(n, prefs=(512, 256, 128)):
    for t in prefs:
        if n % t == 0:
            return t
    return n


def _sigmoid(x):
    return 1.0 / (1.0 + jnp.exp(-x))


def _softplus(x):
    return jnp.maximum(x, 0.0) + jnp.log1p(jnp.exp(-jnp.abs(x)))


_INV_SQRT2 = 1.0 / math.sqrt(2.0)
_INV_SQRT2PI = 1.0 / math.sqrt(2.0 * math.pi)


def _gelu(x):
    return 0.5 * x * (1.0 + lax.erf(x * _INV_SQRT2))


def _gelu_grad(x):
    return 0.5 * (1.0 + lax.erf(x * _INV_SQRT2)) + x * jnp.exp(-0.5 * x * x) * _INV_SQRT2PI


def _dot(a, b, dims):
    return lax.dot_general(a, b, (dims, ((), ())), preferred_element_type=F32)


_NN = ((1,), (0,))
_NT = ((1,), (1,))
_TN = ((0,), (0,))


def _split_bf16(x):
    hi = x.astype(BF16)
    lo = (x - hi.astype(F32)).astype(BF16)
    return jnp.concatenate([hi, lo], axis=1)


def _matmul(a, b, mode, *, name, out_dtype=F32, residual=None, n=None, b_n0=0):
    if mode == "nn":
        (m, k), n = a.shape, (n or b.shape[1])
    elif mode == "nt":
        (m, k), n = a.shape, (n or b.shape[0])
    else:
        (k, m), n = a.shape, b.shape[1]
    tm, tn = _tile(m), _tile(n)
    tk = k if k <= 1024 else _tile(k, (512, 256))
    nk = k // tk
    assert b_n0 % tn == 0
    j0 = b_n0 // tn

    if mode == "nn":
        a_spec = pl.BlockSpec((tm, tk), lambda i, j, kk: (i, kk))
        b_spec = pl.BlockSpec((tk, tn), lambda i, j, kk: (kk, j + j0))
        dims = _NN
    elif mode == "nt":
        a_spec = pl.BlockSpec((tm, tk), lambda i, j, kk: (i, kk))
        b_spec = pl.BlockSpec((tn, tk), lambda i, j, kk: (j + j0, kk))
        dims = _NT
    else:
        a_spec = pl.BlockSpec((tk, tm), lambda i, j, kk: (kk, i))
        b_spec = pl.BlockSpec((tk, tn), lambda i, j, kk: (kk, j))
        dims = _TN
    o_spec = pl.BlockSpec((tm, tn), lambda i, j, kk: (i, j))
    has_res = residual is not None

    def body(*refs):
        if has_res:
            a_ref, b_ref, r_ref, o_ref, acc_ref = refs
        else:
            a_ref, b_ref, o_ref, acc_ref = refs
        kk = pl.program_id(2)
        part = _dot(a_ref[...].astype(BF16), b_ref[...].astype(BF16), dims)

        @pl.when(kk == 0)
        def _():
            acc_ref[...] = part

        @pl.when(kk > 0)
        def _():
            acc_ref[...] += part

        @pl.when(kk == nk - 1)
        def _():
            acc = acc_ref[...]
            if has_res:
                acc = acc + r_ref[...]
            o_ref[...] = acc.astype(out_dtype)

    in_specs = [a_spec, b_spec] + ([o_spec] if has_res else [])
    args = (a, b) + ((residual,) if has_res else ())
    return pl.pallas_call(
        body,
        name=name,
        grid=(m // tm, n // tn, nk),
        in_specs=in_specs,
        out_specs=o_spec,
        out_shape=jax.ShapeDtypeStruct((m, n), out_dtype),
        scratch_shapes=[pltpu.VMEM((tm, tn), F32)],
        compiler_params=_cparams(("parallel", "parallel", "arbitrary")),
    )(*args)


ROW_TILE = 256


def _rmsnorm_fwd(x, g, *, name):
    s, d = x.shape

    def body(x_ref, g_ref, h_ref):
        xv = x_ref[...]
        r = lax.rsqrt(jnp.mean(xv * xv, axis=-1, keepdims=True) + EPS)
        h_ref[...] = (xv * r * g_ref[...]).astype(BF16)

    return pl.pallas_call(
        body,
        name=name,
        grid=(s // ROW_TILE,),
        in_specs=[pl.BlockSpec((ROW_TILE, d), lambda i: (i, 0)), pl.BlockSpec((1, d), lambda i: (0, 0))],
        out_specs=pl.BlockSpec((ROW_TILE, d), lambda i: (i, 0)),
        out_shape=jax.ShapeDtypeStruct((s, d), BF16),
        compiler_params=_cparams(("parallel",)),
    )(x, g)


def _rmsnorm_bwd(x, g, dh, dres, *, name):
    s, d = x.shape

    def body(x_ref, g_ref, dh_ref, dres_ref, dx_ref, dg_ref):
        xv = x_ref[...]
        r = lax.rsqrt(jnp.mean(xv * xv, axis=-1, keepdims=True) + EPS)
        xhat = xv * r
        dhv = dh_ref[...]
        dxhat = dhv * g_ref[...]
        dx_ref[...] = dres_ref[...] + r * (dxhat - xhat * jnp.mean(dxhat * xhat, axis=-1, keepdims=True))
        part = jnp.sum(dhv * xhat, axis=0, keepdims=True)

        @pl.when(pl.program_id(0) == 0)
        def _():
            dg_ref[...] = part

        @pl.when(pl.program_id(0) > 0)
        def _():
            dg_ref[...] += part

    row = pl.BlockSpec((ROW_TILE, d), lambda i: (i, 0))
    vec = pl.BlockSpec((1, d), lambda i: (0, 0))
    return pl.pallas_call(
        body,
        name=name,
        grid=(s // ROW_TILE,),
        in_specs=[row, vec, row, row],
        out_specs=[row, vec],
        out_shape=[jax.ShapeDtypeStruct((s, d), F32), jax.ShapeDtypeStruct((1, d), F32)],
        compiler_params=_cparams(("arbitrary",)),
    )(x, g, dh, dres)


def _loss_head(x, g, target, *, name):
    s, d = x.shape

    def body(x_ref, g_ref, t_ref, loss_ref, dx_ref, dg_ref):
        xv = x_ref[...]
        gv = g_ref[...]
        r = lax.rsqrt(jnp.mean(xv * xv, axis=-1, keepdims=True) + EPS)
        xhat = xv * r
        diff = xhat * gv - t_ref[...]
        dy = diff * (1.0 / d)
        dxhat = dy * gv
        dx_ref[...] = r * (dxhat - xhat * jnp.mean(dxhat * xhat, axis=-1, keepdims=True))
        dg_part = jnp.sum(dy * xhat, axis=0, keepdims=True)
        row_loss = jnp.sum(diff * diff, axis=-1, keepdims=True)
        loss_part = jnp.sum(row_loss, axis=0, keepdims=True) * (0.5 / d)

        @pl.when(pl.program_id(0) == 0)
        def _():
            dg_ref[...] = dg_part
            loss_ref[...] = jnp.broadcast_to(loss_part, loss_ref.shape)

        @pl.when(pl.program_id(0) > 0)
        def _():
            dg_ref[...] += dg_part
            loss_ref[...] += jnp.broadcast_to(loss_part, loss_ref.shape)

    row = pl.BlockSpec((ROW_TILE, d), lambda i: (i, 0))
    vec = pl.BlockSpec((1, d), lambda i: (0, 0))
    tile = pl.BlockSpec((8, LANES), lambda i: (0, 0))
    return pl.pallas_call(
        body,
        name=name,
        grid=(s // ROW_TILE,),
        in_specs=[row, vec, row],
        out_specs=[tile, row, vec],
        out_shape=[jax.ShapeDtypeStruct((8, LANES), F32), jax.ShapeDtypeStruct((s, d), F32),
                   jax.ShapeDtypeStruct((1, d), F32)],
        compiler_params=_cparams(("arbitrary",)),
    )(x, g, target)


_BRANCHES = ((0, D_CONV), (D_CONV, D_SGU), (D_CONV + D_SGU, D_SB))


def _combine_fwd(ya, yb, yc, g, *, name):
    s = ya.shape[0]

    def body(ya_ref, yb_ref, yc_ref, g_ref, y_ref):
        for ref, (off, w) in zip((ya_ref, yb_ref, yc_ref), _BRANCHES):
            v = ref[...]
            r = lax.rsqrt(jnp.mean(v * v, axis=-1, keepdims=True) + EPS)
            y_ref[:, off:off + w] = (v * r * g_ref[:, off:off + w]).astype(BF16)

    def row(w):
        return pl.BlockSpec((ROW_TILE, w), lambda i: (i, 0))

    return pl.pallas_call(
        body,
        name=name,
        grid=(s // ROW_TILE,),
        in_specs=[row(D_CONV), row(D_SGU), row(D_SB), pl.BlockSpec((1, D_MODEL), lambda i: (0, 0))],
        out_specs=row(D_MODEL),
        out_shape=jax.ShapeDtypeStruct((s, D_MODEL), BF16),
        compiler_params=_cparams(("parallel",)),
    )(ya, yb, yc, g)


def _combine_bwd(dy, ya, yb, yc, g, *, name):
    s = ya.shape[0]

    def body(dy_ref, ya_ref, yb_ref, yc_ref, g_ref, dya_ref, dyb_ref, dyc_ref, dg_ref):
        first = pl.program_id(0) == 0
        for ref, dref, (off, w) in zip((ya_ref, yb_ref, yc_ref), (dya_ref, dyb_ref, dyc_ref), _BRANCHES):
            v = ref[...]
            r = lax.rsqrt(jnp.mean(v * v, axis=-1, keepdims=True) + EPS)
            n = v * r
            dout = dy_ref[:, off:off + w]
            dn = dout * g_ref[:, off:off + w]
            dref[...] = r * (dn - n * jnp.mean(dn * n, axis=-1, keepdims=True))
            part = jnp.sum(dout * n, axis=0, keepdims=True)

            @pl.when(first)
            def _():
                dg_ref[:, off:off + w] = part

            @pl.when(jnp.logical_not(first))
            def _():
                dg_ref[:, off:off + w] += part

    def row(w):
        return pl.BlockSpec((ROW_TILE, w), lambda i: (i, 0))

    vec = pl.BlockSpec((1, D_MODEL), lambda i: (0, 0))
    return pl.pallas_call(
        body,
        name=name,
        grid=(s // ROW_TILE,),
        in_specs=[row(D_MODEL), row(D_CONV), row(D_SGU), row(D_SB), vec],
        out_specs=[row(D_CONV), row(D_SGU), row(D_SB), vec],
        out_shape=[jax.ShapeDtypeStruct((s, D_CONV), F32), jax.ShapeDtypeStruct((s, D_SGU), F32),
                   jax.ShapeDtypeStruct((s, D_SB), F32), jax.ShapeDtypeStruct((1, D_MODEL), F32)],
        compiler_params=_cparams(("arbitrary",)),
    )(dy, ya, yb, yc, g)


CONV_TILE = 128


def _shift_down(window, j, halo):
    return pltpu.roll(window, j, 0)[halo:, :] if j else window[halo:, :]


def _shift_up(window, j, n_out):
    n = window.shape[0]
    return pltpu.roll(window, n - j, 0)[:n_out, :] if j else window[:n_out, :]


def _mixer_a_fwd(p_ab, conv_w, conv_b, ln_g, ln_b, *, name):
    s = p_ab.shape[0]
    nt = s // CONV_TILE

    def body(p_ref, w_ref, b_ref, g_ref, beta_ref, y_ref, h_ref):
        h_ref[0:CONV_HALO, :] = jnp.zeros((CONV_HALO, D_CONV), F32)

        def glu(i, c):
            t0 = pl.multiple_of(i * CONV_TILE, CONV_TILE)
            a = p_ref[pl.ds(t0, CONV_TILE), 0:D_CONV]
            gate = p_ref[pl.ds(t0, CONV_TILE), D_CONV:2 * D_CONV]
            h_ref[pl.ds(t0 + CONV_HALO, CONV_TILE), :] = a * _sigmoid(gate)
            return c

        lax.fori_loop(0, nt, glu, 0)

        def conv(i, c):
            t0 = pl.multiple_of(i * CONV_TILE, CONV_TILE)
            window = h_ref[pl.ds(t0, CONV_TILE + CONV_HALO), :]
            acc = jnp.zeros((CONV_TILE, D_CONV), F32) + b_ref[...]
            for k in range(CONV_K):
                acc = acc + w_ref[k:k + 1, :] * _shift_down(window, CONV_K - 1 - k, CONV_HALO)
            mu = jnp.mean(acc, axis=-1, keepdims=True)
            xc = acc - mu
            rstd = lax.rsqrt(jnp.mean(xc * xc, axis=-1, keepdims=True) + EPS)
            z = xc * rstd * g_ref[...] + beta_ref[...]
            y_ref[pl.ds(t0, CONV_TILE), :] = z * _sigmoid(z)
            return c

        lax.fori_loop(0, nt, conv, 0)

    full = lambda shape: pl.BlockSpec(shape, lambda i: (0, 0))
    return pl.pallas_call(
        body,
        name=name,
        grid=(1,),
        in_specs=[full((s, 2 * D_CONV)), full((CONV_K, D_CONV)), full((1, D_CONV)), full((1, D_CONV)),
                  full((1, D_CONV))],
        out_specs=full((s, D_CONV)),
        out_shape=jax.ShapeDtypeStruct((s, D_CONV), F32),
        scratch_shapes=[pltpu.VMEM((s + CONV_HALO, D_CONV), F32)],
        compiler_params=_cparams(("arbitrary",)),
    )(p_ab, conv_w, conv_b, ln_g, ln_b)


def _mixer_a_bwd(p_ab, dya, conv_w, conv_b, ln_g, ln_b, *, name):
    s = p_ab.shape[0]
    nt = s // CONV_TILE

    def body(p_ref, dy_ref, w_ref, b_ref, g_ref, beta_ref, dp_ref, dw_ref, db_ref, dg_ref, dbeta_ref, h_ref, dc_ref):
        h_ref[0:CONV_HALO, :] = jnp.zeros((CONV_HALO, D_CONV), F32)
        dc_ref[s:s + CONV_HALO, :] = jnp.zeros((CONV_HALO, D_CONV), F32)
        dw_ref[...] = jnp.zeros_like(dw_ref)
        db_ref[...] = jnp.zeros_like(db_ref)
        dg_ref[...] = jnp.zeros_like(dg_ref)
        dbeta_ref[...] = jnp.zeros_like(dbeta_ref)

        def glu(i, c):
            t0 = pl.multiple_of(i * CONV_TILE, CONV_TILE)
            a = p_ref[pl.ds(t0, CONV_TILE), 0:D_CONV]
            gate = p_ref[pl.ds(t0, CONV_TILE), D_CONV:2 * D_CONV]
            h_ref[pl.ds(t0 + CONV_HALO, CONV_TILE), :] = a * _sigmoid(gate)
            return c

        lax.fori_loop(0, nt, glu, 0)

        def conv_bwd(i, c):
            t0 = pl.multiple_of(i * CONV_TILE, CONV_TILE)
            window = h_ref[pl.ds(t0, CONV_TILE + CONV_HALO), :]
            taps = [_shift_down(window, CONV_K - 1 - k, CONV_HALO) for k in range(CONV_K)]
            acc = jnp.zeros((CONV_TILE, D_CONV), F32) + b_ref[...]
            for k in range(CONV_K):
                acc = acc + w_ref[k:k + 1, :] * taps[k]
            mu = jnp.mean(acc, axis=-1, keepdims=True)
            xc = acc - mu
            rstd = lax.rsqrt(jnp.mean(xc * xc, axis=-1, keepdims=True) + EPS)
            xhat = xc * rstd
            z = xhat * g_ref[...] + beta_ref[...]
            sg = _sigmoid(z)
            dz = dy_ref[pl.ds(t0, CONV_TILE), :] * (sg * (1.0 + z * (1.0 - sg)))
            dg_ref[...] += jnp.sum(dz * xhat, axis=0, keepdims=True)
            dbeta_ref[...] += jnp.sum(dz, axis=0, keepdims=True)
            dxhat = dz * g_ref[...]
            dc = rstd * (dxhat - jnp.mean(dxhat, axis=-1, keepdims=True)
                         - xhat * jnp.mean(dxhat * xhat, axis=-1, keepdims=True))
            dc_ref[pl.ds(t0, CONV_TILE), :] = dc
            db_ref[...] += jnp.sum(dc, axis=0, keepdims=True)
            for k in range(CONV_K):
                dw_ref[k:k + 1, :] += jnp.sum(dc * taps[k], axis=0, keepdims=True)
            return c

        lax.fori_loop(0, nt, conv_bwd, 0)

        def glu_bwd(i, c):
            t0 = pl.multiple_of(i * CONV_TILE, CONV_TILE)
            window = dc_ref[pl.ds(t0, CONV_TILE + CONV_HALO), :]
            dh = jnp.zeros((CONV_TILE, D_CONV), F32)
            for j in range(CONV_K):
                dh = dh + w_ref[CONV_K - 1 - j:CONV_K - j, :] * _shift_up(window, j, CONV_TILE)
            a = p_ref[pl.ds(t0, CONV_TILE), 0:D_CONV]
            sg = _sigmoid(p_ref[pl.ds(t0, CONV_TILE), D_CONV:2 * D_CONV])
            dp_ref[pl.ds(t0, CONV_TILE), 0:D_CONV] = dh * sg
            dp_ref[pl.ds(t0, CONV_TILE), D_CONV:2 * D_CONV] = dh * a * sg * (1.0 - sg)
            return c

        lax.fori_loop(0, nt, glu_bwd, 0)

    full = lambda shape: pl.BlockSpec(shape, lambda i: (0, 0))
    vec = jax.ShapeDtypeStruct((1, D_CONV), F32)
    return pl.pallas_call(
        body,
        name=name,
        grid=(1,),
        in_specs=[full((s, 2 * D_CONV)), full((s, D_CONV)), full((CONV_K, D_CONV)), full((1, D_CONV)),
                  full((1, D_CONV)), full((1, D_CONV))],
        out_specs=[full((s, 2 * D_CONV)), full((CONV_K, D_CONV)), full((1, D_CONV)), full((1, D_CONV)),
                   full((1, D_CONV))],
        out_shape=[jax.ShapeDtypeStruct((s, 2 * D_CONV), F32), jax.ShapeDtypeStruct((CONV_K, D_CONV), F32),
                   vec, vec, vec],
        scratch_shapes=[pltpu.VMEM((s + CONV_HALO, D_CONV), F32), pltpu.VMEM((s + CONV_HALO, D_CONV), F32)],
        compiler_params=_cparams(("arbitrary",)),
    )(p_ab, dya, conv_w, conv_b, ln_g, ln_b)


N_SGU_HEADS = D_SGU // HEAD_DIM


def _head_masks(width):
    lane = lax.broadcasted_iota(jnp.int32, (1, width), 1)
    return [(lane >= h * HEAD_DIM) & (lane < (h + 1) * HEAD_DIM) for h in range(width // HEAD_DIM)]


def _tril_mask():
    r = lax.broadcasted_iota(jnp.int32, (CHUNK, CHUNK), 0)
    c = lax.broadcasted_iota(jnp.int32, (CHUNK, CHUNK), 1)
    return c <= r


def _sgu_norm(bv, g, beta):
    vg = _gelu(bv)
    mu = jnp.mean(vg, axis=-1, keepdims=True)
    xc = vg - mu
    rstd = lax.rsqrt(jnp.mean(xc * xc, axis=-1, keepdims=True) + EPS)
    xhat = xc * rstd
    return xhat, rstd, xhat * g + beta


def _sgu_fwd(p_ab, ln_g, ln_b, w_s, bias, *, name):
    s = p_ab.shape[0]

    def body(p_ref, g_ref, beta_ref, w_ref, bias_ref, y_ref):
        u = _gelu(p_ref[:, 0:D_SGU])
        _, _, vn = _sgu_norm(p_ref[:, D_SGU:2 * D_SGU], g_ref[...], beta_ref[...])
        vb = vn.astype(BF16)
        tril = _tril_mask()
        mixed = bias_ref[...]
        for h, m in enumerate(_head_masks(D_SGU)):
            wh = jnp.where(tril, w_ref[h], 0.0).astype(BF16)
            mixed = mixed + _dot(wh, jnp.where(m, vb, jnp.zeros_like(vb)), _NN)
        y_ref[...] = u * mixed

    return pl.pallas_call(
        body,
        name=name,
        grid=(s // CHUNK,),
        in_specs=[pl.BlockSpec((CHUNK, 2 * D_SGU), lambda i: (i, 1)),
                  pl.BlockSpec((1, D_SGU), lambda i: (0, 0)), pl.BlockSpec((1, D_SGU), lambda i: (0, 0)),
                  pl.BlockSpec((N_SGU_HEADS, CHUNK, CHUNK), lambda i: (0, 0, 0)),
                  pl.BlockSpec((CHUNK, D_SGU), lambda i: (0, 0))],
        out_specs=pl.BlockSpec((CHUNK, D_SGU), lambda i: (i, 0)),
        out_shape=jax.ShapeDtypeStruct((s, D_SGU), F32),
        compiler_params=_cparams(("parallel",)),
    )(p_ab, ln_g, ln_b, w_s, bias)


def _sgu_bwd(p_ab, dyb, ln_g, ln_b, w_s, bias, *, name):
    s = p_ab.shape[0]
    n_chunks = s // CHUNK

    def body(p_ref, dy_ref, g_ref, beta_ref, w_ref, bias_ref, dp_ref, dw_ref, db_ref, dg_ref, dbeta_ref, dbias_ref):
        @pl.when(pl.program_id(0) == 0)
        def _():
            dw_ref[...] = jnp.zeros_like(dw_ref)
            dbias_ref[...] = jnp.zeros_like(dbias_ref)
            dg_ref[...] = jnp.zeros_like(dg_ref)
            dbeta_ref[...] = jnp.zeros_like(dbeta_ref)

        bu = p_ref[:, 0:D_SGU]
        bv = p_ref[:, D_SGU:2 * D_SGU]
        u = _gelu(bu)
        gv = g_ref[...]
        xhat, rstd, vn = _sgu_norm(bv, gv, beta_ref[...])
        vb = vn.astype(BF16)
        tril = _tril_mask()
        masks = _head_masks(D_SGU)
        whs = [jnp.where(tril, w_ref[h], 0.0).astype(BF16) for h in range(N_SGU_HEADS)]
        mixed = bias_ref[...]
        for h, m in enumerate(masks):
            mixed = mixed + _dot(whs[h], jnp.where(m, vb, jnp.zeros_like(vb)), _NN)
        dy = dy_ref[...]
        dp_ref[:, 0:D_SGU] = dy * mixed * _gelu_grad(bu)
        dmixed = dy * u
        dbias_ref[...] += dmixed
        dmb = dmixed.astype(BF16)
        dvn = jnp.zeros((CHUNK, D_SGU), F32)
        for h, m in enumerate(masks):
            dmh = jnp.where(m, dmb, jnp.zeros_like(dmb))
            dvn = dvn + _dot(whs[h], dmh, _TN)
            dw_ref[h] += jnp.where(tril, _dot(dmh, vb, _NT), 0.0)
        dg_ref[...] += jnp.sum(dvn * xhat, axis=0, keepdims=True)
        dbeta_ref[...] += jnp.sum(dvn, axis=0, keepdims=True)
        dxhat = dvn * gv
        dvg = rstd * (dxhat - jnp.mean(dxhat, axis=-1, keepdims=True)
                      - xhat * jnp.mean(dxhat * xhat, axis=-1, keepdims=True))
        dp_ref[:, D_SGU:2 * D_SGU] = dvg * _gelu_grad(bv)

        @pl.when(pl.program_id(0) == n_chunks - 1)
        def _():
            chan = lax.broadcasted_iota(jnp.int32, (D_SGU, LANES), 0)
            head = lax.broadcasted_iota(jnp.int32, (D_SGU, LANES), 1)
            to_head = jnp.where(chan // HEAD_DIM == head, 1.0, 0.0).astype(BF16)
            db_ref[...] = _dot(_split_bf16(dbias_ref[...]), jnp.concatenate([to_head, to_head], axis=0), _NN)

    vec = pl.BlockSpec((1, D_SGU), lambda i: (0, 0))
    wspec = pl.BlockSpec((N_SGU_HEADS, CHUNK, CHUNK), lambda i: (0, 0, 0))
    bspec = pl.BlockSpec((CHUNK, D_SGU), lambda i: (0, 0))
    return pl.pallas_call(
        body,
        name=name,
        grid=(n_chunks,),
        in_specs=[pl.BlockSpec((CHUNK, 2 * D_SGU), lambda i: (i, 1)), pl.BlockSpec((CHUNK, D_SGU), lambda i: (i, 0)),
                  vec, vec, wspec, bspec],
        out_specs=[pl.BlockSpec((CHUNK, 2 * D_SGU), lambda i: (i, 0)), wspec,
                   pl.BlockSpec((CHUNK, LANES), lambda i: (0, 0)), vec, vec],
        out_shape=[jax.ShapeDtypeStruct((s, 2 * D_SGU), F32),
                   jax.ShapeDtypeStruct((N_SGU_HEADS, CHUNK, CHUNK), F32),
                   jax.ShapeDtypeStruct((CHUNK, LANES), F32),
                   jax.ShapeDtypeStruct((1, D_SGU), F32), jax.ShapeDtypeStruct((1, D_SGU), F32)],
        scratch_shapes=[pltpu.VMEM((CHUNK, D_SGU), F32)],
        compiler_params=_cparams(("arbitrary",)),
    )(p_ab, dyb, ln_g, ln_b, w_s, bias)


N_PAIRS = D_SB // LANES
SB_SCALE = HEAD_DIM ** -0.5


def _sb_logits(z, valid):
    nz = -z
    t = jnp.log(1.0 + jnp.exp(jnp.minimum(z, nz)))
    l1 = jnp.minimum(nz, 0.0) - t
    if valid is not None:
        l1 = jnp.where(valid, l1, 0.0)
    return l1, jnp.minimum(z, 0.0) - t


def _split_hi_lo(x):
    hi = lax.bitcast_convert_type(lax.bitcast_convert_type(x, jnp.uint32) & jnp.uint32(0xFFFF0000), F32)
    return jnp.concatenate([hi, x - hi], axis=1)


def _cumsum_operand(keep):
    half = jnp.concatenate([keep.astype(F32), jnp.ones((CHUNK, CHUNK), F32)], axis=1)
    return jnp.concatenate([half, half], axis=0)


def _attn_fwd(qkv, *, name, exchanges=None):
    s = qkv.shape[0]
    nq = s // CHUNK
    ex = exchanges or _Exchanges([])
    n_ex = len(ex.arrays)

    def body(*refs):
        q_ref, k_ref, v_ref = refs[:3]
        o_ref, t_ref = refs[3 + n_ex:5 + n_ex]
        ex_refs = (refs[3:3 + n_ex], refs[5 + n_ex:5 + 2 * n_ex]) + refs[5 + 2 * n_ex:]
        qi = pl.program_id(1)
        if n_ex:
            @pl.when((pl.program_id(0) == 0) & (qi == 0))
            def _():
                ex.start(*ex_refs)

        q = q_ref[...] * SB_SCALE
        zero = jnp.zeros_like(q)
        masks = _head_masks(LANES)
        qs = [jnp.where(m, q, zero) for m in masks]
        row = lax.broadcasted_iota(jnp.int32, (CHUNK, CHUNK), 0)
        col = lax.broadcasted_iota(jnp.int32, (CHUNK, CHUNK), 1)
        after_op = _cumsum_operand(row > col)

        cmr = col - row

        def blocks(js, carry):
            o, c0, c1 = carry
            kvs, valids = [], []
            for j in js:
                k0 = pl.multiple_of(jnp.maximum(j, 0) * CHUNK, CHUNK)
                kvs.append((k_ref[pl.ds(k0, CHUNK), :], v_ref[pl.ds(k0, CHUNK), :]))
                valids.append(cmr < jnp.where(j >= 0, (qi - j) * CHUNK, -CHUNK))
            units = [(h, b) for b in range(len(js)) for h in range(2)]
            zs = [_dot(qs[h], kvs[b][0], _NT) for h, b in units]
            logits = [_sb_logits(z, valids[b]) for z, (h, b) in zip(zs, units)]
            sums = [_dot(_split_hi_lo(l1), after_op, _NN) for l1, _ in logits]
            cs = [c0, c1]
            probs = []
            for (h, b), (_, lb), sm in zip(units, logits, sums):
                probs.append(jnp.where(valids[b], jnp.exp(lb + sm[:, :CHUNK] + cs[h]), 0.0))
                cs[h] = cs[h] + sm[:, CHUNK:]
            for (h, b), a in zip(units, probs):
                o = o + _dot(a.astype(BF16), jnp.where(masks[h], kvs[b][1], zero), _NN)
            return o, cs[0], cs[1]

        zc = jnp.zeros((CHUNK, LANES), F32)
        o, c0, c1 = lax.fori_loop(0, (qi + 2) // 2, lambda jj, c: blocks([qi - 2 * jj, qi - 2 * jj - 1], c), (zc,) * 3)
        o_ref[...] = o
        t_ref[:, 0:LANES] = c0
        t_ref[:, LANES:2 * LANES] = c1
        if n_ex:
            @pl.when((pl.program_id(0) == N_PAIRS - 1) & (qi == nq - 1))
            def _():
                ex.wait(*ex_refs)

    return pl.pallas_call(
        body,
        name=name,
        grid=(N_PAIRS, nq),
        in_specs=[pl.BlockSpec((CHUNK, LANES), lambda p, i: (i, p)),
                  pl.BlockSpec((s, LANES), lambda p, i: (0, N_PAIRS + p)),
                  pl.BlockSpec((s, LANES), lambda p, i: (0, 2 * N_PAIRS + p))] + ex.in_specs,
        out_specs=[pl.BlockSpec((CHUNK, LANES), lambda p, i: (i, p)),
                   pl.BlockSpec((CHUNK, 2 * LANES), lambda p, i: (i, p))] + ex.out_specs,
        out_shape=[jax.ShapeDtypeStruct((s, D_SB), F32), jax.ShapeDtypeStruct((s, 2 * D_SB), F32)] + ex.out_shapes,
        scratch_shapes=ex.scratch_shapes if n_ex else [],
        compiler_params=_cparams(("arbitrary", "arbitrary")),
    )(qkv, qkv, qkv, *ex.arrays)


def _attn_bwd(qkv, t_tot, do, *, name, exchanges=None):
    s = qkv.shape[0]
    nq = s // CHUNK
    ex = exchanges or _Exchanges([])
    n_ex = len(ex.arrays)

    def body(*refs):
        q_ref, k_ref, v_ref, t_ref, do_ref = refs[:5]
        dq_ref, dk_ref, dv_ref = refs[5 + n_ex:8 + n_ex]
        ex_refs = (refs[5:5 + n_ex], refs[8 + n_ex:8 + 2 * n_ex]) + refs[8 + 2 * n_ex:]
        qi = pl.program_id(1)
        if n_ex:
            @pl.when((pl.program_id(0) == 0) & (qi == 0))
            def _():
                ex.start(*ex_refs)

        @pl.when(qi == 0)
        def _():
            dk_ref[...] = jnp.zeros_like(dk_ref)
            dv_ref[...] = jnp.zeros_like(dv_ref)

        q = q_ref[...] * SB_SCALE
        dob = do_ref[...].astype(BF16)
        zero = jnp.zeros_like(q)
        masks = _head_masks(LANES)
        qs = [jnp.where(m, q, zero) for m in masks]
        dos = [jnp.where(m, dob, zero) for m in masks]
        tots = [t_ref[:, 0:LANES], t_ref[:, LANES:2 * LANES]]
        row = lax.broadcasted_iota(jnp.int32, (CHUNK, CHUNK), 0)
        col = lax.broadcasted_iota(jnp.int32, (CHUNK, CHUNK), 1)
        upto_op = _cumsum_operand(row <= col)
        before_op = _cumsum_operand(row < col)

        cmr = col - row

        def blocks(js, carry):
            dq, cl0, cl1, cp0, cp1 = carry
            starts = [pl.multiple_of(jnp.minimum(j, nq - 1) * CHUNK, CHUNK) for j in js]
            valids = [cmr < (qi - j) * CHUNK for j in js]
            kvs = [(k_ref[pl.ds(k0, CHUNK), :], v_ref[pl.ds(k0, CHUNK), :]) for k0 in starts]
            units = [(h, b) for b in range(len(js)) for h in range(2)]
            zs = [_dot(qs[h], kvs[b][0], _NT) for h, b in units]
            das = [_dot(dos[h], kvs[b][1], _NT) for h, b in units]
            logits = [_sb_logits(z, valids[b]) for z, (h, b) in zip(zs, units)]
            sums = [_dot(_split_hi_lo(l1), upto_op, _NN) for l1, _ in logits]
            cls, cps = [cl0, cl1], [cp0, cp1]
            probs, gs = [], []
            for (h, b), (_, lb), sm, da in zip(units, logits, sums, das):
                a = jnp.where(valids[b], jnp.exp(lb + (tots[h] - cls[h] - sm[:, :CHUNK])), 0.0)
                probs.append(a)
                gs.append(a * da)
                cls[h] = cls[h] + sm[:, CHUNK:]
            sums_g = [_dot(_split_hi_lo(g), before_op, _NN) for g in gs]
            dzs = []
            for (h, b), (_, lb), g, sg in zip(units, logits, gs, sums_g):
                dz = g - (g + sg[:, :CHUNK] + cps[h]) * jnp.exp(lb)
                dzs.append(jnp.where(valids[b], dz, 0.0).astype(BF16))
                cps[h] = cps[h] + sg[:, CHUNK:]
            for (h, b), dzb in zip(units, dzs):
                dq = dq + _dot(dzb, jnp.where(masks[h], kvs[b][0], zero), _NN)
            for b, k0 in enumerate(starts):
                dk_ref[pl.ds(k0, CHUNK), :] += _dot(dzs[2 * b], qs[0], _TN) + _dot(dzs[2 * b + 1], qs[1], _TN)
                dv_ref[pl.ds(k0, CHUNK), :] += (_dot(probs[2 * b].astype(BF16), dos[0], _TN)
                                                + _dot(probs[2 * b + 1].astype(BF16), dos[1], _TN))
            return dq, cls[0], cls[1], cps[0], cps[1]

        zc = jnp.zeros((CHUNK, LANES), F32)
        carry = lax.fori_loop(0, (qi + 2) // 2, lambda jj, c: blocks([2 * jj, 2 * jj + 1], c), (zc,) * 5)
        dq_ref[...] = carry[0] * SB_SCALE
        if n_ex:
            @pl.when((pl.program_id(0) == N_PAIRS - 1) & (qi == nq - 1))
            def _():
                ex.wait(*ex_refs)

    blk = pl.BlockSpec((CHUNK, LANES), lambda p, i: (i, p))
    col_blk = pl.BlockSpec((s, LANES), lambda p, i: (0, p))
    out = jax.ShapeDtypeStruct((s, D_SB), F32)
    return pl.pallas_call(
        body,
        name=name,
        grid=(N_PAIRS, nq),
        in_specs=[blk,
                  pl.BlockSpec((s, LANES), lambda p, i: (0, N_PAIRS + p)),
                  pl.BlockSpec((s, LANES), lambda p, i: (0, 2 * N_PAIRS + p)),
                  pl.BlockSpec((CHUNK, 2 * LANES), lambda p, i: (i, p)),
                  blk] + ex.in_specs,
        out_specs=[blk, col_blk, col_blk] + ex.out_specs,
        out_shape=[out, out, out] + ex.out_shapes,
        scratch_shapes=ex.scratch_shapes if n_ex else [],
        compiler_params=_cparams(("arbitrary", "arbitrary")),
    )(qkv, qkv, qkv, t_tot, do, *ex.arrays)


FFN_TILE = 256
FFN_COLS = 256
N_FF_BLOCKS = D_FF // FFN_COLS


def _ffn_act_fwd(up0, conv_w, conv_b, *, name):
    s = up0.shape[0]
    nt = s // FFN_TILE

    def body(xg_ref, xv_ref, wg_ref, wv_ref, bg_ref, bv_ref, act_ref, pg_ref, pv_ref):
        pg_ref[0:FFN_HALO, :] = jnp.zeros((FFN_HALO, FFN_COLS), F32)
        pv_ref[0:FFN_HALO, :] = jnp.zeros((FFN_HALO, FFN_COLS), F32)
        pg_ref[FFN_HALO:, :] = xg_ref[...]
        pv_ref[FFN_HALO:, :] = xv_ref[...]

        def tile(i, c):
            t0 = pl.multiple_of(i * FFN_TILE, FFN_TILE)
            outs = []
            for p_ref, w_ref, b_ref in ((pg_ref, wg_ref, bg_ref), (pv_ref, wv_ref, bv_ref)):
                window = p_ref[pl.ds(t0, FFN_TILE + FFN_HALO), :]
                acc = b_ref[...] + w_ref[2:3, :] * window[FFN_HALO:, :]
                for j in range(1, FFN_K):
                    acc = acc + w_ref[FFN_K - 1 - j:FFN_K - j, :] * _shift_down(window, j, FFN_HALO)
                outs.append(acc)
            gate, val = outs
            act_ref[pl.ds(t0, FFN_TILE), :] = (gate * _sigmoid(gate) * val).astype(BF16)
            return c

        lax.fori_loop(0, nt, tile, 0)

    gcol = lambda rows: pl.BlockSpec((rows, FFN_COLS), lambda j: (0, j))
    vcol = lambda rows: pl.BlockSpec((rows, FFN_COLS), lambda j: (0, j + N_FF_BLOCKS))
    return pl.pallas_call(
        body,
        name=name,
        grid=(N_FF_BLOCKS,),
        in_specs=[gcol(s), vcol(s), gcol(FFN_K), vcol(FFN_K), gcol(1), vcol(1)],
        out_specs=gcol(s),
        out_shape=jax.ShapeDtypeStruct((s, D_FF), BF16),
        scratch_shapes=[pltpu.VMEM((s + FFN_HALO, FFN_COLS), F32), pltpu.VMEM((s + FFN_HALO, FFN_COLS), F32)],
        compiler_params=_cparams(("parallel",)),
    )(up0, up0, conv_w, conv_w, conv_b, conv_b)


def _ffn_act_bwd(up0, dact, conv_w, conv_b, *, name):
    s = up0.shape[0]
    nt = s // FFN_TILE

    def body(xm_ref, xp_ref, da_ref, wm_ref, wp_ref, bm_ref, bp_ref, dx_ref, dw_ref, db_ref, pm_ref, pp_ref, dm_ref):
        is_gate = pl.program_id(0) < N_FF_BLOCKS
        zeros = jnp.zeros((FFN_HALO, FFN_COLS), F32)
        pm_ref[0:FFN_HALO, :] = zeros
        pp_ref[0:FFN_HALO, :] = zeros
        pm_ref[FFN_HALO:, :] = xm_ref[...]
        pp_ref[FFN_HALO:, :] = xp_ref[...]
        dm_ref[s:s + FFN_HALO, :] = zeros
        dw_ref[...] = jnp.zeros_like(dw_ref)
        db_ref[...] = jnp.zeros_like(db_ref)

        def tile(i, c):
            t0 = pl.multiple_of(i * FFN_TILE, FFN_TILE)
            wm = pm_ref[pl.ds(t0, FFN_TILE + FFN_HALO), :]
            wp = pp_ref[pl.ds(t0, FFN_TILE + FFN_HALO), :]
            taps = [_shift_down(wm, j, FFN_HALO) for j in range(FFN_K)]
            mine = bm_ref[...]
            partner = bp_ref[...]
            for j in range(FFN_K):
                mine = mine + wm_ref[FFN_K - 1 - j:FFN_K - j, :] * taps[j]
                partner = partner + wp_ref[FFN_K - 1 - j:FFN_K - j, :] * _shift_down(wp, j, FFN_HALO)
            da = da_ref[pl.ds(t0, FFN_TILE), :]
            sg_m = _sigmoid(mine)
            sg_p = _sigmoid(partner)
            d_as_gate = da * partner * (sg_m * (1.0 + mine * (1.0 - sg_m)))
            d_as_val = da * partner * sg_p
            dm = jnp.where(is_gate, d_as_gate, d_as_val)
            dm_ref[pl.ds(t0, FFN_TILE), :] = dm
            db_ref[...] += jnp.sum(dm, axis=0, keepdims=True)
            for j in range(FFN_K):
                dw_ref[FFN_K - 1 - j:FFN_K - j, :] += jnp.sum(dm * taps[j], axis=0, keepdims=True)
            return c

        lax.fori_loop(0, nt, tile, 0)

        def tile_dx(i, c):
            t0 = pl.multiple_of(i * FFN_TILE, FFN_TILE)
            window = dm_ref[pl.ds(t0, FFN_TILE + FFN_HALO), :]
            dx = jnp.zeros((FFN_TILE, FFN_COLS), F32)
            for j in range(FFN_K):
                dx = dx + wm_ref[FFN_K - 1 - j:FFN_K - j, :] * _shift_up(window, j, FFN_TILE)
            dx_ref[pl.ds(t0, FFN_TILE), :] = dx
            return c

        lax.fori_loop(0, nt, tile_dx, 0)

    nb = 2 * N_FF_BLOCKS
    mine = lambda rows: pl.BlockSpec((rows, FFN_COLS), lambda j: (0, j))
    partner = lambda rows: pl.BlockSpec((rows, FFN_COLS), lambda j: (0, (j + N_FF_BLOCKS) % nb))
    return pl.pallas_call(
        body,
        name=name,
        grid=(nb,),
        in_specs=[mine(s), partner(s), pl.BlockSpec((s, FFN_COLS), lambda j: (0, j % N_FF_BLOCKS)),
                  mine(FFN_K), partner(FFN_K), mine(1), partner(1)],
        out_specs=[mine(s), mine(FFN_K), mine(1)],
        out_shape=[jax.ShapeDtypeStruct((s, 2 * D_FF), F32), jax.ShapeDtypeStruct((FFN_K, 2 * D_FF), F32),
                   jax.ShapeDtypeStruct((1, 2 * D_FF), F32)],
        scratch_shapes=[pltpu.VMEM((s + FFN_HALO, FFN_COLS), F32), pltpu.VMEM((s + FFN_HALO, FFN_COLS), F32),
                        pltpu.VMEM((s + FFN_HALO, FFN_COLS), F32)],
        compiler_params=_cparams(("parallel",)),
    )(up0, up0, dact, conv_w, conv_w, conv_b, conv_b)


MESH = pl.DeviceIdType.MESH


def _position():
    x, y, c = lax.axis_index("x"), lax.axis_index("y"), lax.axis_index("c")
    return x, y, c, 4 * x + 2 * y + c


def _peer(k):
    x, y, c, _ = _position()
    px = 1 - x if k & 4 else x
    py = 1 - y if k & 2 else y
    pc = 1 - c if k & 1 else c
    return (px, py, pc), 4 * px + 2 * py + pc


def _exchange(src, *, kind, name):
    ex = _Exchanges([(kind, src)])

    def body(src_ref, out_ref, send_sems, recv_sems, local_sems):
        ex.start([src_ref], [out_ref], send_sems, recv_sems, local_sems)
        ex.wait([src_ref], [out_ref], send_sems, recv_sems, local_sems)

    return pl.pallas_call(
        body,
        name=name,
        in_specs=ex.in_specs,
        out_specs=ex.out_specs[0],
        out_shape=ex.out_shapes[0],
        scratch_shapes=ex.scratch_shapes,
    )(src)


class _Exchanges:
    def __init__(self, items):
        self.kinds = [kind for kind, _ in items]
        self.arrays = [src for _, src in items]
        self.out_shapes = []
        self.block_rows = []
        for kind, src in items:
            scatter, by_rows = kind.startswith("scatter"), kind.endswith("rows")
            if by_rows:
                r = src.shape[0] // N_DEV if scatter else src.shape[0]
                shape = (N_DEV, r, src.shape[1]) if scatter else (N_DEV * r, src.shape[1])
            else:
                r = None
                shape = src.shape if scatter else (N_DEV,) + src.shape
            self.block_rows.append(r)
            self.out_shapes.append(jax.ShapeDtypeStruct(shape, src.dtype))
        n = len(items)
        self.in_specs = [pl.BlockSpec(memory_space=pl.ANY)] * n
        self.out_specs = [pl.BlockSpec(memory_space=pl.ANY)] * n
        self.scratch_shapes = [pltpu.SemaphoreType.DMA((n * (N_DEV - 1),)), pltpu.SemaphoreType.DMA((n * (N_DEV - 1),)),
                               pltpu.SemaphoreType.DMA((n,))]

    def _copies(self, i, src_ref, out_ref, send_sems, recv_sems, local_sems):
        kind, r = self.kinds[i], self.block_rows[i]
        scatter, by_rows = kind.startswith("scatter"), kind.endswith("rows")
        me = _position()[3]

        def rows(ref, idx):
            return ref.at[pl.ds(pl.multiple_of(idx * r, r), r), :]

        def outgoing(idx):
            if not scatter:
                return src_ref
            return rows(src_ref, idx) if by_rows else src_ref.at[idx]

        def slot(idx):
            return rows(out_ref, idx) if (by_rows and not scatter) else out_ref.at[idx]

        local = pltpu.make_async_copy(outgoing(me), slot(me), local_sems.at[i])
        sends, recvs = [], []
        for k in range(1, N_DEV):
            peer, pidx = _peer(k)
            sem = i * (N_DEV - 1) + k - 1
            sends.append(pltpu.make_async_remote_copy(src_ref=outgoing(pidx), dst_ref=slot(me), send_sem=send_sems.at[sem],
                                                      recv_sem=recv_sems.at[sem], device_id=peer, device_id_type=MESH))
            recvs.append(pltpu.make_async_remote_copy(src_ref=outgoing(pidx), dst_ref=slot(pidx), send_sem=send_sems.at[sem],
                                                      recv_sem=recv_sems.at[sem], device_id=peer, device_id_type=MESH))
        return local, sends, recvs

    def start(self, src_refs, out_refs, *sems):
        for i, (src_ref, out_ref) in enumerate(zip(src_refs, out_refs)):
            local, sends, _ = self._copies(i, src_ref, out_ref, *sems)
            local.start()
            for cp in sends:
                cp.start()

    def wait(self, src_refs, out_refs, *sems):
        for i, (src_ref, out_ref) in enumerate(zip(src_refs, out_refs)):
            local, sends, recvs = self._copies(i, src_ref, out_ref, *sems)
            for cp in recvs:
                cp.wait_recv()
            for cp in sends:
                cp.wait_send()
            local.wait()


def _row_tile(rows):
    return _tile(rows, (256, 128, 64, 32, 16, 8))


def _layer_parts_specs(n_layers, n_parts, tr, cols):
    return [pl.BlockSpec((n_parts, tr, cols), lambda l, i, j=j: (0, jnp.where(l == j, i, 0), 0)) for j in range(n_layers)]


def _select_layer_sum(p_refs):
    l = pl.program_id(0)
    g = None
    for j, p_ref in enumerate(p_refs):
        gj = p_ref[0].astype(F32)
        for k in range(1, p_ref.shape[0]):
            gj = gj + p_ref[k].astype(F32)
        g = gj if g is None else jnp.where(l == j, gj, g)
    return g


def _sum_parts(parts, *, name):
    n_layers = len(parts)
    n_parts, rows, cols = parts[0].shape
    tr = _row_tile(rows)

    def body(*refs):
        refs[-1][...] = _select_layer_sum(refs[:n_layers])

    return pl.pallas_call(
        body,
        name=name,
        grid=(n_layers, rows // tr),
        in_specs=_layer_parts_specs(n_layers, n_parts, tr, cols),
        out_specs=pl.BlockSpec((None, tr, cols), lambda l, i: (l, i, 0)),
        out_shape=jax.ShapeDtypeStruct((n_layers, rows, cols), F32),
        compiler_params=_cparams(("arbitrary", "arbitrary")),
    )(*parts)


def _adamw(parts, w, m, v, *, name):
    n_layers, rows, cols = w.shape
    summed = not isinstance(parts, (list, tuple))
    tr = _row_tile(rows)
    n_in = 1 if summed else n_layers

    def body(*refs):
        w_ref, m_ref, v_ref, g_ref, d_ref, m2_ref, v2_ref = refs[n_in:]
        g = refs[0][...] if summed else _select_layer_sum(refs[:n_in])
        m2 = ADAM_B1 * m_ref[...] + (1.0 - ADAM_B1) * g
        v2 = ADAM_B2 * v_ref[...] + (1.0 - ADAM_B2) * (g * g)
        m_hat = m2 / (1.0 - ADAM_B1 ** ADAM_STEP)
        v_hat = v2 / (1.0 - ADAM_B2 ** ADAM_STEP)
        g_ref[...] = g
        d_ref[...] = -ADAM_LR * (m_hat / (jnp.sqrt(v_hat) + ADAM_EPS) + ADAM_WD * w_ref[...])
        m2_ref[...] = m2
        v2_ref[...] = v2

    slab = pl.BlockSpec((None, tr, cols), lambda l, i: (l, i, 0))
    out = jax.ShapeDtypeStruct((n_layers, rows, cols), F32)
    p_specs = [slab] if summed else _layer_parts_specs(n_layers, parts[0].shape[0], tr, cols)
    return pl.pallas_call(
        body,
        name=name,
        grid=(n_layers, rows // tr),
        in_specs=p_specs + [slab, slab, slab],
        out_specs=[slab, slab, slab, slab],
        out_shape=[out, out, out, out],
        compiler_params=_cparams(("arbitrary", "arbitrary")),
    )(*((parts,) if summed else tuple(parts)), w, m, v)


SLAB_ROWS = 256
_SMALL_SHARDED = (("conv_w", (2, 31, 32)), ("ffn_conv_w", (2, 3, 704)))
_REPLICATED = (("g_mix", (2, 1024)), ("conv_b", (2, 256)), ("conv_ln_g", (2, 256)), ("conv_ln_b", (2, 256)),
               ("sgu_ln_g", (2, 256)), ("sgu_ln_b", (2, 256)), ("sgu_w", (2, 4, 128, 128)), ("sgu_b", (2, 4, 128)),
               ("g_out", (2, 1024)), ("g_ffn", (2, 1024)), ("ffn_conv_b", (2, 5632)), ("g_final", (1024,)))


def _seg_rows(n_elems):
    return -(-n_elems // LANES)


def _pack(arrays, lead=()):
    segs = []
    for a in arrays:
        flat = a.reshape(lead + (-1,)).astype(F32)
        pad = _seg_rows(flat.shape[-1]) * LANES - flat.shape[-1]
        if pad:
            flat = jnp.pad(flat, [(0, 0)] * len(lead) + [(0, pad)])
        segs.append(flat)
    flat = jnp.concatenate(segs, axis=-1)
    rows = flat.shape[-1] // LANES
    pad_rows = -rows % SLAB_ROWS
    if pad_rows:
        flat = jnp.pad(flat, [(0, 0)] * len(lead) + [(0, pad_rows * LANES)])
    return flat.reshape(lead + (rows + pad_rows, LANES))


def _unpack(slab, shapes, lead=()):
    flat = slab.reshape(lead + (-1,))
    out, off = [], 0
    for shape in shapes:
        n = math.prod(shape)
        out.append(flat[..., off:off + n].reshape(lead + tuple(shape)))
        off += _seg_rows(n) * LANES
    return out


def _split_last(full):
    split = full.shape[:-1] + (N_DEV, full.shape[-1] // N_DEV)
    return jnp.moveaxis(full.reshape(split), -2, 0)


def _join_last(blocks):
    moved = jnp.moveaxis(blocks, 0, -2)
    return moved.reshape(moved.shape[:-2] + (moved.shape[-2] * moved.shape[-1],))


def _layer_fwd(l, x, wt, small, gathers):
    tag = f"l{l}"
    h = _rmsnorm_fwd(x, small["g_mix"][l][None], name=f"{tag}_norm_mix")
    p_ab = _matmul(h, wt["w_in_t"][l], "nt", name=f"{tag}_proj_ab", n=D_AB)
    qkv = _matmul(h, wt["w_in_t"][l], "nt", name=f"{tag}_proj_qkv", n=D_QKV, b_n0=D_AB, out_dtype=BF16)
    ya = _mixer_a_fwd(p_ab, wt["conv_w"][l], small["conv_b"][l][None], small["conv_ln_g"][l][None],
                      small["conv_ln_b"][l][None], name=f"{tag}_mixer_a")
    bias = jnp.repeat(small["sgu_b"][l].T, HEAD_DIM, axis=1)
    yb = _sgu_fwd(p_ab, small["sgu_ln_g"][l][None], small["sgu_ln_b"][l][None], small["sgu_w"][l], bias,
                  name=f"{tag}_sgu")
    yc, t_tot, *gathered = _attn_fwd(qkv, name=f"{tag}_attn",
                                     exchanges=_Exchanges([("gather_rows", a) for _, _, a in gathers]))
    for (n, j, _), full in zip(gathers, gathered):
        wt[n][j] = full
    y = _combine_fwd(ya, yb, yc, small["g_out"][l][None], name=f"{tag}_combine")
    x1 = _matmul(y, wt["w_out"][l], "nn", name=f"{tag}_out_proj", residual=x)
    h2 = _rmsnorm_fwd(x1, small["g_ffn"][l][None], name=f"{tag}_norm_ffn")
    up0 = _matmul(h2, wt["w_up_t"][l], "nt", name=f"{tag}_up")
    act = _ffn_act_fwd(up0, wt["ffn_conv_w"][l], small["ffn_conv_b"][l][None], name=f"{tag}_ffn_act")
    x2 = _matmul(act, wt["w_down"][l], "nn", name=f"{tag}_down", residual=x1)
    saved = dict(x=x, h=h, p_ab=p_ab, qkv=qkv, ya=ya, yb=yb, yc=yc, t_tot=t_tot, y=y, x1=x1, h2=h2, up0=up0,
                 act=act, bias=bias)
    return x2, saved


def _layer_bwd(l, dx2, sv, wt, small, pending, received):
    tag = f"l{l}b"
    big, g = {}, {}
    dact = _matmul(dx2, wt["w_down"][l], "nt", name=f"{tag}_dact")
    big["w_down"] = _matmul(sv["act"], dx2, "tn", name=f"{tag}_dw_down", out_dtype=BF16)
    dup0, g["ffn_conv_w"], dfb = _ffn_act_bwd(sv["up0"], dact, wt["ffn_conv_w"][l], small["ffn_conv_b"][l][None],
                                              name=f"{tag}_ffn_act")
    g["ffn_conv_b"] = dfb[0]
    dh2 = _matmul(dup0, wt["w_up_t"][l], "nn", name=f"{tag}_dh2")
    big["w_up_t"] = _matmul(dup0, sv["h2"], "tn", name=f"{tag}_dw_up", out_dtype=BF16)
    dx1, dg = _rmsnorm_bwd(sv["x1"], small["g_ffn"][l][None], dh2, dx2, name=f"{tag}_norm_ffn")
    g["g_ffn"] = dg[0]
    dy = _matmul(dx1, wt["w_out"][l], "nt", name=f"{tag}_dy")
    big["w_out"] = _matmul(sv["y"], dx1, "tn", name=f"{tag}_dw_out", out_dtype=BF16)
    dya, dyb, dyc, dg = _combine_bwd(dy, sv["ya"], sv["yb"], sv["yc"], small["g_out"][l][None], name=f"{tag}_combine")
    g["g_out"] = dg[0]
    scatters = pending + [("w_down", l, big["w_down"]), ("w_up_t", l, big["w_up_t"])]
    dq, dk, dv, *landed = _attn_bwd(sv["qkv"], sv["t_tot"], dyc, name=f"{tag}_attn",
                                    exchanges=_Exchanges([("scatter_rows", a) for _, _, a in scatters]))
    for (n, j, _), got in zip(scatters, landed):
        received[n][j] = got
    dp_b, g["sgu_w"], db, dg, dbeta = _sgu_bwd(sv["p_ab"], dyb, small["sgu_ln_g"][l][None], small["sgu_ln_b"][l][None],
                                               small["sgu_w"][l], sv["bias"], name=f"{tag}_sgu")
    g["sgu_b"] = db[:, :N_SGU_HEADS].T
    g["sgu_ln_g"], g["sgu_ln_b"] = dg[0], dbeta[0]
    dp_a, g["conv_w"], dcb, dg, dbeta = _mixer_a_bwd(sv["p_ab"], dya, wt["conv_w"][l], small["conv_b"][l][None],
                                                     small["conv_ln_g"][l][None], small["conv_ln_b"][l][None],
                                                     name=f"{tag}_mixer_a")
    g["conv_b"], g["conv_ln_g"], g["conv_ln_b"] = dcb[0], dg[0], dbeta[0]
    dp = jnp.concatenate([dp_a.astype(BF16), dp_b.astype(BF16), dq.astype(BF16), dk.astype(BF16), dv.astype(BF16)],
                         axis=1)
    dh = _matmul(dp, wt["w_in_t"][l], "nn", name=f"{tag}_dh")
    big["w_in_t"] = _matmul(dp, sv["h"], "tn", name=f"{tag}_dw_in", out_dtype=BF16)
    dx, dg = _rmsnorm_bwd(sv["x"], small["g_mix"][l][None], dh, dx1, name=f"{tag}_norm_mix")
    g["g_mix"] = dg[0]
    return dx, [("w_out", l, big["w_out"]), ("w_in_t", l, big["w_in_t"])], g


_BIG = ("w_in_t", "w_out", "w_up_t", "w_down")


def kernel(x, g_mix, w_in, conv_w, conv_b, conv_ln_g, conv_ln_b, sgu_ln_g, sgu_ln_b, sgu_w, sgu_b, g_out, w_out, g_ffn, w_up, ffn_conv_w, ffn_conv_b, w_down, g_final, loss_target, m_g_mix, m_w_in, m_conv_w, m_conv_b, m_conv_ln_g, m_conv_ln_b, m_sgu_ln_g, m_sgu_ln_b, m_sgu_w, m_sgu_b, m_g_out, m_w_out, m_g_ffn, m_w_up, m_ffn_conv_w, m_ffn_conv_b, m_w_down, m_g_final, v_g_mix, v_w_in, v_conv_w, v_conv_b, v_conv_ln_g, v_conv_ln_b, v_sgu_ln_g, v_sgu_ln_b, v_sgu_w, v_sgu_b, v_g_out, v_w_out, v_g_ffn, v_w_up, v_ffn_conv_w, v_ffn_conv_b, v_w_down, v_g_final):
    given = dict(locals())
    n_layers = g_mix.shape[0]
    layers = range(n_layers)
    small_sharded = [n for n, _ in _SMALL_SHARDED]
    replicated = [n for n, _ in _REPLICATED]
    small = {n: given[n] for n in replicated}

    shard = {"w_in_t": [w_in[l].T.astype(BF16) for l in layers], "w_out": [w_out[l].astype(BF16) for l in layers],
             "w_up_t": [w_up[l].T.astype(BF16) for l in layers], "w_down": [w_down[l].astype(BF16) for l in layers]}
    wt = {n: [None] * n_layers for n in _BIG}
    wt["w_in_t"][0] = _exchange(shard["w_in_t"][0], kind="gather_rows", name="gather_w_in_t_l0")
    filters = _exchange(_pack([given[n] for n in small_sharded]), kind="gather_blocks", name="gather_filters")
    for n, blocks in zip(small_sharded, _unpack(filters, [s for _, s in _SMALL_SHARDED], lead=(N_DEV,))):
        wt[n] = _join_last(blocks)

    xs = x[0]
    saved = []
    for l in layers:
        gathers = [(n, l, shard[n][l]) for n in ("w_out", "w_up_t", "w_down")]
        if l + 1 < n_layers:
            gathers.append(("w_in_t", l + 1, shard["w_in_t"][l + 1]))
        xs, sv = _layer_fwd(l, xs, wt, small, gathers)
        saved.append(sv)
    loss_tile, dx, dgf = _loss_head(xs, g_final[None], loss_target[0], name="loss_head")
    received = {n: [None] * n_layers for n in _BIG}
    grads = [None] * n_layers
    pending = []
    for l in reversed(layers):
        dx, pending, grads[l] = _layer_bwd(l, dx, saved[l], wt, small, pending, received)
    for n, l, part in pending:
        received[n][l] = _exchange(part, kind="scatter_rows", name=f"scatter_{n}_l{l}")
    partial = {n: jnp.stack([g[n] for g in grads]) for n in grads[0]}
    partial["g_final"] = dgf[0]

    out = {}

    def update(n, parts):
        results = _adamw(parts, given[n], given["m_" + n], given["v_" + n], name=f"adamw_{n}")
        for pre, res in zip(("grad_", "delta_", "new_m_", "new_v_"), results):
            out[pre + n] = res

    update("w_out", received["w_out"])
    update("w_down", received["w_down"])
    for n in ("w_in", "w_up"):
        update(n, jnp.swapaxes(_sum_parts(received[n + "_t"], name=f"sum_{n}"), 1, 2))

    own = _pack([_split_last(partial[n]) for n in small_sharded], lead=(N_DEV,))
    shared = _pack([partial[n] for n in replicated])
    slab = jnp.concatenate([own, jnp.broadcast_to(shared[None], (N_DEV,) + shared.shape)], axis=1)
    slab = _exchange(slab, kind="scatter_blocks", name="scatter_small_grads")
    stacks = [jnp.concatenate([_pack([given[pre + n] for n in small_sharded]),
                               _pack([given[pre + n] for n in replicated])])[None] for pre in ("", "m_", "v_")]
    results = _adamw([slab], *stacks, name="adamw_small")
    n_own = own.shape[1]
    for pre, res in zip(("grad_", "delta_", "new_m_", "new_v_"), results):
        unpacked = (_unpack(res[0, :n_own], [s for _, s in _SMALL_SHARDED])
                    + _unpack(res[0, n_own:], [s for _, s in _REPLICATED]))
        for n, a in zip(small_sharded + replicated, unpacked):
            out[pre + n] = a

    loss = lax.psum(loss_tile[0, 0], ("x", "y", "c"))
    order = list(_WEIGHT_ORDER)
    return (loss, dx[None], *[out["grad_" + n] for n in order], *[out["delta_" + n] for n in order],
            *[out["new_m_" + n] for n in order], *[out["new_v_" + n] for n in order])


_WEIGHT_ORDER = ("g_mix", "w_in", "conv_w", "conv_b", "conv_ln_g", "conv_ln_b", "sgu_ln_g", "sgu_ln_b", "sgu_w", "sgu_b",
                 "g_out", "w_out", "g_ffn", "w_up", "ffn_conv_w", "ffn_conv_b", "w_down", "g_final")
```

```python
import functools
import math

import jax
import jax.numpy as jnp
from jax import lax
from jax.experimental import pallas as pl
from jax.experimental.pallas import tpu as pltpu

F32 = jnp.float32
BF16 = jnp.bfloat16

N_DEV = 8
D_MODEL = 1024
HEAD_DIM = 64
D_CONV = 256
D_SGU = 256
D_SB = 512
D_AB = 2 * D_CONV + 2 * D_SGU
D_QKV = 3 * D_SB
D_IN = D_AB + D_QKV
CONV_K = 31
CONV_HALO = 32
FFN_K = 3
FFN_HALO = 8
D_FF = 2816
CHUNK = 128
EPS = 1e-6
LANES = 128

ADAM_LR = 0.001
ADAM_B1 = 0.9
ADAM_B2 = 0.999
ADAM_EPS = 1e-08
ADAM_WD = 0.01
ADAM_STEP = 10

VMEM_LIMIT = 56 * 1024 * 1024


def _cparams(sem=None):
    return pltpu.CompilerParams(dimension_semantics=sem, vmem_limit_bytes=VMEM_LIMIT)


def _tile(n, prefs=(512, 256, 128)):
    for t in prefs:
        if n % t == 0:
            return t
    return n


def _sigmoid(x):
    return 1.0 / (1.0 + jnp.exp(-x))


def _softplus(x):
    return jnp.maximum(x, 0.0) + jnp.log1p(jnp.exp(-jnp.abs(x)))


_INV_SQRT2 = 1.0 / math.sqrt(2.0)
_INV_SQRT2PI = 1.0 / math.sqrt(2.0 * math.pi)


def _gelu(x):
    return 0.5 * x * (1.0 + lax.erf(x * _INV_SQRT2))


def _gelu_grad(x):
    return 0.5 * (1.0 + lax.erf(x * _INV_SQRT2)) + x * jnp.exp(-0.5 * x * x) * _INV_SQRT2PI


def _dot(a, b, dims):
    return lax.dot_general(a, b, (dims, ((), ())), preferred_element_type=F32)


_NN = ((1,), (0,))
_NT = ((1,), (1,))
_TN = ((0,), (0,))


def _split_bf16(x):
    hi = x.astype(BF16)
    lo = (x - hi.astype(F32)).astype(BF16)
    return jnp.concatenate([hi, lo], axis=1)


def _matmul(a, b, mode, *, name, out_dtype=F32, residual=None, n=None, b_n0=0):
    if mode == "nn":
        (m, k), n = a.shape, (n or b.shape[1])
    elif mode == "nt":
        (m, k), n = a.shape, (n or b.shape[0])
    else:
        (k, m), n = a.shape, b.shape[1]
    has_res = residual is not None
    tm, tn = _matmul_tiles(m, n, k, a.dtype.itemsize, b.dtype.itemsize, jnp.dtype(out_dtype).itemsize, has_res, b_n0)
    j0 = b_n0 // tn

    if mode == "nn":
        a_spec = pl.BlockSpec((tm, k), lambda i, j: (i, 0))
        b_spec = pl.BlockSpec((k, tn), lambda i, j: (0, j + j0))
        dims = _NN
    elif mode == "nt":
        a_spec = pl.BlockSpec((tm, k), lambda i, j: (i, 0))
        b_spec = pl.BlockSpec((tn, k), lambda i, j: (j + j0, 0))
        dims = _NT
    else:
        a_spec = pl.BlockSpec((k, tm), lambda i, j: (0, i))
        b_spec = pl.BlockSpec((k, tn), lambda i, j: (0, j))
        dims = _TN
    o_spec = pl.BlockSpec((tm, tn), lambda i, j: (i, j))

    def body(*refs):
        a_ref, b_ref = refs[:2]
        acc = _dot(a_ref[...].astype(BF16), b_ref[...].astype(BF16), dims)
        if has_res:
            acc = acc + refs[2][...]
        refs[-1][...] = acc.astype(out_dtype)

    in_specs = [a_spec, b_spec] + ([o_spec] if has_res else [])
    args = (a, b) + ((residual,) if has_res else ())
    return pl.pallas_call(
        body,
        name=name,
        grid=(m // tm, n // tn),
        in_specs=in_specs,
        out_specs=o_spec,
        out_shape=jax.ShapeDtypeStruct((m, n), out_dtype),
        compiler_params=_cparams(("parallel", "parallel")),
    )(*args)


MATMUL_VMEM_BUDGET = 40 * 1024 * 1024


def _matmul_tiles(m, n, k, a_bytes, b_bytes, out_bytes, has_res, n_offset):
    def divisors(size, cap, also=0):
        return [t for t in range(cap, 0, -LANES) if size % t == 0 and also % t == 0] or [size]

    for tm in divisors(m, 1024):
        for tn in divisors(n, 1408, n_offset):
            blocks = tm * k * a_bytes + k * tn * b_bytes + tm * tn * (out_bytes + (4 if has_res else 0))
            if 2 * blocks <= MATMUL_VMEM_BUDGET:
                return tm, tn
    raise ValueError(f"no matmul tiling for {m} x {n} x {k}")


ROW_TILE = 256


def _rmsnorm_fwd(x, g, *, name):
    s, d = x.shape

    def body(x_ref, g_ref, h_ref):
        xv = x_ref[...]
        r = lax.rsqrt(jnp.mean(xv * xv, axis=-1, keepdims=True) + EPS)
        h_ref[...] = (xv * r * g_ref[...]).astype(BF16)

    return pl.pallas_call(
        body,
        name=name,
        grid=(s // ROW_TILE,),
        in_specs=[pl.BlockSpec((ROW_TILE, d), lambda i: (i, 0)), pl.BlockSpec((1, d), lambda i: (0, 0))],
        out_specs=pl.BlockSpec((ROW_TILE, d), lambda i: (i, 0)),
        out_shape=jax.ShapeDtypeStruct((s, d), BF16),
        compiler_params=_cparams(("parallel",)),
    )(x, g)


def _rmsnorm_bwd(x, g, dh, dres, *, name):
    s, d = x.shape

    def body(x_ref, g_ref, dh_ref, dres_ref, dx_ref, dxb_ref, dg_ref):
        xv = x_ref[...]
        r = lax.rsqrt(jnp.mean(xv * xv, axis=-1, keepdims=True) + EPS)
        xhat = xv * r
        dhv = dh_ref[...]
        dxhat = dhv * g_ref[...]
        dx = dres_ref[...] + r * (dxhat - xhat * jnp.mean(dxhat * xhat, axis=-1, keepdims=True))
        dx_ref[...] = dx
        dxb_ref[...] = dx.astype(BF16)
        part = jnp.sum(dhv * xhat, axis=0, keepdims=True)

        @pl.when(pl.program_id(0) == 0)
        def _():
            dg_ref[...] = part

        @pl.when(pl.program_id(0) > 0)
        def _():
            dg_ref[...] += part

    row = pl.BlockSpec((ROW_TILE, d), lambda i: (i, 0))
    vec = pl.BlockSpec((1, d), lambda i: (0, 0))
    return pl.pallas_call(
        body,
        name=name,
        grid=(s // ROW_TILE,),
        in_specs=[row, vec, row, row],
        out_specs=[row, row, vec],
        out_shape=[jax.ShapeDtypeStruct((s, d), F32), jax.ShapeDtypeStruct((s, d), BF16),
                   jax.ShapeDtypeStruct((1, d), F32)],
        compiler_params=_cparams(("arbitrary",)),
    )(x, g, dh, dres)


def _loss_head(x, g, target, *, name):
    s, d = x.shape

    def body(x_ref, g_ref, t_ref, loss_ref, dx_ref, dxb_ref, dg_ref):
        xv = x_ref[...]
        gv = g_ref[...]
        r = lax.rsqrt(jnp.mean(xv * xv, axis=-1, keepdims=True) + EPS)
        xhat = xv * r
        diff = xhat * gv - t_ref[...]
        dy = diff * (1.0 / d)
        dxhat = dy * gv
        dx = r * (dxhat - xhat * jnp.mean(dxhat * xhat, axis=-1, keepdims=True))
        dx_ref[...] = dx
        dxb_ref[...] = dx.astype(BF16)
        dg_part = jnp.sum(dy * xhat, axis=0, keepdims=True)
        row_loss = jnp.sum(diff * diff, axis=-1, keepdims=True)
        loss_part = jnp.sum(row_loss, axis=0, keepdims=True) * (0.5 / d)

        @pl.when(pl.program_id(0) == 0)
        def _():
            dg_ref[...] = dg_part
            loss_ref[...] = jnp.broadcast_to(loss_part, loss_ref.shape)

        @pl.when(pl.program_id(0) > 0)
        def _():
            dg_ref[...] += dg_part
            loss_ref[...] += jnp.broadcast_to(loss_part, loss_ref.shape)

    row = pl.BlockSpec((ROW_TILE, d), lambda i: (i, 0))
    vec = pl.BlockSpec((1, d), lambda i: (0, 0))
    tile = pl.BlockSpec((8, LANES), lambda i: (0, 0))
    return pl.pallas_call(
        body,
        name=name,
        grid=(s // ROW_TILE,),
        in_specs=[row, vec, row],
        out_specs=[tile, row, row, vec],
        out_shape=[jax.ShapeDtypeStruct((8, LANES), F32), jax.ShapeDtypeStruct((s, d), F32),
                   jax.ShapeDtypeStruct((s, d), BF16), jax.ShapeDtypeStruct((1, d), F32)],
        compiler_params=_cparams(("arbitrary",)),
    )(x, g, target)


_BRANCHES = ((0, D_CONV), (D_CONV, D_SGU), (D_CONV + D_SGU, D_SB))


def _combine_fwd(ya, yb, yc, g, *, name):
    s = ya.shape[0]

    def body(ya_ref, yb_ref, yc_ref, g_ref, y_ref):
        for ref, (off, w) in zip((ya_ref, yb_ref, yc_ref), _BRANCHES):
            v = ref[...]
            r = lax.rsqrt(jnp.mean(v * v, axis=-1, keepdims=True) + EPS)
            y_ref[:, off:off + w] = (v * r * g_ref[:, off:off + w]).astype(BF16)

    def row(w):
        return pl.BlockSpec((ROW_TILE, w), lambda i: (i, 0))

    return pl.pallas_call(
        body,
        name=name,
        grid=(s // ROW_TILE,),
        in_specs=[row(D_CONV), row(D_SGU), row(D_SB), pl.BlockSpec((1, D_MODEL), lambda i: (0, 0))],
        out_specs=row(D_MODEL),
        out_shape=jax.ShapeDtypeStruct((s, D_MODEL), BF16),
        compiler_params=_cparams(("parallel",)),
    )(ya, yb, yc, g)


def _combine_bwd(dy, ya, yb, yc, g, *, name):
    s = ya.shape[0]

    def body(dy_ref, ya_ref, yb_ref, yc_ref, g_ref, dya_ref, dyb_ref, dyc_ref, dg_ref):
        first = pl.program_id(0) == 0
        for ref, dref, (off, w) in zip((ya_ref, yb_ref, yc_ref), (dya_ref, dyb_ref, dyc_ref), _BRANCHES):
            v = ref[...]
            r = lax.rsqrt(jnp.mean(v * v, axis=-1, keepdims=True) + EPS)
            n = v * r
            dout = dy_ref[:, off:off + w]
            dn = dout * g_ref[:, off:off + w]
            dref[...] = r * (dn - n * jnp.mean(dn * n, axis=-1, keepdims=True))
            part = jnp.sum(dout * n, axis=0, keepdims=True)

            @pl.when(first)
            def _():
                dg_ref[:, off:off + w] = part

            @pl.when(jnp.logical_not(first))
            def _():
                dg_ref[:, off:off + w] += part

    def row(w):
        return pl.BlockSpec((ROW_TILE, w), lambda i: (i, 0))

    vec = pl.BlockSpec((1, D_MODEL), lambda i: (0, 0))
    return pl.pallas_call(
        body,
        name=name,
        grid=(s // ROW_TILE,),
        in_specs=[row(D_MODEL), row(D_CONV), row(D_SGU), row(D_SB), vec],
        out_specs=[row(D_CONV), row(D_SGU), row(D_SB), vec],
        out_shape=[jax.ShapeDtypeStruct((s, D_CONV), F32), jax.ShapeDtypeStruct((s, D_SGU), F32),
                   jax.ShapeDtypeStruct((s, D_SB), F32), jax.ShapeDtypeStruct((1, D_MODEL), F32)],
        compiler_params=_cparams(("arbitrary",)),
    )(dy, ya, yb, yc, g)


CONV_TILE = 128


def _shift_down(window, j, halo):
    return pltpu.roll(window, j, 0)[halo:, :] if j else window[halo:, :]


def _shift_up(window, j, n_out):
    n = window.shape[0]
    return pltpu.roll(window, n - j, 0)[:n_out, :] if j else window[:n_out, :]


def _mixer_a_fwd(p_ab, conv_w, conv_b, ln_g, ln_b, *, name):
    s = p_ab.shape[0]
    nt = s // CONV_TILE

    def body(p_ref, w_ref, b_ref, g_ref, beta_ref, y_ref, h_ref):
        h_ref[0:CONV_HALO, :] = jnp.zeros((CONV_HALO, D_CONV), F32)

        def glu(i, c):
            t0 = pl.multiple_of(i * CONV_TILE, CONV_TILE)
            a = p_ref[pl.ds(t0, CONV_TILE), 0:D_CONV]
            gate = p_ref[pl.ds(t0, CONV_TILE), D_CONV:2 * D_CONV]
            h_ref[pl.ds(t0 + CONV_HALO, CONV_TILE), :] = a * _sigmoid(gate)
            return c

        lax.fori_loop(0, nt, glu, 0)

        def conv(i, c):
            t0 = pl.multiple_of(i * CONV_TILE, CONV_TILE)
            window = h_ref[pl.ds(t0, CONV_TILE + CONV_HALO), :]
            acc = jnp.zeros((CONV_TILE, D_CONV), F32) + b_ref[...]
            for k in range(CONV_K):
                acc = acc + w_ref[k:k + 1, :] * _shift_down(window, CONV_K - 1 - k, CONV_HALO)
            mu = jnp.mean(acc, axis=-1, keepdims=True)
            xc = acc - mu
            rstd = lax.rsqrt(jnp.mean(xc * xc, axis=-1, keepdims=True) + EPS)
            z = xc * rstd * g_ref[...] + beta_ref[...]
            y_ref[pl.ds(t0, CONV_TILE), :] = z * _sigmoid(z)
            return c

        lax.fori_loop(0, nt, conv, 0)

    full = lambda shape: pl.BlockSpec(shape, lambda i: (0, 0))
    return pl.pallas_call(
        body,
        name=name,
        grid=(1,),
        in_specs=[full((s, 2 * D_CONV)), full((CONV_K, D_CONV)), full((1, D_CONV)), full((1, D_CONV)),
                  full((1, D_CONV))],
        out_specs=full((s, D_CONV)),
        out_shape=jax.ShapeDtypeStruct((s, D_CONV), F32),
        scratch_shapes=[pltpu.VMEM((s + CONV_HALO, D_CONV), F32)],
        compiler_params=_cparams(("arbitrary",)),
    )(p_ab, conv_w, conv_b, ln_g, ln_b)


def _mixer_a_bwd(p_ab, dya, conv_w, conv_b, ln_g, ln_b, *, name):
    s = p_ab.shape[0]
    nt = s // CONV_TILE

    def body(p_ref, dy_ref, w_ref, b_ref, g_ref, beta_ref, dp_ref, dw_ref, db_ref, dg_ref, dbeta_ref, h_ref, dc_ref):
        h_ref[0:CONV_HALO, :] = jnp.zeros((CONV_HALO, D_CONV), F32)
        dc_ref[s:s + CONV_HALO, :] = jnp.zeros((CONV_HALO, D_CONV), F32)
        dw_ref[...] = jnp.zeros_like(dw_ref)
        db_ref[...] = jnp.zeros_like(db_ref)
        dg_ref[...] = jnp.zeros_like(dg_ref)
        dbeta_ref[...] = jnp.zeros_like(dbeta_ref)

        def glu(i, c):
            t0 = pl.multiple_of(i * CONV_TILE, CONV_TILE)
            a = p_ref[pl.ds(t0, CONV_TILE), 0:D_CONV]
            gate = p_ref[pl.ds(t0, CONV_TILE), D_CONV:2 * D_CONV]
            h_ref[pl.ds(t0 + CONV_HALO, CONV_TILE), :] = a * _sigmoid(gate)
            return c

        lax.fori_loop(0, nt, glu, 0)

        def conv_bwd(i, c):
            t0 = pl.multiple_of(i * CONV_TILE, CONV_TILE)
            window = h_ref[pl.ds(t0, CONV_TILE + CONV_HALO), :]
            taps = [_shift_down(window, CONV_K - 1 - k, CONV_HALO) for k in range(CONV_K)]
            acc = jnp.zeros((CONV_TILE, D_CONV), F32) + b_ref[...]
            for k in range(CONV_K):
                acc = acc + w_ref[k:k + 1, :] * taps[k]
            mu = jnp.mean(acc, axis=-1, keepdims=True)
            xc = acc - mu
            rstd = lax.rsqrt(jnp.mean(xc * xc, axis=-1, keepdims=True) + EPS)
            xhat = xc * rstd
            z = xhat * g_ref[...] + beta_ref[...]
            sg = _sigmoid(z)
            dz = dy_ref[pl.ds(t0, CONV_TILE), :] * (sg * (1.0 + z * (1.0 - sg)))
            dg_ref[...] += jnp.sum(dz * xhat, axis=0, keepdims=True)
            dbeta_ref[...] += jnp.sum(dz, axis=0, keepdims=True)
            dxhat = dz * g_ref[...]
            dc = rstd * (dxhat - jnp.mean(dxhat, axis=-1, keepdims=True)
                         - xhat * jnp.mean(dxhat * xhat, axis=-1, keepdims=True))
            dc_ref[pl.ds(t0, CONV_TILE), :] = dc
            db_ref[...] += jnp.sum(dc, axis=0, keepdims=True)
            for k in range(CONV_K):
                dw_ref[k:k + 1, :] += jnp.sum(dc * taps[k], axis=0, keepdims=True)
            return c

        lax.fori_loop(0, nt, conv_bwd, 0)

        def glu_bwd(i, c):
            t0 = pl.multiple_of(i * CONV_TILE, CONV_TILE)
            window = dc_ref[pl.ds(t0, CONV_TILE + CONV_HALO), :]
            dh = jnp.zeros((CONV_TILE, D_CONV), F32)
            for j in range(CONV_K):
                dh = dh + w_ref[CONV_K - 1 - j:CONV_K - j, :] * _shift_up(window, j, CONV_TILE)
            a = p_ref[pl.ds(t0, CONV_TILE), 0:D_CONV]
            sg = _sigmoid(p_ref[pl.ds(t0, CONV_TILE), D_CONV:2 * D_CONV])
            dp_ref[pl.ds(t0, CONV_TILE), 0:D_CONV] = dh * sg
            dp_ref[pl.ds(t0, CONV_TILE), D_CONV:2 * D_CONV] = dh * a * sg * (1.0 - sg)
            return c

        lax.fori_loop(0, nt, glu_bwd, 0)

    full = lambda shape: pl.BlockSpec(shape, lambda i: (0, 0))
    vec = jax.ShapeDtypeStruct((1, D_CONV), F32)
    return pl.pallas_call(
        body,
        name=name,
        grid=(1,),
        in_specs=[full((s, 2 * D_CONV)), full((s, D_CONV)), full((CONV_K, D_CONV)), full((1, D_CONV)),
                  full((1, D_CONV)), full((1, D_CONV))],
        out_specs=[full((s, 2 * D_CONV)), full((CONV_K, D_CONV)), full((1, D_CONV)), full((1, D_CONV)),
                   full((1, D_CONV))],
        out_shape=[jax.ShapeDtypeStruct((s, 2 * D_CONV), F32), jax.ShapeDtypeStruct((CONV_K, D_CONV), F32),
                   vec, vec, vec],
        scratch_shapes=[pltpu.VMEM((s + CONV_HALO, D_CONV), F32), pltpu.VMEM((s + CONV_HALO, D_CONV), F32)],
        compiler_params=_cparams(("arbitrary",)),
    )(p_ab, dya, conv_w, conv_b, ln_g, ln_b)


N_SGU_HEADS = D_SGU // HEAD_DIM


def _head_masks(width):
    lane = lax.broadcasted_iota(jnp.int32, (1, width), 1)
    return [(lane >= h * HEAD_DIM) & (lane < (h + 1) * HEAD_DIM) for h in range(width // HEAD_DIM)]


def _tril_mask():
    r = lax.broadcasted_iota(jnp.int32, (CHUNK, CHUNK), 0)
    c = lax.broadcasted_iota(jnp.int32, (CHUNK, CHUNK), 1)
    return c <= r


def _sgu_norm(bv, g, beta):
    vg = _gelu(bv)
    mu = jnp.mean(vg, axis=-1, keepdims=True)
    xc = vg - mu
    rstd = lax.rsqrt(jnp.mean(xc * xc, axis=-1, keepdims=True) + EPS)
    xhat = xc * rstd
    return xhat, rstd, xhat * g + beta


def _sgu_fwd(p_ab, ln_g, ln_b, w_s, bias, *, name):
    s = p_ab.shape[0]

    def body(p_ref, g_ref, beta_ref, w_ref, bias_ref, y_ref):
        u = _gelu(p_ref[:, 0:D_SGU])
        _, _, vn = _sgu_norm(p_ref[:, D_SGU:2 * D_SGU], g_ref[...], beta_ref[...])
        vb = vn.astype(BF16)
        tril = _tril_mask()
        mixed = bias_ref[...]
        for h, m in enumerate(_head_masks(D_SGU)):
            wh = jnp.where(tril, w_ref[h], 0.0).astype(BF16)
            mixed = mixed + _dot(wh, jnp.where(m, vb, jnp.zeros_like(vb)), _NN)
        y_ref[...] = u * mixed

    return pl.pallas_call(
        body,
        name=name,
        grid=(s // CHUNK,),
        in_specs=[pl.BlockSpec((CHUNK, 2 * D_SGU), lambda i: (i, 1)),
                  pl.BlockSpec((1, D_SGU), lambda i: (0, 0)), pl.BlockSpec((1, D_SGU), lambda i: (0, 0)),
                  pl.BlockSpec((N_SGU_HEADS, CHUNK, CHUNK), lambda i: (0, 0, 0)),
                  pl.BlockSpec((CHUNK, D_SGU), lambda i: (0, 0))],
        out_specs=pl.BlockSpec((CHUNK, D_SGU), lambda i: (i, 0)),
        out_shape=jax.ShapeDtypeStruct((s, D_SGU), F32),
        compiler_params=_cparams(("parallel",)),
    )(p_ab, ln_g, ln_b, w_s, bias)


def _sgu_bwd(p_ab, dyb, ln_g, ln_b, w_s, bias, *, name):
    s = p_ab.shape[0]
    n_chunks = s // CHUNK

    def body(p_ref, dy_ref, g_ref, beta_ref, w_ref, bias_ref, dp_ref, dw_ref, db_ref, dg_ref, dbeta_ref, dbias_ref):
        @pl.when(pl.program_id(0) == 0)
        def _():
            dw_ref[...] = jnp.zeros_like(dw_ref)
            dbias_ref[...] = jnp.zeros_like(dbias_ref)
            dg_ref[...] = jnp.zeros_like(dg_ref)
            dbeta_ref[...] = jnp.zeros_like(dbeta_ref)

        bu = p_ref[:, 0:D_SGU]
        bv = p_ref[:, D_SGU:2 * D_SGU]
        u = _gelu(bu)
        gv = g_ref[...]
        xhat, rstd, vn = _sgu_norm(bv, gv, beta_ref[...])
        vb = vn.astype(BF16)
        tril = _tril_mask()
        masks = _head_masks(D_SGU)
        whs = [jnp.where(tril, w_ref[h], 0.0).astype(BF16) for h in range(N_SGU_HEADS)]
        mixed = bias_ref[...]
        for h, m in enumerate(masks):
            mixed = mixed + _dot(whs[h], jnp.where(m, vb, jnp.zeros_like(vb)), _NN)
        dy = dy_ref[...]
        dp_ref[:, 0:D_SGU] = dy * mixed * _gelu_grad(bu)
        dmixed = dy * u
        dbias_ref[...] += dmixed
        dmb = dmixed.astype(BF16)
        dvn = jnp.zeros((CHUNK, D_SGU), F32)
        for h, m in enumerate(masks):
            dmh = jnp.where(m, dmb, jnp.zeros_like(dmb))
            dvn = dvn + _dot(whs[h], dmh, _TN)
            dw_ref[h] += jnp.where(tril, _dot(dmh, vb, _NT), 0.0)
        dg_ref[...] += jnp.sum(dvn * xhat, axis=0, keepdims=True)
        dbeta_ref[...] += jnp.sum(dvn, axis=0, keepdims=True)
        dxhat = dvn * gv
        dvg = rstd * (dxhat - jnp.mean(dxhat, axis=-1, keepdims=True)
                      - xhat * jnp.mean(dxhat * xhat, axis=-1, keepdims=True))
        dp_ref[:, D_SGU:2 * D_SGU] = dvg * _gelu_grad(bv)

        @pl.when(pl.program_id(0) == n_chunks - 1)
        def _():
            chan = lax.broadcasted_iota(jnp.int32, (D_SGU, LANES), 0)
            head = lax.broadcasted_iota(jnp.int32, (D_SGU, LANES), 1)
            to_head = jnp.where(chan // HEAD_DIM == head, 1.0, 0.0).astype(BF16)
            db_ref[...] = _dot(_split_bf16(dbias_ref[...]), jnp.concatenate([to_head, to_head], axis=0), _NN)

    vec = pl.BlockSpec((1, D_SGU), lambda i: (0, 0))
    wspec = pl.BlockSpec((N_SGU_HEADS, CHUNK, CHUNK), lambda i: (0, 0, 0))
    bspec = pl.BlockSpec((CHUNK, D_SGU), lambda i: (0, 0))
    return pl.pallas_call(
        body,
        name=name,
        grid=(n_chunks,),
        in_specs=[pl.BlockSpec((CHUNK, 2 * D_SGU), lambda i: (i, 1)), pl.BlockSpec((CHUNK, D_SGU), lambda i: (i, 0)),
                  vec, vec, wspec, bspec],
        out_specs=[pl.BlockSpec((CHUNK, 2 * D_SGU), lambda i: (i, 0)), wspec,
                   pl.BlockSpec((CHUNK, LANES), lambda i: (0, 0)), vec, vec],
        out_shape=[jax.ShapeDtypeStruct((s, 2 * D_SGU), F32),
                   jax.ShapeDtypeStruct((N_SGU_HEADS, CHUNK, CHUNK), F32),
                   jax.ShapeDtypeStruct((CHUNK, LANES), F32),
                   jax.ShapeDtypeStruct((1, D_SGU), F32), jax.ShapeDtypeStruct((1, D_SGU), F32)],
        scratch_shapes=[pltpu.VMEM((CHUNK, D_SGU), F32)],
        compiler_params=_cparams(("arbitrary",)),
    )(p_ab, dyb, ln_g, ln_b, w_s, bias)


N_PAIRS = D_SB // LANES
SB_SCALE = HEAD_DIM ** -0.5


def _sb_logits(z, valid):
    nz = -z
    t = jnp.log(1.0 + jnp.exp(jnp.minimum(z, nz)))
    l1 = jnp.minimum(nz, 0.0) - t
    if valid is not None:
        l1 = jnp.where(valid, l1, 0.0)
    return l1, jnp.minimum(z, 0.0) - t


def _split_hi_lo(x):
    hi = lax.bitcast_convert_type(lax.bitcast_convert_type(x, jnp.uint32) & jnp.uint32(0xFFFF0000), F32)
    return jnp.concatenate([hi, x - hi], axis=1)


def _cumsum_operand(keep):
    half = jnp.concatenate([keep.astype(F32), jnp.ones((CHUNK, CHUNK), F32)], axis=1)
    return jnp.concatenate([half, half], axis=0)


def _attn_fwd(qkv, *, name, exchanges=None):
    s = qkv.shape[0]
    nq = s // CHUNK
    ex = exchanges or _Exchanges([])
    n_ex = len(ex.arrays)

    def body(*refs):
        q_ref, k_ref, v_ref = refs[:3]
        o_ref, t_ref = refs[3 + n_ex:5 + n_ex]
        ex_refs = (refs[3:3 + n_ex], refs[5 + n_ex:5 + 2 * n_ex]) + refs[5 + 2 * n_ex:]
        qi = pl.program_id(1)
        if n_ex:
            @pl.when((pl.program_id(0) == 0) & (qi == 0))
            def _():
                ex.start(*ex_refs)

        q = q_ref[...] * SB_SCALE
        zero = jnp.zeros_like(q)
        masks = _head_masks(LANES)
        qs = [jnp.where(m, q, zero) for m in masks]
        row = lax.broadcasted_iota(jnp.int32, (CHUNK, CHUNK), 0)
        col = lax.broadcasted_iota(jnp.int32, (CHUNK, CHUNK), 1)
        after_op = _cumsum_operand(row > col)

        cmr = col - row

        def blocks(js, carry):
            o, c0, c1 = carry
            kvs, valids = [], []
            for j in js:
                k0 = pl.multiple_of(jnp.maximum(j, 0) * CHUNK, CHUNK)
                kvs.append((k_ref[pl.ds(k0, CHUNK), :], v_ref[pl.ds(k0, CHUNK), :]))
                valids.append(cmr < jnp.where(j >= 0, (qi - j) * CHUNK, -CHUNK))
            units = [(h, b) for b in range(len(js)) for h in range(2)]
            zs = [_dot(qs[h], kvs[b][0], _NT) for h, b in units]
            logits = [_sb_logits(z, valids[b]) for z, (h, b) in zip(zs, units)]
            sums = [_dot(_split_hi_lo(l1), after_op, _NN) for l1, _ in logits]
            cs = [c0, c1]
            probs = []
            for (h, b), (_, lb), sm in zip(units, logits, sums):
                probs.append(jnp.where(valids[b], jnp.exp(lb + sm[:, :CHUNK] + cs[h]), 0.0))
                cs[h] = cs[h] + sm[:, CHUNK:]
            for (h, b), a in zip(units, probs):
                o = o + _dot(a.astype(BF16), jnp.where(masks[h], kvs[b][1], zero), _NN)
            return o, cs[0], cs[1]

        zc = jnp.zeros((CHUNK, LANES), F32)
        o, c0, c1 = lax.fori_loop(0, (qi + 2) // 2, lambda jj, c: blocks([qi - 2 * jj, qi - 2 * jj - 1], c), (zc,) * 3)
        o_ref[...] = o
        t_ref[:, 0:LANES] = c0
        t_ref[:, LANES:2 * LANES] = c1
        if n_ex:
            @pl.when((pl.program_id(0) == N_PAIRS - 1) & (qi == nq - 1))
            def _():
                ex.wait(*ex_refs)

    return pl.pallas_call(
        body,
        name=name,
        grid=(N_PAIRS, nq),
        in_specs=[pl.BlockSpec((CHUNK, LANES), lambda p, i: (i, p)),
                  pl.BlockSpec((s, LANES), lambda p, i: (0, N_PAIRS + p)),
                  pl.BlockSpec((s, LANES), lambda p, i: (0, 2 * N_PAIRS + p))] + ex.in_specs,
        out_specs=[pl.BlockSpec((CHUNK, LANES), lambda p, i: (i, p)),
                   pl.BlockSpec((CHUNK, 2 * LANES), lambda p, i: (i, p))] + ex.out_specs,
        out_shape=[jax.ShapeDtypeStruct((s, D_SB), F32), jax.ShapeDtypeStruct((s, 2 * D_SB), F32)] + ex.out_shapes,
        scratch_shapes=ex.scratch_shapes if n_ex else [],
        compiler_params=_cparams(("arbitrary", "arbitrary")),
    )(qkv, qkv, qkv, *ex.arrays)


def _attn_bwd(qkv, t_tot, do, *, name, exchanges=None):
    s = qkv.shape[0]
    nq = s // CHUNK
    ex = exchanges or _Exchanges([])
    n_ex = len(ex.arrays)

    def body(*refs):
        q_ref, k_ref, v_ref, t_ref, do_ref = refs[:5]
        dq_ref, dk_ref, dv_ref = refs[5 + n_ex:8 + n_ex]
        ex_refs = (refs[5:5 + n_ex], refs[8 + n_ex:8 + 2 * n_ex]) + refs[8 + 2 * n_ex:]
        qi = pl.program_id(1)
        if n_ex:
            @pl.when((pl.program_id(0) == 0) & (qi == 0))
            def _():
                ex.start(*ex_refs)

        @pl.when(qi == 0)
        def _():
            dk_ref[...] = jnp.zeros_like(dk_ref)
            dv_ref[...] = jnp.zeros_like(dv_ref)

        q = q_ref[...] * SB_SCALE
        dob = do_ref[...].astype(BF16)
        zero = jnp.zeros_like(q)
        masks = _head_masks(LANES)
        qs = [jnp.where(m, q, zero) for m in masks]
        dos = [jnp.where(m, dob, zero) for m in masks]
        tots = [t_ref[:, 0:LANES], t_ref[:, LANES:2 * LANES]]
        row = lax.broadcasted_iota(jnp.int32, (CHUNK, CHUNK), 0)
        col = lax.broadcasted_iota(jnp.int32, (CHUNK, CHUNK), 1)
        upto_op = _cumsum_operand(row <= col)
        before_op = _cumsum_operand(row < col)

        cmr = col - row

        def blocks(js, carry):
            dq, cl0, cl1, cp0, cp1 = carry
            starts = [pl.multiple_of(jnp.minimum(j, nq - 1) * CHUNK, CHUNK) for j in js]
            valids = [cmr < (qi - j) * CHUNK for j in js]
            kvs = [(k_ref[pl.ds(k0, CHUNK), :], v_ref[pl.ds(k0, CHUNK), :]) for k0 in starts]
            units = [(h, b) for b in range(len(js)) for h in range(2)]
            zs = [_dot(qs[h], kvs[b][0], _NT) for h, b in units]
            das = [_dot(dos[h], kvs[b][1], _NT) for h, b in units]
            logits = [_sb_logits(z, valids[b]) for z, (h, b) in zip(zs, units)]
            sums = [_dot(_split_hi_lo(l1), upto_op, _NN) for l1, _ in logits]
            cls, cps = [cl0, cl1], [cp0, cp1]
            probs, gs = [], []
            for (h, b), (_, lb), sm, da in zip(units, logits, sums, das):
                a = jnp.where(valids[b], jnp.exp(lb + (tots[h] - cls[h] - sm[:, :CHUNK])), 0.0)
                probs.append(a)
                gs.append(a * da)
                cls[h] = cls[h] + sm[:, CHUNK:]
            sums_g = [_dot(_split_hi_lo(g), before_op, _NN) for g in gs]
            dzs = []
            for (h, b), (_, lb), g, sg in zip(units, logits, gs, sums_g):
                dz = g - (g + sg[:, :CHUNK] + cps[h]) * jnp.exp(lb)
                dzs.append(jnp.where(valids[b], dz, 0.0).astype(BF16))
                cps[h] = cps[h] + sg[:, CHUNK:]
            for (h, b), dzb in zip(units, dzs):
                dq = dq + _dot(dzb, jnp.where(masks[h], kvs[b][0], zero), _NN)
            for b, k0 in enumerate(starts):
                dk_ref[pl.ds(k0, CHUNK), :] += _dot(dzs[2 * b], qs[0], _TN) + _dot(dzs[2 * b + 1], qs[1], _TN)
                dv_ref[pl.ds(k0, CHUNK), :] += (_dot(probs[2 * b].astype(BF16), dos[0], _TN)
                                                + _dot(probs[2 * b + 1].astype(BF16), dos[1], _TN))
            return dq, cls[0], cls[1], cps[0], cps[1]

        zc = jnp.zeros((CHUNK, LANES), F32)
        carry = lax.fori_loop(0, (qi + 2) // 2, lambda jj, c: blocks([2 * jj, 2 * jj + 1], c), (zc,) * 5)
        dq_ref[...] = carry[0] * SB_SCALE
        if n_ex:
            @pl.when((pl.program_id(0) == N_PAIRS - 1) & (qi == nq - 1))
            def _():
                ex.wait(*ex_refs)

    blk = pl.BlockSpec((CHUNK, LANES), lambda p, i: (i, p))
    col_blk = pl.BlockSpec((s, LANES), lambda p, i: (0, p))
    out = jax.ShapeDtypeStruct((s, D_SB), F32)
    return pl.pallas_call(
        body,
        name=name,
        grid=(N_PAIRS, nq),
        in_specs=[blk,
                  pl.BlockSpec((s, LANES), lambda p, i: (0, N_PAIRS + p)),
                  pl.BlockSpec((s, LANES), lambda p, i: (0, 2 * N_PAIRS + p)),
                  pl.BlockSpec((CHUNK, 2 * LANES), lambda p, i: (i, p)),
                  blk] + ex.in_specs,
        out_specs=[blk, col_blk, col_blk] + ex.out_specs,
        out_shape=[out, out, out] + ex.out_shapes,
        scratch_shapes=ex.scratch_shapes if n_ex else [],
        compiler_params=_cparams(("arbitrary", "arbitrary")),
    )(qkv, qkv, qkv, t_tot, do, *ex.arrays)


FFN_TILE = 256
FFN_COLS = 256
N_FF_BLOCKS = D_FF // FFN_COLS


def _ffn_act_fwd(up0, conv_w, conv_b, *, name):
    s = up0.shape[0]
    nt = s // FFN_TILE

    def body(xg_ref, xv_ref, wg_ref, wv_ref, bg_ref, bv_ref, act_ref, pg_ref, pv_ref):
        pg_ref[0:FFN_HALO, :] = jnp.zeros((FFN_HALO, FFN_COLS), F32)
        pv_ref[0:FFN_HALO, :] = jnp.zeros((FFN_HALO, FFN_COLS), F32)
        pg_ref[FFN_HALO:, :] = xg_ref[...]
        pv_ref[FFN_HALO:, :] = xv_ref[...]

        def tile(i, c):
            t0 = pl.multiple_of(i * FFN_TILE, FFN_TILE)
            outs = []
            for p_ref, w_ref, b_ref in ((pg_ref, wg_ref, bg_ref), (pv_ref, wv_ref, bv_ref)):
                window = p_ref[pl.ds(t0, FFN_TILE + FFN_HALO), :]
                acc = b_ref[...] + w_ref[2:3, :] * window[FFN_HALO:, :]
                for j in range(1, FFN_K):
                    acc = acc + w_ref[FFN_K - 1 - j:FFN_K - j, :] * _shift_down(window, j, FFN_HALO)
                outs.append(acc)
            gate, val = outs
            act_ref[pl.ds(t0, FFN_TILE), :] = (gate * _sigmoid(gate) * val).astype(BF16)
            return c

        lax.fori_loop(0, nt, tile, 0)

    gcol = lambda rows: pl.BlockSpec((rows, FFN_COLS), lambda j: (0, j))
    vcol = lambda rows: pl.BlockSpec((rows, FFN_COLS), lambda j: (0, j + N_FF_BLOCKS))
    return pl.pallas_call(
        body,
        name=name,
        grid=(N_FF_BLOCKS,),
        in_specs=[gcol(s), vcol(s), gcol(FFN_K), vcol(FFN_K), gcol(1), vcol(1)],
        out_specs=gcol(s),
        out_shape=jax.ShapeDtypeStruct((s, D_FF), BF16),
        scratch_shapes=[pltpu.VMEM((s + FFN_HALO, FFN_COLS), F32), pltpu.VMEM((s + FFN_HALO, FFN_COLS), F32)],
        compiler_params=_cparams(("parallel",)),
    )(up0, up0, conv_w, conv_w, conv_b, conv_b)


def _ffn_act_bwd(up0, dact, conv_w, conv_b, *, name):
    s = up0.shape[0]
    nt = s // FFN_TILE

    def body(xm_ref, xp_ref, da_ref, wm_ref, wp_ref, bm_ref, bp_ref, dx_ref, dw_ref, db_ref, pm_ref, pp_ref, dm_ref):
        is_gate = pl.program_id(0) < N_FF_BLOCKS
        zeros = jnp.zeros((FFN_HALO, FFN_COLS), F32)
        pm_ref[0:FFN_HALO, :] = zeros
        pp_ref[0:FFN_HALO, :] = zeros
        pm_ref[FFN_HALO:, :] = xm_ref[...]
        pp_ref[FFN_HALO:, :] = xp_ref[...]
        dm_ref[s:s + FFN_HALO, :] = zeros
        dw_ref[...] = jnp.zeros_like(dw_ref)
        db_ref[...] = jnp.zeros_like(db_ref)

        def tile(i, c):
            t0 = pl.multiple_of(i * FFN_TILE, FFN_TILE)
            wm = pm_ref[pl.ds(t0, FFN_TILE + FFN_HALO), :]
            wp = pp_ref[pl.ds(t0, FFN_TILE + FFN_HALO), :]
            taps = [_shift_down(wm, j, FFN_HALO) for j in range(FFN_K)]
            mine = bm_ref[...]
            partner = bp_ref[...]
            for j in range(FFN_K):
                mine = mine + wm_ref[FFN_K - 1 - j:FFN_K - j, :] * taps[j]
                partner = partner + wp_ref[FFN_K - 1 - j:FFN_K - j, :] * _shift_down(wp, j, FFN_HALO)
            da = da_ref[pl.ds(t0, FFN_TILE), :]
            sg_m = _sigmoid(mine)
            sg_p = _sigmoid(partner)
            d_as_gate = da * partner * (sg_m * (1.0 + mine * (1.0 - sg_m)))
            d_as_val = da * partner * sg_p
            dm = jnp.where(is_gate, d_as_gate, d_as_val)
            dm_ref[pl.ds(t0, FFN_TILE), :] = dm
            db_ref[...] += jnp.sum(dm, axis=0, keepdims=True)
            for j in range(FFN_K):
                dw_ref[FFN_K - 1 - j:FFN_K - j, :] += jnp.sum(dm * taps[j], axis=0, keepdims=True)
            return c

        lax.fori_loop(0, nt, tile, 0)

        def tile_dx(i, c):
            t0 = pl.multiple_of(i * FFN_TILE, FFN_TILE)
            window = dm_ref[pl.ds(t0, FFN_TILE + FFN_HALO), :]
            dx = jnp.zeros((FFN_TILE, FFN_COLS), F32)
            for j in range(FFN_K):
                dx = dx + wm_ref[FFN_K - 1 - j:FFN_K - j, :] * _shift_up(window, j, FFN_TILE)
            dx_ref[pl.ds(t0, FFN_TILE), :] = dx.astype(BF16)
            return c

        lax.fori_loop(0, nt, tile_dx, 0)

    nb = 2 * N_FF_BLOCKS
    mine = lambda rows: pl.BlockSpec((rows, FFN_COLS), lambda j: (0, j))
    partner = lambda rows: pl.BlockSpec((rows, FFN_COLS), lambda j: (0, (j + N_FF_BLOCKS) % nb))
    return pl.pallas_call(
        body,
        name=name,
        grid=(nb,),
        in_specs=[mine(s), partner(s), pl.BlockSpec((s, FFN_COLS), lambda j: (0, j % N_FF_BLOCKS)),
                  mine(FFN_K), partner(FFN_K), mine(1), partner(1)],
        out_specs=[mine(s), mine(FFN_K), mine(1)],
        out_shape=[jax.ShapeDtypeStruct((s, 2 * D_FF), BF16), jax.ShapeDtypeStruct((FFN_K, 2 * D_FF), F32),
                   jax.ShapeDtypeStruct((1, 2 * D_FF), F32)],
        scratch_shapes=[pltpu.VMEM((s + FFN_HALO, FFN_COLS), F32), pltpu.VMEM((s + FFN_HALO, FFN_COLS), F32),
                        pltpu.VMEM((s + FFN_HALO, FFN_COLS), F32)],
        compiler_params=_cparams(("parallel",)),
    )(up0, up0, dact, conv_w, conv_w, conv_b, conv_b)


MESH = pl.DeviceIdType.MESH


def _position():
    x, y, c = lax.axis_index("x"), lax.axis_index("y"), lax.axis_index("c")
    return x, y, c, 4 * x + 2 * y + c


def _peer(k):
    x, y, c, _ = _position()
    px = 1 - x if k & 4 else x
    py = 1 - y if k & 2 else y
    pc = 1 - c if k & 1 else c
    return (px, py, pc), 4 * px + 2 * py + pc


def _exchange(src, *, kind, name):
    ex = _Exchanges([(kind, src)])

    def body(src_ref, out_ref, send_sems, recv_sems, local_sems):
        ex.start([src_ref], [out_ref], send_sems, recv_sems, local_sems)
        ex.wait([src_ref], [out_ref], send_sems, recv_sems, local_sems)

    return pl.pallas_call(
        body,
        name=name,
        in_specs=ex.in_specs,
        out_specs=ex.out_specs[0],
        out_shape=ex.out_shapes[0],
        scratch_shapes=ex.scratch_shapes,
    )(src)


class _Exchanges:
    def __init__(self, items):
        self.kinds = [kind for kind, _ in items]
        self.arrays = [src for _, src in items]
        self.out_shapes = []
        self.block_rows = []
        for kind, src in items:
            scatter, by_rows = kind.startswith("scatter"), kind.endswith("rows")
            if by_rows:
                r = src.shape[0] // N_DEV if scatter else src.shape[0]
                shape = (N_DEV, r, src.shape[1]) if scatter else (N_DEV * r, src.shape[1])
            else:
                r = None
                shape = src.shape if scatter else (N_DEV,) + src.shape
            self.block_rows.append(r)
            self.out_shapes.append(jax.ShapeDtypeStruct(shape, src.dtype))
        n = len(items)
        self.in_specs = [pl.BlockSpec(memory_space=pl.ANY)] * n
        self.out_specs = [pl.BlockSpec(memory_space=pl.ANY)] * n
        self.scratch_shapes = [pltpu.SemaphoreType.DMA((n * (N_DEV - 1),)), pltpu.SemaphoreType.DMA((n * (N_DEV - 1),)),
                               pltpu.SemaphoreType.DMA((n,))]

    def _copies(self, i, src_ref, out_ref, send_sems, recv_sems, local_sems):
        kind, r = self.kinds[i], self.block_rows[i]
        scatter, by_rows = kind.startswith("scatter"), kind.endswith("rows")
        me = _position()[3]

        def rows(ref, idx):
            return ref.at[pl.ds(pl.multiple_of(idx * r, r), r), :]

        def outgoing(idx):
            if not scatter:
                return src_ref
            return rows(src_ref, idx) if by_rows else src_ref.at[idx]

        def slot(idx):
            return rows(out_ref, idx) if (by_rows and not scatter) else out_ref.at[idx]

        local = pltpu.make_async_copy(outgoing(me), slot(me), local_sems.at[i])
        sends, recvs = [], []
        for k in range(1, N_DEV):
            peer, pidx = _peer(k)
            sem = i * (N_DEV - 1) + k - 1
            sends.append(pltpu.make_async_remote_copy(src_ref=outgoing(pidx), dst_ref=slot(me), send_sem=send_sems.at[sem],
                                                      recv_sem=recv_sems.at[sem], device_id=peer, device_id_type=MESH))
            recvs.append(pltpu.make_async_remote_copy(src_ref=outgoing(pidx), dst_ref=slot(pidx), send_sem=send_sems.at[sem],
                                                      recv_sem=recv_sems.at[sem], device_id=peer, device_id_type=MESH))
        return local, sends, recvs

    def start(self, src_refs, out_refs, *sems):
        for i, (src_ref, out_ref) in enumerate(zip(src_refs, out_refs)):
            local, sends, _ = self._copies(i, src_ref, out_ref, *sems)
            local.start()
            for cp in sends:
                cp.start()

    def wait(self, src_refs, out_refs, *sems):
        for i, (src_ref, out_ref) in enumerate(zip(src_refs, out_refs)):
            local, sends, recvs = self._copies(i, src_ref, out_ref, *sems)
            for cp in recvs:
                cp.wait_recv()
            for cp in sends:
                cp.wait_send()
            local.wait()


def _row_tile(rows):
    return _tile(rows, (256, 128, 64, 32, 16, 8))


def _layer_parts_specs(n_layers, n_parts, tr, cols):
    return [pl.BlockSpec((n_parts, tr, cols), lambda l, i, j=j: (0, jnp.where(l == j, i, 0), 0)) for j in range(n_layers)]


def _select_layer_sum(p_refs):
    l = pl.program_id(0)
    g = None
    for j, p_ref in enumerate(p_refs):
        gj = p_ref[0].astype(F32)
        for k in range(1, p_ref.shape[0]):
            gj = gj + p_ref[k].astype(F32)
        g = gj if g is None else jnp.where(l == j, gj, g)
    return g


def _sum_parts(parts, *, name):
    n_layers = len(parts)
    n_parts, rows, cols = parts[0].shape
    tr = _row_tile(rows)

    def body(*refs):
        refs[-1][...] = _select_layer_sum(refs[:n_layers])

    return pl.pallas_call(
        body,
        name=name,
        grid=(n_layers, rows // tr),
        in_specs=_layer_parts_specs(n_layers, n_parts, tr, cols),
        out_specs=pl.BlockSpec((None, tr, cols), lambda l, i: (l, i, 0)),
        out_shape=jax.ShapeDtypeStruct((n_layers, rows, cols), F32),
        compiler_params=_cparams(("arbitrary", "arbitrary")),
    )(*parts)


def _adamw(parts, w, m, v, *, name):
    n_layers, rows, cols = w.shape
    summed = not isinstance(parts, (list, tuple))
    tr = _row_tile(rows)
    n_in = 1 if summed else n_layers

    def body(*refs):
        w_ref, m_ref, v_ref, g_ref, d_ref, m2_ref, v2_ref = refs[n_in:]
        g = refs[0][...] if summed else _select_layer_sum(refs[:n_in])
        m2 = ADAM_B1 * m_ref[...] + (1.0 - ADAM_B1) * g
        v2 = ADAM_B2 * v_ref[...] + (1.0 - ADAM_B2) * (g * g)
        m_hat = m2 / (1.0 - ADAM_B1 ** ADAM_STEP)
        v_hat = v2 / (1.0 - ADAM_B2 ** ADAM_STEP)
        g_ref[...] = g
        d_ref[...] = -ADAM_LR * (m_hat / (jnp.sqrt(v_hat) + ADAM_EPS) + ADAM_WD * w_ref[...])
        m2_ref[...] = m2
        v2_ref[...] = v2

    slab = pl.BlockSpec((None, tr, cols), lambda l, i: (l, i, 0))
    out = jax.ShapeDtypeStruct((n_layers, rows, cols), F32)
    p_specs = [slab] if summed else _layer_parts_specs(n_layers, parts[0].shape[0], tr, cols)
    return pl.pallas_call(
        body,
        name=name,
        grid=(n_layers, rows // tr),
        in_specs=p_specs + [slab, slab, slab],
        out_specs=[slab, slab, slab, slab],
        out_shape=[out, out, out, out],
        compiler_params=_cparams(("arbitrary", "arbitrary")),
    )(*((parts,) if summed else tuple(parts)), w, m, v)


SLAB_ROWS = 256
_SMALL_SHARDED = (("conv_w", (2, 31, 32)), ("ffn_conv_w", (2, 3, 704)))
_REPLICATED = (("g_mix", (2, 1024)), ("conv_b", (2, 256)), ("conv_ln_g", (2, 256)), ("conv_ln_b", (2, 256)),
               ("sgu_ln_g", (2, 256)), ("sgu_ln_b", (2, 256)), ("sgu_w", (2, 4, 128, 128)), ("sgu_b", (2, 4, 128)),
               ("g_out", (2, 1024)), ("g_ffn", (2, 1024)), ("ffn_conv_b", (2, 5632)), ("g_final", (1024,)))


def _seg_rows(n_elems):
    return -(-n_elems // LANES)


def _pack(arrays, lead=()):
    segs = []
    for a in arrays:
        flat = a.reshape(lead + (-1,)).astype(F32)
        pad = _seg_rows(flat.shape[-1]) * LANES - flat.shape[-1]
        if pad:
            flat = jnp.pad(flat, [(0, 0)] * len(lead) + [(0, pad)])
        segs.append(flat)
    flat = jnp.concatenate(segs, axis=-1)
    rows = flat.shape[-1] // LANES
    pad_rows = -rows % SLAB_ROWS
    if pad_rows:
        flat = jnp.pad(flat, [(0, 0)] * len(lead) + [(0, pad_rows * LANES)])
    return flat.reshape(lead + (rows + pad_rows, LANES))


def _unpack(slab, shapes, lead=()):
    flat = slab.reshape(lead + (-1,))
    out, off = [], 0
    for shape in shapes:
        n = math.prod(shape)
        out.append(flat[..., off:off + n].reshape(lead + tuple(shape)))
        off += _seg_rows(n) * LANES
    return out


def _split_last(full):
    split = full.shape[:-1] + (N_DEV, full.shape[-1] // N_DEV)
    return jnp.moveaxis(full.reshape(split), -2, 0)


def _join_last(blocks):
    moved = jnp.moveaxis(blocks, 0, -2)
    return moved.reshape(moved.shape[:-2] + (moved.shape[-2] * moved.shape[-1],))


def _layer_fwd(l, x, wt, small, gathers):
    tag = f"l{l}"
    h = _rmsnorm_fwd(x, small["g_mix"][l][None], name=f"{tag}_norm_mix")
    p_ab = _matmul(h, wt["w_in_t"][l], "nt", name=f"{tag}_proj_ab", n=D_AB)
    qkv = _matmul(h, wt["w_in_t"][l], "nt", name=f"{tag}_proj_qkv", n=D_QKV, b_n0=D_AB, out_dtype=BF16)
    ya = _mixer_a_fwd(p_ab, wt["conv_w"][l], small["conv_b"][l][None], small["conv_ln_g"][l][None],
                      small["conv_ln_b"][l][None], name=f"{tag}_mixer_a")
    bias = jnp.repeat(small["sgu_b"][l].T, HEAD_DIM, axis=1)
    yb = _sgu_fwd(p_ab, small["sgu_ln_g"][l][None], small["sgu_ln_b"][l][None], small["sgu_w"][l], bias,
                  name=f"{tag}_sgu")
    yc, t_tot, *gathered = _attn_fwd(qkv, name=f"{tag}_attn",
                                     exchanges=_Exchanges([("gather_rows", a) for _, _, a in gathers]))
    for (n, j, _), full in zip(gathers, gathered):
        wt[n][j] = full
    y = _combine_fwd(ya, yb, yc, small["g_out"][l][None], name=f"{tag}_combine")
    x1 = _matmul(y, wt["w_out"][l], "nn", name=f"{tag}_out_proj", residual=x)
    h2 = _rmsnorm_fwd(x1, small["g_ffn"][l][None], name=f"{tag}_norm_ffn")
    up0 = _matmul(h2, wt["w_up_t"][l], "nt", name=f"{tag}_up")
    act = _ffn_act_fwd(up0, wt["ffn_conv_w"][l], small["ffn_conv_b"][l][None], name=f"{tag}_ffn_act")
    x2 = _matmul(act, wt["w_down"][l], "nn", name=f"{tag}_down", residual=x1)
    saved = dict(x=x, h=h, p_ab=p_ab, qkv=qkv, ya=ya, yb=yb, yc=yc, t_tot=t_tot, y=y, x1=x1, h2=h2, up0=up0,
                 act=act, bias=bias)
    return x2, saved


def _layer_bwd(l, dres, sv, wt, small, pending, received):
    tag = f"l{l}b"
    big, g = {}, {}
    dx2, dx2_b = dres
    dact = _matmul(dx2_b, wt["w_down"][l], "nt", name=f"{tag}_dact")
    big["w_down"] = _matmul(sv["act"], dx2_b, "tn", name=f"{tag}_dw_down", out_dtype=BF16)
    dup0, g["ffn_conv_w"], dfb = _ffn_act_bwd(sv["up0"], dact, wt["ffn_conv_w"][l], small["ffn_conv_b"][l][None],
                                              name=f"{tag}_ffn_act")
    g["ffn_conv_b"] = dfb[0]
    dh2 = _matmul(dup0, wt["w_up_t"][l], "nn", name=f"{tag}_dh2")
    big["w_up_t"] = _matmul(dup0, sv["h2"], "tn", name=f"{tag}_dw_up", out_dtype=BF16)
    dx1, dx1_b, dg = _rmsnorm_bwd(sv["x1"], small["g_ffn"][l][None], dh2, dx2, name=f"{tag}_norm_ffn")
    g["g_ffn"] = dg[0]
    dy = _matmul(dx1_b, wt["w_out"][l], "nt", name=f"{tag}_dy")
    big["w_out"] = _matmul(sv["y"], dx1_b, "tn", name=f"{tag}_dw_out", out_dtype=BF16)
    dya, dyb, dyc, dg = _combine_bwd(dy, sv["ya"], sv["yb"], sv["yc"], small["g_out"][l][None], name=f"{tag}_combine")
    g["g_out"] = dg[0]
    scatters = pending + [("w_down", l, big["w_down"]), ("w_up_t", l, big["w_up_t"])]
    dq, dk, dv, *landed = _attn_bwd(sv["qkv"], sv["t_tot"], dyc, name=f"{tag}_attn",
                                    exchanges=_Exchanges([("scatter_rows", a) for _, _, a in scatters]))
    for (n, j, _), got in zip(scatters, landed):
        received[n][j] = got
    dp_b, g["sgu_w"], db, dg, dbeta = _sgu_bwd(sv["p_ab"], dyb, small["sgu_ln_g"][l][None], small["sgu_ln_b"][l][None],
                                               small["sgu_w"][l], sv["bias"], name=f"{tag}_sgu")
    g["sgu_b"] = db[:, :N_SGU_HEADS].T
    g["sgu_ln_g"], g["sgu_ln_b"] = dg[0], dbeta[0]
    dp_a, g["conv_w"], dcb, dg, dbeta = _mixer_a_bwd(sv["p_ab"], dya, wt["conv_w"][l], small["conv_b"][l][None],
                                                     small["conv_ln_g"][l][None], small["conv_ln_b"][l][None],
                                                     name=f"{tag}_mixer_a")
    g["conv_b"], g["conv_ln_g"], g["conv_ln_b"] = dcb[0], dg[0], dbeta[0]
    dp = jnp.concatenate([dp_a.astype(BF16), dp_b.astype(BF16), dq.astype(BF16), dk.astype(BF16), dv.astype(BF16)],
                         axis=1)
    dh = _matmul(dp, wt["w_in_t"][l], "nn", name=f"{tag}_dh")
    big["w_in_t"] = _matmul(dp, sv["h"], "tn", name=f"{tag}_dw_in", out_dtype=BF16)
    dx, dx_b, dg = _rmsnorm_bwd(sv["x"], small["g_mix"][l][None], dh, dx1, name=f"{tag}_norm_mix")
    g["g_mix"] = dg[0]
    return (dx, dx_b), [("w_out", l, big["w_out"]), ("w_in_t", l, big["w_in_t"])], g


_BIG = ("w_in_t", "w_out", "w_up_t", "w_down")


def kernel(x, g_mix, w_in, conv_w, conv_b, conv_ln_g, conv_ln_b, sgu_ln_g, sgu_ln_b, sgu_w, sgu_b, g_out, w_out, g_ffn, w_up, ffn_conv_w, ffn_conv_b, w_down, g_final, loss_target, m_g_mix, m_w_in, m_conv_w, m_conv_b, m_conv_ln_g, m_conv_ln_b, m_sgu_ln_g, m_sgu_ln_b, m_sgu_w, m_sgu_b, m_g_out, m_w_out, m_g_ffn, m_w_up, m_ffn_conv_w, m_ffn_conv_b, m_w_down, m_g_final, v_g_mix, v_w_in, v_conv_w, v_conv_b, v_conv_ln_g, v_conv_ln_b, v_sgu_ln_g, v_sgu_ln_b, v_sgu_w, v_sgu_b, v_g_out, v_w_out, v_g_ffn, v_w_up, v_ffn_conv_w, v_ffn_conv_b, v_w_down, v_g_final):
    given = dict(locals())
    n_layers = g_mix.shape[0]
    layers = range(n_layers)
    small_sharded = [n for n, _ in _SMALL_SHARDED]
    replicated = [n for n, _ in _REPLICATED]
    small = {n: given[n] for n in replicated}

    shard = {"w_in_t": [w_in[l].T.astype(BF16) for l in layers], "w_out": [w_out[l].astype(BF16) for l in layers],
             "w_up_t": [w_up[l].T.astype(BF16) for l in layers], "w_down": [w_down[l].astype(BF16) for l in layers]}
    wt = {n: [None] * n_layers for n in _BIG}
    wt["w_in_t"][0] = _exchange(shard["w_in_t"][0], kind="gather_rows", name="gather_w_in_t_l0")
    filters = _exchange(_pack([given[n] for n in small_sharded]), kind="gather_blocks", name="gather_filters")
    for n, blocks in zip(small_sharded, _unpack(filters, [s for _, s in _SMALL_SHARDED], lead=(N_DEV,))):
        wt[n] = _join_last(blocks)

    xs = x[0]
    saved = []
    for l in layers:
        gathers = [(n, l, shard[n][l]) for n in ("w_out", "w_up_t", "w_down")]
        if l + 1 < n_layers:
            gathers.append(("w_in_t", l + 1, shard["w_in_t"][l + 1]))
        xs, sv = _layer_fwd(l, xs, wt, small, gathers)
        saved.append(sv)
    loss_tile, dx, dx_b, dgf = _loss_head(xs, g_final[None], loss_target[0], name="loss_head")
    dres = (dx, dx_b)
    received = {n: [None] * n_layers for n in _BIG}
    grads = [None] * n_layers
    pending = []
    for l in reversed(layers):
        dres, pending, grads[l] = _layer_bwd(l, dres, saved[l], wt, small, pending, received)
    for n, l, part in pending:
        received[n][l] = _exchange(part, kind="scatter_rows", name=f"scatter_{n}_l{l}")
    partial = {n: jnp.stack([g[n] for g in grads]) for n in grads[0]}
    partial["g_final"] = dgf[0]

    out = {}

    def update(n, parts):
        results = _adamw(parts, given[n], given["m_" + n], given["v_" + n], name=f"adamw_{n}")
        for pre, res in zip(("grad_", "delta_", "new_m_", "new_v_"), results):
            out[pre + n] = res

    update("w_out", received["w_out"])
    update("w_down", received["w_down"])
    for n in ("w_in", "w_up"):
        update(n, jnp.swapaxes(_sum_parts(received[n + "_t"], name=f"sum_{n}"), 1, 2))

    own = _pack([_split_last(partial[n]) for n in small_sharded], lead=(N_DEV,))
    shared = _pack([partial[n] for n in replicated])
    slab = jnp.concatenate([own, jnp.broadcast_to(shared[None], (N_DEV,) + shared.shape)], axis=1)
    slab = _exchange(slab, kind="scatter_blocks", name="scatter_small_grads")
    stacks = [jnp.concatenate([_pack([given[pre + n] for n in small_sharded]),
                               _pack([given[pre + n] for n in replicated])])[None] for pre in ("", "m_", "v_")]
    results = _adamw([slab], *stacks, name="adamw_small")
    n_own = own.shape[1]
    for pre, res in zip(("grad_", "delta_", "new_m_", "new_v_"), results):
        unpacked = (_unpack(res[0, :n_own], [s for _, s in _SMALL_SHARDED])
                    + _unpack(res[0, n_own:], [s for _, s in _REPLICATED]))
        for n, a in zip(small_sharded + replicated, unpacked):
            out[pre + n] = a

    loss = lax.psum(loss_tile[0, 0], ("x", "y", "c"))
    order = list(_WEIGHT_ORDER)
    return (loss, dres[0][None], *[out["grad_" + n] for n in order], *[out["delta_" + n] for n in order],
            *[out["new_m_" + n] for n in order], *[out["new_v_" + n] for n in order])


_WEIGHT_ORDER = ("g_mix", "w_in", "conv_w", "conv_b", "conv_ln_g", "conv_ln_b", "sgu_ln_g", "sgu_ln_b", "sgu_w", "sgu_b",
                 "g_out", "w_out", "g_ffn", "w_up", "ffn_conv_w", "ffn_conv_b", "w_down", "g_final")
```

```python
import functools
import math

import jax
import jax.numpy as jnp
from jax import lax
from jax.experimental import pallas as pl
from jax.experimental.pallas import tpu as pltpu

F32 = jnp.float32
BF16 = jnp.bfloat16

N_DEV = 8
D_MODEL = 1024
HEAD_DIM = 64
D_CONV = 256
D_SGU = 256
D_SB = 512
D_AB = 2 * D_CONV + 2 * D_SGU
D_QKV = 3 * D_SB
D_IN = D_AB + D_QKV
CONV_K = 31
CONV_HALO = 32
FFN_K = 3
FFN_HALO = 8
D_FF = 2816
CHUNK = 128
EPS = 1e-6
LANES = 128

ADAM_LR = 0.001
ADAM_B1 = 0.9
ADAM_B2 = 0.999
ADAM_EPS = 1e-08
ADAM_WD = 0.01
ADAM_STEP = 10

VMEM_LIMIT = 56 * 1024 * 1024


def _cparams(sem=None):
    return pltpu.CompilerParams(dimension_semantics=sem, vmem_limit_bytes=VMEM_LIMIT)


def _tile(n, prefs=(512, 256, 128)):
    for t in prefs:
        if n % t == 0:
            return t
    return n


def _sigmoid(x):
    return 1.0 / (1.0 + jnp.exp(-x))


def _softplus(x):
    return jnp.maximum(x, 0.0) + jnp.log1p(jnp.exp(-jnp.abs(x)))


_INV_SQRT2 = 1.0 / math.sqrt(2.0)
_INV_SQRT2PI = 1.0 / math.sqrt(2.0 * math.pi)


def _gelu(x):
    return 0.5 * x * (1.0 + lax.erf(x * _INV_SQRT2))


def _gelu_grad(x):
    return 0.5 * (1.0 + lax.erf(x * _INV_SQRT2)) + x * jnp.exp(-0.5 * x * x) * _INV_SQRT2PI


def _dot(a, b, dims):
    return lax.dot_general(a, b, (dims, ((), ())), preferred_element_type=F32)


_NN = ((1,), (0,))
_NT = ((1,), (1,))
_TN = ((0,), (0,))


def _split_bf16(x):
    hi = x.astype(BF16)
    lo = (x - hi.astype(F32)).astype(BF16)
    return jnp.concatenate([hi, lo], axis=1)


def _matmul(a, b, mode, *, name, out_dtype=F32, residual=None, n=None, b_n0=0):
    if mode == "nn":
        (m, k), n = a.shape, (n or b.shape[1])
    elif mode == "nt":
        (m, k), n = a.shape, (n or b.shape[0])
    else:
        (k, m), n = a.shape, b.shape[1]
    has_res = residual is not None
    tm, tn = _matmul_tiles(m, n, k, a.dtype.itemsize, b.dtype.itemsize, jnp.dtype(out_dtype).itemsize, has_res, b_n0)
    j0 = b_n0 // tn

    if mode == "nn":
        a_spec = pl.BlockSpec((tm, k), lambda i, j: (i, 0))
        b_spec = pl.BlockSpec((k, tn), lambda i, j: (0, j + j0))
        dims = _NN
    elif mode == "nt":
        a_spec = pl.BlockSpec((tm, k), lambda i, j: (i, 0))
        b_spec = pl.BlockSpec((tn, k), lambda i, j: (j + j0, 0))
        dims = _NT
    else:
        a_spec = pl.BlockSpec((k, tm), lambda i, j: (0, i))
        b_spec = pl.BlockSpec((k, tn), lambda i, j: (0, j))
        dims = _TN
    o_spec = pl.BlockSpec((tm, tn), lambda i, j: (i, j))

    def body(*refs):
        a_ref, b_ref = refs[:2]
        acc = _dot(a_ref[...].astype(BF16), b_ref[...].astype(BF16), dims)
        if has_res:
            acc = acc + refs[2][...]
        refs[-1][...] = acc.astype(out_dtype)

    in_specs = [a_spec, b_spec] + ([o_spec] if has_res else [])
    args = (a, b) + ((residual,) if has_res else ())
    return pl.pallas_call(
        body,
        name=name,
        grid=(m // tm, n // tn),
        in_specs=in_specs,
        out_specs=o_spec,
        out_shape=jax.ShapeDtypeStruct((m, n), out_dtype),
        compiler_params=_cparams(("parallel", "parallel")),
    )(*args)


MATMUL_VMEM_BUDGET = 40 * 1024 * 1024


def _matmul_tiles(m, n, k, a_bytes, b_bytes, out_bytes, has_res, n_offset):
    def divisors(size, cap, also=0):
        return [t for t in range(cap, 0, -LANES) if size % t == 0 and also % t == 0] or [size]

    for tm in divisors(m, 1024):
        for tn in divisors(n, 1408, n_offset):
            blocks = tm * k * a_bytes + k * tn * b_bytes + tm * tn * (out_bytes + (4 if has_res else 0))
            if 2 * blocks <= MATMUL_VMEM_BUDGET:
                return tm, tn
    raise ValueError(f"no matmul tiling for {m} x {n} x {k}")


ROW_TILE = 256


def _rmsnorm_fwd(x, g, *, name):
    s, d = x.shape

    def body(x_ref, g_ref, h_ref):
        xv = x_ref[...]
        r = lax.rsqrt(jnp.mean(xv * xv, axis=-1, keepdims=True) + EPS)
        h_ref[...] = (xv * r * g_ref[...]).astype(BF16)

    return pl.pallas_call(
        body,
        name=name,
        grid=(s // ROW_TILE,),
        in_specs=[pl.BlockSpec((ROW_TILE, d), lambda i: (i, 0)), pl.BlockSpec((1, d), lambda i: (0, 0))],
        out_specs=pl.BlockSpec((ROW_TILE, d), lambda i: (i, 0)),
        out_shape=jax.ShapeDtypeStruct((s, d), BF16),
        compiler_params=_cparams(("parallel",)),
    )(x, g)


def _rmsnorm_bwd(x, g, dh, dres, *, name):
    s, d = x.shape

    def body(x_ref, g_ref, dh_ref, dres_ref, dx_ref, dxb_ref, dg_ref):
        xv = x_ref[...]
        r = lax.rsqrt(jnp.mean(xv * xv, axis=-1, keepdims=True) + EPS)
        xhat = xv * r
        dhv = dh_ref[...]
        dxhat = dhv * g_ref[...]
        dx = dres_ref[...] + r * (dxhat - xhat * jnp.mean(dxhat * xhat, axis=-1, keepdims=True))
        dx_ref[...] = dx
        dxb_ref[...] = dx.astype(BF16)
        part = jnp.sum(dhv * xhat, axis=0, keepdims=True)

        @pl.when(pl.program_id(0) == 0)
        def _():
            dg_ref[...] = part

        @pl.when(pl.program_id(0) > 0)
        def _():
            dg_ref[...] += part

    row = pl.BlockSpec((ROW_TILE, d), lambda i: (i, 0))
    vec = pl.BlockSpec((1, d), lambda i: (0, 0))
    return pl.pallas_call(
        body,
        name=name,
        grid=(s // ROW_TILE,),
        in_specs=[row, vec, row, row],
        out_specs=[row, row, vec],
        out_shape=[jax.ShapeDtypeStruct((s, d), F32), jax.ShapeDtypeStruct((s, d), BF16),
                   jax.ShapeDtypeStruct((1, d), F32)],
        compiler_params=_cparams(("arbitrary",)),
    )(x, g, dh, dres)


def _loss_head(x, g, target, *, name):
    s, d = x.shape

    def body(x_ref, g_ref, t_ref, loss_ref, dx_ref, dxb_ref, dg_ref):
        xv = x_ref[...]
        gv = g_ref[...]
        r = lax.rsqrt(jnp.mean(xv * xv, axis=-1, keepdims=True) + EPS)
        xhat = xv * r
        diff = xhat * gv - t_ref[...]
        dy = diff * (1.0 / d)
        dxhat = dy * gv
        dx = r * (dxhat - xhat * jnp.mean(dxhat * xhat, axis=-1, keepdims=True))
        dx_ref[...] = dx
        dxb_ref[...] = dx.astype(BF16)
        dg_part = jnp.sum(dy * xhat, axis=0, keepdims=True)
        row_loss = jnp.sum(diff * diff, axis=-1, keepdims=True)
        loss_part = jnp.sum(row_loss, axis=0, keepdims=True) * (0.5 / d)

        @pl.when(pl.program_id(0) == 0)
        def _():
            dg_ref[...] = dg_part
            loss_ref[...] = jnp.broadcast_to(loss_part, loss_ref.shape)

        @pl.when(pl.program_id(0) > 0)
        def _():
            dg_ref[...] += dg_part
            loss_ref[...] += jnp.broadcast_to(loss_part, loss_ref.shape)

    row = pl.BlockSpec((ROW_TILE, d), lambda i: (i, 0))
    vec = pl.BlockSpec((1, d), lambda i: (0, 0))
    tile = pl.BlockSpec((8, LANES), lambda i: (0, 0))
    return pl.pallas_call(
        body,
        name=name,
        grid=(s // ROW_TILE,),
        in_specs=[row, vec, row],
        out_specs=[tile, row, row, vec],
        out_shape=[jax.ShapeDtypeStruct((8, LANES), F32), jax.ShapeDtypeStruct((s, d), F32),
                   jax.ShapeDtypeStruct((s, d), BF16), jax.ShapeDtypeStruct((1, d), F32)],
        compiler_params=_cparams(("arbitrary",)),
    )(x, g, target)


_BRANCHES = ((0, D_CONV), (D_CONV, D_SGU), (D_CONV + D_SGU, D_SB))


def _combine_fwd(ya, yb, yc, g, *, name):
    s = ya.shape[0]

    def body(ya_ref, yb_ref, yc_ref, g_ref, y_ref):
        for ref, (off, w) in zip((ya_ref, yb_ref, yc_ref), _BRANCHES):
            v = ref[...]
            r = lax.rsqrt(jnp.mean(v * v, axis=-1, keepdims=True) + EPS)
            y_ref[:, off:off + w] = (v * r * g_ref[:, off:off + w]).astype(BF16)

    def row(w):
        return pl.BlockSpec((ROW_TILE, w), lambda i: (i, 0))

    return pl.pallas_call(
        body,
        name=name,
        grid=(s // ROW_TILE,),
        in_specs=[row(D_CONV), row(D_SGU), row(D_SB), pl.BlockSpec((1, D_MODEL), lambda i: (0, 0))],
        out_specs=row(D_MODEL),
        out_shape=jax.ShapeDtypeStruct((s, D_MODEL), BF16),
        compiler_params=_cparams(("parallel",)),
    )(ya, yb, yc, g)


def _combine_bwd(dy, ya, yb, yc, g, *, name):
    s = ya.shape[0]

    def body(dy_ref, ya_ref, yb_ref, yc_ref, g_ref, dya_ref, dyb_ref, dyc_ref, dg_ref):
        first = pl.program_id(0) == 0
        for ref, dref, (off, w) in zip((ya_ref, yb_ref, yc_ref), (dya_ref, dyb_ref, dyc_ref), _BRANCHES):
            v = ref[...]
            r = lax.rsqrt(jnp.mean(v * v, axis=-1, keepdims=True) + EPS)
            n = v * r
            dout = dy_ref[:, off:off + w]
            dn = dout * g_ref[:, off:off + w]
            dref[...] = r * (dn - n * jnp.mean(dn * n, axis=-1, keepdims=True))
            part = jnp.sum(dout * n, axis=0, keepdims=True)

            @pl.when(first)
            def _():
                dg_ref[:, off:off + w] = part

            @pl.when(jnp.logical_not(first))
            def _():
                dg_ref[:, off:off + w] += part

    def row(w):
        return pl.BlockSpec((ROW_TILE, w), lambda i: (i, 0))

    vec = pl.BlockSpec((1, D_MODEL), lambda i: (0, 0))
    return pl.pallas_call(
        body,
        name=name,
        grid=(s // ROW_TILE,),
        in_specs=[row(D_MODEL), row(D_CONV), row(D_SGU), row(D_SB), vec],
        out_specs=[row(D_CONV), row(D_SGU), row(D_SB), vec],
        out_shape=[jax.ShapeDtypeStruct((s, D_CONV), F32), jax.ShapeDtypeStruct((s, D_SGU), F32),
                   jax.ShapeDtypeStruct((s, D_SB), F32), jax.ShapeDtypeStruct((1, D_MODEL), F32)],
        compiler_params=_cparams(("arbitrary",)),
    )(dy, ya, yb, yc, g)


CONV_TILE = 128


def _shift_down(window, j, halo):
    return pltpu.roll(window, j, 0)[halo:, :] if j else window[halo:, :]


def _shift_up(window, j, n_out):
    n = window.shape[0]
    return pltpu.roll(window, n - j, 0)[:n_out, :] if j else window[:n_out, :]


def _mixer_a_fwd(p_ab, conv_w, conv_b, ln_g, ln_b, *, name):
    s = p_ab.shape[0]
    nt = s // CONV_TILE

    def body(p_ref, w_ref, b_ref, g_ref, beta_ref, y_ref, h_ref):
        h_ref[0:CONV_HALO, :] = jnp.zeros((CONV_HALO, D_CONV), F32)

        def glu(i, c):
            t0 = pl.multiple_of(i * CONV_TILE, CONV_TILE)
            a = p_ref[pl.ds(t0, CONV_TILE), 0:D_CONV]
            gate = p_ref[pl.ds(t0, CONV_TILE), D_CONV:2 * D_CONV]
            h_ref[pl.ds(t0 + CONV_HALO, CONV_TILE), :] = a * _sigmoid(gate)
            return c

        lax.fori_loop(0, nt, glu, 0)

        def conv(i, c):
            t0 = pl.multiple_of(i * CONV_TILE, CONV_TILE)
            window = h_ref[pl.ds(t0, CONV_TILE + CONV_HALO), :]
            acc = jnp.zeros((CONV_TILE, D_CONV), F32) + b_ref[...]
            for k in range(CONV_K):
                acc = acc + w_ref[k:k + 1, :] * _shift_down(window, CONV_K - 1 - k, CONV_HALO)
            mu = jnp.mean(acc, axis=-1, keepdims=True)
            xc = acc - mu
            rstd = lax.rsqrt(jnp.mean(xc * xc, axis=-1, keepdims=True) + EPS)
            z = xc * rstd * g_ref[...] + beta_ref[...]
            y_ref[pl.ds(t0, CONV_TILE), :] = z * _sigmoid(z)
            return c

        lax.fori_loop(0, nt, conv, 0)

    full = lambda shape: pl.BlockSpec(shape, lambda i: (0, 0))
    return pl.pallas_call(
        body,
        name=name,
        grid=(1,),
        in_specs=[full((s, 2 * D_CONV)), full((CONV_K, D_CONV)), full((1, D_CONV)), full((1, D_CONV)),
                  full((1, D_CONV))],
        out_specs=full((s, D_CONV)),
        out_shape=jax.ShapeDtypeStruct((s, D_CONV), F32),
        scratch_shapes=[pltpu.VMEM((s + CONV_HALO, D_CONV), F32)],
        compiler_params=_cparams(("arbitrary",)),
    )(p_ab, conv_w, conv_b, ln_g, ln_b)


def _mixer_a_bwd(p_ab, dya, conv_w, conv_b, ln_g, ln_b, *, name):
    s = p_ab.shape[0]
    nt = s // CONV_TILE

    def body(p_ref, dy_ref, w_ref, b_ref, g_ref, beta_ref, dp_ref, dw_ref, db_ref, dg_ref, dbeta_ref, h_ref, dc_ref):
        h_ref[0:CONV_HALO, :] = jnp.zeros((CONV_HALO, D_CONV), F32)
        dc_ref[s:s + CONV_HALO, :] = jnp.zeros((CONV_HALO, D_CONV), F32)
        dw_ref[...] = jnp.zeros_like(dw_ref)
        db_ref[...] = jnp.zeros_like(db_ref)
        dg_ref[...] = jnp.zeros_like(dg_ref)
        dbeta_ref[...] = jnp.zeros_like(dbeta_ref)

        def glu(i, c):
            t0 = pl.multiple_of(i * CONV_TILE, CONV_TILE)
            a = p_ref[pl.ds(t0, CONV_TILE), 0:D_CONV]
            gate = p_ref[pl.ds(t0, CONV_TILE), D_CONV:2 * D_CONV]
            h_ref[pl.ds(t0 + CONV_HALO, CONV_TILE), :] = a * _sigmoid(gate)
            return c

        lax.fori_loop(0, nt, glu, 0)

        def conv_bwd(i, c):
            t0 = pl.multiple_of(i * CONV_TILE, CONV_TILE)
            window = h_ref[pl.ds(t0, CONV_TILE + CONV_HALO), :]
            taps = [_shift_down(window, CONV_K - 1 - k, CONV_HALO) for k in range(CONV_K)]
            acc = jnp.zeros((CONV_TILE, D_CONV), F32) + b_ref[...]
            for k in range(CONV_K):
                acc = acc + w_ref[k:k + 1, :] * taps[k]
            mu = jnp.mean(acc, axis=-1, keepdims=True)
            xc = acc - mu
            rstd = lax.rsqrt(jnp.mean(xc * xc, axis=-1, keepdims=True) + EPS)
            xhat = xc * rstd
            z = xhat * g_ref[...] + beta_ref[...]
            sg = _sigmoid(z)
            dz = dy_ref[pl.ds(t0, CONV_TILE), :] * (sg * (1.0 + z * (1.0 - sg)))
            dg_ref[...] += jnp.sum(dz * xhat, axis=0, keepdims=True)
            dbeta_ref[...] += jnp.sum(dz, axis=0, keepdims=True)
            dxhat = dz * g_ref[...]
            dc = rstd * (dxhat - jnp.mean(dxhat, axis=-1, keepdims=True)
                         - xhat * jnp.mean(dxhat * xhat, axis=-1, keepdims=True))
            dc_ref[pl.ds(t0, CONV_TILE), :] = dc
            db_ref[...] += jnp.sum(dc, axis=0, keepdims=True)
            for k in range(CONV_K):
                dw_ref[k:k + 1, :] += jnp.sum(dc * taps[k], axis=0, keepdims=True)
            return c

        lax.fori_loop(0, nt, conv_bwd, 0)

        def glu_bwd(i, c):
            t0 = pl.multiple_of(i * CONV_TILE, CONV_TILE)
            window = dc_ref[pl.ds(t0, CONV_TILE + CONV_HALO), :]
            dh = jnp.zeros((CONV_TILE, D_CONV), F32)
            for j in range(CONV_K):
                dh = dh + w_ref[CONV_K - 1 - j:CONV_K - j, :] * _shift_up(window, j, CONV_TILE)
            a = p_ref[pl.ds(t0, CONV_TILE), 0:D_CONV]
            sg = _sigmoid(p_ref[pl.ds(t0, CONV_TILE), D_CONV:2 * D_CONV])
            dp_ref[pl.ds(t0, CONV_TILE), 0:D_CONV] = dh * sg
            dp_ref[pl.ds(t0, CONV_TILE), D_CONV:2 * D_CONV] = dh * a * sg * (1.0 - sg)
            return c

        lax.fori_loop(0, nt, glu_bwd, 0)

    full = lambda shape: pl.BlockSpec(shape, lambda i: (0, 0))
    vec = jax.ShapeDtypeStruct((1, D_CONV), F32)
    return pl.pallas_call(
        body,
        name=name,
        grid=(1,),
        in_specs=[full((s, 2 * D_CONV)), full((s, D_CONV)), full((CONV_K, D_CONV)), full((1, D_CONV)),
                  full((1, D_CONV)), full((1, D_CONV))],
        out_specs=[full((s, 2 * D_CONV)), full((CONV_K, D_CONV)), full((1, D_CONV)), full((1, D_CONV)),
                   full((1, D_CONV))],
        out_shape=[jax.ShapeDtypeStruct((s, 2 * D_CONV), F32), jax.ShapeDtypeStruct((CONV_K, D_CONV), F32),
                   vec, vec, vec],
        scratch_shapes=[pltpu.VMEM((s + CONV_HALO, D_CONV), F32), pltpu.VMEM((s + CONV_HALO, D_CONV), F32)],
        compiler_params=_cparams(("arbitrary",)),
    )(p_ab, dya, conv_w, conv_b, ln_g, ln_b)


N_SGU_HEADS = D_SGU // HEAD_DIM


def _head_masks(width):
    lane = lax.broadcasted_iota(jnp.int32, (1, width), 1)
    return [(lane >= h * HEAD_DIM) & (lane < (h + 1) * HEAD_DIM) for h in range(width // HEAD_DIM)]


def _tril_mask():
    r = lax.broadcasted_iota(jnp.int32, (CHUNK, CHUNK), 0)
    c = lax.broadcasted_iota(jnp.int32, (CHUNK, CHUNK), 1)
    return c <= r


def _sgu_norm(bv, g, beta):
    vg = _gelu(bv)
    mu = jnp.mean(vg, axis=-1, keepdims=True)
    xc = vg - mu
    rstd = lax.rsqrt(jnp.mean(xc * xc, axis=-1, keepdims=True) + EPS)
    xhat = xc * rstd
    return xhat, rstd, xhat * g + beta


def _sgu_fwd(p_ab, ln_g, ln_b, w_s, bias, *, name):
    s = p_ab.shape[0]

    def body(p_ref, g_ref, beta_ref, w_ref, bias_ref, y_ref):
        u = _gelu(p_ref[:, 0:D_SGU])
        _, _, vn = _sgu_norm(p_ref[:, D_SGU:2 * D_SGU], g_ref[...], beta_ref[...])
        vb = vn.astype(BF16)
        tril = _tril_mask()
        mixed = bias_ref[...]
        for h, m in enumerate(_head_masks(D_SGU)):
            wh = jnp.where(tril, w_ref[h], 0.0).astype(BF16)
            mixed = mixed + _dot(wh, jnp.where(m, vb, jnp.zeros_like(vb)), _NN)
        y_ref[...] = u * mixed

    return pl.pallas_call(
        body,
        name=name,
        grid=(s // CHUNK,),
        in_specs=[pl.BlockSpec((CHUNK, 2 * D_SGU), lambda i: (i, 1)),
                  pl.BlockSpec((1, D_SGU), lambda i: (0, 0)), pl.BlockSpec((1, D_SGU), lambda i: (0, 0)),
                  pl.BlockSpec((N_SGU_HEADS, CHUNK, CHUNK), lambda i: (0, 0, 0)),
                  pl.BlockSpec((CHUNK, D_SGU), lambda i: (0, 0))],
        out_specs=pl.BlockSpec((CHUNK, D_SGU), lambda i: (i, 0)),
        out_shape=jax.ShapeDtypeStruct((s, D_SGU), F32),
        compiler_params=_cparams(("parallel",)),
    )(p_ab, ln_g, ln_b, w_s, bias)


def _sgu_bwd(p_ab, dyb, ln_g, ln_b, w_s, bias, *, name):
    s = p_ab.shape[0]
    n_chunks = s // CHUNK

    def body(p_ref, dy_ref, g_ref, beta_ref, w_ref, bias_ref, dp_ref, dw_ref, db_ref, dg_ref, dbeta_ref, dbias_ref):
        @pl.when(pl.program_id(0) == 0)
        def _():
            dw_ref[...] = jnp.zeros_like(dw_ref)
            dbias_ref[...] = jnp.zeros_like(dbias_ref)
            dg_ref[...] = jnp.zeros_like(dg_ref)
            dbeta_ref[...] = jnp.zeros_like(dbeta_ref)

        bu = p_ref[:, 0:D_SGU]
        bv = p_ref[:, D_SGU:2 * D_SGU]
        u = _gelu(bu)
        gv = g_ref[...]
        xhat, rstd, vn = _sgu_norm(bv, gv, beta_ref[...])
        vb = vn.astype(BF16)
        tril = _tril_mask()
        masks = _head_masks(D_SGU)
        whs = [jnp.where(tril, w_ref[h], 0.0).astype(BF16) for h in range(N_SGU_HEADS)]
        mixed = bias_ref[...]
        for h, m in enumerate(masks):
            mixed = mixed + _dot(whs[h], jnp.where(m, vb, jnp.zeros_like(vb)), _NN)
        dy = dy_ref[...]
        dp_ref[:, 0:D_SGU] = dy * mixed * _gelu_grad(bu)
        dmixed = dy * u
        dbias_ref[...] += dmixed
        dmb = dmixed.astype(BF16)
        dvn = jnp.zeros((CHUNK, D_SGU), F32)
        for h, m in enumerate(masks):
            dmh = jnp.where(m, dmb, jnp.zeros_like(dmb))
            dvn = dvn + _dot(whs[h], dmh, _TN)
            dw_ref[h] += jnp.where(tril, _dot(dmh, vb, _NT), 0.0)
        dg_ref[...] += jnp.sum(dvn * xhat, axis=0, keepdims=True)
        dbeta_ref[...] += jnp.sum(dvn, axis=0, keepdims=True)
        dxhat = dvn * gv
        dvg = rstd * (dxhat - jnp.mean(dxhat, axis=-1, keepdims=True)
                      - xhat * jnp.mean(dxhat * xhat, axis=-1, keepdims=True))
        dp_ref[:, D_SGU:2 * D_SGU] = dvg * _gelu_grad(bv)

        @pl.when(pl.program_id(0) == n_chunks - 1)
        def _():
            chan = lax.broadcasted_iota(jnp.int32, (D_SGU, LANES), 0)
            head = lax.broadcasted_iota(jnp.int32, (D_SGU, LANES), 1)
            to_head = jnp.where(chan // HEAD_DIM == head, 1.0, 0.0).astype(BF16)
            db_ref[...] = _dot(_split_bf16(dbias_ref[...]), jnp.concatenate([to_head, to_head], axis=0), _NN)

    vec = pl.BlockSpec((1, D_SGU), lambda i: (0, 0))
    wspec = pl.BlockSpec((N_SGU_HEADS, CHUNK, CHUNK), lambda i: (0, 0, 0))
    bspec = pl.BlockSpec((CHUNK, D_SGU), lambda i: (0, 0))
    return pl.pallas_call(
        body,
        name=name,
        grid=(n_chunks,),
        in_specs=[pl.BlockSpec((CHUNK, 2 * D_SGU), lambda i: (i, 1)), pl.BlockSpec((CHUNK, D_SGU), lambda i: (i, 0)),
                  vec, vec, wspec, bspec],
        out_specs=[pl.BlockSpec((CHUNK, 2 * D_SGU), lambda i: (i, 0)), wspec,
                   pl.BlockSpec((CHUNK, LANES), lambda i: (0, 0)), vec, vec],
        out_shape=[jax.ShapeDtypeStruct((s, 2 * D_SGU), F32),
                   jax.ShapeDtypeStruct((N_SGU_HEADS, CHUNK, CHUNK), F32),
                   jax.ShapeDtypeStruct((CHUNK, LANES), F32),
                   jax.ShapeDtypeStruct((1, D_SGU), F32), jax.ShapeDtypeStruct((1, D_SGU), F32)],
        scratch_shapes=[pltpu.VMEM((CHUNK, D_SGU), F32)],
        compiler_params=_cparams(("arbitrary",)),
    )(p_ab, dyb, ln_g, ln_b, w_s, bias)


N_PAIRS = D_SB // LANES
SB_SCALE = HEAD_DIM ** -0.5


def _sb_logits(z, valid):
    nz = -z
    t = jnp.log(1.0 + jnp.exp(jnp.minimum(z, nz)))
    l1 = jnp.minimum(nz, 0.0) - t
    if valid is not None:
        l1 = jnp.where(valid, l1, 0.0)
    return l1, jnp.minimum(z, 0.0) - t


def _split_hi_lo(x):
    hi = lax.bitcast_convert_type(lax.bitcast_convert_type(x, jnp.uint32) & jnp.uint32(0xFFFF0000), F32)
    return jnp.concatenate([hi, x - hi], axis=1)


def _cumsum_operand(keep):
    half = jnp.concatenate([keep.astype(F32), jnp.ones((CHUNK, CHUNK), F32)], axis=1)
    return jnp.concatenate([half, half], axis=0)


def _attn_fwd(qkv, *, name, exchanges=None):
    s = qkv.shape[0]
    nq = s // CHUNK
    ex = exchanges or _Exchanges([])
    n_ex = len(ex.arrays)

    def body(*refs):
        q_ref, k_ref, v_ref = refs[:3]
        o_ref, t_ref = refs[3 + n_ex:5 + n_ex]
        ex_refs = (refs[3:3 + n_ex], refs[5 + n_ex:5 + 2 * n_ex]) + refs[5 + 2 * n_ex:]
        qi = pl.program_id(1)
        if n_ex:
            @pl.when((pl.program_id(0) == 0) & (qi == 0))
            def _():
                ex.start(*ex_refs)

        q = q_ref[...] * SB_SCALE
        zero = jnp.zeros_like(q)
        masks = _head_masks(LANES)
        qs = [jnp.where(m, q, zero) for m in masks]
        row = lax.broadcasted_iota(jnp.int32, (CHUNK, CHUNK), 0)
        col = lax.broadcasted_iota(jnp.int32, (CHUNK, CHUNK), 1)
        after_op = _cumsum_operand(row > col)

        cmr = col - row

        def blocks(js, carry):
            o, c0, c1 = carry
            kvs, valids = [], []
            for j in js:
                k0 = pl.multiple_of(jnp.maximum(j, 0) * CHUNK, CHUNK)
                kvs.append((k_ref[pl.ds(k0, CHUNK), :], v_ref[pl.ds(k0, CHUNK), :]))
                valids.append(cmr < jnp.where(j >= 0, (qi - j) * CHUNK, -CHUNK))
            units = [(h, b) for b in range(len(js)) for h in range(2)]
            zs = [_dot(qs[h], kvs[b][0], _NT) for h, b in units]
            logits = [_sb_logits(z, valids[b]) for z, (h, b) in zip(zs, units)]
            sums = [_dot(_split_hi_lo(l1), after_op, _NN) for l1, _ in logits]
            cs = [c0, c1]
            probs = []
            for (h, b), (_, lb), sm in zip(units, logits, sums):
                probs.append(jnp.where(valids[b], jnp.exp(lb + sm[:, :CHUNK] + cs[h]), 0.0))
                cs[h] = cs[h] + sm[:, CHUNK:]
            for (h, b), a in zip(units, probs):
                o = o + _dot(a.astype(BF16), jnp.where(masks[h], kvs[b][1], zero), _NN)
            return o, cs[0], cs[1]

        zc = jnp.zeros((CHUNK, LANES), F32)
        n_four = (qi + 1) // 4
        carry = lax.fori_loop(0, n_four, lambda jj, c: blocks([qi - 4 * jj - i for i in range(4)], c), (zc,) * 3)
        top = qi - 4 * n_four
        o, c0, c1 = lax.fori_loop(0, (top + 2) // 2, lambda jj, c: blocks([top - 2 * jj, top - 2 * jj - 1], c), carry)
        o_ref[...] = o
        t_ref[:, 0:LANES] = c0
        t_ref[:, LANES:2 * LANES] = c1
        if n_ex:
            @pl.when((pl.program_id(0) == N_PAIRS - 1) & (qi == nq - 1))
            def _():
                ex.wait(*ex_refs)

    return pl.pallas_call(
        body,
        name=name,
        grid=(N_PAIRS, nq),
        in_specs=[pl.BlockSpec((CHUNK, LANES), lambda p, i: (i, p)),
                  pl.BlockSpec((s, LANES), lambda p, i: (0, N_PAIRS + p)),
                  pl.BlockSpec((s, LANES), lambda p, i: (0, 2 * N_PAIRS + p))] + ex.in_specs,
        out_specs=[pl.BlockSpec((CHUNK, LANES), lambda p, i: (i, p)),
                   pl.BlockSpec((CHUNK, 2 * LANES), lambda p, i: (i, p))] + ex.out_specs,
        out_shape=[jax.ShapeDtypeStruct((s, D_SB), F32), jax.ShapeDtypeStruct((s, 2 * D_SB), F32)] + ex.out_shapes,
        scratch_shapes=ex.scratch_shapes if n_ex else [],
        compiler_params=_cparams(("arbitrary", "arbitrary")),
    )(qkv, qkv, qkv, *ex.arrays)


def _attn_bwd(qkv, t_tot, do, *, name, exchanges=None):
    s = qkv.shape[0]
    nq = s // CHUNK
    ex = exchanges or _Exchanges([])
    n_ex = len(ex.arrays)

    def body(*refs):
        q_ref, k_ref, v_ref, t_ref, do_ref = refs[:5]
        dq_ref, dk_ref, dv_ref = refs[5 + n_ex:8 + n_ex]
        ex_refs = (refs[5:5 + n_ex], refs[8 + n_ex:8 + 2 * n_ex]) + refs[8 + 2 * n_ex:]
        qi = pl.program_id(1)
        if n_ex:
            @pl.when((pl.program_id(0) == 0) & (qi == 0))
            def _():
                ex.start(*ex_refs)

        @pl.when(qi == 0)
        def _():
            dk_ref[...] = jnp.zeros_like(dk_ref)
            dv_ref[...] = jnp.zeros_like(dv_ref)

        q = q_ref[...] * SB_SCALE
        dob = do_ref[...].astype(BF16)
        zero = jnp.zeros_like(q)
        masks = _head_masks(LANES)
        qs = [jnp.where(m, q, zero) for m in masks]
        dos = [jnp.where(m, dob, zero) for m in masks]
        tots = [t_ref[:, 0:LANES], t_ref[:, LANES:2 * LANES]]
        row = lax.broadcasted_iota(jnp.int32, (CHUNK, CHUNK), 0)
        col = lax.broadcasted_iota(jnp.int32, (CHUNK, CHUNK), 1)
        upto_op = _cumsum_operand(row <= col)
        before_op = _cumsum_operand(row < col)

        cmr = col - row

        def blocks(js, carry):
            dq, cl0, cl1, cp0, cp1 = carry
            starts = [pl.multiple_of(jnp.minimum(j, nq - 1) * CHUNK, CHUNK) for j in js]
            valids = [cmr < (qi - j) * CHUNK for j in js]
            kvs = [(k_ref[pl.ds(k0, CHUNK), :], v_ref[pl.ds(k0, CHUNK), :]) for k0 in starts]
            units = [(h, b) for b in range(len(js)) for h in range(2)]
            zs = [_dot(qs[h], kvs[b][0], _NT) for h, b in units]
            das = [_dot(dos[h], kvs[b][1], _NT) for h, b in units]
            logits = [_sb_logits(z, valids[b]) for z, (h, b) in zip(zs, units)]
            sums = [_dot(_split_hi_lo(l1), upto_op, _NN) for l1, _ in logits]
            cls, cps = [cl0, cl1], [cp0, cp1]
            probs, gs = [], []
            for (h, b), (_, lb), sm, da in zip(units, logits, sums, das):
                a = jnp.where(valids[b], jnp.exp(lb + (tots[h] - cls[h] - sm[:, :CHUNK])), 0.0)
                probs.append(a)
                gs.append(a * da)
                cls[h] = cls[h] + sm[:, CHUNK:]
            sums_g = [_dot(_split_hi_lo(g), before_op, _NN) for g in gs]
            dzs = []
            for (h, b), (_, lb), g, sg in zip(units, logits, gs, sums_g):
                dz = g - (g + sg[:, :CHUNK] + cps[h]) * jnp.exp(lb)
                dzs.append(jnp.where(valids[b], dz, 0.0).astype(BF16))
                cps[h] = cps[h] + sg[:, CHUNK:]
            for (h, b), dzb in zip(units, dzs):
                dq = dq + _dot(dzb, jnp.where(masks[h], kvs[b][0], zero), _NN)
            for b, k0 in enumerate(starts):
                dk_ref[pl.ds(k0, CHUNK), :] += _dot(dzs[2 * b], qs[0], _TN) + _dot(dzs[2 * b + 1], qs[1], _TN)
                dv_ref[pl.ds(k0, CHUNK), :] += (_dot(probs[2 * b].astype(BF16), dos[0], _TN)
                                                + _dot(probs[2 * b + 1].astype(BF16), dos[1], _TN))
            return dq, cls[0], cls[1], cps[0], cps[1]

        zc = jnp.zeros((CHUNK, LANES), F32)
        n_four = (qi + 1) // 4
        carry = lax.fori_loop(0, n_four, lambda jj, c: blocks([4 * jj + i for i in range(4)], c), (zc,) * 5)
        base = 4 * n_four
        carry = lax.fori_loop(0, (qi - base + 2) // 2, lambda jj, c: blocks([base + 2 * jj, base + 2 * jj + 1], c), carry)
        dq_ref[...] = carry[0] * SB_SCALE
        if n_ex:
            @pl.when((pl.program_id(0) == N_PAIRS - 1) & (qi == nq - 1))
            def _():
                ex.wait(*ex_refs)

    blk = pl.BlockSpec((CHUNK, LANES), lambda p, i: (i, p))
    col_blk = pl.BlockSpec((s, LANES), lambda p, i: (0, p))
    out = jax.ShapeDtypeStruct((s, D_SB), F32)
    return pl.pallas_call(
        body,
        name=name,
        grid=(N_PAIRS, nq),
        in_specs=[blk,
                  pl.BlockSpec((s, LANES), lambda p, i: (0, N_PAIRS + p)),
                  pl.BlockSpec((s, LANES), lambda p, i: (0, 2 * N_PAIRS + p)),
                  pl.BlockSpec((CHUNK, 2 * LANES), lambda p, i: (i, p)),
                  blk] + ex.in_specs,
        out_specs=[blk, col_blk, col_blk] + ex.out_specs,
        out_shape=[out, out, out] + ex.out_shapes,
        scratch_shapes=ex.scratch_shapes if n_ex else [],
        compiler_params=_cparams(("arbitrary", "arbitrary")),
    )(qkv, qkv, qkv, t_tot, do, *ex.arrays)


FFN_TILE = 256
FFN_COLS = 256
N_FF_BLOCKS = D_FF // FFN_COLS


def _ffn_act_fwd(up0, conv_w, conv_b, *, name):
    s = up0.shape[0]
    nt = s // FFN_TILE

    def body(xg_ref, xv_ref, wg_ref, wv_ref, bg_ref, bv_ref, act_ref, pg_ref, pv_ref):
        pg_ref[0:FFN_HALO, :] = jnp.zeros((FFN_HALO, FFN_COLS), F32)
        pv_ref[0:FFN_HALO, :] = jnp.zeros((FFN_HALO, FFN_COLS), F32)
        pg_ref[FFN_HALO:, :] = xg_ref[...]
        pv_ref[FFN_HALO:, :] = xv_ref[...]

        def tile(i, c):
            t0 = pl.multiple_of(i * FFN_TILE, FFN_TILE)
            outs = []
            for p_ref, w_ref, b_ref in ((pg_ref, wg_ref, bg_ref), (pv_ref, wv_ref, bv_ref)):
                window = p_ref[pl.ds(t0, FFN_TILE + FFN_HALO), :]
                acc = b_ref[...] + w_ref[2:3, :] * window[FFN_HALO:, :]
                for j in range(1, FFN_K):
                    acc = acc + w_ref[FFN_K - 1 - j:FFN_K - j, :] * _shift_down(window, j, FFN_HALO)
                outs.append(acc)
            gate, val = outs
            act_ref[pl.ds(t0, FFN_TILE), :] = (gate * _sigmoid(gate) * val).astype(BF16)
            return c

        lax.fori_loop(0, nt, tile, 0)

    gcol = lambda rows: pl.BlockSpec((rows, FFN_COLS), lambda j: (0, j))
    vcol = lambda rows: pl.BlockSpec((rows, FFN_COLS), lambda j: (0, j + N_FF_BLOCKS))
    return pl.pallas_call(
        body,
        name=name,
        grid=(N_FF_BLOCKS,),
        in_specs=[gcol(s), vcol(s), gcol(FFN_K), vcol(FFN_K), gcol(1), vcol(1)],
        out_specs=gcol(s),
        out_shape=jax.ShapeDtypeStruct((s, D_FF), BF16),
        scratch_shapes=[pltpu.VMEM((s + FFN_HALO, FFN_COLS), F32), pltpu.VMEM((s + FFN_HALO, FFN_COLS), F32)],
        compiler_params=_cparams(("parallel",)),
    )(up0, up0, conv_w, conv_w, conv_b, conv_b)


def _ffn_act_bwd(up0, dact, conv_w, conv_b, *, name):
    s = up0.shape[0]
    nt = s // FFN_TILE

    def body(xm_ref, xp_ref, da_ref, wm_ref, wp_ref, bm_ref, bp_ref, dx_ref, dw_ref, db_ref, pm_ref, pp_ref, dm_ref):
        is_gate = pl.program_id(0) < N_FF_BLOCKS
        zeros = jnp.zeros((FFN_HALO, FFN_COLS), F32)
        pm_ref[0:FFN_HALO, :] = zeros
        pp_ref[0:FFN_HALO, :] = zeros
        pm_ref[FFN_HALO:, :] = xm_ref[...]
        pp_ref[FFN_HALO:, :] = xp_ref[...]
        dm_ref[s:s + FFN_HALO, :] = zeros
        dw_ref[...] = jnp.zeros_like(dw_ref)
        db_ref[...] = jnp.zeros_like(db_ref)

        def tile(i, c):
            t0 = pl.multiple_of(i * FFN_TILE, FFN_TILE)
            wm = pm_ref[pl.ds(t0, FFN_TILE + FFN_HALO), :]
            wp = pp_ref[pl.ds(t0, FFN_TILE + FFN_HALO), :]
            taps = [_shift_down(wm, j, FFN_HALO) for j in range(FFN_K)]
            mine = bm_ref[...]
            partner = bp_ref[...]
            for j in range(FFN_K):
                mine = mine + wm_ref[FFN_K - 1 - j:FFN_K - j, :] * taps[j]
                partner = partner + wp_ref[FFN_K - 1 - j:FFN_K - j, :] * _shift_down(wp, j, FFN_HALO)
            da = da_ref[pl.ds(t0, FFN_TILE), :]
            sg_m = _sigmoid(mine)
            sg_p = _sigmoid(partner)
            d_as_gate = da * partner * (sg_m * (1.0 + mine * (1.0 - sg_m)))
            d_as_val = da * partner * sg_p
            dm = jnp.where(is_gate, d_as_gate, d_as_val)
            dm_ref[pl.ds(t0, FFN_TILE), :] = dm
            db_ref[...] += jnp.sum(dm, axis=0, keepdims=True)
            for j in range(FFN_K):
                dw_ref[FFN_K - 1 - j:FFN_K - j, :] += jnp.sum(dm * taps[j], axis=0, keepdims=True)
            return c

        lax.fori_loop(0, nt, tile, 0)

        def tile_dx(i, c):
            t0 = pl.multiple_of(i * FFN_TILE, FFN_TILE)
            window = dm_ref[pl.ds(t0, FFN_TILE + FFN_HALO), :]
            dx = jnp.zeros((FFN_TILE, FFN_COLS), F32)
            for j in range(FFN_K):
                dx = dx + wm_ref[FFN_K - 1 - j:FFN_K - j, :] * _shift_up(window, j, FFN_TILE)
            dx_ref[pl.ds(t0, FFN_TILE), :] = dx.astype(BF16)
            return c

        lax.fori_loop(0, nt, tile_dx, 0)

    nb = 2 * N_FF_BLOCKS
    mine = lambda rows: pl.BlockSpec((rows, FFN_COLS), lambda j: (0, j))
    partner = lambda rows: pl.BlockSpec((rows, FFN_COLS), lambda j: (0, (j + N_FF_BLOCKS) % nb))
    return pl.pallas_call(
        body,
        name=name,
        grid=(nb,),
        in_specs=[mine(s), partner(s), pl.BlockSpec((s, FFN_COLS), lambda j: (0, j % N_FF_BLOCKS)),
                  mine(FFN_K), partner(FFN_K), mine(1), partner(1)],
        out_specs=[mine(s), mine(FFN_K), mine(1)],
        out_shape=[jax.ShapeDtypeStruct((s, 2 * D_FF), BF16), jax.ShapeDtypeStruct((FFN_K, 2 * D_FF), F32),
                   jax.ShapeDtypeStruct((1, 2 * D_FF), F32)],
        scratch_shapes=[pltpu.VMEM((s + FFN_HALO, FFN_COLS), F32), pltpu.VMEM((s + FFN_HALO, FFN_COLS), F32),
                        pltpu.VMEM((s + FFN_HALO, FFN_COLS), F32)],
        compiler_params=_cparams(("parallel",)),
    )(up0, up0, dact, conv_w, conv_w, conv_b, conv_b)


MESH = pl.DeviceIdType.MESH


def _position():
    x, y, c = lax.axis_index("x"), lax.axis_index("y"), lax.axis_index("c")
    return x, y, c, 4 * x + 2 * y + c


def _peer(k):
    x, y, c, _ = _position()
    px = 1 - x if k & 4 else x
    py = 1 - y if k & 2 else y
    pc = 1 - c if k & 1 else c
    return (px, py, pc), 4 * px + 2 * py + pc


def _exchange(src, *, kind, name):
    ex = _Exchanges([(kind, src)])

    def body(src_ref, out_ref, send_sems, recv_sems, local_sems):
        ex.start([src_ref], [out_ref], send_sems, recv_sems, local_sems)
        ex.wait([src_ref], [out_ref], send_sems, recv_sems, local_sems)

    return pl.pallas_call(
        body,
        name=name,
        in_specs=ex.in_specs,
        out_specs=ex.out_specs[0],
        out_shape=ex.out_shapes[0],
        scratch_shapes=ex.scratch_shapes,
    )(src)


_HBM = pl.BlockSpec(memory_space=pltpu.HBM)
_SEM = pl.BlockSpec(memory_space=pltpu.SEMAPHORE)
_DATAFLOW = pltpu.SideEffectType.DATAFLOW_SIDE_EFFECTING
N_PEERS = N_DEV - 1


class _SplitExchange:
    def __init__(self, src, *, kind, name):
        self.kind, self.name, self.dtype = kind, name, src.dtype
        scatter = kind == "scatter_rows"
        self.scatter = scatter
        self.r = src.shape[0] // N_DEV if scatter else src.shape[0]
        self.cols = src.shape[1]
        self.land_shape = (N_DEV, self.r, self.cols) if scatter else (N_DEV * self.r, self.cols)
        r = self.r

        def copies(src_ref, land_ref, send_sems, recv_sems):
            me = _position()[3]

            def rows(ref, idx):
                return ref.at[pl.ds(pl.multiple_of(idx * r, r), r), :]

            outgoing = (lambda idx: rows(src_ref, idx)) if scatter else (lambda idx: src_ref)
            slot = (lambda idx: land_ref.at[idx]) if scatter else (lambda idx: rows(land_ref, idx))
            sends, recvs = [], []
            for k in range(1, N_DEV):
                peer, pidx = _peer(k)
                sems = dict(send_sem=send_sems[k - 1], recv_sem=recv_sems[k - 1], device_id=peer, device_id_type=MESH)
                sends.append(pltpu.make_async_remote_copy(src_ref=outgoing(pidx), dst_ref=slot(me), **sems))
                recvs.append(pltpu.make_async_remote_copy(src_ref=outgoing(pidx), dst_ref=slot(pidx), **sems))
            return sends, recvs

        def start_body(src_ref, land_ref, *outs):
            sends, _ = copies(src_ref, land_ref, outs[:N_PEERS], outs[N_PEERS:2 * N_PEERS])
            for cp in sends:
                cp.start()
            outs[-1][...] = jnp.zeros_like(outs[-1])

        sem = pltpu.SemaphoreType.DMA(())
        out = pl.pallas_call(
            start_body,
            name=f"{name}_start",
            in_specs=(_HBM, _HBM),
            out_specs=(_SEM,) * (2 * N_PEERS) + (_HBM, _HBM, pl.BlockSpec(memory_space=pltpu.VMEM)),
            out_shape=(sem,) * (2 * N_PEERS) + (pltpu.HBM(src.shape, src.dtype), pltpu.HBM(self.land_shape, src.dtype),
                                                jax.ShapeDtypeStruct((8, LANES), F32)),
            input_output_aliases={0: 2 * N_PEERS, 1: 2 * N_PEERS + 1},
            compiler_params=pltpu.CompilerParams(has_side_effects=_DATAFLOW),
        )(pltpu.with_memory_space_constraint(src, pltpu.HBM),
          pltpu.with_memory_space_constraint(lax.empty(self.land_shape, src.dtype), pltpu.HBM))
        self.sems, self.src_thru, self.land_thru = out[:2 * N_PEERS], out[2 * N_PEERS], out[2 * N_PEERS + 1]
        self.token = out[-1][0, 0]
        self._copies = copies

    def finish(self, after):
        copies = self._copies

        def wait_body(src_ref, land_ref, *rest):
            sends, recvs = copies(src_ref, land_ref, rest[:N_PEERS], rest[N_PEERS:2 * N_PEERS])
            for cp in sends:
                cp.wait_send()
            for cp in recvs:
                cp.wait_recv()

        src, got = pl.pallas_call(
            wait_body,
            name=f"{self.name}_wait",
            in_specs=(_HBM, _HBM) + (_SEM,) * (2 * N_PEERS) + (pl.BlockSpec(memory_space=pl.ANY),),
            out_specs=(_HBM, _HBM),
            out_shape=(pltpu.HBM(self.src_thru.shape, self.dtype), pltpu.HBM(self.land_shape, self.dtype)),
            input_output_aliases={0: 0, 1: 1},
            compiler_params=pltpu.CompilerParams(has_side_effects=_DATAFLOW),
        )(self.src_thru, self.land_thru, *self.sems, after)
        x, y, c = lax.axis_index("x"), lax.axis_index("y"), lax.axis_index("c")
        me = 4 * x + 2 * y + c
        if self.scatter:
            own = lax.dynamic_slice(src, (me * self.r, 0), (self.r, self.cols))
            return lax.dynamic_update_slice(got, own[None], (me, 0, 0))
        return lax.dynamic_update_slice(got, src, (me * self.r, 0))


class _Exchanges:
    def __init__(self, items):
        self.kinds = [kind for kind, _ in items]
        self.arrays = [src for _, src in items]
        self.out_shapes = []
        self.block_rows = []
        for kind, src in items:
            scatter, by_rows = kind.startswith("scatter"), kind.endswith("rows")
            if by_rows:
                r = src.shape[0] // N_DEV if scatter else src.shape[0]
                shape = (N_DEV, r, src.shape[1]) if scatter else (N_DEV * r, src.shape[1])
            else:
                r = None
                shape = src.shape if scatter else (N_DEV,) + src.shape
            self.block_rows.append(r)
            self.out_shapes.append(jax.ShapeDtypeStruct(shape, src.dtype))
        n = len(items)
        self.in_specs = [pl.BlockSpec(memory_space=pl.ANY)] * n
        self.out_specs = [pl.BlockSpec(memory_space=pl.ANY)] * n
        self.scratch_shapes = [pltpu.SemaphoreType.DMA((n * (N_DEV - 1),)), pltpu.SemaphoreType.DMA((n * (N_DEV - 1),)),
                               pltpu.SemaphoreType.DMA((n,))]

    def _copies(self, i, src_ref, out_ref, send_sems, recv_sems, local_sems):
        kind, r = self.kinds[i], self.block_rows[i]
        scatter, by_rows = kind.startswith("scatter"), kind.endswith("rows")
        me = _position()[3]

        def rows(ref, idx):
            return ref.at[pl.ds(pl.multiple_of(idx * r, r), r), :]

        def outgoing(idx):
            if not scatter:
                return src_ref
            return rows(src_ref, idx) if by_rows else src_ref.at[idx]

        def slot(idx):
            return rows(out_ref, idx) if (by_rows and not scatter) else out_ref.at[idx]

        local = pltpu.make_async_copy(outgoing(me), slot(me), local_sems.at[i])
        sends, recvs = [], []
        for k in range(1, N_DEV):
            peer, pidx = _peer(k)
            sem = i * (N_DEV - 1) + k - 1
            sends.append(pltpu.make_async_remote_copy(src_ref=outgoing(pidx), dst_ref=slot(me), send_sem=send_sems.at[sem],
                                                      recv_sem=recv_sems.at[sem], device_id=peer, device_id_type=MESH))
            recvs.append(pltpu.make_async_remote_copy(src_ref=outgoing(pidx), dst_ref=slot(pidx), send_sem=send_sems.at[sem],
                                                      recv_sem=recv_sems.at[sem], device_id=peer, device_id_type=MESH))
        return local, sends, recvs

    def start(self, src_refs, out_refs, *sems):
        for i, (src_ref, out_ref) in enumerate(zip(src_refs, out_refs)):
            local, sends, _ = self._copies(i, src_ref, out_ref, *sems)
            local.start()
            for cp in sends:
                cp.start()

    def wait(self, src_refs, out_refs, *sems):
        for i, (src_ref, out_ref) in enumerate(zip(src_refs, out_refs)):
            local, sends, recvs = self._copies(i, src_ref, out_ref, *sems)
            for cp in recvs:
                cp.wait_recv()
            for cp in sends:
                cp.wait_send()
            local.wait()


def _row_tile(rows):
    return _tile(rows, (256, 128, 64, 32, 16, 8))


def _layer_parts_specs(n_layers, n_parts, tr, cols):
    return [pl.BlockSpec((n_parts, tr, cols), lambda l, i, j=j: (0, jnp.where(l == j, i, 0), 0)) for j in range(n_layers)]


def _select_layer_sum(p_refs):
    l = pl.program_id(0)
    g = None
    for j, p_ref in enumerate(p_refs):
        gj = p_ref[0].astype(F32)
        for k in range(1, p_ref.shape[0]):
            gj = gj + p_ref[k].astype(F32)
        g = gj if g is None else jnp.where(l == j, gj, g)
    return g


def _sum_parts(parts, *, name):
    n_layers = len(parts)
    n_parts, rows, cols = parts[0].shape
    tr = _row_tile(rows)

    def body(*refs):
        refs[-1][...] = _select_layer_sum(refs[:n_layers])

    return pl.pallas_call(
        body,
        name=name,
        grid=(n_layers, rows // tr),
        in_specs=_layer_parts_specs(n_layers, n_parts, tr, cols),
        out_specs=pl.BlockSpec((None, tr, cols), lambda l, i: (l, i, 0)),
        out_shape=jax.ShapeDtypeStruct((n_layers, rows, cols), F32),
        compiler_params=_cparams(("arbitrary", "arbitrary")),
    )(*parts)


def _adamw(parts, w, m, v, *, name):
    n_layers, rows, cols = w.shape
    summed = not isinstance(parts, (list, tuple))
    tr = _row_tile(rows)
    n_in = 1 if summed else n_layers

    def body(*refs):
        w_ref, m_ref, v_ref, g_ref, d_ref, m2_ref, v2_ref = refs[n_in:]
        g = refs[0][...] if summed else _select_layer_sum(refs[:n_in])
        m2 = ADAM_B1 * m_ref[...] + (1.0 - ADAM_B1) * g
        v2 = ADAM_B2 * v_ref[...] + (1.0 - ADAM_B2) * (g * g)
        m_hat = m2 / (1.0 - ADAM_B1 ** ADAM_STEP)
        v_hat = v2 / (1.0 - ADAM_B2 ** ADAM_STEP)
        g_ref[...] = g
        d_ref[...] = -ADAM_LR * (m_hat / (jnp.sqrt(v_hat) + ADAM_EPS) + ADAM_WD * w_ref[...])
        m2_ref[...] = m2
        v2_ref[...] = v2

    slab = pl.BlockSpec((None, tr, cols), lambda l, i: (l, i, 0))
    out = jax.ShapeDtypeStruct((n_layers, rows, cols), F32)
    p_specs = [slab] if summed else _layer_parts_specs(n_layers, parts[0].shape[0], tr, cols)
    return pl.pallas_call(
        body,
        name=name,
        grid=(n_layers, rows // tr),
        in_specs=p_specs + [slab, slab, slab],
        out_specs=[slab, slab, slab, slab],
        out_shape=[out, out, out, out],
        compiler_params=_cparams(("arbitrary", "arbitrary")),
    )(*((parts,) if summed else tuple(parts)), w, m, v)


SLAB_ROWS = 256
_SMALL_SHARDED = (("conv_w", (2, 31, 32)), ("ffn_conv_w", (2, 3, 704)))
_REPLICATED = (("g_mix", (2, 1024)), ("conv_b", (2, 256)), ("conv_ln_g", (2, 256)), ("conv_ln_b", (2, 256)),
               ("sgu_ln_g", (2, 256)), ("sgu_ln_b", (2, 256)), ("sgu_w", (2, 4, 128, 128)), ("sgu_b", (2, 4, 128)),
               ("g_out", (2, 1024)), ("g_ffn", (2, 1024)), ("ffn_conv_b", (2, 5632)), ("g_final", (1024,)))


def _seg_rows(n_elems):
    return -(-n_elems // LANES)


def _pack(arrays, lead=()):
    segs = []
    for a in arrays:
        flat = a.reshape(lead + (-1,)).astype(F32)
        pad = _seg_rows(flat.shape[-1]) * LANES - flat.shape[-1]
        if pad:
            flat = jnp.pad(flat, [(0, 0)] * len(lead) + [(0, pad)])
        segs.append(flat)
    flat = jnp.concatenate(segs, axis=-1)
    rows = flat.shape[-1] // LANES
    pad_rows = -rows % SLAB_ROWS
    if pad_rows:
        flat = jnp.pad(flat, [(0, 0)] * len(lead) + [(0, pad_rows * LANES)])
    return flat.reshape(lead + (rows + pad_rows, LANES))


def _unpack(slab, shapes, lead=()):
    flat = slab.reshape(lead + (-1,))
    out, off = [], 0
    for shape in shapes:
        n = math.prod(shape)
        out.append(flat[..., off:off + n].reshape(lead + tuple(shape)))
        off += _seg_rows(n) * LANES
    return out


def _split_last(full):
    split = full.shape[:-1] + (N_DEV, full.shape[-1] // N_DEV)
    return jnp.moveaxis(full.reshape(split), -2, 0)


def _join_last(blocks):
    moved = jnp.moveaxis(blocks, 0, -2)
    return moved.reshape(moved.shape[:-2] + (moved.shape[-2] * moved.shape[-1],))


def _gathered(wt, n, l, after):
    if isinstance(wt[n][l], _SplitExchange):
        wt[n][l] = wt[n][l].finish(after)
    return wt[n][l]


def _layer_fwd(l, x, wt, small):
    tag = f"l{l}"
    h = _rmsnorm_fwd(x, small["g_mix"][l][None], name=f"{tag}_norm_mix")
    w_in_t = _gathered(wt, "w_in_t", l, h)
    p_ab = _matmul(h, w_in_t, "nt", name=f"{tag}_proj_ab", n=D_AB)
    qkv = _matmul(h, w_in_t, "nt", name=f"{tag}_proj_qkv", n=D_QKV, b_n0=D_AB, out_dtype=BF16)
    ya = _mixer_a_fwd(p_ab, wt["conv_w"][l], small["conv_b"][l][None], small["conv_ln_g"][l][None],
                      small["conv_ln_b"][l][None], name=f"{tag}_mixer_a")
    bias = jnp.repeat(small["sgu_b"][l].T, HEAD_DIM, axis=1)
    yb = _sgu_fwd(p_ab, small["sgu_ln_g"][l][None], small["sgu_ln_b"][l][None], small["sgu_w"][l], bias,
                  name=f"{tag}_sgu")
    yc, t_tot = _attn_fwd(qkv, name=f"{tag}_attn")
    y = _combine_fwd(ya, yb, yc, small["g_out"][l][None], name=f"{tag}_combine")
    x1 = _matmul(y, _gathered(wt, "w_out", l, y), "nn", name=f"{tag}_out_proj", residual=x)
    h2 = _rmsnorm_fwd(x1, small["g_ffn"][l][None], name=f"{tag}_norm_ffn")
    up0 = _matmul(h2, _gathered(wt, "w_up_t", l, h2), "nt", name=f"{tag}_up")
    act = _ffn_act_fwd(up0, wt["ffn_conv_w"][l], small["ffn_conv_b"][l][None], name=f"{tag}_ffn_act")
    x2 = _matmul(act, _gathered(wt, "w_down", l, act), "nn", name=f"{tag}_down", residual=x1)
    saved = dict(x=x, h=h, p_ab=p_ab, qkv=qkv, ya=ya, yb=yb, yc=yc, t_tot=t_tot, y=y, x1=x1, h2=h2, up0=up0,
                 act=act, bias=bias)
    return x2, saved


def _layer_bwd(l, dres, sv, wt, small, scattering):
    tag = f"l{l}b"
    g = {}

    def scatter(n, partial):
        scattering[n][l] = _SplitExchange(partial, kind="scatter_rows", name=f"scatter_{n}_l{l}")
        return scattering[n][l].token

    dx2, dx2_b = dres
    dact = _matmul(dx2_b, wt["w_down"][l], "nt", name=f"{tag}_dact")
    tok = scatter("w_down", _matmul(sv["act"], dx2_b, "tn", name=f"{tag}_dw_down", out_dtype=BF16))
    dup0, g["ffn_conv_w"], dfb = _ffn_act_bwd(sv["up0"], dact, wt["ffn_conv_w"][l], small["ffn_conv_b"][l][None] + tok,
                                              name=f"{tag}_ffn_act")
    g["ffn_conv_b"] = dfb[0]
    dh2 = _matmul(dup0, wt["w_up_t"][l], "nn", name=f"{tag}_dh2")
    tok = scatter("w_up_t", _matmul(dup0, sv["h2"], "tn", name=f"{tag}_dw_up", out_dtype=BF16))
    dx1, dx1_b, dg = _rmsnorm_bwd(sv["x1"], small["g_ffn"][l][None] + tok, dh2, dx2, name=f"{tag}_norm_ffn")
    g["g_ffn"] = dg[0]
    dy = _matmul(dx1_b, wt["w_out"][l], "nt", name=f"{tag}_dy")
    tok = scatter("w_out", _matmul(sv["y"], dx1_b, "tn", name=f"{tag}_dw_out", out_dtype=BF16))
    dya, dyb, dyc, dg = _combine_bwd(dy, sv["ya"], sv["yb"], sv["yc"], small["g_out"][l][None] + tok,
                                     name=f"{tag}_combine")
    g["g_out"] = dg[0]
    dq, dk, dv = _attn_bwd(sv["qkv"], sv["t_tot"], dyc, name=f"{tag}_attn")
    dp_b, g["sgu_w"], db, dg, dbeta = _sgu_bwd(sv["p_ab"], dyb, small["sgu_ln_g"][l][None], small["sgu_ln_b"][l][None],
                                               small["sgu_w"][l], sv["bias"], name=f"{tag}_sgu")
    g["sgu_b"] = db[:, :N_SGU_HEADS].T
    g["sgu_ln_g"], g["sgu_ln_b"] = dg[0], dbeta[0]
    dp_a, g["conv_w"], dcb, dg, dbeta = _mixer_a_bwd(sv["p_ab"], dya, wt["conv_w"][l], small["conv_b"][l][None],
                                                     small["conv_ln_g"][l][None], small["conv_ln_b"][l][None],
                                                     name=f"{tag}_mixer_a")
    g["conv_b"], g["conv_ln_g"], g["conv_ln_b"] = dcb[0], dg[0], dbeta[0]
    dp = jnp.concatenate([dp_a.astype(BF16), dp_b.astype(BF16), dq.astype(BF16), dk.astype(BF16), dv.astype(BF16)],
                         axis=1)
    dh = _matmul(dp, wt["w_in_t"][l], "nn", name=f"{tag}_dh")
    tok = scatter("w_in_t", _matmul(dp, sv["h"], "tn", name=f"{tag}_dw_in", out_dtype=BF16))
    dx, dx_b, dg = _rmsnorm_bwd(sv["x"], small["g_mix"][l][None] + tok, dh, dx1, name=f"{tag}_norm_mix")
    g["g_mix"] = dg[0]
    return (dx, dx_b), g


_BIG = ("w_in_t", "w_out", "w_up_t", "w_down")


def kernel(x, g_mix, w_in, conv_w, conv_b, conv_ln_g, conv_ln_b, sgu_ln_g, sgu_ln_b, sgu_w, sgu_b, g_out, w_out, g_ffn, w_up, ffn_conv_w, ffn_conv_b, w_down, g_final, loss_target, m_g_mix, m_w_in, m_conv_w, m_conv_b, m_conv_ln_g, m_conv_ln_b, m_sgu_ln_g, m_sgu_ln_b, m_sgu_w, m_sgu_b, m_g_out, m_w_out, m_g_ffn, m_w_up, m_ffn_conv_w, m_ffn_conv_b, m_w_down, m_g_final, v_g_mix, v_w_in, v_conv_w, v_conv_b, v_conv_ln_g, v_conv_ln_b, v_sgu_ln_g, v_sgu_ln_b, v_sgu_w, v_sgu_b, v_g_out, v_w_out, v_g_ffn, v_w_up, v_ffn_conv_w, v_ffn_conv_b, v_w_down, v_g_final):
    given = dict(locals())
    n_layers = g_mix.shape[0]
    layers = range(n_layers)
    small_sharded = [n for n, _ in _SMALL_SHARDED]
    replicated = [n for n, _ in _REPLICATED]
    small = {n: given[n] for n in replicated}

    shard = {"w_in_t": [w_in[l].T.astype(BF16) for l in layers], "w_out": [w_out[l].astype(BF16) for l in layers],
             "w_up_t": [w_up[l].T.astype(BF16) for l in layers], "w_down": [w_down[l].astype(BF16) for l in layers]}
    order = [(n, l) for l in layers for n in _BIG]
    wt = {n: [None] * n_layers for n in _BIG}
    for n, l in order:
        wt[n][l] = _SplitExchange(shard[n][l], kind="gather_rows", name=f"gather_{n}_l{l}")
    small["g_mix"] = g_mix + sum(wt[n][l].token for n, l in order)
    filters = _exchange(_pack([given[n] for n in small_sharded]), kind="gather_blocks", name="gather_filters")
    for n, blocks in zip(small_sharded, _unpack(filters, [s for _, s in _SMALL_SHARDED], lead=(N_DEV,))):
        wt[n] = _join_last(blocks)

    xs = x[0]
    saved = []
    for l in layers:
        xs, sv = _layer_fwd(l, xs, wt, small)
        saved.append(sv)
    loss_tile, dx, dx_b, dgf = _loss_head(xs, g_final[None], loss_target[0], name="loss_head")
    dres = (dx, dx_b)
    scattering = {n: [None] * n_layers for n in _BIG}
    grads = [None] * n_layers
    for l in reversed(layers):
        dres, grads[l] = _layer_bwd(l, dres, saved[l], wt, small, scattering)
    received = {n: [scattering[n][l].finish(dres[0]) for l in layers] for n in _BIG}
    partial = {n: jnp.stack([g[n] for g in grads]) for n in grads[0]}
    partial["g_final"] = dgf[0]

    out = {}

    def update(n, parts):
        results = _adamw(parts, given[n], given["m_" + n], given["v_" + n], name=f"adamw_{n}")
        for pre, res in zip(("grad_", "delta_", "new_m_", "new_v_"), results):
            out[pre + n] = res

    update("w_out", received["w_out"])
    update("w_down", received["w_down"])
    for n in ("w_in", "w_up"):
        update(n, jnp.swapaxes(_sum_parts(received[n + "_t"], name=f"sum_{n}"), 1, 2))

    own = _pack([_split_last(partial[n]) for n in small_sharded], lead=(N_DEV,))
    shared = _pack([partial[n] for n in replicated])
    slab = jnp.concatenate([own, jnp.broadcast_to(shared[None], (N_DEV,) + shared.shape)], axis=1)
    slab = _exchange(slab, kind="scatter_blocks", name="scatter_small_grads")
    stacks = [jnp.concatenate([_pack([given[pre + n] for n in small_sharded]),
                               _pack([given[pre + n] for n in replicated])])[None] for pre in ("", "m_", "v_")]
    results = _adamw([slab], *stacks, name="adamw_small")
    n_own = own.shape[1]
    for pre, res in zip(("grad_", "delta_", "new_m_", "new_v_"), results):
        unpacked = (_unpack(res[0, :n_own], [s for _, s in _SMALL_SHARDED])
                    + _unpack(res[0, n_own:], [s for _, s in _REPLICATED]))
        for n, a in zip(small_sharded + replicated, unpacked):
            out[pre + n] = a

    loss = lax.psum(loss_tile[0, 0], ("x", "y", "c"))
    order = list(_WEIGHT_ORDER)
    return (loss, dres[0][None], *[out["grad_" + n] for n in order], *[out["delta_" + n] for n in order],
            *[out["new_m_" + n] for n in order], *[out["new_v_" + n] for n in order])


_WEIGHT_ORDER = ("g_mix", "w_in", "conv_w", "conv_b", "conv_ln_g", "conv_ln_b", "sgu_ln_g", "sgu_ln_b", "sgu_w", "sgu_b",
                 "g_out", "w_out", "g_ffn", "w_up", "ffn_conv_w", "ffn_conv_b", "w_down", "g_final")
```

```python
import functools
import math

import jax
import jax.numpy as jnp
from jax import lax
from jax.experimental import pallas as pl
from jax.experimental.pallas import tpu as pltpu

F32 = jnp.float32
BF16 = jnp.bfloat16

N_DEV = 8
D_MODEL = 1024
HEAD_DIM = 64
D_CONV = 256
D_SGU = 256
D_SB = 512
D_AB = 2 * D_CONV + 2 * D_SGU
D_QKV = 3 * D_SB
D_IN = D_AB + D_QKV
CONV_K = 31
CONV_HALO = 32
FFN_K = 3
FFN_HALO = 8
D_FF = 2816
CHUNK = 128
EPS = 1e-6
LANES = 128

ADAM_LR = 0.001
ADAM_B1 = 0.9
ADAM_B2 = 0.999
ADAM_EPS = 1e-08
ADAM_WD = 0.01
ADAM_STEP = 10

VMEM_LIMIT = 56 * 1024 * 1024


def _cparams(sem=None):
    return pltpu.CompilerParams(dimension_semantics=sem, vmem_limit_bytes=VMEM_LIMIT)


def _tile(n, prefs=(512, 256, 128)):
    for t in prefs:
        if n % t == 0:
            return t
    return n


def _sigmoid(x):
    return 1.0 / (1.0 + jnp.exp(-x))


def _softplus(x):
    return jnp.maximum(x, 0.0) + jnp.log1p(jnp.exp(-jnp.abs(x)))


_INV_SQRT2 = 1.0 / math.sqrt(2.0)
_INV_SQRT2PI = 1.0 / math.sqrt(2.0 * math.pi)


def _gelu(x):
    return 0.5 * x * (1.0 + lax.erf(x * _INV_SQRT2))


def _gelu_grad(x):
    return 0.5 * (1.0 + lax.erf(x * _INV_SQRT2)) + x * jnp.exp(-0.5 * x * x) * _INV_SQRT2PI


def _dot(a, b, dims):
    return lax.dot_general(a, b, (dims, ((), ())), preferred_element_type=F32)


_NN = ((1,), (0,))
_NT = ((1,), (1,))
_TN = ((0,), (0,))


def _split_bf16(x):
    hi = x.astype(BF16)
    lo = (x - hi.astype(F32)).astype(BF16)
    return jnp.concatenate([hi, lo], axis=1)


def _matmul(a, b, mode, *, name, out_dtype=F32, residual=None, n=None, b_n0=0):
    if mode == "nn":
        (m, k), n = a.shape, (n or b.shape[1])
    elif mode == "nt":
        (m, k), n = a.shape, (n or b.shape[0])
    else:
        (k, m), n = a.shape, b.shape[1]
    has_res = residual is not None
    tm, tn = _matmul_tiles(m, n, k, a.dtype.itemsize, b.dtype.itemsize, jnp.dtype(out_dtype).itemsize, has_res, b_n0)
    j0 = b_n0 // tn

    if mode == "nn":
        a_spec = pl.BlockSpec((tm, k), lambda i, j: (i, 0))
        b_spec = pl.BlockSpec((k, tn), lambda i, j: (0, j + j0))
        dims = _NN
    elif mode == "nt":
        a_spec = pl.BlockSpec((tm, k), lambda i, j: (i, 0))
        b_spec = pl.BlockSpec((tn, k), lambda i, j: (j + j0, 0))
        dims = _NT
    else:
        a_spec = pl.BlockSpec((k, tm), lambda i, j: (0, i))
        b_spec = pl.BlockSpec((k, tn), lambda i, j: (0, j))
        dims = _TN
    o_spec = pl.BlockSpec((tm, tn), lambda i, j: (i, j))

    def body(*refs):
        a_ref, b_ref = refs[:2]
        acc = _dot(a_ref[...].astype(BF16), b_ref[...].astype(BF16), dims)
        if has_res:
            acc = acc + refs[2][...]
        refs[-1][...] = acc.astype(out_dtype)

    in_specs = [a_spec, b_spec] + ([o_spec] if has_res else [])
    args = (a, b) + ((residual,) if has_res else ())
    return pl.pallas_call(
        body,
        name=name,
        grid=(m // tm, n // tn),
        in_specs=in_specs,
        out_specs=o_spec,
        out_shape=jax.ShapeDtypeStruct((m, n), out_dtype),
        compiler_params=_cparams(("parallel", "parallel")),
    )(*args)


MATMUL_VMEM_BUDGET = 40 * 1024 * 1024


def _matmul_tiles(m, n, k, a_bytes, b_bytes, out_bytes, has_res, n_offset):
    def divisors(size, cap, also=0):
        return [t for t in range(cap, 0, -LANES) if size % t == 0 and also % t == 0] or [size]

    for tm in divisors(m, 1024):
        for tn in divisors(n, 1408, n_offset):
            blocks = tm * k * a_bytes + k * tn * b_bytes + tm * tn * (out_bytes + (4 if has_res else 0))
            if 2 * blocks <= MATMUL_VMEM_BUDGET:
                return tm, tn
    raise ValueError(f"no matmul tiling for {m} x {n} x {k}")


ROW_TILE = 256


def _rmsnorm_fwd(x, g, *, name):
    s, d = x.shape

    def body(x_ref, g_ref, h_ref):
        xv = x_ref[...]
        r = lax.rsqrt(jnp.mean(xv * xv, axis=-1, keepdims=True) + EPS)
        h_ref[...] = (xv * r * g_ref[...]).astype(BF16)

    return pl.pallas_call(
        body,
        name=name,
        grid=(s // ROW_TILE,),
        in_specs=[pl.BlockSpec((ROW_TILE, d), lambda i: (i, 0)), pl.BlockSpec((1, d), lambda i: (0, 0))],
        out_specs=pl.BlockSpec((ROW_TILE, d), lambda i: (i, 0)),
        out_shape=jax.ShapeDtypeStruct((s, d), BF16),
        compiler_params=_cparams(("parallel",)),
    )(x, g)


def _rmsnorm_bwd(x, g, dh, dres, *, name):
    s, d = x.shape

    def body(x_ref, g_ref, dh_ref, dres_ref, dx_ref, dxb_ref, dg_ref):
        xv = x_ref[...]
        r = lax.rsqrt(jnp.mean(xv * xv, axis=-1, keepdims=True) + EPS)
        xhat = xv * r
        dhv = dh_ref[...]
        dxhat = dhv * g_ref[...]
        dx = dres_ref[...] + r * (dxhat - xhat * jnp.mean(dxhat * xhat, axis=-1, keepdims=True))
        dx_ref[...] = dx
        dxb_ref[...] = dx.astype(BF16)
        part = jnp.sum(dhv * xhat, axis=0, keepdims=True)

        @pl.when(pl.program_id(0) == 0)
        def _():
            dg_ref[...] = part

        @pl.when(pl.program_id(0) > 0)
        def _():
            dg_ref[...] += part

    row = pl.BlockSpec((ROW_TILE, d), lambda i: (i, 0))
    vec = pl.BlockSpec((1, d), lambda i: (0, 0))
    return pl.pallas_call(
        body,
        name=name,
        grid=(s // ROW_TILE,),
        in_specs=[row, vec, row, row],
        out_specs=[row, row, vec],
        out_shape=[jax.ShapeDtypeStruct((s, d), F32), jax.ShapeDtypeStruct((s, d), BF16),
                   jax.ShapeDtypeStruct((1, d), F32)],
        compiler_params=_cparams(("arbitrary",)),
    )(x, g, dh, dres)


def _loss_head(x, g, target, *, name):
    s, d = x.shape

    def body(x_ref, g_ref, t_ref, loss_ref, dx_ref, dxb_ref, dg_ref):
        xv = x_ref[...]
        gv = g_ref[...]
        r = lax.rsqrt(jnp.mean(xv * xv, axis=-1, keepdims=True) + EPS)
        xhat = xv * r
        diff = xhat * gv - t_ref[...]
        dy = diff * (1.0 / d)
        dxhat = dy * gv
        dx = r * (dxhat - xhat * jnp.mean(dxhat * xhat, axis=-1, keepdims=True))
        dx_ref[...] = dx
        dxb_ref[...] = dx.astype(BF16)
        dg_part = jnp.sum(dy * xhat, axis=0, keepdims=True)
        row_loss = jnp.sum(diff * diff, axis=-1, keepdims=True)
        loss_part = jnp.sum(row_loss, axis=0, keepdims=True) * (0.5 / d)

        @pl.when(pl.program_id(0) == 0)
        def _():
            dg_ref[...] = dg_part
            loss_ref[...] = jnp.broadcast_to(loss_part, loss_ref.shape)

        @pl.when(pl.program_id(0) > 0)
        def _():
            dg_ref[...] += dg_part
            loss_ref[...] += jnp.broadcast_to(loss_part, loss_ref.shape)

    row = pl.BlockSpec((ROW_TILE, d), lambda i: (i, 0))
    vec = pl.BlockSpec((1, d), lambda i: (0, 0))
    tile = pl.BlockSpec((8, LANES), lambda i: (0, 0))
    return pl.pallas_call(
        body,
        name=name,
        grid=(s // ROW_TILE,),
        in_specs=[row, vec, row],
        out_specs=[tile, row, row, vec],
        out_shape=[jax.ShapeDtypeStruct((8, LANES), F32), jax.ShapeDtypeStruct((s, d), F32),
                   jax.ShapeDtypeStruct((s, d), BF16), jax.ShapeDtypeStruct((1, d), F32)],
        compiler_params=_cparams(("arbitrary",)),
    )(x, g, target)


_BRANCHES = ((0, D_CONV), (D_CONV, D_SGU), (D_CONV + D_SGU, D_SB))


def _combine_fwd(ya, yb, yc, g, *, name):
    s = ya.shape[0]

    def body(ya_ref, yb_ref, yc_ref, g_ref, y_ref):
        for ref, (off, w) in zip((ya_ref, yb_ref, yc_ref), _BRANCHES):
            v = ref[...]
            r = lax.rsqrt(jnp.mean(v * v, axis=-1, keepdims=True) + EPS)
            y_ref[:, off:off + w] = (v * r * g_ref[:, off:off + w]).astype(BF16)

    def row(w):
        return pl.BlockSpec((ROW_TILE, w), lambda i: (i, 0))

    return pl.pallas_call(
        body,
        name=name,
        grid=(s // ROW_TILE,),
        in_specs=[row(D_CONV), row(D_SGU), row(D_SB), pl.BlockSpec((1, D_MODEL), lambda i: (0, 0))],
        out_specs=row(D_MODEL),
        out_shape=jax.ShapeDtypeStruct((s, D_MODEL), BF16),
        compiler_params=_cparams(("parallel",)),
    )(ya, yb, yc, g)


def _combine_bwd(dy, ya, yb, yc, g, *, name):
    s = ya.shape[0]

    def body(dy_ref, ya_ref, yb_ref, yc_ref, g_ref, dya_ref, dyb_ref, dyc_ref, dg_ref):
        first = pl.program_id(0) == 0
        for ref, dref, (off, w) in zip((ya_ref, yb_ref, yc_ref), (dya_ref, dyb_ref, dyc_ref), _BRANCHES):
            v = ref[...]
            r = lax.rsqrt(jnp.mean(v * v, axis=-1, keepdims=True) + EPS)
            n = v * r
            dout = dy_ref[:, off:off + w]
            dn = dout * g_ref[:, off:off + w]
            dref[...] = r * (dn - n * jnp.mean(dn * n, axis=-1, keepdims=True))
            part = jnp.sum(dout * n, axis=0, keepdims=True)

            @pl.when(first)
            def _():
                dg_ref[:, off:off + w] = part

            @pl.when(jnp.logical_not(first))
            def _():
                dg_ref[:, off:off + w] += part

    def row(w):
        return pl.BlockSpec((ROW_TILE, w), lambda i: (i, 0))

    vec = pl.BlockSpec((1, D_MODEL), lambda i: (0, 0))
    return pl.pallas_call(
        body,
        name=name,
        grid=(s // ROW_TILE,),
        in_specs=[row(D_MODEL), row(D_CONV), row(D_SGU), row(D_SB), vec],
        out_specs=[row(D_CONV), row(D_SGU), row(D_SB), vec],
        out_shape=[jax.ShapeDtypeStruct((s, D_CONV), F32), jax.ShapeDtypeStruct((s, D_SGU), F32),
                   jax.ShapeDtypeStruct((s, D_SB), F32), jax.ShapeDtypeStruct((1, D_MODEL), F32)],
        compiler_params=_cparams(("arbitrary",)),
    )(dy, ya, yb, yc, g)


CONV_TILE = 128


def _shift_down(window, j, halo):
    return pltpu.roll(window, j, 0)[halo:, :] if j else window[halo:, :]


def _shift_up(window, j, n_out):
    n = window.shape[0]
    return pltpu.roll(window, n - j, 0)[:n_out, :] if j else window[:n_out, :]


def _mixer_a_fwd(p_ab, conv_w, conv_b, ln_g, ln_b, *, name):
    s = p_ab.shape[0]
    nt = s // CONV_TILE

    def body(p_ref, w_ref, b_ref, g_ref, beta_ref, y_ref, h_ref):
        h_ref[0:CONV_HALO, :] = jnp.zeros((CONV_HALO, D_CONV), F32)

        def glu(i, c):
            t0 = pl.multiple_of(i * CONV_TILE, CONV_TILE)
            a = p_ref[pl.ds(t0, CONV_TILE), 0:D_CONV]
            gate = p_ref[pl.ds(t0, CONV_TILE), D_CONV:2 * D_CONV]
            h_ref[pl.ds(t0 + CONV_HALO, CONV_TILE), :] = a * _sigmoid(gate)
            return c

        lax.fori_loop(0, nt, glu, 0)

        def conv(i, c):
            t0 = pl.multiple_of(i * CONV_TILE, CONV_TILE)
            window = h_ref[pl.ds(t0, CONV_TILE + CONV_HALO), :]
            acc = jnp.zeros((CONV_TILE, D_CONV), F32) + b_ref[...]
            for k in range(CONV_K):
                acc = acc + w_ref[k:k + 1, :] * _shift_down(window, CONV_K - 1 - k, CONV_HALO)
            mu = jnp.mean(acc, axis=-1, keepdims=True)
            xc = acc - mu
            rstd = lax.rsqrt(jnp.mean(xc * xc, axis=-1, keepdims=True) + EPS)
            z = xc * rstd * g_ref[...] + beta_ref[...]
            y_ref[pl.ds(t0, CONV_TILE), :] = z * _sigmoid(z)
            return c

        lax.fori_loop(0, nt, conv, 0)

    full = lambda shape: pl.BlockSpec(shape, lambda i: (0, 0))
    return pl.pallas_call(
        body,
        name=name,
        grid=(1,),
        in_specs=[full((s, 2 * D_CONV)), full((CONV_K, D_CONV)), full((1, D_CONV)), full((1, D_CONV)),
                  full((1, D_CONV))],
        out_specs=full((s, D_CONV)),
        out_shape=jax.ShapeDtypeStruct((s, D_CONV), F32),
        scratch_shapes=[pltpu.VMEM((s + CONV_HALO, D_CONV), F32)],
        compiler_params=_cparams(("arbitrary",)),
    )(p_ab, conv_w, conv_b, ln_g, ln_b)


def _mixer_a_bwd(p_ab, dya, conv_w, conv_b, ln_g, ln_b, *, name):
    s = p_ab.shape[0]
    nt = s // CONV_TILE

    def body(p_ref, dy_ref, w_ref, b_ref, g_ref, beta_ref, dp_ref, dw_ref, db_ref, dg_ref, dbeta_ref, h_ref, dc_ref):
        h_ref[0:CONV_HALO, :] = jnp.zeros((CONV_HALO, D_CONV), F32)
        dc_ref[s:s + CONV_HALO, :] = jnp.zeros((CONV_HALO, D_CONV), F32)
        dw_ref[...] = jnp.zeros_like(dw_ref)
        db_ref[...] = jnp.zeros_like(db_ref)
        dg_ref[...] = jnp.zeros_like(dg_ref)
        dbeta_ref[...] = jnp.zeros_like(dbeta_ref)

        def glu(i, c):
            t0 = pl.multiple_of(i * CONV_TILE, CONV_TILE)
            a = p_ref[pl.ds(t0, CONV_TILE), 0:D_CONV]
            gate = p_ref[pl.ds(t0, CONV_TILE), D_CONV:2 * D_CONV]
            h_ref[pl.ds(t0 + CONV_HALO, CONV_TILE), :] = a * _sigmoid(gate)
            return c

        lax.fori_loop(0, nt, glu, 0)

        def conv_bwd(i, c):
            t0 = pl.multiple_of(i * CONV_TILE, CONV_TILE)
            window = h_ref[pl.ds(t0, CONV_TILE + CONV_HALO), :]
            taps = [_shift_down(window, CONV_K - 1 - k, CONV_HALO) for k in range(CONV_K)]
            acc = jnp.zeros((CONV_TILE, D_CONV), F32) + b_ref[...]
            for k in range(CONV_K):
                acc = acc + w_ref[k:k + 1, :] * taps[k]
            mu = jnp.mean(acc, axis=-1, keepdims=True)
            xc = acc - mu
            rstd = lax.rsqrt(jnp.mean(xc * xc, axis=-1, keepdims=True) + EPS)
            xhat = xc * rstd
            z = xhat * g_ref[...] + beta_ref[...]
            sg = _sigmoid(z)
            dz = dy_ref[pl.ds(t0, CONV_TILE), :] * (sg * (1.0 + z * (1.0 - sg)))
            dg_ref[...] += jnp.sum(dz * xhat, axis=0, keepdims=True)
            dbeta_ref[...] += jnp.sum(dz, axis=0, keepdims=True)
            dxhat = dz * g_ref[...]
            dc = rstd * (dxhat - jnp.mean(dxhat, axis=-1, keepdims=True)
                         - xhat * jnp.mean(dxhat * xhat, axis=-1, keepdims=True))
            dc_ref[pl.ds(t0, CONV_TILE), :] = dc
            db_ref[...] += jnp.sum(dc, axis=0, keepdims=True)
            for k in range(CONV_K):
                dw_ref[k:k + 1, :] += jnp.sum(dc * taps[k], axis=0, keepdims=True)
            return c

        lax.fori_loop(0, nt, conv_bwd, 0)

        def glu_bwd(i, c):
            t0 = pl.multiple_of(i * CONV_TILE, CONV_TILE)
            window = dc_ref[pl.ds(t0, CONV_TILE + CONV_HALO), :]
            dh = jnp.zeros((CONV_TILE, D_CONV), F32)
            for j in range(CONV_K):
                dh = dh + w_ref[CONV_K - 1 - j:CONV_K - j, :] * _shift_up(window, j, CONV_TILE)
            a = p_ref[pl.ds(t0, CONV_TILE), 0:D_CONV]
            sg = _sigmoid(p_ref[pl.ds(t0, CONV_TILE), D_CONV:2 * D_CONV])
            dp_ref[pl.ds(t0, CONV_TILE), 0:D_CONV] = dh * sg
            dp_ref[pl.ds(t0, CONV_TILE), D_CONV:2 * D_CONV] = dh * a * sg * (1.0 - sg)
            return c

        lax.fori_loop(0, nt, glu_bwd, 0)

    full = lambda shape: pl.BlockSpec(shape, lambda i: (0, 0))
    vec = jax.ShapeDtypeStruct((1, D_CONV), F32)
    return pl.pallas_call(
        body,
        name=name,
        grid=(1,),
        in_specs=[full((s, 2 * D_CONV)), full((s, D_CONV)), full((CONV_K, D_CONV)), full((1, D_CONV)),
                  full((1, D_CONV)), full((1, D_CONV))],
        out_specs=[full((s, 2 * D_CONV)), full((CONV_K, D_CONV)), full((1, D_CONV)), full((1, D_CONV)),
                   full((1, D_CONV))],
        out_shape=[jax.ShapeDtypeStruct((s, 2 * D_CONV), F32), jax.ShapeDtypeStruct((CONV_K, D_CONV), F32),
                   vec, vec, vec],
        scratch_shapes=[pltpu.VMEM((s + CONV_HALO, D_CONV), F32), pltpu.VMEM((s + CONV_HALO, D_CONV), F32)],
        compiler_params=_cparams(("arbitrary",)),
    )(p_ab, dya, conv_w, conv_b, ln_g, ln_b)


N_SGU_HEADS = D_SGU // HEAD_DIM


def _head_masks(width):
    lane = lax.broadcasted_iota(jnp.int32, (1, width), 1)
    return [(lane >= h * HEAD_DIM) & (lane < (h + 1) * HEAD_DIM) for h in range(width // HEAD_DIM)]


def _tril_mask():
    r = lax.broadcasted_iota(jnp.int32, (CHUNK, CHUNK), 0)
    c = lax.broadcasted_iota(jnp.int32, (CHUNK, CHUNK), 1)
    return c <= r


def _sgu_norm(bv, g, beta):
    vg = _gelu(bv)
    mu = jnp.mean(vg, axis=-1, keepdims=True)
    xc = vg - mu
    rstd = lax.rsqrt(jnp.mean(xc * xc, axis=-1, keepdims=True) + EPS)
    xhat = xc * rstd
    return xhat, rstd, xhat * g + beta


def _sgu_fwd(p_ab, ln_g, ln_b, w_s, bias, *, name):
    s = p_ab.shape[0]

    def body(p_ref, g_ref, beta_ref, w_ref, bias_ref, y_ref):
        u = _gelu(p_ref[:, 0:D_SGU])
        _, _, vn = _sgu_norm(p_ref[:, D_SGU:2 * D_SGU], g_ref[...], beta_ref[...])
        vb = vn.astype(BF16)
        tril = _tril_mask()
        mixed = bias_ref[...]
        for h, m in enumerate(_head_masks(D_SGU)):
            wh = jnp.where(tril, w_ref[h], 0.0).astype(BF16)
            mixed = mixed + _dot(wh, jnp.where(m, vb, jnp.zeros_like(vb)), _NN)
        y_ref[...] = u * mixed

    return pl.pallas_call(
        body,
        name=name,
        grid=(s // CHUNK,),
        in_specs=[pl.BlockSpec((CHUNK, 2 * D_SGU), lambda i: (i, 1)),
                  pl.BlockSpec((1, D_SGU), lambda i: (0, 0)), pl.BlockSpec((1, D_SGU), lambda i: (0, 0)),
                  pl.BlockSpec((N_SGU_HEADS, CHUNK, CHUNK), lambda i: (0, 0, 0)),
                  pl.BlockSpec((CHUNK, D_SGU), lambda i: (0, 0))],
        out_specs=pl.BlockSpec((CHUNK, D_SGU), lambda i: (i, 0)),
        out_shape=jax.ShapeDtypeStruct((s, D_SGU), F32),
        compiler_params=_cparams(("parallel",)),
    )(p_ab, ln_g, ln_b, w_s, bias)


def _sgu_bwd(p_ab, dyb, ln_g, ln_b, w_s, bias, *, name):
    s = p_ab.shape[0]
    n_chunks = s // CHUNK

    def body(p_ref, dy_ref, g_ref, beta_ref, w_ref, bias_ref, dp_ref, dw_ref, db_ref, dg_ref, dbeta_ref, dbias_ref):
        @pl.when(pl.program_id(0) == 0)
        def _():
            dw_ref[...] = jnp.zeros_like(dw_ref)
            dbias_ref[...] = jnp.zeros_like(dbias_ref)
            dg_ref[...] = jnp.zeros_like(dg_ref)
            dbeta_ref[...] = jnp.zeros_like(dbeta_ref)

        bu = p_ref[:, 0:D_SGU]
        bv = p_ref[:, D_SGU:2 * D_SGU]
        u = _gelu(bu)
        gv = g_ref[...]
        xhat, rstd, vn = _sgu_norm(bv, gv, beta_ref[...])
        vb = vn.astype(BF16)
        tril = _tril_mask()
        masks = _head_masks(D_SGU)
        whs = [jnp.where(tril, w_ref[h], 0.0).astype(BF16) for h in range(N_SGU_HEADS)]
        mixed = bias_ref[...]
        for h, m in enumerate(masks):
            mixed = mixed + _dot(whs[h], jnp.where(m, vb, jnp.zeros_like(vb)), _NN)
        dy = dy_ref[...]
        dp_ref[:, 0:D_SGU] = dy * mixed * _gelu_grad(bu)
        dmixed = dy * u
        dbias_ref[...] += dmixed
        dmb = dmixed.astype(BF16)
        dvn = jnp.zeros((CHUNK, D_SGU), F32)
        for h, m in enumerate(masks):
            dmh = jnp.where(m, dmb, jnp.zeros_like(dmb))
            dvn = dvn + _dot(whs[h], dmh, _TN)
            dw_ref[h] += jnp.where(tril, _dot(dmh, vb, _NT), 0.0)
        dg_ref[...] += jnp.sum(dvn * xhat, axis=0, keepdims=True)
        dbeta_ref[...] += jnp.sum(dvn, axis=0, keepdims=True)
        dxhat = dvn * gv
        dvg = rstd * (dxhat - jnp.mean(dxhat, axis=-1, keepdims=True)
                      - xhat * jnp.mean(dxhat * xhat, axis=-1, keepdims=True))
        dp_ref[:, D_SGU:2 * D_SGU] = dvg * _gelu_grad(bv)

        @pl.when(pl.program_id(0) == n_chunks - 1)
        def _():
            chan = lax.broadcasted_iota(jnp.int32, (D_SGU, LANES), 0)
            head = lax.broadcasted_iota(jnp.int32, (D_SGU, LANES), 1)
            to_head = jnp.where(chan // HEAD_DIM == head, 1.0, 0.0).astype(BF16)
            db_ref[...] = _dot(_split_bf16(dbias_ref[...]), jnp.concatenate([to_head, to_head], axis=0), _NN)

    vec = pl.BlockSpec((1, D_SGU), lambda i: (0, 0))
    wspec = pl.BlockSpec((N_SGU_HEADS, CHUNK, CHUNK), lambda i: (0, 0, 0))
    bspec = pl.BlockSpec((CHUNK, D_SGU), lambda i: (0, 0))
    return pl.pallas_call(
        body,
        name=name,
        grid=(n_chunks,),
        in_specs=[pl.BlockSpec((CHUNK, 2 * D_SGU), lambda i: (i, 1)), pl.BlockSpec((CHUNK, D_SGU), lambda i: (i, 0)),
                  vec, vec, wspec, bspec],
        out_specs=[pl.BlockSpec((CHUNK, 2 * D_SGU), lambda i: (i, 0)), wspec,
                   pl.BlockSpec((CHUNK, LANES), lambda i: (0, 0)), vec, vec],
        out_shape=[jax.ShapeDtypeStruct((s, 2 * D_SGU), F32),
                   jax.ShapeDtypeStruct((N_SGU_HEADS, CHUNK, CHUNK), F32),
                   jax.ShapeDtypeStruct((CHUNK, LANES), F32),
                   jax.ShapeDtypeStruct((1, D_SGU), F32), jax.ShapeDtypeStruct((1, D_SGU), F32)],
        scratch_shapes=[pltpu.VMEM((CHUNK, D_SGU), F32)],
        compiler_params=_cparams(("arbitrary",)),
    )(p_ab, dyb, ln_g, ln_b, w_s, bias)


N_PAIRS = D_SB // LANES
SB_SCALE = HEAD_DIM ** -0.5


def _sb_logits(z, valid):
    nz = -z
    t = jnp.log(1.0 + jnp.exp(jnp.minimum(z, nz)))
    l1 = jnp.minimum(nz, 0.0) - t
    if valid is not None:
        l1 = jnp.where(valid, l1, 0.0)
    return l1, jnp.minimum(z, 0.0) - t


def _split_hi_lo(x):
    hi = lax.bitcast_convert_type(lax.bitcast_convert_type(x, jnp.uint32) & jnp.uint32(0xFFFF0000), F32)
    return jnp.concatenate([hi, x - hi], axis=1)


def _cumsum_operand(keep):
    half = jnp.concatenate([keep.astype(F32), jnp.ones((CHUNK, CHUNK), F32)], axis=1)
    return jnp.concatenate([half, half], axis=0)


def _attn_fwd(qkv, *, name, exchanges=None):
    s = qkv.shape[0]
    nq = s // CHUNK
    ex = exchanges or _Exchanges([])
    n_ex = len(ex.arrays)

    def body(*refs):
        q_ref, k_ref, v_ref = refs[:3]
        o_ref, t_ref = refs[3 + n_ex:5 + n_ex]
        ex_refs = (refs[3:3 + n_ex], refs[5 + n_ex:5 + 2 * n_ex]) + refs[5 + 2 * n_ex:]
        qi = pl.program_id(1)
        if n_ex:
            @pl.when((pl.program_id(0) == 0) & (qi == 0))
            def _():
                ex.start(*ex_refs)

        q = q_ref[...] * SB_SCALE
        zero = jnp.zeros_like(q)
        masks = _head_masks(LANES)
        qs = [jnp.where(m, q, zero) for m in masks]
        row = lax.broadcasted_iota(jnp.int32, (CHUNK, CHUNK), 0)
        col = lax.broadcasted_iota(jnp.int32, (CHUNK, CHUNK), 1)
        after_op = _cumsum_operand(row > col)

        cmr = col - row

        def blocks(js, carry):
            o, c0, c1 = carry
            kvs, valids = [], []
            for j in js:
                k0 = pl.multiple_of(jnp.maximum(j, 0) * CHUNK, CHUNK)
                kvs.append((k_ref[pl.ds(k0, CHUNK), :], v_ref[pl.ds(k0, CHUNK), :]))
                valids.append(cmr < jnp.where(j >= 0, (qi - j) * CHUNK, -CHUNK))
            units = [(h, b) for b in range(len(js)) for h in range(2)]
            zs = [_dot(qs[h], kvs[b][0], _NT) for h, b in units]
            logits = [_sb_logits(z, valids[b]) for z, (h, b) in zip(zs, units)]
            sums = [_dot(_split_hi_lo(l1), after_op, _NN) for l1, _ in logits]
            cs = [c0, c1]
            probs = []
            for (h, b), (_, lb), sm in zip(units, logits, sums):
                probs.append(jnp.where(valids[b], jnp.exp(lb + sm[:, :CHUNK] + cs[h]), 0.0))
                cs[h] = cs[h] + sm[:, CHUNK:]
            for (h, b), a in zip(units, probs):
                o = o + _dot(a.astype(BF16), jnp.where(masks[h], kvs[b][1], zero), _NN)
            return o, cs[0], cs[1]

        zc = jnp.zeros((CHUNK, LANES), F32)
        n_four = (qi + 1) // 4
        carry = lax.fori_loop(0, n_four, lambda jj, c: blocks([qi - 4 * jj - i for i in range(4)], c), (zc,) * 3)
        top = qi - 4 * n_four
        o, c0, c1 = lax.fori_loop(0, (top + 2) // 2, lambda jj, c: blocks([top - 2 * jj, top - 2 * jj - 1], c), carry)
        o_ref[...] = o
        t_ref[:, 0:LANES] = c0
        t_ref[:, LANES:2 * LANES] = c1
        if n_ex:
            @pl.when((pl.program_id(0) == N_PAIRS - 1) & (qi == nq - 1))
            def _():
                ex.wait(*ex_refs)

    return pl.pallas_call(
        body,
        name=name,
        grid=(N_PAIRS, nq),
        in_specs=[pl.BlockSpec((CHUNK, LANES), lambda p, i: (i, p)),
                  pl.BlockSpec((s, LANES), lambda p, i: (0, N_PAIRS + p)),
                  pl.BlockSpec((s, LANES), lambda p, i: (0, 2 * N_PAIRS + p))] + ex.in_specs,
        out_specs=[pl.BlockSpec((CHUNK, LANES), lambda p, i: (i, p)),
                   pl.BlockSpec((CHUNK, 2 * LANES), lambda p, i: (i, p))] + ex.out_specs,
        out_shape=[jax.ShapeDtypeStruct((s, D_SB), F32), jax.ShapeDtypeStruct((s, 2 * D_SB), F32)] + ex.out_shapes,
        scratch_shapes=ex.scratch_shapes if n_ex else [],
        compiler_params=_cparams(("arbitrary", "arbitrary")),
    )(qkv, qkv, qkv, *ex.arrays)


def _attn_bwd(qkv, t_tot, do, *, name, exchanges=None):
    s = qkv.shape[0]
    nq = s // CHUNK
    ex = exchanges or _Exchanges([])
    n_ex = len(ex.arrays)

    def body(*refs):
        q_ref, k_ref, v_ref, t_ref, do_ref = refs[:5]
        dq_ref, dk_ref, dv_ref = refs[5 + n_ex:8 + n_ex]
        ex_refs = (refs[5:5 + n_ex], refs[8 + n_ex:8 + 2 * n_ex]) + refs[8 + 2 * n_ex:]
        qi = pl.program_id(1)
        if n_ex:
            @pl.when((pl.program_id(0) == 0) & (qi == 0))
            def _():
                ex.start(*ex_refs)

        @pl.when(qi == 0)
        def _():
            dk_ref[...] = jnp.zeros_like(dk_ref)
            dv_ref[...] = jnp.zeros_like(dv_ref)

        q = q_ref[...] * SB_SCALE
        dob = do_ref[...].astype(BF16)
        zero = jnp.zeros_like(q)
        masks = _head_masks(LANES)
        qs = [jnp.where(m, q, zero) for m in masks]
        dos = [jnp.where(m, dob, zero) for m in masks]
        tots = [t_ref[:, 0:LANES], t_ref[:, LANES:2 * LANES]]
        row = lax.broadcasted_iota(jnp.int32, (CHUNK, CHUNK), 0)
        col = lax.broadcasted_iota(jnp.int32, (CHUNK, CHUNK), 1)
        upto_op = _cumsum_operand(row <= col)
        before_op = _cumsum_operand(row < col)

        cmr = col - row

        def blocks(js, carry):
            dq, cl0, cl1, cp0, cp1 = carry
            starts = [pl.multiple_of(jnp.minimum(j, nq - 1) * CHUNK, CHUNK) for j in js]
            valids = [cmr < (qi - j) * CHUNK for j in js]
            kvs = [(k_ref[pl.ds(k0, CHUNK), :], v_ref[pl.ds(k0, CHUNK), :]) for k0 in starts]
            units = [(h, b) for b in range(len(js)) for h in range(2)]
            zs = [_dot(qs[h], kvs[b][0], _NT) for h, b in units]
            das = [_dot(dos[h], kvs[b][1], _NT) for h, b in units]
            logits = [_sb_logits(z, valids[b]) for z, (h, b) in zip(zs, units)]
            sums = [_dot(_split_hi_lo(l1), upto_op, _NN) for l1, _ in logits]
            cls, cps = [cl0, cl1], [cp0, cp1]
            probs, gs = [], []
            for (h, b), (_, lb), sm, da in zip(units, logits, sums, das):
                a = jnp.where(valids[b], jnp.exp(lb + (tots[h] - cls[h] - sm[:, :CHUNK])), 0.0)
                probs.append(a)
                gs.append(a * da)
                cls[h] = cls[h] + sm[:, CHUNK:]
            sums_g = [_dot(_split_hi_lo(g), before_op, _NN) for g in gs]
            dzs = []
            for (h, b), (_, lb), g, sg in zip(units, logits, gs, sums_g):
                dz = g - (g + sg[:, :CHUNK] + cps[h]) * jnp.exp(lb)
                dzs.append(jnp.where(valids[b], dz, 0.0).astype(BF16))
                cps[h] = cps[h] + sg[:, CHUNK:]
            for (h, b), dzb in zip(units, dzs):
                dq = dq + _dot(dzb, jnp.where(masks[h], kvs[b][0], zero), _NN)
            for b, k0 in enumerate(starts):
                dk_ref[pl.ds(k0, CHUNK), :] += _dot(dzs[2 * b], qs[0], _TN) + _dot(dzs[2 * b + 1], qs[1], _TN)
                dv_ref[pl.ds(k0, CHUNK), :] += (_dot(probs[2 * b].astype(BF16), dos[0], _TN)
                                                + _dot(probs[2 * b + 1].astype(BF16), dos[1], _TN))
            return dq, cls[0], cls[1], cps[0], cps[1]

        zc = jnp.zeros((CHUNK, LANES), F32)
        n_four = (qi + 1) // 4
        carry = lax.fori_loop(0, n_four, lambda jj, c: blocks([4 * jj + i for i in range(4)], c), (zc,) * 5)
        base = 4 * n_four
        carry = lax.fori_loop(0, (qi - base + 2) // 2, lambda jj, c: blocks([base + 2 * jj, base + 2 * jj + 1], c), carry)
        dq_ref[...] = carry[0] * SB_SCALE
        if n_ex:
            @pl.when((pl.program_id(0) == N_PAIRS - 1) & (qi == nq - 1))
            def _():
                ex.wait(*ex_refs)

    blk = pl.BlockSpec((CHUNK, LANES), lambda p, i: (i, p))
    col_blk = pl.BlockSpec((s, LANES), lambda p, i: (0, p))
    out = jax.ShapeDtypeStruct((s, D_SB), F32)
    return pl.pallas_call(
        body,
        name=name,
        grid=(N_PAIRS, nq),
        in_specs=[blk,
                  pl.BlockSpec((s, LANES), lambda p, i: (0, N_PAIRS + p)),
                  pl.BlockSpec((s, LANES), lambda p, i: (0, 2 * N_PAIRS + p)),
                  pl.BlockSpec((CHUNK, 2 * LANES), lambda p, i: (i, p)),
                  blk] + ex.in_specs,
        out_specs=[blk, col_blk, col_blk] + ex.out_specs,
        out_shape=[out, out, out] + ex.out_shapes,
        scratch_shapes=ex.scratch_shapes if n_ex else [],
        compiler_params=_cparams(("arbitrary", "arbitrary")),
    )(qkv, qkv, qkv, t_tot, do, *ex.arrays)


FFN_TILE = 256
FFN_COLS = 256
N_FF_BLOCKS = D_FF // FFN_COLS


def _ffn_act_fwd(up0, conv_w, conv_b, *, name):
    s = up0.shape[0]
    nt = s // FFN_TILE

    def body(xg_ref, xv_ref, wg_ref, wv_ref, bg_ref, bv_ref, act_ref, pg_ref, pv_ref):
        pg_ref[0:FFN_HALO, :] = jnp.zeros((FFN_HALO, FFN_COLS), F32)
        pv_ref[0:FFN_HALO, :] = jnp.zeros((FFN_HALO, FFN_COLS), F32)
        pg_ref[FFN_HALO:, :] = xg_ref[...]
        pv_ref[FFN_HALO:, :] = xv_ref[...]

        def tile(i, c):
            t0 = pl.multiple_of(i * FFN_TILE, FFN_TILE)
            outs = []
            for p_ref, w_ref, b_ref in ((pg_ref, wg_ref, bg_ref), (pv_ref, wv_ref, bv_ref)):
                window = p_ref[pl.ds(t0, FFN_TILE + FFN_HALO), :]
                acc = b_ref[...] + w_ref[2:3, :] * window[FFN_HALO:, :]
                for j in range(1, FFN_K):
                    acc = acc + w_ref[FFN_K - 1 - j:FFN_K - j, :] * _shift_down(window, j, FFN_HALO)
                outs.append(acc)
            gate, val = outs
            act_ref[pl.ds(t0, FFN_TILE), :] = (gate * _sigmoid(gate) * val).astype(BF16)
            return c

        lax.fori_loop(0, nt, tile, 0)

    gcol = lambda rows: pl.BlockSpec((rows, FFN_COLS), lambda j: (0, j))
    vcol = lambda rows: pl.BlockSpec((rows, FFN_COLS), lambda j: (0, j + N_FF_BLOCKS))
    return pl.pallas_call(
        body,
        name=name,
        grid=(N_FF_BLOCKS,),
        in_specs=[gcol(s), vcol(s), gcol(FFN_K), vcol(FFN_K), gcol(1), vcol(1)],
        out_specs=gcol(s),
        out_shape=jax.ShapeDtypeStruct((s, D_FF), BF16),
        scratch_shapes=[pltpu.VMEM((s + FFN_HALO, FFN_COLS), F32), pltpu.VMEM((s + FFN_HALO, FFN_COLS), F32)],
        compiler_params=_cparams(("parallel",)),
    )(up0, up0, conv_w, conv_w, conv_b, conv_b)


def _ffn_act_bwd(up0, dact, conv_w, conv_b, *, name):
    s = up0.shape[0]
    nt = s // FFN_TILE

    def body(xm_ref, xp_ref, da_ref, wm_ref, wp_ref, bm_ref, bp_ref, dx_ref, dw_ref, db_ref, pm_ref, pp_ref, dm_ref):
        is_gate = pl.program_id(0) < N_FF_BLOCKS
        zeros = jnp.zeros((FFN_HALO, FFN_COLS), F32)
        pm_ref[0:FFN_HALO, :] = zeros
        pp_ref[0:FFN_HALO, :] = zeros
        pm_ref[FFN_HALO:, :] = xm_ref[...]
        pp_ref[FFN_HALO:, :] = xp_ref[...]
        dm_ref[s:s + FFN_HALO, :] = zeros
        dw_ref[...] = jnp.zeros_like(dw_ref)
        db_ref[...] = jnp.zeros_like(db_ref)

        def tile(i, c):
            t0 = pl.multiple_of(i * FFN_TILE, FFN_TILE)
            wm = pm_ref[pl.ds(t0, FFN_TILE + FFN_HALO), :]
            wp = pp_ref[pl.ds(t0, FFN_TILE + FFN_HALO), :]
            taps = [_shift_down(wm, j, FFN_HALO) for j in range(FFN_K)]
            mine = bm_ref[...]
            partner = bp_ref[...]
            for j in range(FFN_K):
                mine = mine + wm_ref[FFN_K - 1 - j:FFN_K - j, :] * taps[j]
                partner = partner + wp_ref[FFN_K - 1 - j:FFN_K - j, :] * _shift_down(wp, j, FFN_HALO)
            da = da_ref[pl.ds(t0, FFN_TILE), :]
            sg_m = _sigmoid(mine)
            sg_p = _sigmoid(partner)
            d_as_gate = da * partner * (sg_m * (1.0 + mine * (1.0 - sg_m)))
            d_as_val = da * partner * sg_p
            dm = jnp.where(is_gate, d_as_gate, d_as_val)
            dm_ref[pl.ds(t0, FFN_TILE), :] = dm
            db_ref[...] += jnp.sum(dm, axis=0, keepdims=True)
            for j in range(FFN_K):
                dw_ref[FFN_K - 1 - j:FFN_K - j, :] += jnp.sum(dm * taps[j], axis=0, keepdims=True)
            return c

        lax.fori_loop(0, nt, tile, 0)

        def tile_dx(i, c):
            t0 = pl.multiple_of(i * FFN_TILE, FFN_TILE)
            window = dm_ref[pl.ds(t0, FFN_TILE + FFN_HALO), :]
            dx = jnp.zeros((FFN_TILE, FFN_COLS), F32)
            for j in range(FFN_K):
                dx = dx + wm_ref[FFN_K - 1 - j:FFN_K - j, :] * _shift_up(window, j, FFN_TILE)
            dx_ref[pl.ds(t0, FFN_TILE), :] = dx.astype(BF16)
            return c

        lax.fori_loop(0, nt, tile_dx, 0)

    nb = 2 * N_FF_BLOCKS
    mine = lambda rows: pl.BlockSpec((rows, FFN_COLS), lambda j: (0, j))
    partner = lambda rows: pl.BlockSpec((rows, FFN_COLS), lambda j: (0, (j + N_FF_BLOCKS) % nb))
    return pl.pallas_call(
        body,
        name=name,
        grid=(nb,),
        in_specs=[mine(s), partner(s), pl.BlockSpec((s, FFN_COLS), lambda j: (0, j % N_FF_BLOCKS)),
                  mine(FFN_K), partner(FFN_K), mine(1), partner(1)],
        out_specs=[mine(s), mine(FFN_K), mine(1)],
        out_shape=[jax.ShapeDtypeStruct((s, 2 * D_FF), BF16), jax.ShapeDtypeStruct((FFN_K, 2 * D_FF), F32),
                   jax.ShapeDtypeStruct((1, 2 * D_FF), F32)],
        scratch_shapes=[pltpu.VMEM((s + FFN_HALO, FFN_COLS), F32), pltpu.VMEM((s + FFN_HALO, FFN_COLS), F32),
                        pltpu.VMEM((s + FFN_HALO, FFN_COLS), F32)],
        compiler_params=_cparams(("parallel",)),
    )(up0, up0, dact, conv_w, conv_w, conv_b, conv_b)


MESH = pl.DeviceIdType.MESH


def _position():
    x, y, c = lax.axis_index("x"), lax.axis_index("y"), lax.axis_index("c")
    return x, y, c, 4 * x + 2 * y + c


def _peer(k):
    x, y, c, _ = _position()
    px = 1 - x if k & 4 else x
    py = 1 - y if k & 2 else y
    pc = 1 - c if k & 1 else c
    return (px, py, pc), 4 * px + 2 * py + pc


def _exchange(src, *, kind, name):
    ex = _Exchanges([(kind, src)])

    def body(src_ref, out_ref, send_sems, recv_sems, local_sems):
        ex.start([src_ref], [out_ref], send_sems, recv_sems, local_sems)
        ex.wait([src_ref], [out_ref], send_sems, recv_sems, local_sems)

    return pl.pallas_call(
        body,
        name=name,
        in_specs=ex.in_specs,
        out_specs=ex.out_specs[0],
        out_shape=ex.out_shapes[0],
        scratch_shapes=ex.scratch_shapes,
    )(src)


_HBM = pl.BlockSpec(memory_space=pltpu.HBM)
_SEM = pl.BlockSpec(memory_space=pltpu.SEMAPHORE)
_DATAFLOW = pltpu.SideEffectType.DATAFLOW_SIDE_EFFECTING
N_PEERS = N_DEV - 1


class _SplitExchange:
    def __init__(self, src, *, kind, name):
        self.kind, self.name, self.dtype = kind, name, src.dtype
        scatter = kind == "scatter_rows"
        self.scatter = scatter
        self.r = src.shape[0] // N_DEV if scatter else src.shape[0]
        self.cols = src.shape[1]
        self.land_shape = (N_DEV, self.r, self.cols) if scatter else (N_DEV * self.r, self.cols)
        r = self.r

        def copies(src_ref, land_ref, send_sems, recv_sems):
            me = _position()[3]

            def rows(ref, idx):
                return ref.at[pl.ds(pl.multiple_of(idx * r, r), r), :]

            outgoing = (lambda idx: rows(src_ref, idx)) if scatter else (lambda idx: src_ref)
            slot = (lambda idx: land_ref.at[idx]) if scatter else (lambda idx: rows(land_ref, idx))
            sends, recvs = [], []
            for k in range(1, N_DEV):
                peer, pidx = _peer(k)
                sems = dict(send_sem=send_sems[k - 1], recv_sem=recv_sems[k - 1], device_id=peer, device_id_type=MESH)
                sends.append(pltpu.make_async_remote_copy(src_ref=outgoing(pidx), dst_ref=slot(me), **sems))
                recvs.append(pltpu.make_async_remote_copy(src_ref=outgoing(pidx), dst_ref=slot(pidx), **sems))
            return sends, recvs

        self._copies = copies
        self.src = src

    @staticmethod
    def start(exchanges, name):
        n = len(exchanges)
        per = 2 * N_PEERS

        def start_body(*refs):
            outs = refs[2 * n:]
            for i, ex in enumerate(exchanges):
                sems = outs[per * i:per * (i + 1)]
                sends, _ = ex._copies(refs[2 * i], refs[2 * i + 1], sems[:N_PEERS], sems[N_PEERS:])
                for cp in sends:
                    cp.start()
            outs[-1][...] = jnp.zeros_like(outs[-1])

        sem = pltpu.SemaphoreType.DMA(())
        operands, thru_shapes = [], []
        for ex in exchanges:
            operands += [pltpu.with_memory_space_constraint(ex.src, pltpu.HBM),
                         pltpu.with_memory_space_constraint(lax.empty(ex.land_shape, ex.dtype), pltpu.HBM)]
            thru_shapes += [pltpu.HBM(ex.src.shape, ex.dtype), pltpu.HBM(ex.land_shape, ex.dtype)]
        out = pl.pallas_call(
            start_body,
            name=name,
            in_specs=(_HBM,) * (2 * n),
            out_specs=(_SEM,) * (per * n) + (_HBM,) * (2 * n) + (pl.BlockSpec(memory_space=pltpu.VMEM),),
            out_shape=(sem,) * (per * n) + tuple(thru_shapes) + (jax.ShapeDtypeStruct((8, LANES), F32),),
            input_output_aliases={i: per * n + i for i in range(2 * n)},
            compiler_params=pltpu.CompilerParams(has_side_effects=_DATAFLOW),
        )(*operands)
        for i, ex in enumerate(exchanges):
            ex.sems = out[per * i:per * (i + 1)]
            ex.src_thru, ex.land_thru = out[per * n + 2 * i], out[per * n + 2 * i + 1]
        return out[-1][0, 0]

    def finish(self, after):
        copies = self._copies

        def wait_body(src_ref, land_ref, *rest):
            sends, recvs = copies(src_ref, land_ref, rest[:N_PEERS], rest[N_PEERS:2 * N_PEERS])
            for cp in sends:
                cp.wait_send()
            for cp in recvs:
                cp.wait_recv()

        src, got = pl.pallas_call(
            wait_body,
            name=f"{self.name}_wait",
            in_specs=(_HBM, _HBM) + (_SEM,) * (2 * N_PEERS) + (pl.BlockSpec(memory_space=pl.ANY),),
            out_specs=(_HBM, _HBM),
            out_shape=(pltpu.HBM(self.src_thru.shape, self.dtype), pltpu.HBM(self.land_shape, self.dtype)),
            input_output_aliases={0: 0, 1: 1},
            compiler_params=pltpu.CompilerParams(has_side_effects=_DATAFLOW),
        )(self.src_thru, self.land_thru, *self.sems, after)
        x, y, c = lax.axis_index("x"), lax.axis_index("y"), lax.axis_index("c")
        me = 4 * x + 2 * y + c
        if self.scatter:
            own = lax.dynamic_slice(src, (me * self.r, 0), (self.r, self.cols))
            return lax.dynamic_update_slice(got, own[None], (me, 0, 0))
        return lax.dynamic_update_slice(got, src, (me * self.r, 0))


class _Exchanges:
    def __init__(self, items):
        self.kinds = [kind for kind, _ in items]
        self.arrays = [src for _, src in items]
        self.out_shapes = []
        self.block_rows = []
        for kind, src in items:
            scatter, by_rows = kind.startswith("scatter"), kind.endswith("rows")
            if by_rows:
                r = src.shape[0] // N_DEV if scatter else src.shape[0]
                shape = (N_DEV, r, src.shape[1]) if scatter else (N_DEV * r, src.shape[1])
            else:
                r = None
                shape = src.shape if scatter else (N_DEV,) + src.shape
            self.block_rows.append(r)
            self.out_shapes.append(jax.ShapeDtypeStruct(shape, src.dtype))
        n = len(items)
        self.in_specs = [pl.BlockSpec(memory_space=pl.ANY)] * n
        self.out_specs = [pl.BlockSpec(memory_space=pl.ANY)] * n
        self.scratch_shapes = [pltpu.SemaphoreType.DMA((n * (N_DEV - 1),)), pltpu.SemaphoreType.DMA((n * (N_DEV - 1),)),
                               pltpu.SemaphoreType.DMA((n,))]

    def _copies(self, i, src_ref, out_ref, send_sems, recv_sems, local_sems):
        kind, r = self.kinds[i], self.block_rows[i]
        scatter, by_rows = kind.startswith("scatter"), kind.endswith("rows")
        me = _position()[3]

        def rows(ref, idx):
            return ref.at[pl.ds(pl.multiple_of(idx * r, r), r), :]

        def outgoing(idx):
            if not scatter:
                return src_ref
            return rows(src_ref, idx) if by_rows else src_ref.at[idx]

        def slot(idx):
            return rows(out_ref, idx) if (by_rows and not scatter) else out_ref.at[idx]

        local = pltpu.make_async_copy(outgoing(me), slot(me), local_sems.at[i])
        sends, recvs = [], []
        for k in range(1, N_DEV):
            peer, pidx = _peer(k)
            sem = i * (N_DEV - 1) + k - 1
            sends.append(pltpu.make_async_remote_copy(src_ref=outgoing(pidx), dst_ref=slot(me), send_sem=send_sems.at[sem],
                                                      recv_sem=recv_sems.at[sem], device_id=peer, device_id_type=MESH))
            recvs.append(pltpu.make_async_remote_copy(src_ref=outgoing(pidx), dst_ref=slot(pidx), send_sem=send_sems.at[sem],
                                                      recv_sem=recv_sems.at[sem], device_id=peer, device_id_type=MESH))
        return local, sends, recvs

    def start(self, src_refs, out_refs, *sems):
        for i, (src_ref, out_ref) in enumerate(zip(src_refs, out_refs)):
            local, sends, _ = self._copies(i, src_ref, out_ref, *sems)
            local.start()
            for cp in sends:
                cp.start()

    def wait(self, src_refs, out_refs, *sems):
        for i, (src_ref, out_ref) in enumerate(zip(src_refs, out_refs)):
            local, sends, recvs = self._copies(i, src_ref, out_ref, *sems)
            for cp in recvs:
                cp.wait_recv()
            for cp in sends:
                cp.wait_send()
            local.wait()


def _row_tile(rows):
    return _tile(rows, (256, 128, 64, 32, 16, 8))


def _layer_parts_specs(n_layers, n_parts, tr, cols):
    return [pl.BlockSpec((n_parts, tr, cols), lambda l, i, j=j: (0, jnp.where(l == j, i, 0), 0)) for j in range(n_layers)]


def _select_layer_sum(p_refs):
    l = pl.program_id(0)
    g = None
    for j, p_ref in enumerate(p_refs):
        gj = p_ref[0].astype(F32)
        for k in range(1, p_ref.shape[0]):
            gj = gj + p_ref[k].astype(F32)
        g = gj if g is None else jnp.where(l == j, gj, g)
    return g


def _sum_parts(parts, *, name):
    n_layers = len(parts)
    n_parts, rows, cols = parts[0].shape
    tr = _row_tile(rows)

    def body(*refs):
        refs[-1][...] = _select_layer_sum(refs[:n_layers])

    return pl.pallas_call(
        body,
        name=name,
        grid=(n_layers, rows // tr),
        in_specs=_layer_parts_specs(n_layers, n_parts, tr, cols),
        out_specs=pl.BlockSpec((None, tr, cols), lambda l, i: (l, i, 0)),
        out_shape=jax.ShapeDtypeStruct((n_layers, rows, cols), F32),
        compiler_params=_cparams(("arbitrary", "arbitrary")),
    )(*parts)


def _adamw(parts, w, m, v, *, name):
    n_layers, rows, cols = w.shape
    summed = not isinstance(parts, (list, tuple))
    tr = _row_tile(rows)
    n_in = 1 if summed else n_layers

    def body(*refs):
        w_ref, m_ref, v_ref, g_ref, d_ref, m2_ref, v2_ref = refs[n_in:]
        g = refs[0][...] if summed else _select_layer_sum(refs[:n_in])
        m2 = ADAM_B1 * m_ref[...] + (1.0 - ADAM_B1) * g
        v2 = ADAM_B2 * v_ref[...] + (1.0 - ADAM_B2) * (g * g)
        m_hat = m2 / (1.0 - ADAM_B1 ** ADAM_STEP)
        v_hat = v2 / (1.0 - ADAM_B2 ** ADAM_STEP)
        g_ref[...] = g
        d_ref[...] = -ADAM_LR * (m_hat / (jnp.sqrt(v_hat) + ADAM_EPS) + ADAM_WD * w_ref[...])
        m2_ref[...] = m2
        v2_ref[...] = v2

    slab = pl.BlockSpec((None, tr, cols), lambda l, i: (l, i, 0))
    out = jax.ShapeDtypeStruct((n_layers, rows, cols), F32)
    p_specs = [slab] if summed else _layer_parts_specs(n_layers, parts[0].shape[0], tr, cols)
    return pl.pallas_call(
        body,
        name=name,
        grid=(n_layers, rows // tr),
        in_specs=p_specs + [slab, slab, slab],
        out_specs=[slab, slab, slab, slab],
        out_shape=[out, out, out, out],
        compiler_params=_cparams(("arbitrary", "arbitrary")),
    )(*((parts,) if summed else tuple(parts)), w, m, v)


SLAB_ROWS = 256
_SMALL_SHARDED = (("conv_w", (2, 31, 32)), ("ffn_conv_w", (2, 3, 704)))
_REPLICATED = (("g_mix", (2, 1024)), ("conv_b", (2, 256)), ("conv_ln_g", (2, 256)), ("conv_ln_b", (2, 256)),
               ("sgu_ln_g", (2, 256)), ("sgu_ln_b", (2, 256)), ("sgu_w", (2, 4, 128, 128)), ("sgu_b", (2, 4, 128)),
               ("g_out", (2, 1024)), ("g_ffn", (2, 1024)), ("ffn_conv_b", (2, 5632)), ("g_final", (1024,)))


def _seg_rows(n_elems):
    return -(-n_elems // LANES)


def _pack(arrays, lead=()):
    segs = []
    for a in arrays:
        flat = a.reshape(lead + (-1,)).astype(F32)
        pad = _seg_rows(flat.shape[-1]) * LANES - flat.shape[-1]
        if pad:
            flat = jnp.pad(flat, [(0, 0)] * len(lead) + [(0, pad)])
        segs.append(flat)
    flat = jnp.concatenate(segs, axis=-1)
    rows = flat.shape[-1] // LANES
    pad_rows = -rows % SLAB_ROWS
    if pad_rows:
        flat = jnp.pad(flat, [(0, 0)] * len(lead) + [(0, pad_rows * LANES)])
    return flat.reshape(lead + (rows + pad_rows, LANES))


def _unpack(slab, shapes, lead=()):
    flat = slab.reshape(lead + (-1,))
    out, off = [], 0
    for shape in shapes:
        n = math.prod(shape)
        out.append(flat[..., off:off + n].reshape(lead + tuple(shape)))
        off += _seg_rows(n) * LANES
    return out


def _split_last(full):
    split = full.shape[:-1] + (N_DEV, full.shape[-1] // N_DEV)
    return jnp.moveaxis(full.reshape(split), -2, 0)


def _join_last(blocks):
    moved = jnp.moveaxis(blocks, 0, -2)
    return moved.reshape(moved.shape[:-2] + (moved.shape[-2] * moved.shape[-1],))


def _gathered(wt, n, l, after):
    if isinstance(wt[n][l], _SplitExchange):
        wt[n][l] = wt[n][l].finish(after)
    return wt[n][l]


def _layer_fwd(l, x, wt, small):
    tag = f"l{l}"
    h = _rmsnorm_fwd(x, small["g_mix"][l][None], name=f"{tag}_norm_mix")
    w_in_t = _gathered(wt, "w_in_t", l, h)
    p_ab = _matmul(h, w_in_t, "nt", name=f"{tag}_proj_ab", n=D_AB)
    qkv = _matmul(h, w_in_t, "nt", name=f"{tag}_proj_qkv", n=D_QKV, b_n0=D_AB, out_dtype=BF16)
    ya = _mixer_a_fwd(p_ab, wt["conv_w"][l], small["conv_b"][l][None], small["conv_ln_g"][l][None],
                      small["conv_ln_b"][l][None], name=f"{tag}_mixer_a")
    bias = jnp.repeat(small["sgu_b"][l].T, HEAD_DIM, axis=1)
    yb = _sgu_fwd(p_ab, small["sgu_ln_g"][l][None], small["sgu_ln_b"][l][None], small["sgu_w"][l], bias,
                  name=f"{tag}_sgu")
    yc, t_tot = _attn_fwd(qkv, name=f"{tag}_attn")
    y = _combine_fwd(ya, yb, yc, small["g_out"][l][None], name=f"{tag}_combine")
    x1 = _matmul(y, _gathered(wt, "w_out", l, y), "nn", name=f"{tag}_out_proj", residual=x)
    h2 = _rmsnorm_fwd(x1, small["g_ffn"][l][None], name=f"{tag}_norm_ffn")
    up0 = _matmul(h2, _gathered(wt, "w_up_t", l, h2), "nt", name=f"{tag}_up")
    act = _ffn_act_fwd(up0, wt["ffn_conv_w"][l], small["ffn_conv_b"][l][None], name=f"{tag}_ffn_act")
    x2 = _matmul(act, _gathered(wt, "w_down", l, act), "nn", name=f"{tag}_down", residual=x1)
    saved = dict(x=x, h=h, p_ab=p_ab, qkv=qkv, ya=ya, yb=yb, yc=yc, t_tot=t_tot, y=y, x1=x1, h2=h2, up0=up0,
                 act=act, bias=bias)
    return x2, saved


def _layer_bwd(l, dres, sv, wt, small, scattering):
    tag = f"l{l}b"
    g = {}

    def scatter(n, partial):
        scattering[n][l] = _SplitExchange(partial, kind="scatter_rows", name=f"scatter_{n}_l{l}")
        return _SplitExchange.start([scattering[n][l]], name=f"scatter_{n}_l{l}_start")

    dx2, dx2_b = dres
    dact = _matmul(dx2_b, wt["w_down"][l], "nt", name=f"{tag}_dact")
    tok = scatter("w_down", _matmul(sv["act"], dx2_b, "tn", name=f"{tag}_dw_down", out_dtype=BF16))
    dup0, g["ffn_conv_w"], dfb = _ffn_act_bwd(sv["up0"], dact, wt["ffn_conv_w"][l], small["ffn_conv_b"][l][None] + tok,
                                              name=f"{tag}_ffn_act")
    g["ffn_conv_b"] = dfb[0]
    dh2 = _matmul(dup0, wt["w_up_t"][l], "nn", name=f"{tag}_dh2")
    tok = scatter("w_up_t", _matmul(dup0, sv["h2"], "tn", name=f"{tag}_dw_up", out_dtype=BF16))
    dx1, dx1_b, dg = _rmsnorm_bwd(sv["x1"], small["g_ffn"][l][None] + tok, dh2, dx2, name=f"{tag}_norm_ffn")
    g["g_ffn"] = dg[0]
    dy = _matmul(dx1_b, wt["w_out"][l], "nt", name=f"{tag}_dy")
    tok = scatter("w_out", _matmul(sv["y"], dx1_b, "tn", name=f"{tag}_dw_out", out_dtype=BF16))
    dya, dyb, dyc, dg = _combine_bwd(dy, sv["ya"], sv["yb"], sv["yc"], small["g_out"][l][None] + tok,
                                     name=f"{tag}_combine")
    g["g_out"] = dg[0]
    dq, dk, dv = _attn_bwd(sv["qkv"], sv["t_tot"], dyc, name=f"{tag}_attn")
    dp_b, g["sgu_w"], db, dg, dbeta = _sgu_bwd(sv["p_ab"], dyb, small["sgu_ln_g"][l][None], small["sgu_ln_b"][l][None],
                                               small["sgu_w"][l], sv["bias"], name=f"{tag}_sgu")
    g["sgu_b"] = db[:, :N_SGU_HEADS].T
    g["sgu_ln_g"], g["sgu_ln_b"] = dg[0], dbeta[0]
    dp_a, g["conv_w"], dcb, dg, dbeta = _mixer_a_bwd(sv["p_ab"], dya, wt["conv_w"][l], small["conv_b"][l][None],
                                                     small["conv_ln_g"][l][None], small["conv_ln_b"][l][None],
                                                     name=f"{tag}_mixer_a")
    g["conv_b"], g["conv_ln_g"], g["conv_ln_b"] = dcb[0], dg[0], dbeta[0]
    dp = jnp.concatenate([dp_a.astype(BF16), dp_b.astype(BF16), dq.astype(BF16), dk.astype(BF16), dv.astype(BF16)],
                         axis=1)
    dh = _matmul(dp, wt["w_in_t"][l], "nn", name=f"{tag}_dh")
    tok = scatter("w_in_t", _matmul(dp, sv["h"], "tn", name=f"{tag}_dw_in", out_dtype=BF16))
    dx, dx_b, dg = _rmsnorm_bwd(sv["x"], small["g_mix"][l][None] + tok, dh, dx1, name=f"{tag}_norm_mix")
    g["g_mix"] = dg[0]
    return (dx, dx_b), g


_BIG = ("w_in_t", "w_out", "w_up_t", "w_down")


def kernel(x, g_mix, w_in, conv_w, conv_b, conv_ln_g, conv_ln_b, sgu_ln_g, sgu_ln_b, sgu_w, sgu_b, g_out, w_out, g_ffn, w_up, ffn_conv_w, ffn_conv_b, w_down, g_final, loss_target, m_g_mix, m_w_in, m_conv_w, m_conv_b, m_conv_ln_g, m_conv_ln_b, m_sgu_ln_g, m_sgu_ln_b, m_sgu_w, m_sgu_b, m_g_out, m_w_out, m_g_ffn, m_w_up, m_ffn_conv_w, m_ffn_conv_b, m_w_down, m_g_final, v_g_mix, v_w_in, v_conv_w, v_conv_b, v_conv_ln_g, v_conv_ln_b, v_sgu_ln_g, v_sgu_ln_b, v_sgu_w, v_sgu_b, v_g_out, v_w_out, v_g_ffn, v_w_up, v_ffn_conv_w, v_ffn_conv_b, v_w_down, v_g_final):
    given = dict(locals())
    n_layers = g_mix.shape[0]
    layers = range(n_layers)
    small_sharded = [n for n, _ in _SMALL_SHARDED]
    replicated = [n for n, _ in _REPLICATED]
    small = {n: given[n] for n in replicated}

    shard = {"w_in_t": [w_in[l].T.astype(BF16) for l in layers], "w_out": [w_out[l].astype(BF16) for l in layers],
             "w_up_t": [w_up[l].T.astype(BF16) for l in layers], "w_down": [w_down[l].astype(BF16) for l in layers]}
    filters = _exchange(_pack([given[n] for n in small_sharded]), kind="gather_blocks", name="gather_filters")
    filters, shard = lax.optimization_barrier((filters, shard))
    wt = {n: [_SplitExchange(shard[n][l], kind="gather_rows", name=f"gather_{n}_l{l}") for l in layers] for n in _BIG}
    small["g_mix"] = g_mix + _SplitExchange.start([wt[n][l] for l in layers for n in _BIG], name="gather_weights_start")
    for n, blocks in zip(small_sharded, _unpack(filters, [s for _, s in _SMALL_SHARDED], lead=(N_DEV,))):
        wt[n] = _join_last(blocks)

    xs = x[0]
    saved = []
    for l in layers:
        xs, sv = _layer_fwd(l, xs, wt, small)
        saved.append(sv)
    loss_tile, dx, dx_b, dgf = _loss_head(xs, g_final[None], loss_target[0], name="loss_head")
    dres = (dx, dx_b)
    scattering = {n: [None] * n_layers for n in _BIG}
    grads = [None] * n_layers
    for l in reversed(layers):
        dres, grads[l] = _layer_bwd(l, dres, saved[l], wt, small, scattering)
    received = {n: [scattering[n][l].finish(dres[0]) for l in layers] for n in _BIG}
    partial = {n: jnp.stack([g[n] for g in grads]) for n in grads[0]}
    partial["g_final"] = dgf[0]

    out = {}

    def update(n, parts):
        results = _adamw(parts, given[n], given["m_" + n], given["v_" + n], name=f"adamw_{n}")
        for pre, res in zip(("grad_", "delta_", "new_m_", "new_v_"), results):
            out[pre + n] = res

    update("w_out", received["w_out"])
    update("w_down", received["w_down"])
    for n in ("w_in", "w_up"):
        update(n, jnp.swapaxes(_sum_parts(received[n + "_t"], name=f"sum_{n}"), 1, 2))

    own = _pack([_split_last(partial[n]) for n in small_sharded], lead=(N_DEV,))
    shared = _pack([partial[n] for n in replicated])
    slab = jnp.concatenate([own, jnp.broadcast_to(shared[None], (N_DEV,) + shared.shape)], axis=1)
    slab = _exchange(slab, kind="scatter_blocks", name="scatter_small_grads")
    stacks = [jnp.concatenate([_pack([given[pre + n] for n in small_sharded]),
                               _pack([given[pre + n] for n in replicated])])[None] for pre in ("", "m_", "v_")]
    results = _adamw([slab], *stacks, name="adamw_small")
    n_own = own.shape[1]
    for pre, res in zip(("grad_", "delta_", "new_m_", "new_v_"), results):
        unpacked = (_unpack(res[0, :n_own], [s for _, s in _SMALL_SHARDED])
                    + _unpack(res[0, n_own:], [s for _, s in _REPLICATED]))
        for n, a in zip(small_sharded + replicated, unpacked):
            out[pre + n] = a

    loss = lax.psum(loss_tile[0, 0], ("x", "y", "c"))
    order = list(_WEIGHT_ORDER)
    return (loss, dres[0][None], *[out["grad_" + n] for n in order], *[out["delta_" + n] for n in order],
            *[out["new_m_" + n] for n in order], *[out["new_v_" + n] for n in order])


_WEIGHT_ORDER = ("g_mix", "w_in", "conv_w", "conv_b", "conv_ln_g", "conv_ln_b", "sgu_ln_g", "sgu_ln_b", "sgu_w", "sgu_b",
                 "g_out", "w_out", "g_ffn", "w_up", "ffn_conv_w", "ffn_conv_b", "w_down", "g_final")
```

```python
import functools
import math

import jax
import jax.numpy as jnp
from jax import lax
from jax.experimental import pallas as pl
from jax.experimental.pallas import tpu as pltpu

F32 = jnp.float32
BF16 = jnp.bfloat16

N_DEV = 8
D_MODEL = 1024
HEAD_DIM = 64
D_CONV = 256
D_SGU = 256
D_SB = 512
D_AB = 2 * D_CONV + 2 * D_SGU
D_QKV = 3 * D_SB
D_IN = D_AB + D_QKV
CONV_K = 31
CONV_HALO = 32
FFN_K = 3
FFN_HALO = 8
D_FF = 2816
CHUNK = 128
EPS = 1e-6
LANES = 128

ADAM_LR = 0.001
ADAM_B1 = 0.9
ADAM_B2 = 0.999
ADAM_EPS = 1e-08
ADAM_WD = 0.01
ADAM_STEP = 10

VMEM_LIMIT = 56 * 1024 * 1024


def _cparams(sem=None):
    return pltpu.CompilerParams(dimension_semantics=sem, vmem_limit_bytes=VMEM_LIMIT)


def _tile(n, prefs=(512, 256, 128)):
    for t in prefs:
        if n % t == 0:
            return t
    return n


def _sigmoid(x):
    return 1.0 / (1.0 + jnp.exp(-x))


def _softplus(x):
    return jnp.maximum(x, 0.0) + jnp.log1p(jnp.exp(-jnp.abs(x)))


_INV_SQRT2 = 1.0 / math.sqrt(2.0)
_INV_SQRT2PI = 1.0 / math.sqrt(2.0 * math.pi)


def _gelu(x):
    return 0.5 * x * (1.0 + lax.erf(x * _INV_SQRT2))


def _gelu_grad(x):
    return 0.5 * (1.0 + lax.erf(x * _INV_SQRT2)) + x * jnp.exp(-0.5 * x * x) * _INV_SQRT2PI


def _dot(a, b, dims):
    return lax.dot_general(a, b, (dims, ((), ())), preferred_element_type=F32)


_NN = ((1,), (0,))
_NT = ((1,), (1,))
_TN = ((0,), (0,))


def _split_bf16(x):
    hi = x.astype(BF16)
    lo = (x - hi.astype(F32)).astype(BF16)
    return jnp.concatenate([hi, lo], axis=1)


def _matmul(a, b, mode, *, name, out_dtype=F32, residual=None, n=None, b_n0=0, b_k0=0, rows=None, into=None):
    if mode == "nn":
        (m, k), n = a.shape, (n or b.shape[1])
    elif mode == "nt":
        (m, k), n = a.shape, (n or b.shape[0])
    else:
        (k, m), n = a.shape, b.shape[1]
    has_res = residual is not None
    tm, tn = _matmul_tiles(m, n, k, a.dtype.itemsize, b.dtype.itemsize, jnp.dtype(out_dtype).itemsize, has_res, b_n0)
    j0 = b_n0 // tn
    total_rows, first_row = rows or (m, 0)
    assert b_k0 % k == 0 and first_row % tm == 0
    kb, i0 = b_k0 // k, first_row // tm

    if mode == "nn":
        a_spec = pl.BlockSpec((tm, k), lambda i, j: (i, 0))
        b_spec = pl.BlockSpec((k, tn), lambda i, j: (kb, j + j0))
        dims = _NN
    elif mode == "nt":
        a_spec = pl.BlockSpec((tm, k), lambda i, j: (i, 0))
        b_spec = pl.BlockSpec((tn, k), lambda i, j: (j + j0, 0))
        dims = _NT
    else:
        a_spec = pl.BlockSpec((k, tm), lambda i, j: (0, i))
        b_spec = pl.BlockSpec((k, tn), lambda i, j: (0, j))
        dims = _TN
    o_spec = pl.BlockSpec((tm, tn), lambda i, j: (i + i0, j))
    r_spec = pl.BlockSpec((tm, tn), lambda i, j: (i, j))

    def body(*refs):
        a_ref, b_ref = refs[:2]
        acc = _dot(a_ref[...].astype(BF16), b_ref[...].astype(BF16), dims)
        if has_res:
            acc = acc + refs[2][...]
        refs[-1][...] = acc.astype(out_dtype)

    in_specs = [a_spec, b_spec] + ([r_spec] if has_res else [])
    args = (a, b) + ((residual,) if has_res else ())
    aliases = {}
    if into is not None:
        aliases = {len(args): 0}
        in_specs.append(pl.BlockSpec(memory_space=pl.ANY))
        args += (into,)

        def body(*refs, inner=body):
            inner(*refs[:len(args) - 1], refs[-1])

    return pl.pallas_call(
        body,
        name=name,
        grid=(m // tm, n // tn),
        in_specs=in_specs,
        out_specs=o_spec,
        out_shape=jax.ShapeDtypeStruct((total_rows, n), out_dtype),
        input_output_aliases=aliases,
        compiler_params=_cparams(("parallel", "parallel")),
    )(*args)


MATMUL_VMEM_BUDGET = 40 * 1024 * 1024


def _matmul_tiles(m, n, k, a_bytes, b_bytes, out_bytes, has_res, n_offset):
    def divisors(size, cap, also=0):
        return [t for t in range(cap, 0, -LANES) if size % t == 0 and also % t == 0] or [size]

    for tm in divisors(m, 1024):
        for tn in divisors(n, 1408, n_offset):
            blocks = tm * k * a_bytes + k * tn * b_bytes + tm * tn * (out_bytes + (4 if has_res else 0))
            if 2 * blocks <= MATMUL_VMEM_BUDGET:
                return tm, tn
    raise ValueError(f"no matmul tiling for {m} x {n} x {k}")


ROW_TILE = 256


def _rmsnorm_fwd(x, g, *, name):
    s, d = x.shape

    def body(x_ref, g_ref, h_ref):
        xv = x_ref[...]
        r = lax.rsqrt(jnp.mean(xv * xv, axis=-1, keepdims=True) + EPS)
        h_ref[...] = (xv * r * g_ref[...]).astype(BF16)

    return pl.pallas_call(
        body,
        name=name,
        grid=(s // ROW_TILE,),
        in_specs=[pl.BlockSpec((ROW_TILE, d), lambda i: (i, 0)), pl.BlockSpec((1, d), lambda i: (0, 0))],
        out_specs=pl.BlockSpec((ROW_TILE, d), lambda i: (i, 0)),
        out_shape=jax.ShapeDtypeStruct((s, d), BF16),
        compiler_params=_cparams(("parallel",)),
    )(x, g)


def _rmsnorm_bwd(x, g, dh, dres, *, name):
    s, d = x.shape

    def body(x_ref, g_ref, dh_ref, dres_ref, dx_ref, dxb_ref, dg_ref):
        xv = x_ref[...]
        r = lax.rsqrt(jnp.mean(xv * xv, axis=-1, keepdims=True) + EPS)
        xhat = xv * r
        dhv = dh_ref[...]
        dxhat = dhv * g_ref[...]
        dx = dres_ref[...] + r * (dxhat - xhat * jnp.mean(dxhat * xhat, axis=-1, keepdims=True))
        dx_ref[...] = dx
        dxb_ref[...] = dx.astype(BF16)
        part = jnp.sum(dhv * xhat, axis=0, keepdims=True)

        @pl.when(pl.program_id(0) == 0)
        def _():
            dg_ref[...] = part

        @pl.when(pl.program_id(0) > 0)
        def _():
            dg_ref[...] += part

    row = pl.BlockSpec((ROW_TILE, d), lambda i: (i, 0))
    vec = pl.BlockSpec((1, d), lambda i: (0, 0))
    return pl.pallas_call(
        body,
        name=name,
        grid=(s // ROW_TILE,),
        in_specs=[row, vec, row, row],
        out_specs=[row, row, vec],
        out_shape=[jax.ShapeDtypeStruct((s, d), F32), jax.ShapeDtypeStruct((s, d), BF16),
                   jax.ShapeDtypeStruct((1, d), F32)],
        compiler_params=_cparams(("arbitrary",)),
    )(x, g, dh, dres)


def _loss_head(x, g, target, *, name):
    s, d = x.shape

    def body(x_ref, g_ref, t_ref, loss_ref, dx_ref, dxb_ref, dg_ref):
        xv = x_ref[...]
        gv = g_ref[...]
        r = lax.rsqrt(jnp.mean(xv * xv, axis=-1, keepdims=True) + EPS)
        xhat = xv * r
        diff = xhat * gv - t_ref[...]
        dy = diff * (1.0 / d)
        dxhat = dy * gv
        dx = r * (dxhat - xhat * jnp.mean(dxhat * xhat, axis=-1, keepdims=True))
        dx_ref[...] = dx
        dxb_ref[...] = dx.astype(BF16)
        dg_part = jnp.sum(dy * xhat, axis=0, keepdims=True)
        row_loss = jnp.sum(diff * diff, axis=-1, keepdims=True)
        loss_part = jnp.sum(row_loss, axis=0, keepdims=True) * (0.5 / d)

        @pl.when(pl.program_id(0) == 0)
        def _():
            dg_ref[...] = dg_part
            loss_ref[...] = jnp.broadcast_to(loss_part, loss_ref.shape)

        @pl.when(pl.program_id(0) > 0)
        def _():
            dg_ref[...] += dg_part
            loss_ref[...] += jnp.broadcast_to(loss_part, loss_ref.shape)

    row = pl.BlockSpec((ROW_TILE, d), lambda i: (i, 0))
    vec = pl.BlockSpec((1, d), lambda i: (0, 0))
    tile = pl.BlockSpec((8, LANES), lambda i: (0, 0))
    return pl.pallas_call(
        body,
        name=name,
        grid=(s // ROW_TILE,),
        in_specs=[row, vec, row],
        out_specs=[tile, row, row, vec],
        out_shape=[jax.ShapeDtypeStruct((8, LANES), F32), jax.ShapeDtypeStruct((s, d), F32),
                   jax.ShapeDtypeStruct((s, d), BF16), jax.ShapeDtypeStruct((1, d), F32)],
        compiler_params=_cparams(("arbitrary",)),
    )(x, g, target)


_BRANCHES = ((0, D_CONV), (D_CONV, D_SGU), (D_CONV + D_SGU, D_SB))


def _combine_fwd(ya, yb, yc, g, *, name):
    s = ya.shape[0]

    def body(ya_ref, yb_ref, yc_ref, g_ref, y_ref):
        for ref, (off, w) in zip((ya_ref, yb_ref, yc_ref), _BRANCHES):
            v = ref[...]
            r = lax.rsqrt(jnp.mean(v * v, axis=-1, keepdims=True) + EPS)
            y_ref[:, off:off + w] = (v * r * g_ref[:, off:off + w]).astype(BF16)

    def row(w):
        return pl.BlockSpec((ROW_TILE, w), lambda i: (i, 0))

    return pl.pallas_call(
        body,
        name=name,
        grid=(s // ROW_TILE,),
        in_specs=[row(D_CONV), row(D_SGU), row(D_SB), pl.BlockSpec((1, D_MODEL), lambda i: (0, 0))],
        out_specs=row(D_MODEL),
        out_shape=jax.ShapeDtypeStruct((s, D_MODEL), BF16),
        compiler_params=_cparams(("parallel",)),
    )(ya, yb, yc, g)


def _combine_bwd(dy, ya, yb, yc, g, *, name):
    s = ya.shape[0]

    def body(dy_ref, ya_ref, yb_ref, yc_ref, g_ref, dya_ref, dyb_ref, dyc_ref, dg_ref):
        first = pl.program_id(0) == 0
        for ref, dref, (off, w) in zip((ya_ref, yb_ref, yc_ref), (dya_ref, dyb_ref, dyc_ref), _BRANCHES):
            v = ref[...]
            r = lax.rsqrt(jnp.mean(v * v, axis=-1, keepdims=True) + EPS)
            n = v * r
            dout = dy_ref[:, off:off + w]
            dn = dout * g_ref[:, off:off + w]
            dref[...] = r * (dn - n * jnp.mean(dn * n, axis=-1, keepdims=True))
            part = jnp.sum(dout * n, axis=0, keepdims=True)

            @pl.when(first)
            def _():
                dg_ref[:, off:off + w] = part

            @pl.when(jnp.logical_not(first))
            def _():
                dg_ref[:, off:off + w] += part

    def row(w):
        return pl.BlockSpec((ROW_TILE, w), lambda i: (i, 0))

    vec = pl.BlockSpec((1, D_MODEL), lambda i: (0, 0))
    return pl.pallas_call(
        body,
        name=name,
        grid=(s // ROW_TILE,),
        in_specs=[row(D_MODEL), row(D_CONV), row(D_SGU), row(D_SB), vec],
        out_specs=[row(D_CONV), row(D_SGU), row(D_SB), vec],
        out_shape=[jax.ShapeDtypeStruct((s, D_CONV), F32), jax.ShapeDtypeStruct((s, D_SGU), F32),
                   jax.ShapeDtypeStruct((s, D_SB), F32), jax.ShapeDtypeStruct((1, D_MODEL), F32)],
        compiler_params=_cparams(("arbitrary",)),
    )(dy, ya, yb, yc, g)


CONV_TILE = 128


def _shift_down(window, j, halo):
    return pltpu.roll(window, j, 0)[halo:, :] if j else window[halo:, :]


def _shift_up(window, j, n_out):
    n = window.shape[0]
    return pltpu.roll(window, n - j, 0)[:n_out, :] if j else window[:n_out, :]


def _mixer_a_fwd(p_ab, conv_w, conv_b, ln_g, ln_b, *, name):
    s = p_ab.shape[0]
    nt = s // CONV_TILE

    def body(p_ref, w_ref, b_ref, g_ref, beta_ref, y_ref, h_ref):
        h_ref[0:CONV_HALO, :] = jnp.zeros((CONV_HALO, D_CONV), F32)

        def glu(i, c):
            t0 = pl.multiple_of(i * CONV_TILE, CONV_TILE)
            a = p_ref[pl.ds(t0, CONV_TILE), 0:D_CONV]
            gate = p_ref[pl.ds(t0, CONV_TILE), D_CONV:2 * D_CONV]
            h_ref[pl.ds(t0 + CONV_HALO, CONV_TILE), :] = a * _sigmoid(gate)
            return c

        lax.fori_loop(0, nt, glu, 0)

        def conv(i, c):
            t0 = pl.multiple_of(i * CONV_TILE, CONV_TILE)
            window = h_ref[pl.ds(t0, CONV_TILE + CONV_HALO), :]
            acc = jnp.zeros((CONV_TILE, D_CONV), F32) + b_ref[...]
            for k in range(CONV_K):
                acc = acc + w_ref[k:k + 1, :] * _shift_down(window, CONV_K - 1 - k, CONV_HALO)
            mu = jnp.mean(acc, axis=-1, keepdims=True)
            xc = acc - mu
            rstd = lax.rsqrt(jnp.mean(xc * xc, axis=-1, keepdims=True) + EPS)
            z = xc * rstd * g_ref[...] + beta_ref[...]
            y_ref[pl.ds(t0, CONV_TILE), :] = z * _sigmoid(z)
            return c

        lax.fori_loop(0, nt, conv, 0)

    full = lambda shape: pl.BlockSpec(shape, lambda i: (0, 0))
    return pl.pallas_call(
        body,
        name=name,
        grid=(1,),
        in_specs=[full((s, 2 * D_CONV)), full((CONV_K, D_CONV)), full((1, D_CONV)), full((1, D_CONV)),
                  full((1, D_CONV))],
        out_specs=full((s, D_CONV)),
        out_shape=jax.ShapeDtypeStruct((s, D_CONV), F32),
        scratch_shapes=[pltpu.VMEM((s + CONV_HALO, D_CONV), F32)],
        compiler_params=_cparams(("arbitrary",)),
    )(p_ab, conv_w, conv_b, ln_g, ln_b)


def _mixer_a_bwd(p_ab, dya, conv_w, conv_b, ln_g, ln_b, *, name):
    s = p_ab.shape[0]
    nt = s // CONV_TILE

    def body(p_ref, dy_ref, w_ref, b_ref, g_ref, beta_ref, dp_ref, dw_ref, db_ref, dg_ref, dbeta_ref, h_ref, dc_ref):
        h_ref[0:CONV_HALO, :] = jnp.zeros((CONV_HALO, D_CONV), F32)
        dc_ref[s:s + CONV_HALO, :] = jnp.zeros((CONV_HALO, D_CONV), F32)
        dw_ref[...] = jnp.zeros_like(dw_ref)
        db_ref[...] = jnp.zeros_like(db_ref)
        dg_ref[...] = jnp.zeros_like(dg_ref)
        dbeta_ref[...] = jnp.zeros_like(dbeta_ref)

        def glu(i, c):
            t0 = pl.multiple_of(i * CONV_TILE, CONV_TILE)
            a = p_ref[pl.ds(t0, CONV_TILE), 0:D_CONV]
            gate = p_ref[pl.ds(t0, CONV_TILE), D_CONV:2 * D_CONV]
            h_ref[pl.ds(t0 + CONV_HALO, CONV_TILE), :] = a * _sigmoid(gate)
            return c

        lax.fori_loop(0, nt, glu, 0)

        def conv_bwd(i, c):
            t0 = pl.multiple_of(i * CONV_TILE, CONV_TILE)
            window = h_ref[pl.ds(t0, CONV_TILE + CONV_HALO), :]
            taps = [_shift_down(window, CONV_K - 1 - k, CONV_HALO) for k in range(CONV_K)]
            acc = jnp.zeros((CONV_TILE, D_CONV), F32) + b_ref[...]
            for k in range(CONV_K):
                acc = acc + w_ref[k:k + 1, :] * taps[k]
            mu = jnp.mean(acc, axis=-1, keepdims=True)
            xc = acc - mu
            rstd = lax.rsqrt(jnp.mean(xc * xc, axis=-1, keepdims=True) + EPS)
            xhat = xc * rstd
            z = xhat * g_ref[...] + beta_ref[...]
            sg = _sigmoid(z)
            dz = dy_ref[pl.ds(t0, CONV_TILE), :] * (sg * (1.0 + z * (1.0 - sg)))
            dg_ref[...] += jnp.sum(dz * xhat, axis=0, keepdims=True)
            dbeta_ref[...] += jnp.sum(dz, axis=0, keepdims=True)
            dxhat = dz * g_ref[...]
            dc = rstd * (dxhat - jnp.mean(dxhat, axis=-1, keepdims=True)
                         - xhat * jnp.mean(dxhat * xhat, axis=-1, keepdims=True))
            dc_ref[pl.ds(t0, CONV_TILE), :] = dc
            db_ref[...] += jnp.sum(dc, axis=0, keepdims=True)
            for k in range(CONV_K):
                dw_ref[k:k + 1, :] += jnp.sum(dc * taps[k], axis=0, keepdims=True)
            return c

        lax.fori_loop(0, nt, conv_bwd, 0)

        def glu_bwd(i, c):
            t0 = pl.multiple_of(i * CONV_TILE, CONV_TILE)
            window = dc_ref[pl.ds(t0, CONV_TILE + CONV_HALO), :]
            dh = jnp.zeros((CONV_TILE, D_CONV), F32)
            for j in range(CONV_K):
                dh = dh + w_ref[CONV_K - 1 - j:CONV_K - j, :] * _shift_up(window, j, CONV_TILE)
            a = p_ref[pl.ds(t0, CONV_TILE), 0:D_CONV]
            sg = _sigmoid(p_ref[pl.ds(t0, CONV_TILE), D_CONV:2 * D_CONV])
            dp_ref[pl.ds(t0, CONV_TILE), 0:D_CONV] = dh * sg
            dp_ref[pl.ds(t0, CONV_TILE), D_CONV:2 * D_CONV] = dh * a * sg * (1.0 - sg)
            return c

        lax.fori_loop(0, nt, glu_bwd, 0)

    full = lambda shape: pl.BlockSpec(shape, lambda i: (0, 0))
    vec = jax.ShapeDtypeStruct((1, D_CONV), F32)
    return pl.pallas_call(
        body,
        name=name,
        grid=(1,),
        in_specs=[full((s, 2 * D_CONV)), full((s, D_CONV)), full((CONV_K, D_CONV)), full((1, D_CONV)),
                  full((1, D_CONV)), full((1, D_CONV))],
        out_specs=[full((s, 2 * D_CONV)), full((CONV_K, D_CONV)), full((1, D_CONV)), full((1, D_CONV)),
                   full((1, D_CONV))],
        out_shape=[jax.ShapeDtypeStruct((s, 2 * D_CONV), F32), jax.ShapeDtypeStruct((CONV_K, D_CONV), F32),
                   vec, vec, vec],
        scratch_shapes=[pltpu.VMEM((s + CONV_HALO, D_CONV), F32), pltpu.VMEM((s + CONV_HALO, D_CONV), F32)],
        compiler_params=_cparams(("arbitrary",)),
    )(p_ab, dya, conv_w, conv_b, ln_g, ln_b)


N_SGU_HEADS = D_SGU // HEAD_DIM


def _head_masks(width):
    lane = lax.broadcasted_iota(jnp.int32, (1, width), 1)
    return [(lane >= h * HEAD_DIM) & (lane < (h + 1) * HEAD_DIM) for h in range(width // HEAD_DIM)]


def _tril_mask():
    r = lax.broadcasted_iota(jnp.int32, (CHUNK, CHUNK), 0)
    c = lax.broadcasted_iota(jnp.int32, (CHUNK, CHUNK), 1)
    return c <= r


def _sgu_norm(bv, g, beta):
    vg = _gelu(bv)
    mu = jnp.mean(vg, axis=-1, keepdims=True)
    xc = vg - mu
    rstd = lax.rsqrt(jnp.mean(xc * xc, axis=-1, keepdims=True) + EPS)
    xhat = xc * rstd
    return xhat, rstd, xhat * g + beta


def _sgu_fwd(p_ab, ln_g, ln_b, w_s, bias, *, name):
    s = p_ab.shape[0]

    def body(p_ref, g_ref, beta_ref, w_ref, bias_ref, y_ref):
        u = _gelu(p_ref[:, 0:D_SGU])
        _, _, vn = _sgu_norm(p_ref[:, D_SGU:2 * D_SGU], g_ref[...], beta_ref[...])
        vb = vn.astype(BF16)
        tril = _tril_mask()
        mixed = bias_ref[...]
        for h, m in enumerate(_head_masks(D_SGU)):
            wh = jnp.where(tril, w_ref[h], 0.0).astype(BF16)
            mixed = mixed + _dot(wh, jnp.where(m, vb, jnp.zeros_like(vb)), _NN)
        y_ref[...] = u * mixed

    return pl.pallas_call(
        body,
        name=name,
        grid=(s // CHUNK,),
        in_specs=[pl.BlockSpec((CHUNK, 2 * D_SGU), lambda i: (i, 1)),
                  pl.BlockSpec((1, D_SGU), lambda i: (0, 0)), pl.BlockSpec((1, D_SGU), lambda i: (0, 0)),
                  pl.BlockSpec((N_SGU_HEADS, CHUNK, CHUNK), lambda i: (0, 0, 0)),
                  pl.BlockSpec((CHUNK, D_SGU), lambda i: (0, 0))],
        out_specs=pl.BlockSpec((CHUNK, D_SGU), lambda i: (i, 0)),
        out_shape=jax.ShapeDtypeStruct((s, D_SGU), F32),
        compiler_params=_cparams(("parallel",)),
    )(p_ab, ln_g, ln_b, w_s, bias)


def _sgu_bwd(p_ab, dyb, ln_g, ln_b, w_s, bias, *, name):
    s = p_ab.shape[0]
    n_chunks = s // CHUNK

    def body(p_ref, dy_ref, g_ref, beta_ref, w_ref, bias_ref, dp_ref, dw_ref, db_ref, dg_ref, dbeta_ref, dbias_ref):
        @pl.when(pl.program_id(0) == 0)
        def _():
            dw_ref[...] = jnp.zeros_like(dw_ref)
            dbias_ref[...] = jnp.zeros_like(dbias_ref)
            dg_ref[...] = jnp.zeros_like(dg_ref)
            dbeta_ref[...] = jnp.zeros_like(dbeta_ref)

        bu = p_ref[:, 0:D_SGU]
        bv = p_ref[:, D_SGU:2 * D_SGU]
        u = _gelu(bu)
        gv = g_ref[...]
        xhat, rstd, vn = _sgu_norm(bv, gv, beta_ref[...])
        vb = vn.astype(BF16)
        tril = _tril_mask()
        masks = _head_masks(D_SGU)
        whs = [jnp.where(tril, w_ref[h], 0.0).astype(BF16) for h in range(N_SGU_HEADS)]
        mixed = bias_ref[...]
        for h, m in enumerate(masks):
            mixed = mixed + _dot(whs[h], jnp.where(m, vb, jnp.zeros_like(vb)), _NN)
        dy = dy_ref[...]
        dp_ref[:, 0:D_SGU] = dy * mixed * _gelu_grad(bu)
        dmixed = dy * u
        dbias_ref[...] += dmixed
        dmb = dmixed.astype(BF16)
        dvn = jnp.zeros((CHUNK, D_SGU), F32)
        for h, m in enumerate(masks):
            dmh = jnp.where(m, dmb, jnp.zeros_like(dmb))
            dvn = dvn + _dot(whs[h], dmh, _TN)
            dw_ref[h] += jnp.where(tril, _dot(dmh, vb, _NT), 0.0)
        dg_ref[...] += jnp.sum(dvn * xhat, axis=0, keepdims=True)
        dbeta_ref[...] += jnp.sum(dvn, axis=0, keepdims=True)
        dxhat = dvn * gv
        dvg = rstd * (dxhat - jnp.mean(dxhat, axis=-1, keepdims=True)
                      - xhat * jnp.mean(dxhat * xhat, axis=-1, keepdims=True))
        dp_ref[:, D_SGU:2 * D_SGU] = dvg * _gelu_grad(bv)

        @pl.when(pl.program_id(0) == n_chunks - 1)
        def _():
            chan = lax.broadcasted_iota(jnp.int32, (D_SGU, LANES), 0)
            head = lax.broadcasted_iota(jnp.int32, (D_SGU, LANES), 1)
            to_head = jnp.where(chan // HEAD_DIM == head, 1.0, 0.0).astype(BF16)
            db_ref[...] = _dot(_split_bf16(dbias_ref[...]), jnp.concatenate([to_head, to_head], axis=0), _NN)

    vec = pl.BlockSpec((1, D_SGU), lambda i: (0, 0))
    wspec = pl.BlockSpec((N_SGU_HEADS, CHUNK, CHUNK), lambda i: (0, 0, 0))
    bspec = pl.BlockSpec((CHUNK, D_SGU), lambda i: (0, 0))
    return pl.pallas_call(
        body,
        name=name,
        grid=(n_chunks,),
        in_specs=[pl.BlockSpec((CHUNK, 2 * D_SGU), lambda i: (i, 1)), pl.BlockSpec((CHUNK, D_SGU), lambda i: (i, 0)),
                  vec, vec, wspec, bspec],
        out_specs=[pl.BlockSpec((CHUNK, 2 * D_SGU), lambda i: (i, 0)), wspec,
                   pl.BlockSpec((CHUNK, LANES), lambda i: (0, 0)), vec, vec],
        out_shape=[jax.ShapeDtypeStruct((s, 2 * D_SGU), F32),
                   jax.ShapeDtypeStruct((N_SGU_HEADS, CHUNK, CHUNK), F32),
                   jax.ShapeDtypeStruct((CHUNK, LANES), F32),
                   jax.ShapeDtypeStruct((1, D_SGU), F32), jax.ShapeDtypeStruct((1, D_SGU), F32)],
        scratch_shapes=[pltpu.VMEM((CHUNK, D_SGU), F32)],
        compiler_params=_cparams(("arbitrary",)),
    )(p_ab, dyb, ln_g, ln_b, w_s, bias)


N_PAIRS = D_SB // LANES
SB_SCALE = HEAD_DIM ** -0.5


def _sb_logits(z, valid):
    nz = -z
    t = jnp.log(1.0 + jnp.exp(jnp.minimum(z, nz)))
    l1 = jnp.minimum(nz, 0.0) - t
    if valid is not None:
        l1 = jnp.where(valid, l1, 0.0)
    return l1, jnp.minimum(z, 0.0) - t


def _split_hi_lo(x):
    hi = lax.bitcast_convert_type(lax.bitcast_convert_type(x, jnp.uint32) & jnp.uint32(0xFFFF0000), F32)
    return jnp.concatenate([hi, x - hi], axis=1)


def _cumsum_operand(keep):
    half = jnp.concatenate([keep.astype(F32), jnp.ones((CHUNK, CHUNK), F32)], axis=1)
    return jnp.concatenate([half, half], axis=0)


def _attn_fwd(qkv, *, name, exchanges=None):
    s = qkv.shape[0]
    nq = s // CHUNK
    ex = exchanges or _Exchanges([])
    n_ex = len(ex.arrays)

    def body(*refs):
        q_ref, k_ref, v_ref = refs[:3]
        o_ref, t_ref = refs[3 + n_ex:5 + n_ex]
        ex_refs = (refs[3:3 + n_ex], refs[5 + n_ex:5 + 2 * n_ex]) + refs[5 + 2 * n_ex:]
        qi = pl.program_id(1)
        if n_ex:
            @pl.when((pl.program_id(0) == 0) & (qi == 0))
            def _():
                ex.start(*ex_refs)

        q = q_ref[...] * SB_SCALE
        zero = jnp.zeros_like(q)
        masks = _head_masks(LANES)
        qs = [jnp.where(m, q, zero) for m in masks]
        row = lax.broadcasted_iota(jnp.int32, (CHUNK, CHUNK), 0)
        col = lax.broadcasted_iota(jnp.int32, (CHUNK, CHUNK), 1)
        after_op = _cumsum_operand(row > col)

        cmr = col - row

        def blocks(js, carry):
            o, c0, c1 = carry
            kvs, valids = [], []
            for j in js:
                k0 = pl.multiple_of(jnp.maximum(j, 0) * CHUNK, CHUNK)
                kvs.append((k_ref[pl.ds(k0, CHUNK), :], v_ref[pl.ds(k0, CHUNK), :]))
                valids.append(cmr < jnp.where(j >= 0, (qi - j) * CHUNK, -CHUNK))
            units = [(h, b) for b in range(len(js)) for h in range(2)]
            zs = [_dot(qs[h], kvs[b][0], _NT) for h, b in units]
            logits = [_sb_logits(z, valids[b]) for z, (h, b) in zip(zs, units)]
            sums = [_dot(_split_hi_lo(l1), after_op, _NN) for l1, _ in logits]
            cs = [c0, c1]
            probs = []
            for (h, b), (_, lb), sm in zip(units, logits, sums):
                probs.append(jnp.where(valids[b], jnp.exp(lb + sm[:, :CHUNK] + cs[h]), 0.0))
                cs[h] = cs[h] + sm[:, CHUNK:]
            for (h, b), a in zip(units, probs):
                o = o + _dot(a.astype(BF16), jnp.where(masks[h], kvs[b][1], zero), _NN)
            return o, cs[0], cs[1]

        zc = jnp.zeros((CHUNK, LANES), F32)
        n_four = (qi + 1) // 4
        carry = lax.fori_loop(0, n_four, lambda jj, c: blocks([qi - 4 * jj - i for i in range(4)], c), (zc,) * 3)
        top = qi - 4 * n_four
        o, c0, c1 = lax.fori_loop(0, (top + 2) // 2, lambda jj, c: blocks([top - 2 * jj, top - 2 * jj - 1], c), carry)
        o_ref[...] = o
        t_ref[:, 0:LANES] = c0
        t_ref[:, LANES:2 * LANES] = c1
        if n_ex:
            @pl.when((pl.program_id(0) == N_PAIRS - 1) & (qi == nq - 1))
            def _():
                ex.wait(*ex_refs)

    return pl.pallas_call(
        body,
        name=name,
        grid=(N_PAIRS, nq),
        in_specs=[pl.BlockSpec((CHUNK, LANES), lambda p, i: (i, p)),
                  pl.BlockSpec((s, LANES), lambda p, i: (0, N_PAIRS + p)),
                  pl.BlockSpec((s, LANES), lambda p, i: (0, 2 * N_PAIRS + p))] + ex.in_specs,
        out_specs=[pl.BlockSpec((CHUNK, LANES), lambda p, i: (i, p)),
                   pl.BlockSpec((CHUNK, 2 * LANES), lambda p, i: (i, p))] + ex.out_specs,
        out_shape=[jax.ShapeDtypeStruct((s, D_SB), F32), jax.ShapeDtypeStruct((s, 2 * D_SB), F32)] + ex.out_shapes,
        scratch_shapes=ex.scratch_shapes if n_ex else [],
        compiler_params=_cparams(("arbitrary", "arbitrary")),
    )(qkv, qkv, qkv, *ex.arrays)


def _attn_bwd(qkv, t_tot, do, *, name, exchanges=None):
    s = qkv.shape[0]
    nq = s // CHUNK
    ex = exchanges or _Exchanges([])
    n_ex = len(ex.arrays)

    def body(*refs):
        q_ref, k_ref, v_ref, t_ref, do_ref = refs[:5]
        dq_ref, dk_ref, dv_ref = refs[5 + n_ex:8 + n_ex]
        ex_refs = (refs[5:5 + n_ex], refs[8 + n_ex:8 + 2 * n_ex]) + refs[8 + 2 * n_ex:]
        qi = pl.program_id(1)
        if n_ex:
            @pl.when((pl.program_id(0) == 0) & (qi == 0))
            def _():
                ex.start(*ex_refs)

        @pl.when(qi == 0)
        def _():
            dk_ref[...] = jnp.zeros_like(dk_ref)
            dv_ref[...] = jnp.zeros_like(dv_ref)

        q = q_ref[...] * SB_SCALE
        dob = do_ref[...].astype(BF16)
        zero = jnp.zeros_like(q)
        masks = _head_masks(LANES)
        qs = [jnp.where(m, q, zero) for m in masks]
        dos = [jnp.where(m, dob, zero) for m in masks]
        tots = [t_ref[:, 0:LANES], t_ref[:, LANES:2 * LANES]]
        row = lax.broadcasted_iota(jnp.int32, (CHUNK, CHUNK), 0)
        col = lax.broadcasted_iota(jnp.int32, (CHUNK, CHUNK), 1)
        upto_op = _cumsum_operand(row <= col)
        before_op = _cumsum_operand(row < col)

        cmr = col - row

        def blocks(js, carry):
            dq, cl0, cl1, cp0, cp1 = carry
            starts = [pl.multiple_of(jnp.minimum(j, nq - 1) * CHUNK, CHUNK) for j in js]
            valids = [cmr < (qi - j) * CHUNK for j in js]
            kvs = [(k_ref[pl.ds(k0, CHUNK), :], v_ref[pl.ds(k0, CHUNK), :]) for k0 in starts]
            units = [(h, b) for b in range(len(js)) for h in range(2)]
            zs = [_dot(qs[h], kvs[b][0], _NT) for h, b in units]
            das = [_dot(dos[h], kvs[b][1], _NT) for h, b in units]
            logits = [_sb_logits(z, valids[b]) for z, (h, b) in zip(zs, units)]
            sums = [_dot(_split_hi_lo(l1), upto_op, _NN) for l1, _ in logits]
            cls, cps = [cl0, cl1], [cp0, cp1]
            probs, gs = [], []
            for (h, b), (_, lb), sm, da in zip(units, logits, sums, das):
                a = jnp.where(valids[b], jnp.exp(lb + (tots[h] - cls[h] - sm[:, :CHUNK])), 0.0)
                probs.append(a)
                gs.append(a * da)
                cls[h] = cls[h] + sm[:, CHUNK:]
            sums_g = [_dot(_split_hi_lo(g), before_op, _NN) for g in gs]
            dzs = []
            for (h, b), (_, lb), g, sg in zip(units, logits, gs, sums_g):
                dz = g - (g + sg[:, :CHUNK] + cps[h]) * jnp.exp(lb)
                dzs.append(jnp.where(valids[b], dz, 0.0).astype(BF16))
                cps[h] = cps[h] + sg[:, CHUNK:]
            for (h, b), dzb in zip(units, dzs):
                dq = dq + _dot(dzb, jnp.where(masks[h], kvs[b][0], zero), _NN)
            for b, k0 in enumerate(starts):
                dk_ref[pl.ds(k0, CHUNK), :] += _dot(dzs[2 * b], qs[0], _TN) + _dot(dzs[2 * b + 1], qs[1], _TN)
                dv_ref[pl.ds(k0, CHUNK), :] += (_dot(probs[2 * b].astype(BF16), dos[0], _TN)
                                                + _dot(probs[2 * b + 1].astype(BF16), dos[1], _TN))
            return dq, cls[0], cls[1], cps[0], cps[1]

        zc = jnp.zeros((CHUNK, LANES), F32)
        n_four = (qi + 1) // 4
        carry = lax.fori_loop(0, n_four, lambda jj, c: blocks([4 * jj + i for i in range(4)], c), (zc,) * 5)
        base = 4 * n_four
        carry = lax.fori_loop(0, (qi - base + 2) // 2, lambda jj, c: blocks([base + 2 * jj, base + 2 * jj + 1], c), carry)
        dq_ref[...] = carry[0] * SB_SCALE
        if n_ex:
            @pl.when((pl.program_id(0) == N_PAIRS - 1) & (qi == nq - 1))
            def _():
                ex.wait(*ex_refs)

    blk = pl.BlockSpec((CHUNK, LANES), lambda p, i: (i, p))
    col_blk = pl.BlockSpec((s, LANES), lambda p, i: (0, p))
    out = jax.ShapeDtypeStruct((s, D_SB), F32)
    return pl.pallas_call(
        body,
        name=name,
        grid=(N_PAIRS, nq),
        in_specs=[blk,
                  pl.BlockSpec((s, LANES), lambda p, i: (0, N_PAIRS + p)),
                  pl.BlockSpec((s, LANES), lambda p, i: (0, 2 * N_PAIRS + p)),
                  pl.BlockSpec((CHUNK, 2 * LANES), lambda p, i: (i, p)),
                  blk] + ex.in_specs,
        out_specs=[blk, col_blk, col_blk] + ex.out_specs,
        out_shape=[out, out, out] + ex.out_shapes,
        scratch_shapes=ex.scratch_shapes if n_ex else [],
        compiler_params=_cparams(("arbitrary", "arbitrary")),
    )(qkv, qkv, qkv, t_tot, do, *ex.arrays)


FFN_TILE = 256
FFN_COLS = 256
N_FF_BLOCKS = D_FF // FFN_COLS


def _ffn_act_fwd(up0, conv_w, conv_b, *, name):
    s = up0.shape[0]
    nt = s // FFN_TILE

    def body(xg_ref, xv_ref, wg_ref, wv_ref, bg_ref, bv_ref, act_ref, pg_ref, pv_ref):
        pg_ref[0:FFN_HALO, :] = jnp.zeros((FFN_HALO, FFN_COLS), F32)
        pv_ref[0:FFN_HALO, :] = jnp.zeros((FFN_HALO, FFN_COLS), F32)
        pg_ref[FFN_HALO:, :] = xg_ref[...]
        pv_ref[FFN_HALO:, :] = xv_ref[...]

        def tile(i, c):
            t0 = pl.multiple_of(i * FFN_TILE, FFN_TILE)
            outs = []
            for p_ref, w_ref, b_ref in ((pg_ref, wg_ref, bg_ref), (pv_ref, wv_ref, bv_ref)):
                window = p_ref[pl.ds(t0, FFN_TILE + FFN_HALO), :]
                acc = b_ref[...] + w_ref[2:3, :] * window[FFN_HALO:, :]
                for j in range(1, FFN_K):
                    acc = acc + w_ref[FFN_K - 1 - j:FFN_K - j, :] * _shift_down(window, j, FFN_HALO)
                outs.append(acc)
            gate, val = outs
            act_ref[pl.ds(t0, FFN_TILE), :] = (gate * _sigmoid(gate) * val).astype(BF16)
            return c

        lax.fori_loop(0, nt, tile, 0)

    gcol = lambda rows: pl.BlockSpec((rows, FFN_COLS), lambda j: (0, j))
    vcol = lambda rows: pl.BlockSpec((rows, FFN_COLS), lambda j: (0, j + N_FF_BLOCKS))
    return pl.pallas_call(
        body,
        name=name,
        grid=(N_FF_BLOCKS,),
        in_specs=[gcol(s), vcol(s), gcol(FFN_K), vcol(FFN_K), gcol(1), vcol(1)],
        out_specs=gcol(s),
        out_shape=jax.ShapeDtypeStruct((s, D_FF), BF16),
        scratch_shapes=[pltpu.VMEM((s + FFN_HALO, FFN_COLS), F32), pltpu.VMEM((s + FFN_HALO, FFN_COLS), F32)],
        compiler_params=_cparams(("parallel",)),
    )(up0, up0, conv_w, conv_w, conv_b, conv_b)


def _ffn_act_bwd(up0, dact, conv_w, conv_b, *, name):
    s = up0.shape[0]
    nt = s // FFN_TILE

    def body(xg_ref, xv_ref, da_ref, wg_ref, wv_ref, bg_ref, bv_ref, dxg_ref, dxv_ref, dwg_ref, dwv_ref, dbg_ref, dbv_ref,
             pg_ref, pv_ref, dg_ref, dv_ref):
        zeros = jnp.zeros((FFN_HALO, FFN_COLS), F32)
        for p_ref, x_ref in ((pg_ref, xg_ref), (pv_ref, xv_ref)):
            p_ref[0:FFN_HALO, :] = zeros
            p_ref[FFN_HALO:, :] = x_ref[...]
        dg_ref[s:s + FFN_HALO, :] = zeros
        dv_ref[s:s + FFN_HALO, :] = zeros
        for ref in (dwg_ref, dwv_ref, dbg_ref, dbv_ref):
            ref[...] = jnp.zeros_like(ref)

        def conv(p_ref, w_ref, b_ref, t0):
            window = p_ref[pl.ds(t0, FFN_TILE + FFN_HALO), :]
            taps = [_shift_down(window, j, FFN_HALO) for j in range(FFN_K)]
            out = b_ref[...]
            for j in range(FFN_K):
                out = out + w_ref[FFN_K - 1 - j:FFN_K - j, :] * taps[j]
            return out, taps

        def tile(i, c):
            t0 = pl.multiple_of(i * FFN_TILE, FFN_TILE)
            gate, taps_g = conv(pg_ref, wg_ref, bg_ref, t0)
            val, taps_v = conv(pv_ref, wv_ref, bv_ref, t0)
            da = da_ref[pl.ds(t0, FFN_TILE), :]
            sg = lax.logistic(gate)
            dgate = da * val * (sg * (1.0 + gate * (1.0 - sg)))
            dval = da * gate * sg
            dg_ref[pl.ds(t0, FFN_TILE), :] = dgate
            dv_ref[pl.ds(t0, FFN_TILE), :] = dval
            dbg_ref[...] += jnp.sum(dgate, axis=0, keepdims=True)
            dbv_ref[...] += jnp.sum(dval, axis=0, keepdims=True)
            for j in range(FFN_K):
                dwg_ref[FFN_K - 1 - j:FFN_K - j, :] += jnp.sum(dgate * taps_g[j], axis=0, keepdims=True)
                dwv_ref[FFN_K - 1 - j:FFN_K - j, :] += jnp.sum(dval * taps_v[j], axis=0, keepdims=True)
            return c

        lax.fori_loop(0, nt, tile, 0)

        def tile_dx(i, c):
            t0 = pl.multiple_of(i * FFN_TILE, FFN_TILE)
            for d_ref, w_ref, dx_ref in ((dg_ref, wg_ref, dxg_ref), (dv_ref, wv_ref, dxv_ref)):
                window = d_ref[pl.ds(t0, FFN_TILE + FFN_HALO), :]
                dx = w_ref[FFN_K - 1:FFN_K, :] * window[:FFN_TILE, :]
                for j in range(1, FFN_K):
                    dx = dx + w_ref[FFN_K - 1 - j:FFN_K - j, :] * _shift_up(window, j, FFN_TILE)
                dx_ref[pl.ds(t0, FFN_TILE), :] = dx.astype(BF16)
            return c

        lax.fori_loop(0, nt, tile_dx, 0)

    gcol = lambda rows: pl.BlockSpec((rows, FFN_COLS), lambda j: (0, j))
    vcol = lambda rows: pl.BlockSpec((rows, FFN_COLS), lambda j: (0, j + N_FF_BLOCKS))
    half = lambda rows, dtype: jax.ShapeDtypeStruct((rows, D_FF), dtype)
    padded = pltpu.VMEM((s + FFN_HALO, FFN_COLS), F32)
    return pl.pallas_call(
        body,
        name=name,
        grid=(N_FF_BLOCKS,),
        in_specs=[gcol(s), vcol(s), gcol(s), gcol(FFN_K), vcol(FFN_K), gcol(1), vcol(1)],
        out_specs=[gcol(s), gcol(s), gcol(FFN_K), gcol(FFN_K), gcol(1), gcol(1)],
        out_shape=[half(s, BF16), half(s, BF16), half(FFN_K, F32), half(FFN_K, F32), half(1, F32), half(1, F32)],
        scratch_shapes=[padded, padded, padded, padded],
        compiler_params=_cparams(("parallel",)),
    )(up0, up0, dact, conv_w, conv_w, conv_b, conv_b)


MESH = pl.DeviceIdType.MESH


def _position():
    x, y, c = lax.axis_index("x"), lax.axis_index("y"), lax.axis_index("c")
    return x, y, c, 4 * x + 2 * y + c


def _peer(k):
    x, y, c, _ = _position()
    px = 1 - x if k & 4 else x
    py = 1 - y if k & 2 else y
    pc = 1 - c if k & 1 else c
    return (px, py, pc), 4 * px + 2 * py + pc


def _exchange(src, *, kind, name):
    ex = _Exchanges([(kind, src)])

    def body(src_ref, out_ref, send_sems, recv_sems, local_sems):
        ex.start([src_ref], [out_ref], send_sems, recv_sems, local_sems)
        ex.wait([src_ref], [out_ref], send_sems, recv_sems, local_sems)

    return pl.pallas_call(
        body,
        name=name,
        in_specs=ex.in_specs,
        out_specs=ex.out_specs[0],
        out_shape=ex.out_shapes[0],
        scratch_shapes=ex.scratch_shapes,
    )(src)


_HBM = pl.BlockSpec(memory_space=pltpu.HBM)
_SEM = pl.BlockSpec(memory_space=pltpu.SEMAPHORE)
_DATAFLOW = pltpu.SideEffectType.DATAFLOW_SIDE_EFFECTING
N_PEERS = N_DEV - 1


class _SplitExchange:
    def __init__(self, src, *, kind, name):
        self.kind, self.name, self.dtype = kind, name, src.dtype
        scatter = kind.startswith("scatter")
        by_blocks = kind == "scatter_blocks"
        self.scatter, self.by_blocks = scatter, by_blocks
        if by_blocks:
            self.r, self.cols, self.land_shape = None, None, src.shape
        else:
            self.r = src.shape[0] // N_DEV if scatter else src.shape[0]
            self.cols = src.shape[1]
            self.land_shape = (N_DEV, self.r, self.cols) if scatter else (N_DEV * self.r, self.cols)
        r = self.r

        def copies(src_ref, land_ref, send_sems, recv_sems):
            me = _position()[3]

            def rows(ref, idx):
                return ref.at[pl.ds(pl.multiple_of(idx * r, r), r), :]

            if by_blocks:
                outgoing = lambda idx: src_ref.at[idx]
            else:
                outgoing = (lambda idx: rows(src_ref, idx)) if scatter else (lambda idx: src_ref)
            slot = (lambda idx: land_ref.at[idx]) if scatter else (lambda idx: rows(land_ref, idx))
            sends, recvs = [], []
            for k in range(1, N_DEV):
                peer, pidx = _peer(k)
                sems = dict(send_sem=send_sems[k - 1], recv_sem=recv_sems[k - 1], device_id=peer, device_id_type=MESH)
                sends.append(pltpu.make_async_remote_copy(src_ref=outgoing(pidx), dst_ref=slot(me), **sems))
                recvs.append(pltpu.make_async_remote_copy(src_ref=outgoing(pidx), dst_ref=slot(pidx), **sems))
            return sends, recvs

        self._copies = copies
        self.src = src

    @staticmethod
    def start(exchanges, name):
        n = len(exchanges)
        per = 2 * N_PEERS

        def start_body(*refs):
            outs = refs[2 * n:]
            for i, ex in enumerate(exchanges):
                sems = outs[per * i:per * (i + 1)]
                sends, _ = ex._copies(refs[2 * i], refs[2 * i + 1], sems[:N_PEERS], sems[N_PEERS:])
                for cp in sends:
                    cp.start()
            outs[-1][...] = jnp.zeros_like(outs[-1])

        sem = pltpu.SemaphoreType.DMA(())
        operands, thru_shapes = [], []
        for ex in exchanges:
            operands += [pltpu.with_memory_space_constraint(ex.src, pltpu.HBM),
                         pltpu.with_memory_space_constraint(lax.empty(ex.land_shape, ex.dtype), pltpu.HBM)]
            thru_shapes += [pltpu.HBM(ex.src.shape, ex.dtype), pltpu.HBM(ex.land_shape, ex.dtype)]
        out = pl.pallas_call(
            start_body,
            name=name,
            in_specs=(_HBM,) * (2 * n),
            out_specs=(_SEM,) * (per * n) + (_HBM,) * (2 * n) + (pl.BlockSpec(memory_space=pltpu.VMEM),),
            out_shape=(sem,) * (per * n) + tuple(thru_shapes) + (jax.ShapeDtypeStruct((8, LANES), F32),),
            input_output_aliases={i: per * n + i for i in range(2 * n)},
            compiler_params=pltpu.CompilerParams(has_side_effects=_DATAFLOW),
        )(*operands)
        for i, ex in enumerate(exchanges):
            ex.sems = out[per * i:per * (i + 1)]
            ex.src_thru, ex.land_thru = out[per * n + 2 * i], out[per * n + 2 * i + 1]
        return out[-1][0, 0]

    def finish(self, after):
        copies = self._copies

        def wait_body(src_ref, land_ref, *rest):
            sends, recvs = copies(src_ref, land_ref, rest[:N_PEERS], rest[N_PEERS:2 * N_PEERS])
            for cp in sends:
                cp.wait_send()
            for cp in recvs:
                cp.wait_recv()

        src, got = pl.pallas_call(
            wait_body,
            name=f"{self.name}_wait",
            in_specs=(_HBM, _HBM) + (_SEM,) * (2 * N_PEERS) + (pl.BlockSpec(memory_space=pl.ANY),),
            out_specs=(_HBM, _HBM),
            out_shape=(pltpu.HBM(self.src_thru.shape, self.dtype), pltpu.HBM(self.land_shape, self.dtype)),
            input_output_aliases={0: 0, 1: 1},
            compiler_params=pltpu.CompilerParams(has_side_effects=_DATAFLOW),
        )(self.src_thru, self.land_thru, *self.sems, after)
        x, y, c = lax.axis_index("x"), lax.axis_index("y"), lax.axis_index("c")
        me = 4 * x + 2 * y + c
        if self.by_blocks:
            return lax.dynamic_update_slice(got, lax.dynamic_slice(src, (me, 0, 0), (1,) + src.shape[1:]), (me, 0, 0))
        if self.scatter:
            own = lax.dynamic_slice(src, (me * self.r, 0), (self.r, self.cols))
            return lax.dynamic_update_slice(got, own[None], (me, 0, 0))
        return lax.dynamic_update_slice(got, src, (me * self.r, 0))


class _Exchanges:
    def __init__(self, items):
        self.kinds = [kind for kind, _ in items]
        self.arrays = [src for _, src in items]
        self.out_shapes = []
        self.block_rows = []
        for kind, src in items:
            scatter, by_rows = kind.startswith("scatter"), kind.endswith("rows")
            if by_rows:
                r = src.shape[0] // N_DEV if scatter else src.shape[0]
                shape = (N_DEV, r, src.shape[1]) if scatter else (N_DEV * r, src.shape[1])
            else:
                r = None
                shape = src.shape if scatter else (N_DEV,) + src.shape
            self.block_rows.append(r)
            self.out_shapes.append(jax.ShapeDtypeStruct(shape, src.dtype))
        n = len(items)
        self.in_specs = [pl.BlockSpec(memory_space=pl.ANY)] * n
        self.out_specs = [pl.BlockSpec(memory_space=pl.ANY)] * n
        self.scratch_shapes = [pltpu.SemaphoreType.DMA((n * (N_DEV - 1),)), pltpu.SemaphoreType.DMA((n * (N_DEV - 1),)),
                               pltpu.SemaphoreType.DMA((n,))]

    def _copies(self, i, src_ref, out_ref, send_sems, recv_sems, local_sems):
        kind, r = self.kinds[i], self.block_rows[i]
        scatter, by_rows = kind.startswith("scatter"), kind.endswith("rows")
        me = _position()[3]

        def rows(ref, idx):
            return ref.at[pl.ds(pl.multiple_of(idx * r, r), r), :]

        def outgoing(idx):
            if not scatter:
                return src_ref
            return rows(src_ref, idx) if by_rows else src_ref.at[idx]

        def slot(idx):
            return rows(out_ref, idx) if (by_rows and not scatter) else out_ref.at[idx]

        local = pltpu.make_async_copy(outgoing(me), slot(me), local_sems.at[i])
        sends, recvs = [], []
        for k in range(1, N_DEV):
            peer, pidx = _peer(k)
            sem = i * (N_DEV - 1) + k - 1
            sends.append(pltpu.make_async_remote_copy(src_ref=outgoing(pidx), dst_ref=slot(me), send_sem=send_sems.at[sem],
                                                      recv_sem=recv_sems.at[sem], device_id=peer, device_id_type=MESH))
            recvs.append(pltpu.make_async_remote_copy(src_ref=outgoing(pidx), dst_ref=slot(pidx), send_sem=send_sems.at[sem],
                                                      recv_sem=recv_sems.at[sem], device_id=peer, device_id_type=MESH))
        return local, sends, recvs

    def start(self, src_refs, out_refs, *sems):
        for i, (src_ref, out_ref) in enumerate(zip(src_refs, out_refs)):
            local, sends, _ = self._copies(i, src_ref, out_ref, *sems)
            local.start()
            for cp in sends:
                cp.start()

    def wait(self, src_refs, out_refs, *sems):
        for i, (src_ref, out_ref) in enumerate(zip(src_refs, out_refs)):
            local, sends, recvs = self._copies(i, src_ref, out_ref, *sems)
            for cp in recvs:
                cp.wait_recv()
            for cp in sends:
                cp.wait_send()
            local.wait()


def _row_tile(rows):
    return _tile(rows, (256, 128, 64, 32, 16, 8))


def _layer_parts_specs(n_layers, n_parts, tr, cols):
    return [pl.BlockSpec((n_parts, tr, cols), lambda l, i, j=j: (0, jnp.where(l == j, i, 0), 0)) for j in range(n_layers)]


def _select_layer_sum(p_refs):
    l = pl.program_id(0)
    g = None
    for j, p_ref in enumerate(p_refs):
        gj = p_ref[0].astype(F32)
        for k in range(1, p_ref.shape[0]):
            gj = gj + p_ref[k].astype(F32)
        g = gj if g is None else jnp.where(l == j, gj, g)
    return g


def _sum_parts(parts, *, name):
    n_layers = len(parts)
    n_parts, rows, cols = parts[0].shape
    tr = _row_tile(rows)

    def body(*refs):
        refs[-1][...] = _select_layer_sum(refs[:n_layers])

    return pl.pallas_call(
        body,
        name=name,
        grid=(n_layers, rows // tr),
        in_specs=_layer_parts_specs(n_layers, n_parts, tr, cols),
        out_specs=pl.BlockSpec((None, tr, cols), lambda l, i: (l, i, 0)),
        out_shape=jax.ShapeDtypeStruct((n_layers, rows, cols), F32),
        compiler_params=_cparams(("arbitrary", "arbitrary")),
    )(*parts)


def _adamw(parts, w, m, v, *, name):
    n_layers, rows, cols = w.shape
    summed = not isinstance(parts, (list, tuple))
    tr = _row_tile(rows)
    n_in = 1 if summed else n_layers

    def body(*refs):
        w_ref, m_ref, v_ref, g_ref, d_ref, m2_ref, v2_ref = refs[n_in:]
        g = refs[0][...] if summed else _select_layer_sum(refs[:n_in])
        m2 = ADAM_B1 * m_ref[...] + (1.0 - ADAM_B1) * g
        v2 = ADAM_B2 * v_ref[...] + (1.0 - ADAM_B2) * (g * g)
        m_hat = m2 / (1.0 - ADAM_B1 ** ADAM_STEP)
        v_hat = v2 / (1.0 - ADAM_B2 ** ADAM_STEP)
        g_ref[...] = g
        d_ref[...] = -ADAM_LR * (m_hat / (jnp.sqrt(v_hat) + ADAM_EPS) + ADAM_WD * w_ref[...])
        m2_ref[...] = m2
        v2_ref[...] = v2

    slab = pl.BlockSpec((None, tr, cols), lambda l, i: (l, i, 0))
    out = jax.ShapeDtypeStruct((n_layers, rows, cols), F32)
    p_specs = [slab] if summed else _layer_parts_specs(n_layers, parts[0].shape[0], tr, cols)
    return pl.pallas_call(
        body,
        name=name,
        grid=(n_layers, rows // tr),
        in_specs=p_specs + [slab, slab, slab],
        out_specs=[slab, slab, slab, slab],
        out_shape=[out, out, out, out],
        compiler_params=_cparams(("arbitrary", "arbitrary")),
    )(*((parts,) if summed else tuple(parts)), w, m, v)


SLAB_ROWS = 256
_SMALL_SHARDED = (("conv_w", (2, 31, 32)), ("ffn_conv_w", (2, 3, 704)))
_REPLICATED = (("g_mix", (2, 1024)), ("conv_b", (2, 256)), ("conv_ln_g", (2, 256)), ("conv_ln_b", (2, 256)),
               ("sgu_ln_g", (2, 256)), ("sgu_ln_b", (2, 256)), ("sgu_w", (2, 4, 128, 128)), ("sgu_b", (2, 4, 128)),
               ("g_out", (2, 1024)), ("g_ffn", (2, 1024)), ("ffn_conv_b", (2, 5632)), ("g_final", (1024,)))


def _seg_rows(n_elems):
    return -(-n_elems // LANES)


def _pack(arrays, lead=()):
    segs = []
    for a in arrays:
        flat = a.reshape(lead + (-1,)).astype(F32)
        pad = _seg_rows(flat.shape[-1]) * LANES - flat.shape[-1]
        if pad:
            flat = jnp.pad(flat, [(0, 0)] * len(lead) + [(0, pad)])
        segs.append(flat)
    flat = jnp.concatenate(segs, axis=-1)
    rows = flat.shape[-1] // LANES
    pad_rows = -rows % SLAB_ROWS
    if pad_rows:
        flat = jnp.pad(flat, [(0, 0)] * len(lead) + [(0, pad_rows * LANES)])
    return flat.reshape(lead + (rows + pad_rows, LANES))


def _unpack(slab, shapes, lead=()):
    flat = slab.reshape(lead + (-1,))
    out, off = [], 0
    for shape in shapes:
        n = math.prod(shape)
        out.append(flat[..., off:off + n].reshape(lead + tuple(shape)))
        off += _seg_rows(n) * LANES
    return out


def _split_last(full):
    split = full.shape[:-1] + (N_DEV, full.shape[-1] // N_DEV)
    return jnp.moveaxis(full.reshape(split), -2, 0)


def _join_last(blocks):
    moved = jnp.moveaxis(blocks, 0, -2)
    return moved.reshape(moved.shape[:-2] + (moved.shape[-2] * moved.shape[-1],))


def _gathered(wt, n, l, after):
    if isinstance(wt[n][l], _SplitExchange):
        wt[n][l] = wt[n][l].finish(after)
    return wt[n][l]


def _layer_fwd(l, x, wt, small):
    tag = f"l{l}"
    h = _rmsnorm_fwd(x, small["g_mix"][l][None], name=f"{tag}_norm_mix")
    w_in_t = _gathered(wt, "w_in_t", l, h)
    p_ab = _matmul(h, w_in_t, "nt", name=f"{tag}_proj_ab", n=D_AB)
    qkv = _matmul(h, w_in_t, "nt", name=f"{tag}_proj_qkv", n=D_QKV, b_n0=D_AB, out_dtype=BF16)
    ya = _mixer_a_fwd(p_ab, wt["conv_w"][l], small["conv_b"][l][None], small["conv_ln_g"][l][None],
                      small["conv_ln_b"][l][None], name=f"{tag}_mixer_a")
    bias = jnp.repeat(small["sgu_b"][l].T, HEAD_DIM, axis=1)
    yb = _sgu_fwd(p_ab, small["sgu_ln_g"][l][None], small["sgu_ln_b"][l][None], small["sgu_w"][l], bias,
                  name=f"{tag}_sgu")
    yc, t_tot = _attn_fwd(qkv, name=f"{tag}_attn")
    y = _combine_fwd(ya, yb, yc, small["g_out"][l][None], name=f"{tag}_combine")
    x1 = _matmul(y, _gathered(wt, "w_out", l, y), "nn", name=f"{tag}_out_proj", residual=x)
    h2 = _rmsnorm_fwd(x1, small["g_ffn"][l][None], name=f"{tag}_norm_ffn")
    up0 = _matmul(h2, _gathered(wt, "w_up_t", l, h2), "nt", name=f"{tag}_up")
    act = _ffn_act_fwd(up0, wt["ffn_conv_w"][l], small["ffn_conv_b"][l][None], name=f"{tag}_ffn_act")
    x2 = _matmul(act, _gathered(wt, "w_down", l, act), "nn", name=f"{tag}_down", residual=x1)
    saved = dict(x=x, h=h, p_ab=p_ab, qkv=qkv, ya=ya, yb=yb, yc=yc, t_tot=t_tot, y=y, x1=x1, h2=h2, up0=up0,
                 act=act, bias=bias)
    return x2, saved


def _layer_bwd(l, dres, sv, wt, small, scattering):
    tag = f"l{l}b"
    g = {}

    def scatter(n, partial):
        scattering[n][l] = _SplitExchange(partial, kind="scatter_rows", name=f"scatter_{n}_l{l}")
        return _SplitExchange.start([scattering[n][l]], name=f"scatter_{n}_l{l}_start")

    dx2, dx2_b = dres
    dact = _matmul(dx2_b, wt["w_down"][l], "nt", name=f"{tag}_dact")
    tok = scatter("w_down", _matmul(sv["act"], dx2_b, "tn", name=f"{tag}_dw_down", out_dtype=BF16))
    dup_g, dup_v, dwg, dwv, dbg, dbv = _ffn_act_bwd(sv["up0"], dact, wt["ffn_conv_w"][l], small["ffn_conv_b"][l][None] + tok,
                                                    name=f"{tag}_ffn_act")
    g["ffn_conv_w"] = jnp.concatenate([dwg, dwv], axis=1)
    g["ffn_conv_b"] = jnp.concatenate([dbg[0], dbv[0]])
    dh2 = _matmul(dup_g, wt["w_up_t"][l], "nn", name=f"{tag}_dh2_gate")
    dh2 = _matmul(dup_v, wt["w_up_t"][l], "nn", name=f"{tag}_dh2_val", b_k0=D_FF, residual=dh2)
    dw_up = _matmul(dup_g, sv["h2"], "tn", name=f"{tag}_dw_up_gate", out_dtype=BF16, rows=(2 * D_FF, 0))
    dw_up = _matmul(dup_v, sv["h2"], "tn", name=f"{tag}_dw_up_val", out_dtype=BF16, rows=(2 * D_FF, D_FF), into=dw_up)
    tok = scatter("w_up_t", dw_up)
    dx1, dx1_b, dg = _rmsnorm_bwd(sv["x1"], small["g_ffn"][l][None] + tok, dh2, dx2, name=f"{tag}_norm_ffn")
    g["g_ffn"] = dg[0]
    dy = _matmul(dx1_b, wt["w_out"][l], "nt", name=f"{tag}_dy")
    tok = scatter("w_out", _matmul(sv["y"], dx1_b, "tn", name=f"{tag}_dw_out", out_dtype=BF16))
    dya, dyb, dyc, dg = _combine_bwd(dy, sv["ya"], sv["yb"], sv["yc"], small["g_out"][l][None] + tok,
                                     name=f"{tag}_combine")
    g["g_out"] = dg[0]
    dq, dk, dv = _attn_bwd(sv["qkv"], sv["t_tot"], dyc, name=f"{tag}_attn")
    dp_b, g["sgu_w"], db, dg, dbeta = _sgu_bwd(sv["p_ab"], dyb, small["sgu_ln_g"][l][None], small["sgu_ln_b"][l][None],
                                               small["sgu_w"][l], sv["bias"], name=f"{tag}_sgu")
    g["sgu_b"] = db[:, :N_SGU_HEADS].T
    g["sgu_ln_g"], g["sgu_ln_b"] = dg[0], dbeta[0]
    dp_a, g["conv_w"], dcb, dg, dbeta = _mixer_a_bwd(sv["p_ab"], dya, wt["conv_w"][l], small["conv_b"][l][None],
                                                     small["conv_ln_g"][l][None], small["conv_ln_b"][l][None],
                                                     name=f"{tag}_mixer_a")
    g["conv_b"], g["conv_ln_g"], g["conv_ln_b"] = dcb[0], dg[0], dbeta[0]
    dp = jnp.concatenate([dp_a.astype(BF16), dp_b.astype(BF16), dq.astype(BF16), dk.astype(BF16), dv.astype(BF16)],
                         axis=1)
    dh = _matmul(dp, wt["w_in_t"][l], "nn", name=f"{tag}_dh")
    tok = scatter("w_in_t", _matmul(dp, sv["h"], "tn", name=f"{tag}_dw_in", out_dtype=BF16))
    dx, dx_b, dg = _rmsnorm_bwd(sv["x"], small["g_mix"][l][None] + tok, dh, dx1, name=f"{tag}_norm_mix")
    g["g_mix"] = dg[0]
    return (dx, dx_b), g


_BIG = ("w_in_t", "w_out", "w_up_t", "w_down")


def kernel(x, g_mix, w_in, conv_w, conv_b, conv_ln_g, conv_ln_b, sgu_ln_g, sgu_ln_b, sgu_w, sgu_b, g_out, w_out, g_ffn, w_up, ffn_conv_w, ffn_conv_b, w_down, g_final, loss_target, m_g_mix, m_w_in, m_conv_w, m_conv_b, m_conv_ln_g, m_conv_ln_b, m_sgu_ln_g, m_sgu_ln_b, m_sgu_w, m_sgu_b, m_g_out, m_w_out, m_g_ffn, m_w_up, m_ffn_conv_w, m_ffn_conv_b, m_w_down, m_g_final, v_g_mix, v_w_in, v_conv_w, v_conv_b, v_conv_ln_g, v_conv_ln_b, v_sgu_ln_g, v_sgu_ln_b, v_sgu_w, v_sgu_b, v_g_out, v_w_out, v_g_ffn, v_w_up, v_ffn_conv_w, v_ffn_conv_b, v_w_down, v_g_final):
    given = dict(locals())
    n_layers = g_mix.shape[0]
    layers = range(n_layers)
    small_sharded = [n for n, _ in _SMALL_SHARDED]
    replicated = [n for n, _ in _REPLICATED]
    small = {n: given[n] for n in replicated}

    filters = _exchange(_pack([given[n] for n in small_sharded]), kind="gather_blocks", name="gather_filters")
    filters, first_shard = lax.optimization_barrier((filters, w_in[0].T.astype(BF16)))
    wt = {n: [None] * n_layers for n in _BIG}
    wt["w_in_t"][0] = _SplitExchange(first_shard, kind="gather_rows", name="gather_w_in_t_l0")
    tok = _SplitExchange.start([wt["w_in_t"][0]], name="gather_w_in_t_l0_start")
    w_in, w_out, w_up, w_down, tok = lax.optimization_barrier((w_in, w_out, w_up, w_down, tok))
    shard = {"w_in_t": [w_in[l].T.astype(BF16) for l in layers], "w_out": [w_out[l].astype(BF16) for l in layers],
             "w_up_t": [w_up[l].T.astype(BF16) for l in layers], "w_down": [w_down[l].astype(BF16) for l in layers]}
    later = [(n, l) for l in layers for n in _BIG if (n, l) != ("w_in_t", 0)]
    for n, l in later:
        wt[n][l] = _SplitExchange(shard[n][l], kind="gather_rows", name=f"gather_{n}_l{l}")
    small["g_mix"] = g_mix + tok + _SplitExchange.start([wt[n][l] for n, l in later], name="gather_weights_start")
    for n, blocks in zip(small_sharded, _unpack(filters, [s for _, s in _SMALL_SHARDED], lead=(N_DEV,))):
        wt[n] = _join_last(blocks)

    xs = x[0]
    saved = []
    for l in layers:
        xs, sv = _layer_fwd(l, xs, wt, small)
        saved.append(sv)
    loss_tile, dx, dx_b, dgf = _loss_head(xs, g_final[None], loss_target[0], name="loss_head")
    dres = (dx, dx_b)
    scattering = {n: [None] * n_layers for n in _BIG}
    grads = [None] * n_layers
    for l in reversed(layers):
        dres, grads[l] = _layer_bwd(l, dres, saved[l], wt, small, scattering)
    partial = {n: jnp.stack([g[n] for g in grads]) for n in grads[0]}
    partial["g_final"] = dgf[0]

    own = _pack([_split_last(partial[n]) for n in small_sharded], lead=(N_DEV,))
    shared = _pack([partial[n] for n in replicated])
    slab = jnp.concatenate([own, jnp.broadcast_to(shared[None], (N_DEV,) + shared.shape)], axis=1)
    small_grads = _SplitExchange(slab, kind="scatter_blocks", name="scatter_small_grads")
    tok = _SplitExchange.start([small_grads], name="scatter_small_grads_start")
    dx_out, tok = lax.optimization_barrier((dres[0], tok))
    received = {n: [scattering[n][l].finish(dx_out) for l in layers] for n in _BIG}
    out = {}

    def update(n, parts):
        results = _adamw(parts, given[n], given["m_" + n], given["v_" + n], name=f"adamw_{n}")
        for pre, res in zip(("grad_", "delta_", "new_m_", "new_v_"), results):
            out[pre + n] = res

    update("w_out", received["w_out"])
    update("w_down", received["w_down"])
    for n in ("w_in", "w_up"):
        update(n, jnp.swapaxes(_sum_parts(received[n + "_t"], name=f"sum_{n}"), 1, 2))

    slab = small_grads.finish(out["grad_w_up"])
    stacks = [jnp.concatenate([_pack([given[pre + n] for n in small_sharded]),
                               _pack([given[pre + n] for n in replicated])])[None] for pre in ("", "m_", "v_")]
    results = _adamw([slab], *stacks, name="adamw_small")
    n_own = own.shape[1]
    for pre, res in zip(("grad_", "delta_", "new_m_", "new_v_"), results):
        unpacked = (_unpack(res[0, :n_own], [s for _, s in _SMALL_SHARDED])
                    + _unpack(res[0, n_own:], [s for _, s in _REPLICATED]))
        for n, a in zip(small_sharded + replicated, unpacked):
            out[pre + n] = a

    loss = lax.psum(loss_tile[0, 0], ("x", "y", "c"))
    order = list(_WEIGHT_ORDER)
    return (loss, dres[0][None], *[out["grad_" + n] for n in order], *[out["delta_" + n] for n in order],
            *[out["new_m_" + n] for n in order], *[out["new_v_" + n] for n in order])


_WEIGHT_ORDER = ("g_mix", "w_in", "conv_w", "conv_b", "conv_ln_g", "conv_ln_b", "sgu_ln_g", "sgu_ln_b", "sgu_w", "sgu_b",
                 "g_out", "w_out", "g_ffn", "w_up", "ffn_conv_w", "ffn_conv_b", "w_down", "g_final")
```

```python
import functools
import math

import jax
import jax.numpy as jnp
from jax import lax
from jax.experimental import pallas as pl
from jax.experimental.pallas import tpu as pltpu

F32 = jnp.float32
BF16 = jnp.bfloat16

N_DEV = 8
D_MODEL = 1024
HEAD_DIM = 64
D_CONV = 256
D_SGU = 256
D_SB = 512
D_AB = 2 * D_CONV + 2 * D_SGU
D_QKV = 3 * D_SB
D_IN = D_AB + D_QKV
CONV_K = 31
CONV_HALO = 32
FFN_K = 3
FFN_HALO = 8
D_FF = 2816
CHUNK = 128
EPS = 1e-6
LANES = 128

ADAM_LR = 0.001
ADAM_B1 = 0.9
ADAM_B2 = 0.999
ADAM_EPS = 1e-08
ADAM_WD = 0.01
ADAM_STEP = 10

VMEM_LIMIT = 56 * 1024 * 1024


def _cparams(sem=None):
    return pltpu.CompilerParams(dimension_semantics=sem, vmem_limit_bytes=VMEM_LIMIT)


def _tile(n, prefs=(512, 256, 128)):
    for t in prefs:
        if n % t == 0:
            return t
    return n


def _sigmoid(x):
    return 1.0 / (1.0 + jnp.exp(-x))


def _softplus(x):
    return jnp.maximum(x, 0.0) + jnp.log1p(jnp.exp(-jnp.abs(x)))


_INV_SQRT2 = 1.0 / math.sqrt(2.0)
_INV_SQRT2PI = 1.0 / math.sqrt(2.0 * math.pi)


def _gelu(x):
    return 0.5 * x * (1.0 + lax.erf(x * _INV_SQRT2))


def _gelu_grad(x):
    return 0.5 * (1.0 + lax.erf(x * _INV_SQRT2)) + x * jnp.exp(-0.5 * x * x) * _INV_SQRT2PI


def _dot(a, b, dims):
    return lax.dot_general(a, b, (dims, ((), ())), preferred_element_type=F32)


_NN = ((1,), (0,))
_NT = ((1,), (1,))
_TN = ((0,), (0,))


def _split_bf16(x):
    hi = x.astype(BF16)
    lo = (x - hi.astype(F32)).astype(BF16)
    return jnp.concatenate([hi, lo], axis=1)


def _matmul(a, b, mode, *, name, out_dtype=F32, residual=None, n=None, b_n0=0, b_k0=0, rows=None, into=None):
    if mode == "nn":
        (m, k), n = a.shape, (n or b.shape[1])
    elif mode == "nt":
        (m, k), n = a.shape, (n or b.shape[0])
    else:
        (k, m), n = a.shape, b.shape[1]
    has_res = residual is not None
    tm, tn = _matmul_tiles(m, n, k, a.dtype.itemsize, b.dtype.itemsize, jnp.dtype(out_dtype).itemsize, has_res, b_n0)
    j0 = b_n0 // tn
    total_rows, first_row = rows or (m, 0)
    assert b_k0 % k == 0 and first_row % tm == 0
    kb, i0 = b_k0 // k, first_row // tm

    if mode == "nn":
        a_spec = pl.BlockSpec((tm, k), lambda i, j: (i, 0))
        b_spec = pl.BlockSpec((k, tn), lambda i, j: (kb, j + j0))
        dims = _NN
    elif mode == "nt":
        a_spec = pl.BlockSpec((tm, k), lambda i, j: (i, 0))
        b_spec = pl.BlockSpec((tn, k), lambda i, j: (j + j0, 0))
        dims = _NT
    else:
        a_spec = pl.BlockSpec((k, tm), lambda i, j: (0, i))
        b_spec = pl.BlockSpec((k, tn), lambda i, j: (0, j))
        dims = _TN
    o_spec = pl.BlockSpec((tm, tn), lambda i, j: (i + i0, j))
    r_spec = pl.BlockSpec((tm, tn), lambda i, j: (i, j))

    def body(*refs):
        a_ref, b_ref = refs[:2]
        acc = _dot(a_ref[...].astype(BF16), b_ref[...].astype(BF16), dims)
        if has_res:
            acc = acc + refs[2][...]
        refs[-1][...] = acc.astype(out_dtype)

    in_specs = [a_spec, b_spec] + ([r_spec] if has_res else [])
    args = (a, b) + ((residual,) if has_res else ())
    aliases = {}
    if into is not None:
        aliases = {len(args): 0}
        in_specs.append(pl.BlockSpec(memory_space=pl.ANY))
        args += (into,)

        def body(*refs, inner=body):
            inner(*refs[:len(args) - 1], refs[-1])

    return pl.pallas_call(
        body,
        name=name,
        grid=(m // tm, n // tn),
        in_specs=in_specs,
        out_specs=o_spec,
        out_shape=jax.ShapeDtypeStruct((total_rows, n), out_dtype),
        input_output_aliases=aliases,
        compiler_params=_cparams(("parallel", "parallel")),
    )(*args)


MATMUL_VMEM_BUDGET = 40 * 1024 * 1024


def _matmul_tiles(m, n, k, a_bytes, b_bytes, out_bytes, has_res, n_offset):
    def divisors(size, cap, also=0):
        return [t for t in range(cap, 0, -LANES) if size % t == 0 and also % t == 0] or [size]

    for tm in divisors(m, 1024):
        for tn in divisors(n, 1408, n_offset):
            blocks = tm * k * a_bytes + k * tn * b_bytes + tm * tn * (out_bytes + (4 if has_res else 0))
            if 2 * blocks <= MATMUL_VMEM_BUDGET:
                return tm, tn
    raise ValueError(f"no matmul tiling for {m} x {n} x {k}")


ROW_TILE = 256


def _rmsnorm_fwd(x, g, *, name):
    s, d = x.shape

    def body(x_ref, g_ref, h_ref):
        xv = x_ref[...]
        r = lax.rsqrt(jnp.mean(xv * xv, axis=-1, keepdims=True) + EPS)
        h_ref[...] = (xv * r * g_ref[...]).astype(BF16)

    return pl.pallas_call(
        body,
        name=name,
        grid=(s // ROW_TILE,),
        in_specs=[pl.BlockSpec((ROW_TILE, d), lambda i: (i, 0)), pl.BlockSpec((1, d), lambda i: (0, 0))],
        out_specs=pl.BlockSpec((ROW_TILE, d), lambda i: (i, 0)),
        out_shape=jax.ShapeDtypeStruct((s, d), BF16),
        compiler_params=_cparams(("parallel",)),
    )(x, g)


def _rmsnorm_bwd(x, g, dh, dres, *, name):
    s, d = x.shape

    def body(x_ref, g_ref, dh_ref, dres_ref, dx_ref, dxb_ref, dg_ref):
        xv = x_ref[...]
        r = lax.rsqrt(jnp.mean(xv * xv, axis=-1, keepdims=True) + EPS)
        xhat = xv * r
        dhv = dh_ref[...]
        dxhat = dhv * g_ref[...]
        dx = dres_ref[...] + r * (dxhat - xhat * jnp.mean(dxhat * xhat, axis=-1, keepdims=True))
        dx_ref[...] = dx
        dxb_ref[...] = dx.astype(BF16)
        part = jnp.sum(dhv * xhat, axis=0, keepdims=True)

        @pl.when(pl.program_id(0) == 0)
        def _():
            dg_ref[...] = part

        @pl.when(pl.program_id(0) > 0)
        def _():
            dg_ref[...] += part

    row = pl.BlockSpec((ROW_TILE, d), lambda i: (i, 0))
    vec = pl.BlockSpec((1, d), lambda i: (0, 0))
    return pl.pallas_call(
        body,
        name=name,
        grid=(s // ROW_TILE,),
        in_specs=[row, vec, row, row],
        out_specs=[row, row, vec],
        out_shape=[jax.ShapeDtypeStruct((s, d), F32), jax.ShapeDtypeStruct((s, d), BF16),
                   jax.ShapeDtypeStruct((1, d), F32)],
        compiler_params=_cparams(("arbitrary",)),
    )(x, g, dh, dres)


def _loss_head(x, g, target, *, name):
    s, d = x.shape

    def body(x_ref, g_ref, t_ref, loss_ref, dx_ref, dxb_ref, dg_ref):
        xv = x_ref[...]
        gv = g_ref[...]
        r = lax.rsqrt(jnp.mean(xv * xv, axis=-1, keepdims=True) + EPS)
        xhat = xv * r
        diff = xhat * gv - t_ref[...]
        dy = diff * (1.0 / d)
        dxhat = dy * gv
        dx = r * (dxhat - xhat * jnp.mean(dxhat * xhat, axis=-1, keepdims=True))
        dx_ref[...] = dx
        dxb_ref[...] = dx.astype(BF16)
        dg_part = jnp.sum(dy * xhat, axis=0, keepdims=True)
        row_loss = jnp.sum(diff * diff, axis=-1, keepdims=True)
        loss_part = jnp.sum(row_loss, axis=0, keepdims=True) * (0.5 / d)

        @pl.when(pl.program_id(0) == 0)
        def _():
            dg_ref[...] = dg_part
            loss_ref[...] = jnp.broadcast_to(loss_part, loss_ref.shape)

        @pl.when(pl.program_id(0) > 0)
        def _():
            dg_ref[...] += dg_part
            loss_ref[...] += jnp.broadcast_to(loss_part, loss_ref.shape)

    row = pl.BlockSpec((ROW_TILE, d), lambda i: (i, 0))
    vec = pl.BlockSpec((1, d), lambda i: (0, 0))
    tile = pl.BlockSpec((8, LANES), lambda i: (0, 0))
    return pl.pallas_call(
        body,
        name=name,
        grid=(s // ROW_TILE,),
        in_specs=[row, vec, row],
        out_specs=[tile, row, row, vec],
        out_shape=[jax.ShapeDtypeStruct((8, LANES), F32), jax.ShapeDtypeStruct((s, d), F32),
                   jax.ShapeDtypeStruct((s, d), BF16), jax.ShapeDtypeStruct((1, d), F32)],
        compiler_params=_cparams(("arbitrary",)),
    )(x, g, target)


_BRANCHES = ((0, D_CONV), (D_CONV, D_SGU), (D_CONV + D_SGU, D_SB))


def _combine_fwd(ya, yb, yc, g, *, name):
    s = ya.shape[0]

    def body(ya_ref, yb_ref, yc_ref, g_ref, y_ref):
        for ref, (off, w) in zip((ya_ref, yb_ref, yc_ref), _BRANCHES):
            v = ref[...]
            r = lax.rsqrt(jnp.mean(v * v, axis=-1, keepdims=True) + EPS)
            y_ref[:, off:off + w] = (v * r * g_ref[:, off:off + w]).astype(BF16)

    def row(w):
        return pl.BlockSpec((ROW_TILE, w), lambda i: (i, 0))

    return pl.pallas_call(
        body,
        name=name,
        grid=(s // ROW_TILE,),
        in_specs=[row(D_CONV), row(D_SGU), row(D_SB), pl.BlockSpec((1, D_MODEL), lambda i: (0, 0))],
        out_specs=row(D_MODEL),
        out_shape=jax.ShapeDtypeStruct((s, D_MODEL), BF16),
        compiler_params=_cparams(("parallel",)),
    )(ya, yb, yc, g)


def _combine_bwd(dy, ya, yb, yc, g, *, name):
    s = ya.shape[0]

    def body(dy_ref, ya_ref, yb_ref, yc_ref, g_ref, dya_ref, dyb_ref, dyc_ref, dg_ref):
        first = pl.program_id(0) == 0
        for ref, dref, (off, w) in zip((ya_ref, yb_ref, yc_ref), (dya_ref, dyb_ref, dyc_ref), _BRANCHES):
            v = ref[...]
            r = lax.rsqrt(jnp.mean(v * v, axis=-1, keepdims=True) + EPS)
            n = v * r
            dout = dy_ref[:, off:off + w]
            dn = dout * g_ref[:, off:off + w]
            dref[...] = r * (dn - n * jnp.mean(dn * n, axis=-1, keepdims=True))
            part = jnp.sum(dout * n, axis=0, keepdims=True)

            @pl.when(first)
            def _():
                dg_ref[:, off:off + w] = part

            @pl.when(jnp.logical_not(first))
            def _():
                dg_ref[:, off:off + w] += part

    def row(w):
        return pl.BlockSpec((ROW_TILE, w), lambda i: (i, 0))

    vec = pl.BlockSpec((1, D_MODEL), lambda i: (0, 0))
    return pl.pallas_call(
        body,
        name=name,
        grid=(s // ROW_TILE,),
        in_specs=[row(D_MODEL), row(D_CONV), row(D_SGU), row(D_SB), vec],
        out_specs=[row(D_CONV), row(D_SGU), row(D_SB), vec],
        out_shape=[jax.ShapeDtypeStruct((s, D_CONV), F32), jax.ShapeDtypeStruct((s, D_SGU), F32),
                   jax.ShapeDtypeStruct((s, D_SB), F32), jax.ShapeDtypeStruct((1, D_MODEL), F32)],
        compiler_params=_cparams(("arbitrary",)),
    )(dy, ya, yb, yc, g)


CONV_TILE = 128


def _shift_down(window, j, halo):
    return pltpu.roll(window, j, 0)[halo:, :] if j else window[halo:, :]


def _shift_up(window, j, n_out):
    n = window.shape[0]
    return pltpu.roll(window, n - j, 0)[:n_out, :] if j else window[:n_out, :]


def _mixer_a_fwd(p_ab, conv_w, conv_b, ln_g, ln_b, *, name):
    s = p_ab.shape[0]
    nt = s // CONV_TILE

    def body(p_ref, w_ref, b_ref, g_ref, beta_ref, y_ref, h_ref):
        h_ref[0:CONV_HALO, :] = jnp.zeros((CONV_HALO, D_CONV), F32)

        def glu(i, c):
            t0 = pl.multiple_of(i * CONV_TILE, CONV_TILE)
            a = p_ref[pl.ds(t0, CONV_TILE), 0:D_CONV]
            gate = p_ref[pl.ds(t0, CONV_TILE), D_CONV:2 * D_CONV]
            h_ref[pl.ds(t0 + CONV_HALO, CONV_TILE), :] = a * _sigmoid(gate)
            return c

        lax.fori_loop(0, nt, glu, 0)

        def conv(i, c):
            t0 = pl.multiple_of(i * CONV_TILE, CONV_TILE)
            window = h_ref[pl.ds(t0, CONV_TILE + CONV_HALO), :]
            acc = jnp.zeros((CONV_TILE, D_CONV), F32) + b_ref[...]
            for k in range(CONV_K):
                acc = acc + w_ref[k:k + 1, :] * _shift_down(window, CONV_K - 1 - k, CONV_HALO)
            mu = jnp.mean(acc, axis=-1, keepdims=True)
            xc = acc - mu
            rstd = lax.rsqrt(jnp.mean(xc * xc, axis=-1, keepdims=True) + EPS)
            z = xc * rstd * g_ref[...] + beta_ref[...]
            y_ref[pl.ds(t0, CONV_TILE), :] = z * _sigmoid(z)
            return c

        lax.fori_loop(0, nt, conv, 0)

    full = lambda shape: pl.BlockSpec(shape, lambda i: (0, 0))
    return pl.pallas_call(
        body,
        name=name,
        grid=(1,),
        in_specs=[full((s, 2 * D_CONV)), full((CONV_K, D_CONV)), full((1, D_CONV)), full((1, D_CONV)),
                  full((1, D_CONV))],
        out_specs=full((s, D_CONV)),
        out_shape=jax.ShapeDtypeStruct((s, D_CONV), F32),
        scratch_shapes=[pltpu.VMEM((s + CONV_HALO, D_CONV), F32)],
        compiler_params=_cparams(("arbitrary",)),
    )(p_ab, conv_w, conv_b, ln_g, ln_b)


def _mixer_a_bwd(p_ab, dya, conv_w, conv_b, ln_g, ln_b, *, name):
    s = p_ab.shape[0]
    nt = s // CONV_TILE

    def body(p_ref, dy_ref, w_ref, b_ref, g_ref, beta_ref, dp_ref, dw_ref, db_ref, dg_ref, dbeta_ref, h_ref, dc_ref):
        h_ref[0:CONV_HALO, :] = jnp.zeros((CONV_HALO, D_CONV), F32)
        dc_ref[s:s + CONV_HALO, :] = jnp.zeros((CONV_HALO, D_CONV), F32)
        dw_ref[...] = jnp.zeros_like(dw_ref)
        db_ref[...] = jnp.zeros_like(db_ref)
        dg_ref[...] = jnp.zeros_like(dg_ref)
        dbeta_ref[...] = jnp.zeros_like(dbeta_ref)

        def glu(i, c):
            t0 = pl.multiple_of(i * CONV_TILE, CONV_TILE)
            a = p_ref[pl.ds(t0, CONV_TILE), 0:D_CONV]
            gate = p_ref[pl.ds(t0, CONV_TILE), D_CONV:2 * D_CONV]
            h_ref[pl.ds(t0 + CONV_HALO, CONV_TILE), :] = a * _sigmoid(gate)
            return c

        lax.fori_loop(0, nt, glu, 0)

        def conv_bwd(i, c):
            t0 = pl.multiple_of(i * CONV_TILE, CONV_TILE)
            window = h_ref[pl.ds(t0, CONV_TILE + CONV_HALO), :]
            taps = [_shift_down(window, CONV_K - 1 - k, CONV_HALO) for k in range(CONV_K)]
            acc = jnp.zeros((CONV_TILE, D_CONV), F32) + b_ref[...]
            for k in range(CONV_K):
                acc = acc + w_ref[k:k + 1, :] * taps[k]
            mu = jnp.mean(acc, axis=-1, keepdims=True)
            xc = acc - mu
            rstd = lax.rsqrt(jnp.mean(xc * xc, axis=-1, keepdims=True) + EPS)
            xhat = xc * rstd
            z = xhat * g_ref[...] + beta_ref[...]
            sg = _sigmoid(z)
            dz = dy_ref[pl.ds(t0, CONV_TILE), :] * (sg * (1.0 + z * (1.0 - sg)))
            dg_ref[...] += jnp.sum(dz * xhat, axis=0, keepdims=True)
            dbeta_ref[...] += jnp.sum(dz, axis=0, keepdims=True)
            dxhat = dz * g_ref[...]
            dc = rstd * (dxhat - jnp.mean(dxhat, axis=-1, keepdims=True)
                         - xhat * jnp.mean(dxhat * xhat, axis=-1, keepdims=True))
            dc_ref[pl.ds(t0, CONV_TILE), :] = dc
            db_ref[...] += jnp.sum(dc, axis=0, keepdims=True)
            for k in range(CONV_K):
                dw_ref[k:k + 1, :] += jnp.sum(dc * taps[k], axis=0, keepdims=True)
            return c

        lax.fori_loop(0, nt, conv_bwd, 0)

        def glu_bwd(i, c):
            t0 = pl.multiple_of(i * CONV_TILE, CONV_TILE)
            window = dc_ref[pl.ds(t0, CONV_TILE + CONV_HALO), :]
            dh = jnp.zeros((CONV_TILE, D_CONV), F32)
            for j in range(CONV_K):
                dh = dh + w_ref[CONV_K - 1 - j:CONV_K - j, :] * _shift_up(window, j, CONV_TILE)
            a = p_ref[pl.ds(t0, CONV_TILE), 0:D_CONV]
            sg = _sigmoid(p_ref[pl.ds(t0, CONV_TILE), D_CONV:2 * D_CONV])
            dp_ref[pl.ds(t0, CONV_TILE), 0:D_CONV] = dh * sg
            dp_ref[pl.ds(t0, CONV_TILE), D_CONV:2 * D_CONV] = dh * a * sg * (1.0 - sg)
            return c

        lax.fori_loop(0, nt, glu_bwd, 0)

    full = lambda shape: pl.BlockSpec(shape, lambda i: (0, 0))
    vec = jax.ShapeDtypeStruct((1, D_CONV), F32)
    return pl.pallas_call(
        body,
        name=name,
        grid=(1,),
        in_specs=[full((s, 2 * D_CONV)), full((s, D_CONV)), full((CONV_K, D_CONV)), full((1, D_CONV)),
                  full((1, D_CONV)), full((1, D_CONV))],
        out_specs=[full((s, 2 * D_CONV)), full((CONV_K, D_CONV)), full((1, D_CONV)), full((1, D_CONV)),
                   full((1, D_CONV))],
        out_shape=[jax.ShapeDtypeStruct((s, 2 * D_CONV), F32), jax.ShapeDtypeStruct((CONV_K, D_CONV), F32),
                   vec, vec, vec],
        scratch_shapes=[pltpu.VMEM((s + CONV_HALO, D_CONV), F32), pltpu.VMEM((s + CONV_HALO, D_CONV), F32)],
        compiler_params=_cparams(("arbitrary",)),
    )(p_ab, dya, conv_w, conv_b, ln_g, ln_b)


N_SGU_HEADS = D_SGU // HEAD_DIM


def _head_masks(width):
    lane = lax.broadcasted_iota(jnp.int32, (1, width), 1)
    return [(lane >= h * HEAD_DIM) & (lane < (h + 1) * HEAD_DIM) for h in range(width // HEAD_DIM)]


def _tril_mask():
    r = lax.broadcasted_iota(jnp.int32, (CHUNK, CHUNK), 0)
    c = lax.broadcasted_iota(jnp.int32, (CHUNK, CHUNK), 1)
    return c <= r


def _sgu_norm(bv, g, beta):
    vg = _gelu(bv)
    mu = jnp.mean(vg, axis=-1, keepdims=True)
    xc = vg - mu
    rstd = lax.rsqrt(jnp.mean(xc * xc, axis=-1, keepdims=True) + EPS)
    xhat = xc * rstd
    return xhat, rstd, xhat * g + beta


def _sgu_fwd(p_ab, ln_g, ln_b, w_s, bias, *, name):
    s = p_ab.shape[0]

    def body(p_ref, g_ref, beta_ref, w_ref, bias_ref, y_ref):
        u = _gelu(p_ref[:, 0:D_SGU])
        _, _, vn = _sgu_norm(p_ref[:, D_SGU:2 * D_SGU], g_ref[...], beta_ref[...])
        vb = vn.astype(BF16)
        tril = _tril_mask()
        mixed = bias_ref[...]
        for h, m in enumerate(_head_masks(D_SGU)):
            wh = jnp.where(tril, w_ref[h], 0.0).astype(BF16)
            mixed = mixed + _dot(wh, jnp.where(m, vb, jnp.zeros_like(vb)), _NN)
        y_ref[...] = u * mixed

    return pl.pallas_call(
        body,
        name=name,
        grid=(s // CHUNK,),
        in_specs=[pl.BlockSpec((CHUNK, 2 * D_SGU), lambda i: (i, 1)),
                  pl.BlockSpec((1, D_SGU), lambda i: (0, 0)), pl.BlockSpec((1, D_SGU), lambda i: (0, 0)),
                  pl.BlockSpec((N_SGU_HEADS, CHUNK, CHUNK), lambda i: (0, 0, 0)),
                  pl.BlockSpec((CHUNK, D_SGU), lambda i: (0, 0))],
        out_specs=pl.BlockSpec((CHUNK, D_SGU), lambda i: (i, 0)),
        out_shape=jax.ShapeDtypeStruct((s, D_SGU), F32),
        compiler_params=_cparams(("parallel",)),
    )(p_ab, ln_g, ln_b, w_s, bias)


def _sgu_bwd(p_ab, dyb, ln_g, ln_b, w_s, bias, *, name):
    s = p_ab.shape[0]
    n_chunks = s // CHUNK

    def body(p_ref, dy_ref, g_ref, beta_ref, w_ref, bias_ref, dp_ref, dw_ref, db_ref, dg_ref, dbeta_ref, dbias_ref):
        @pl.when(pl.program_id(0) == 0)
        def _():
            dw_ref[...] = jnp.zeros_like(dw_ref)
            dbias_ref[...] = jnp.zeros_like(dbias_ref)
            dg_ref[...] = jnp.zeros_like(dg_ref)
            dbeta_ref[...] = jnp.zeros_like(dbeta_ref)

        bu = p_ref[:, 0:D_SGU]
        bv = p_ref[:, D_SGU:2 * D_SGU]
        u = _gelu(bu)
        gv = g_ref[...]
        xhat, rstd, vn = _sgu_norm(bv, gv, beta_ref[...])
        vb = vn.astype(BF16)
        tril = _tril_mask()
        masks = _head_masks(D_SGU)
        whs = [jnp.where(tril, w_ref[h], 0.0).astype(BF16) for h in range(N_SGU_HEADS)]
        mixed = bias_ref[...]
        for h, m in enumerate(masks):
            mixed = mixed + _dot(whs[h], jnp.where(m, vb, jnp.zeros_like(vb)), _NN)
        dy = dy_ref[...]
        dp_ref[:, 0:D_SGU] = dy * mixed * _gelu_grad(bu)
        dmixed = dy * u
        dbias_ref[...] += dmixed
        dmb = dmixed.astype(BF16)
        dvn = jnp.zeros((CHUNK, D_SGU), F32)
        for h, m in enumerate(masks):
            dmh = jnp.where(m, dmb, jnp.zeros_like(dmb))
            dvn = dvn + _dot(whs[h], dmh, _TN)
            dw_ref[h] += jnp.where(tril, _dot(dmh, vb, _NT), 0.0)
        dg_ref[...] += jnp.sum(dvn * xhat, axis=0, keepdims=True)
        dbeta_ref[...] += jnp.sum(dvn, axis=0, keepdims=True)
        dxhat = dvn * gv
        dvg = rstd * (dxhat - jnp.mean(dxhat, axis=-1, keepdims=True)
                      - xhat * jnp.mean(dxhat * xhat, axis=-1, keepdims=True))
        dp_ref[:, D_SGU:2 * D_SGU] = dvg * _gelu_grad(bv)

        @pl.when(pl.program_id(0) == n_chunks - 1)
        def _():
            chan = lax.broadcasted_iota(jnp.int32, (D_SGU, LANES), 0)
            head = lax.broadcasted_iota(jnp.int32, (D_SGU, LANES), 1)
            to_head = jnp.where(chan // HEAD_DIM == head, 1.0, 0.0).astype(BF16)
            db_ref[...] = _dot(_split_bf16(dbias_ref[...]), jnp.concatenate([to_head, to_head], axis=0), _NN)

    vec = pl.BlockSpec((1, D_SGU), lambda i: (0, 0))
    wspec = pl.BlockSpec((N_SGU_HEADS, CHUNK, CHUNK), lambda i: (0, 0, 0))
    bspec = pl.BlockSpec((CHUNK, D_SGU), lambda i: (0, 0))
    return pl.pallas_call(
        body,
        name=name,
        grid=(n_chunks,),
        in_specs=[pl.BlockSpec((CHUNK, 2 * D_SGU), lambda i: (i, 1)), pl.BlockSpec((CHUNK, D_SGU), lambda i: (i, 0)),
                  vec, vec, wspec, bspec],
        out_specs=[pl.BlockSpec((CHUNK, 2 * D_SGU), lambda i: (i, 0)), wspec,
                   pl.BlockSpec((CHUNK, LANES), lambda i: (0, 0)), vec, vec],
        out_shape=[jax.ShapeDtypeStruct((s, 2 * D_SGU), F32),
                   jax.ShapeDtypeStruct((N_SGU_HEADS, CHUNK, CHUNK), F32),
                   jax.ShapeDtypeStruct((CHUNK, LANES), F32),
                   jax.ShapeDtypeStruct((1, D_SGU), F32), jax.ShapeDtypeStruct((1, D_SGU), F32)],
        scratch_shapes=[pltpu.VMEM((CHUNK, D_SGU), F32)],
        compiler_params=_cparams(("arbitrary",)),
    )(p_ab, dyb, ln_g, ln_b, w_s, bias)


N_PAIRS = D_SB // LANES
SB_SCALE = HEAD_DIM ** -0.5


def _sb_logits(z, valid):
    nz = -z
    t = jnp.log(1.0 + jnp.exp(jnp.minimum(z, nz)))
    l1 = jnp.minimum(nz, 0.0) - t
    if valid is not None:
        l1 = jnp.where(valid, l1, 0.0)
    return l1, jnp.minimum(z, 0.0) - t


def _split_hi_lo(x):
    hi = lax.bitcast_convert_type(lax.bitcast_convert_type(x, jnp.uint32) & jnp.uint32(0xFFFF0000), F32)
    return jnp.concatenate([hi, x - hi], axis=1)


def _cumsum_operand(keep):
    half = jnp.concatenate([keep.astype(F32), jnp.ones((CHUNK, CHUNK), F32)], axis=1)
    return jnp.concatenate([half, half], axis=0)


def _attn_fwd(qkv, *, name, exchanges=None):
    s = qkv.shape[0]
    nq = s // CHUNK
    ex = exchanges or _Exchanges([])
    n_ex = len(ex.arrays)

    def body(*refs):
        q_ref, k_ref, v_ref = refs[:3]
        o_ref, t_ref = refs[3 + n_ex:5 + n_ex]
        ex_refs = (refs[3:3 + n_ex], refs[5 + n_ex:5 + 2 * n_ex]) + refs[5 + 2 * n_ex:]
        qi = pl.program_id(1)
        if n_ex:
            @pl.when((pl.program_id(0) == 0) & (qi == 0))
            def _():
                ex.start(*ex_refs)

        q = q_ref[...] * SB_SCALE
        zero = jnp.zeros_like(q)
        masks = _head_masks(LANES)
        qs = [jnp.where(m, q, zero) for m in masks]
        row = lax.broadcasted_iota(jnp.int32, (CHUNK, CHUNK), 0)
        col = lax.broadcasted_iota(jnp.int32, (CHUNK, CHUNK), 1)
        after_op = _cumsum_operand(row > col)

        cmr = col - row

        def blocks(js, carry):
            o, c0, c1 = carry
            kvs, valids = [], []
            for j in js:
                k0 = pl.multiple_of(jnp.maximum(j, 0) * CHUNK, CHUNK)
                kvs.append((k_ref[pl.ds(k0, CHUNK), :], v_ref[pl.ds(k0, CHUNK), :]))
                valids.append(cmr < jnp.where(j >= 0, (qi - j) * CHUNK, -CHUNK))
            units = [(h, b) for b in range(len(js)) for h in range(2)]
            zs = [_dot(qs[h], kvs[b][0], _NT) for h, b in units]
            logits = [_sb_logits(z, valids[b]) for z, (h, b) in zip(zs, units)]
            sums = [_dot(_split_hi_lo(l1), after_op, _NN) for l1, _ in logits]
            cs = [c0, c1]
            probs = []
            for (h, b), (_, lb), sm in zip(units, logits, sums):
                probs.append(jnp.where(valids[b], jnp.exp(lb + sm[:, :CHUNK] + cs[h]), 0.0))
                cs[h] = cs[h] + sm[:, CHUNK:]
            for (h, b), a in zip(units, probs):
                o = o + _dot(a.astype(BF16), jnp.where(masks[h], kvs[b][1], zero), _NN)
            return o, cs[0], cs[1]

        zc = jnp.zeros((CHUNK, LANES), F32)
        n_four = (qi + 1) // 4
        carry = lax.fori_loop(0, n_four, lambda jj, c: blocks([qi - 4 * jj - i for i in range(4)], c), (zc,) * 3)
        top = qi - 4 * n_four
        o, c0, c1 = lax.fori_loop(0, (top + 2) // 2, lambda jj, c: blocks([top - 2 * jj, top - 2 * jj - 1], c), carry)
        o_ref[...] = o
        t_ref[:, 0:LANES] = c0
        t_ref[:, LANES:2 * LANES] = c1
        if n_ex:
            @pl.when((pl.program_id(0) == N_PAIRS - 1) & (qi == nq - 1))
            def _():
                ex.wait(*ex_refs)

    return pl.pallas_call(
        body,
        name=name,
        grid=(N_PAIRS, nq),
        in_specs=[pl.BlockSpec((CHUNK, LANES), lambda p, i: (i, p)),
                  pl.BlockSpec((s, LANES), lambda p, i: (0, N_PAIRS + p)),
                  pl.BlockSpec((s, LANES), lambda p, i: (0, 2 * N_PAIRS + p))] + ex.in_specs,
        out_specs=[pl.BlockSpec((CHUNK, LANES), lambda p, i: (i, p)),
                   pl.BlockSpec((CHUNK, 2 * LANES), lambda p, i: (i, p))] + ex.out_specs,
        out_shape=[jax.ShapeDtypeStruct((s, D_SB), F32), jax.ShapeDtypeStruct((s, 2 * D_SB), F32)] + ex.out_shapes,
        scratch_shapes=ex.scratch_shapes if n_ex else [],
        compiler_params=_cparams(("arbitrary", "arbitrary")),
    )(qkv, qkv, qkv, *ex.arrays)


def _attn_bwd(qkv, t_tot, do, *, name, exchanges=None):
    s = qkv.shape[0]
    nq = s // CHUNK
    ex = exchanges or _Exchanges([])
    n_ex = len(ex.arrays)

    def body(*refs):
        q_ref, k_ref, v_ref, t_ref, do_ref = refs[:5]
        dq_ref, dk_ref, dv_ref = refs[5 + n_ex:8 + n_ex]
        ex_refs = (refs[5:5 + n_ex], refs[8 + n_ex:8 + 2 * n_ex]) + refs[8 + 2 * n_ex:]
        qi = pl.program_id(1)
        if n_ex:
            @pl.when((pl.program_id(0) == 0) & (qi == 0))
            def _():
                ex.start(*ex_refs)

        @pl.when(qi == 0)
        def _():
            dk_ref[...] = jnp.zeros_like(dk_ref)
            dv_ref[...] = jnp.zeros_like(dv_ref)

        q = q_ref[...] * SB_SCALE
        dob = do_ref[...].astype(BF16)
        zero = jnp.zeros_like(q)
        masks = _head_masks(LANES)
        qs = [jnp.where(m, q, zero) for m in masks]
        dos = [jnp.where(m, dob, zero) for m in masks]
        tots = [t_ref[:, 0:LANES], t_ref[:, LANES:2 * LANES]]
        row = lax.broadcasted_iota(jnp.int32, (CHUNK, CHUNK), 0)
        col = lax.broadcasted_iota(jnp.int32, (CHUNK, CHUNK), 1)
        upto_op = _cumsum_operand(row <= col)
        before_op = _cumsum_operand(row < col)

        cmr = col - row

        def blocks(js, carry):
            dq, cl0, cl1, cp0, cp1 = carry
            starts = [pl.multiple_of(jnp.minimum(j, nq - 1) * CHUNK, CHUNK) for j in js]
            valids = [cmr < (qi - j) * CHUNK for j in js]
            kvs = [(k_ref[pl.ds(k0, CHUNK), :], v_ref[pl.ds(k0, CHUNK), :]) for k0 in starts]
            units = [(h, b) for b in range(len(js)) for h in range(2)]
            zs = [_dot(qs[h], kvs[b][0], _NT) for h, b in units]
            das = [_dot(dos[h], kvs[b][1], _NT) for h, b in units]
            logits = [_sb_logits(z, valids[b]) for z, (h, b) in zip(zs, units)]
            sums = [_dot(_split_hi_lo(l1), upto_op, _NN) for l1, _ in logits]
            cls, cps = [cl0, cl1], [cp0, cp1]
            probs, gs = [], []
            for (h, b), (_, lb), sm, da in zip(units, logits, sums, das):
                a = jnp.where(valids[b], jnp.exp(lb + (tots[h] - cls[h] - sm[:, :CHUNK])), 0.0)
                probs.append(a)
                gs.append(a * da)
                cls[h] = cls[h] + sm[:, CHUNK:]
            sums_g = [_dot(_split_hi_lo(g), before_op, _NN) for g in gs]
            dzs = []
            for (h, b), (_, lb), g, sg in zip(units, logits, gs, sums_g):
                dz = g - (g + sg[:, :CHUNK] + cps[h]) * jnp.exp(lb)
                dzs.append(jnp.where(valids[b], dz, 0.0).astype(BF16))
                cps[h] = cps[h] + sg[:, CHUNK:]
            for (h, b), dzb in zip(units, dzs):
                dq = dq + _dot(dzb, jnp.where(masks[h], kvs[b][0], zero), _NN)
            for b, k0 in enumerate(starts):
                dk_ref[pl.ds(k0, CHUNK), :] += _dot(dzs[2 * b], qs[0], _TN) + _dot(dzs[2 * b + 1], qs[1], _TN)
                dv_ref[pl.ds(k0, CHUNK), :] += (_dot(probs[2 * b].astype(BF16), dos[0], _TN)
                                                + _dot(probs[2 * b + 1].astype(BF16), dos[1], _TN))
            return dq, cls[0], cls[1], cps[0], cps[1]

        zc = jnp.zeros((CHUNK, LANES), F32)
        n_four = (qi + 1) // 4
        carry = lax.fori_loop(0, n_four, lambda jj, c: blocks([4 * jj + i for i in range(4)], c), (zc,) * 5)
        base = 4 * n_four
        carry = lax.fori_loop(0, (qi - base + 2) // 2, lambda jj, c: blocks([base + 2 * jj, base + 2 * jj + 1], c), carry)
        dq_ref[...] = carry[0] * SB_SCALE
        if n_ex:
            @pl.when((pl.program_id(0) == N_PAIRS - 1) & (qi == nq - 1))
            def _():
                ex.wait(*ex_refs)

    blk = pl.BlockSpec((CHUNK, LANES), lambda p, i: (i, p))
    col_blk = pl.BlockSpec((s, LANES), lambda p, i: (0, p))
    out = jax.ShapeDtypeStruct((s, D_SB), F32)
    return pl.pallas_call(
        body,
        name=name,
        grid=(N_PAIRS, nq),
        in_specs=[blk,
                  pl.BlockSpec((s, LANES), lambda p, i: (0, N_PAIRS + p)),
                  pl.BlockSpec((s, LANES), lambda p, i: (0, 2 * N_PAIRS + p)),
                  pl.BlockSpec((CHUNK, 2 * LANES), lambda p, i: (i, p)),
                  blk] + ex.in_specs,
        out_specs=[blk, col_blk, col_blk] + ex.out_specs,
        out_shape=[out, out, out] + ex.out_shapes,
        scratch_shapes=ex.scratch_shapes if n_ex else [],
        compiler_params=_cparams(("arbitrary", "arbitrary")),
    )(qkv, qkv, qkv, t_tot, do, *ex.arrays)


FFN_TILE = 256
FFN_COLS = 256
N_FF_BLOCKS = D_FF // FFN_COLS


def _ffn_act_fwd(up0, conv_w, conv_b, *, name):
    s = up0.shape[0]
    nt = s // FFN_TILE

    def body(xg_ref, xv_ref, wg_ref, wv_ref, bg_ref, bv_ref, act_ref, pg_ref, pv_ref):
        pg_ref[0:FFN_HALO, :] = jnp.zeros((FFN_HALO, FFN_COLS), F32)
        pv_ref[0:FFN_HALO, :] = jnp.zeros((FFN_HALO, FFN_COLS), F32)
        pg_ref[FFN_HALO:, :] = xg_ref[...]
        pv_ref[FFN_HALO:, :] = xv_ref[...]

        def tile(i, c):
            t0 = pl.multiple_of(i * FFN_TILE, FFN_TILE)
            outs = []
            for p_ref, w_ref, b_ref in ((pg_ref, wg_ref, bg_ref), (pv_ref, wv_ref, bv_ref)):
                window = p_ref[pl.ds(t0, FFN_TILE + FFN_HALO), :]
                acc = b_ref[...] + w_ref[2:3, :] * window[FFN_HALO:, :]
                for j in range(1, FFN_K):
                    acc = acc + w_ref[FFN_K - 1 - j:FFN_K - j, :] * _shift_down(window, j, FFN_HALO)
                outs.append(acc)
            gate, val = outs
            act_ref[pl.ds(t0, FFN_TILE), :] = (gate * _sigmoid(gate) * val).astype(BF16)
            return c

        lax.fori_loop(0, nt, tile, 0)

    gcol = lambda rows: pl.BlockSpec((rows, FFN_COLS), lambda j: (0, j))
    vcol = lambda rows: pl.BlockSpec((rows, FFN_COLS), lambda j: (0, j + N_FF_BLOCKS))
    return pl.pallas_call(
        body,
        name=name,
        grid=(N_FF_BLOCKS,),
        in_specs=[gcol(s), vcol(s), gcol(FFN_K), vcol(FFN_K), gcol(1), vcol(1)],
        out_specs=gcol(s),
        out_shape=jax.ShapeDtypeStruct((s, D_FF), BF16),
        scratch_shapes=[pltpu.VMEM((s + FFN_HALO, FFN_COLS), F32), pltpu.VMEM((s + FFN_HALO, FFN_COLS), F32)],
        compiler_params=_cparams(("parallel",)),
    )(up0, up0, conv_w, conv_w, conv_b, conv_b)


def _ffn_act_bwd(up0, dact, conv_w, conv_b, *, name):
    s = up0.shape[0]
    nt = s // FFN_TILE

    def body(xg_ref, xv_ref, da_ref, wg_ref, wv_ref, bg_ref, bv_ref, dxg_ref, dxv_ref, dwg_ref, dwv_ref, dbg_ref, dbv_ref,
             pg_ref, pv_ref, dg_ref, dv_ref):
        zeros = jnp.zeros((FFN_HALO, FFN_COLS), F32)
        for p_ref, x_ref in ((pg_ref, xg_ref), (pv_ref, xv_ref)):
            p_ref[0:FFN_HALO, :] = zeros
            p_ref[FFN_HALO:, :] = x_ref[...]
        dg_ref[s:s + FFN_HALO, :] = zeros
        dv_ref[s:s + FFN_HALO, :] = zeros
        for ref in (dwg_ref, dwv_ref, dbg_ref, dbv_ref):
            ref[...] = jnp.zeros_like(ref)

        def conv(p_ref, w_ref, b_ref, t0):
            window = p_ref[pl.ds(t0, FFN_TILE + FFN_HALO), :]
            taps = [_shift_down(window, j, FFN_HALO) for j in range(FFN_K)]
            out = b_ref[...]
            for j in range(FFN_K):
                out = out + w_ref[FFN_K - 1 - j:FFN_K - j, :] * taps[j]
            return out, taps

        def tile(i, c):
            t0 = pl.multiple_of(i * FFN_TILE, FFN_TILE)
            gate, taps_g = conv(pg_ref, wg_ref, bg_ref, t0)
            val, taps_v = conv(pv_ref, wv_ref, bv_ref, t0)
            da = da_ref[pl.ds(t0, FFN_TILE), :]
            sg = lax.logistic(gate)
            dgate = da * val * (sg * (1.0 + gate * (1.0 - sg)))
            dval = da * gate * sg
            dg_ref[pl.ds(t0, FFN_TILE), :] = dgate
            dv_ref[pl.ds(t0, FFN_TILE), :] = dval
            dbg_ref[...] += jnp.sum(dgate, axis=0, keepdims=True)
            dbv_ref[...] += jnp.sum(dval, axis=0, keepdims=True)
            for j in range(FFN_K):
                dwg_ref[FFN_K - 1 - j:FFN_K - j, :] += jnp.sum(dgate * taps_g[j], axis=0, keepdims=True)
                dwv_ref[FFN_K - 1 - j:FFN_K - j, :] += jnp.sum(dval * taps_v[j], axis=0, keepdims=True)
            return c

        lax.fori_loop(0, nt, tile, 0)

        def tile_dx(i, c):
            t0 = pl.multiple_of(i * FFN_TILE, FFN_TILE)
            for d_ref, w_ref, dx_ref in ((dg_ref, wg_ref, dxg_ref), (dv_ref, wv_ref, dxv_ref)):
                window = d_ref[pl.ds(t0, FFN_TILE + FFN_HALO), :]
                dx = w_ref[FFN_K - 1:FFN_K, :] * window[:FFN_TILE, :]
                for j in range(1, FFN_K):
                    dx = dx + w_ref[FFN_K - 1 - j:FFN_K - j, :] * _shift_up(window, j, FFN_TILE)
                dx_ref[pl.ds(t0, FFN_TILE), :] = dx.astype(BF16)
            return c

        lax.fori_loop(0, nt, tile_dx, 0)

    gcol = lambda rows: pl.BlockSpec((rows, FFN_COLS), lambda j: (0, j))
    vcol = lambda rows: pl.BlockSpec((rows, FFN_COLS), lambda j: (0, j + N_FF_BLOCKS))
    half = lambda rows, dtype: jax.ShapeDtypeStruct((rows, D_FF), dtype)
    padded = pltpu.VMEM((s + FFN_HALO, FFN_COLS), F32)
    return pl.pallas_call(
        body,
        name=name,
        grid=(N_FF_BLOCKS,),
        in_specs=[gcol(s), vcol(s), gcol(s), gcol(FFN_K), vcol(FFN_K), gcol(1), vcol(1)],
        out_specs=[gcol(s), gcol(s), gcol(FFN_K), gcol(FFN_K), gcol(1), gcol(1)],
        out_shape=[half(s, BF16), half(s, BF16), half(FFN_K, F32), half(FFN_K, F32), half(1, F32), half(1, F32)],
        scratch_shapes=[padded, padded, padded, padded],
        compiler_params=_cparams(("parallel",)),
    )(up0, up0, dact, conv_w, conv_w, conv_b, conv_b)


MESH = pl.DeviceIdType.MESH


def _position():
    x, y, c = lax.axis_index("x"), lax.axis_index("y"), lax.axis_index("c")
    return x, y, c, 4 * x + 2 * y + c


def _peer(k):
    x, y, c, _ = _position()
    px = 1 - x if k & 4 else x
    py = 1 - y if k & 2 else y
    pc = 1 - c if k & 1 else c
    return (px, py, pc), 4 * px + 2 * py + pc


def _exchange(src, *, kind, name):
    ex = _Exchanges([(kind, src)])

    def body(src_ref, out_ref, send_sems, recv_sems, local_sems):
        ex.start([src_ref], [out_ref], send_sems, recv_sems, local_sems)
        ex.wait([src_ref], [out_ref], send_sems, recv_sems, local_sems)

    return pl.pallas_call(
        body,
        name=name,
        in_specs=ex.in_specs,
        out_specs=ex.out_specs[0],
        out_shape=ex.out_shapes[0],
        scratch_shapes=ex.scratch_shapes,
    )(src)


_HBM = pl.BlockSpec(memory_space=pltpu.HBM)
_SEM = pl.BlockSpec(memory_space=pltpu.SEMAPHORE)
_DATAFLOW = pltpu.SideEffectType.DATAFLOW_SIDE_EFFECTING
N_PEERS = N_DEV - 1


class _SplitExchange:
    def __init__(self, src, *, kind, name):
        self.kind, self.name, self.dtype = kind, name, src.dtype
        scatter = kind.startswith("scatter")
        by_blocks = kind == "scatter_blocks"
        self.scatter, self.by_blocks = scatter, by_blocks
        if by_blocks:
            self.r, self.cols, self.land_shape = None, None, src.shape
        else:
            self.r = src.shape[0] // N_DEV if scatter else src.shape[0]
            self.cols = src.shape[1]
            self.land_shape = (N_DEV, self.r, self.cols) if scatter else (N_DEV * self.r, self.cols)
        r = self.r

        def copies(src_ref, land_ref, send_sems, recv_sems, local_sem):
            me = _position()[3]

            def rows(ref, idx):
                return ref.at[pl.ds(pl.multiple_of(idx * r, r), r), :]

            if by_blocks:
                outgoing = lambda idx: src_ref.at[idx]
            else:
                outgoing = (lambda idx: rows(src_ref, idx)) if scatter else (lambda idx: src_ref)
            slot = (lambda idx: land_ref.at[idx]) if scatter else (lambda idx: rows(land_ref, idx))
            sends, recvs = [], []
            for k in range(1, N_DEV):
                peer, pidx = _peer(k)
                sems = dict(send_sem=send_sems[k - 1], recv_sem=recv_sems[k - 1], device_id=peer, device_id_type=MESH)
                sends.append(pltpu.make_async_remote_copy(src_ref=outgoing(pidx), dst_ref=slot(me), **sems))
                recvs.append(pltpu.make_async_remote_copy(src_ref=outgoing(pidx), dst_ref=slot(pidx), **sems))
            return sends, recvs, pltpu.make_async_copy(outgoing(me), slot(me), local_sem)

        self._copies = copies
        self.src = src

    @staticmethod
    def start(exchanges, name):
        n = len(exchanges)
        per = 2 * N_PEERS + 1

        def start_body(*refs):
            outs = refs[2 * n:]
            for i, ex in enumerate(exchanges):
                sems = outs[per * i:per * (i + 1)]
                sends, _, local = ex._copies(refs[2 * i], refs[2 * i + 1], sems[:N_PEERS], sems[N_PEERS:-1], sems[-1])
                for cp in sends + [local]:
                    cp.start()
            outs[-1][...] = jnp.zeros_like(outs[-1])

        sem = pltpu.SemaphoreType.DMA(())
        operands, thru_shapes = [], []
        for ex in exchanges:
            operands += [pltpu.with_memory_space_constraint(ex.src, pltpu.HBM),
                         pltpu.with_memory_space_constraint(lax.empty(ex.land_shape, ex.dtype), pltpu.HBM)]
            thru_shapes += [pltpu.HBM(ex.src.shape, ex.dtype), pltpu.HBM(ex.land_shape, ex.dtype)]
        out = pl.pallas_call(
            start_body,
            name=name,
            in_specs=(_HBM,) * (2 * n),
            out_specs=(_SEM,) * (per * n) + (_HBM,) * (2 * n) + (pl.BlockSpec(memory_space=pltpu.VMEM),),
            out_shape=(sem,) * (per * n) + tuple(thru_shapes) + (jax.ShapeDtypeStruct((8, LANES), F32),),
            input_output_aliases={i: per * n + i for i in range(2 * n)},
            compiler_params=pltpu.CompilerParams(has_side_effects=_DATAFLOW),
        )(*operands)
        for i, ex in enumerate(exchanges):
            ex.sems = out[per * i:per * (i + 1)]
            ex.src_thru, ex.land_thru = out[per * n + 2 * i], out[per * n + 2 * i + 1]
        return out[-1][0, 0]

    def finish(self, after):
        copies = self._copies

        def wait_body(src_ref, land_ref, *rest):
            sends, recvs, local = copies(src_ref, land_ref, rest[:N_PEERS], rest[N_PEERS:2 * N_PEERS], rest[2 * N_PEERS])
            for cp in sends:
                cp.wait_send()
            for cp in recvs:
                cp.wait_recv()
            local.wait()

        return pl.pallas_call(
            wait_body,
            name=f"{self.name}_wait",
            in_specs=(_HBM, _HBM) + (_SEM,) * (2 * N_PEERS + 1) + (pl.BlockSpec(memory_space=pl.ANY),),
            out_specs=(_HBM, _HBM),
            out_shape=(pltpu.HBM(self.src_thru.shape, self.dtype), pltpu.HBM(self.land_shape, self.dtype)),
            input_output_aliases={0: 0, 1: 1},
            compiler_params=pltpu.CompilerParams(has_side_effects=_DATAFLOW),
        )(self.src_thru, self.land_thru, *self.sems, after)[1]


class _Exchanges:
    def __init__(self, items):
        self.kinds = [kind for kind, _ in items]
        self.arrays = [src for _, src in items]
        self.out_shapes = []
        self.block_rows = []
        for kind, src in items:
            scatter, by_rows = kind.startswith("scatter"), kind.endswith("rows")
            if by_rows:
                r = src.shape[0] // N_DEV if scatter else src.shape[0]
                shape = (N_DEV, r, src.shape[1]) if scatter else (N_DEV * r, src.shape[1])
            else:
                r = None
                shape = src.shape if scatter else (N_DEV,) + src.shape
            self.block_rows.append(r)
            self.out_shapes.append(jax.ShapeDtypeStruct(shape, src.dtype))
        n = len(items)
        self.in_specs = [pl.BlockSpec(memory_space=pl.ANY)] * n
        self.out_specs = [pl.BlockSpec(memory_space=pl.ANY)] * n
        self.scratch_shapes = [pltpu.SemaphoreType.DMA((n * (N_DEV - 1),)), pltpu.SemaphoreType.DMA((n * (N_DEV - 1),)),
                               pltpu.SemaphoreType.DMA((n,))]

    def _copies(self, i, src_ref, out_ref, send_sems, recv_sems, local_sems):
        kind, r = self.kinds[i], self.block_rows[i]
        scatter, by_rows = kind.startswith("scatter"), kind.endswith("rows")
        me = _position()[3]

        def rows(ref, idx):
            return ref.at[pl.ds(pl.multiple_of(idx * r, r), r), :]

        def outgoing(idx):
            if not scatter:
                return src_ref
            return rows(src_ref, idx) if by_rows else src_ref.at[idx]

        def slot(idx):
            return rows(out_ref, idx) if (by_rows and not scatter) else out_ref.at[idx]

        local = pltpu.make_async_copy(outgoing(me), slot(me), local_sems.at[i])
        sends, recvs = [], []
        for k in range(1, N_DEV):
            peer, pidx = _peer(k)
            sem = i * (N_DEV - 1) + k - 1
            sends.append(pltpu.make_async_remote_copy(src_ref=outgoing(pidx), dst_ref=slot(me), send_sem=send_sems.at[sem],
                                                      recv_sem=recv_sems.at[sem], device_id=peer, device_id_type=MESH))
            recvs.append(pltpu.make_async_remote_copy(src_ref=outgoing(pidx), dst_ref=slot(pidx), send_sem=send_sems.at[sem],
                                                      recv_sem=recv_sems.at[sem], device_id=peer, device_id_type=MESH))
        return local, sends, recvs

    def start(self, src_refs, out_refs, *sems):
        for i, (src_ref, out_ref) in enumerate(zip(src_refs, out_refs)):
            local, sends, _ = self._copies(i, src_ref, out_ref, *sems)
            local.start()
            for cp in sends:
                cp.start()

    def wait(self, src_refs, out_refs, *sems):
        for i, (src_ref, out_ref) in enumerate(zip(src_refs, out_refs)):
            local, sends, recvs = self._copies(i, src_ref, out_ref, *sems)
            for cp in recvs:
                cp.wait_recv()
            for cp in sends:
                cp.wait_send()
            local.wait()


def _row_tile(rows):
    return _tile(rows, (256, 128, 64, 32, 16, 8))


def _layer_parts_specs(n_layers, n_parts, tr, cols):
    return [pl.BlockSpec((n_parts, tr, cols), lambda l, i, j=j: (0, jnp.where(l == j, i, 0), 0)) for j in range(n_layers)]


def _select_layer_sum(p_refs):
    l = pl.program_id(0)
    g = None
    for j, p_ref in enumerate(p_refs):
        gj = p_ref[0].astype(F32)
        for k in range(1, p_ref.shape[0]):
            gj = gj + p_ref[k].astype(F32)
        g = gj if g is None else jnp.where(l == j, gj, g)
    return g


def _sum_parts(parts, *, name):
    n_layers = len(parts)
    n_parts, rows, cols = parts[0].shape
    tr = _row_tile(rows)

    def body(*refs):
        refs[-1][...] = _select_layer_sum(refs[:n_layers])

    return pl.pallas_call(
        body,
        name=name,
        grid=(n_layers, rows // tr),
        in_specs=_layer_parts_specs(n_layers, n_parts, tr, cols),
        out_specs=pl.BlockSpec((None, tr, cols), lambda l, i: (l, i, 0)),
        out_shape=jax.ShapeDtypeStruct((n_layers, rows, cols), F32),
        compiler_params=_cparams(("arbitrary", "arbitrary")),
    )(*parts)


def _adamw(parts, w, m, v, *, name):
    n_layers, rows, cols = w.shape
    summed = not isinstance(parts, (list, tuple))
    tr = _row_tile(rows)
    n_in = 1 if summed else n_layers

    def body(*refs):
        w_ref, m_ref, v_ref, g_ref, d_ref, m2_ref, v2_ref = refs[n_in:]
        g = refs[0][...] if summed else _select_layer_sum(refs[:n_in])
        m2 = ADAM_B1 * m_ref[...] + (1.0 - ADAM_B1) * g
        v2 = ADAM_B2 * v_ref[...] + (1.0 - ADAM_B2) * (g * g)
        m_hat = m2 / (1.0 - ADAM_B1 ** ADAM_STEP)
        v_hat = v2 / (1.0 - ADAM_B2 ** ADAM_STEP)
        g_ref[...] = g
        d_ref[...] = -ADAM_LR * (m_hat / (jnp.sqrt(v_hat) + ADAM_EPS) + ADAM_WD * w_ref[...])
        m2_ref[...] = m2
        v2_ref[...] = v2

    slab = pl.BlockSpec((None, tr, cols), lambda l, i: (l, i, 0))
    out = jax.ShapeDtypeStruct((n_layers, rows, cols), F32)
    p_specs = [slab] if summed else _layer_parts_specs(n_layers, parts[0].shape[0], tr, cols)
    return pl.pallas_call(
        body,
        name=name,
        grid=(n_layers, rows // tr),
        in_specs=p_specs + [slab, slab, slab],
        out_specs=[slab, slab, slab, slab],
        out_shape=[out, out, out, out],
        compiler_params=_cparams(("arbitrary", "arbitrary")),
    )(*((parts,) if summed else tuple(parts)), w, m, v)


SLAB_ROWS = 256
_SMALL_SHARDED = (("conv_w", (2, 31, 32)), ("ffn_conv_w", (2, 3, 704)))
_REPLICATED = (("g_mix", (2, 1024)), ("conv_b", (2, 256)), ("conv_ln_g", (2, 256)), ("conv_ln_b", (2, 256)),
               ("sgu_ln_g", (2, 256)), ("sgu_ln_b", (2, 256)), ("sgu_w", (2, 4, 128, 128)), ("sgu_b", (2, 4, 128)),
               ("g_out", (2, 1024)), ("g_ffn", (2, 1024)), ("ffn_conv_b", (2, 5632)), ("g_final", (1024,)))


def _seg_rows(n_elems):
    return -(-n_elems // LANES)


def _pack(arrays, lead=()):
    segs = []
    for a in arrays:
        flat = a.reshape(lead + (-1,)).astype(F32)
        pad = _seg_rows(flat.shape[-1]) * LANES - flat.shape[-1]
        if pad:
            flat = jnp.pad(flat, [(0, 0)] * len(lead) + [(0, pad)])
        segs.append(flat)
    flat = jnp.concatenate(segs, axis=-1)
    rows = flat.shape[-1] // LANES
    pad_rows = -rows % SLAB_ROWS
    if pad_rows:
        flat = jnp.pad(flat, [(0, 0)] * len(lead) + [(0, pad_rows * LANES)])
    return flat.reshape(lead + (rows + pad_rows, LANES))


def _unpack(slab, shapes, lead=()):
    flat = slab.reshape(lead + (-1,))
    out, off = [], 0
    for shape in shapes:
        n = math.prod(shape)
        out.append(flat[..., off:off + n].reshape(lead + tuple(shape)))
        off += _seg_rows(n) * LANES
    return out


def _split_last(full):
    split = full.shape[:-1] + (N_DEV, full.shape[-1] // N_DEV)
    return jnp.moveaxis(full.reshape(split), -2, 0)


def _join_last(blocks):
    moved = jnp.moveaxis(blocks, 0, -2)
    return moved.reshape(moved.shape[:-2] + (moved.shape[-2] * moved.shape[-1],))


def _gathered(wt, n, l, after):
    if isinstance(wt[n][l], _SplitExchange):
        wt[n][l] = wt[n][l].finish(after)
    return wt[n][l]


def _layer_fwd(l, x, wt, small):
    tag = f"l{l}"
    h = _rmsnorm_fwd(x, small["g_mix"][l][None], name=f"{tag}_norm_mix")
    w_in_t = _gathered(wt, "w_in_t", l, h)
    p_ab = _matmul(h, w_in_t, "nt", name=f"{tag}_proj_ab", n=D_AB)
    qkv = _matmul(h, w_in_t, "nt", name=f"{tag}_proj_qkv", n=D_QKV, b_n0=D_AB, out_dtype=BF16)
    ya = _mixer_a_fwd(p_ab, wt["conv_w"][l], small["conv_b"][l][None], small["conv_ln_g"][l][None],
                      small["conv_ln_b"][l][None], name=f"{tag}_mixer_a")
    bias = jnp.repeat(small["sgu_b"][l].T, HEAD_DIM, axis=1)
    yb = _sgu_fwd(p_ab, small["sgu_ln_g"][l][None], small["sgu_ln_b"][l][None], small["sgu_w"][l], bias,
                  name=f"{tag}_sgu")
    yc, t_tot = _attn_fwd(qkv, name=f"{tag}_attn")
    y = _combine_fwd(ya, yb, yc, small["g_out"][l][None], name=f"{tag}_combine")
    x1 = _matmul(y, _gathered(wt, "w_out", l, y), "nn", name=f"{tag}_out_proj", residual=x)
    h2 = _rmsnorm_fwd(x1, small["g_ffn"][l][None], name=f"{tag}_norm_ffn")
    up0 = _matmul(h2, _gathered(wt, "w_up_t", l, h2), "nt", name=f"{tag}_up")
    act = _ffn_act_fwd(up0, wt["ffn_conv_w"][l], small["ffn_conv_b"][l][None], name=f"{tag}_ffn_act")
    x2 = _matmul(act, _gathered(wt, "w_down", l, act), "nn", name=f"{tag}_down", residual=x1)
    saved = dict(x=x, h=h, p_ab=p_ab, qkv=qkv, ya=ya, yb=yb, yc=yc, t_tot=t_tot, y=y, x1=x1, h2=h2, up0=up0,
                 act=act, bias=bias)
    return x2, saved


def _layer_bwd(l, dres, sv, wt, small, scattering):
    tag = f"l{l}b"
    g = {}

    def scatter(n, partial):
        scattering[n][l] = _SplitExchange(partial, kind="scatter_rows", name=f"scatter_{n}_l{l}")
        return _SplitExchange.start([scattering[n][l]], name=f"scatter_{n}_l{l}_start")

    dx2, dx2_b = dres
    dact = _matmul(dx2_b, wt["w_down"][l], "nt", name=f"{tag}_dact")
    tok = scatter("w_down", _matmul(sv["act"], dx2_b, "tn", name=f"{tag}_dw_down", out_dtype=BF16))
    dup_g, dup_v, dwg, dwv, dbg, dbv = _ffn_act_bwd(sv["up0"], dact, wt["ffn_conv_w"][l], small["ffn_conv_b"][l][None] + tok,
                                                    name=f"{tag}_ffn_act")
    g["ffn_conv_w"] = jnp.concatenate([dwg, dwv], axis=1)
    g["ffn_conv_b"] = jnp.concatenate([dbg[0], dbv[0]])
    dh2 = _matmul(dup_g, wt["w_up_t"][l], "nn", name=f"{tag}_dh2_gate")
    dh2 = _matmul(dup_v, wt["w_up_t"][l], "nn", name=f"{tag}_dh2_val", b_k0=D_FF, residual=dh2)
    dw_up = _matmul(dup_g, sv["h2"], "tn", name=f"{tag}_dw_up_gate", out_dtype=BF16, rows=(2 * D_FF, 0))
    dw_up = _matmul(dup_v, sv["h2"], "tn", name=f"{tag}_dw_up_val", out_dtype=BF16, rows=(2 * D_FF, D_FF), into=dw_up)
    tok = scatter("w_up_t", dw_up)
    dx1, dx1_b, dg = _rmsnorm_bwd(sv["x1"], small["g_ffn"][l][None] + tok, dh2, dx2, name=f"{tag}_norm_ffn")
    g["g_ffn"] = dg[0]
    dy = _matmul(dx1_b, wt["w_out"][l], "nt", name=f"{tag}_dy")
    tok = scatter("w_out", _matmul(sv["y"], dx1_b, "tn", name=f"{tag}_dw_out", out_dtype=BF16))
    dya, dyb, dyc, dg = _combine_bwd(dy, sv["ya"], sv["yb"], sv["yc"], small["g_out"][l][None] + tok,
                                     name=f"{tag}_combine")
    g["g_out"] = dg[0]
    dq, dk, dv = _attn_bwd(sv["qkv"], sv["t_tot"], dyc, name=f"{tag}_attn")
    dp_b, g["sgu_w"], db, dg, dbeta = _sgu_bwd(sv["p_ab"], dyb, small["sgu_ln_g"][l][None], small["sgu_ln_b"][l][None],
                                               small["sgu_w"][l], sv["bias"], name=f"{tag}_sgu")
    g["sgu_b"] = db[:, :N_SGU_HEADS].T
    g["sgu_ln_g"], g["sgu_ln_b"] = dg[0], dbeta[0]
    dp_a, g["conv_w"], dcb, dg, dbeta = _mixer_a_bwd(sv["p_ab"], dya, wt["conv_w"][l], small["conv_b"][l][None],
                                                     small["conv_ln_g"][l][None], small["conv_ln_b"][l][None],
                                                     name=f"{tag}_mixer_a")
    g["conv_b"], g["conv_ln_g"], g["conv_ln_b"] = dcb[0], dg[0], dbeta[0]
    dp = jnp.concatenate([dp_a.astype(BF16), dp_b.astype(BF16), dq.astype(BF16), dk.astype(BF16), dv.astype(BF16)],
                         axis=1)
    dh = _matmul(dp, wt["w_in_t"][l], "nn", name=f"{tag}_dh")
    tok = scatter("w_in_t", _matmul(dp, sv["h"], "tn", name=f"{tag}_dw_in", out_dtype=BF16))
    dx, dx_b, dg = _rmsnorm_bwd(sv["x"], small["g_mix"][l][None] + tok, dh, dx1, name=f"{tag}_norm_mix")
    g["g_mix"] = dg[0]
    return (dx, dx_b), g


_BIG = ("w_in_t", "w_out", "w_up_t", "w_down")


def kernel(x, g_mix, w_in, conv_w, conv_b, conv_ln_g, conv_ln_b, sgu_ln_g, sgu_ln_b, sgu_w, sgu_b, g_out, w_out, g_ffn, w_up, ffn_conv_w, ffn_conv_b, w_down, g_final, loss_target, m_g_mix, m_w_in, m_conv_w, m_conv_b, m_conv_ln_g, m_conv_ln_b, m_sgu_ln_g, m_sgu_ln_b, m_sgu_w, m_sgu_b, m_g_out, m_w_out, m_g_ffn, m_w_up, m_ffn_conv_w, m_ffn_conv_b, m_w_down, m_g_final, v_g_mix, v_w_in, v_conv_w, v_conv_b, v_conv_ln_g, v_conv_ln_b, v_sgu_ln_g, v_sgu_ln_b, v_sgu_w, v_sgu_b, v_g_out, v_w_out, v_g_ffn, v_w_up, v_ffn_conv_w, v_ffn_conv_b, v_w_down, v_g_final):
    given = dict(locals())
    n_layers = g_mix.shape[0]
    layers = range(n_layers)
    small_sharded = [n for n, _ in _SMALL_SHARDED]
    replicated = [n for n, _ in _REPLICATED]
    small = {n: given[n] for n in replicated}

    filters = _exchange(_pack([given[n] for n in small_sharded]), kind="gather_blocks", name="gather_filters")
    filters, first_shard = lax.optimization_barrier((filters, w_in[0].T.astype(BF16)))
    wt = {n: [None] * n_layers for n in _BIG}
    wt["w_in_t"][0] = _SplitExchange(first_shard, kind="gather_rows", name="gather_w_in_t_l0")
    tok = _SplitExchange.start([wt["w_in_t"][0]], name="gather_w_in_t_l0_start")
    w_in, w_out, w_up, w_down, tok = lax.optimization_barrier((w_in, w_out, w_up, w_down, tok))
    shard = {"w_in_t": [w_in[l].T.astype(BF16) for l in layers], "w_out": [w_out[l].astype(BF16) for l in layers],
             "w_up_t": [w_up[l].T.astype(BF16) for l in layers], "w_down": [w_down[l].astype(BF16) for l in layers]}
    later = [(n, l) for l in layers for n in _BIG if (n, l) != ("w_in_t", 0)]
    for n, l in later:
        wt[n][l] = _SplitExchange(shard[n][l], kind="gather_rows", name=f"gather_{n}_l{l}")
    small["g_mix"] = g_mix + tok + _SplitExchange.start([wt[n][l] for n, l in later], name="gather_weights_start")
    for n, blocks in zip(small_sharded, _unpack(filters, [s for _, s in _SMALL_SHARDED], lead=(N_DEV,))):
        wt[n] = _join_last(blocks)

    xs = x[0]
    saved = []
    for l in layers:
        xs, sv = _layer_fwd(l, xs, wt, small)
        saved.append(sv)
    loss_tile, dx, dx_b, dgf = _loss_head(xs, g_final[None], loss_target[0], name="loss_head")
    dres = (dx, dx_b)
    scattering = {n: [None] * n_layers for n in _BIG}
    grads = [None] * n_layers
    for l in reversed(layers):
        dres, grads[l] = _layer_bwd(l, dres, saved[l], wt, small, scattering)
    partial = {n: jnp.stack([g[n] for g in grads]) for n in grads[0]}
    partial["g_final"] = dgf[0]

    own = _pack([_split_last(partial[n]) for n in small_sharded], lead=(N_DEV,))
    shared = _pack([partial[n] for n in replicated])
    slab = jnp.concatenate([own, jnp.broadcast_to(shared[None], (N_DEV,) + shared.shape)], axis=1)
    small_grads = _SplitExchange(slab, kind="scatter_blocks", name="scatter_small_grads")
    tok = _SplitExchange.start([small_grads], name="scatter_small_grads_start")
    dx_out, tok = lax.optimization_barrier((dres[0], tok))
    received = {n: [scattering[n][l].finish(dx_out) for l in layers] for n in _BIG}
    out = {}

    def update(n, parts):
        results = _adamw(parts, given[n], given["m_" + n], given["v_" + n], name=f"adamw_{n}")
        for pre, res in zip(("grad_", "delta_", "new_m_", "new_v_"), results):
            out[pre + n] = res

    update("w_out", received["w_out"])
    update("w_down", received["w_down"])
    for n in ("w_in", "w_up"):
        update(n, jnp.swapaxes(_sum_parts(received[n + "_t"], name=f"sum_{n}"), 1, 2))

    slab = small_grads.finish(out["grad_w_up"])
    stacks = [jnp.concatenate([_pack([given[pre + n] for n in small_sharded]),
                               _pack([given[pre + n] for n in replicated])])[None] for pre in ("", "m_", "v_")]
    results = _adamw([slab], *stacks, name="adamw_small")
    n_own = own.shape[1]
    for pre, res in zip(("grad_", "delta_", "new_m_", "new_v_"), results):
        unpacked = (_unpack(res[0, :n_own], [s for _, s in _SMALL_SHARDED])
                    + _unpack(res[0, n_own:], [s for _, s in _REPLICATED]))
        for n, a in zip(small_sharded + replicated, unpacked):
            out[pre + n] = a

    loss = lax.psum(loss_tile[0, 0], ("x", "y", "c"))
    order = list(_WEIGHT_ORDER)
    return (loss, dres[0][None], *[out["grad_" + n] for n in order], *[out["delta_" + n] for n in order],
            *[out["new_m_" + n] for n in order], *[out["new_v_" + n] for n in order])


_WEIGHT_ORDER = ("g_mix", "w_in", "conv_w", "conv_b", "conv_ln_g", "conv_ln_b", "sgu_ln_g", "sgu_ln_b", "sgu_w", "sgu_b",
                 "g_out", "w_out", "g_ffn", "w_up", "ffn_conv_w", "ffn_conv_b", "w_down", "g_final")
```

```python
import functools
import math

import jax
import jax.numpy as jnp
from jax import lax
from jax.experimental import pallas as pl
from jax.experimental.pallas import tpu as pltpu

F32 = jnp.float32
BF16 = jnp.bfloat16

N_DEV = 8
D_MODEL = 1024
HEAD_DIM = 64
D_CONV = 256
D_SGU = 256
D_SB = 512
D_AB = 2 * D_CONV + 2 * D_SGU
D_QKV = 3 * D_SB
D_IN = D_AB + D_QKV
CONV_K = 31
CONV_HALO = 32
FFN_K = 3
FFN_HALO = 8
D_FF = 2816
CHUNK = 128
EPS = 1e-6
LANES = 128

ADAM_LR = 0.001
ADAM_B1 = 0.9
ADAM_B2 = 0.999
ADAM_EPS = 1e-08
ADAM_WD = 0.01
ADAM_STEP = 10

VMEM_LIMIT = 56 * 1024 * 1024


def _cparams(sem=None):
    return pltpu.CompilerParams(dimension_semantics=sem, vmem_limit_bytes=VMEM_LIMIT)


def _tile(n, prefs=(512, 256, 128)):
    for t in prefs:
        if n % t == 0:
            return t
    return n


def _sigmoid(x):
    return 1.0 / (1.0 + jnp.exp(-x))


def _softplus(x):
    return jnp.maximum(x, 0.0) + jnp.log1p(jnp.exp(-jnp.abs(x)))


_INV_SQRT2 = 1.0 / math.sqrt(2.0)
_INV_SQRT2PI = 1.0 / math.sqrt(2.0 * math.pi)


def _gelu(x):
    return 0.5 * x * (1.0 + lax.erf(x * _INV_SQRT2))


def _gelu_grad(x):
    return 0.5 * (1.0 + lax.erf(x * _INV_SQRT2)) + x * jnp.exp(-0.5 * x * x) * _INV_SQRT2PI


def _dot(a, b, dims):
    return lax.dot_general(a, b, (dims, ((), ())), preferred_element_type=F32)


_NN = ((1,), (0,))
_NT = ((1,), (1,))
_TN = ((0,), (0,))


def _split_bf16(x):
    hi = x.astype(BF16)
    lo = (x - hi.astype(F32)).astype(BF16)
    return jnp.concatenate([hi, lo], axis=1)


def _matmul(a, b, mode, *, name, out_dtype=F32, residual=None, n=None, b_n0=0, b_k0=0, rows=None, into=None):
    if mode == "nn":
        (m, k), n = a.shape, (n or b.shape[1])
    elif mode == "nt":
        (m, k), n = a.shape, (n or b.shape[0])
    else:
        (k, m), n = a.shape, b.shape[1]
    has_res = residual is not None
    tm, tn = _matmul_tiles(m, n, k, a.dtype.itemsize, b.dtype.itemsize, jnp.dtype(out_dtype).itemsize, has_res, b_n0)
    j0 = b_n0 // tn
    total_rows, first_row = rows or (m, 0)
    assert b_k0 % k == 0 and first_row % tm == 0
    kb, i0 = b_k0 // k, first_row // tm

    if mode == "nn":
        a_spec = pl.BlockSpec((tm, k), lambda i, j: (i, 0))
        b_spec = pl.BlockSpec((k, tn), lambda i, j: (kb, j + j0))
        dims = _NN
    elif mode == "nt":
        a_spec = pl.BlockSpec((tm, k), lambda i, j: (i, 0))
        b_spec = pl.BlockSpec((tn, k), lambda i, j: (j + j0, 0))
        dims = _NT
    else:
        a_spec = pl.BlockSpec((k, tm), lambda i, j: (0, i))
        b_spec = pl.BlockSpec((k, tn), lambda i, j: (0, j))
        dims = _TN
    o_spec = pl.BlockSpec((tm, tn), lambda i, j: (i + i0, j))
    r_spec = pl.BlockSpec((tm, tn), lambda i, j: (i, j))

    def body(*refs):
        a_ref, b_ref = refs[:2]
        acc = _dot(a_ref[...].astype(BF16), b_ref[...].astype(BF16), dims)
        if has_res:
            acc = acc + refs[2][...]
        refs[-1][...] = acc.astype(out_dtype)

    in_specs = [a_spec, b_spec] + ([r_spec] if has_res else [])
    args = (a, b) + ((residual,) if has_res else ())
    aliases = {}
    if into is not None:
        aliases = {len(args): 0}
        in_specs.append(pl.BlockSpec(memory_space=pl.ANY))
        args += (into,)

        def body(*refs, inner=body):
            inner(*refs[:len(args) - 1], refs[-1])

    return pl.pallas_call(
        body,
        name=name,
        grid=(m // tm, n // tn),
        in_specs=in_specs,
        out_specs=o_spec,
        out_shape=jax.ShapeDtypeStruct((total_rows, n), out_dtype),
        input_output_aliases=aliases,
        compiler_params=_cparams(("parallel", "parallel")),
    )(*args)


MATMUL_VMEM_BUDGET = 40 * 1024 * 1024


def _matmul_tiles(m, n, k, a_bytes, b_bytes, out_bytes, has_res, n_offset):
    def divisors(size, cap, also=0):
        return [t for t in range(cap, 0, -LANES) if size % t == 0 and also % t == 0] or [size]

    for tm in divisors(m, 1024):
        for tn in divisors(n, 1408, n_offset):
            blocks = tm * k * a_bytes + k * tn * b_bytes + tm * tn * (out_bytes + (4 if has_res else 0))
            if 2 * blocks <= MATMUL_VMEM_BUDGET:
                return tm, tn
    raise ValueError(f"no matmul tiling for {m} x {n} x {k}")


ROW_TILE = 256


def _rmsnorm_fwd(x, g, *, name):
    s, d = x.shape

    def body(x_ref, g_ref, h_ref):
        xv = x_ref[...]
        r = lax.rsqrt(jnp.mean(xv * xv, axis=-1, keepdims=True) + EPS)
        h_ref[...] = (xv * r * g_ref[...]).astype(BF16)

    return pl.pallas_call(
        body,
        name=name,
        grid=(s // ROW_TILE,),
        in_specs=[pl.BlockSpec((ROW_TILE, d), lambda i: (i, 0)), pl.BlockSpec((1, d), lambda i: (0, 0))],
        out_specs=pl.BlockSpec((ROW_TILE, d), lambda i: (i, 0)),
        out_shape=jax.ShapeDtypeStruct((s, d), BF16),
        compiler_params=_cparams(("parallel",)),
    )(x, g)


def _rmsnorm_bwd(x, g, dh, dres, *, name):
    s, d = x.shape

    def body(x_ref, g_ref, dh_ref, dres_ref, dx_ref, dxb_ref, dg_ref):
        xv = x_ref[...]
        r = lax.rsqrt(jnp.mean(xv * xv, axis=-1, keepdims=True) + EPS)
        xhat = xv * r
        dhv = dh_ref[...]
        dxhat = dhv * g_ref[...]
        dx = dres_ref[...] + r * (dxhat - xhat * jnp.mean(dxhat * xhat, axis=-1, keepdims=True))
        dx_ref[...] = dx
        dxb_ref[...] = dx.astype(BF16)
        part = jnp.sum(dhv * xhat, axis=0, keepdims=True)

        @pl.when(pl.program_id(0) == 0)
        def _():
            dg_ref[...] = part

        @pl.when(pl.program_id(0) > 0)
        def _():
            dg_ref[...] += part

    row = pl.BlockSpec((ROW_TILE, d), lambda i: (i, 0))
    vec = pl.BlockSpec((1, d), lambda i: (0, 0))
    return pl.pallas_call(
        body,
        name=name,
        grid=(s // ROW_TILE,),
        in_specs=[row, vec, row, row],
        out_specs=[row, row, vec],
        out_shape=[jax.ShapeDtypeStruct((s, d), F32), jax.ShapeDtypeStruct((s, d), BF16),
                   jax.ShapeDtypeStruct((1, d), F32)],
        compiler_params=_cparams(("arbitrary",)),
    )(x, g, dh, dres)


def _loss_head(x, g, target, *, name):
    s, d = x.shape

    def body(x_ref, g_ref, t_ref, loss_ref, dx_ref, dxb_ref, dg_ref):
        xv = x_ref[...]
        gv = g_ref[...]
        r = lax.rsqrt(jnp.mean(xv * xv, axis=-1, keepdims=True) + EPS)
        xhat = xv * r
        diff = xhat * gv - t_ref[...]
        dy = diff * (1.0 / d)
        dxhat = dy * gv
        dx = r * (dxhat - xhat * jnp.mean(dxhat * xhat, axis=-1, keepdims=True))
        dx_ref[...] = dx
        dxb_ref[...] = dx.astype(BF16)
        dg_part = jnp.sum(dy * xhat, axis=0, keepdims=True)
        row_loss = jnp.sum(diff * diff, axis=-1, keepdims=True)
        loss_part = jnp.sum(row_loss, axis=0, keepdims=True) * (0.5 / d)

        @pl.when(pl.program_id(0) == 0)
        def _():
            dg_ref[...] = dg_part
            loss_ref[...] = jnp.broadcast_to(loss_part, loss_ref.shape)

        @pl.when(pl.program_id(0) > 0)
        def _():
            dg_ref[...] += dg_part
            loss_ref[...] += jnp.broadcast_to(loss_part, loss_ref.shape)

    row = pl.BlockSpec((ROW_TILE, d), lambda i: (i, 0))
    vec = pl.BlockSpec((1, d), lambda i: (0, 0))
    tile = pl.BlockSpec((8, LANES), lambda i: (0, 0))
    return pl.pallas_call(
        body,
        name=name,
        grid=(s // ROW_TILE,),
        in_specs=[row, vec, row],
        out_specs=[tile, row, row, vec],
        out_shape=[jax.ShapeDtypeStruct((8, LANES), F32), jax.ShapeDtypeStruct((s, d), F32),
                   jax.ShapeDtypeStruct((s, d), BF16), jax.ShapeDtypeStruct((1, d), F32)],
        compiler_params=_cparams(("arbitrary",)),
    )(x, g, target)


_BRANCHES = ((0, D_CONV), (D_CONV, D_SGU), (D_CONV + D_SGU, D_SB))


def _combine_fwd(ya, yb, yc, g, *, name):
    s = ya.shape[0]

    def body(ya_ref, yb_ref, yc_ref, g_ref, y_ref):
        for ref, (off, w) in zip((ya_ref, yb_ref, yc_ref), _BRANCHES):
            v = ref[...]
            r = lax.rsqrt(jnp.mean(v * v, axis=-1, keepdims=True) + EPS)
            y_ref[:, off:off + w] = (v * r * g_ref[:, off:off + w]).astype(BF16)

    def row(w):
        return pl.BlockSpec((ROW_TILE, w), lambda i: (i, 0))

    return pl.pallas_call(
        body,
        name=name,
        grid=(s // ROW_TILE,),
        in_specs=[row(D_CONV), row(D_SGU), row(D_SB), pl.BlockSpec((1, D_MODEL), lambda i: (0, 0))],
        out_specs=row(D_MODEL),
        out_shape=jax.ShapeDtypeStruct((s, D_MODEL), BF16),
        compiler_params=_cparams(("parallel",)),
    )(ya, yb, yc, g)


def _combine_bwd(dy, ya, yb, yc, g, *, name):
    s = ya.shape[0]

    def body(dy_ref, ya_ref, yb_ref, yc_ref, g_ref, dya_ref, dyb_ref, dyc_ref, dg_ref):
        first = pl.program_id(0) == 0
        for ref, dref, (off, w) in zip((ya_ref, yb_ref, yc_ref), (dya_ref, dyb_ref, dyc_ref), _BRANCHES):
            v = ref[...]
            r = lax.rsqrt(jnp.mean(v * v, axis=-1, keepdims=True) + EPS)
            n = v * r
            dout = dy_ref[:, off:off + w]
            dn = dout * g_ref[:, off:off + w]
            dref[...] = r * (dn - n * jnp.mean(dn * n, axis=-1, keepdims=True))
            part = jnp.sum(dout * n, axis=0, keepdims=True)

            @pl.when(first)
            def _():
                dg_ref[:, off:off + w] = part

            @pl.when(jnp.logical_not(first))
            def _():
                dg_ref[:, off:off + w] += part

    def row(w):
        return pl.BlockSpec((ROW_TILE, w), lambda i: (i, 0))

    vec = pl.BlockSpec((1, D_MODEL), lambda i: (0, 0))
    return pl.pallas_call(
        body,
        name=name,
        grid=(s // ROW_TILE,),
        in_specs=[row(D_MODEL), row(D_CONV), row(D_SGU), row(D_SB), vec],
        out_specs=[row(D_CONV), row(D_SGU), row(D_SB), vec],
        out_shape=[jax.ShapeDtypeStruct((s, D_CONV), F32), jax.ShapeDtypeStruct((s, D_SGU), F32),
                   jax.ShapeDtypeStruct((s, D_SB), F32), jax.ShapeDtypeStruct((1, D_MODEL), F32)],
        compiler_params=_cparams(("arbitrary",)),
    )(dy, ya, yb, yc, g)


CONV_TILE = 128


def _shift_down(window, j, halo):
    return pltpu.roll(window, j, 0)[halo:, :] if j else window[halo:, :]


def _shift_up(window, j, n_out):
    n = window.shape[0]
    return pltpu.roll(window, n - j, 0)[:n_out, :] if j else window[:n_out, :]


def _mixer_a_fwd(p_ab, conv_w, conv_b, ln_g, ln_b, *, name):
    s = p_ab.shape[0]
    nt = s // CONV_TILE

    def body(p_ref, w_ref, b_ref, g_ref, beta_ref, y_ref, h_ref):
        h_ref[0:CONV_HALO, :] = jnp.zeros((CONV_HALO, D_CONV), F32)

        def glu(i, c):
            t0 = pl.multiple_of(i * CONV_TILE, CONV_TILE)
            a = p_ref[pl.ds(t0, CONV_TILE), 0:D_CONV]
            gate = p_ref[pl.ds(t0, CONV_TILE), D_CONV:2 * D_CONV]
            h_ref[pl.ds(t0 + CONV_HALO, CONV_TILE), :] = a * _sigmoid(gate)
            return c

        lax.fori_loop(0, nt, glu, 0)

        def conv(i, c):
            t0 = pl.multiple_of(i * CONV_TILE, CONV_TILE)
            window = h_ref[pl.ds(t0, CONV_TILE + CONV_HALO), :]
            acc = jnp.zeros((CONV_TILE, D_CONV), F32) + b_ref[...]
            for k in range(CONV_K):
                acc = acc + w_ref[k:k + 1, :] * _shift_down(window, CONV_K - 1 - k, CONV_HALO)
            mu = jnp.mean(acc, axis=-1, keepdims=True)
            xc = acc - mu
            rstd = lax.rsqrt(jnp.mean(xc * xc, axis=-1, keepdims=True) + EPS)
            z = xc * rstd * g_ref[...] + beta_ref[...]
            y_ref[pl.ds(t0, CONV_TILE), :] = z * _sigmoid(z)
            return c

        lax.fori_loop(0, nt, conv, 0)

    full = lambda shape: pl.BlockSpec(shape, lambda i: (0, 0))
    return pl.pallas_call(
        body,
        name=name,
        grid=(1,),
        in_specs=[full((s, 2 * D_CONV)), full((CONV_K, D_CONV)), full((1, D_CONV)), full((1, D_CONV)),
                  full((1, D_CONV))],
        out_specs=full((s, D_CONV)),
        out_shape=jax.ShapeDtypeStruct((s, D_CONV), F32),
        scratch_shapes=[pltpu.VMEM((s + CONV_HALO, D_CONV), F32)],
        compiler_params=_cparams(("arbitrary",)),
    )(p_ab, conv_w, conv_b, ln_g, ln_b)


def _mixer_a_bwd(p_ab, dya, conv_w, conv_b, ln_g, ln_b, *, name):
    s = p_ab.shape[0]
    nt = s // CONV_TILE

    def body(p_ref, dy_ref, w_ref, b_ref, g_ref, beta_ref, dp_ref, dw_ref, db_ref, dg_ref, dbeta_ref, h_ref, dc_ref):
        h_ref[0:CONV_HALO, :] = jnp.zeros((CONV_HALO, D_CONV), F32)
        dc_ref[s:s + CONV_HALO, :] = jnp.zeros((CONV_HALO, D_CONV), F32)
        dw_ref[...] = jnp.zeros_like(dw_ref)
        db_ref[...] = jnp.zeros_like(db_ref)
        dg_ref[...] = jnp.zeros_like(dg_ref)
        dbeta_ref[...] = jnp.zeros_like(dbeta_ref)

        def glu(i, c):
            t0 = pl.multiple_of(i * CONV_TILE, CONV_TILE)
            a = p_ref[pl.ds(t0, CONV_TILE), 0:D_CONV]
            gate = p_ref[pl.ds(t0, CONV_TILE), D_CONV:2 * D_CONV]
            h_ref[pl.ds(t0 + CONV_HALO, CONV_TILE), :] = a * _sigmoid(gate)
            return c

        lax.fori_loop(0, nt, glu, 0)

        def conv_bwd(i, c):
            t0 = pl.multiple_of(i * CONV_TILE, CONV_TILE)
            window = h_ref[pl.ds(t0, CONV_TILE + CONV_HALO), :]
            taps = [_shift_down(window, CONV_K - 1 - k, CONV_HALO) for k in range(CONV_K)]
            acc = jnp.zeros((CONV_TILE, D_CONV), F32) + b_ref[...]
            for k in range(CONV_K):
                acc = acc + w_ref[k:k + 1, :] * taps[k]
            mu = jnp.mean(acc, axis=-1, keepdims=True)
            xc = acc - mu
            rstd = lax.rsqrt(jnp.mean(xc * xc, axis=-1, keepdims=True) + EPS)
            xhat = xc * rstd
            z = xhat * g_ref[...] + beta_ref[...]
            sg = _sigmoid(z)
            dz = dy_ref[pl.ds(t0, CONV_TILE), :] * (sg * (1.0 + z * (1.0 - sg)))
            dg_ref[...] += jnp.sum(dz * xhat, axis=0, keepdims=True)
            dbeta_ref[...] += jnp.sum(dz, axis=0, keepdims=True)
            dxhat = dz * g_ref[...]
            dc = rstd * (dxhat - jnp.mean(dxhat, axis=-1, keepdims=True)
                         - xhat * jnp.mean(dxhat * xhat, axis=-1, keepdims=True))
            dc_ref[pl.ds(t0, CONV_TILE), :] = dc
            db_ref[...] += jnp.sum(dc, axis=0, keepdims=True)
            for k in range(CONV_K):
                dw_ref[k:k + 1, :] += jnp.sum(dc * taps[k], axis=0, keepdims=True)
            return c

        lax.fori_loop(0, nt, conv_bwd, 0)

        def glu_bwd(i, c):
            t0 = pl.multiple_of(i * CONV_TILE, CONV_TILE)
            window = dc_ref[pl.ds(t0, CONV_TILE + CONV_HALO), :]
            dh = jnp.zeros((CONV_TILE, D_CONV), F32)
            for j in range(CONV_K):
                dh = dh + w_ref[CONV_K - 1 - j:CONV_K - j, :] * _shift_up(window, j, CONV_TILE)
            a = p_ref[pl.ds(t0, CONV_TILE), 0:D_CONV]
            sg = _sigmoid(p_ref[pl.ds(t0, CONV_TILE), D_CONV:2 * D_CONV])
            dp_ref[pl.ds(t0, CONV_TILE), 0:D_CONV] = dh * sg
            dp_ref[pl.ds(t0, CONV_TILE), D_CONV:2 * D_CONV] = dh * a * sg * (1.0 - sg)
            return c

        lax.fori_loop(0, nt, glu_bwd, 0)

    full = lambda shape: pl.BlockSpec(shape, lambda i: (0, 0))
    vec = jax.ShapeDtypeStruct((1, D_CONV), F32)
    return pl.pallas_call(
        body,
        name=name,
        grid=(1,),
        in_specs=[full((s, 2 * D_CONV)), full((s, D_CONV)), full((CONV_K, D_CONV)), full((1, D_CONV)),
                  full((1, D_CONV)), full((1, D_CONV))],
        out_specs=[full((s, 2 * D_CONV)), full((CONV_K, D_CONV)), full((1, D_CONV)), full((1, D_CONV)),
                   full((1, D_CONV))],
        out_shape=[jax.ShapeDtypeStruct((s, 2 * D_CONV), F32), jax.ShapeDtypeStruct((CONV_K, D_CONV), F32),
                   vec, vec, vec],
        scratch_shapes=[pltpu.VMEM((s + CONV_HALO, D_CONV), F32), pltpu.VMEM((s + CONV_HALO, D_CONV), F32)],
        compiler_params=_cparams(("arbitrary",)),
    )(p_ab, dya, conv_w, conv_b, ln_g, ln_b)


N_SGU_HEADS = D_SGU // HEAD_DIM


def _head_masks(width):
    lane = lax.broadcasted_iota(jnp.int32, (1, width), 1)
    return [(lane >= h * HEAD_DIM) & (lane < (h + 1) * HEAD_DIM) for h in range(width // HEAD_DIM)]


def _tril_mask():
    r = lax.broadcasted_iota(jnp.int32, (CHUNK, CHUNK), 0)
    c = lax.broadcasted_iota(jnp.int32, (CHUNK, CHUNK), 1)
    return c <= r


def _sgu_norm(bv, g, beta):
    vg = _gelu(bv)
    mu = jnp.mean(vg, axis=-1, keepdims=True)
    xc = vg - mu
    rstd = lax.rsqrt(jnp.mean(xc * xc, axis=-1, keepdims=True) + EPS)
    xhat = xc * rstd
    return xhat, rstd, xhat * g + beta


def _sgu_fwd(p_ab, ln_g, ln_b, w_s, bias, *, name):
    s = p_ab.shape[0]

    def body(p_ref, g_ref, beta_ref, w_ref, bias_ref, y_ref):
        u = _gelu(p_ref[:, 0:D_SGU])
        _, _, vn = _sgu_norm(p_ref[:, D_SGU:2 * D_SGU], g_ref[...], beta_ref[...])
        vb = vn.astype(BF16)
        tril = _tril_mask()
        mixed = bias_ref[...]
        for h, m in enumerate(_head_masks(D_SGU)):
            wh = jnp.where(tril, w_ref[h], 0.0).astype(BF16)
            mixed = mixed + _dot(wh, jnp.where(m, vb, jnp.zeros_like(vb)), _NN)
        y_ref[...] = u * mixed

    return pl.pallas_call(
        body,
        name=name,
        grid=(s // CHUNK,),
        in_specs=[pl.BlockSpec((CHUNK, 2 * D_SGU), lambda i: (i, 1)),
                  pl.BlockSpec((1, D_SGU), lambda i: (0, 0)), pl.BlockSpec((1, D_SGU), lambda i: (0, 0)),
                  pl.BlockSpec((N_SGU_HEADS, CHUNK, CHUNK), lambda i: (0, 0, 0)),
                  pl.BlockSpec((CHUNK, D_SGU), lambda i: (0, 0))],
        out_specs=pl.BlockSpec((CHUNK, D_SGU), lambda i: (i, 0)),
        out_shape=jax.ShapeDtypeStruct((s, D_SGU), F32),
        compiler_params=_cparams(("parallel",)),
    )(p_ab, ln_g, ln_b, w_s, bias)


def _sgu_bwd(p_ab, dyb, ln_g, ln_b, w_s, bias, *, name):
    s = p_ab.shape[0]
    n_chunks = s // CHUNK

    def body(p_ref, dy_ref, g_ref, beta_ref, w_ref, bias_ref, dp_ref, dw_ref, db_ref, dg_ref, dbeta_ref, dbias_ref):
        @pl.when(pl.program_id(0) == 0)
        def _():
            dw_ref[...] = jnp.zeros_like(dw_ref)
            dbias_ref[...] = jnp.zeros_like(dbias_ref)
            dg_ref[...] = jnp.zeros_like(dg_ref)
            dbeta_ref[...] = jnp.zeros_like(dbeta_ref)

        bu = p_ref[:, 0:D_SGU]
        bv = p_ref[:, D_SGU:2 * D_SGU]
        u = _gelu(bu)
        gv = g_ref[...]
        xhat, rstd, vn = _sgu_norm(bv, gv, beta_ref[...])
        vb = vn.astype(BF16)
        tril = _tril_mask()
        masks = _head_masks(D_SGU)
        whs = [jnp.where(tril, w_ref[h], 0.0).astype(BF16) for h in range(N_SGU_HEADS)]
        mixed = bias_ref[...]
        for h, m in enumerate(masks):
            mixed = mixed + _dot(whs[h], jnp.where(m, vb, jnp.zeros_like(vb)), _NN)
        dy = dy_ref[...]
        dp_ref[:, 0:D_SGU] = dy * mixed * _gelu_grad(bu)
        dmixed = dy * u
        dbias_ref[...] += dmixed
        dmb = dmixed.astype(BF16)
        dvn = jnp.zeros((CHUNK, D_SGU), F32)
        for h, m in enumerate(masks):
            dmh = jnp.where(m, dmb, jnp.zeros_like(dmb))
            dvn = dvn + _dot(whs[h], dmh, _TN)
            dw_ref[h] += jnp.where(tril, _dot(dmh, vb, _NT), 0.0)
        dg_ref[...] += jnp.sum(dvn * xhat, axis=0, keepdims=True)
        dbeta_ref[...] += jnp.sum(dvn, axis=0, keepdims=True)
        dxhat = dvn * gv
        dvg = rstd * (dxhat - jnp.mean(dxhat, axis=-1, keepdims=True)
                      - xhat * jnp.mean(dxhat * xhat, axis=-1, keepdims=True))
        dp_ref[:, D_SGU:2 * D_SGU] = dvg * _gelu_grad(bv)

        @pl.when(pl.program_id(0) == n_chunks - 1)
        def _():
            chan = lax.broadcasted_iota(jnp.int32, (D_SGU, LANES), 0)
            head = lax.broadcasted_iota(jnp.int32, (D_SGU, LANES), 1)
            to_head = jnp.where(chan // HEAD_DIM == head, 1.0, 0.0).astype(BF16)
            db_ref[...] = _dot(_split_bf16(dbias_ref[...]), jnp.concatenate([to_head, to_head], axis=0), _NN)

    vec = pl.BlockSpec((1, D_SGU), lambda i: (0, 0))
    wspec = pl.BlockSpec((N_SGU_HEADS, CHUNK, CHUNK), lambda i: (0, 0, 0))
    bspec = pl.BlockSpec((CHUNK, D_SGU), lambda i: (0, 0))
    return pl.pallas_call(
        body,
        name=name,
        grid=(n_chunks,),
        in_specs=[pl.BlockSpec((CHUNK, 2 * D_SGU), lambda i: (i, 1)), pl.BlockSpec((CHUNK, D_SGU), lambda i: (i, 0)),
                  vec, vec, wspec, bspec],
        out_specs=[pl.BlockSpec((CHUNK, 2 * D_SGU), lambda i: (i, 0)), wspec,
                   pl.BlockSpec((CHUNK, LANES), lambda i: (0, 0)), vec, vec],
        out_shape=[jax.ShapeDtypeStruct((s, 2 * D_SGU), F32),
                   jax.ShapeDtypeStruct((N_SGU_HEADS, CHUNK, CHUNK), F32),
                   jax.ShapeDtypeStruct((CHUNK, LANES), F32),
                   jax.ShapeDtypeStruct((1, D_SGU), F32), jax.ShapeDtypeStruct((1, D_SGU), F32)],
        scratch_shapes=[pltpu.VMEM((CHUNK, D_SGU), F32)],
        compiler_params=_cparams(("arbitrary",)),
    )(p_ab, dyb, ln_g, ln_b, w_s, bias)


N_PAIRS = D_SB // LANES
SB_SCALE = HEAD_DIM ** -0.5


def _sb_logits(z, valid):
    nz = -z
    t = jnp.log(1.0 + jnp.exp(jnp.minimum(z, nz)))
    l1 = jnp.minimum(nz, 0.0) - t
    if valid is not None:
        l1 = jnp.where(valid, l1, 0.0)
    return l1, jnp.minimum(z, 0.0) - t


def _split_hi_lo(x):
    hi = lax.bitcast_convert_type(lax.bitcast_convert_type(x, jnp.uint32) & jnp.uint32(0xFFFF0000), F32)
    return jnp.concatenate([hi, x - hi], axis=1)


def _cumsum_operand(keep):
    half = jnp.concatenate([keep.astype(F32), jnp.ones((CHUNK, CHUNK), F32)], axis=1)
    return jnp.concatenate([half, half], axis=0)


Q_BLOCKS_PER_STEP = 4


def _q_blocks_per_step(nq):
    return next(n for n in (Q_BLOCKS_PER_STEP, 2, 1) if nq % n == 0)


def _attn_fwd(qkv, *, name):
    s = qkv.shape[0]
    nq = s // CHUNK
    per_step = _q_blocks_per_step(nq)

    def body(q_ref, k_ref, v_ref, o_ref, t_ref):
        masks = _head_masks(LANES)
        row = lax.broadcasted_iota(jnp.int32, (CHUNK, CHUNK), 0)
        col = lax.broadcasted_iota(jnp.int32, (CHUNK, CHUNK), 1)
        after_op = _cumsum_operand(row > col)
        cmr = col - row
        zc = jnp.zeros((CHUNK, LANES), F32)

        def q_block(sub, _):
            qi = pl.program_id(1) * per_step + sub
            q_rows = pl.ds(pl.multiple_of(sub * CHUNK, CHUNK), CHUNK)
            q = q_ref[q_rows, :] * SB_SCALE
            zero = jnp.zeros_like(q)
            qs = [jnp.where(m, q, zero) for m in masks]

            def blocks(js, carry):
                o, c0, c1 = carry
                kvs, valids = [], []
                for j in js:
                    k0 = pl.multiple_of(jnp.maximum(j, 0) * CHUNK, CHUNK)
                    kvs.append((k_ref[pl.ds(k0, CHUNK), :], v_ref[pl.ds(k0, CHUNK), :]))
                    valids.append(cmr < jnp.where(j >= 0, (qi - j) * CHUNK, -CHUNK))
                units = [(h, b) for b in range(len(js)) for h in range(2)]
                zs = [_dot(qs[h], kvs[b][0], _NT) for h, b in units]
                logits = [_sb_logits(z, valids[b]) for z, (h, b) in zip(zs, units)]
                sums = [_dot(_split_hi_lo(l1), after_op, _NN) for l1, _ in logits]
                cs = [c0, c1]
                probs = []
                for (h, b), (_, lb), sm in zip(units, logits, sums):
                    probs.append(jnp.where(valids[b], jnp.exp(lb + sm[:, :CHUNK] + cs[h]), 0.0))
                    cs[h] = cs[h] + sm[:, CHUNK:]
                for (h, b), a in zip(units, probs):
                    o = o + _dot(a.astype(BF16), jnp.where(masks[h], kvs[b][1], zero), _NN)
                return o, cs[0], cs[1]

            n_four = (qi + 1) // 4
            carry = lax.fori_loop(0, n_four, lambda jj, c: blocks([qi - 4 * jj - i for i in range(4)], c), (zc,) * 3)
            top = qi - 4 * n_four
            o, c0, c1 = lax.fori_loop(0, (top + 2) // 2, lambda jj, c: blocks([top - 2 * jj, top - 2 * jj - 1], c), carry)
            o_ref[q_rows, :] = o
            t_ref[q_rows, 0:LANES] = c0
            t_ref[q_rows, LANES:2 * LANES] = c1
            return 0

        lax.fori_loop(0, per_step, q_block, 0)

    rows = per_step * CHUNK
    return pl.pallas_call(
        body,
        name=name,
        grid=(N_PAIRS, nq // per_step),
        in_specs=[pl.BlockSpec((rows, LANES), lambda p, i: (i, p)),
                  pl.BlockSpec((s, LANES), lambda p, i: (0, N_PAIRS + p)),
                  pl.BlockSpec((s, LANES), lambda p, i: (0, 2 * N_PAIRS + p))],
        out_specs=[pl.BlockSpec((rows, LANES), lambda p, i: (i, p)),
                   pl.BlockSpec((rows, 2 * LANES), lambda p, i: (i, p))],
        out_shape=[jax.ShapeDtypeStruct((s, D_SB), F32), jax.ShapeDtypeStruct((s, 2 * D_SB), F32)],
        compiler_params=_cparams(("parallel", "parallel")),
    )(qkv, qkv, qkv)


def _attn_bwd(qkv, t_tot, do, *, name):
    s = qkv.shape[0]
    nq = s // CHUNK
    per_step = _q_blocks_per_step(nq)

    def body(q_ref, k_ref, v_ref, t_ref, do_ref, dq_ref, dk_ref, dv_ref):
        @pl.when(pl.program_id(1) == 0)
        def _():
            dk_ref[...] = jnp.zeros_like(dk_ref)
            dv_ref[...] = jnp.zeros_like(dv_ref)

        masks = _head_masks(LANES)
        row = lax.broadcasted_iota(jnp.int32, (CHUNK, CHUNK), 0)
        col = lax.broadcasted_iota(jnp.int32, (CHUNK, CHUNK), 1)
        upto_op = _cumsum_operand(row <= col)
        before_op = _cumsum_operand(row < col)
        cmr = col - row
        zc = jnp.zeros((CHUNK, LANES), F32)

        def q_block(sub, _):
            qi = pl.program_id(1) * per_step + sub
            q_rows = pl.ds(pl.multiple_of(sub * CHUNK, CHUNK), CHUNK)
            q = q_ref[q_rows, :] * SB_SCALE
            dob = do_ref[q_rows, :].astype(BF16)
            zero = jnp.zeros_like(q)
            qs = [jnp.where(m, q, zero) for m in masks]
            dos = [jnp.where(m, dob, zero) for m in masks]
            tots = [t_ref[q_rows, 0:LANES], t_ref[q_rows, LANES:2 * LANES]]

            def blocks(js, carry):
                dq, cl0, cl1, cp0, cp1 = carry
                starts = [pl.multiple_of(jnp.minimum(j, nq - 1) * CHUNK, CHUNK) for j in js]
                valids = [cmr < (qi - j) * CHUNK for j in js]
                kvs = [(k_ref[pl.ds(k0, CHUNK), :], v_ref[pl.ds(k0, CHUNK), :]) for k0 in starts]
                units = [(h, b) for b in range(len(js)) for h in range(2)]
                zs = [_dot(qs[h], kvs[b][0], _NT) for h, b in units]
                das = [_dot(dos[h], kvs[b][1], _NT) for h, b in units]
                logits = [_sb_logits(z, valids[b]) for z, (h, b) in zip(zs, units)]
                sums = [_dot(_split_hi_lo(l1), upto_op, _NN) for l1, _ in logits]
                cls, cps = [cl0, cl1], [cp0, cp1]
                probs, gs = [], []
                for (h, b), (_, lb), sm, da in zip(units, logits, sums, das):
                    a = jnp.where(valids[b], jnp.exp(lb + (tots[h] - cls[h] - sm[:, :CHUNK])), 0.0)
                    probs.append(a)
                    gs.append(a * da)
                    cls[h] = cls[h] + sm[:, CHUNK:]
                sums_g = [_dot(_split_hi_lo(g), before_op, _NN) for g in gs]
                dzs = []
                for (h, b), (_, lb), g, sg in zip(units, logits, gs, sums_g):
                    dz = g - (g + sg[:, :CHUNK] + cps[h]) * jnp.exp(lb)
                    dzs.append(jnp.where(valids[b], dz, 0.0).astype(BF16))
                    cps[h] = cps[h] + sg[:, CHUNK:]
                for (h, b), dzb in zip(units, dzs):
                    dq = dq + _dot(dzb, jnp.where(masks[h], kvs[b][0], zero), _NN)
                for b, k0 in enumerate(starts):
                    dk_ref[pl.ds(k0, CHUNK), :] += _dot(dzs[2 * b], qs[0], _TN) + _dot(dzs[2 * b + 1], qs[1], _TN)
                    dv_ref[pl.ds(k0, CHUNK), :] += (_dot(probs[2 * b].astype(BF16), dos[0], _TN)
                                                    + _dot(probs[2 * b + 1].astype(BF16), dos[1], _TN))
                return dq, cls[0], cls[1], cps[0], cps[1]

            n_four = (qi + 1) // 4
            carry = lax.fori_loop(0, n_four, lambda jj, c: blocks([4 * jj + i for i in range(4)], c), (zc,) * 5)
            base = 4 * n_four
            carry = lax.fori_loop(0, (qi - base + 2) // 2, lambda jj, c: blocks([base + 2 * jj, base + 2 * jj + 1], c), carry)
            dq_ref[q_rows, :] = carry[0] * SB_SCALE
            return 0

        lax.fori_loop(0, per_step, q_block, 0)

    rows = per_step * CHUNK
    blk = pl.BlockSpec((rows, LANES), lambda p, i: (i, p))
    col_blk = pl.BlockSpec((s, LANES), lambda p, i: (0, p))
    out = jax.ShapeDtypeStruct((s, D_SB), F32)
    return pl.pallas_call(
        body,
        name=name,
        grid=(N_PAIRS, nq // per_step),
        in_specs=[blk,
                  pl.BlockSpec((s, LANES), lambda p, i: (0, N_PAIRS + p)),
                  pl.BlockSpec((s, LANES), lambda p, i: (0, 2 * N_PAIRS + p)),
                  pl.BlockSpec((rows, 2 * LANES), lambda p, i: (i, p)),
                  blk],
        out_specs=[blk, col_blk, col_blk],
        out_shape=[out, out, out],
        compiler_params=_cparams(("parallel", "arbitrary")),
    )(qkv, qkv, qkv, t_tot, do)


FFN_TILE = 256
FFN_COLS = 256
N_FF_BLOCKS = D_FF // FFN_COLS


def _ffn_act_fwd(up0, conv_w, conv_b, *, name):
    s = up0.shape[0]
    nt = s // FFN_TILE

    def body(xg_ref, xv_ref, wg_ref, wv_ref, bg_ref, bv_ref, act_ref, pg_ref, pv_ref):
        pg_ref[0:FFN_HALO, :] = jnp.zeros((FFN_HALO, FFN_COLS), F32)
        pv_ref[0:FFN_HALO, :] = jnp.zeros((FFN_HALO, FFN_COLS), F32)
        pg_ref[FFN_HALO:, :] = xg_ref[...]
        pv_ref[FFN_HALO:, :] = xv_ref[...]

        def tile(i, c):
            t0 = pl.multiple_of(i * FFN_TILE, FFN_TILE)
            outs = []
            for p_ref, w_ref, b_ref in ((pg_ref, wg_ref, bg_ref), (pv_ref, wv_ref, bv_ref)):
                window = p_ref[pl.ds(t0, FFN_TILE + FFN_HALO), :]
                acc = b_ref[...] + w_ref[2:3, :] * window[FFN_HALO:, :]
                for j in range(1, FFN_K):
                    acc = acc + w_ref[FFN_K - 1 - j:FFN_K - j, :] * _shift_down(window, j, FFN_HALO)
                outs.append(acc)
            gate, val = outs
            act_ref[pl.ds(t0, FFN_TILE), :] = (gate * _sigmoid(gate) * val).astype(BF16)
            return c

        lax.fori_loop(0, nt, tile, 0)

    gcol = lambda rows: pl.BlockSpec((rows, FFN_COLS), lambda j: (0, j))
    vcol = lambda rows: pl.BlockSpec((rows, FFN_COLS), lambda j: (0, j + N_FF_BLOCKS))
    return pl.pallas_call(
        body,
        name=name,
        grid=(N_FF_BLOCKS,),
        in_specs=[gcol(s), vcol(s), gcol(FFN_K), vcol(FFN_K), gcol(1), vcol(1)],
        out_specs=gcol(s),
        out_shape=jax.ShapeDtypeStruct((s, D_FF), BF16),
        scratch_shapes=[pltpu.VMEM((s + FFN_HALO, FFN_COLS), F32), pltpu.VMEM((s + FFN_HALO, FFN_COLS), F32)],
        compiler_params=_cparams(("parallel",)),
    )(up0, up0, conv_w, conv_w, conv_b, conv_b)


def _ffn_act_bwd(up0, dact, conv_w, conv_b, *, name):
    s = up0.shape[0]
    nt = s // FFN_TILE

    def body(xg_ref, xv_ref, da_ref, wg_ref, wv_ref, bg_ref, bv_ref, dxg_ref, dxv_ref, dwg_ref, dwv_ref, dbg_ref, dbv_ref,
             pg_ref, pv_ref, dg_ref, dv_ref):
        zeros = jnp.zeros((FFN_HALO, FFN_COLS), F32)
        for p_ref, x_ref in ((pg_ref, xg_ref), (pv_ref, xv_ref)):
            p_ref[0:FFN_HALO, :] = zeros
            p_ref[FFN_HALO:, :] = x_ref[...]
        dg_ref[s:s + FFN_HALO, :] = zeros
        dv_ref[s:s + FFN_HALO, :] = zeros
        for ref in (dwg_ref, dwv_ref, dbg_ref, dbv_ref):
            ref[...] = jnp.zeros_like(ref)

        def conv(p_ref, w_ref, b_ref, t0):
            window = p_ref[pl.ds(t0, FFN_TILE + FFN_HALO), :]
            taps = [_shift_down(window, j, FFN_HALO) for j in range(FFN_K)]
            out = b_ref[...]
            for j in range(FFN_K):
                out = out + w_ref[FFN_K - 1 - j:FFN_K - j, :] * taps[j]
            return out, taps

        def tile(i, c):
            t0 = pl.multiple_of(i * FFN_TILE, FFN_TILE)
            gate, taps_g = conv(pg_ref, wg_ref, bg_ref, t0)
            val, taps_v = conv(pv_ref, wv_ref, bv_ref, t0)
            da = da_ref[pl.ds(t0, FFN_TILE), :]
            sg = lax.logistic(gate)
            dgate = da * val * (sg * (1.0 + gate * (1.0 - sg)))
            dval = da * gate * sg
            dg_ref[pl.ds(t0, FFN_TILE), :] = dgate
            dv_ref[pl.ds(t0, FFN_TILE), :] = dval
            dbg_ref[...] += jnp.sum(dgate, axis=0, keepdims=True)
            dbv_ref[...] += jnp.sum(dval, axis=0, keepdims=True)
            for j in range(FFN_K):
                dwg_ref[FFN_K - 1 - j:FFN_K - j, :] += jnp.sum(dgate * taps_g[j], axis=0, keepdims=True)
                dwv_ref[FFN_K - 1 - j:FFN_K - j, :] += jnp.sum(dval * taps_v[j], axis=0, keepdims=True)
            return c

        lax.fori_loop(0, nt, tile, 0)

        def tile_dx(i, c):
            t0 = pl.multiple_of(i * FFN_TILE, FFN_TILE)
            for d_ref, w_ref, dx_ref in ((dg_ref, wg_ref, dxg_ref), (dv_ref, wv_ref, dxv_ref)):
                window = d_ref[pl.ds(t0, FFN_TILE + FFN_HALO), :]
                dx = w_ref[FFN_K - 1:FFN_K, :] * window[:FFN_TILE, :]
                for j in range(1, FFN_K):
                    dx = dx + w_ref[FFN_K - 1 - j:FFN_K - j, :] * _shift_up(window, j, FFN_TILE)
                dx_ref[pl.ds(t0, FFN_TILE), :] = dx.astype(BF16)
            return c

        lax.fori_loop(0, nt, tile_dx, 0)

    gcol = lambda rows: pl.BlockSpec((rows, FFN_COLS), lambda j: (0, j))
    vcol = lambda rows: pl.BlockSpec((rows, FFN_COLS), lambda j: (0, j + N_FF_BLOCKS))
    half = lambda rows, dtype: jax.ShapeDtypeStruct((rows, D_FF), dtype)
    padded = pltpu.VMEM((s + FFN_HALO, FFN_COLS), F32)
    return pl.pallas_call(
        body,
        name=name,
        grid=(N_FF_BLOCKS,),
        in_specs=[gcol(s), vcol(s), gcol(s), gcol(FFN_K), vcol(FFN_K), gcol(1), vcol(1)],
        out_specs=[gcol(s), gcol(s), gcol(FFN_K), gcol(FFN_K), gcol(1), gcol(1)],
        out_shape=[half(s, BF16), half(s, BF16), half(FFN_K, F32), half(FFN_K, F32), half(1, F32), half(1, F32)],
        scratch_shapes=[padded, padded, padded, padded],
        compiler_params=_cparams(("parallel",)),
    )(up0, up0, dact, conv_w, conv_w, conv_b, conv_b)


MESH = pl.DeviceIdType.MESH


def _position():
    x, y, c = lax.axis_index("x"), lax.axis_index("y"), lax.axis_index("c")
    return x, y, c, 4 * x + 2 * y + c


def _peer(k):
    x, y, c, _ = _position()
    px = 1 - x if k & 4 else x
    py = 1 - y if k & 2 else y
    pc = 1 - c if k & 1 else c
    return (px, py, pc), 4 * px + 2 * py + pc


def _exchange(src, *, kind, name):
    ex = _Exchanges([(kind, src)])

    def body(src_ref, out_ref, send_sems, recv_sems, local_sems):
        ex.start([src_ref], [out_ref], send_sems, recv_sems, local_sems)
        ex.wait([src_ref], [out_ref], send_sems, recv_sems, local_sems)

    return pl.pallas_call(
        body,
        name=name,
        in_specs=ex.in_specs,
        out_specs=ex.out_specs[0],
        out_shape=ex.out_shapes[0],
        scratch_shapes=ex.scratch_shapes,
    )(src)


_HBM = pl.BlockSpec(memory_space=pltpu.HBM)
_SEM = pl.BlockSpec(memory_space=pltpu.SEMAPHORE)
_DATAFLOW = pltpu.SideEffectType.DATAFLOW_SIDE_EFFECTING
N_PEERS = N_DEV - 1


class _SplitExchange:
    def __init__(self, src, *, kind, name):
        self.kind, self.name, self.dtype = kind, name, src.dtype
        scatter = kind.startswith("scatter")
        by_blocks = kind == "scatter_blocks"
        self.scatter, self.by_blocks = scatter, by_blocks
        if by_blocks:
            self.r, self.cols, self.land_shape = None, None, src.shape
        else:
            self.r = src.shape[0] // N_DEV if scatter else src.shape[0]
            self.cols = src.shape[1]
            self.land_shape = (N_DEV, self.r, self.cols) if scatter else (N_DEV * self.r, self.cols)
        r = self.r

        def copies(src_ref, land_ref, send_sems, recv_sems, local_sem):
            me = _position()[3]

            def rows(ref, idx):
                return ref.at[pl.ds(pl.multiple_of(idx * r, r), r), :]

            if by_blocks:
                outgoing = lambda idx: src_ref.at[idx]
            else:
                outgoing = (lambda idx: rows(src_ref, idx)) if scatter else (lambda idx: src_ref)
            slot = (lambda idx: land_ref.at[idx]) if scatter else (lambda idx: rows(land_ref, idx))
            sends, recvs = [], []
            for k in range(1, N_DEV):
                peer, pidx = _peer(k)
                sems = dict(send_sem=send_sems[k - 1], recv_sem=recv_sems[k - 1], device_id=peer, device_id_type=MESH)
                sends.append(pltpu.make_async_remote_copy(src_ref=outgoing(pidx), dst_ref=slot(me), **sems))
                recvs.append(pltpu.make_async_remote_copy(src_ref=outgoing(pidx), dst_ref=slot(pidx), **sems))
            return sends, recvs, pltpu.make_async_copy(outgoing(me), slot(me), local_sem)

        self._copies = copies
        self.src = src

    @staticmethod
    def start(exchanges, name):
        n = len(exchanges)
        per = 2 * N_PEERS + 1

        def start_body(*refs):
            outs = refs[2 * n:]
            for i, ex in enumerate(exchanges):
                sems = outs[per * i:per * (i + 1)]
                sends, _, local = ex._copies(refs[2 * i], refs[2 * i + 1], sems[:N_PEERS], sems[N_PEERS:-1], sems[-1])
                for cp in sends + [local]:
                    cp.start()
            outs[-1][...] = jnp.zeros_like(outs[-1])

        sem = pltpu.SemaphoreType.DMA(())
        operands, thru_shapes = [], []
        for ex in exchanges:
            operands += [pltpu.with_memory_space_constraint(ex.src, pltpu.HBM),
                         pltpu.with_memory_space_constraint(lax.empty(ex.land_shape, ex.dtype), pltpu.HBM)]
            thru_shapes += [pltpu.HBM(ex.src.shape, ex.dtype), pltpu.HBM(ex.land_shape, ex.dtype)]
        out = pl.pallas_call(
            start_body,
            name=name,
            in_specs=(_HBM,) * (2 * n),
            out_specs=(_SEM,) * (per * n) + (_HBM,) * (2 * n) + (pl.BlockSpec(memory_space=pltpu.VMEM),),
            out_shape=(sem,) * (per * n) + tuple(thru_shapes) + (jax.ShapeDtypeStruct((8, LANES), F32),),
            input_output_aliases={i: per * n + i for i in range(2 * n)},
            compiler_params=pltpu.CompilerParams(has_side_effects=_DATAFLOW),
        )(*operands)
        for i, ex in enumerate(exchanges):
            ex.sems = out[per * i:per * (i + 1)]
            ex.src_thru, ex.land_thru = out[per * n + 2 * i], out[per * n + 2 * i + 1]
        return out[-1][0, 0]

    def finish(self, after):
        copies = self._copies

        def wait_body(src_ref, land_ref, *rest):
            sends, recvs, local = copies(src_ref, land_ref, rest[:N_PEERS], rest[N_PEERS:2 * N_PEERS], rest[2 * N_PEERS])
            for cp in sends:
                cp.wait_send()
            for cp in recvs:
                cp.wait_recv()
            local.wait()

        return pl.pallas_call(
            wait_body,
            name=f"{self.name}_wait",
            in_specs=(_HBM, _HBM) + (_SEM,) * (2 * N_PEERS + 1) + (pl.BlockSpec(memory_space=pl.ANY),),
            out_specs=(_HBM, _HBM),
            out_shape=(pltpu.HBM(self.src_thru.shape, self.dtype), pltpu.HBM(self.land_shape, self.dtype)),
            input_output_aliases={0: 0, 1: 1},
            compiler_params=pltpu.CompilerParams(has_side_effects=_DATAFLOW),
        )(self.src_thru, self.land_thru, *self.sems, after)[1]


class _Exchanges:
    def __init__(self, items):
        self.kinds = [kind for kind, _ in items]
        self.arrays = [src for _, src in items]
        self.out_shapes = []
        self.block_rows = []
        for kind, src in items:
            scatter, by_rows = kind.startswith("scatter"), kind.endswith("rows")
            if by_rows:
                r = src.shape[0] // N_DEV if scatter else src.shape[0]
                shape = (N_DEV, r, src.shape[1]) if scatter else (N_DEV * r, src.shape[1])
            else:
                r = None
                shape = src.shape if scatter else (N_DEV,) + src.shape
            self.block_rows.append(r)
            self.out_shapes.append(jax.ShapeDtypeStruct(shape, src.dtype))
        n = len(items)
        self.in_specs = [pl.BlockSpec(memory_space=pl.ANY)] * n
        self.out_specs = [pl.BlockSpec(memory_space=pl.ANY)] * n
        self.scratch_shapes = [pltpu.SemaphoreType.DMA((n * (N_DEV - 1),)), pltpu.SemaphoreType.DMA((n * (N_DEV - 1),)),
                               pltpu.SemaphoreType.DMA((n,))]

    def _copies(self, i, src_ref, out_ref, send_sems, recv_sems, local_sems):
        kind, r = self.kinds[i], self.block_rows[i]
        scatter, by_rows = kind.startswith("scatter"), kind.endswith("rows")
        me = _position()[3]

        def rows(ref, idx):
            return ref.at[pl.ds(pl.multiple_of(idx * r, r), r), :]

        def outgoing(idx):
            if not scatter:
                return src_ref
            return rows(src_ref, idx) if by_rows else src_ref.at[idx]

        def slot(idx):
            return rows(out_ref, idx) if (by_rows and not scatter) else out_ref.at[idx]

        local = pltpu.make_async_copy(outgoing(me), slot(me), local_sems.at[i])
        sends, recvs = [], []
        for k in range(1, N_DEV):
            peer, pidx = _peer(k)
            sem = i * (N_DEV - 1) + k - 1
            sends.append(pltpu.make_async_remote_copy(src_ref=outgoing(pidx), dst_ref=slot(me), send_sem=send_sems.at[sem],
                                                      recv_sem=recv_sems.at[sem], device_id=peer, device_id_type=MESH))
            recvs.append(pltpu.make_async_remote_copy(src_ref=outgoing(pidx), dst_ref=slot(pidx), send_sem=send_sems.at[sem],
                                                      recv_sem=recv_sems.at[sem], device_id=peer, device_id_type=MESH))
        return local, sends, recvs

    def start(self, src_refs, out_refs, *sems):
        for i, (src_ref, out_ref) in enumerate(zip(src_refs, out_refs)):
            local, sends, _ = self._copies(i, src_ref, out_ref, *sems)
            local.start()
            for cp in sends:
                cp.start()

    def wait(self, src_refs, out_refs, *sems):
        for i, (src_ref, out_ref) in enumerate(zip(src_refs, out_refs)):
            local, sends, recvs = self._copies(i, src_ref, out_ref, *sems)
            for cp in recvs:
                cp.wait_recv()
            for cp in sends:
                cp.wait_send()
            local.wait()


def _row_tile(rows):
    return _tile(rows, (256, 128, 64, 32, 16, 8))


def _layer_parts_specs(n_layers, n_parts, tr, cols):
    return [pl.BlockSpec((n_parts, tr, cols), lambda l, i, j=j: (0, jnp.where(l == j, i, 0), 0)) for j in range(n_layers)]


def _select_layer_sum(p_refs):
    l = pl.program_id(0)
    g = None
    for j, p_ref in enumerate(p_refs):
        gj = p_ref[0].astype(F32)
        for k in range(1, p_ref.shape[0]):
            gj = gj + p_ref[k].astype(F32)
        g = gj if g is None else jnp.where(l == j, gj, g)
    return g


def _sum_parts(parts, *, name):
    n_layers = len(parts)
    n_parts, rows, cols = parts[0].shape
    tr = _row_tile(rows)

    def body(*refs):
        refs[-1][...] = _select_layer_sum(refs[:n_layers])

    return pl.pallas_call(
        body,
        name=name,
        grid=(n_layers, rows // tr),
        in_specs=_layer_parts_specs(n_layers, n_parts, tr, cols),
        out_specs=pl.BlockSpec((None, tr, cols), lambda l, i: (l, i, 0)),
        out_shape=jax.ShapeDtypeStruct((n_layers, rows, cols), F32),
        compiler_params=_cparams(("arbitrary", "arbitrary")),
    )(*parts)


def _adamw(parts, w, m, v, *, name):
    n_layers, rows, cols = w.shape
    summed = not isinstance(parts, (list, tuple))
    tr = _row_tile(rows)
    n_in = 1 if summed else n_layers

    def body(*refs):
        w_ref, m_ref, v_ref, g_ref, d_ref, m2_ref, v2_ref = refs[n_in:]
        g = refs[0][...] if summed else _select_layer_sum(refs[:n_in])
        m2 = ADAM_B1 * m_ref[...] + (1.0 - ADAM_B1) * g
        v2 = ADAM_B2 * v_ref[...] + (1.0 - ADAM_B2) * (g * g)
        m_hat = m2 / (1.0 - ADAM_B1 ** ADAM_STEP)
        v_hat = v2 / (1.0 - ADAM_B2 ** ADAM_STEP)
        g_ref[...] = g
        d_ref[...] = -ADAM_LR * (m_hat / (jnp.sqrt(v_hat) + ADAM_EPS) + ADAM_WD * w_ref[...])
        m2_ref[...] = m2
        v2_ref[...] = v2

    slab = pl.BlockSpec((None, tr, cols), lambda l, i: (l, i, 0))
    out = jax.ShapeDtypeStruct((n_layers, rows, cols), F32)
    p_specs = [slab] if summed else _layer_parts_specs(n_layers, parts[0].shape[0], tr, cols)
    return pl.pallas_call(
        body,
        name=name,
        grid=(n_layers, rows // tr),
        in_specs=p_specs + [slab, slab, slab],
        out_specs=[slab, slab, slab, slab],
        out_shape=[out, out, out, out],
        compiler_params=_cparams(("arbitrary", "arbitrary")),
    )(*((parts,) if summed else tuple(parts)), w, m, v)


SLAB_ROWS = 256
_SMALL_SHARDED = (("conv_w", (2, 31, 32)), ("ffn_conv_w", (2, 3, 704)))
_REPLICATED = (("g_mix", (2, 1024)), ("conv_b", (2, 256)), ("conv_ln_g", (2, 256)), ("conv_ln_b", (2, 256)),
               ("sgu_ln_g", (2, 256)), ("sgu_ln_b", (2, 256)), ("sgu_w", (2, 4, 128, 128)), ("sgu_b", (2, 4, 128)),
               ("g_out", (2, 1024)), ("g_ffn", (2, 1024)), ("ffn_conv_b", (2, 5632)), ("g_final", (1024,)))


def _seg_rows(n_elems):
    return -(-n_elems // LANES)


def _pack(arrays, lead=()):
    segs = []
    for a in arrays:
        flat = a.reshape(lead + (-1,)).astype(F32)
        pad = _seg_rows(flat.shape[-1]) * LANES - flat.shape[-1]
        if pad:
            flat = jnp.pad(flat, [(0, 0)] * len(lead) + [(0, pad)])
        segs.append(flat)
    flat = jnp.concatenate(segs, axis=-1)
    rows = flat.shape[-1] // LANES
    pad_rows = -rows % SLAB_ROWS
    if pad_rows:
        flat = jnp.pad(flat, [(0, 0)] * len(lead) + [(0, pad_rows * LANES)])
    return flat.reshape(lead + (rows + pad_rows, LANES))


def _unpack(slab, shapes, lead=()):
    flat = slab.reshape(lead + (-1,))
    out, off = [], 0
    for shape in shapes:
        n = math.prod(shape)
        out.append(flat[..., off:off + n].reshape(lead + tuple(shape)))
        off += _seg_rows(n) * LANES
    return out


def _split_last(full):
    split = full.shape[:-1] + (N_DEV, full.shape[-1] // N_DEV)
    return jnp.moveaxis(full.reshape(split), -2, 0)


def _join_last(blocks):
    moved = jnp.moveaxis(blocks, 0, -2)
    return moved.reshape(moved.shape[:-2] + (moved.shape[-2] * moved.shape[-1],))


def _gathered(wt, n, l, after):
    if isinstance(wt[n][l], _SplitExchange):
        wt[n][l] = wt[n][l].finish(after)
    return wt[n][l]


def _layer_fwd(l, x, wt, small):
    tag = f"l{l}"
    h = _rmsnorm_fwd(x, small["g_mix"][l][None], name=f"{tag}_norm_mix")
    w_in_t = _gathered(wt, "w_in_t", l, h)
    p_ab = _matmul(h, w_in_t, "nt", name=f"{tag}_proj_ab", n=D_AB)
    qkv = _matmul(h, w_in_t, "nt", name=f"{tag}_proj_qkv", n=D_QKV, b_n0=D_AB, out_dtype=BF16)
    ya = _mixer_a_fwd(p_ab, wt["conv_w"][l], small["conv_b"][l][None], small["conv_ln_g"][l][None],
                      small["conv_ln_b"][l][None], name=f"{tag}_mixer_a")
    bias = jnp.repeat(small["sgu_b"][l].T, HEAD_DIM, axis=1)
    yb = _sgu_fwd(p_ab, small["sgu_ln_g"][l][None], small["sgu_ln_b"][l][None], small["sgu_w"][l], bias,
                  name=f"{tag}_sgu")
    yc, t_tot = _attn_fwd(qkv, name=f"{tag}_attn")
    y = _combine_fwd(ya, yb, yc, small["g_out"][l][None], name=f"{tag}_combine")
    x1 = _matmul(y, _gathered(wt, "w_out", l, y), "nn", name=f"{tag}_out_proj", residual=x)
    h2 = _rmsnorm_fwd(x1, small["g_ffn"][l][None], name=f"{tag}_norm_ffn")
    up0 = _matmul(h2, _gathered(wt, "w_up_t", l, h2), "nt", name=f"{tag}_up")
    act = _ffn_act_fwd(up0, wt["ffn_conv_w"][l], small["ffn_conv_b"][l][None], name=f"{tag}_ffn_act")
    x2 = _matmul(act, _gathered(wt, "w_down", l, act), "nn", name=f"{tag}_down", residual=x1)
    saved = dict(x=x, h=h, p_ab=p_ab, qkv=qkv, ya=ya, yb=yb, yc=yc, t_tot=t_tot, y=y, x1=x1, h2=h2, up0=up0,
                 act=act, bias=bias)
    return x2, saved


def _layer_bwd(l, dres, sv, wt, small, scattering):
    tag = f"l{l}b"
    g = {}

    def scatter(n, partial):
        scattering[n][l] = _SplitExchange(partial, kind="scatter_rows", name=f"scatter_{n}_l{l}")
        return _SplitExchange.start([scattering[n][l]], name=f"scatter_{n}_l{l}_start")

    dx2, dx2_b = dres
    dact = _matmul(dx2_b, wt["w_down"][l], "nt", name=f"{tag}_dact")
    tok = scatter("w_down", _matmul(sv["act"], dx2_b, "tn", name=f"{tag}_dw_down", out_dtype=BF16))
    dup_g, dup_v, dwg, dwv, dbg, dbv = _ffn_act_bwd(sv["up0"], dact, wt["ffn_conv_w"][l], small["ffn_conv_b"][l][None] + tok,
                                                    name=f"{tag}_ffn_act")
    g["ffn_conv_w"] = jnp.concatenate([dwg, dwv], axis=1)
    g["ffn_conv_b"] = jnp.concatenate([dbg[0], dbv[0]])
    dh2 = _matmul(dup_g, wt["w_up_t"][l], "nn", name=f"{tag}_dh2_gate")
    dh2 = _matmul(dup_v, wt["w_up_t"][l], "nn", name=f"{tag}_dh2_val", b_k0=D_FF, residual=dh2)
    dw_up = _matmul(dup_g, sv["h2"], "tn", name=f"{tag}_dw_up_gate", out_dtype=BF16, rows=(2 * D_FF, 0))
    dw_up = _matmul(dup_v, sv["h2"], "tn", name=f"{tag}_dw_up_val", out_dtype=BF16, rows=(2 * D_FF, D_FF), into=dw_up)
    tok = scatter("w_up_t", dw_up)
    dx1, dx1_b, dg = _rmsnorm_bwd(sv["x1"], small["g_ffn"][l][None] + tok, dh2, dx2, name=f"{tag}_norm_ffn")
    g["g_ffn"] = dg[0]
    dy = _matmul(dx1_b, wt["w_out"][l], "nt", name=f"{tag}_dy")
    tok = scatter("w_out", _matmul(sv["y"], dx1_b, "tn", name=f"{tag}_dw_out", out_dtype=BF16))
    dya, dyb, dyc, dg = _combine_bwd(dy, sv["ya"], sv["yb"], sv["yc"], small["g_out"][l][None] + tok,
                                     name=f"{tag}_combine")
    g["g_out"] = dg[0]
    dq, dk, dv = _attn_bwd(sv["qkv"], sv["t_tot"], dyc, name=f"{tag}_attn")
    dp_b, g["sgu_w"], db, dg, dbeta = _sgu_bwd(sv["p_ab"], dyb, small["sgu_ln_g"][l][None], small["sgu_ln_b"][l][None],
                                               small["sgu_w"][l], sv["bias"], name=f"{tag}_sgu")
    g["sgu_b"] = db[:, :N_SGU_HEADS].T
    g["sgu_ln_g"], g["sgu_ln_b"] = dg[0], dbeta[0]
    dp_a, g["conv_w"], dcb, dg, dbeta = _mixer_a_bwd(sv["p_ab"], dya, wt["conv_w"][l], small["conv_b"][l][None],
                                                     small["conv_ln_g"][l][None], small["conv_ln_b"][l][None],
                                                     name=f"{tag}_mixer_a")
    g["conv_b"], g["conv_ln_g"], g["conv_ln_b"] = dcb[0], dg[0], dbeta[0]
    dp = jnp.concatenate([dp_a.astype(BF16), dp_b.astype(BF16), dq.astype(BF16), dk.astype(BF16), dv.astype(BF16)],
                         axis=1)
    dh = _matmul(dp, wt["w_in_t"][l], "nn", name=f"{tag}_dh")
    tok = scatter("w_in_t", _matmul(dp, sv["h"], "tn", name=f"{tag}_dw_in", out_dtype=BF16))
    dx, dx_b, dg = _rmsnorm_bwd(sv["x"], small["g_mix"][l][None] + tok, dh, dx1, name=f"{tag}_norm_mix")
    g["g_mix"] = dg[0]
    return (dx, dx_b), g


_BIG = ("w_in_t", "w_out", "w_up_t", "w_down")


def kernel(x, g_mix, w_in, conv_w, conv_b, conv_ln_g, conv_ln_b, sgu_ln_g, sgu_ln_b, sgu_w, sgu_b, g_out, w_out, g_ffn, w_up, ffn_conv_w, ffn_conv_b, w_down, g_final, loss_target, m_g_mix, m_w_in, m_conv_w, m_conv_b, m_conv_ln_g, m_conv_ln_b, m_sgu_ln_g, m_sgu_ln_b, m_sgu_w, m_sgu_b, m_g_out, m_w_out, m_g_ffn, m_w_up, m_ffn_conv_w, m_ffn_conv_b, m_w_down, m_g_final, v_g_mix, v_w_in, v_conv_w, v_conv_b, v_conv_ln_g, v_conv_ln_b, v_sgu_ln_g, v_sgu_ln_b, v_sgu_w, v_sgu_b, v_g_out, v_w_out, v_g_ffn, v_w_up, v_ffn_conv_w, v_ffn_conv_b, v_w_down, v_g_final):
    given = dict(locals())
    n_layers = g_mix.shape[0]
    layers = range(n_layers)
    small_sharded = [n for n, _ in _SMALL_SHARDED]
    replicated = [n for n, _ in _REPLICATED]
    small = {n: given[n] for n in replicated}

    filters = _exchange(_pack([given[n] for n in small_sharded]), kind="gather_blocks", name="gather_filters")
    filters, first_shard = lax.optimization_barrier((filters, w_in[0].T.astype(BF16)))
    wt = {n: [None] * n_layers for n in _BIG}
    wt["w_in_t"][0] = _SplitExchange(first_shard, kind="gather_rows", name="gather_w_in_t_l0")
    tok = _SplitExchange.start([wt["w_in_t"][0]], name="gather_w_in_t_l0_start")
    w_in, w_out, w_up, w_down, tok = lax.optimization_barrier((w_in, w_out, w_up, w_down, tok))
    shard = {"w_in_t": [w_in[l].T.astype(BF16) for l in layers], "w_out": [w_out[l].astype(BF16) for l in layers],
             "w_up_t": [w_up[l].T.astype(BF16) for l in layers], "w_down": [w_down[l].astype(BF16) for l in layers]}
    later = [(n, l) for l in layers for n in _BIG if (n, l) != ("w_in_t", 0)]
    for n, l in later:
        wt[n][l] = _SplitExchange(shard[n][l], kind="gather_rows", name=f"gather_{n}_l{l}")
    small["g_mix"] = g_mix + tok + _SplitExchange.start([wt[n][l] for n, l in later], name="gather_weights_start")
    for n, blocks in zip(small_sharded, _unpack(filters, [s for _, s in _SMALL_SHARDED], lead=(N_DEV,))):
        wt[n] = _join_last(blocks)

    xs = x[0]
    saved = []
    for l in layers:
        xs, sv = _layer_fwd(l, xs, wt, small)
        saved.append(sv)
    loss_tile, dx, dx_b, dgf = _loss_head(xs, g_final[None], loss_target[0], name="loss_head")
    dres = (dx, dx_b)
    scattering = {n: [None] * n_layers for n in _BIG}
    grads = [None] * n_layers
    for l in reversed(layers):
        dres, grads[l] = _layer_bwd(l, dres, saved[l], wt, small, scattering)
    partial = {n: jnp.stack([g[n] for g in grads]) for n in grads[0]}
    partial["g_final"] = dgf[0]

    own = _pack([_split_last(partial[n]) for n in small_sharded], lead=(N_DEV,))
    shared = _pack([partial[n] for n in replicated])
    slab = jnp.concatenate([own, jnp.broadcast_to(shared[None], (N_DEV,) + shared.shape)], axis=1)
    small_grads = _SplitExchange(slab, kind="scatter_blocks", name="scatter_small_grads")
    tok = _SplitExchange.start([small_grads], name="scatter_small_grads_start")
    dx_out, tok = lax.optimization_barrier((dres[0], tok))
    received = {n: [scattering[n][l].finish(dx_out) for l in layers] for n in _BIG}
    out = {}

    def update(n, parts):
        results = _adamw(parts, given[n], given["m_" + n], given["v_" + n], name=f"adamw_{n}")
        for pre, res in zip(("grad_", "delta_", "new_m_", "new_v_"), results):
            out[pre + n] = res

    update("w_out", received["w_out"])
    update("w_down", received["w_down"])
    for n in ("w_in", "w_up"):
        update(n, jnp.swapaxes(_sum_parts(received[n + "_t"], name=f"sum_{n}"), 1, 2))

    slab = small_grads.finish(out["grad_w_up"])
    stacks = [jnp.concatenate([_pack([given[pre + n] for n in small_sharded]),
                               _pack([given[pre + n] for n in replicated])])[None] for pre in ("", "m_", "v_")]
    results = _adamw([slab], *stacks, name="adamw_small")
    n_own = own.shape[1]
    for pre, res in zip(("grad_", "delta_", "new_m_", "new_v_"), results):
        unpacked = (_unpack(res[0, :n_own], [s for _, s in _SMALL_SHARDED])
                    + _unpack(res[0, n_own:], [s for _, s in _REPLICATED]))
        for n, a in zip(small_sharded + replicated, unpacked):
            out[pre + n] = a

    loss = lax.psum(loss_tile[0, 0], ("x", "y", "c"))
    order = list(_WEIGHT_ORDER)
    return (loss, dres[0][None], *[out["grad_" + n] for n in order], *[out["delta_" + n] for n in order],
            *[out["new_m_" + n] for n in order], *[out["new_v_" + n] for n in order])


_WEIGHT_ORDER = ("g_mix", "w_in", "conv_w", "conv_b", "conv_ln_g", "conv_ln_b", "sgu_ln_g", "sgu_ln_b", "sgu_w", "sgu_b",
                 "g_out", "w_out", "g_ffn", "w_up", "ffn_conv_w", "ffn_conv_b", "w_down", "g_final")
```

```python
import math

import jax
import jax.numpy as jnp
from jax import lax
from jax.experimental import pallas as pl
from jax.experimental.pallas import tpu as pltpu

F32 = jnp.float32
BF16 = jnp.bfloat16

N_DEV = 8
D_MODEL = 1024
HEAD_DIM = 64
D_CONV = 256
D_SGU = 256
D_SB = 512
D_AB = 2 * D_CONV + 2 * D_SGU
D_QKV = 3 * D_SB
D_IN = D_AB + D_QKV
CONV_K = 31
CONV_HALO = 32
FFN_K = 3
FFN_HALO = 8
D_FF = 2816
CHUNK = 128
EPS = 1e-6
LANES = 128

ADAM_LR = 0.001
ADAM_B1 = 0.9
ADAM_B2 = 0.999
ADAM_EPS = 1e-08
ADAM_WD = 0.01
ADAM_STEP = 10

VMEM_LIMIT = 56 * 1024 * 1024


def _cparams(sem=None):
    return pltpu.CompilerParams(dimension_semantics=sem, vmem_limit_bytes=VMEM_LIMIT)


def _tile(n, prefs=(512, 256, 128)):
    for t in prefs:
        if n % t == 0:
            return t
    return n


def _sigmoid(x):
    return 1.0 / (1.0 + jnp.exp(-x))


_INV_SQRT2 = 1.0 / math.sqrt(2.0)
_INV_SQRT2PI = 1.0 / math.sqrt(2.0 * math.pi)


def _gelu(x):
    return 0.5 * x * (1.0 + lax.erf(x * _INV_SQRT2))


def _gelu_grad(x):
    return 0.5 * (1.0 + lax.erf(x * _INV_SQRT2)) + x * jnp.exp(-0.5 * x * x) * _INV_SQRT2PI


def _dot(a, b, dims):
    return lax.dot_general(a, b, (dims, ((), ())), preferred_element_type=F32)


_NN = ((1,), (0,))
_NT = ((1,), (1,))
_TN = ((0,), (0,))


def _split_bf16(x):
    hi = x.astype(BF16)
    lo = (x - hi.astype(F32)).astype(BF16)
    return jnp.concatenate([hi, lo], axis=1)


def _matmul(a, b, mode, *, name, out_dtype=F32, residual=None, n=None, b_n0=0, b_k0=0, rows=None, into=None):
    if mode == "nn":
        (m, k), n = a.shape, (n or b.shape[1])
    elif mode == "nt":
        (m, k), n = a.shape, (n or b.shape[0])
    else:
        (k, m), n = a.shape, b.shape[1]
    has_res = residual is not None
    tm, tn = _matmul_tiles(m, n, k, a.dtype.itemsize, b.dtype.itemsize, jnp.dtype(out_dtype).itemsize, has_res, b_n0)
    j0 = b_n0 // tn
    total_rows, first_row = rows or (m, 0)
    assert b_k0 % k == 0 and first_row % tm == 0
    kb, i0 = b_k0 // k, first_row // tm

    if mode == "nn":
        a_spec = pl.BlockSpec((tm, k), lambda i, j: (i, 0))
        b_spec = pl.BlockSpec((k, tn), lambda i, j: (kb, j + j0))
        dims = _NN
    elif mode == "nt":
        a_spec = pl.BlockSpec((tm, k), lambda i, j: (i, 0))
        b_spec = pl.BlockSpec((tn, k), lambda i, j: (j + j0, 0))
        dims = _NT
    else:
        a_spec = pl.BlockSpec((k, tm), lambda i, j: (0, i))
        b_spec = pl.BlockSpec((k, tn), lambda i, j: (0, j))
        dims = _TN
    o_spec = pl.BlockSpec((tm, tn), lambda i, j: (i + i0, j))
    r_spec = pl.BlockSpec((tm, tn), lambda i, j: (i, j))

    def body(*refs):
        a_ref, b_ref = refs[:2]
        acc = _dot(a_ref[...].astype(BF16), b_ref[...].astype(BF16), dims)
        if has_res:
            acc = acc + refs[2][...]
        refs[-1][...] = acc.astype(out_dtype)

    in_specs = [a_spec, b_spec] + ([r_spec] if has_res else [])
    args = (a, b) + ((residual,) if has_res else ())
    aliases = {}
    if into is not None:
        aliases = {len(args): 0}
        in_specs.append(pl.BlockSpec(memory_space=pl.ANY))
        args += (into,)

        def body(*refs, inner=body):
            inner(*refs[:len(args) - 1], refs[-1])

    return pl.pallas_call(
        body,
        name=name,
        grid=(m // tm, n // tn),
        in_specs=in_specs,
        out_specs=o_spec,
        out_shape=jax.ShapeDtypeStruct((total_rows, n), out_dtype),
        input_output_aliases=aliases,
        compiler_params=_cparams(("parallel", "parallel")),
    )(*args)


MATMUL_VMEM_BUDGET = 40 * 1024 * 1024


def _matmul_tiles(m, n, k, a_bytes, b_bytes, out_bytes, has_res, n_offset):
    def divisors(size, cap, also=0):
        return [t for t in range(cap, 0, -LANES) if size % t == 0 and also % t == 0] or [size]

    for tm in divisors(m, 1024):
        for tn in divisors(n, 1408, n_offset):
            blocks = tm * k * a_bytes + k * tn * b_bytes + tm * tn * (out_bytes + (4 if has_res else 0))
            if 2 * blocks <= MATMUL_VMEM_BUDGET:
                return tm, tn
    raise ValueError(f"no matmul tiling for {m} x {n} x {k}")


ROW_TILE = 256


def _rmsnorm_fwd(x, g, *, name):
    s, d = x.shape

    def body(x_ref, g_ref, h_ref):
        xv = x_ref[...]
        r = lax.rsqrt(jnp.mean(xv * xv, axis=-1, keepdims=True) + EPS)
        h_ref[...] = (xv * r * g_ref[...]).astype(BF16)

    return pl.pallas_call(
        body,
        name=name,
        grid=(s // ROW_TILE,),
        in_specs=[pl.BlockSpec((ROW_TILE, d), lambda i: (i, 0)), pl.BlockSpec((1, d), lambda i: (0, 0))],
        out_specs=pl.BlockSpec((ROW_TILE, d), lambda i: (i, 0)),
        out_shape=jax.ShapeDtypeStruct((s, d), BF16),
        compiler_params=_cparams(("parallel",)),
    )(x, g)


def _rmsnorm_bwd(x, g, dh, dres, *, name):
    s, d = x.shape

    def body(x_ref, g_ref, dh_ref, dres_ref, dx_ref, dxb_ref, dg_ref):
        xv = x_ref[...]
        r = lax.rsqrt(jnp.mean(xv * xv, axis=-1, keepdims=True) + EPS)
        xhat = xv * r
        dhv = dh_ref[...]
        dxhat = dhv * g_ref[...]
        dx = dres_ref[...] + r * (dxhat - xhat * jnp.mean(dxhat * xhat, axis=-1, keepdims=True))
        dx_ref[...] = dx
        dxb_ref[...] = dx.astype(BF16)
        part = jnp.sum(dhv * xhat, axis=0, keepdims=True)

        @pl.when(pl.program_id(0) == 0)
        def _():
            dg_ref[...] = part

        @pl.when(pl.program_id(0) > 0)
        def _():
            dg_ref[...] += part

    row = pl.BlockSpec((ROW_TILE, d), lambda i: (i, 0))
    vec = pl.BlockSpec((1, d), lambda i: (0, 0))
    return pl.pallas_call(
        body,
        name=name,
        grid=(s // ROW_TILE,),
        in_specs=[row, vec, row, row],
        out_specs=[row, row, vec],
        out_shape=[jax.ShapeDtypeStruct((s, d), F32), jax.ShapeDtypeStruct((s, d), BF16),
                   jax.ShapeDtypeStruct((1, d), F32)],
        compiler_params=_cparams(("arbitrary",)),
    )(x, g, dh, dres)


def _loss_head(x, g, target, *, name):
    s, d = x.shape

    def body(x_ref, g_ref, t_ref, loss_ref, dx_ref, dxb_ref, dg_ref):
        xv = x_ref[...]
        gv = g_ref[...]
        r = lax.rsqrt(jnp.mean(xv * xv, axis=-1, keepdims=True) + EPS)
        xhat = xv * r
        diff = xhat * gv - t_ref[...]
        dy = diff * (1.0 / d)
        dxhat = dy * gv
        dx = r * (dxhat - xhat * jnp.mean(dxhat * xhat, axis=-1, keepdims=True))
        dx_ref[...] = dx
        dxb_ref[...] = dx.astype(BF16)
        dg_part = jnp.sum(dy * xhat, axis=0, keepdims=True)
        row_loss = jnp.sum(diff * diff, axis=-1, keepdims=True)
        loss_part = jnp.sum(row_loss, axis=0, keepdims=True) * (0.5 / d)

        @pl.when(pl.program_id(0) == 0)
        def _():
            dg_ref[...] = dg_part
            loss_ref[...] = jnp.broadcast_to(loss_part, loss_ref.shape)

        @pl.when(pl.program_id(0) > 0)
        def _():
            dg_ref[...] += dg_part
            loss_ref[...] += jnp.broadcast_to(loss_part, loss_ref.shape)

    row = pl.BlockSpec((ROW_TILE, d), lambda i: (i, 0))
    vec = pl.BlockSpec((1, d), lambda i: (0, 0))
    tile = pl.BlockSpec((8, LANES), lambda i: (0, 0))
    return pl.pallas_call(
        body,
        name=name,
        grid=(s // ROW_TILE,),
        in_specs=[row, vec, row],
        out_specs=[tile, row, row, vec],
        out_shape=[jax.ShapeDtypeStruct((8, LANES), F32), jax.ShapeDtypeStruct((s, d), F32),
                   jax.ShapeDtypeStruct((s, d), BF16), jax.ShapeDtypeStruct((1, d), F32)],
        compiler_params=_cparams(("arbitrary",)),
    )(x, g, target)


_BRANCHES = ((0, D_CONV), (D_CONV, D_SGU), (D_CONV + D_SGU, D_SB))


def _combine_fwd(ya, yb, yc, g, *, name):
    s = ya.shape[0]

    def body(ya_ref, yb_ref, yc_ref, g_ref, y_ref):
        for ref, (off, w) in zip((ya_ref, yb_ref, yc_ref), _BRANCHES):
            v = ref[...]
            r = lax.rsqrt(jnp.mean(v * v, axis=-1, keepdims=True) + EPS)
            y_ref[:, off:off + w] = (v * r * g_ref[:, off:off + w]).astype(BF16)

    def row(w):
        return pl.BlockSpec((ROW_TILE, w), lambda i: (i, 0))

    return pl.pallas_call(
        body,
        name=name,
        grid=(s // ROW_TILE,),
        in_specs=[row(D_CONV), row(D_SGU), row(D_SB), pl.BlockSpec((1, D_MODEL), lambda i: (0, 0))],
        out_specs=row(D_MODEL),
        out_shape=jax.ShapeDtypeStruct((s, D_MODEL), BF16),
        compiler_params=_cparams(("parallel",)),
    )(ya, yb, yc, g)


def _combine_bwd(dy, ya, yb, yc, g, *, name):
    s = ya.shape[0]

    def body(dy_ref, ya_ref, yb_ref, yc_ref, g_ref, dya_ref, dyb_ref, dyc_ref, dg_ref):
        first = pl.program_id(0) == 0
        for ref, dref, (off, w) in zip((ya_ref, yb_ref, yc_ref), (dya_ref, dyb_ref, dyc_ref), _BRANCHES):
            v = ref[...]
            r = lax.rsqrt(jnp.mean(v * v, axis=-1, keepdims=True) + EPS)
            n = v * r
            dout = dy_ref[:, off:off + w]
            dn = dout * g_ref[:, off:off + w]
            dref[...] = r * (dn - n * jnp.mean(dn * n, axis=-1, keepdims=True))
            part = jnp.sum(dout * n, axis=0, keepdims=True)

            @pl.when(first)
            def _():
                dg_ref[:, off:off + w] = part

            @pl.when(jnp.logical_not(first))
            def _():
                dg_ref[:, off:off + w] += part

    def row(w):
        return pl.BlockSpec((ROW_TILE, w), lambda i: (i, 0))

    vec = pl.BlockSpec((1, D_MODEL), lambda i: (0, 0))
    return pl.pallas_call(
        body,
        name=name,
        grid=(s // ROW_TILE,),
        in_specs=[row(D_MODEL), row(D_CONV), row(D_SGU), row(D_SB), vec],
        out_specs=[row(D_CONV), row(D_SGU), row(D_SB), vec],
        out_shape=[jax.ShapeDtypeStruct((s, D_CONV), F32), jax.ShapeDtypeStruct((s, D_SGU), F32),
                   jax.ShapeDtypeStruct((s, D_SB), F32), jax.ShapeDtypeStruct((1, D_MODEL), F32)],
        compiler_params=_cparams(("arbitrary",)),
    )(dy, ya, yb, yc, g)


CONV_TILE = 128


def _shift_down(window, j, halo):
    return pltpu.roll(window, j, 0)[halo:, :] if j else window[halo:, :]


def _shift_up(window, j, n_out):
    n = window.shape[0]
    return pltpu.roll(window, n - j, 0)[:n_out, :] if j else window[:n_out, :]


def _mixer_a_fwd(p_ab, conv_w, conv_b, ln_g, ln_b, *, name):
    s = p_ab.shape[0]
    nt = s // CONV_TILE

    def body(p_ref, w_ref, b_ref, g_ref, beta_ref, y_ref, h_ref):
        h_ref[0:CONV_HALO, :] = jnp.zeros((CONV_HALO, D_CONV), F32)

        def glu(i, c):
            t0 = pl.multiple_of(i * CONV_TILE, CONV_TILE)
            a = p_ref[pl.ds(t0, CONV_TILE), 0:D_CONV]
            gate = p_ref[pl.ds(t0, CONV_TILE), D_CONV:2 * D_CONV]
            h_ref[pl.ds(t0 + CONV_HALO, CONV_TILE), :] = a * _sigmoid(gate)
            return c

        lax.fori_loop(0, nt, glu, 0)

        def conv(i, c):
            t0 = pl.multiple_of(i * CONV_TILE, CONV_TILE)
            window = h_ref[pl.ds(t0, CONV_TILE + CONV_HALO), :]
            acc = jnp.zeros((CONV_TILE, D_CONV), F32) + b_ref[...]
            for k in range(CONV_K):
                acc = acc + w_ref[k:k + 1, :] * _shift_down(window, CONV_K - 1 - k, CONV_HALO)
            mu = jnp.mean(acc, axis=-1, keepdims=True)
            xc = acc - mu
            rstd = lax.rsqrt(jnp.mean(xc * xc, axis=-1, keepdims=True) + EPS)
            z = xc * rstd * g_ref[...] + beta_ref[...]
            y_ref[pl.ds(t0, CONV_TILE), :] = z * _sigmoid(z)
            return c

        lax.fori_loop(0, nt, conv, 0)

    full = lambda shape: pl.BlockSpec(shape, lambda i: (0, 0))
    return pl.pallas_call(
        body,
        name=name,
        grid=(1,),
        in_specs=[full((s, 2 * D_CONV)), full((CONV_K, D_CONV)), full((1, D_CONV)), full((1, D_CONV)),
                  full((1, D_CONV))],
        out_specs=full((s, D_CONV)),
        out_shape=jax.ShapeDtypeStruct((s, D_CONV), F32),
        scratch_shapes=[pltpu.VMEM((s + CONV_HALO, D_CONV), F32)],
        compiler_params=_cparams(("arbitrary",)),
    )(p_ab, conv_w, conv_b, ln_g, ln_b)


def _mixer_a_bwd(p_ab, dya, conv_w, conv_b, ln_g, ln_b, *, name):
    s = p_ab.shape[0]
    nt = s // CONV_TILE

    def body(p_ref, dy_ref, w_ref, b_ref, g_ref, beta_ref, dp_ref, dw_ref, db_ref, dg_ref, dbeta_ref, h_ref, dc_ref):
        h_ref[0:CONV_HALO, :] = jnp.zeros((CONV_HALO, D_CONV), F32)
        dc_ref[s:s + CONV_HALO, :] = jnp.zeros((CONV_HALO, D_CONV), F32)
        dw_ref[...] = jnp.zeros_like(dw_ref)
        db_ref[...] = jnp.zeros_like(db_ref)
        dg_ref[...] = jnp.zeros_like(dg_ref)
        dbeta_ref[...] = jnp.zeros_like(dbeta_ref)

        def glu(i, c):
            t0 = pl.multiple_of(i * CONV_TILE, CONV_TILE)
            a = p_ref[pl.ds(t0, CONV_TILE), 0:D_CONV]
            gate = p_ref[pl.ds(t0, CONV_TILE), D_CONV:2 * D_CONV]
            h_ref[pl.ds(t0 + CONV_HALO, CONV_TILE), :] = a * _sigmoid(gate)
            return c

        lax.fori_loop(0, nt, glu, 0)

        def conv_bwd(i, c):
            t0 = pl.multiple_of(i * CONV_TILE, CONV_TILE)
            window = h_ref[pl.ds(t0, CONV_TILE + CONV_HALO), :]
            taps = [_shift_down(window, CONV_K - 1 - k, CONV_HALO) for k in range(CONV_K)]
            acc = jnp.zeros((CONV_TILE, D_CONV), F32) + b_ref[...]
            for k in range(CONV_K):
                acc = acc + w_ref[k:k + 1, :] * taps[k]
            mu = jnp.mean(acc, axis=-1, keepdims=True)
            xc = acc - mu
            rstd = lax.rsqrt(jnp.mean(xc * xc, axis=-1, keepdims=True) + EPS)
            xhat = xc * rstd
            z = xhat * g_ref[...] + beta_ref[...]
            sg = _sigmoid(z)
            dz = dy_ref[pl.ds(t0, CONV_TILE), :] * (sg * (1.0 + z * (1.0 - sg)))
            dg_ref[...] += jnp.sum(dz * xhat, axis=0, keepdims=True)
            dbeta_ref[...] += jnp.sum(dz, axis=0, keepdims=True)
            dxhat = dz * g_ref[...]
            dc = rstd * (dxhat - jnp.mean(dxhat, axis=-1, keepdims=True)
                         - xhat * jnp.mean(dxhat * xhat, axis=-1, keepdims=True))
            dc_ref[pl.ds(t0, CONV_TILE), :] = dc
            db_ref[...] += jnp.sum(dc, axis=0, keepdims=True)
            for k in range(CONV_K):
                dw_ref[k:k + 1, :] += jnp.sum(dc * taps[k], axis=0, keepdims=True)
            return c

        lax.fori_loop(0, nt, conv_bwd, 0)

        def glu_bwd(i, c):
            t0 = pl.multiple_of(i * CONV_TILE, CONV_TILE)
            window = dc_ref[pl.ds(t0, CONV_TILE + CONV_HALO), :]
            dh = jnp.zeros((CONV_TILE, D_CONV), F32)
            for j in range(CONV_K):
                dh = dh + w_ref[CONV_K - 1 - j:CONV_K - j, :] * _shift_up(window, j, CONV_TILE)
            a = p_ref[pl.ds(t0, CONV_TILE), 0:D_CONV]
            sg = _sigmoid(p_ref[pl.ds(t0, CONV_TILE), D_CONV:2 * D_CONV])
            dp_ref[pl.ds(t0, CONV_TILE), 0:D_CONV] = (dh * sg).astype(BF16)
            dp_ref[pl.ds(t0, CONV_TILE), D_CONV:2 * D_CONV] = (dh * a * sg * (1.0 - sg)).astype(BF16)
            return c

        lax.fori_loop(0, nt, glu_bwd, 0)

    full = lambda shape: pl.BlockSpec(shape, lambda i: (0, 0))
    vec = jax.ShapeDtypeStruct((1, D_CONV), F32)
    return pl.pallas_call(
        body,
        name=name,
        grid=(1,),
        in_specs=[full((s, 2 * D_CONV)), full((s, D_CONV)), full((CONV_K, D_CONV)), full((1, D_CONV)),
                  full((1, D_CONV)), full((1, D_CONV))],
        out_specs=[full((s, 2 * D_CONV)), full((CONV_K, D_CONV)), full((1, D_CONV)), full((1, D_CONV)),
                   full((1, D_CONV))],
        out_shape=[jax.ShapeDtypeStruct((s, 2 * D_CONV), BF16), jax.ShapeDtypeStruct((CONV_K, D_CONV), F32),
                   vec, vec, vec],
        scratch_shapes=[pltpu.VMEM((s + CONV_HALO, D_CONV), F32), pltpu.VMEM((s + CONV_HALO, D_CONV), F32)],
        compiler_params=_cparams(("arbitrary",)),
    )(p_ab, dya, conv_w, conv_b, ln_g, ln_b)


N_SGU_HEADS = D_SGU // HEAD_DIM


def _head_masks(width):
    lane = lax.broadcasted_iota(jnp.int32, (1, width), 1)
    return [(lane >= h * HEAD_DIM) & (lane < (h + 1) * HEAD_DIM) for h in range(width // HEAD_DIM)]


def _tril_mask():
    r = lax.broadcasted_iota(jnp.int32, (CHUNK, CHUNK), 0)
    c = lax.broadcasted_iota(jnp.int32, (CHUNK, CHUNK), 1)
    return c <= r


def _sgu_norm(bv, g, beta):
    vg = _gelu(bv)
    mu = jnp.mean(vg, axis=-1, keepdims=True)
    xc = vg - mu
    rstd = lax.rsqrt(jnp.mean(xc * xc, axis=-1, keepdims=True) + EPS)
    xhat = xc * rstd
    return xhat, rstd, xhat * g + beta


def _sgu_fwd(p_ab, ln_g, ln_b, w_s, bias, *, name):
    s = p_ab.shape[0]

    def body(p_ref, g_ref, beta_ref, w_ref, bias_ref, y_ref):
        u = _gelu(p_ref[:, 0:D_SGU])
        _, _, vn = _sgu_norm(p_ref[:, D_SGU:2 * D_SGU], g_ref[...], beta_ref[...])
        vb = vn.astype(BF16)
        tril = _tril_mask()
        mixed = bias_ref[...]
        for h, m in enumerate(_head_masks(D_SGU)):
            wh = jnp.where(tril, w_ref[h], 0.0).astype(BF16)
            mixed = mixed + _dot(wh, jnp.where(m, vb, jnp.zeros_like(vb)), _NN)
        y_ref[...] = u * mixed

    return pl.pallas_call(
        body,
        name=name,
        grid=(s // CHUNK,),
        in_specs=[pl.BlockSpec((CHUNK, 2 * D_SGU), lambda i: (i, 1)),
                  pl.BlockSpec((1, D_SGU), lambda i: (0, 0)), pl.BlockSpec((1, D_SGU), lambda i: (0, 0)),
                  pl.BlockSpec((N_SGU_HEADS, CHUNK, CHUNK), lambda i: (0, 0, 0)),
                  pl.BlockSpec((CHUNK, D_SGU), lambda i: (0, 0))],
        out_specs=pl.BlockSpec((CHUNK, D_SGU), lambda i: (i, 0)),
        out_shape=jax.ShapeDtypeStruct((s, D_SGU), F32),
        compiler_params=_cparams(("parallel",)),
    )(p_ab, ln_g, ln_b, w_s, bias)


def _sgu_bwd(p_ab, dyb, ln_g, ln_b, w_s, bias, *, name):
    s = p_ab.shape[0]
    n_chunks = s // CHUNK

    def body(p_ref, dy_ref, g_ref, beta_ref, w_ref, bias_ref, dp_ref, dw_ref, db_ref, dg_ref, dbeta_ref, dbias_ref):
        @pl.when(pl.program_id(0) == 0)
        def _():
            dw_ref[...] = jnp.zeros_like(dw_ref)
            dbias_ref[...] = jnp.zeros_like(dbias_ref)
            dg_ref[...] = jnp.zeros_like(dg_ref)
            dbeta_ref[...] = jnp.zeros_like(dbeta_ref)

        bu = p_ref[:, 0:D_SGU]
        bv = p_ref[:, D_SGU:2 * D_SGU]
        u = _gelu(bu)
        gv = g_ref[...]
        xhat, rstd, vn = _sgu_norm(bv, gv, beta_ref[...])
        vb = vn.astype(BF16)
        tril = _tril_mask()
        masks = _head_masks(D_SGU)
        whs = [jnp.where(tril, w_ref[h], 0.0).astype(BF16) for h in range(N_SGU_HEADS)]
        mixed = bias_ref[...]
        for h, m in enumerate(masks):
            mixed = mixed + _dot(whs[h], jnp.where(m, vb, jnp.zeros_like(vb)), _NN)
        dy = dy_ref[...]
        dp_ref[:, 0:D_SGU] = (dy * mixed * _gelu_grad(bu)).astype(BF16)
        dmixed = dy * u
        dbias_ref[...] += dmixed
        dmb = dmixed.astype(BF16)
        dvn = jnp.zeros((CHUNK, D_SGU), F32)
        for h, m in enumerate(masks):
            dmh = jnp.where(m, dmb, jnp.zeros_like(dmb))
            dvn = dvn + _dot(whs[h], dmh, _TN)
            dw_ref[h] += jnp.where(tril, _dot(dmh, vb, _NT), 0.0)
        dg_ref[...] += jnp.sum(dvn * xhat, axis=0, keepdims=True)
        dbeta_ref[...] += jnp.sum(dvn, axis=0, keepdims=True)
        dxhat = dvn * gv
        dvg = rstd * (dxhat - jnp.mean(dxhat, axis=-1, keepdims=True)
                      - xhat * jnp.mean(dxhat * xhat, axis=-1, keepdims=True))
        dp_ref[:, D_SGU:2 * D_SGU] = (dvg * _gelu_grad(bv)).astype(BF16)

        @pl.when(pl.program_id(0) == n_chunks - 1)
        def _():
            chan = lax.broadcasted_iota(jnp.int32, (D_SGU, LANES), 0)
            head = lax.broadcasted_iota(jnp.int32, (D_SGU, LANES), 1)
            to_head = jnp.where(chan // HEAD_DIM == head, 1.0, 0.0).astype(BF16)
            db_ref[...] = _dot(_split_bf16(dbias_ref[...]), jnp.concatenate([to_head, to_head], axis=0), _NN)

    vec = pl.BlockSpec((1, D_SGU), lambda i: (0, 0))
    wspec = pl.BlockSpec((N_SGU_HEADS, CHUNK, CHUNK), lambda i: (0, 0, 0))
    bspec = pl.BlockSpec((CHUNK, D_SGU), lambda i: (0, 0))
    return pl.pallas_call(
        body,
        name=name,
        grid=(n_chunks,),
        in_specs=[pl.BlockSpec((CHUNK, 2 * D_SGU), lambda i: (i, 1)), pl.BlockSpec((CHUNK, D_SGU), lambda i: (i, 0)),
                  vec, vec, wspec, bspec],
        out_specs=[pl.BlockSpec((CHUNK, 2 * D_SGU), lambda i: (i, 0)), wspec,
                   pl.BlockSpec((CHUNK, LANES), lambda i: (0, 0)), vec, vec],
        out_shape=[jax.ShapeDtypeStruct((s, 2 * D_SGU), BF16),
                   jax.ShapeDtypeStruct((N_SGU_HEADS, CHUNK, CHUNK), F32),
                   jax.ShapeDtypeStruct((CHUNK, LANES), F32),
                   jax.ShapeDtypeStruct((1, D_SGU), F32), jax.ShapeDtypeStruct((1, D_SGU), F32)],
        scratch_shapes=[pltpu.VMEM((CHUNK, D_SGU), F32)],
        compiler_params=_cparams(("arbitrary",)),
    )(p_ab, dyb, ln_g, ln_b, w_s, bias)


N_PAIRS = D_SB // LANES
SB_SCALE = HEAD_DIM ** -0.5


def _sb_logits(z, valid):
    nz = -z
    t = jnp.log(1.0 + jnp.exp(jnp.minimum(z, nz)))
    l1 = jnp.minimum(nz, 0.0) - t
    if valid is not None:
        l1 = jnp.where(valid, l1, 0.0)
    return l1, jnp.minimum(z, 0.0) - t


def _split_hi_lo(x):
    hi = lax.bitcast_convert_type(lax.bitcast_convert_type(x, jnp.uint32) & jnp.uint32(0xFFFF0000), F32)
    return jnp.concatenate([hi, x - hi], axis=1)


def _cumsum_operand(keep):
    half = jnp.concatenate([keep.astype(F32), jnp.ones((CHUNK, CHUNK), F32)], axis=1)
    return jnp.concatenate([half, half], axis=0)


Q_BLOCKS_PER_STEP = 4


def _q_blocks_per_step(nq):
    return next(n for n in (Q_BLOCKS_PER_STEP, 2, 1) if nq % n == 0)


def _attn_fwd(qkv, *, name):
    s = qkv.shape[0]
    nq = s // CHUNK
    per_step = _q_blocks_per_step(nq)

    def body(q_ref, k_ref, v_ref, o_ref, t_ref):
        masks = _head_masks(LANES)
        row = lax.broadcasted_iota(jnp.int32, (CHUNK, CHUNK), 0)
        col = lax.broadcasted_iota(jnp.int32, (CHUNK, CHUNK), 1)
        after_op = _cumsum_operand(row > col)
        cmr = col - row
        zc = jnp.zeros((CHUNK, LANES), F32)

        def q_block(sub, _):
            qi = pl.program_id(1) * per_step + sub
            q_rows = pl.ds(pl.multiple_of(sub * CHUNK, CHUNK), CHUNK)
            q = q_ref[q_rows, :] * SB_SCALE
            zero = jnp.zeros_like(q)
            qs = [jnp.where(m, q, zero) for m in masks]

            def blocks(js, carry):
                o, c0, c1 = carry
                kvs, valids = [], []
                for j in js:
                    k0 = pl.multiple_of(jnp.maximum(j, 0) * CHUNK, CHUNK)
                    kvs.append((k_ref[pl.ds(k0, CHUNK), :], v_ref[pl.ds(k0, CHUNK), :]))
                    valids.append(cmr < jnp.where(j >= 0, (qi - j) * CHUNK, -CHUNK))
                units = [(h, b) for b in range(len(js)) for h in range(2)]
                zs = [_dot(qs[h], kvs[b][0], _NT) for h, b in units]
                logits = [_sb_logits(z, valids[b]) for z, (h, b) in zip(zs, units)]
                sums = [_dot(_split_hi_lo(l1), after_op, _NN) for l1, _ in logits]
                cs = [c0, c1]
                probs = []
                for (h, b), (_, lb), sm in zip(units, logits, sums):
                    probs.append(jnp.where(valids[b], jnp.exp(lb + sm[:, :CHUNK] + cs[h]), 0.0))
                    cs[h] = cs[h] + sm[:, CHUNK:]
                for (h, b), a in zip(units, probs):
                    o = o + _dot(a.astype(BF16), jnp.where(masks[h], kvs[b][1], zero), _NN)
                return o, cs[0], cs[1]

            n_four = (qi + 1) // 4
            carry = lax.fori_loop(0, n_four, lambda jj, c: blocks([qi - 4 * jj - i for i in range(4)], c), (zc,) * 3)
            top = qi - 4 * n_four
            o, c0, c1 = lax.fori_loop(0, (top + 2) // 2, lambda jj, c: blocks([top - 2 * jj, top - 2 * jj - 1], c), carry)
            o_ref[q_rows, :] = o
            t_ref[q_rows, 0:LANES] = c0
            t_ref[q_rows, LANES:2 * LANES] = c1
            return 0

        lax.fori_loop(0, per_step, q_block, 0)

    rows = per_step * CHUNK
    return pl.pallas_call(
        body,
        name=name,
        grid=(N_PAIRS, nq // per_step),
        in_specs=[pl.BlockSpec((rows, LANES), lambda p, i: (i, p)),
                  pl.BlockSpec((s, LANES), lambda p, i: (0, N_PAIRS + p)),
                  pl.BlockSpec((s, LANES), lambda p, i: (0, 2 * N_PAIRS + p))],
        out_specs=[pl.BlockSpec((rows, LANES), lambda p, i: (i, p)),
                   pl.BlockSpec((rows, 2 * LANES), lambda p, i: (i, p))],
        out_shape=[jax.ShapeDtypeStruct((s, D_SB), F32), jax.ShapeDtypeStruct((s, 2 * D_SB), F32)],
        compiler_params=_cparams(("parallel", "parallel")),
    )(qkv, qkv, qkv)


def _attn_bwd(qkv, t_tot, do, *, name):
    s = qkv.shape[0]
    nq = s // CHUNK
    per_step = _q_blocks_per_step(nq)

    def body(q_ref, k_ref, v_ref, t_ref, do_ref, dq_ref, dk_ref, dv_ref):
        @pl.when(pl.program_id(1) == 0)
        def _():
            dk_ref[...] = jnp.zeros_like(dk_ref)
            dv_ref[...] = jnp.zeros_like(dv_ref)

        masks = _head_masks(LANES)
        row = lax.broadcasted_iota(jnp.int32, (CHUNK, CHUNK), 0)
        col = lax.broadcasted_iota(jnp.int32, (CHUNK, CHUNK), 1)
        upto_op = _cumsum_operand(row <= col)
        before_op = _cumsum_operand(row < col)
        cmr = col - row
        zc = jnp.zeros((CHUNK, LANES), F32)

        def q_block(sub, _):
            qi = pl.program_id(1) * per_step + sub
            q_rows = pl.ds(pl.multiple_of(sub * CHUNK, CHUNK), CHUNK)
            q = q_ref[q_rows, :] * SB_SCALE
            dob = do_ref[q_rows, :].astype(BF16)
            zero = jnp.zeros_like(q)
            qs = [jnp.where(m, q, zero) for m in masks]
            dos = [jnp.where(m, dob, zero) for m in masks]
            tots = [t_ref[q_rows, 0:LANES], t_ref[q_rows, LANES:2 * LANES]]

            def blocks(js, carry):
                dq, cl0, cl1, cp0, cp1 = carry
                starts = [pl.multiple_of(jnp.minimum(j, nq - 1) * CHUNK, CHUNK) for j in js]
                valids = [cmr < (qi - j) * CHUNK for j in js]
                kvs = [(k_ref[pl.ds(k0, CHUNK), :], v_ref[pl.ds(k0, CHUNK), :]) for k0 in starts]
                units = [(h, b) for b in range(len(js)) for h in range(2)]
                zs = [_dot(qs[h], kvs[b][0], _NT) for h, b in units]
                das = [_dot(dos[h], kvs[b][1], _NT) for h, b in units]
                logits = [_sb_logits(z, valids[b]) for z, (h, b) in zip(zs, units)]
                sums = [_dot(_split_hi_lo(l1), upto_op, _NN) for l1, _ in logits]
                cls, cps = [cl0, cl1], [cp0, cp1]
                probs, gs = [], []
                for (h, b), (_, lb), sm, da in zip(units, logits, sums, das):
                    a = jnp.where(valids[b], jnp.exp(lb + (tots[h] - cls[h] - sm[:, :CHUNK])), 0.0)
                    probs.append(a)
                    gs.append(a * da)
                    cls[h] = cls[h] + sm[:, CHUNK:]
                sums_g = [_dot(_split_hi_lo(g), before_op, _NN) for g in gs]
                dzs = []
                for (h, b), (_, lb), g, sg in zip(units, logits, gs, sums_g):
                    dz = g - (g + sg[:, :CHUNK] + cps[h]) * jnp.exp(lb)
                    dzs.append(jnp.where(valids[b], dz, 0.0).astype(BF16))
                    cps[h] = cps[h] + sg[:, CHUNK:]
                for (h, b), dzb in zip(units, dzs):
                    dq = dq + _dot(dzb, jnp.where(masks[h], kvs[b][0], zero), _NN)
                for b, k0 in enumerate(starts):
                    dk_ref[pl.ds(k0, CHUNK), :] += _dot(dzs[2 * b], qs[0], _TN) + _dot(dzs[2 * b + 1], qs[1], _TN)
                    dv_ref[pl.ds(k0, CHUNK), :] += (_dot(probs[2 * b].astype(BF16), dos[0], _TN)
                                                    + _dot(probs[2 * b + 1].astype(BF16), dos[1], _TN))
                return dq, cls[0], cls[1], cps[0], cps[1]

            n_four = (qi + 1) // 4
            carry = lax.fori_loop(0, n_four, lambda jj, c: blocks([4 * jj + i for i in range(4)], c), (zc,) * 5)
            base = 4 * n_four
            carry = lax.fori_loop(0, (qi - base + 2) // 2, lambda jj, c: blocks([base + 2 * jj, base + 2 * jj + 1], c), carry)
            dq_ref[q_rows, :] = (carry[0] * SB_SCALE).astype(BF16)
            return 0

        lax.fori_loop(0, per_step, q_block, 0)

    rows = per_step * CHUNK
    blk = pl.BlockSpec((rows, LANES), lambda p, i: (i, p))
    col_blk = pl.BlockSpec((s, LANES), lambda p, i: (0, p))
    out = jax.ShapeDtypeStruct((s, D_SB), F32)
    return pl.pallas_call(
        body,
        name=name,
        grid=(N_PAIRS, nq // per_step),
        in_specs=[blk,
                  pl.BlockSpec((s, LANES), lambda p, i: (0, N_PAIRS + p)),
                  pl.BlockSpec((s, LANES), lambda p, i: (0, 2 * N_PAIRS + p)),
                  pl.BlockSpec((rows, 2 * LANES), lambda p, i: (i, p)),
                  blk],
        out_specs=[blk, col_blk, col_blk],
        out_shape=[jax.ShapeDtypeStruct((s, D_SB), BF16), out, out],
        compiler_params=_cparams(("parallel", "arbitrary")),
    )(qkv, qkv, qkv, t_tot, do)


FFN_TILE = 256
FFN_COLS = 256
N_FF_BLOCKS = D_FF // FFN_COLS


def _ffn_act_fwd(up0, conv_w, conv_b, *, name):
    s = up0.shape[0]
    nt = s // FFN_TILE

    def body(xg_ref, xv_ref, wg_ref, wv_ref, bg_ref, bv_ref, act_ref, pg_ref, pv_ref):
        pg_ref[0:FFN_HALO, :] = jnp.zeros((FFN_HALO, FFN_COLS), F32)
        pv_ref[0:FFN_HALO, :] = jnp.zeros((FFN_HALO, FFN_COLS), F32)
        pg_ref[FFN_HALO:, :] = xg_ref[...].astype(F32)
        pv_ref[FFN_HALO:, :] = xv_ref[...].astype(F32)

        def tile(i, c):
            t0 = pl.multiple_of(i * FFN_TILE, FFN_TILE)
            outs = []
            for p_ref, w_ref, b_ref in ((pg_ref, wg_ref, bg_ref), (pv_ref, wv_ref, bv_ref)):
                window = p_ref[pl.ds(t0, FFN_TILE + FFN_HALO), :]
                acc = b_ref[...] + w_ref[2:3, :] * window[FFN_HALO:, :]
                for j in range(1, FFN_K):
                    acc = acc + w_ref[FFN_K - 1 - j:FFN_K - j, :] * _shift_down(window, j, FFN_HALO)
                outs.append(acc)
            gate, val = outs
            act_ref[pl.ds(t0, FFN_TILE), :] = (gate * _sigmoid(gate) * val).astype(BF16)
            return c

        lax.fori_loop(0, nt, tile, 0)

    gcol = lambda rows: pl.BlockSpec((rows, FFN_COLS), lambda j: (0, j))
    vcol = lambda rows: pl.BlockSpec((rows, FFN_COLS), lambda j: (0, j + N_FF_BLOCKS))
    return pl.pallas_call(
        body,
        name=name,
        grid=(N_FF_BLOCKS,),
        in_specs=[gcol(s), vcol(s), gcol(FFN_K), vcol(FFN_K), gcol(1), vcol(1)],
        out_specs=gcol(s),
        out_shape=jax.ShapeDtypeStruct((s, D_FF), BF16),
        scratch_shapes=[pltpu.VMEM((s + FFN_HALO, FFN_COLS), F32), pltpu.VMEM((s + FFN_HALO, FFN_COLS), F32)],
        compiler_params=_cparams(("parallel",)),
    )(up0, up0, conv_w, conv_w, conv_b, conv_b)


def _ffn_act_bwd(up0, dact, conv_w, conv_b, *, name):
    s = up0.shape[0]
    nt = s // FFN_TILE

    def body(xg_ref, xv_ref, da_ref, wg_ref, wv_ref, bg_ref, bv_ref, dxg_ref, dxv_ref, dwg_ref, dwv_ref, dbg_ref, dbv_ref,
             pg_ref, pv_ref, dg_ref, dv_ref):
        zeros = jnp.zeros((FFN_HALO, FFN_COLS), F32)
        for p_ref, x_ref in ((pg_ref, xg_ref), (pv_ref, xv_ref)):
            p_ref[0:FFN_HALO, :] = zeros
            p_ref[FFN_HALO:, :] = x_ref[...].astype(F32)
        dg_ref[s:s + FFN_HALO, :] = zeros
        dv_ref[s:s + FFN_HALO, :] = zeros
        for ref in (dwg_ref, dwv_ref, dbg_ref, dbv_ref):
            ref[...] = jnp.zeros_like(ref)

        def conv(p_ref, w_ref, b_ref, t0):
            window = p_ref[pl.ds(t0, FFN_TILE + FFN_HALO), :]
            taps = [_shift_down(window, j, FFN_HALO) for j in range(FFN_K)]
            out = b_ref[...]
            for j in range(FFN_K):
                out = out + w_ref[FFN_K - 1 - j:FFN_K - j, :] * taps[j]
            return out, taps

        def tile(i, c):
            t0 = pl.multiple_of(i * FFN_TILE, FFN_TILE)
            gate, taps_g = conv(pg_ref, wg_ref, bg_ref, t0)
            val, taps_v = conv(pv_ref, wv_ref, bv_ref, t0)
            da = da_ref[pl.ds(t0, FFN_TILE), :].astype(F32)
            sg = lax.logistic(gate)
            dgate = da * val * (sg * (1.0 + gate * (1.0 - sg)))
            dval = da * gate * sg
            dg_ref[pl.ds(t0, FFN_TILE), :] = dgate
            dv_ref[pl.ds(t0, FFN_TILE), :] = dval
            dbg_ref[...] += jnp.sum(dgate, axis=0, keepdims=True)
            dbv_ref[...] += jnp.sum(dval, axis=0, keepdims=True)
            for j in range(FFN_K):
                dwg_ref[FFN_K - 1 - j:FFN_K - j, :] += jnp.sum(dgate * taps_g[j], axis=0, keepdims=True)
                dwv_ref[FFN_K - 1 - j:FFN_K - j, :] += jnp.sum(dval * taps_v[j], axis=0, keepdims=True)
            return c

        lax.fori_loop(0, nt, tile, 0)

        def tile_dx(i, c):
            t0 = pl.multiple_of(i * FFN_TILE, FFN_TILE)
            for d_ref, w_ref, dx_ref in ((dg_ref, wg_ref, dxg_ref), (dv_ref, wv_ref, dxv_ref)):
                window = d_ref[pl.ds(t0, FFN_TILE + FFN_HALO), :]
                dx = w_ref[FFN_K - 1:FFN_K, :] * window[:FFN_TILE, :]
                for j in range(1, FFN_K):
                    dx = dx + w_ref[FFN_K - 1 - j:FFN_K - j, :] * _shift_up(window, j, FFN_TILE)
                dx_ref[pl.ds(t0, FFN_TILE), :] = dx.astype(BF16)
            return c

        lax.fori_loop(0, nt, tile_dx, 0)

    gcol = lambda rows: pl.BlockSpec((rows, FFN_COLS), lambda j: (0, j))
    vcol = lambda rows: pl.BlockSpec((rows, FFN_COLS), lambda j: (0, j + N_FF_BLOCKS))
    half = lambda rows, dtype: jax.ShapeDtypeStruct((rows, D_FF), dtype)
    padded = pltpu.VMEM((s + FFN_HALO, FFN_COLS), F32)
    return pl.pallas_call(
        body,
        name=name,
        grid=(N_FF_BLOCKS,),
        in_specs=[gcol(s), vcol(s), gcol(s), gcol(FFN_K), vcol(FFN_K), gcol(1), vcol(1)],
        out_specs=[gcol(s), gcol(s), gcol(FFN_K), gcol(FFN_K), gcol(1), gcol(1)],
        out_shape=[half(s, BF16), half(s, BF16), half(FFN_K, F32), half(FFN_K, F32), half(1, F32), half(1, F32)],
        scratch_shapes=[padded, padded, padded, padded],
        compiler_params=_cparams(("parallel",)),
    )(up0, up0, dact, conv_w, conv_w, conv_b, conv_b)


MESH = pl.DeviceIdType.MESH


def _position():
    x, y, c = lax.axis_index("x"), lax.axis_index("y"), lax.axis_index("c")
    return x, y, c, 4 * x + 2 * y + c


def _peer(k):
    x, y, c, _ = _position()
    px = 1 - x if k & 4 else x
    py = 1 - y if k & 2 else y
    pc = 1 - c if k & 1 else c
    return (px, py, pc), 4 * px + 2 * py + pc


def _gather_blocks(src, *, name):
    def body(src_ref, out_ref, send_sems, recv_sems, local_sem):
        me = _position()[3]
        local = pltpu.make_async_copy(src_ref, out_ref.at[me], local_sem)
        local.start()
        sends, recvs = [], []
        for k in range(1, N_DEV):
            peer, pidx = _peer(k)
            sems = dict(send_sem=send_sems.at[k - 1], recv_sem=recv_sems.at[k - 1], device_id=peer, device_id_type=MESH)
            sends.append(pltpu.make_async_remote_copy(src_ref=src_ref, dst_ref=out_ref.at[me], **sems))
            recvs.append(pltpu.make_async_remote_copy(src_ref=src_ref, dst_ref=out_ref.at[pidx], **sems))
        for cp in sends:
            cp.start()
        for cp in recvs:
            cp.wait_recv()
        for cp in sends:
            cp.wait_send()
        local.wait()

    return pl.pallas_call(
        body,
        name=name,
        in_specs=[pl.BlockSpec(memory_space=pl.ANY)],
        out_specs=pl.BlockSpec(memory_space=pl.ANY),
        out_shape=jax.ShapeDtypeStruct((N_DEV,) + src.shape, src.dtype),
        scratch_shapes=[pltpu.SemaphoreType.DMA((N_DEV - 1,)), pltpu.SemaphoreType.DMA((N_DEV - 1,)),
                        pltpu.SemaphoreType.DMA],
    )(src)


_HBM = pl.BlockSpec(memory_space=pltpu.HBM)
_SEM = pl.BlockSpec(memory_space=pltpu.SEMAPHORE)
_DATAFLOW = pltpu.SideEffectType.DATAFLOW_SIDE_EFFECTING
N_PEERS = N_DEV - 1


class _SplitExchange:
    def __init__(self, src, *, kind, name):
        self.kind, self.name, self.dtype = kind, name, src.dtype
        scatter = kind.startswith("scatter")
        by_blocks = kind == "scatter_blocks"
        self.scatter, self.by_blocks = scatter, by_blocks
        if by_blocks:
            self.r, self.cols, self.land_shape = None, None, src.shape
        else:
            self.r = src.shape[0] // N_DEV if scatter else src.shape[0]
            self.cols = src.shape[1]
            self.land_shape = (N_DEV, self.r, self.cols) if scatter else (N_DEV * self.r, self.cols)
        r = self.r

        def copies(src_ref, land_ref, send_sems, recv_sems, local_sem):
            me = _position()[3]

            def rows(ref, idx):
                return ref.at[pl.ds(pl.multiple_of(idx * r, r), r), :]

            if by_blocks:
                outgoing = lambda idx: src_ref.at[idx]
            else:
                outgoing = (lambda idx: rows(src_ref, idx)) if scatter else (lambda idx: src_ref)
            slot = (lambda idx: land_ref.at[idx]) if scatter else (lambda idx: rows(land_ref, idx))
            sends, recvs = [], []
            for k in range(1, N_DEV):
                peer, pidx = _peer(k)
                sems = dict(send_sem=send_sems[k - 1], recv_sem=recv_sems[k - 1], device_id=peer, device_id_type=MESH)
                sends.append(pltpu.make_async_remote_copy(src_ref=outgoing(pidx), dst_ref=slot(me), **sems))
                recvs.append(pltpu.make_async_remote_copy(src_ref=outgoing(pidx), dst_ref=slot(pidx), **sems))
            return sends, recvs, pltpu.make_async_copy(outgoing(me), slot(me), local_sem)

        self._copies = copies
        self.src = src

    @staticmethod
    def start(exchanges, name):
        n = len(exchanges)
        per = 2 * N_PEERS + 1

        def start_body(*refs):
            outs = refs[2 * n:]
            for i, ex in enumerate(exchanges):
                sems = outs[per * i:per * (i + 1)]
                sends, _, local = ex._copies(refs[2 * i], refs[2 * i + 1], sems[:N_PEERS], sems[N_PEERS:-1], sems[-1])
                for cp in sends + [local]:
                    cp.start()
            outs[-1][...] = jnp.zeros_like(outs[-1])

        sem = pltpu.SemaphoreType.DMA(())
        operands, thru_shapes = [], []
        for ex in exchanges:
            operands += [pltpu.with_memory_space_constraint(ex.src, pltpu.HBM),
                         pltpu.with_memory_space_constraint(lax.empty(ex.land_shape, ex.dtype), pltpu.HBM)]
            thru_shapes += [pltpu.HBM(ex.src.shape, ex.dtype), pltpu.HBM(ex.land_shape, ex.dtype)]
        out = pl.pallas_call(
            start_body,
            name=name,
            in_specs=(_HBM,) * (2 * n),
            out_specs=(_SEM,) * (per * n) + (_HBM,) * (2 * n) + (pl.BlockSpec(memory_space=pltpu.VMEM),),
            out_shape=(sem,) * (per * n) + tuple(thru_shapes) + (jax.ShapeDtypeStruct((8, LANES), F32),),
            input_output_aliases={i: per * n + i for i in range(2 * n)},
            compiler_params=pltpu.CompilerParams(has_side_effects=_DATAFLOW),
        )(*operands)
        for i, ex in enumerate(exchanges):
            ex.sems = out[per * i:per * (i + 1)]
            ex.src_thru, ex.land_thru = out[per * n + 2 * i], out[per * n + 2 * i + 1]
        return out[-1][0, 0]

    def finish(self, after):
        copies = self._copies

        def wait_body(src_ref, land_ref, *rest):
            sends, recvs, local = copies(src_ref, land_ref, rest[:N_PEERS], rest[N_PEERS:2 * N_PEERS], rest[2 * N_PEERS])
            for cp in sends:
                cp.wait_send()
            for cp in recvs:
                cp.wait_recv()
            local.wait()

        return pl.pallas_call(
            wait_body,
            name=f"{self.name}_wait",
            in_specs=(_HBM, _HBM) + (_SEM,) * (2 * N_PEERS + 1) + (pl.BlockSpec(memory_space=pl.ANY),),
            out_specs=(_HBM, _HBM),
            out_shape=(pltpu.HBM(self.src_thru.shape, self.dtype), pltpu.HBM(self.land_shape, self.dtype)),
            input_output_aliases={0: 0, 1: 1},
            compiler_params=pltpu.CompilerParams(has_side_effects=_DATAFLOW),
        )(self.src_thru, self.land_thru, *self.sems, after)[1]


def _row_tile(rows):
    return _tile(rows, (256, 128, 64, 32, 16, 8))


def _layer_parts_specs(n_layers, n_parts, tr, cols):
    return [pl.BlockSpec((n_parts, tr, cols), lambda l, i, j=j: (0, jnp.where(l == j, i, 0), 0)) for j in range(n_layers)]


def _select_layer_sum(p_refs):
    l = pl.program_id(0)
    g = None
    for j, p_ref in enumerate(p_refs):
        gj = p_ref[0].astype(F32)
        for k in range(1, p_ref.shape[0]):
            gj = gj + p_ref[k].astype(F32)
        g = gj if g is None else jnp.where(l == j, gj, g)
    return g


def _sum_parts(parts, *, name):
    n_layers = len(parts)
    n_parts, rows, cols = parts[0].shape
    tr = _row_tile(rows)

    def body(*refs):
        refs[-1][...] = _select_layer_sum(refs[:n_layers])

    return pl.pallas_call(
        body,
        name=name,
        grid=(n_layers, rows // tr),
        in_specs=_layer_parts_specs(n_layers, n_parts, tr, cols),
        out_specs=pl.BlockSpec((None, tr, cols), lambda l, i: (l, i, 0)),
        out_shape=jax.ShapeDtypeStruct((n_layers, rows, cols), F32),
        compiler_params=_cparams(("arbitrary", "arbitrary")),
    )(*parts)


def _adamw(parts, w, m, v, *, name):
    n_layers, rows, cols = w.shape
    summed = not isinstance(parts, (list, tuple))
    tr = _row_tile(rows)
    n_in = 1 if summed else n_layers

    def body(*refs):
        w_ref, m_ref, v_ref, g_ref, d_ref, m2_ref, v2_ref = refs[n_in:]
        g = refs[0][...] if summed else _select_layer_sum(refs[:n_in])
        m2 = ADAM_B1 * m_ref[...] + (1.0 - ADAM_B1) * g
        v2 = ADAM_B2 * v_ref[...] + (1.0 - ADAM_B2) * (g * g)
        m_hat = m2 / (1.0 - ADAM_B1 ** ADAM_STEP)
        v_hat = v2 / (1.0 - ADAM_B2 ** ADAM_STEP)
        g_ref[...] = g
        d_ref[...] = -ADAM_LR * (m_hat / (jnp.sqrt(v_hat) + ADAM_EPS) + ADAM_WD * w_ref[...])
        m2_ref[...] = m2
        v2_ref[...] = v2

    slab = pl.BlockSpec((None, tr, cols), lambda l, i: (l, i, 0))
    out = jax.ShapeDtypeStruct((n_layers, rows, cols), F32)
    p_specs = [slab] if summed else _layer_parts_specs(n_layers, parts[0].shape[0], tr, cols)
    return pl.pallas_call(
        body,
        name=name,
        grid=(n_layers, rows // tr),
        in_specs=p_specs + [slab, slab, slab],
        out_specs=[slab, slab, slab, slab],
        out_shape=[out, out, out, out],
        compiler_params=_cparams(("arbitrary", "arbitrary")),
    )(*((parts,) if summed else tuple(parts)), w, m, v)


SLAB_ROWS = 256
_SMALL_SHARDED = (("conv_w", (2, 31, 32)), ("ffn_conv_w", (2, 3, 704)))
_REPLICATED = (("g_mix", (2, 1024)), ("conv_b", (2, 256)), ("conv_ln_g", (2, 256)), ("conv_ln_b", (2, 256)),
               ("sgu_ln_g", (2, 256)), ("sgu_ln_b", (2, 256)), ("sgu_w", (2, 4, 128, 128)), ("sgu_b", (2, 4, 128)),
               ("g_out", (2, 1024)), ("g_ffn", (2, 1024)), ("ffn_conv_b", (2, 5632)), ("g_final", (1024,)))


def _seg_rows(n_elems):
    return -(-n_elems // LANES)


def _pack(arrays, lead=()):
    segs = []
    for a in arrays:
        flat = a.reshape(lead + (-1,)).astype(F32)
        pad = _seg_rows(flat.shape[-1]) * LANES - flat.shape[-1]
        if pad:
            flat = jnp.pad(flat, [(0, 0)] * len(lead) + [(0, pad)])
        segs.append(flat)
    flat = jnp.concatenate(segs, axis=-1)
    rows = flat.shape[-1] // LANES
    pad_rows = -rows % SLAB_ROWS
    if pad_rows:
        flat = jnp.pad(flat, [(0, 0)] * len(lead) + [(0, pad_rows * LANES)])
    return flat.reshape(lead + (rows + pad_rows, LANES))


def _unpack(slab, shapes, lead=()):
    flat = slab.reshape(lead + (-1,))
    out, off = [], 0
    for shape in shapes:
        n = math.prod(shape)
        out.append(flat[..., off:off + n].reshape(lead + tuple(shape)))
        off += _seg_rows(n) * LANES
    return out


def _split_last(full):
    split = full.shape[:-1] + (N_DEV, full.shape[-1] // N_DEV)
    return jnp.moveaxis(full.reshape(split), -2, 0)


def _join_last(blocks):
    moved = jnp.moveaxis(blocks, 0, -2)
    return moved.reshape(moved.shape[:-2] + (moved.shape[-2] * moved.shape[-1],))


def _gathered(wt, n, l, after):
    if isinstance(wt[n][l], _SplitExchange):
        wt[n][l] = wt[n][l].finish(after)
    return wt[n][l]


def _layer_fwd(l, x, wt, small):
    tag = f"l{l}"
    h = _rmsnorm_fwd(x, small["g_mix"][l][None], name=f"{tag}_norm_mix")
    w_in_t = _gathered(wt, "w_in_t", l, h)
    p_ab = _matmul(h, w_in_t, "nt", name=f"{tag}_proj_ab", n=D_AB)
    qkv = _matmul(h, w_in_t, "nt", name=f"{tag}_proj_qkv", n=D_QKV, b_n0=D_AB, out_dtype=BF16)
    ya = _mixer_a_fwd(p_ab, wt["conv_w"][l], small["conv_b"][l][None], small["conv_ln_g"][l][None],
                      small["conv_ln_b"][l][None], name=f"{tag}_mixer_a")
    bias = jnp.repeat(small["sgu_b"][l].T, HEAD_DIM, axis=1)
    yb = _sgu_fwd(p_ab, small["sgu_ln_g"][l][None], small["sgu_ln_b"][l][None], small["sgu_w"][l], bias,
                  name=f"{tag}_sgu")
    yc, t_tot = _attn_fwd(qkv, name=f"{tag}_attn")
    y = _combine_fwd(ya, yb, yc, small["g_out"][l][None], name=f"{tag}_combine")
    x1 = _matmul(y, _gathered(wt, "w_out", l, y), "nn", name=f"{tag}_out_proj", residual=x)
    h2 = _rmsnorm_fwd(x1, small["g_ffn"][l][None], name=f"{tag}_norm_ffn")
    up0 = _matmul(h2, _gathered(wt, "w_up_t", l, h2), "nt", name=f"{tag}_up", out_dtype=BF16)
    act = _ffn_act_fwd(up0, wt["ffn_conv_w"][l], small["ffn_conv_b"][l][None], name=f"{tag}_ffn_act")
    x2 = _matmul(act, _gathered(wt, "w_down", l, act), "nn", name=f"{tag}_down", residual=x1)
    saved = dict(x=x, h=h, p_ab=p_ab, qkv=qkv, ya=ya, yb=yb, yc=yc, t_tot=t_tot, y=y, x1=x1, h2=h2, up0=up0,
                 act=act, bias=bias)
    return x2, saved


def _layer_bwd(l, dres, sv, wt, small, scattering):
    tag = f"l{l}b"
    g = {}

    def scatter(n, partial):
        scattering[n][l] = _SplitExchange(partial, kind="scatter_rows", name=f"scatter_{n}_l{l}")
        return _SplitExchange.start([scattering[n][l]], name=f"scatter_{n}_l{l}_start")

    dx2, dx2_b = dres
    dact = _matmul(dx2_b, wt["w_down"][l], "nt", name=f"{tag}_dact", out_dtype=BF16)
    tok = scatter("w_down", _matmul(sv["act"], dx2_b, "tn", name=f"{tag}_dw_down", out_dtype=BF16))
    dup_g, dup_v, dwg, dwv, dbg, dbv = _ffn_act_bwd(sv["up0"], dact, wt["ffn_conv_w"][l], small["ffn_conv_b"][l][None] + tok,
                                                    name=f"{tag}_ffn_act")
    g["ffn_conv_w"] = jnp.concatenate([dwg, dwv], axis=1)
    g["ffn_conv_b"] = jnp.concatenate([dbg[0], dbv[0]])
    dh2 = _matmul(dup_g, wt["w_up_t"][l], "nn", name=f"{tag}_dh2_gate")
    dh2 = _matmul(dup_v, wt["w_up_t"][l], "nn", name=f"{tag}_dh2_val", b_k0=D_FF, residual=dh2)
    dw_up = _matmul(dup_g, sv["h2"], "tn", name=f"{tag}_dw_up_gate", out_dtype=BF16, rows=(2 * D_FF, 0))
    dw_up = _matmul(dup_v, sv["h2"], "tn", name=f"{tag}_dw_up_val", out_dtype=BF16, rows=(2 * D_FF, D_FF), into=dw_up)
    tok = scatter("w_up_t", dw_up)
    dx1, dx1_b, dg = _rmsnorm_bwd(sv["x1"], small["g_ffn"][l][None] + tok, dh2, dx2, name=f"{tag}_norm_ffn")
    g["g_ffn"] = dg[0]
    dy = _matmul(dx1_b, wt["w_out"][l], "nt", name=f"{tag}_dy")
    tok = scatter("w_out", _matmul(sv["y"], dx1_b, "tn", name=f"{tag}_dw_out", out_dtype=BF16))
    dya, dyb, dyc, dg = _combine_bwd(dy, sv["ya"], sv["yb"], sv["yc"], small["g_out"][l][None] + tok,
                                     name=f"{tag}_combine")
    g["g_out"] = dg[0]
    dq, dk, dv = _attn_bwd(sv["qkv"], sv["t_tot"], dyc, name=f"{tag}_attn")
    dp_b, g["sgu_w"], db, dg, dbeta = _sgu_bwd(sv["p_ab"], dyb, small["sgu_ln_g"][l][None], small["sgu_ln_b"][l][None],
                                               small["sgu_w"][l], sv["bias"], name=f"{tag}_sgu")
    g["sgu_b"] = db[:, :N_SGU_HEADS].T
    g["sgu_ln_g"], g["sgu_ln_b"] = dg[0], dbeta[0]
    dp_a, g["conv_w"], dcb, dg, dbeta = _mixer_a_bwd(sv["p_ab"], dya, wt["conv_w"][l], small["conv_b"][l][None],
                                                     small["conv_ln_g"][l][None], small["conv_ln_b"][l][None],
                                                     name=f"{tag}_mixer_a")
    g["conv_b"], g["conv_ln_g"], g["conv_ln_b"] = dcb[0], dg[0], dbeta[0]
    dp = jnp.concatenate([dp_a, dp_b, dq, dk.astype(BF16), dv.astype(BF16)], axis=1)
    dh = _matmul(dp, wt["w_in_t"][l], "nn", name=f"{tag}_dh")
    tok = scatter("w_in_t", _matmul(dp, sv["h"], "tn", name=f"{tag}_dw_in", out_dtype=BF16))
    dx, dx_b, dg = _rmsnorm_bwd(sv["x"], small["g_mix"][l][None] + tok, dh, dx1, name=f"{tag}_norm_mix")
    g["g_mix"] = dg[0]
    return (dx, dx_b), g


_BIG = ("w_in_t", "w_out", "w_up_t", "w_down")


def kernel(x, g_mix, w_in, conv_w, conv_b, conv_ln_g, conv_ln_b, sgu_ln_g, sgu_ln_b, sgu_w, sgu_b, g_out, w_out, g_ffn, w_up, ffn_conv_w, ffn_conv_b, w_down, g_final, loss_target, m_g_mix, m_w_in, m_conv_w, m_conv_b, m_conv_ln_g, m_conv_ln_b, m_sgu_ln_g, m_sgu_ln_b, m_sgu_w, m_sgu_b, m_g_out, m_w_out, m_g_ffn, m_w_up, m_ffn_conv_w, m_ffn_conv_b, m_w_down, m_g_final, v_g_mix, v_w_in, v_conv_w, v_conv_b, v_conv_ln_g, v_conv_ln_b, v_sgu_ln_g, v_sgu_ln_b, v_sgu_w, v_sgu_b, v_g_out, v_w_out, v_g_ffn, v_w_up, v_ffn_conv_w, v_ffn_conv_b, v_w_down, v_g_final):
    given = dict(locals())
    n_layers = g_mix.shape[0]
    layers = range(n_layers)
    small_sharded = [n for n, _ in _SMALL_SHARDED]
    replicated = [n for n, _ in _REPLICATED]
    small = {n: given[n] for n in replicated}

    filters = _gather_blocks(_pack([given[n] for n in small_sharded]), name="gather_filters")
    filters, first_shard = lax.optimization_barrier((filters, w_in[0].T.astype(BF16)))
    wt = {n: [None] * n_layers for n in _BIG}
    wt["w_in_t"][0] = _SplitExchange(first_shard, kind="gather_rows", name="gather_w_in_t_l0")
    tok = _SplitExchange.start([wt["w_in_t"][0]], name="gather_w_in_t_l0_start")
    w_in, w_out, w_up, w_down, tok = lax.optimization_barrier((w_in, w_out, w_up, w_down, tok))
    shard = {"w_in_t": [w_in[l].T.astype(BF16) for l in layers], "w_out": [w_out[l].astype(BF16) for l in layers],
             "w_up_t": [w_up[l].T.astype(BF16) for l in layers], "w_down": [w_down[l].astype(BF16) for l in layers]}
    later = [(n, l) for l in layers for n in _BIG if (n, l) != ("w_in_t", 0)]
    for n, l in later:
        wt[n][l] = _SplitExchange(shard[n][l], kind="gather_rows", name=f"gather_{n}_l{l}")
    small["g_mix"] = g_mix + tok + _SplitExchange.start([wt[n][l] for n, l in later], name="gather_weights_start")
    for n, blocks in zip(small_sharded, _unpack(filters, [s for _, s in _SMALL_SHARDED], lead=(N_DEV,))):
        wt[n] = _join_last(blocks)

    xs = x[0]
    saved = []
    for l in layers:
        xs, sv = _layer_fwd(l, xs, wt, small)
        saved.append(sv)
    loss_tile, dx, dx_b, dgf = _loss_head(xs, g_final[None], loss_target[0], name="loss_head")
    dres = (dx, dx_b)
    scattering = {n: [None] * n_layers for n in _BIG}
    grads = [None] * n_layers
    for l in reversed(layers):
        dres, grads[l] = _layer_bwd(l, dres, saved[l], wt, small, scattering)
    partial = {n: jnp.stack([g[n] for g in grads]) for n in grads[0]}
    partial["g_final"] = dgf[0]

    own = _pack([_split_last(partial[n]) for n in small_sharded], lead=(N_DEV,))
    shared = _pack([partial[n] for n in replicated])
    slab = jnp.concatenate([own, jnp.broadcast_to(shared[None], (N_DEV,) + shared.shape)], axis=1)
    small_grads = _SplitExchange(slab, kind="scatter_blocks", name="scatter_small_grads")
    tok = _SplitExchange.start([small_grads], name="scatter_small_grads_start")
    dx_out, tok = lax.optimization_barrier((dres[0], tok))
    received = {n: [scattering[n][l].finish(dx_out) for l in layers] for n in _BIG}
    out = {}

    def update(n, parts):
        results = _adamw(parts, given[n], given["m_" + n], given["v_" + n], name=f"adamw_{n}")
        for pre, res in zip(("grad_", "delta_", "new_m_", "new_v_"), results):
            out[pre + n] = res

    update("w_out", received["w_out"])
    update("w_down", received["w_down"])
    for n in ("w_in", "w_up"):
        update(n, jnp.swapaxes(_sum_parts(received[n + "_t"], name=f"sum_{n}"), 1, 2))

    slab = small_grads.finish(out["grad_w_up"])
    stacks = [jnp.concatenate([_pack([given[pre + n] for n in small_sharded]),
                               _pack([given[pre + n] for n in replicated])])[None] for pre in ("", "m_", "v_")]
    results = _adamw([slab], *stacks, name="adamw_small")
    n_own = own.shape[1]
    for pre, res in zip(("grad_", "delta_", "new_m_", "new_v_"), results):
        unpacked = (_unpack(res[0, :n_own], [s for _, s in _SMALL_SHARDED])
                    + _unpack(res[0, n_own:], [s for _, s in _REPLICATED]))
        for n, a in zip(small_sharded + replicated, unpacked):
            out[pre + n] = a

    loss = lax.psum(loss_tile[0, 0], ("x", "y", "c"))
    order = list(_WEIGHT_ORDER)
    return (loss, dres[0][None], *[out["grad_" + n] for n in order], *[out["delta_" + n] for n in order],
            *[out["new_m_" + n] for n in order], *[out["new_v_" + n] for n in order])


_WEIGHT_ORDER = ("g_mix", "w_in", "conv_w", "conv_b", "conv_ln_g", "conv_ln_b", "sgu_ln_g", "sgu_ln_b", "sgu_w", "sgu_b",
                 "g_out", "w_out", "g_ffn", "w_up", "ffn_conv_w", "ffn_conv_b", "w_down", "g_final")
```

```python
import math

import jax
import jax.numpy as jnp
from jax import lax
from jax.experimental import pallas as pl
from jax.experimental.pallas import tpu as pltpu

F32 = jnp.float32
BF16 = jnp.bfloat16

N_DEV = 8
D_MODEL = 1024
HEAD_DIM = 64
D_CONV = 256
D_SGU = 256
D_SB = 512
D_AB = 2 * D_CONV + 2 * D_SGU
D_QKV = 3 * D_SB
D_IN = D_AB + D_QKV
CONV_K = 31
CONV_HALO = 32
FFN_K = 3
FFN_HALO = 8
D_FF = 2816
CHUNK = 128
EPS = 1e-6
LANES = 128

ADAM_LR = 0.001
ADAM_B1 = 0.9
ADAM_B2 = 0.999
ADAM_EPS = 1e-08
ADAM_WD = 0.01
ADAM_STEP = 10

VMEM_LIMIT = 56 * 1024 * 1024


def _cparams(sem=None):
    return pltpu.CompilerParams(dimension_semantics=sem, vmem_limit_bytes=VMEM_LIMIT)


def _tile(n, prefs=(512, 256, 128)):
    for t in prefs:
        if n % t == 0:
            return t
    return n


def _sigmoid(x):
    return 1.0 / (1.0 + jnp.exp(-x))


_INV_SQRT2 = 1.0 / math.sqrt(2.0)
_INV_SQRT2PI = 1.0 / math.sqrt(2.0 * math.pi)


def _gelu(x):
    return 0.5 * x * (1.0 + lax.erf(x * _INV_SQRT2))


def _gelu_grad(x):
    return 0.5 * (1.0 + lax.erf(x * _INV_SQRT2)) + x * jnp.exp(-0.5 * x * x) * _INV_SQRT2PI


def _dot(a, b, dims):
    return lax.dot_general(a, b, (dims, ((), ())), preferred_element_type=F32)


_NN = ((1,), (0,))
_NT = ((1,), (1,))
_TN = ((0,), (0,))


def _split_bf16(x):
    hi = x.astype(BF16)
    lo = (x - hi.astype(F32)).astype(BF16)
    return jnp.concatenate([hi, lo], axis=1)


def _matmul(a, b, mode, *, name, out_dtype=F32, residual=None, n=None, b_n0=0, b_k0=0, rows=None, into=None):
    if mode == "nn":
        (m, k), n = a.shape, (n or b.shape[1])
    elif mode == "nt":
        (m, k), n = a.shape, (n or b.shape[0])
    else:
        (k, m), n = a.shape, b.shape[1]
    has_res = residual is not None
    tm, tn = _matmul_tiles(m, n, k, a.dtype.itemsize, b.dtype.itemsize, jnp.dtype(out_dtype).itemsize, has_res, b_n0)
    j0 = b_n0 // tn
    total_rows, first_row = rows or (m, 0)
    assert b_k0 % k == 0 and first_row % tm == 0
    kb, i0 = b_k0 // k, first_row // tm

    if mode == "nn":
        a_spec = pl.BlockSpec((tm, k), lambda i, j: (i, 0))
        b_spec = pl.BlockSpec((k, tn), lambda i, j: (kb, j + j0))
        dims = _NN
    elif mode == "nt":
        a_spec = pl.BlockSpec((tm, k), lambda i, j: (i, 0))
        b_spec = pl.BlockSpec((tn, k), lambda i, j: (j + j0, 0))
        dims = _NT
    else:
        a_spec = pl.BlockSpec((k, tm), lambda i, j: (0, i))
        b_spec = pl.BlockSpec((k, tn), lambda i, j: (0, j))
        dims = _TN
    o_spec = pl.BlockSpec((tm, tn), lambda i, j: (i + i0, j))
    r_spec = pl.BlockSpec((tm, tn), lambda i, j: (i, j))

    def body(*refs):
        a_ref, b_ref = refs[:2]
        acc = _dot(a_ref[...].astype(BF16), b_ref[...].astype(BF16), dims)
        if has_res:
            acc = acc + refs[2][...]
        refs[-1][...] = acc.astype(out_dtype)

    in_specs = [a_spec, b_spec] + ([r_spec] if has_res else [])
    args = (a, b) + ((residual,) if has_res else ())
    aliases = {}
    if into is not None:
        aliases = {len(args): 0}
        in_specs.append(pl.BlockSpec(memory_space=pl.ANY))
        args += (into,)

        def body(*refs, inner=body):
            inner(*refs[:len(args) - 1], refs[-1])

    return pl.pallas_call(
        body,
        name=name,
        grid=(m // tm, n // tn),
        in_specs=in_specs,
        out_specs=o_spec,
        out_shape=jax.ShapeDtypeStruct((total_rows, n), out_dtype),
        input_output_aliases=aliases,
        compiler_params=_cparams(("parallel", "parallel")),
    )(*args)


MATMUL_VMEM_BUDGET = 40 * 1024 * 1024


def _matmul_tiles(m, n, k, a_bytes, b_bytes, out_bytes, has_res, n_offset):
    def divisors(size, cap, also=0):
        return [t for t in range(cap, 0, -LANES) if size % t == 0 and also % t == 0] or [size]

    for tm in divisors(m, 1024):
        for tn in divisors(n, 1408, n_offset):
            blocks = tm * k * a_bytes + k * tn * b_bytes + tm * tn * (out_bytes + (4 if has_res else 0))
            if 2 * blocks <= MATMUL_VMEM_BUDGET:
                return tm, tn
    raise ValueError(f"no matmul tiling for {m} x {n} x {k}")


ROW_TILE = 256


def _rmsnorm_fwd(x, g, *, name):
    s, d = x.shape

    def body(x_ref, g_ref, h_ref):
        xv = x_ref[...]
        r = lax.rsqrt(jnp.mean(xv * xv, axis=-1, keepdims=True) + EPS)
        h_ref[...] = (xv * r * g_ref[...]).astype(BF16)

    return pl.pallas_call(
        body,
        name=name,
        grid=(s // ROW_TILE,),
        in_specs=[pl.BlockSpec((ROW_TILE, d), lambda i: (i, 0)), pl.BlockSpec((1, d), lambda i: (0, 0))],
        out_specs=pl.BlockSpec((ROW_TILE, d), lambda i: (i, 0)),
        out_shape=jax.ShapeDtypeStruct((s, d), BF16),
        compiler_params=_cparams(("parallel",)),
    )(x, g)


def _rmsnorm_bwd(x, g, dh, dres, *, name):
    s, d = x.shape

    def body(x_ref, g_ref, dh_ref, dres_ref, dx_ref, dxb_ref, dg_ref):
        xv = x_ref[...]
        r = lax.rsqrt(jnp.mean(xv * xv, axis=-1, keepdims=True) + EPS)
        xhat = xv * r
        dhv = dh_ref[...]
        dxhat = dhv * g_ref[...]
        dx = dres_ref[...] + r * (dxhat - xhat * jnp.mean(dxhat * xhat, axis=-1, keepdims=True))
        dx_ref[...] = dx
        dxb_ref[...] = dx.astype(BF16)
        part = jnp.sum(dhv * xhat, axis=0, keepdims=True)

        @pl.when(pl.program_id(0) == 0)
        def _():
            dg_ref[...] = part

        @pl.when(pl.program_id(0) > 0)
        def _():
            dg_ref[...] += part

    row = pl.BlockSpec((ROW_TILE, d), lambda i: (i, 0))
    vec = pl.BlockSpec((1, d), lambda i: (0, 0))
    return pl.pallas_call(
        body,
        name=name,
        grid=(s // ROW_TILE,),
        in_specs=[row, vec, row, row],
        out_specs=[row, row, vec],
        out_shape=[jax.ShapeDtypeStruct((s, d), F32), jax.ShapeDtypeStruct((s, d), BF16),
                   jax.ShapeDtypeStruct((1, d), F32)],
        compiler_params=_cparams(("arbitrary",)),
    )(x, g, dh, dres)


def _loss_head(x, g, target, *, name):
    s, d = x.shape

    def body(x_ref, g_ref, t_ref, loss_ref, dx_ref, dxb_ref, dg_ref):
        xv = x_ref[...]
        gv = g_ref[...]
        r = lax.rsqrt(jnp.mean(xv * xv, axis=-1, keepdims=True) + EPS)
        xhat = xv * r
        diff = xhat * gv - t_ref[...]
        dy = diff * (1.0 / d)
        dxhat = dy * gv
        dx = r * (dxhat - xhat * jnp.mean(dxhat * xhat, axis=-1, keepdims=True))
        dx_ref[...] = dx
        dxb_ref[...] = dx.astype(BF16)
        dg_part = jnp.sum(dy * xhat, axis=0, keepdims=True)
        row_loss = jnp.sum(diff * diff, axis=-1, keepdims=True)
        loss_part = jnp.sum(row_loss, axis=0, keepdims=True) * (0.5 / d)

        @pl.when(pl.program_id(0) == 0)
        def _():
            dg_ref[...] = dg_part
            loss_ref[...] = jnp.broadcast_to(loss_part, loss_ref.shape)

        @pl.when(pl.program_id(0) > 0)
        def _():
            dg_ref[...] += dg_part
            loss_ref[...] += jnp.broadcast_to(loss_part, loss_ref.shape)

    row = pl.BlockSpec((ROW_TILE, d), lambda i: (i, 0))
    vec = pl.BlockSpec((1, d), lambda i: (0, 0))
    tile = pl.BlockSpec((8, LANES), lambda i: (0, 0))
    return pl.pallas_call(
        body,
        name=name,
        grid=(s // ROW_TILE,),
        in_specs=[row, vec, row],
        out_specs=[tile, row, row, vec],
        out_shape=[jax.ShapeDtypeStruct((8, LANES), F32), jax.ShapeDtypeStruct((s, d), F32),
                   jax.ShapeDtypeStruct((s, d), BF16), jax.ShapeDtypeStruct((1, d), F32)],
        compiler_params=_cparams(("arbitrary",)),
    )(x, g, target)


_BRANCHES = ((0, D_CONV), (D_CONV, D_SGU), (D_CONV + D_SGU, D_SB))


def _combine_fwd(ya, yb, yc, g, *, name):
    s = ya.shape[0]

    def body(ya_ref, yb_ref, yc_ref, g_ref, y_ref):
        for ref, (off, w) in zip((ya_ref, yb_ref, yc_ref), _BRANCHES):
            v = ref[...]
            r = lax.rsqrt(jnp.mean(v * v, axis=-1, keepdims=True) + EPS)
            y_ref[:, off:off + w] = (v * r * g_ref[:, off:off + w]).astype(BF16)

    def row(w):
        return pl.BlockSpec((ROW_TILE, w), lambda i: (i, 0))

    return pl.pallas_call(
        body,
        name=name,
        grid=(s // ROW_TILE,),
        in_specs=[row(D_CONV), row(D_SGU), row(D_SB), pl.BlockSpec((1, D_MODEL), lambda i: (0, 0))],
        out_specs=row(D_MODEL),
        out_shape=jax.ShapeDtypeStruct((s, D_MODEL), BF16),
        compiler_params=_cparams(("parallel",)),
    )(ya, yb, yc, g)


def _combine_bwd(dy, ya, yb, yc, g, *, name):
    s = ya.shape[0]

    def body(dy_ref, ya_ref, yb_ref, yc_ref, g_ref, dya_ref, dyb_ref, dyc_ref, dg_ref):
        first = pl.program_id(0) == 0
        for ref, dref, (off, w) in zip((ya_ref, yb_ref, yc_ref), (dya_ref, dyb_ref, dyc_ref), _BRANCHES):
            v = ref[...]
            r = lax.rsqrt(jnp.mean(v * v, axis=-1, keepdims=True) + EPS)
            n = v * r
            dout = dy_ref[:, off:off + w]
            dn = dout * g_ref[:, off:off + w]
            dref[...] = r * (dn - n * jnp.mean(dn * n, axis=-1, keepdims=True))
            part = jnp.sum(dout * n, axis=0, keepdims=True)

            @pl.when(first)
            def _():
                dg_ref[:, off:off + w] = part

            @pl.when(jnp.logical_not(first))
            def _():
                dg_ref[:, off:off + w] += part

    def row(w):
        return pl.BlockSpec((ROW_TILE, w), lambda i: (i, 0))

    vec = pl.BlockSpec((1, D_MODEL), lambda i: (0, 0))
    return pl.pallas_call(
        body,
        name=name,
        grid=(s // ROW_TILE,),
        in_specs=[row(D_MODEL), row(D_CONV), row(D_SGU), row(D_SB), vec],
        out_specs=[row(D_CONV), row(D_SGU), row(D_SB), vec],
        out_shape=[jax.ShapeDtypeStruct((s, D_CONV), F32), jax.ShapeDtypeStruct((s, D_SGU), F32),
                   jax.ShapeDtypeStruct((s, D_SB), F32), jax.ShapeDtypeStruct((1, D_MODEL), F32)],
        compiler_params=_cparams(("arbitrary",)),
    )(dy, ya, yb, yc, g)


CONV_TILE = 128


def _shift_down(window, j, halo):
    return pltpu.roll(window, j, 0)[halo:, :] if j else window[halo:, :]


def _shift_up(window, j, n_out):
    n = window.shape[0]
    return pltpu.roll(window, n - j, 0)[:n_out, :] if j else window[:n_out, :]


def _mixer_a_fwd(p_ab, conv_w, conv_b, ln_g, ln_b, *, name):
    s = p_ab.shape[0]
    nt = s // CONV_TILE

    def body(p_ref, w_ref, b_ref, g_ref, beta_ref, y_ref, h_ref):
        h_ref[0:CONV_HALO, :] = jnp.zeros((CONV_HALO, D_CONV), F32)

        def glu(i, c):
            t0 = pl.multiple_of(i * CONV_TILE, CONV_TILE)
            a = p_ref[pl.ds(t0, CONV_TILE), 0:D_CONV]
            gate = p_ref[pl.ds(t0, CONV_TILE), D_CONV:2 * D_CONV]
            h_ref[pl.ds(t0 + CONV_HALO, CONV_TILE), :] = a * _sigmoid(gate)
            return c

        lax.fori_loop(0, nt, glu, 0)

        def conv(i, c):
            t0 = pl.multiple_of(i * CONV_TILE, CONV_TILE)
            window = h_ref[pl.ds(t0, CONV_TILE + CONV_HALO), :]
            acc = jnp.zeros((CONV_TILE, D_CONV), F32) + b_ref[...]
            for k in range(CONV_K):
                acc = acc + w_ref[k:k + 1, :] * _shift_down(window, CONV_K - 1 - k, CONV_HALO)
            mu = jnp.mean(acc, axis=-1, keepdims=True)
            xc = acc - mu
            rstd = lax.rsqrt(jnp.mean(xc * xc, axis=-1, keepdims=True) + EPS)
            z = xc * rstd * g_ref[...] + beta_ref[...]
            y_ref[pl.ds(t0, CONV_TILE), :] = z * _sigmoid(z)
            return c

        lax.fori_loop(0, nt, conv, 0)

    full = lambda shape: pl.BlockSpec(shape, lambda i: (0, 0))
    return pl.pallas_call(
        body,
        name=name,
        grid=(1,),
        in_specs=[full((s, 2 * D_CONV)), full((CONV_K, D_CONV)), full((1, D_CONV)), full((1, D_CONV)),
                  full((1, D_CONV))],
        out_specs=full((s, D_CONV)),
        out_shape=jax.ShapeDtypeStruct((s, D_CONV), F32),
        scratch_shapes=[pltpu.VMEM((s + CONV_HALO, D_CONV), F32)],
        compiler_params=_cparams(("arbitrary",)),
    )(p_ab, conv_w, conv_b, ln_g, ln_b)


def _mixer_a_bwd(p_ab, dya, conv_w, conv_b, ln_g, ln_b, *, name):
    s = p_ab.shape[0]
    nt = s // CONV_TILE

    def body(p_ref, dy_ref, w_ref, b_ref, g_ref, beta_ref, dp_ref, dw_ref, db_ref, dg_ref, dbeta_ref, h_ref, dc_ref):
        h_ref[0:CONV_HALO, :] = jnp.zeros((CONV_HALO, D_CONV), F32)
        dc_ref[s:s + CONV_HALO, :] = jnp.zeros((CONV_HALO, D_CONV), F32)
        dw_ref[...] = jnp.zeros_like(dw_ref)
        db_ref[...] = jnp.zeros_like(db_ref)
        dg_ref[...] = jnp.zeros_like(dg_ref)
        dbeta_ref[...] = jnp.zeros_like(dbeta_ref)

        def glu(i, c):
            t0 = pl.multiple_of(i * CONV_TILE, CONV_TILE)
            a = p_ref[pl.ds(t0, CONV_TILE), 0:D_CONV]
            gate = p_ref[pl.ds(t0, CONV_TILE), D_CONV:2 * D_CONV]
            h_ref[pl.ds(t0 + CONV_HALO, CONV_TILE), :] = a * _sigmoid(gate)
            return c

        lax.fori_loop(0, nt, glu, 0)

        def conv_bwd(i, c):
            t0 = pl.multiple_of(i * CONV_TILE, CONV_TILE)
            window = h_ref[pl.ds(t0, CONV_TILE + CONV_HALO), :]
            taps = [_shift_down(window, CONV_K - 1 - k, CONV_HALO) for k in range(CONV_K)]
            acc = jnp.zeros((CONV_TILE, D_CONV), F32) + b_ref[...]
            for k in range(CONV_K):
                acc = acc + w_ref[k:k + 1, :] * taps[k]
            mu = jnp.mean(acc, axis=-1, keepdims=True)
            xc = acc - mu
            rstd = lax.rsqrt(jnp.mean(xc * xc, axis=-1, keepdims=True) + EPS)
            xhat = xc * rstd
            z = xhat * g_ref[...] + beta_ref[...]
            sg = _sigmoid(z)
            dz = dy_ref[pl.ds(t0, CONV_TILE), :] * (sg * (1.0 + z * (1.0 - sg)))
            dg_ref[...] += jnp.sum(dz * xhat, axis=0, keepdims=True)
            dbeta_ref[...] += jnp.sum(dz, axis=0, keepdims=True)
            dxhat = dz * g_ref[...]
            dc = rstd * (dxhat - jnp.mean(dxhat, axis=-1, keepdims=True)
                         - xhat * jnp.mean(dxhat * xhat, axis=-1, keepdims=True))
            dc_ref[pl.ds(t0, CONV_TILE), :] = dc
            db_ref[...] += jnp.sum(dc, axis=0, keepdims=True)
            for k in range(CONV_K):
                dw_ref[k:k + 1, :] += jnp.sum(dc * taps[k], axis=0, keepdims=True)
            return c

        lax.fori_loop(0, nt, conv_bwd, 0)

        def glu_bwd(i, c):
            t0 = pl.multiple_of(i * CONV_TILE, CONV_TILE)
            window = dc_ref[pl.ds(t0, CONV_TILE + CONV_HALO), :]
            dh = jnp.zeros((CONV_TILE, D_CONV), F32)
            for j in range(CONV_K):
                dh = dh + w_ref[CONV_K - 1 - j:CONV_K - j, :] * _shift_up(window, j, CONV_TILE)
            a = p_ref[pl.ds(t0, CONV_TILE), 0:D_CONV]
            sg = _sigmoid(p_ref[pl.ds(t0, CONV_TILE), D_CONV:2 * D_CONV])
            dp_ref[pl.ds(t0, CONV_TILE), 0:D_CONV] = (dh * sg).astype(BF16)
            dp_ref[pl.ds(t0, CONV_TILE), D_CONV:2 * D_CONV] = (dh * a * sg * (1.0 - sg)).astype(BF16)
            return c

        lax.fori_loop(0, nt, glu_bwd, 0)

    full = lambda shape: pl.BlockSpec(shape, lambda i: (0, 0))
    vec = jax.ShapeDtypeStruct((1, D_CONV), F32)
    return pl.pallas_call(
        body,
        name=name,
        grid=(1,),
        in_specs=[full((s, 2 * D_CONV)), full((s, D_CONV)), full((CONV_K, D_CONV)), full((1, D_CONV)),
                  full((1, D_CONV)), full((1, D_CONV))],
        out_specs=[full((s, 2 * D_CONV)), full((CONV_K, D_CONV)), full((1, D_CONV)), full((1, D_CONV)),
                   full((1, D_CONV))],
        out_shape=[jax.ShapeDtypeStruct((s, 2 * D_CONV), BF16), jax.ShapeDtypeStruct((CONV_K, D_CONV), F32),
                   vec, vec, vec],
        scratch_shapes=[pltpu.VMEM((s + CONV_HALO, D_CONV), F32), pltpu.VMEM((s + CONV_HALO, D_CONV), F32)],
        compiler_params=_cparams(("arbitrary",)),
    )(p_ab, dya, conv_w, conv_b, ln_g, ln_b)


N_SGU_HEADS = D_SGU // HEAD_DIM


def _head_masks(width):
    lane = lax.broadcasted_iota(jnp.int32, (1, width), 1)
    return [(lane >= h * HEAD_DIM) & (lane < (h + 1) * HEAD_DIM) for h in range(width // HEAD_DIM)]


def _tril_mask():
    r = lax.broadcasted_iota(jnp.int32, (CHUNK, CHUNK), 0)
    c = lax.broadcasted_iota(jnp.int32, (CHUNK, CHUNK), 1)
    return c <= r


def _sgu_norm(bv, g, beta):
    vg = _gelu(bv)
    mu = jnp.mean(vg, axis=-1, keepdims=True)
    xc = vg - mu
    rstd = lax.rsqrt(jnp.mean(xc * xc, axis=-1, keepdims=True) + EPS)
    xhat = xc * rstd
    return xhat, rstd, xhat * g + beta


def _sgu_fwd(p_ab, ln_g, ln_b, w_s, bias, *, name):
    s = p_ab.shape[0]

    def body(p_ref, g_ref, beta_ref, w_ref, bias_ref, y_ref):
        u = _gelu(p_ref[:, 0:D_SGU])
        _, _, vn = _sgu_norm(p_ref[:, D_SGU:2 * D_SGU], g_ref[...], beta_ref[...])
        vb = vn.astype(BF16)
        tril = _tril_mask()
        mixed = bias_ref[...]
        for h, m in enumerate(_head_masks(D_SGU)):
            wh = jnp.where(tril, w_ref[h], 0.0).astype(BF16)
            mixed = mixed + _dot(wh, jnp.where(m, vb, jnp.zeros_like(vb)), _NN)
        y_ref[...] = u * mixed

    return pl.pallas_call(
        body,
        name=name,
        grid=(s // CHUNK,),
        in_specs=[pl.BlockSpec((CHUNK, 2 * D_SGU), lambda i: (i, 1)),
                  pl.BlockSpec((1, D_SGU), lambda i: (0, 0)), pl.BlockSpec((1, D_SGU), lambda i: (0, 0)),
                  pl.BlockSpec((N_SGU_HEADS, CHUNK, CHUNK), lambda i: (0, 0, 0)),
                  pl.BlockSpec((CHUNK, D_SGU), lambda i: (0, 0))],
        out_specs=pl.BlockSpec((CHUNK, D_SGU), lambda i: (i, 0)),
        out_shape=jax.ShapeDtypeStruct((s, D_SGU), F32),
        compiler_params=_cparams(("parallel",)),
    )(p_ab, ln_g, ln_b, w_s, bias)


def _sgu_bwd(p_ab, dyb, ln_g, ln_b, w_s, bias, *, name):
    s = p_ab.shape[0]
    n_chunks = s // CHUNK

    def body(p_ref, dy_ref, g_ref, beta_ref, w_ref, bias_ref, dp_ref, dw_ref, db_ref, dg_ref, dbeta_ref, dbias_ref):
        @pl.when(pl.program_id(0) == 0)
        def _():
            dw_ref[...] = jnp.zeros_like(dw_ref)
            dbias_ref[...] = jnp.zeros_like(dbias_ref)
            dg_ref[...] = jnp.zeros_like(dg_ref)
            dbeta_ref[...] = jnp.zeros_like(dbeta_ref)

        bu = p_ref[:, 0:D_SGU]
        bv = p_ref[:, D_SGU:2 * D_SGU]
        u = _gelu(bu)
        gv = g_ref[...]
        xhat, rstd, vn = _sgu_norm(bv, gv, beta_ref[...])
        vb = vn.astype(BF16)
        tril = _tril_mask()
        masks = _head_masks(D_SGU)
        whs = [jnp.where(tril, w_ref[h], 0.0).astype(BF16) for h in range(N_SGU_HEADS)]
        mixed = bias_ref[...]
        for h, m in enumerate(masks):
            mixed = mixed + _dot(whs[h], jnp.where(m, vb, jnp.zeros_like(vb)), _NN)
        dy = dy_ref[...]
        dp_ref[:, 0:D_SGU] = (dy * mixed * _gelu_grad(bu)).astype(BF16)
        dmixed = dy * u
        dbias_ref[...] += dmixed
        dmb = dmixed.astype(BF16)
        dvn = jnp.zeros((CHUNK, D_SGU), F32)
        for h, m in enumerate(masks):
            dmh = jnp.where(m, dmb, jnp.zeros_like(dmb))
            dvn = dvn + _dot(whs[h], dmh, _TN)
            dw_ref[h] += jnp.where(tril, _dot(dmh, vb, _NT), 0.0)
        dg_ref[...] += jnp.sum(dvn * xhat, axis=0, keepdims=True)
        dbeta_ref[...] += jnp.sum(dvn, axis=0, keepdims=True)
        dxhat = dvn * gv
        dvg = rstd * (dxhat - jnp.mean(dxhat, axis=-1, keepdims=True)
                      - xhat * jnp.mean(dxhat * xhat, axis=-1, keepdims=True))
        dp_ref[:, D_SGU:2 * D_SGU] = (dvg * _gelu_grad(bv)).astype(BF16)

        @pl.when(pl.program_id(0) == n_chunks - 1)
        def _():
            chan = lax.broadcasted_iota(jnp.int32, (D_SGU, LANES), 0)
            head = lax.broadcasted_iota(jnp.int32, (D_SGU, LANES), 1)
            to_head = jnp.where(chan // HEAD_DIM == head, 1.0, 0.0).astype(BF16)
            db_ref[...] = _dot(_split_bf16(dbias_ref[...]), jnp.concatenate([to_head, to_head], axis=0), _NN)

    vec = pl.BlockSpec((1, D_SGU), lambda i: (0, 0))
    wspec = pl.BlockSpec((N_SGU_HEADS, CHUNK, CHUNK), lambda i: (0, 0, 0))
    bspec = pl.BlockSpec((CHUNK, D_SGU), lambda i: (0, 0))
    return pl.pallas_call(
        body,
        name=name,
        grid=(n_chunks,),
        in_specs=[pl.BlockSpec((CHUNK, 2 * D_SGU), lambda i: (i, 1)), pl.BlockSpec((CHUNK, D_SGU), lambda i: (i, 0)),
                  vec, vec, wspec, bspec],
        out_specs=[pl.BlockSpec((CHUNK, 2 * D_SGU), lambda i: (i, 0)), wspec,
                   pl.BlockSpec((CHUNK, LANES), lambda i: (0, 0)), vec, vec],
        out_shape=[jax.ShapeDtypeStruct((s, 2 * D_SGU), BF16),
                   jax.ShapeDtypeStruct((N_SGU_HEADS, CHUNK, CHUNK), F32),
                   jax.ShapeDtypeStruct((CHUNK, LANES), F32),
                   jax.ShapeDtypeStruct((1, D_SGU), F32), jax.ShapeDtypeStruct((1, D_SGU), F32)],
        scratch_shapes=[pltpu.VMEM((CHUNK, D_SGU), F32)],
        compiler_params=_cparams(("arbitrary",)),
    )(p_ab, dyb, ln_g, ln_b, w_s, bias)


N_PAIRS = D_SB // LANES
SB_SCALE = HEAD_DIM ** -0.5


def _sb_logits(z, valid):
    nz = -z
    t = jnp.log(1.0 + jnp.exp(jnp.minimum(z, nz)))
    l1 = jnp.minimum(nz, 0.0) - t
    if valid is not None:
        l1 = jnp.where(valid, l1, 0.0)
    return l1, jnp.minimum(z, 0.0) - t


def _split_hi_lo(x):
    hi = lax.bitcast_convert_type(lax.bitcast_convert_type(x, jnp.uint32) & jnp.uint32(0xFFFF0000), F32)
    return jnp.concatenate([hi, x - hi], axis=1)


def _cumsum_operand(keep):
    half = jnp.concatenate([keep.astype(F32), jnp.ones((CHUNK, CHUNK), F32)], axis=1)
    return jnp.concatenate([half, half], axis=0)


Q_BLOCKS_PER_STEP = 4


def _q_blocks_per_step(nq):
    return next(n for n in (Q_BLOCKS_PER_STEP, 2, 1) if nq % n == 0)


def _attn_fwd(qkv, *, name):
    s = qkv.shape[0]
    nq = s // CHUNK
    per_step = _q_blocks_per_step(nq)

    def body(q_ref, k_ref, v_ref, o_ref, t_ref):
        masks = _head_masks(LANES)
        row = lax.broadcasted_iota(jnp.int32, (CHUNK, CHUNK), 0)
        col = lax.broadcasted_iota(jnp.int32, (CHUNK, CHUNK), 1)
        after_op = _cumsum_operand(row > col)
        cmr = col - row
        zc = jnp.zeros((CHUNK, LANES), F32)

        def q_block(sub, _):
            qi = pl.program_id(1) * per_step + sub
            q_rows = pl.ds(pl.multiple_of(sub * CHUNK, CHUNK), CHUNK)
            q = q_ref[q_rows, :] * SB_SCALE
            zero = jnp.zeros_like(q)
            qs = [jnp.where(m, q, zero) for m in masks]

            def blocks(js, carry):
                o, c0, c1 = carry
                kvs, valids = [], []
                for j in js:
                    k0 = pl.multiple_of(jnp.maximum(j, 0) * CHUNK, CHUNK)
                    kvs.append((k_ref[pl.ds(k0, CHUNK), :], v_ref[pl.ds(k0, CHUNK), :]))
                    valids.append(cmr < jnp.where(j >= 0, (qi - j) * CHUNK, -CHUNK))
                units = [(h, b) for b in range(len(js)) for h in range(2)]
                zs = [_dot(qs[h], kvs[b][0], _NT) for h, b in units]
                logits = [_sb_logits(z, valids[b]) for z, (h, b) in zip(zs, units)]
                sums = [_dot(_split_hi_lo(l1), after_op, _NN) for l1, _ in logits]
                cs = [c0, c1]
                probs = []
                for (h, b), (_, lb), sm in zip(units, logits, sums):
                    probs.append(jnp.where(valids[b], jnp.exp(lb + sm[:, :CHUNK] + cs[h]), 0.0))
                    cs[h] = cs[h] + sm[:, CHUNK:]
                for (h, b), a in zip(units, probs):
                    o = o + _dot(a.astype(BF16), jnp.where(masks[h], kvs[b][1], zero), _NN)
                return o, cs[0], cs[1]

            n_four = (qi + 1) // 4
            carry = lax.fori_loop(0, n_four, lambda jj, c: blocks([qi - 4 * jj - i for i in range(4)], c), (zc,) * 3)
            top = qi - 4 * n_four
            o, c0, c1 = lax.fori_loop(0, (top + 2) // 2, lambda jj, c: blocks([top - 2 * jj, top - 2 * jj - 1], c), carry)
            o_ref[q_rows, :] = o
            t_ref[q_rows, 0:LANES] = c0
            t_ref[q_rows, LANES:2 * LANES] = c1
            return 0

        lax.fori_loop(0, per_step, q_block, 0)

    rows = per_step * CHUNK
    return pl.pallas_call(
        body,
        name=name,
        grid=(N_PAIRS, nq // per_step),
        in_specs=[pl.BlockSpec((rows, LANES), lambda p, i: (i, p)),
                  pl.BlockSpec((s, LANES), lambda p, i: (0, N_PAIRS + p)),
                  pl.BlockSpec((s, LANES), lambda p, i: (0, 2 * N_PAIRS + p))],
        out_specs=[pl.BlockSpec((rows, LANES), lambda p, i: (i, p)),
                   pl.BlockSpec((rows, 2 * LANES), lambda p, i: (i, p))],
        out_shape=[jax.ShapeDtypeStruct((s, D_SB), F32), jax.ShapeDtypeStruct((s, 2 * D_SB), F32)],
        compiler_params=_cparams(("parallel", "parallel")),
    )(qkv, qkv, qkv)


def _attn_bwd(qkv, t_tot, do, *, name):
    s = qkv.shape[0]
    nq = s // CHUNK
    per_step = _q_blocks_per_step(nq)

    def body(q_ref, k_ref, v_ref, t_ref, do_ref, dq_ref, dk_ref, dv_ref):
        @pl.when(pl.program_id(1) == 0)
        def _():
            dk_ref[...] = jnp.zeros_like(dk_ref)
            dv_ref[...] = jnp.zeros_like(dv_ref)

        masks = _head_masks(LANES)
        row = lax.broadcasted_iota(jnp.int32, (CHUNK, CHUNK), 0)
        col = lax.broadcasted_iota(jnp.int32, (CHUNK, CHUNK), 1)
        upto_op = _cumsum_operand(row <= col)
        before_op = _cumsum_operand(row < col)
        cmr = col - row
        zc = jnp.zeros((CHUNK, LANES), F32)

        def q_block(sub, _):
            qi = pl.program_id(1) * per_step + sub
            q_rows = pl.ds(pl.multiple_of(sub * CHUNK, CHUNK), CHUNK)
            q = q_ref[q_rows, :] * SB_SCALE
            dob = do_ref[q_rows, :].astype(BF16)
            zero = jnp.zeros_like(q)
            qs = [jnp.where(m, q, zero) for m in masks]
            dos = [jnp.where(m, dob, zero) for m in masks]
            tots = [t_ref[q_rows, 0:LANES], t_ref[q_rows, LANES:2 * LANES]]

            def blocks(js, carry):
                dq, cl0, cl1, cp0, cp1 = carry
                starts = [pl.multiple_of(jnp.minimum(j, nq - 1) * CHUNK, CHUNK) for j in js]
                valids = [cmr < (qi - j) * CHUNK for j in js]
                kvs = [(k_ref[pl.ds(k0, CHUNK), :], v_ref[pl.ds(k0, CHUNK), :]) for k0 in starts]
                units = [(h, b) for b in range(len(js)) for h in range(2)]
                zs = [_dot(qs[h], kvs[b][0], _NT) for h, b in units]
                das = [_dot(dos[h], kvs[b][1], _NT) for h, b in units]
                logits = [_sb_logits(z, valids[b]) for z, (h, b) in zip(zs, units)]
                sums = [_dot(_split_hi_lo(l1), upto_op, _NN) for l1, _ in logits]
                cls, cps = [cl0, cl1], [cp0, cp1]
                probs, gs = [], []
                for (h, b), (_, lb), sm, da in zip(units, logits, sums, das):
                    a = jnp.where(valids[b], jnp.exp(lb + (tots[h] - cls[h] - sm[:, :CHUNK])), 0.0)
                    probs.append(a)
                    gs.append(a * da)
                    cls[h] = cls[h] + sm[:, CHUNK:]
                sums_g = [_dot(_split_hi_lo(g), before_op, _NN) for g in gs]
                dzs = []
                for (h, b), (_, lb), g, sg in zip(units, logits, gs, sums_g):
                    dz = g - (g + sg[:, :CHUNK] + cps[h]) * jnp.exp(lb)
                    dzs.append(jnp.where(valids[b], dz, 0.0).astype(BF16))
                    cps[h] = cps[h] + sg[:, CHUNK:]
                for (h, b), dzb in zip(units, dzs):
                    dq = dq + _dot(dzb, jnp.where(masks[h], kvs[b][0], zero), _NN)
                for b, k0 in enumerate(starts):
                    dk_ref[pl.ds(k0, CHUNK), :] += _dot(dzs[2 * b], qs[0], _TN) + _dot(dzs[2 * b + 1], qs[1], _TN)
                    dv_ref[pl.ds(k0, CHUNK), :] += (_dot(probs[2 * b].astype(BF16), dos[0], _TN)
                                                    + _dot(probs[2 * b + 1].astype(BF16), dos[1], _TN))
                return dq, cls[0], cls[1], cps[0], cps[1]

            n_four = (qi + 1) // 4
            carry = lax.fori_loop(0, n_four, lambda jj, c: blocks([4 * jj + i for i in range(4)], c), (zc,) * 5)
            base = 4 * n_four
            carry = lax.fori_loop(0, (qi - base + 2) // 2, lambda jj, c: blocks([base + 2 * jj, base + 2 * jj + 1], c), carry)
            dq_ref[q_rows, :] = (carry[0] * SB_SCALE).astype(BF16)
            return 0

        lax.fori_loop(0, per_step, q_block, 0)

    rows = per_step * CHUNK
    blk = pl.BlockSpec((rows, LANES), lambda p, i: (i, p))
    col_blk = pl.BlockSpec((s, LANES), lambda p, i: (0, p))
    out = jax.ShapeDtypeStruct((s, D_SB), F32)
    return pl.pallas_call(
        body,
        name=name,
        grid=(N_PAIRS, nq // per_step),
        in_specs=[blk,
                  pl.BlockSpec((s, LANES), lambda p, i: (0, N_PAIRS + p)),
                  pl.BlockSpec((s, LANES), lambda p, i: (0, 2 * N_PAIRS + p)),
                  pl.BlockSpec((rows, 2 * LANES), lambda p, i: (i, p)),
                  blk],
        out_specs=[blk, col_blk, col_blk],
        out_shape=[jax.ShapeDtypeStruct((s, D_SB), BF16), out, out],
        compiler_params=_cparams(("parallel", "arbitrary")),
    )(qkv, qkv, qkv, t_tot, do)


FFN_TILE = 256
FFN_COLS = 256
N_FF_BLOCKS = D_FF // FFN_COLS


def _ffn_act_fwd(up0, conv_w, conv_b, *, name):
    s = up0.shape[0]
    nt = s // FFN_TILE

    def body(xg_ref, xv_ref, wg_ref, wv_ref, bg_ref, bv_ref, act_ref, pg_ref, pv_ref):
        pg_ref[0:FFN_HALO, :] = jnp.zeros((FFN_HALO, FFN_COLS), F32)
        pv_ref[0:FFN_HALO, :] = jnp.zeros((FFN_HALO, FFN_COLS), F32)
        pg_ref[FFN_HALO:, :] = xg_ref[...].astype(F32)
        pv_ref[FFN_HALO:, :] = xv_ref[...].astype(F32)

        def tile(i, c):
            t0 = pl.multiple_of(i * FFN_TILE, FFN_TILE)
            outs = []
            for p_ref, w_ref, b_ref in ((pg_ref, wg_ref, bg_ref), (pv_ref, wv_ref, bv_ref)):
                window = p_ref[pl.ds(t0, FFN_TILE + FFN_HALO), :]
                acc = b_ref[...] + w_ref[2:3, :] * window[FFN_HALO:, :]
                for j in range(1, FFN_K):
                    acc = acc + w_ref[FFN_K - 1 - j:FFN_K - j, :] * _shift_down(window, j, FFN_HALO)
                outs.append(acc)
            gate, val = outs
            act_ref[pl.ds(t0, FFN_TILE), :] = (gate * _sigmoid(gate) * val).astype(BF16)
            return c

        lax.fori_loop(0, nt, tile, 0)

    gcol = lambda rows: pl.BlockSpec((rows, FFN_COLS), lambda j: (0, j))
    vcol = lambda rows: pl.BlockSpec((rows, FFN_COLS), lambda j: (0, j + N_FF_BLOCKS))
    return pl.pallas_call(
        body,
        name=name,
        grid=(N_FF_BLOCKS,),
        in_specs=[gcol(s), vcol(s), gcol(FFN_K), vcol(FFN_K), gcol(1), vcol(1)],
        out_specs=gcol(s),
        out_shape=jax.ShapeDtypeStruct((s, D_FF), BF16),
        scratch_shapes=[pltpu.VMEM((s + FFN_HALO, FFN_COLS), F32), pltpu.VMEM((s + FFN_HALO, FFN_COLS), F32)],
        compiler_params=_cparams(("parallel",)),
    )(up0, up0, conv_w, conv_w, conv_b, conv_b)


def _ffn_act_bwd(up0, dact, conv_w, conv_b, *, name):
    s = up0.shape[0]
    nt = s // FFN_TILE

    def body(xg_ref, xv_ref, da_ref, wg_ref, wv_ref, bg_ref, bv_ref, dxg_ref, dxv_ref, dwg_ref, dwv_ref, dbg_ref, dbv_ref,
             pg_ref, pv_ref, dg_ref, dv_ref):
        zeros = jnp.zeros((FFN_HALO, FFN_COLS), F32)
        for p_ref, x_ref in ((pg_ref, xg_ref), (pv_ref, xv_ref)):
            p_ref[0:FFN_HALO, :] = zeros
            p_ref[FFN_HALO:, :] = x_ref[...].astype(F32)
        dg_ref[s:s + FFN_HALO, :] = zeros
        dv_ref[s:s + FFN_HALO, :] = zeros
        for ref in (dwg_ref, dwv_ref, dbg_ref, dbv_ref):
            ref[...] = jnp.zeros_like(ref)

        def conv(p_ref, w_ref, b_ref, t0):
            window = p_ref[pl.ds(t0, FFN_TILE + FFN_HALO), :]
            taps = [_shift_down(window, j, FFN_HALO) for j in range(FFN_K)]
            out = b_ref[...]
            for j in range(FFN_K):
                out = out + w_ref[FFN_K - 1 - j:FFN_K - j, :] * taps[j]
            return out, taps

        def tile(i, c):
            t0 = pl.multiple_of(i * FFN_TILE, FFN_TILE)
            gate, taps_g = conv(pg_ref, wg_ref, bg_ref, t0)
            val, taps_v = conv(pv_ref, wv_ref, bv_ref, t0)
            da = da_ref[pl.ds(t0, FFN_TILE), :].astype(F32)
            sg = lax.logistic(gate)
            dgate = da * val * (sg * (1.0 + gate * (1.0 - sg)))
            dval = da * gate * sg
            dg_ref[pl.ds(t0, FFN_TILE), :] = dgate
            dv_ref[pl.ds(t0, FFN_TILE), :] = dval
            dbg_ref[...] += jnp.sum(dgate, axis=0, keepdims=True)
            dbv_ref[...] += jnp.sum(dval, axis=0, keepdims=True)
            for j in range(FFN_K):
                dwg_ref[FFN_K - 1 - j:FFN_K - j, :] += jnp.sum(dgate * taps_g[j], axis=0, keepdims=True)
                dwv_ref[FFN_K - 1 - j:FFN_K - j, :] += jnp.sum(dval * taps_v[j], axis=0, keepdims=True)
            return c

        lax.fori_loop(0, nt, tile, 0)

        def tile_dx(i, c):
            t0 = pl.multiple_of(i * FFN_TILE, FFN_TILE)
            for d_ref, w_ref, dx_ref in ((dg_ref, wg_ref, dxg_ref), (dv_ref, wv_ref, dxv_ref)):
                window = d_ref[pl.ds(t0, FFN_TILE + FFN_HALO), :]
                dx = w_ref[FFN_K - 1:FFN_K, :] * window[:FFN_TILE, :]
                for j in range(1, FFN_K):
                    dx = dx + w_ref[FFN_K - 1 - j:FFN_K - j, :] * _shift_up(window, j, FFN_TILE)
                dx_ref[pl.ds(t0, FFN_TILE), :] = dx.astype(BF16)
            return c

        lax.fori_loop(0, nt, tile_dx, 0)

    gcol = lambda rows: pl.BlockSpec((rows, FFN_COLS), lambda j: (0, j))
    vcol = lambda rows: pl.BlockSpec((rows, FFN_COLS), lambda j: (0, j + N_FF_BLOCKS))
    half = lambda rows, dtype: jax.ShapeDtypeStruct((rows, D_FF), dtype)
    padded = pltpu.VMEM((s + FFN_HALO, FFN_COLS), F32)
    return pl.pallas_call(
        body,
        name=name,
        grid=(N_FF_BLOCKS,),
        in_specs=[gcol(s), vcol(s), gcol(s), gcol(FFN_K), vcol(FFN_K), gcol(1), vcol(1)],
        out_specs=[gcol(s), gcol(s), gcol(FFN_K), gcol(FFN_K), gcol(1), gcol(1)],
        out_shape=[half(s, BF16), half(s, BF16), half(FFN_K, F32), half(FFN_K, F32), half(1, F32), half(1, F32)],
        scratch_shapes=[padded, padded, padded, padded],
        compiler_params=_cparams(("parallel",)),
    )(up0, up0, dact, conv_w, conv_w, conv_b, conv_b)


MESH = pl.DeviceIdType.MESH


def _position():
    x, y, c = lax.axis_index("x"), lax.axis_index("y"), lax.axis_index("c")
    return x, y, c, 4 * x + 2 * y + c


def _peer(k):
    x, y, c, _ = _position()
    px = 1 - x if k & 4 else x
    py = 1 - y if k & 2 else y
    pc = 1 - c if k & 1 else c
    return (px, py, pc), 4 * px + 2 * py + pc


def _gather_blocks(src, *, name):
    def body(src_ref, out_ref, send_sems, recv_sems, local_sem):
        me = _position()[3]
        local = pltpu.make_async_copy(src_ref, out_ref.at[me], local_sem)
        local.start()
        sends, recvs = [], []
        for k in range(1, N_DEV):
            peer, pidx = _peer(k)
            sems = dict(send_sem=send_sems.at[k - 1], recv_sem=recv_sems.at[k - 1], device_id=peer, device_id_type=MESH)
            sends.append(pltpu.make_async_remote_copy(src_ref=src_ref, dst_ref=out_ref.at[me], **sems))
            recvs.append(pltpu.make_async_remote_copy(src_ref=src_ref, dst_ref=out_ref.at[pidx], **sems))
        for cp in sends:
            cp.start()
        for cp in recvs:
            cp.wait_recv()
        for cp in sends:
            cp.wait_send()
        local.wait()

    return pl.pallas_call(
        body,
        name=name,
        in_specs=[pl.BlockSpec(memory_space=pl.ANY)],
        out_specs=pl.BlockSpec(memory_space=pl.ANY),
        out_shape=jax.ShapeDtypeStruct((N_DEV,) + src.shape, src.dtype),
        scratch_shapes=[pltpu.SemaphoreType.DMA((N_DEV - 1,)), pltpu.SemaphoreType.DMA((N_DEV - 1,)),
                        pltpu.SemaphoreType.DMA],
    )(src)


_HBM = pl.BlockSpec(memory_space=pltpu.HBM)
_SEM = pl.BlockSpec(memory_space=pltpu.SEMAPHORE)
_DATAFLOW = pltpu.SideEffectType.DATAFLOW_SIDE_EFFECTING
N_PEERS = N_DEV - 1


class _SplitExchange:
    def __init__(self, src, *, kind, name):
        self.kind, self.name, self.dtype = kind, name, src.dtype
        scatter = kind.startswith("scatter")
        by_blocks = kind == "scatter_blocks"
        self.scatter, self.by_blocks = scatter, by_blocks
        if by_blocks:
            self.r, self.cols, self.land_shape = None, None, src.shape
        else:
            self.r = src.shape[0] // N_DEV if scatter else src.shape[0]
            self.cols = src.shape[1]
            self.land_shape = (N_DEV, self.r, self.cols) if scatter else (N_DEV * self.r, self.cols)
        r = self.r

        def copies(src_ref, land_ref, send_sems, recv_sems, local_sem):
            me = _position()[3]

            def rows(ref, idx):
                return ref.at[pl.ds(pl.multiple_of(idx * r, r), r), :]

            if by_blocks:
                outgoing = lambda idx: src_ref.at[idx]
            else:
                outgoing = (lambda idx: rows(src_ref, idx)) if scatter else (lambda idx: src_ref)
            slot = (lambda idx: land_ref.at[idx]) if scatter else (lambda idx: rows(land_ref, idx))
            sends, recvs = [], []
            for k in range(1, N_DEV):
                peer, pidx = _peer(k)
                sems = dict(send_sem=send_sems[k - 1], recv_sem=recv_sems[k - 1], device_id=peer, device_id_type=MESH)
                sends.append(pltpu.make_async_remote_copy(src_ref=outgoing(pidx), dst_ref=slot(me), **sems))
                recvs.append(pltpu.make_async_remote_copy(src_ref=outgoing(pidx), dst_ref=slot(pidx), **sems))
            return sends, recvs, pltpu.make_async_copy(outgoing(me), slot(me), local_sem)

        self._copies = copies
        self.src = src

    @staticmethod
    def start(exchanges, name):
        n = len(exchanges)
        per = 2 * N_PEERS + 1

        def start_body(*refs):
            outs = refs[2 * n:]
            for i, ex in enumerate(exchanges):
                sems = outs[per * i:per * (i + 1)]
                sends, _, local = ex._copies(refs[2 * i], refs[2 * i + 1], sems[:N_PEERS], sems[N_PEERS:-1], sems[-1])
                for cp in sends + [local]:
                    cp.start()
            outs[-1][...] = jnp.zeros_like(outs[-1])

        sem = pltpu.SemaphoreType.DMA(())
        operands, thru_shapes = [], []
        for ex in exchanges:
            operands += [pltpu.with_memory_space_constraint(ex.src, pltpu.HBM),
                         pltpu.with_memory_space_constraint(lax.empty(ex.land_shape, ex.dtype), pltpu.HBM)]
            thru_shapes += [pltpu.HBM(ex.src.shape, ex.dtype), pltpu.HBM(ex.land_shape, ex.dtype)]
        out = pl.pallas_call(
            start_body,
            name=name,
            in_specs=(_HBM,) * (2 * n),
            out_specs=(_SEM,) * (per * n) + (_HBM,) * (2 * n) + (pl.BlockSpec(memory_space=pltpu.VMEM),),
            out_shape=(sem,) * (per * n) + tuple(thru_shapes) + (jax.ShapeDtypeStruct((8, LANES), F32),),
            input_output_aliases={i: per * n + i for i in range(2 * n)},
            compiler_params=pltpu.CompilerParams(has_side_effects=_DATAFLOW),
        )(*operands)
        for i, ex in enumerate(exchanges):
            ex.sems = out[per * i:per * (i + 1)]
            ex.src_thru, ex.land_thru = out[per * n + 2 * i], out[per * n + 2 * i + 1]
        return out[-1][0, 0]

    def finish(self, after):
        copies = self._copies

        def wait_body(src_ref, land_ref, *rest):
            sends, recvs, local = copies(src_ref, land_ref, rest[:N_PEERS], rest[N_PEERS:2 * N_PEERS], rest[2 * N_PEERS])
            for cp in sends:
                cp.wait_send()
            for cp in recvs:
                cp.wait_recv()
            local.wait()

        return pl.pallas_call(
            wait_body,
            name=f"{self.name}_wait",
            in_specs=(_HBM, _HBM) + (_SEM,) * (2 * N_PEERS + 1) + (pl.BlockSpec(memory_space=pl.ANY),),
            out_specs=(_HBM, _HBM),
            out_shape=(pltpu.HBM(self.src_thru.shape, self.dtype), pltpu.HBM(self.land_shape, self.dtype)),
            input_output_aliases={0: 0, 1: 1},
            compiler_params=pltpu.CompilerParams(has_side_effects=_DATAFLOW),
        )(self.src_thru, self.land_thru, *self.sems, after)[1]


def _row_tile(rows):
    return _tile(rows, (256, 128, 64, 32, 16, 8))


def _layer_parts_specs(n_layers, n_parts, tr, cols):
    return [pl.BlockSpec((n_parts, tr, cols), lambda l, i, j=j: (0, jnp.where(l == j, i, 0), 0)) for j in range(n_layers)]


def _select_layer_sum(p_refs):
    l = pl.program_id(0)
    g = None
    for j, p_ref in enumerate(p_refs):
        gj = p_ref[0].astype(F32)
        for k in range(1, p_ref.shape[0]):
            gj = gj + p_ref[k].astype(F32)
        g = gj if g is None else jnp.where(l == j, gj, g)
    return g


def _sum_parts(parts, *, name):
    n_layers = len(parts)
    n_parts, rows, cols = parts[0].shape
    tr = _row_tile(rows)

    def body(*refs):
        refs[-1][...] = _select_layer_sum(refs[:n_layers])

    return pl.pallas_call(
        body,
        name=name,
        grid=(n_layers, rows // tr),
        in_specs=_layer_parts_specs(n_layers, n_parts, tr, cols),
        out_specs=pl.BlockSpec((None, tr, cols), lambda l, i: (l, i, 0)),
        out_shape=jax.ShapeDtypeStruct((n_layers, rows, cols), F32),
        compiler_params=_cparams(("arbitrary", "arbitrary")),
    )(*parts)


def _adamw(parts, w, m, v, *, name):
    n_layers, rows, cols = w.shape
    summed = not isinstance(parts, (list, tuple))
    tr = _row_tile(rows)
    n_in = 1 if summed else n_layers

    def body(*refs):
        w_ref, m_ref, v_ref, g_ref, d_ref, m2_ref, v2_ref = refs[n_in:]
        g = refs[0][...] if summed else _select_layer_sum(refs[:n_in])
        m2 = ADAM_B1 * m_ref[...] + (1.0 - ADAM_B1) * g
        v2 = ADAM_B2 * v_ref[...] + (1.0 - ADAM_B2) * (g * g)
        m_hat = m2 / (1.0 - ADAM_B1 ** ADAM_STEP)
        v_hat = v2 / (1.0 - ADAM_B2 ** ADAM_STEP)
        g_ref[...] = g
        d_ref[...] = -ADAM_LR * (m_hat / (jnp.sqrt(v_hat) + ADAM_EPS) + ADAM_WD * w_ref[...])
        m2_ref[...] = m2
        v2_ref[...] = v2

    slab = pl.BlockSpec((None, tr, cols), lambda l, i: (l, i, 0))
    out = jax.ShapeDtypeStruct((n_layers, rows, cols), F32)
    p_specs = [slab] if summed else _layer_parts_specs(n_layers, parts[0].shape[0], tr, cols)
    return pl.pallas_call(
        body,
        name=name,
        grid=(n_layers, rows // tr),
        in_specs=p_specs + [slab, slab, slab],
        out_specs=[slab, slab, slab, slab],
        out_shape=[out, out, out, out],
        compiler_params=_cparams(("arbitrary", "arbitrary")),
    )(*((parts,) if summed else tuple(parts)), w, m, v)


SLAB_ROWS = 256
_SMALL_SHARDED = (("conv_w", (2, 31, 32)), ("ffn_conv_w", (2, 3, 704)))
_REPLICATED = (("g_mix", (2, 1024)), ("conv_b", (2, 256)), ("conv_ln_g", (2, 256)), ("conv_ln_b", (2, 256)),
               ("sgu_ln_g", (2, 256)), ("sgu_ln_b", (2, 256)), ("sgu_w", (2, 4, 128, 128)), ("sgu_b", (2, 4, 128)),
               ("g_out", (2, 1024)), ("g_ffn", (2, 1024)), ("ffn_conv_b", (2, 5632)), ("g_final", (1024,)))


def _seg_rows(n_elems):
    return -(-n_elems // LANES)


def _pack(arrays, lead=()):
    segs = []
    for a in arrays:
        flat = a.reshape(lead + (-1,)).astype(F32)
        pad = _seg_rows(flat.shape[-1]) * LANES - flat.shape[-1]
        if pad:
            flat = jnp.pad(flat, [(0, 0)] * len(lead) + [(0, pad)])
        segs.append(flat)
    flat = jnp.concatenate(segs, axis=-1)
    rows = flat.shape[-1] // LANES
    pad_rows = -rows % SLAB_ROWS
    if pad_rows:
        flat = jnp.pad(flat, [(0, 0)] * len(lead) + [(0, pad_rows * LANES)])
    return flat.reshape(lead + (rows + pad_rows, LANES))


def _unpack(slab, shapes, lead=()):
    flat = slab.reshape(lead + (-1,))
    out, off = [], 0
    for shape in shapes:
        n = math.prod(shape)
        out.append(flat[..., off:off + n].reshape(lead + tuple(shape)))
        off += _seg_rows(n) * LANES
    return out


def _split_last(full):
    split = full.shape[:-1] + (N_DEV, full.shape[-1] // N_DEV)
    return jnp.moveaxis(full.reshape(split), -2, 0)


def _join_last(blocks):
    moved = jnp.moveaxis(blocks, 0, -2)
    return moved.reshape(moved.shape[:-2] + (moved.shape[-2] * moved.shape[-1],))


def _gathered(wt, n, l, after):
    if isinstance(wt[n][l], _SplitExchange):
        wt[n][l] = wt[n][l].finish(after)
    return wt[n][l]


def _layer_fwd(l, x, wt, small):
    tag = f"l{l}"
    h = _rmsnorm_fwd(x, small["g_mix"][l][None], name=f"{tag}_norm_mix")
    w_in_t = _gathered(wt, "w_in_t", l, h)
    p_ab = _matmul(h, w_in_t, "nt", name=f"{tag}_proj_ab", n=D_AB)
    qkv = _matmul(h, w_in_t, "nt", name=f"{tag}_proj_qkv", n=D_QKV, b_n0=D_AB, out_dtype=BF16)
    ya = _mixer_a_fwd(p_ab, wt["conv_w"][l], small["conv_b"][l][None], small["conv_ln_g"][l][None],
                      small["conv_ln_b"][l][None], name=f"{tag}_mixer_a")
    bias = jnp.repeat(small["sgu_b"][l].T, HEAD_DIM, axis=1)
    yb = _sgu_fwd(p_ab, small["sgu_ln_g"][l][None], small["sgu_ln_b"][l][None], small["sgu_w"][l], bias,
                  name=f"{tag}_sgu")
    yc, t_tot = _attn_fwd(qkv, name=f"{tag}_attn")
    y = _combine_fwd(ya, yb, yc, small["g_out"][l][None], name=f"{tag}_combine")
    x1 = _matmul(y, _gathered(wt, "w_out", l, y), "nn", name=f"{tag}_out_proj", residual=x)
    h2 = _rmsnorm_fwd(x1, small["g_ffn"][l][None], name=f"{tag}_norm_ffn")
    up0 = _matmul(h2, _gathered(wt, "w_up_t", l, h2), "nt", name=f"{tag}_up", out_dtype=BF16)
    act = _ffn_act_fwd(up0, wt["ffn_conv_w"][l], small["ffn_conv_b"][l][None], name=f"{tag}_ffn_act")
    x2 = _matmul(act, _gathered(wt, "w_down", l, act), "nn", name=f"{tag}_down", residual=x1)
    saved = dict(x=x, h=h, p_ab=p_ab, qkv=qkv, ya=ya, yb=yb, yc=yc, t_tot=t_tot, y=y, x1=x1, h2=h2, up0=up0,
                 act=act, bias=bias)
    return x2, saved


def _layer_bwd(l, dres, sv, wt, small, scattering):
    tag = f"l{l}b"
    g = {}

    def scatter(n, partial):
        scattering[n][l] = _SplitExchange(partial, kind="scatter_rows", name=f"scatter_{n}_l{l}")
        return _SplitExchange.start([scattering[n][l]], name=f"scatter_{n}_l{l}_start")

    dx2, dx2_b = dres
    dact = _matmul(dx2_b, wt["w_down"][l], "nt", name=f"{tag}_dact", out_dtype=BF16)
    tok = scatter("w_down", _matmul(sv["act"], dx2_b, "tn", name=f"{tag}_dw_down", out_dtype=BF16))
    dup_g, dup_v, dwg, dwv, dbg, dbv = _ffn_act_bwd(sv["up0"], dact, wt["ffn_conv_w"][l], small["ffn_conv_b"][l][None] + tok,
                                                    name=f"{tag}_ffn_act")
    g["ffn_conv_w"] = jnp.concatenate([dwg, dwv], axis=1)
    g["ffn_conv_b"] = jnp.concatenate([dbg[0], dbv[0]])
    dh2 = _matmul(dup_g, wt["w_up_t"][l], "nn", name=f"{tag}_dh2_gate")
    dh2 = _matmul(dup_v, wt["w_up_t"][l], "nn", name=f"{tag}_dh2_val", b_k0=D_FF, residual=dh2)
    dw_up = _matmul(dup_g, sv["h2"], "tn", name=f"{tag}_dw_up_gate", out_dtype=BF16, rows=(2 * D_FF, 0))
    dw_up = _matmul(dup_v, sv["h2"], "tn", name=f"{tag}_dw_up_val", out_dtype=BF16, rows=(2 * D_FF, D_FF), into=dw_up)
    tok = scatter("w_up_t", dw_up)
    dx1, dx1_b, dg = _rmsnorm_bwd(sv["x1"], small["g_ffn"][l][None] + tok, dh2, dx2, name=f"{tag}_norm_ffn")
    g["g_ffn"] = dg[0]
    dy = _matmul(dx1_b, wt["w_out"][l], "nt", name=f"{tag}_dy")
    tok = scatter("w_out", _matmul(sv["y"], dx1_b, "tn", name=f"{tag}_dw_out", out_dtype=BF16))
    dya, dyb, dyc, dg = _combine_bwd(dy, sv["ya"], sv["yb"], sv["yc"], small["g_out"][l][None] + tok,
                                     name=f"{tag}_combine")
    g["g_out"] = dg[0]
    dq, dk, dv = _attn_bwd(sv["qkv"], sv["t_tot"], dyc, name=f"{tag}_attn")
    dp_b, g["sgu_w"], db, dg, dbeta = _sgu_bwd(sv["p_ab"], dyb, small["sgu_ln_g"][l][None], small["sgu_ln_b"][l][None],
                                               small["sgu_w"][l], sv["bias"], name=f"{tag}_sgu")
    g["sgu_b"] = db[:, :N_SGU_HEADS].T
    g["sgu_ln_g"], g["sgu_ln_b"] = dg[0], dbeta[0]
    dp_a, g["conv_w"], dcb, dg, dbeta = _mixer_a_bwd(sv["p_ab"], dya, wt["conv_w"][l], small["conv_b"][l][None],
                                                     small["conv_ln_g"][l][None], small["conv_ln_b"][l][None],
                                                     name=f"{tag}_mixer_a")
    g["conv_b"], g["conv_ln_g"], g["conv_ln_b"] = dcb[0], dg[0], dbeta[0]
    dp = jnp.concatenate([dp_a, dp_b, dq, dk.astype(BF16), dv.astype(BF16)], axis=1)
    dh = _matmul(dp, wt["w_in_t"][l], "nn", name=f"{tag}_dh")
    tok = scatter("w_in_t", _matmul(dp, sv["h"], "tn", name=f"{tag}_dw_in", out_dtype=BF16))
    dx, dx_b, dg = _rmsnorm_bwd(sv["x"], small["g_mix"][l][None] + tok, dh, dx1, name=f"{tag}_norm_mix")
    g["g_mix"] = dg[0]
    return (dx, dx_b), g


_BIG = ("w_in_t", "w_out", "w_up_t", "w_down")


def kernel(x, g_mix, w_in, conv_w, conv_b, conv_ln_g, conv_ln_b, sgu_ln_g, sgu_ln_b, sgu_w, sgu_b, g_out, w_out, g_ffn, w_up, ffn_conv_w, ffn_conv_b, w_down, g_final, loss_target, m_g_mix, m_w_in, m_conv_w, m_conv_b, m_conv_ln_g, m_conv_ln_b, m_sgu_ln_g, m_sgu_ln_b, m_sgu_w, m_sgu_b, m_g_out, m_w_out, m_g_ffn, m_w_up, m_ffn_conv_w, m_ffn_conv_b, m_w_down, m_g_final, v_g_mix, v_w_in, v_conv_w, v_conv_b, v_conv_ln_g, v_conv_ln_b, v_sgu_ln_g, v_sgu_ln_b, v_sgu_w, v_sgu_b, v_g_out, v_w_out, v_g_ffn, v_w_up, v_ffn_conv_w, v_ffn_conv_b, v_w_down, v_g_final):
    given = dict(locals())
    n_layers = g_mix.shape[0]
    layers = range(n_layers)
    small_sharded = [n for n, _ in _SMALL_SHARDED]
    replicated = [n for n, _ in _REPLICATED]
    small = {n: given[n] for n in replicated}

    filters = _gather_blocks(_pack([given[n] for n in small_sharded]), name="gather_filters")
    filters, first_shard = lax.optimization_barrier((filters, w_in[0].T.astype(BF16)))
    wt = {n: [None] * n_layers for n in _BIG}
    wt["w_in_t"][0] = _SplitExchange(first_shard, kind="gather_rows", name="gather_w_in_t_l0")
    tok = _SplitExchange.start([wt["w_in_t"][0]], name="gather_w_in_t_l0_start")
    w_in, w_out, w_up, w_down, tok = lax.optimization_barrier((w_in, w_out, w_up, w_down, tok))
    shard = {"w_in_t": [w_in[l].T.astype(BF16) for l in layers], "w_out": [w_out[l].astype(BF16) for l in layers],
             "w_up_t": [w_up[l].T.astype(BF16) for l in layers], "w_down": [w_down[l].astype(BF16) for l in layers]}
    later = [(n, l) for l in layers for n in _BIG if (n, l) != ("w_in_t", 0)]
    for n, l in later:
        wt[n][l] = _SplitExchange(shard[n][l], kind="gather_rows", name=f"gather_{n}_l{l}")
    small["g_mix"] = g_mix + tok + _SplitExchange.start([wt[n][l] for n, l in later], name="gather_weights_start")
    for n, blocks in zip(small_sharded, _unpack(filters, [s for _, s in _SMALL_SHARDED], lead=(N_DEV,))):
        wt[n] = _join_last(blocks)

    xs = x[0]
    saved = []
    for l in layers:
        xs, sv = _layer_fwd(l, xs, wt, small)
        saved.append(sv)
    loss_tile, dx, dx_b, dgf = _loss_head(xs, g_final[None], loss_target[0], name="loss_head")
    dres = (dx, dx_b)
    scattering = {n: [None] * n_layers for n in _BIG}
    layered = [n for n in replicated if n != "g_final"]
    slabs = [None] * n_layers
    for l in reversed(layers):
        dres, g = _layer_bwd(l, dres, saved[l], wt, small, scattering)
        own = _pack([_split_last(g[n]) for n in small_sharded], lead=(N_DEV,))
        shared = _pack([g[n] for n in layered] + ([dgf[0]] if l == n_layers - 1 else []))
        slab = jnp.concatenate([own, jnp.broadcast_to(shared[None], (N_DEV,) + shared.shape)], axis=1)
        slabs[l] = _SplitExchange(slab, kind="scatter_blocks", name=f"scatter_small_grads_l{l}")
        tok = _SplitExchange.start([slabs[l]], name=f"scatter_small_grads_l{l}_start")
        dres, tok = lax.optimization_barrier((dres, tok))
    n_own = own.shape[1]

    received = {n: [scattering[n][l].finish(dres[0]) for l in layers] for n in _BIG}
    out = {}

    def update(n, parts):
        results = _adamw(parts, given[n], given["m_" + n], given["v_" + n], name=f"adamw_{n}")
        for pre, res in zip(("grad_", "delta_", "new_m_", "new_v_"), results):
            out[pre + n] = res

    update("w_out", received["w_out"])
    update("w_down", received["w_down"])
    for n in ("w_in", "w_up"):
        update(n, jnp.swapaxes(_sum_parts(received[n + "_t"], name=f"sum_{n}"), 1, 2))

    per_layer = {pre + n: [None] * n_layers for pre in ("grad_", "delta_", "new_m_", "new_v_") for n in small_sharded + layered}
    for l in reversed(layers):
        last = l == n_layers - 1
        stacks = [jnp.concatenate([_pack([given[pre + n][l] for n in small_sharded]),
                                   _pack([given[pre + n][l] for n in layered] + ([given[pre + "g_final"]] if last else []))])[None]
                  for pre in ("", "m_", "v_")]
        results = _adamw([slabs[l].finish(out["grad_w_up"])], *stacks, name=f"adamw_small_l{l}")
        for pre, res in zip(("grad_", "delta_", "new_m_", "new_v_"), results):
            unpacked = (_unpack(res[0, :n_own], [s[1:] for _, s in _SMALL_SHARDED])
                        + _unpack(res[0, n_own:], [s[1:] for n, s in _REPLICATED if n != "g_final"] + ([g_final.shape] if last else [])))
            for n, a in zip(small_sharded + layered + (["g_final"] if last else []), unpacked):
                if n == "g_final":
                    out[pre + n] = a
                else:
                    per_layer[pre + n][l] = a
    for name, parts in per_layer.items():
        out[name] = jnp.stack(parts)

    loss = lax.psum(loss_tile[0, 0], ("x", "y", "c"))
    order = list(_WEIGHT_ORDER)
    return (loss, dres[0][None], *[out["grad_" + n] for n in order], *[out["delta_" + n] for n in order],
            *[out["new_m_" + n] for n in order], *[out["new_v_" + n] for n in order])


_WEIGHT_ORDER = ("g_mix", "w_in", "conv_w", "conv_b", "conv_ln_g", "conv_ln_b", "sgu_ln_g", "sgu_ln_b", "sgu_w", "sgu_b",
                 "g_out", "w_out", "g_ffn", "w_up", "ffn_conv_w", "ffn_conv_b", "w_down", "g_final")
```

```python
import math

import jax
import jax.numpy as jnp
from jax import lax
from jax.experimental import pallas as pl
from jax.experimental.pallas import tpu as pltpu

F32 = jnp.float32
BF16 = jnp.bfloat16

N_DEV = 8
D_MODEL = 1024
HEAD_DIM = 64
D_CONV = 256
D_SGU = 256
D_SB = 512
D_AB = 2 * D_CONV + 2 * D_SGU
D_QKV = 3 * D_SB
D_IN = D_AB + D_QKV
CONV_K = 31
CONV_HALO = 32
FFN_K = 3
FFN_HALO = 8
D_FF = 2816
CHUNK = 128
EPS = 1e-6
LANES = 128

ADAM_LR = 0.001
ADAM_B1 = 0.9
ADAM_B2 = 0.999
ADAM_EPS = 1e-08
ADAM_WD = 0.01
ADAM_STEP = 10

VMEM_LIMIT = 56 * 1024 * 1024


def _cparams(sem=None):
    return pltpu.CompilerParams(dimension_semantics=sem, vmem_limit_bytes=VMEM_LIMIT)


def _tile(n, prefs=(512, 256, 128)):
    for t in prefs:
        if n % t == 0:
            return t
    return n


def _sigmoid(x):
    return 1.0 / (1.0 + jnp.exp(-x))


_INV_SQRT2 = 1.0 / math.sqrt(2.0)
_INV_SQRT2PI = 1.0 / math.sqrt(2.0 * math.pi)


def _gelu(x):
    return 0.5 * x * (1.0 + lax.erf(x * _INV_SQRT2))


def _gelu_grad(x):
    return 0.5 * (1.0 + lax.erf(x * _INV_SQRT2)) + x * jnp.exp(-0.5 * x * x) * _INV_SQRT2PI


def _dot(a, b, dims):
    return lax.dot_general(a, b, (dims, ((), ())), preferred_element_type=F32)


_NN = ((1,), (0,))
_NT = ((1,), (1,))
_TN = ((0,), (0,))


def _split_bf16(x):
    hi = x.astype(BF16)
    lo = (x - hi.astype(F32)).astype(BF16)
    return jnp.concatenate([hi, lo], axis=1)


def _matmul(a, b, mode, *, name, out_dtype=F32, residual=None, n=None, b_n0=0, b_k0=0, rows=None, into=None):
    if mode == "nn":
        (m, k), n = a.shape, (n or b.shape[1])
    elif mode == "nt":
        (m, k), n = a.shape, (n or b.shape[0])
    else:
        (k, m), n = a.shape, b.shape[1]
    has_res = residual is not None
    tm, tn = _matmul_tiles(m, n, k, a.dtype.itemsize, b.dtype.itemsize, jnp.dtype(out_dtype).itemsize, has_res, b_n0)
    j0 = b_n0 // tn
    total_rows, first_row = rows or (m, 0)
    assert b_k0 % k == 0 and first_row % tm == 0
    kb, i0 = b_k0 // k, first_row // tm

    if mode == "nn":
        a_spec = pl.BlockSpec((tm, k), lambda i, j: (i, 0))
        b_spec = pl.BlockSpec((k, tn), lambda i, j: (kb, j + j0))
        dims = _NN
    elif mode == "nt":
        a_spec = pl.BlockSpec((tm, k), lambda i, j: (i, 0))
        b_spec = pl.BlockSpec((tn, k), lambda i, j: (j + j0, 0))
        dims = _NT
    else:
        a_spec = pl.BlockSpec((k, tm), lambda i, j: (0, i))
        b_spec = pl.BlockSpec((k, tn), lambda i, j: (0, j))
        dims = _TN
    o_spec = pl.BlockSpec((tm, tn), lambda i, j: (i + i0, j))
    r_spec = pl.BlockSpec((tm, tn), lambda i, j: (i, j))

    def body(*refs):
        a_ref, b_ref = refs[:2]
        acc = _dot(a_ref[...].astype(BF16), b_ref[...].astype(BF16), dims)
        if has_res:
            acc = acc + refs[2][...]
        refs[-1][...] = acc.astype(out_dtype)

    in_specs = [a_spec, b_spec] + ([r_spec] if has_res else [])
    args = (a, b) + ((residual,) if has_res else ())
    aliases = {}
    if into is not None:
        aliases = {len(args): 0}
        in_specs.append(pl.BlockSpec(memory_space=pl.ANY))
        args += (into,)

        def body(*refs, inner=body):
            inner(*refs[:len(args) - 1], refs[-1])

    return pl.pallas_call(
        body,
        name=name,
        grid=(m // tm, n // tn),
        in_specs=in_specs,
        out_specs=o_spec,
        out_shape=jax.ShapeDtypeStruct((total_rows, n), out_dtype),
        input_output_aliases=aliases,
        compiler_params=_cparams(("parallel", "parallel")),
    )(*args)


MATMUL_VMEM_BUDGET = 40 * 1024 * 1024


def _matmul_tiles(m, n, k, a_bytes, b_bytes, out_bytes, has_res, n_offset):
    def divisors(size, cap, also=0):
        return [t for t in range(cap, 0, -LANES) if size % t == 0 and also % t == 0] or [size]

    for tm in divisors(m, 1024):
        for tn in divisors(n, 1408, n_offset):
            blocks = tm * k * a_bytes + k * tn * b_bytes + tm * tn * (out_bytes + (4 if has_res else 0))
            if 2 * blocks <= MATMUL_VMEM_BUDGET:
                return tm, tn
    raise ValueError(f"no matmul tiling for {m} x {n} x {k}")


ROW_TILE = 256


def _rmsnorm_fwd(x, g, *, name):
    s, d = x.shape

    def body(x_ref, g_ref, h_ref):
        xv = x_ref[...]
        r = lax.rsqrt(jnp.mean(xv * xv, axis=-1, keepdims=True) + EPS)
        h_ref[...] = (xv * r * g_ref[...]).astype(BF16)

    return pl.pallas_call(
        body,
        name=name,
        grid=(s // ROW_TILE,),
        in_specs=[pl.BlockSpec((ROW_TILE, d), lambda i: (i, 0)), pl.BlockSpec((1, d), lambda i: (0, 0))],
        out_specs=pl.BlockSpec((ROW_TILE, d), lambda i: (i, 0)),
        out_shape=jax.ShapeDtypeStruct((s, d), BF16),
        compiler_params=_cparams(("parallel",)),
    )(x, g)


def _rmsnorm_bwd(x, g, dh, dres, *, name):
    s, d = x.shape

    def body(x_ref, g_ref, dh_ref, dres_ref, dx_ref, dxb_ref, dg_ref):
        xv = x_ref[...]
        r = lax.rsqrt(jnp.mean(xv * xv, axis=-1, keepdims=True) + EPS)
        xhat = xv * r
        dhv = dh_ref[...]
        dxhat = dhv * g_ref[...]
        dx = dres_ref[...] + r * (dxhat - xhat * jnp.mean(dxhat * xhat, axis=-1, keepdims=True))
        dx_ref[...] = dx
        dxb_ref[...] = dx.astype(BF16)
        part = jnp.sum(dhv * xhat, axis=0, keepdims=True)

        @pl.when(pl.program_id(0) == 0)
        def _():
            dg_ref[...] = part

        @pl.when(pl.program_id(0) > 0)
        def _():
            dg_ref[...] += part

    row = pl.BlockSpec((ROW_TILE, d), lambda i: (i, 0))
    vec = pl.BlockSpec((1, d), lambda i: (0, 0))
    return pl.pallas_call(
        body,
        name=name,
        grid=(s // ROW_TILE,),
        in_specs=[row, vec, row, row],
        out_specs=[row, row, vec],
        out_shape=[jax.ShapeDtypeStruct((s, d), F32), jax.ShapeDtypeStruct((s, d), BF16),
                   jax.ShapeDtypeStruct((1, d), F32)],
        compiler_params=_cparams(("arbitrary",)),
    )(x, g, dh, dres)


def _loss_head(x, g, target, *, name):
    s, d = x.shape

    def body(x_ref, g_ref, t_ref, loss_ref, dx_ref, dxb_ref, dg_ref):
        xv = x_ref[...]
        gv = g_ref[...]
        r = lax.rsqrt(jnp.mean(xv * xv, axis=-1, keepdims=True) + EPS)
        xhat = xv * r
        diff = xhat * gv - t_ref[...]
        dy = diff * (1.0 / d)
        dxhat = dy * gv
        dx = r * (dxhat - xhat * jnp.mean(dxhat * xhat, axis=-1, keepdims=True))
        dx_ref[...] = dx
        dxb_ref[...] = dx.astype(BF16)
        dg_part = jnp.sum(dy * xhat, axis=0, keepdims=True)
        row_loss = jnp.sum(diff * diff, axis=-1, keepdims=True)
        loss_part = jnp.sum(row_loss, axis=0, keepdims=True) * (0.5 / d)

        @pl.when(pl.program_id(0) == 0)
        def _():
            dg_ref[...] = dg_part
            loss_ref[...] = jnp.broadcast_to(loss_part, loss_ref.shape)

        @pl.when(pl.program_id(0) > 0)
        def _():
            dg_ref[...] += dg_part
            loss_ref[...] += jnp.broadcast_to(loss_part, loss_ref.shape)

    row = pl.BlockSpec((ROW_TILE, d), lambda i: (i, 0))
    vec = pl.BlockSpec((1, d), lambda i: (0, 0))
    tile = pl.BlockSpec((8, LANES), lambda i: (0, 0))
    return pl.pallas_call(
        body,
        name=name,
        grid=(s // ROW_TILE,),
        in_specs=[row, vec, row],
        out_specs=[tile, row, row, vec],
        out_shape=[jax.ShapeDtypeStruct((8, LANES), F32), jax.ShapeDtypeStruct((s, d), F32),
                   jax.ShapeDtypeStruct((s, d), BF16), jax.ShapeDtypeStruct((1, d), F32)],
        compiler_params=_cparams(("arbitrary",)),
    )(x, g, target)


_BRANCHES = ((0, D_CONV), (D_CONV, D_SGU), (D_CONV + D_SGU, D_SB))


def _combine_fwd(ya, yb, yc, g, *, name):
    s = ya.shape[0]

    def body(ya_ref, yb_ref, yc_ref, g_ref, y_ref):
        for ref, (off, w) in zip((ya_ref, yb_ref, yc_ref), _BRANCHES):
            v = ref[...]
            r = lax.rsqrt(jnp.mean(v * v, axis=-1, keepdims=True) + EPS)
            y_ref[:, off:off + w] = (v * r * g_ref[:, off:off + w]).astype(BF16)

    def row(w):
        return pl.BlockSpec((ROW_TILE, w), lambda i: (i, 0))

    return pl.pallas_call(
        body,
        name=name,
        grid=(s // ROW_TILE,),
        in_specs=[row(D_CONV), row(D_SGU), row(D_SB), pl.BlockSpec((1, D_MODEL), lambda i: (0, 0))],
        out_specs=row(D_MODEL),
        out_shape=jax.ShapeDtypeStruct((s, D_MODEL), BF16),
        compiler_params=_cparams(("parallel",)),
    )(ya, yb, yc, g)


def _combine_bwd(dy, ya, yb, yc, g, *, name):
    s = ya.shape[0]

    def body(dy_ref, ya_ref, yb_ref, yc_ref, g_ref, dya_ref, dyb_ref, dyc_ref, dg_ref):
        first = pl.program_id(0) == 0
        for ref, dref, (off, w) in zip((ya_ref, yb_ref, yc_ref), (dya_ref, dyb_ref, dyc_ref), _BRANCHES):
            v = ref[...]
            r = lax.rsqrt(jnp.mean(v * v, axis=-1, keepdims=True) + EPS)
            n = v * r
            dout = dy_ref[:, off:off + w]
            dn = dout * g_ref[:, off:off + w]
            dref[...] = r * (dn - n * jnp.mean(dn * n, axis=-1, keepdims=True))
            part = jnp.sum(dout * n, axis=0, keepdims=True)

            @pl.when(first)
            def _():
                dg_ref[:, off:off + w] = part

            @pl.when(jnp.logical_not(first))
            def _():
                dg_ref[:, off:off + w] += part

    def row(w):
        return pl.BlockSpec((ROW_TILE, w), lambda i: (i, 0))

    vec = pl.BlockSpec((1, D_MODEL), lambda i: (0, 0))
    return pl.pallas_call(
        body,
        name=name,
        grid=(s // ROW_TILE,),
        in_specs=[row(D_MODEL), row(D_CONV), row(D_SGU), row(D_SB), vec],
        out_specs=[row(D_CONV), row(D_SGU), row(D_SB), vec],
        out_shape=[jax.ShapeDtypeStruct((s, D_CONV), F32), jax.ShapeDtypeStruct((s, D_SGU), F32),
                   jax.ShapeDtypeStruct((s, D_SB), F32), jax.ShapeDtypeStruct((1, D_MODEL), F32)],
        compiler_params=_cparams(("arbitrary",)),
    )(dy, ya, yb, yc, g)


CONV_TILE = 128


def _shift_down(window, j, halo):
    return pltpu.roll(window, j, 0)[halo:, :] if j else window[halo:, :]


def _shift_up(window, j, n_out):
    n = window.shape[0]
    return pltpu.roll(window, n - j, 0)[:n_out, :] if j else window[:n_out, :]


def _mixer_a_fwd(p_ab, conv_w, conv_b, ln_g, ln_b, *, name):
    s = p_ab.shape[0]
    nt = s // CONV_TILE

    def body(p_ref, w_ref, b_ref, g_ref, beta_ref, y_ref, h_ref):
        h_ref[0:CONV_HALO, :] = jnp.zeros((CONV_HALO, D_CONV), F32)

        def glu(i, c):
            t0 = pl.multiple_of(i * CONV_TILE, CONV_TILE)
            a = p_ref[pl.ds(t0, CONV_TILE), 0:D_CONV]
            gate = p_ref[pl.ds(t0, CONV_TILE), D_CONV:2 * D_CONV]
            h_ref[pl.ds(t0 + CONV_HALO, CONV_TILE), :] = a * _sigmoid(gate)
            return c

        lax.fori_loop(0, nt, glu, 0)

        def conv(i, c):
            t0 = pl.multiple_of(i * CONV_TILE, CONV_TILE)
            window = h_ref[pl.ds(t0, CONV_TILE + CONV_HALO), :]
            acc = jnp.zeros((CONV_TILE, D_CONV), F32) + b_ref[...]
            for k in range(CONV_K):
                acc = acc + w_ref[k:k + 1, :] * _shift_down(window, CONV_K - 1 - k, CONV_HALO)
            mu = jnp.mean(acc, axis=-1, keepdims=True)
            xc = acc - mu
            rstd = lax.rsqrt(jnp.mean(xc * xc, axis=-1, keepdims=True) + EPS)
            z = xc * rstd * g_ref[...] + beta_ref[...]
            y_ref[pl.ds(t0, CONV_TILE), :] = z * _sigmoid(z)
            return c

        lax.fori_loop(0, nt, conv, 0)

    full = lambda shape: pl.BlockSpec(shape, lambda i: (0, 0))
    return pl.pallas_call(
        body,
        name=name,
        grid=(1,),
        in_specs=[full((s, 2 * D_CONV)), full((CONV_K, D_CONV)), full((1, D_CONV)), full((1, D_CONV)),
                  full((1, D_CONV))],
        out_specs=full((s, D_CONV)),
        out_shape=jax.ShapeDtypeStruct((s, D_CONV), F32),
        scratch_shapes=[pltpu.VMEM((s + CONV_HALO, D_CONV), F32)],
        compiler_params=_cparams(("arbitrary",)),
    )(p_ab, conv_w, conv_b, ln_g, ln_b)


def _mixer_a_bwd(p_ab, dya, conv_w, conv_b, ln_g, ln_b, *, name):
    s = p_ab.shape[0]
    nt = s // CONV_TILE

    def body(p_ref, dy_ref, w_ref, b_ref, g_ref, beta_ref, dp_ref, dw_ref, db_ref, dg_ref, dbeta_ref, h_ref, dc_ref):
        h_ref[0:CONV_HALO, :] = jnp.zeros((CONV_HALO, D_CONV), F32)
        dc_ref[s:s + CONV_HALO, :] = jnp.zeros((CONV_HALO, D_CONV), F32)
        dw_ref[...] = jnp.zeros_like(dw_ref)
        db_ref[...] = jnp.zeros_like(db_ref)
        dg_ref[...] = jnp.zeros_like(dg_ref)
        dbeta_ref[...] = jnp.zeros_like(dbeta_ref)

        def glu(i, c):
            t0 = pl.multiple_of(i * CONV_TILE, CONV_TILE)
            a = p_ref[pl.ds(t0, CONV_TILE), 0:D_CONV]
            gate = p_ref[pl.ds(t0, CONV_TILE), D_CONV:2 * D_CONV]
            h_ref[pl.ds(t0 + CONV_HALO, CONV_TILE), :] = a * _sigmoid(gate)
            return c

        lax.fori_loop(0, nt, glu, 0)

        def conv_bwd(i, c):
            t0 = pl.multiple_of(i * CONV_TILE, CONV_TILE)
            window = h_ref[pl.ds(t0, CONV_TILE + CONV_HALO), :]
            taps = [_shift_down(window, CONV_K - 1 - k, CONV_HALO) for k in range(CONV_K)]
            acc = jnp.zeros((CONV_TILE, D_CONV), F32) + b_ref[...]
            for k in range(CONV_K):
                acc = acc + w_ref[k:k + 1, :] * taps[k]
            mu = jnp.mean(acc, axis=-1, keepdims=True)
            xc = acc - mu
            rstd = lax.rsqrt(jnp.mean(xc * xc, axis=-1, keepdims=True) + EPS)
            xhat = xc * rstd
            z = xhat * g_ref[...] + beta_ref[...]
            sg = _sigmoid(z)
            dz = dy_ref[pl.ds(t0, CONV_TILE), :] * (sg * (1.0 + z * (1.0 - sg)))
            dg_ref[...] += jnp.sum(dz * xhat, axis=0, keepdims=True)
            dbeta_ref[...] += jnp.sum(dz, axis=0, keepdims=True)
            dxhat = dz * g_ref[...]
            dc = rstd * (dxhat - jnp.mean(dxhat, axis=-1, keepdims=True)
                         - xhat * jnp.mean(dxhat * xhat, axis=-1, keepdims=True))
            dc_ref[pl.ds(t0, CONV_TILE), :] = dc
            db_ref[...] += jnp.sum(dc, axis=0, keepdims=True)
            for k in range(CONV_K):
                dw_ref[k:k + 1, :] += jnp.sum(dc * taps[k], axis=0, keepdims=True)
            return c

        lax.fori_loop(0, nt, conv_bwd, 0)

        def glu_bwd(i, c):
            t0 = pl.multiple_of(i * CONV_TILE, CONV_TILE)
            window = dc_ref[pl.ds(t0, CONV_TILE + CONV_HALO), :]
            dh = jnp.zeros((CONV_TILE, D_CONV), F32)
            for j in range(CONV_K):
                dh = dh + w_ref[CONV_K - 1 - j:CONV_K - j, :] * _shift_up(window, j, CONV_TILE)
            a = p_ref[pl.ds(t0, CONV_TILE), 0:D_CONV]
            sg = _sigmoid(p_ref[pl.ds(t0, CONV_TILE), D_CONV:2 * D_CONV])
            dp_ref[pl.ds(t0, CONV_TILE), 0:D_CONV] = (dh * sg).astype(BF16)
            dp_ref[pl.ds(t0, CONV_TILE), D_CONV:2 * D_CONV] = (dh * a * sg * (1.0 - sg)).astype(BF16)
            return c

        lax.fori_loop(0, nt, glu_bwd, 0)

    full = lambda shape: pl.BlockSpec(shape, lambda i: (0, 0))
    vec = jax.ShapeDtypeStruct((1, D_CONV), F32)
    return pl.pallas_call(
        body,
        name=name,
        grid=(1,),
        in_specs=[full((s, 2 * D_CONV)), full((s, D_CONV)), full((CONV_K, D_CONV)), full((1, D_CONV)),
                  full((1, D_CONV)), full((1, D_CONV))],
        out_specs=[full((s, 2 * D_CONV)), full((CONV_K, D_CONV)), full((1, D_CONV)), full((1, D_CONV)),
                   full((1, D_CONV))],
        out_shape=[jax.ShapeDtypeStruct((s, 2 * D_CONV), BF16), jax.ShapeDtypeStruct((CONV_K, D_CONV), F32),
                   vec, vec, vec],
        scratch_shapes=[pltpu.VMEM((s + CONV_HALO, D_CONV), F32), pltpu.VMEM((s + CONV_HALO, D_CONV), F32)],
        compiler_params=_cparams(("arbitrary",)),
    )(p_ab, dya, conv_w, conv_b, ln_g, ln_b)


N_SGU_HEADS = D_SGU // HEAD_DIM


def _head_masks(width):
    lane = lax.broadcasted_iota(jnp.int32, (1, width), 1)
    return [(lane >= h * HEAD_DIM) & (lane < (h + 1) * HEAD_DIM) for h in range(width // HEAD_DIM)]


def _tril_mask():
    r = lax.broadcasted_iota(jnp.int32, (CHUNK, CHUNK), 0)
    c = lax.broadcasted_iota(jnp.int32, (CHUNK, CHUNK), 1)
    return c <= r


def _sgu_norm(bv, g, beta):
    vg = _gelu(bv)
    mu = jnp.mean(vg, axis=-1, keepdims=True)
    xc = vg - mu
    rstd = lax.rsqrt(jnp.mean(xc * xc, axis=-1, keepdims=True) + EPS)
    xhat = xc * rstd
    return xhat, rstd, xhat * g + beta


def _sgu_fwd(p_ab, ln_g, ln_b, w_s, bias, *, name):
    s = p_ab.shape[0]

    def body(p_ref, g_ref, beta_ref, w_ref, bias_ref, y_ref):
        u = _gelu(p_ref[:, 0:D_SGU])
        _, _, vn = _sgu_norm(p_ref[:, D_SGU:2 * D_SGU], g_ref[...], beta_ref[...])
        vb = vn.astype(BF16)
        tril = _tril_mask()
        mixed = bias_ref[...]
        for h, m in enumerate(_head_masks(D_SGU)):
            wh = jnp.where(tril, w_ref[h], 0.0).astype(BF16)
            mixed = mixed + _dot(wh, jnp.where(m, vb, jnp.zeros_like(vb)), _NN)
        y_ref[...] = u * mixed

    return pl.pallas_call(
        body,
        name=name,
        grid=(s // CHUNK,),
        in_specs=[pl.BlockSpec((CHUNK, 2 * D_SGU), lambda i: (i, 1)),
                  pl.BlockSpec((1, D_SGU), lambda i: (0, 0)), pl.BlockSpec((1, D_SGU), lambda i: (0, 0)),
                  pl.BlockSpec((N_SGU_HEADS, CHUNK, CHUNK), lambda i: (0, 0, 0)),
                  pl.BlockSpec((CHUNK, D_SGU), lambda i: (0, 0))],
        out_specs=pl.BlockSpec((CHUNK, D_SGU), lambda i: (i, 0)),
        out_shape=jax.ShapeDtypeStruct((s, D_SGU), F32),
        compiler_params=_cparams(("parallel",)),
    )(p_ab, ln_g, ln_b, w_s, bias)


def _sgu_bwd(p_ab, dyb, ln_g, ln_b, w_s, bias, *, name):
    s = p_ab.shape[0]
    n_chunks = s // CHUNK

    def body(p_ref, dy_ref, g_ref, beta_ref, w_ref, bias_ref, dp_ref, dw_ref, db_ref, dg_ref, dbeta_ref, dbias_ref):
        @pl.when(pl.program_id(0) == 0)
        def _():
            dw_ref[...] = jnp.zeros_like(dw_ref)
            dbias_ref[...] = jnp.zeros_like(dbias_ref)
            dg_ref[...] = jnp.zeros_like(dg_ref)
            dbeta_ref[...] = jnp.zeros_like(dbeta_ref)

        bu = p_ref[:, 0:D_SGU]
        bv = p_ref[:, D_SGU:2 * D_SGU]
        u = _gelu(bu)
        gv = g_ref[...]
        xhat, rstd, vn = _sgu_norm(bv, gv, beta_ref[...])
        vb = vn.astype(BF16)
        tril = _tril_mask()
        masks = _head_masks(D_SGU)
        whs = [jnp.where(tril, w_ref[h], 0.0).astype(BF16) for h in range(N_SGU_HEADS)]
        mixed = bias_ref[...]
        for h, m in enumerate(masks):
            mixed = mixed + _dot(whs[h], jnp.where(m, vb, jnp.zeros_like(vb)), _NN)
        dy = dy_ref[...]
        dp_ref[:, 0:D_SGU] = (dy * mixed * _gelu_grad(bu)).astype(BF16)
        dmixed = dy * u
        dbias_ref[...] += dmixed
        dmb = dmixed.astype(BF16)
        dvn = jnp.zeros((CHUNK, D_SGU), F32)
        for h, m in enumerate(masks):
            dmh = jnp.where(m, dmb, jnp.zeros_like(dmb))
            dvn = dvn + _dot(whs[h], dmh, _TN)
            dw_ref[h] += jnp.where(tril, _dot(dmh, vb, _NT), 0.0)
        dg_ref[...] += jnp.sum(dvn * xhat, axis=0, keepdims=True)
        dbeta_ref[...] += jnp.sum(dvn, axis=0, keepdims=True)
        dxhat = dvn * gv
        dvg = rstd * (dxhat - jnp.mean(dxhat, axis=-1, keepdims=True)
                      - xhat * jnp.mean(dxhat * xhat, axis=-1, keepdims=True))
        dp_ref[:, D_SGU:2 * D_SGU] = (dvg * _gelu_grad(bv)).astype(BF16)

        @pl.when(pl.program_id(0) == n_chunks - 1)
        def _():
            chan = lax.broadcasted_iota(jnp.int32, (D_SGU, LANES), 0)
            head = lax.broadcasted_iota(jnp.int32, (D_SGU, LANES), 1)
            to_head = jnp.where(chan // HEAD_DIM == head, 1.0, 0.0).astype(BF16)
            db_ref[...] = _dot(_split_bf16(dbias_ref[...]), jnp.concatenate([to_head, to_head], axis=0), _NN)

    vec = pl.BlockSpec((1, D_SGU), lambda i: (0, 0))
    wspec = pl.BlockSpec((N_SGU_HEADS, CHUNK, CHUNK), lambda i: (0, 0, 0))
    bspec = pl.BlockSpec((CHUNK, D_SGU), lambda i: (0, 0))
    return pl.pallas_call(
        body,
        name=name,
        grid=(n_chunks,),
        in_specs=[pl.BlockSpec((CHUNK, 2 * D_SGU), lambda i: (i, 1)), pl.BlockSpec((CHUNK, D_SGU), lambda i: (i, 0)),
                  vec, vec, wspec, bspec],
        out_specs=[pl.BlockSpec((CHUNK, 2 * D_SGU), lambda i: (i, 0)), wspec,
                   pl.BlockSpec((CHUNK, LANES), lambda i: (0, 0)), vec, vec],
        out_shape=[jax.ShapeDtypeStruct((s, 2 * D_SGU), BF16),
                   jax.ShapeDtypeStruct((N_SGU_HEADS, CHUNK, CHUNK), F32),
                   jax.ShapeDtypeStruct((CHUNK, LANES), F32),
                   jax.ShapeDtypeStruct((1, D_SGU), F32), jax.ShapeDtypeStruct((1, D_SGU), F32)],
        scratch_shapes=[pltpu.VMEM((CHUNK, D_SGU), F32)],
        compiler_params=_cparams(("arbitrary",)),
    )(p_ab, dyb, ln_g, ln_b, w_s, bias)


N_PAIRS = D_SB // LANES
SB_SCALE = HEAD_DIM ** -0.5


def _sb_logits(z, valid):
    nz = -z
    t = jnp.log(1.0 + jnp.exp(jnp.minimum(z, nz)))
    l1 = jnp.minimum(nz, 0.0) - t
    if valid is not None:
        l1 = jnp.where(valid, l1, 0.0)
    return l1, jnp.minimum(z, 0.0) - t


def _split_hi_lo(x):
    hi = lax.bitcast_convert_type(lax.bitcast_convert_type(x, jnp.uint32) & jnp.uint32(0xFFFF0000), F32)
    return jnp.concatenate([hi, x - hi], axis=1)


def _cumsum_operand(keep):
    half = jnp.concatenate([keep.astype(F32), jnp.ones((CHUNK, CHUNK), F32)], axis=1)
    return jnp.concatenate([half, half], axis=0)


Q_BLOCKS_PER_STEP = 4


def _q_blocks_per_step(nq):
    return next(n for n in (Q_BLOCKS_PER_STEP, 2, 1) if nq % n == 0)


def _attn_fwd(qkv, *, name):
    s = qkv.shape[0]
    nq = s // CHUNK
    per_step = _q_blocks_per_step(nq)

    def body(q_ref, k_ref, v_ref, o_ref, t_ref):
        masks = _head_masks(LANES)
        row = lax.broadcasted_iota(jnp.int32, (CHUNK, CHUNK), 0)
        col = lax.broadcasted_iota(jnp.int32, (CHUNK, CHUNK), 1)
        after_op = _cumsum_operand(row > col)
        cmr = col - row
        zc = jnp.zeros((CHUNK, LANES), F32)

        def q_block(sub, _):
            qi = pl.program_id(1) * per_step + sub
            q_rows = pl.ds(pl.multiple_of(sub * CHUNK, CHUNK), CHUNK)
            q = q_ref[q_rows, :] * SB_SCALE
            zero = jnp.zeros_like(q)
            qs = [jnp.where(m, q, zero) for m in masks]

            def blocks(js, carry):
                o, c0, c1 = carry
                kvs, valids = [], []
                for j in js:
                    k0 = pl.multiple_of(jnp.maximum(j, 0) * CHUNK, CHUNK)
                    kvs.append((k_ref[pl.ds(k0, CHUNK), :], v_ref[pl.ds(k0, CHUNK), :]))
                    valids.append(cmr < jnp.where(j >= 0, (qi - j) * CHUNK, -CHUNK))
                units = [(h, b) for b in range(len(js)) for h in range(2)]
                zs = [_dot(qs[h], kvs[b][0], _NT) for h, b in units]
                logits = [_sb_logits(z, valids[b]) for z, (h, b) in zip(zs, units)]
                sums = [_dot(_split_hi_lo(l1), after_op, _NN) for l1, _ in logits]
                cs = [c0, c1]
                probs = []
                for (h, b), (_, lb), sm in zip(units, logits, sums):
                    probs.append(jnp.where(valids[b], jnp.exp(lb + sm[:, :CHUNK] + cs[h]), 0.0))
                    cs[h] = cs[h] + sm[:, CHUNK:]
                for (h, b), a in zip(units, probs):
                    o = o + _dot(a.astype(BF16), jnp.where(masks[h], kvs[b][1], zero), _NN)
                return o, cs[0], cs[1]

            n_four = (qi + 1) // 4
            carry = lax.fori_loop(0, n_four, lambda jj, c: blocks([qi - 4 * jj - i for i in range(4)], c), (zc,) * 3)
            top = qi - 4 * n_four
            o, c0, c1 = lax.fori_loop(0, (top + 2) // 2, lambda jj, c: blocks([top - 2 * jj, top - 2 * jj - 1], c), carry)
            o_ref[q_rows, :] = o
            t_ref[q_rows, 0:LANES] = c0
            t_ref[q_rows, LANES:2 * LANES] = c1
            return 0

        lax.fori_loop(0, per_step, q_block, 0)

    rows = per_step * CHUNK
    return pl.pallas_call(
        body,
        name=name,
        grid=(N_PAIRS, nq // per_step),
        in_specs=[pl.BlockSpec((rows, LANES), lambda p, i: (i, p)),
                  pl.BlockSpec((s, LANES), lambda p, i: (0, N_PAIRS + p)),
                  pl.BlockSpec((s, LANES), lambda p, i: (0, 2 * N_PAIRS + p))],
        out_specs=[pl.BlockSpec((rows, LANES), lambda p, i: (i, p)),
                   pl.BlockSpec((rows, 2 * LANES), lambda p, i: (i, p))],
        out_shape=[jax.ShapeDtypeStruct((s, D_SB), F32), jax.ShapeDtypeStruct((s, 2 * D_SB), F32)],
        compiler_params=_cparams(("parallel", "parallel")),
    )(qkv, qkv, qkv)


def _attn_bwd(qkv, t_tot, do, *, name):
    s = qkv.shape[0]
    nq = s // CHUNK
    per_step = _q_blocks_per_step(nq)

    def body(q_ref, k_ref, v_ref, t_ref, do_ref, dq_ref, dk_ref, dv_ref):
        @pl.when(pl.program_id(1) == 0)
        def _():
            dk_ref[...] = jnp.zeros_like(dk_ref)
            dv_ref[...] = jnp.zeros_like(dv_ref)

        masks = _head_masks(LANES)
        row = lax.broadcasted_iota(jnp.int32, (CHUNK, CHUNK), 0)
        col = lax.broadcasted_iota(jnp.int32, (CHUNK, CHUNK), 1)
        upto_op = _cumsum_operand(row <= col)
        before_op = _cumsum_operand(row < col)
        cmr = col - row
        zc = jnp.zeros((CHUNK, LANES), F32)

        def q_block(sub, _):
            qi = pl.program_id(1) * per_step + sub
            q_rows = pl.ds(pl.multiple_of(sub * CHUNK, CHUNK), CHUNK)
            q = q_ref[q_rows, :] * SB_SCALE
            dob = do_ref[q_rows, :].astype(BF16)
            zero = jnp.zeros_like(q)
            qs = [jnp.where(m, q, zero) for m in masks]
            dos = [jnp.where(m, dob, zero) for m in masks]
            tots = [t_ref[q_rows, 0:LANES], t_ref[q_rows, LANES:2 * LANES]]

            def blocks(js, carry):
                dq, cl0, cl1, cp0, cp1 = carry
                starts = [pl.multiple_of(jnp.minimum(j, nq - 1) * CHUNK, CHUNK) for j in js]
                valids = [cmr < (qi - j) * CHUNK for j in js]
                kvs = [(k_ref[pl.ds(k0, CHUNK), :], v_ref[pl.ds(k0, CHUNK), :]) for k0 in starts]
                units = [(h, b) for b in range(len(js)) for h in range(2)]
                zs = [_dot(qs[h], kvs[b][0], _NT) for h, b in units]
                das = [_dot(dos[h], kvs[b][1], _NT) for h, b in units]
                logits = [_sb_logits(z, valids[b]) for z, (h, b) in zip(zs, units)]
                sums = [_dot(_split_hi_lo(l1), upto_op, _NN) for l1, _ in logits]
                cls, cps = [cl0, cl1], [cp0, cp1]
                probs, gs = [], []
                for (h, b), (_, lb), sm, da in zip(units, logits, sums, das):
                    a = jnp.where(valids[b], jnp.exp(lb + (tots[h] - cls[h] - sm[:, :CHUNK])), 0.0)
                    probs.append(a)
                    gs.append(a * da)
                    cls[h] = cls[h] + sm[:, CHUNK:]
                sums_g = [_dot(_split_hi_lo(g), before_op, _NN) for g in gs]
                dzs = []
                for (h, b), (_, lb), g, sg in zip(units, logits, gs, sums_g):
                    dz = g - (g + sg[:, :CHUNK] + cps[h]) * jnp.exp(lb)
                    dzs.append(jnp.where(valids[b], dz, 0.0).astype(BF16))
                    cps[h] = cps[h] + sg[:, CHUNK:]
                for (h, b), dzb in zip(units, dzs):
                    dq = dq + _dot(dzb, jnp.where(masks[h], kvs[b][0], zero), _NN)
                for b, k0 in enumerate(starts):
                    dk_ref[pl.ds(k0, CHUNK), :] += _dot(dzs[2 * b], qs[0], _TN) + _dot(dzs[2 * b + 1], qs[1], _TN)
                    dv_ref[pl.ds(k0, CHUNK), :] += (_dot(probs[2 * b].astype(BF16), dos[0], _TN)
                                                    + _dot(probs[2 * b + 1].astype(BF16), dos[1], _TN))
                return dq, cls[0], cls[1], cps[0], cps[1]

            n_four = (qi + 1) // 4
            carry = lax.fori_loop(0, n_four, lambda jj, c: blocks([4 * jj + i for i in range(4)], c), (zc,) * 5)
            base = 4 * n_four
            carry = lax.fori_loop(0, (qi - base + 2) // 2, lambda jj, c: blocks([base + 2 * jj, base + 2 * jj + 1], c), carry)
            dq_ref[q_rows, :] = (carry[0] * SB_SCALE).astype(BF16)
            return 0

        lax.fori_loop(0, per_step, q_block, 0)

    rows = per_step * CHUNK
    blk = pl.BlockSpec((rows, LANES), lambda p, i: (i, p))
    col_blk = pl.BlockSpec((s, LANES), lambda p, i: (0, p))
    out = jax.ShapeDtypeStruct((s, D_SB), F32)
    return pl.pallas_call(
        body,
        name=name,
        grid=(N_PAIRS, nq // per_step),
        in_specs=[blk,
                  pl.BlockSpec((s, LANES), lambda p, i: (0, N_PAIRS + p)),
                  pl.BlockSpec((s, LANES), lambda p, i: (0, 2 * N_PAIRS + p)),
                  pl.BlockSpec((rows, 2 * LANES), lambda p, i: (i, p)),
                  blk],
        out_specs=[blk, col_blk, col_blk],
        out_shape=[jax.ShapeDtypeStruct((s, D_SB), BF16), out, out],
        compiler_params=_cparams(("parallel", "arbitrary")),
    )(qkv, qkv, qkv, t_tot, do)


FFN_TILE = 256
FFN_COLS = 256
N_FF_BLOCKS = D_FF // FFN_COLS


def _ffn_act_fwd(up0, conv_w, conv_b, *, name):
    s = up0.shape[0]
    nt = s // FFN_TILE

    def body(xg_ref, xv_ref, wg_ref, wv_ref, bg_ref, bv_ref, act_ref, pg_ref, pv_ref):
        pg_ref[0:FFN_HALO, :] = jnp.zeros((FFN_HALO, FFN_COLS), F32)
        pv_ref[0:FFN_HALO, :] = jnp.zeros((FFN_HALO, FFN_COLS), F32)
        pg_ref[FFN_HALO:, :] = xg_ref[...].astype(F32)
        pv_ref[FFN_HALO:, :] = xv_ref[...].astype(F32)

        def tile(i, c):
            t0 = pl.multiple_of(i * FFN_TILE, FFN_TILE)
            outs = []
            for p_ref, w_ref, b_ref in ((pg_ref, wg_ref, bg_ref), (pv_ref, wv_ref, bv_ref)):
                window = p_ref[pl.ds(t0, FFN_TILE + FFN_HALO), :]
                acc = b_ref[...] + w_ref[2:3, :] * window[FFN_HALO:, :]
                for j in range(1, FFN_K):
                    acc = acc + w_ref[FFN_K - 1 - j:FFN_K - j, :] * _shift_down(window, j, FFN_HALO)
                outs.append(acc)
            gate, val = outs
            act_ref[pl.ds(t0, FFN_TILE), :] = (gate * _sigmoid(gate) * val).astype(BF16)
            return c

        lax.fori_loop(0, nt, tile, 0)

    gcol = lambda rows: pl.BlockSpec((rows, FFN_COLS), lambda j: (0, j))
    vcol = lambda rows: pl.BlockSpec((rows, FFN_COLS), lambda j: (0, j + N_FF_BLOCKS))
    return pl.pallas_call(
        body,
        name=name,
        grid=(N_FF_BLOCKS,),
        in_specs=[gcol(s), vcol(s), gcol(FFN_K), vcol(FFN_K), gcol(1), vcol(1)],
        out_specs=gcol(s),
        out_shape=jax.ShapeDtypeStruct((s, D_FF), BF16),
        scratch_shapes=[pltpu.VMEM((s + FFN_HALO, FFN_COLS), F32), pltpu.VMEM((s + FFN_HALO, FFN_COLS), F32)],
        compiler_params=_cparams(("parallel",)),
    )(up0, up0, conv_w, conv_w, conv_b, conv_b)


def _ffn_act_bwd(up0, dact, conv_w, conv_b, *, name):
    s = up0.shape[0]
    nt = s // FFN_TILE

    def body(xg_ref, xv_ref, da_ref, wg_ref, wv_ref, bg_ref, bv_ref, dxg_ref, dxv_ref, dwg_ref, dwv_ref, dbg_ref, dbv_ref,
             pg_ref, pv_ref, dg_ref, dv_ref):
        zeros = jnp.zeros((FFN_HALO, FFN_COLS), F32)
        for p_ref, x_ref in ((pg_ref, xg_ref), (pv_ref, xv_ref)):
            p_ref[0:FFN_HALO, :] = zeros
            p_ref[FFN_HALO:, :] = x_ref[...].astype(F32)
        dg_ref[s:s + FFN_HALO, :] = zeros
        dv_ref[s:s + FFN_HALO, :] = zeros
        for ref in (dwg_ref, dwv_ref, dbg_ref, dbv_ref):
            ref[...] = jnp.zeros_like(ref)

        def conv(p_ref, w_ref, b_ref, t0):
            window = p_ref[pl.ds(t0, FFN_TILE + FFN_HALO), :]
            taps = [_shift_down(window, j, FFN_HALO) for j in range(FFN_K)]
            out = b_ref[...]
            for j in range(FFN_K):
                out = out + w_ref[FFN_K - 1 - j:FFN_K - j, :] * taps[j]
            return out, taps

        def tile(i, c):
            t0 = pl.multiple_of(i * FFN_TILE, FFN_TILE)
            gate, taps_g = conv(pg_ref, wg_ref, bg_ref, t0)
            val, taps_v = conv(pv_ref, wv_ref, bv_ref, t0)
            da = da_ref[pl.ds(t0, FFN_TILE), :].astype(F32)
            sg = lax.logistic(gate)
            dgate = da * val * (sg * (1.0 + gate * (1.0 - sg)))
            dval = da * gate * sg
            dg_ref[pl.ds(t0, FFN_TILE), :] = dgate
            dv_ref[pl.ds(t0, FFN_TILE), :] = dval
            dbg_ref[...] += jnp.sum(dgate, axis=0, keepdims=True)
            dbv_ref[...] += jnp.sum(dval, axis=0, keepdims=True)
            for j in range(FFN_K):
                dwg_ref[FFN_K - 1 - j:FFN_K - j, :] += jnp.sum(dgate * taps_g[j], axis=0, keepdims=True)
                dwv_ref[FFN_K - 1 - j:FFN_K - j, :] += jnp.sum(dval * taps_v[j], axis=0, keepdims=True)
            return c

        lax.fori_loop(0, nt, tile, 0)

        def tile_dx(i, c):
            t0 = pl.multiple_of(i * FFN_TILE, FFN_TILE)
            for d_ref, w_ref, dx_ref in ((dg_ref, wg_ref, dxg_ref), (dv_ref, wv_ref, dxv_ref)):
                window = d_ref[pl.ds(t0, FFN_TILE + FFN_HALO), :]
                dx = w_ref[FFN_K - 1:FFN_K, :] * window[:FFN_TILE, :]
                for j in range(1, FFN_K):
                    dx = dx + w_ref[FFN_K - 1 - j:FFN_K - j, :] * _shift_up(window, j, FFN_TILE)
                dx_ref[pl.ds(t0, FFN_TILE), :] = dx.astype(BF16)
            return c

        lax.fori_loop(0, nt, tile_dx, 0)

    gcol = lambda rows: pl.BlockSpec((rows, FFN_COLS), lambda j: (0, j))
    vcol = lambda rows: pl.BlockSpec((rows, FFN_COLS), lambda j: (0, j + N_FF_BLOCKS))
    half = lambda rows, dtype: jax.ShapeDtypeStruct((rows, D_FF), dtype)
    padded = pltpu.VMEM((s + FFN_HALO, FFN_COLS), F32)
    return pl.pallas_call(
        body,
        name=name,
        grid=(N_FF_BLOCKS,),
        in_specs=[gcol(s), vcol(s), gcol(s), gcol(FFN_K), vcol(FFN_K), gcol(1), vcol(1)],
        out_specs=[gcol(s), gcol(s), gcol(FFN_K), gcol(FFN_K), gcol(1), gcol(1)],
        out_shape=[half(s, BF16), half(s, BF16), half(FFN_K, F32), half(FFN_K, F32), half(1, F32), half(1, F32)],
        scratch_shapes=[padded, padded, padded, padded],
        compiler_params=_cparams(("parallel",)),
    )(up0, up0, dact, conv_w, conv_w, conv_b, conv_b)


MESH = pl.DeviceIdType.MESH


def _position():
    x, y, c = lax.axis_index("x"), lax.axis_index("y"), lax.axis_index("c")
    return x, y, c, 4 * x + 2 * y + c


def _peer(k):
    x, y, c, _ = _position()
    px = 1 - x if k & 4 else x
    py = 1 - y if k & 2 else y
    pc = 1 - c if k & 1 else c
    return (px, py, pc), 4 * px + 2 * py + pc


def _gather_blocks(src, *, name):
    def body(src_ref, out_ref, send_sems, recv_sems, local_sem):
        me = _position()[3]
        local = pltpu.make_async_copy(src_ref, out_ref.at[me], local_sem)
        local.start()
        sends, recvs = [], []
        for k in range(1, N_DEV):
            peer, pidx = _peer(k)
            sems = dict(send_sem=send_sems.at[k - 1], recv_sem=recv_sems.at[k - 1], device_id=peer, device_id_type=MESH)
            sends.append(pltpu.make_async_remote_copy(src_ref=src_ref, dst_ref=out_ref.at[me], **sems))
            recvs.append(pltpu.make_async_remote_copy(src_ref=src_ref, dst_ref=out_ref.at[pidx], **sems))
        for cp in sends:
            cp.start()
        for cp in recvs:
            cp.wait_recv()
        for cp in sends:
            cp.wait_send()
        local.wait()

    return pl.pallas_call(
        body,
        name=name,
        in_specs=[pl.BlockSpec(memory_space=pl.ANY)],
        out_specs=pl.BlockSpec(memory_space=pl.ANY),
        out_shape=jax.ShapeDtypeStruct((N_DEV,) + src.shape, src.dtype),
        scratch_shapes=[pltpu.SemaphoreType.DMA((N_DEV - 1,)), pltpu.SemaphoreType.DMA((N_DEV - 1,)),
                        pltpu.SemaphoreType.DMA],
    )(src)


_HBM = pl.BlockSpec(memory_space=pltpu.HBM)
_SEM = pl.BlockSpec(memory_space=pltpu.SEMAPHORE)
_DATAFLOW = pltpu.SideEffectType.DATAFLOW_SIDE_EFFECTING
N_PEERS = N_DEV - 1


class _SplitExchange:
    def __init__(self, src, *, kind, name):
        self.kind, self.name, self.dtype = kind, name, src.dtype
        scatter = kind.startswith("scatter")
        by_blocks = kind == "scatter_blocks"
        self.scatter, self.by_blocks = scatter, by_blocks
        if by_blocks:
            self.r, self.cols, self.land_shape = None, None, src.shape
        else:
            self.r = src.shape[0] // N_DEV if scatter else src.shape[0]
            self.cols = src.shape[1]
            self.land_shape = (N_DEV, self.r, self.cols) if scatter else (N_DEV * self.r, self.cols)
        r = self.r

        def copies(src_ref, land_ref, send_sems, recv_sems, local_sem):
            me = _position()[3]

            def rows(ref, idx):
                return ref.at[pl.ds(pl.multiple_of(idx * r, r), r), :]

            if by_blocks:
                outgoing = lambda idx: src_ref.at[idx]
            else:
                outgoing = (lambda idx: rows(src_ref, idx)) if scatter else (lambda idx: src_ref)
            slot = (lambda idx: land_ref.at[idx]) if scatter else (lambda idx: rows(land_ref, idx))
            sends, recvs = [], []
            for k in range(1, N_DEV):
                peer, pidx = _peer(k)
                sems = dict(send_sem=send_sems[k - 1], recv_sem=recv_sems[k - 1], device_id=peer, device_id_type=MESH)
                sends.append(pltpu.make_async_remote_copy(src_ref=outgoing(pidx), dst_ref=slot(me), **sems))
                recvs.append(pltpu.make_async_remote_copy(src_ref=outgoing(pidx), dst_ref=slot(pidx), **sems))
            return sends, recvs, pltpu.make_async_copy(outgoing(me), slot(me), local_sem)

        self._copies = copies
        self.src = src

    @staticmethod
    def start(exchanges, name):
        n = len(exchanges)
        per = 2 * N_PEERS + 1

        def start_body(*refs):
            outs = refs[2 * n:]
            for i, ex in enumerate(exchanges):
                sems = outs[per * i:per * (i + 1)]
                sends, _, local = ex._copies(refs[2 * i], refs[2 * i + 1], sems[:N_PEERS], sems[N_PEERS:-1], sems[-1])
                for cp in sends + [local]:
                    cp.start()
            outs[-1][...] = jnp.zeros_like(outs[-1])

        sem = pltpu.SemaphoreType.DMA(())
        operands, thru_shapes = [], []
        for ex in exchanges:
            operands += [pltpu.with_memory_space_constraint(ex.src, pltpu.HBM),
                         pltpu.with_memory_space_constraint(lax.empty(ex.land_shape, ex.dtype), pltpu.HBM)]
            thru_shapes += [pltpu.HBM(ex.src.shape, ex.dtype), pltpu.HBM(ex.land_shape, ex.dtype)]
        out = pl.pallas_call(
            start_body,
            name=name,
            in_specs=(_HBM,) * (2 * n),
            out_specs=(_SEM,) * (per * n) + (_HBM,) * (2 * n) + (pl.BlockSpec(memory_space=pltpu.VMEM),),
            out_shape=(sem,) * (per * n) + tuple(thru_shapes) + (jax.ShapeDtypeStruct((8, LANES), F32),),
            input_output_aliases={i: per * n + i for i in range(2 * n)},
            compiler_params=pltpu.CompilerParams(has_side_effects=_DATAFLOW),
        )(*operands)
        for i, ex in enumerate(exchanges):
            ex.sems = out[per * i:per * (i + 1)]
            ex.src_thru, ex.land_thru = out[per * n + 2 * i], out[per * n + 2 * i + 1]
        return out[-1][0, 0]

    def finish(self, after):
        copies = self._copies

        def wait_body(src_ref, land_ref, *rest):
            sends, recvs, local = copies(src_ref, land_ref, rest[:N_PEERS], rest[N_PEERS:2 * N_PEERS], rest[2 * N_PEERS])
            for cp in sends:
                cp.wait_send()
            for cp in recvs:
                cp.wait_recv()
            local.wait()

        return pl.pallas_call(
            wait_body,
            name=f"{self.name}_wait",
            in_specs=(_HBM, _HBM) + (_SEM,) * (2 * N_PEERS + 1) + (pl.BlockSpec(memory_space=pl.ANY),),
            out_specs=(_HBM, _HBM),
            out_shape=(pltpu.HBM(self.src_thru.shape, self.dtype), pltpu.HBM(self.land_shape, self.dtype)),
            input_output_aliases={0: 0, 1: 1},
            compiler_params=pltpu.CompilerParams(has_side_effects=_DATAFLOW),
        )(self.src_thru, self.land_thru, *self.sems, after)[1]


def _row_tile(rows):
    return _tile(rows, (256, 128, 64, 32, 16, 8))


def _layer_parts_specs(n_layers, n_parts, tr, cols):
    return [pl.BlockSpec((n_parts, tr, cols), lambda l, i, j=j: (0, jnp.where(l == j, i, 0), 0)) for j in range(n_layers)]


def _select_layer_sum(p_refs):
    l = pl.program_id(0)
    g = None
    for j, p_ref in enumerate(p_refs):
        gj = p_ref[0].astype(F32)
        for k in range(1, p_ref.shape[0]):
            gj = gj + p_ref[k].astype(F32)
        g = gj if g is None else jnp.where(l == j, gj, g)
    return g


def _adamw(parts, w, m, v, *, name):
    n_layers, rows, cols = w.shape
    tr = _row_tile(rows)

    def body(*refs):
        w_ref, m_ref, v_ref, g_ref, d_ref, m2_ref, v2_ref = refs[n_layers:]
        g = _select_layer_sum(refs[:n_layers])
        m2 = ADAM_B1 * m_ref[...] + (1.0 - ADAM_B1) * g
        v2 = ADAM_B2 * v_ref[...] + (1.0 - ADAM_B2) * (g * g)
        m_hat = m2 / (1.0 - ADAM_B1 ** ADAM_STEP)
        v_hat = v2 / (1.0 - ADAM_B2 ** ADAM_STEP)
        g_ref[...] = g
        d_ref[...] = -ADAM_LR * (m_hat / (jnp.sqrt(v_hat) + ADAM_EPS) + ADAM_WD * w_ref[...])
        m2_ref[...] = m2
        v2_ref[...] = v2

    slab = pl.BlockSpec((None, tr, cols), lambda l, i: (l, i, 0))
    out = jax.ShapeDtypeStruct((n_layers, rows, cols), F32)
    p_specs = _layer_parts_specs(n_layers, parts[0].shape[0], tr, cols)
    return pl.pallas_call(
        body,
        name=name,
        grid=(n_layers, rows // tr),
        in_specs=p_specs + [slab, slab, slab],
        out_specs=[slab, slab, slab, slab],
        out_shape=[out, out, out, out],
        compiler_params=_cparams(("arbitrary", "arbitrary")),
    )(*parts, w, m, v)


SLAB_ROWS = 256
_SMALL_SHARDED = (("conv_w", (2, 31, 32)), ("ffn_conv_w", (2, 3, 704)))
_REPLICATED = (("g_mix", (2, 1024)), ("conv_b", (2, 256)), ("conv_ln_g", (2, 256)), ("conv_ln_b", (2, 256)),
               ("sgu_ln_g", (2, 256)), ("sgu_ln_b", (2, 256)), ("sgu_w", (2, 4, 128, 128)), ("sgu_b", (2, 4, 128)),
               ("g_out", (2, 1024)), ("g_ffn", (2, 1024)), ("ffn_conv_b", (2, 5632)), ("g_final", (1024,)))


def _seg_rows(n_elems):
    return -(-n_elems // LANES)


def _pack(arrays, lead=()):
    segs = []
    for a in arrays:
        flat = a.reshape(lead + (-1,)).astype(F32)
        pad = _seg_rows(flat.shape[-1]) * LANES - flat.shape[-1]
        if pad:
            flat = jnp.pad(flat, [(0, 0)] * len(lead) + [(0, pad)])
        segs.append(flat)
    flat = jnp.concatenate(segs, axis=-1)
    rows = flat.shape[-1] // LANES
    pad_rows = -rows % SLAB_ROWS
    if pad_rows:
        flat = jnp.pad(flat, [(0, 0)] * len(lead) + [(0, pad_rows * LANES)])
    return flat.reshape(lead + (rows + pad_rows, LANES))


def _unpack(slab, shapes, lead=()):
    flat = slab.reshape(lead + (-1,))
    out, off = [], 0
    for shape in shapes:
        n = math.prod(shape)
        out.append(flat[..., off:off + n].reshape(lead + tuple(shape)))
        off += _seg_rows(n) * LANES
    return out


def _split_last(full):
    split = full.shape[:-1] + (N_DEV, full.shape[-1] // N_DEV)
    return jnp.moveaxis(full.reshape(split), -2, 0)


def _join_last(blocks):
    moved = jnp.moveaxis(blocks, 0, -2)
    return moved.reshape(moved.shape[:-2] + (moved.shape[-2] * moved.shape[-1],))


def _gathered(wt, n, l, after):
    if isinstance(wt[n][l], _SplitExchange):
        wt[n][l] = wt[n][l].finish(after)
    return wt[n][l]


def _layer_fwd(l, x, wt, small):
    tag = f"l{l}"
    h = _rmsnorm_fwd(x, small["g_mix"][l][None], name=f"{tag}_norm_mix")
    w_in_t = _gathered(wt, "w_in_t", l, h)
    p_ab = _matmul(h, w_in_t, "nt", name=f"{tag}_proj_ab", n=D_AB)
    qkv = _matmul(h, w_in_t, "nt", name=f"{tag}_proj_qkv", n=D_QKV, b_n0=D_AB, out_dtype=BF16)
    ya = _mixer_a_fwd(p_ab, wt["conv_w"][l], small["conv_b"][l][None], small["conv_ln_g"][l][None],
                      small["conv_ln_b"][l][None], name=f"{tag}_mixer_a")
    bias = jnp.repeat(small["sgu_b"][l].T, HEAD_DIM, axis=1)
    yb = _sgu_fwd(p_ab, small["sgu_ln_g"][l][None], small["sgu_ln_b"][l][None], small["sgu_w"][l], bias,
                  name=f"{tag}_sgu")
    yc, t_tot = _attn_fwd(qkv, name=f"{tag}_attn")
    y = _combine_fwd(ya, yb, yc, small["g_out"][l][None], name=f"{tag}_combine")
    x1 = _matmul(y, _gathered(wt, "w_out", l, y), "nn", name=f"{tag}_out_proj", residual=x)
    h2 = _rmsnorm_fwd(x1, small["g_ffn"][l][None], name=f"{tag}_norm_ffn")
    up0 = _matmul(h2, _gathered(wt, "w_up_t", l, h2), "nt", name=f"{tag}_up", out_dtype=BF16)
    act = _ffn_act_fwd(up0, wt["ffn_conv_w"][l], small["ffn_conv_b"][l][None], name=f"{tag}_ffn_act")
    x2 = _matmul(act, _gathered(wt, "w_down", l, act), "nn", name=f"{tag}_down", residual=x1)
    saved = dict(x=x, h=h, p_ab=p_ab, qkv=qkv, ya=ya, yb=yb, yc=yc, t_tot=t_tot, y=y, x1=x1, h2=h2, up0=up0,
                 act=act, bias=bias)
    return x2, saved


def _layer_bwd(l, dres, sv, wt, small, scattering, token):
    tag = f"l{l}b"
    g = {}

    def scatter(n, partial):
        scattering[n][l] = _SplitExchange(partial, kind="scatter_rows", name=f"scatter_{n}_l{l}")
        return _SplitExchange.start([scattering[n][l]], name=f"scatter_{n}_l{l}_start")

    dx2, dx2_b = dres
    dact = _matmul(dx2_b, wt["w_down"][l], "nt", name=f"{tag}_dact", out_dtype=BF16)
    tok = scatter("w_down", _matmul(sv["act"], dx2_b, "tn", name=f"{tag}_dw_down", out_dtype=BF16))
    dup_g, dup_v, dwg, dwv, dbg, dbv = _ffn_act_bwd(sv["up0"], dact, wt["ffn_conv_w"][l], small["ffn_conv_b"][l][None] + tok + token,
                                                    name=f"{tag}_ffn_act")
    g["ffn_conv_w"] = jnp.concatenate([dwg, dwv], axis=1)
    g["ffn_conv_b"] = jnp.concatenate([dbg[0], dbv[0]])
    dh2 = _matmul(dup_g, wt["w_up_t"][l], "nn", name=f"{tag}_dh2_gate")
    dh2 = _matmul(dup_v, wt["w_up_t"][l], "nn", name=f"{tag}_dh2_val", b_k0=D_FF, residual=dh2)
    dw_up = _matmul(dup_g, sv["h2"], "tn", name=f"{tag}_dw_up_gate", out_dtype=BF16, rows=(2 * D_FF, 0))
    dw_up = _matmul(dup_v, sv["h2"], "tn", name=f"{tag}_dw_up_val", out_dtype=BF16, rows=(2 * D_FF, D_FF), into=dw_up)
    tok = scatter("w_up_t", dw_up)
    dx1, dx1_b, dg = _rmsnorm_bwd(sv["x1"], small["g_ffn"][l][None] + tok, dh2, dx2, name=f"{tag}_norm_ffn")
    g["g_ffn"] = dg[0]
    dy = _matmul(dx1_b, wt["w_out"][l], "nt", name=f"{tag}_dy")
    tok = scatter("w_out", _matmul(sv["y"], dx1_b, "tn", name=f"{tag}_dw_out", out_dtype=BF16))
    dya, dyb, dyc, dg = _combine_bwd(dy, sv["ya"], sv["yb"], sv["yc"], small["g_out"][l][None] + tok,
                                     name=f"{tag}_combine")
    g["g_out"] = dg[0]
    dq, dk, dv = _attn_bwd(sv["qkv"], sv["t_tot"], dyc, name=f"{tag}_attn")
    dp_b, g["sgu_w"], db, dg, dbeta = _sgu_bwd(sv["p_ab"], dyb, small["sgu_ln_g"][l][None], small["sgu_ln_b"][l][None],
                                               small["sgu_w"][l], sv["bias"], name=f"{tag}_sgu")
    g["sgu_b"] = db[:, :N_SGU_HEADS].T
    g["sgu_ln_g"], g["sgu_ln_b"] = dg[0], dbeta[0]
    dp_a, g["conv_w"], dcb, dg, dbeta = _mixer_a_bwd(sv["p_ab"], dya, wt["conv_w"][l], small["conv_b"][l][None],
                                                     small["conv_ln_g"][l][None], small["conv_ln_b"][l][None],
                                                     name=f"{tag}_mixer_a")
    g["conv_b"], g["conv_ln_g"], g["conv_ln_b"] = dcb[0], dg[0], dbeta[0]
    dp = jnp.concatenate([dp_a, dp_b, dq, dk.astype(BF16), dv.astype(BF16)], axis=1)
    dh = _matmul(dp, wt["w_in_t"][l], "nn", name=f"{tag}_dh")
    tok = scatter("w_in_t", _matmul(dp, sv["h"], "tn", name=f"{tag}_dw_in", out_dtype=BF16))
    dx, dx_b, dg = _rmsnorm_bwd(sv["x"], small["g_mix"][l][None] + tok, dh, dx1, name=f"{tag}_norm_mix")
    g["g_mix"] = dg[0]
    return (dx, dx_b), g


_BIG = ("w_in_t", "w_out", "w_up_t", "w_down")


def kernel(x, g_mix, w_in, conv_w, conv_b, conv_ln_g, conv_ln_b, sgu_ln_g, sgu_ln_b, sgu_w, sgu_b, g_out, w_out, g_ffn, w_up, ffn_conv_w, ffn_conv_b, w_down, g_final, loss_target, m_g_mix, m_w_in, m_conv_w, m_conv_b, m_conv_ln_g, m_conv_ln_b, m_sgu_ln_g, m_sgu_ln_b, m_sgu_w, m_sgu_b, m_g_out, m_w_out, m_g_ffn, m_w_up, m_ffn_conv_w, m_ffn_conv_b, m_w_down, m_g_final, v_g_mix, v_w_in, v_conv_w, v_conv_b, v_conv_ln_g, v_conv_ln_b, v_sgu_ln_g, v_sgu_ln_b, v_sgu_w, v_sgu_b, v_g_out, v_w_out, v_g_ffn, v_w_up, v_ffn_conv_w, v_ffn_conv_b, v_w_down, v_g_final):
    given = dict(locals())
    n_layers = g_mix.shape[0]
    layers = range(n_layers)
    small_sharded = [n for n, _ in _SMALL_SHARDED]
    replicated = [n for n, _ in _REPLICATED]
    small = {n: given[n] for n in replicated}

    filters = _gather_blocks(_pack([given[n] for n in small_sharded]), name="gather_filters")
    filters, first_shard = lax.optimization_barrier((filters, w_in[0].T.astype(BF16)))
    wt = {n: [None] * n_layers for n in _BIG}
    wt["w_in_t"][0] = _SplitExchange(first_shard, kind="gather_rows", name="gather_w_in_t_l0")
    tok = _SplitExchange.start([wt["w_in_t"][0]], name="gather_w_in_t_l0_start")
    w_in, w_out, w_up, w_down, tok = lax.optimization_barrier((w_in, w_out, w_up, w_down, tok))
    shard = {"w_in_t": [w_in[l].T.astype(BF16) for l in layers], "w_out": [w_out[l].astype(BF16) for l in layers],
             "w_up_t": [w_up[l].T.astype(BF16) for l in layers], "w_down": [w_down[l].astype(BF16) for l in layers]}
    later = [(n, l) for l in layers for n in _BIG if (n, l) != ("w_in_t", 0)]
    for n, l in later:
        wt[n][l] = _SplitExchange(shard[n][l], kind="gather_rows", name=f"gather_{n}_l{l}")
    small["g_mix"] = g_mix + tok + _SplitExchange.start([wt[n][l] for n, l in later], name="gather_weights_start")
    for n, blocks in zip(small_sharded, _unpack(filters, [s for _, s in _SMALL_SHARDED], lead=(N_DEV,))):
        wt[n] = _join_last(blocks)

    xs = x[0]
    saved = []
    for l in layers:
        xs, sv = _layer_fwd(l, xs, wt, small)
        saved.append(sv)
    loss_tile, dx, dx_b, dgf = _loss_head(xs, g_final[None], loss_target[0], name="loss_head")
    dres = (dx, dx_b)
    scattering = {n: [None] * n_layers for n in _BIG}
    layered = [n for n in replicated if n != "g_final"]
    slabs = [None] * n_layers
    tok = 0.0
    for l in reversed(layers):
        dres, g = _layer_bwd(l, dres, saved[l], wt, small, scattering, tok)
        own = _pack([_split_last(g[n]) for n in small_sharded], lead=(N_DEV,))
        shared = _pack([g[n] for n in layered] + ([dgf[0]] if l == n_layers - 1 else []))
        slab = jnp.concatenate([own, jnp.broadcast_to(shared[None], (N_DEV,) + shared.shape)], axis=1)
        slabs[l] = _SplitExchange(slab, kind="scatter_blocks", name=f"scatter_small_grads_l{l}")
        tok = _SplitExchange.start([slabs[l]], name=f"scatter_small_grads_l{l}_start")
    n_own = own.shape[1]

    after_backward = jnp.full((8, LANES), tok)
    received = {n: [scattering[n][l].finish(after_backward) for l in layers] for n in _BIG}
    out = {}

    def update(n, parts, transposed=False):
        turn = (lambda a: jnp.swapaxes(a, 1, 2)) if transposed else (lambda a: a)
        results = _adamw(parts, turn(given[n]), turn(given["m_" + n]), turn(given["v_" + n]), name=f"adamw_{n}")
        for pre, res in zip(("grad_", "delta_", "new_m_", "new_v_"), results):
            out[pre + n] = turn(res)
        return results[0][0, :8, :LANES]

    update("w_out", received["w_out"])
    update("w_down", received["w_down"])
    update("w_in", received["w_in_t"], transposed=True)
    big_updated = update("w_up", received["w_up_t"], transposed=True)

    per_layer = {pre + n: [None] * n_layers for pre in ("grad_", "delta_", "new_m_", "new_v_") for n in small_sharded + layered}
    for l in reversed(layers):
        last = l == n_layers - 1
        stacks = [jnp.concatenate([_pack([given[pre + n][l] for n in small_sharded]),
                                   _pack([given[pre + n][l] for n in layered] + ([given[pre + "g_final"]] if last else []))])[None]
                  for pre in ("", "m_", "v_")]
        results = _adamw([slabs[l].finish(big_updated)], *stacks, name=f"adamw_small_l{l}")
        for pre, res in zip(("grad_", "delta_", "new_m_", "new_v_"), results):
            unpacked = (_unpack(res[0, :n_own], [s[1:] for _, s in _SMALL_SHARDED])
                        + _unpack(res[0, n_own:], [s[1:] for n, s in _REPLICATED if n != "g_final"] + ([g_final.shape] if last else [])))
            for n, a in zip(small_sharded + layered + (["g_final"] if last else []), unpacked):
                if n == "g_final":
                    out[pre + n] = a
                else:
                    per_layer[pre + n][l] = a
    for name, parts in per_layer.items():
        out[name] = jnp.stack(parts)

    loss = lax.psum(loss_tile[0, 0], ("x", "y", "c"))
    order = list(_WEIGHT_ORDER)
    return (loss, dres[0][None], *[out["grad_" + n] for n in order], *[out["delta_" + n] for n in order],
            *[out["new_m_" + n] for n in order], *[out["new_v_" + n] for n in order])


_WEIGHT_ORDER = ("g_mix", "w_in", "conv_w", "conv_b", "conv_ln_g", "conv_ln_b", "sgu_ln_g", "sgu_ln_b", "sgu_w", "sgu_b",
                 "g_out", "w_out", "g_ffn", "w_up", "ffn_conv_w", "ffn_conv_b", "w_down", "g_final")
```

```python
import math

import jax
import jax.numpy as jnp
from jax import lax
from jax.experimental import pallas as pl
from jax.experimental.pallas import tpu as pltpu

F32 = jnp.float32
BF16 = jnp.bfloat16

N_DEV = 8
D_MODEL = 1024
HEAD_DIM = 64
D_CONV = 256
D_SGU = 256
D_SB = 512
D_AB = 2 * D_CONV + 2 * D_SGU
D_QKV = 3 * D_SB
D_IN = D_AB + D_QKV
CONV_K = 31
CONV_HALO = 32
FFN_K = 3
FFN_HALO = 8
D_FF = 2816
CHUNK = 128
EPS = 1e-6
LANES = 128

ADAM_LR = 0.001
ADAM_B1 = 0.9
ADAM_B2 = 0.999
ADAM_EPS = 1e-08
ADAM_WD = 0.01
ADAM_STEP = 10

VMEM_LIMIT = 56 * 1024 * 1024


def _cparams(sem=None):
    return pltpu.CompilerParams(dimension_semantics=sem, vmem_limit_bytes=VMEM_LIMIT)


def _tile(n, prefs=(512, 256, 128)):
    for t in prefs:
        if n % t == 0:
            return t
    return n


def _sigmoid(x):
    return 1.0 / (1.0 + jnp.exp(-x))


_INV_SQRT2 = 1.0 / math.sqrt(2.0)
_INV_SQRT2PI = 1.0 / math.sqrt(2.0 * math.pi)


def _gelu(x):
    return 0.5 * x * (1.0 + lax.erf(x * _INV_SQRT2))


def _gelu_grad(x):
    return 0.5 * (1.0 + lax.erf(x * _INV_SQRT2)) + x * jnp.exp(-0.5 * x * x) * _INV_SQRT2PI


def _dot(a, b, dims):
    return lax.dot_general(a, b, (dims, ((), ())), preferred_element_type=F32)


_NN = ((1,), (0,))
_NT = ((1,), (1,))
_TN = ((0,), (0,))


def _split_bf16(x):
    hi = x.astype(BF16)
    lo = (x - hi.astype(F32)).astype(BF16)
    return jnp.concatenate([hi, lo], axis=1)


def _matmul(a, b, mode, *, name, out_dtype=F32, residual=None, n=None, b_n0=0, b_k0=0, rows=None, into=None):
    if mode == "nn":
        (m, k), n = a.shape, (n or b.shape[1])
    elif mode == "nt":
        (m, k), n = a.shape, (n or b.shape[0])
    else:
        (k, m), n = a.shape, b.shape[1]
    has_res = residual is not None
    tm, tn = _matmul_tiles(m, n, k, a.dtype.itemsize, b.dtype.itemsize, jnp.dtype(out_dtype).itemsize, has_res, b_n0)
    j0 = b_n0 // tn
    total_rows, first_row = rows or (m, 0)
    assert b_k0 % k == 0 and first_row % tm == 0
    kb, i0 = b_k0 // k, first_row // tm

    if mode == "nn":
        a_spec = pl.BlockSpec((tm, k), lambda i, j: (i, 0))
        b_spec = pl.BlockSpec((k, tn), lambda i, j: (kb, j + j0))
        dims = _NN
    elif mode == "nt":
        a_spec = pl.BlockSpec((tm, k), lambda i, j: (i, 0))
        b_spec = pl.BlockSpec((tn, k), lambda i, j: (j + j0, 0))
        dims = _NT
    else:
        a_spec = pl.BlockSpec((k, tm), lambda i, j: (0, i))
        b_spec = pl.BlockSpec((k, tn), lambda i, j: (0, j))
        dims = _TN
    o_spec = pl.BlockSpec((tm, tn), lambda i, j: (i + i0, j))
    r_spec = pl.BlockSpec((tm, tn), lambda i, j: (i, j))

    def body(*refs):
        a_ref, b_ref = refs[:2]
        acc = _dot(a_ref[...].astype(BF16), b_ref[...].astype(BF16), dims)
        if has_res:
            acc = acc + refs[2][...]
        refs[-1][...] = acc.astype(out_dtype)

    in_specs = [a_spec, b_spec] + ([r_spec] if has_res else [])
    args = (a, b) + ((residual,) if has_res else ())
    aliases = {}
    if into is not None:
        aliases = {len(args): 0}
        in_specs.append(pl.BlockSpec(memory_space=pl.ANY))
        args += (into,)

        def body(*refs, inner=body):
            inner(*refs[:len(args) - 1], refs[-1])

    return pl.pallas_call(
        body,
        name=name,
        grid=(m // tm, n // tn),
        in_specs=in_specs,
        out_specs=o_spec,
        out_shape=jax.ShapeDtypeStruct((total_rows, n), out_dtype),
        input_output_aliases=aliases,
        compiler_params=_cparams(("parallel", "parallel")),
    )(*args)


MATMUL_VMEM_BUDGET = 40 * 1024 * 1024


def _matmul_tiles(m, n, k, a_bytes, b_bytes, out_bytes, has_res, n_offset):
    def divisors(size, cap, also=0):
        return [t for t in range(cap, 0, -LANES) if size % t == 0 and also % t == 0] or [size]

    for tm in divisors(m, 1024):
        for tn in divisors(n, 1408, n_offset):
            blocks = tm * k * a_bytes + k * tn * b_bytes + tm * tn * (out_bytes + (4 if has_res else 0))
            if 2 * blocks <= MATMUL_VMEM_BUDGET:
                return tm, tn
    raise ValueError(f"no matmul tiling for {m} x {n} x {k}")


ROW_TILE = 256


def _rmsnorm_fwd(x, g, *, name):
    s, d = x.shape

    def body(x_ref, g_ref, h_ref):
        xv = x_ref[...]
        r = lax.rsqrt(jnp.mean(xv * xv, axis=-1, keepdims=True) + EPS)
        h_ref[...] = (xv * r * g_ref[...]).astype(BF16)

    return pl.pallas_call(
        body,
        name=name,
        grid=(s // ROW_TILE,),
        in_specs=[pl.BlockSpec((ROW_TILE, d), lambda i: (i, 0)), pl.BlockSpec((1, d), lambda i: (0, 0))],
        out_specs=pl.BlockSpec((ROW_TILE, d), lambda i: (i, 0)),
        out_shape=jax.ShapeDtypeStruct((s, d), BF16),
        compiler_params=_cparams(("parallel",)),
    )(x, g)


def _rmsnorm_bwd(x, g, dh, dres, *, name):
    s, d = x.shape

    def body(x_ref, g_ref, dh_ref, dres_ref, dx_ref, dxb_ref, dg_ref):
        xv = x_ref[...]
        r = lax.rsqrt(jnp.mean(xv * xv, axis=-1, keepdims=True) + EPS)
        xhat = xv * r
        dhv = dh_ref[...]
        dxhat = dhv * g_ref[...]
        dx = dres_ref[...] + r * (dxhat - xhat * jnp.mean(dxhat * xhat, axis=-1, keepdims=True))
        dx_ref[...] = dx
        dxb_ref[...] = dx.astype(BF16)
        part = jnp.sum(dhv * xhat, axis=0, keepdims=True)

        @pl.when(pl.program_id(0) == 0)
        def _():
            dg_ref[...] = part

        @pl.when(pl.program_id(0) > 0)
        def _():
            dg_ref[...] += part

    row = pl.BlockSpec((ROW_TILE, d), lambda i: (i, 0))
    vec = pl.BlockSpec((1, d), lambda i: (0, 0))
    return pl.pallas_call(
        body,
        name=name,
        grid=(s // ROW_TILE,),
        in_specs=[row, vec, row, row],
        out_specs=[row, row, vec],
        out_shape=[jax.ShapeDtypeStruct((s, d), F32), jax.ShapeDtypeStruct((s, d), BF16),
                   jax.ShapeDtypeStruct((1, d), F32)],
        compiler_params=_cparams(("arbitrary",)),
    )(x, g, dh, dres)


def _loss_head(x, g, target, *, name):
    s, d = x.shape

    def body(x_ref, g_ref, t_ref, loss_ref, dx_ref, dxb_ref, dg_ref):
        xv = x_ref[...]
        gv = g_ref[...]
        r = lax.rsqrt(jnp.mean(xv * xv, axis=-1, keepdims=True) + EPS)
        xhat = xv * r
        diff = xhat * gv - t_ref[...]
        dy = diff * (1.0 / d)
        dxhat = dy * gv
        dx = r * (dxhat - xhat * jnp.mean(dxhat * xhat, axis=-1, keepdims=True))
        dx_ref[...] = dx
        dxb_ref[...] = dx.astype(BF16)
        dg_part = jnp.sum(dy * xhat, axis=0, keepdims=True)
        row_loss = jnp.sum(diff * diff, axis=-1, keepdims=True)
        loss_part = jnp.sum(row_loss, axis=0, keepdims=True) * (0.5 / d)

        @pl.when(pl.program_id(0) == 0)
        def _():
            dg_ref[...] = dg_part
            loss_ref[...] = jnp.broadcast_to(loss_part, loss_ref.shape)

        @pl.when(pl.program_id(0) > 0)
        def _():
            dg_ref[...] += dg_part
            loss_ref[...] += jnp.broadcast_to(loss_part, loss_ref.shape)

    row = pl.BlockSpec((ROW_TILE, d), lambda i: (i, 0))
    vec = pl.BlockSpec((1, d), lambda i: (0, 0))
    tile = pl.BlockSpec((8, LANES), lambda i: (0, 0))
    return pl.pallas_call(
        body,
        name=name,
        grid=(s // ROW_TILE,),
        in_specs=[row, vec, row],
        out_specs=[tile, row, row, vec],
        out_shape=[jax.ShapeDtypeStruct((8, LANES), F32), jax.ShapeDtypeStruct((s, d), F32),
                   jax.ShapeDtypeStruct((s, d), BF16), jax.ShapeDtypeStruct((1, d), F32)],
        compiler_params=_cparams(("arbitrary",)),
    )(x, g, target)


_BRANCHES = ((0, D_CONV), (D_CONV, D_SGU), (D_CONV + D_SGU, D_SB))


def _combine_fwd(ya, yb, yc, g, *, name):
    s = ya.shape[0]

    def body(ya_ref, yb_ref, yc_ref, g_ref, y_ref):
        for ref, (off, w) in zip((ya_ref, yb_ref, yc_ref), _BRANCHES):
            v = ref[...]
            r = lax.rsqrt(jnp.mean(v * v, axis=-1, keepdims=True) + EPS)
            y_ref[:, off:off + w] = (v * r * g_ref[:, off:off + w]).astype(BF16)

    def row(w):
        return pl.BlockSpec((ROW_TILE, w), lambda i: (i, 0))

    return pl.pallas_call(
        body,
        name=name,
        grid=(s // ROW_TILE,),
        in_specs=[row(D_CONV), row(D_SGU), row(D_SB), pl.BlockSpec((1, D_MODEL), lambda i: (0, 0))],
        out_specs=row(D_MODEL),
        out_shape=jax.ShapeDtypeStruct((s, D_MODEL), BF16),
        compiler_params=_cparams(("parallel",)),
    )(ya, yb, yc, g)


def _combine_bwd(dy, ya, yb, yc, g, *, name):
    s = ya.shape[0]

    def body(dy_ref, ya_ref, yb_ref, yc_ref, g_ref, dya_ref, dyb_ref, dyc_ref, dg_ref):
        first = pl.program_id(0) == 0
        for ref, dref, (off, w) in zip((ya_ref, yb_ref, yc_ref), (dya_ref, dyb_ref, dyc_ref), _BRANCHES):
            v = ref[...]
            r = lax.rsqrt(jnp.mean(v * v, axis=-1, keepdims=True) + EPS)
            n = v * r
            dout = dy_ref[:, off:off + w]
            dn = dout * g_ref[:, off:off + w]
            dref[...] = r * (dn - n * jnp.mean(dn * n, axis=-1, keepdims=True))
            part = jnp.sum(dout * n, axis=0, keepdims=True)

            @pl.when(first)
            def _():
                dg_ref[:, off:off + w] = part

            @pl.when(jnp.logical_not(first))
            def _():
                dg_ref[:, off:off + w] += part

    def row(w):
        return pl.BlockSpec((ROW_TILE, w), lambda i: (i, 0))

    vec = pl.BlockSpec((1, D_MODEL), lambda i: (0, 0))
    return pl.pallas_call(
        body,
        name=name,
        grid=(s // ROW_TILE,),
        in_specs=[row(D_MODEL), row(D_CONV), row(D_SGU), row(D_SB), vec],
        out_specs=[row(D_CONV), row(D_SGU), row(D_SB), vec],
        out_shape=[jax.ShapeDtypeStruct((s, D_CONV), F32), jax.ShapeDtypeStruct((s, D_SGU), F32),
                   jax.ShapeDtypeStruct((s, D_SB), F32), jax.ShapeDtypeStruct((1, D_MODEL), F32)],
        compiler_params=_cparams(("arbitrary",)),
    )(dy, ya, yb, yc, g)


CONV_TILE = 128


def _shift_down(window, j, halo):
    return pltpu.roll(window, j, 0)[halo:, :] if j else window[halo:, :]


def _shift_up(window, j, n_out):
    n = window.shape[0]
    return pltpu.roll(window, n - j, 0)[:n_out, :] if j else window[:n_out, :]


def _mixer_a_fwd(p_ab, conv_w, conv_b, ln_g, ln_b, *, name):
    s = p_ab.shape[0]
    nt = s // CONV_TILE

    def body(p_ref, w_ref, b_ref, g_ref, beta_ref, y_ref, h_ref):
        h_ref[0:CONV_HALO, :] = jnp.zeros((CONV_HALO, D_CONV), F32)

        def glu(i, c):
            t0 = pl.multiple_of(i * CONV_TILE, CONV_TILE)
            a = p_ref[pl.ds(t0, CONV_TILE), 0:D_CONV]
            gate = p_ref[pl.ds(t0, CONV_TILE), D_CONV:2 * D_CONV]
            h_ref[pl.ds(t0 + CONV_HALO, CONV_TILE), :] = a * _sigmoid(gate)
            return c

        lax.fori_loop(0, nt, glu, 0)

        def conv(i, c):
            t0 = pl.multiple_of(i * CONV_TILE, CONV_TILE)
            window = h_ref[pl.ds(t0, CONV_TILE + CONV_HALO), :]
            acc = jnp.zeros((CONV_TILE, D_CONV), F32) + b_ref[...]
            for k in range(CONV_K):
                acc = acc + w_ref[k:k + 1, :] * _shift_down(window, CONV_K - 1 - k, CONV_HALO)
            mu = jnp.mean(acc, axis=-1, keepdims=True)
            xc = acc - mu
            rstd = lax.rsqrt(jnp.mean(xc * xc, axis=-1, keepdims=True) + EPS)
            z = xc * rstd * g_ref[...] + beta_ref[...]
            y_ref[pl.ds(t0, CONV_TILE), :] = z * _sigmoid(z)
            return c

        lax.fori_loop(0, nt, conv, 0)

    full = lambda shape: pl.BlockSpec(shape, lambda i: (0, 0))
    return pl.pallas_call(
        body,
        name=name,
        grid=(1,),
        in_specs=[full((s, 2 * D_CONV)), full((CONV_K, D_CONV)), full((1, D_CONV)), full((1, D_CONV)),
                  full((1, D_CONV))],
        out_specs=full((s, D_CONV)),
        out_shape=jax.ShapeDtypeStruct((s, D_CONV), F32),
        scratch_shapes=[pltpu.VMEM((s + CONV_HALO, D_CONV), F32)],
        compiler_params=_cparams(("arbitrary",)),
    )(p_ab, conv_w, conv_b, ln_g, ln_b)


def _mixer_a_bwd(p_ab, dya, conv_w, conv_b, ln_g, ln_b, *, name):
    s = p_ab.shape[0]
    nt = s // CONV_TILE

    def body(p_ref, dy_ref, w_ref, b_ref, g_ref, beta_ref, dp_ref, dw_ref, db_ref, dg_ref, dbeta_ref, h_ref, dc_ref):
        h_ref[0:CONV_HALO, :] = jnp.zeros((CONV_HALO, D_CONV), F32)
        dc_ref[s:s + CONV_HALO, :] = jnp.zeros((CONV_HALO, D_CONV), F32)
        dw_ref[...] = jnp.zeros_like(dw_ref)
        db_ref[...] = jnp.zeros_like(db_ref)
        dg_ref[...] = jnp.zeros_like(dg_ref)
        dbeta_ref[...] = jnp.zeros_like(dbeta_ref)

        def glu(i, c):
            t0 = pl.multiple_of(i * CONV_TILE, CONV_TILE)
            a = p_ref[pl.ds(t0, CONV_TILE), 0:D_CONV]
            gate = p_ref[pl.ds(t0, CONV_TILE), D_CONV:2 * D_CONV]
            h_ref[pl.ds(t0 + CONV_HALO, CONV_TILE), :] = a * _sigmoid(gate)
            return c

        lax.fori_loop(0, nt, glu, 0)

        def conv_bwd(i, c):
            t0 = pl.multiple_of(i * CONV_TILE, CONV_TILE)
            window = h_ref[pl.ds(t0, CONV_TILE + CONV_HALO), :]
            taps = [_shift_down(window, CONV_K - 1 - k, CONV_HALO) for k in range(CONV_K)]
            acc = jnp.zeros((CONV_TILE, D_CONV), F32) + b_ref[...]
            for k in range(CONV_K):
                acc = acc + w_ref[k:k + 1, :] * taps[k]
            mu = jnp.mean(acc, axis=-1, keepdims=True)
            xc = acc - mu
            rstd = lax.rsqrt(jnp.mean(xc * xc, axis=-1, keepdims=True) + EPS)
            xhat = xc * rstd
            z = xhat * g_ref[...] + beta_ref[...]
            sg = _sigmoid(z)
            dz = dy_ref[pl.ds(t0, CONV_TILE), :] * (sg * (1.0 + z * (1.0 - sg)))
            dg_ref[...] += jnp.sum(dz * xhat, axis=0, keepdims=True)
            dbeta_ref[...] += jnp.sum(dz, axis=0, keepdims=True)
            dxhat = dz * g_ref[...]
            dc = rstd * (dxhat - jnp.mean(dxhat, axis=-1, keepdims=True)
                         - xhat * jnp.mean(dxhat * xhat, axis=-1, keepdims=True))
            dc_ref[pl.ds(t0, CONV_TILE), :] = dc
            db_ref[...] += jnp.sum(dc, axis=0, keepdims=True)
            for k in range(CONV_K):
                dw_ref[k:k + 1, :] += jnp.sum(dc * taps[k], axis=0, keepdims=True)
            return c

        lax.fori_loop(0, nt, conv_bwd, 0)

        def glu_bwd(i, c):
            t0 = pl.multiple_of(i * CONV_TILE, CONV_TILE)
            window = dc_ref[pl.ds(t0, CONV_TILE + CONV_HALO), :]
            dh = jnp.zeros((CONV_TILE, D_CONV), F32)
            for j in range(CONV_K):
                dh = dh + w_ref[CONV_K - 1 - j:CONV_K - j, :] * _shift_up(window, j, CONV_TILE)
            a = p_ref[pl.ds(t0, CONV_TILE), 0:D_CONV]
            sg = _sigmoid(p_ref[pl.ds(t0, CONV_TILE), D_CONV:2 * D_CONV])
            dp_ref[pl.ds(t0, CONV_TILE), 0:D_CONV] = (dh * sg).astype(BF16)
            dp_ref[pl.ds(t0, CONV_TILE), D_CONV:2 * D_CONV] = (dh * a * sg * (1.0 - sg)).astype(BF16)
            return c

        lax.fori_loop(0, nt, glu_bwd, 0)

    full = lambda shape: pl.BlockSpec(shape, lambda i: (0, 0))
    vec = jax.ShapeDtypeStruct((1, D_CONV), F32)
    return pl.pallas_call(
        body,
        name=name,
        grid=(1,),
        in_specs=[full((s, 2 * D_CONV)), full((s, D_CONV)), full((CONV_K, D_CONV)), full((1, D_CONV)),
                  full((1, D_CONV)), full((1, D_CONV))],
        out_specs=[full((s, 2 * D_CONV)), full((CONV_K, D_CONV)), full((1, D_CONV)), full((1, D_CONV)),
                   full((1, D_CONV))],
        out_shape=[jax.ShapeDtypeStruct((s, 2 * D_CONV), BF16), jax.ShapeDtypeStruct((CONV_K, D_CONV), F32),
                   vec, vec, vec],
        scratch_shapes=[pltpu.VMEM((s + CONV_HALO, D_CONV), F32), pltpu.VMEM((s + CONV_HALO, D_CONV), F32)],
        compiler_params=_cparams(("arbitrary",)),
    )(p_ab, dya, conv_w, conv_b, ln_g, ln_b)


N_SGU_HEADS = D_SGU // HEAD_DIM


def _head_masks(width):
    lane = lax.broadcasted_iota(jnp.int32, (1, width), 1)
    return [(lane >= h * HEAD_DIM) & (lane < (h + 1) * HEAD_DIM) for h in range(width // HEAD_DIM)]


def _tril_mask():
    r = lax.broadcasted_iota(jnp.int32, (CHUNK, CHUNK), 0)
    c = lax.broadcasted_iota(jnp.int32, (CHUNK, CHUNK), 1)
    return c <= r


def _sgu_norm(bv, g, beta):
    vg = _gelu(bv)
    mu = jnp.mean(vg, axis=-1, keepdims=True)
    xc = vg - mu
    rstd = lax.rsqrt(jnp.mean(xc * xc, axis=-1, keepdims=True) + EPS)
    xhat = xc * rstd
    return xhat, rstd, xhat * g + beta


def _sgu_fwd(p_ab, ln_g, ln_b, w_s, bias, *, name):
    s = p_ab.shape[0]

    def body(p_ref, g_ref, beta_ref, w_ref, bias_ref, y_ref):
        u = _gelu(p_ref[:, 0:D_SGU])
        _, _, vn = _sgu_norm(p_ref[:, D_SGU:2 * D_SGU], g_ref[...], beta_ref[...])
        vb = vn.astype(BF16)
        tril = _tril_mask()
        mixed = bias_ref[...]
        for h, m in enumerate(_head_masks(D_SGU)):
            wh = jnp.where(tril, w_ref[h], 0.0).astype(BF16)
            mixed = mixed + _dot(wh, jnp.where(m, vb, jnp.zeros_like(vb)), _NN)
        y_ref[...] = u * mixed

    return pl.pallas_call(
        body,
        name=name,
        grid=(s // CHUNK,),
        in_specs=[pl.BlockSpec((CHUNK, 2 * D_SGU), lambda i: (i, 1)),
                  pl.BlockSpec((1, D_SGU), lambda i: (0, 0)), pl.BlockSpec((1, D_SGU), lambda i: (0, 0)),
                  pl.BlockSpec((N_SGU_HEADS, CHUNK, CHUNK), lambda i: (0, 0, 0)),
                  pl.BlockSpec((CHUNK, D_SGU), lambda i: (0, 0))],
        out_specs=pl.BlockSpec((CHUNK, D_SGU), lambda i: (i, 0)),
        out_shape=jax.ShapeDtypeStruct((s, D_SGU), F32),
        compiler_params=_cparams(("parallel",)),
    )(p_ab, ln_g, ln_b, w_s, bias)


def _sgu_bwd(p_ab, dyb, ln_g, ln_b, w_s, bias, *, name):
    s = p_ab.shape[0]
    n_chunks = s // CHUNK

    def body(p_ref, dy_ref, g_ref, beta_ref, w_ref, bias_ref, dp_ref, dw_ref, db_ref, dg_ref, dbeta_ref, dbias_ref):
        @pl.when(pl.program_id(0) == 0)
        def _():
            dw_ref[...] = jnp.zeros_like(dw_ref)
            dbias_ref[...] = jnp.zeros_like(dbias_ref)
            dg_ref[...] = jnp.zeros_like(dg_ref)
            dbeta_ref[...] = jnp.zeros_like(dbeta_ref)

        bu = p_ref[:, 0:D_SGU]
        bv = p_ref[:, D_SGU:2 * D_SGU]
        u = _gelu(bu)
        gv = g_ref[...]
        xhat, rstd, vn = _sgu_norm(bv, gv, beta_ref[...])
        vb = vn.astype(BF16)
        tril = _tril_mask()
        masks = _head_masks(D_SGU)
        whs = [jnp.where(tril, w_ref[h], 0.0).astype(BF16) for h in range(N_SGU_HEADS)]
        mixed = bias_ref[...]
        for h, m in enumerate(masks):
            mixed = mixed + _dot(whs[h], jnp.where(m, vb, jnp.zeros_like(vb)), _NN)
        dy = dy_ref[...]
        dp_ref[:, 0:D_SGU] = (dy * mixed * _gelu_grad(bu)).astype(BF16)
        dmixed = dy * u
        dbias_ref[...] += dmixed
        dmb = dmixed.astype(BF16)
        dvn = jnp.zeros((CHUNK, D_SGU), F32)
        for h, m in enumerate(masks):
            dmh = jnp.where(m, dmb, jnp.zeros_like(dmb))
            dvn = dvn + _dot(whs[h], dmh, _TN)
            dw_ref[h] += jnp.where(tril, _dot(dmh, vb, _NT), 0.0)
        dg_ref[...] += jnp.sum(dvn * xhat, axis=0, keepdims=True)
        dbeta_ref[...] += jnp.sum(dvn, axis=0, keepdims=True)
        dxhat = dvn * gv
        dvg = rstd * (dxhat - jnp.mean(dxhat, axis=-1, keepdims=True)
                      - xhat * jnp.mean(dxhat * xhat, axis=-1, keepdims=True))
        dp_ref[:, D_SGU:2 * D_SGU] = (dvg * _gelu_grad(bv)).astype(BF16)

        @pl.when(pl.program_id(0) == n_chunks - 1)
        def _():
            chan = lax.broadcasted_iota(jnp.int32, (D_SGU, LANES), 0)
            head = lax.broadcasted_iota(jnp.int32, (D_SGU, LANES), 1)
            to_head = jnp.where(chan // HEAD_DIM == head, 1.0, 0.0).astype(BF16)
            db_ref[...] = _dot(_split_bf16(dbias_ref[...]), jnp.concatenate([to_head, to_head], axis=0), _NN)

    vec = pl.BlockSpec((1, D_SGU), lambda i: (0, 0))
    wspec = pl.BlockSpec((N_SGU_HEADS, CHUNK, CHUNK), lambda i: (0, 0, 0))
    bspec = pl.BlockSpec((CHUNK, D_SGU), lambda i: (0, 0))
    return pl.pallas_call(
        body,
        name=name,
        grid=(n_chunks,),
        in_specs=[pl.BlockSpec((CHUNK, 2 * D_SGU), lambda i: (i, 1)), pl.BlockSpec((CHUNK, D_SGU), lambda i: (i, 0)),
                  vec, vec, wspec, bspec],
        out_specs=[pl.BlockSpec((CHUNK, 2 * D_SGU), lambda i: (i, 0)), wspec,
                   pl.BlockSpec((CHUNK, LANES), lambda i: (0, 0)), vec, vec],
        out_shape=[jax.ShapeDtypeStruct((s, 2 * D_SGU), BF16),
                   jax.ShapeDtypeStruct((N_SGU_HEADS, CHUNK, CHUNK), F32),
                   jax.ShapeDtypeStruct((CHUNK, LANES), F32),
                   jax.ShapeDtypeStruct((1, D_SGU), F32), jax.ShapeDtypeStruct((1, D_SGU), F32)],
        scratch_shapes=[pltpu.VMEM((CHUNK, D_SGU), F32)],
        compiler_params=_cparams(("arbitrary",)),
    )(p_ab, dyb, ln_g, ln_b, w_s, bias)


N_PAIRS = D_SB // LANES
SB_SCALE = HEAD_DIM ** -0.5


def _sb_logits(z, valid):
    nz = -z
    t = jnp.log(1.0 + jnp.exp(jnp.minimum(z, nz)))
    l1 = jnp.minimum(nz, 0.0) - t
    if valid is not None:
        l1 = jnp.where(valid, l1, 0.0)
    return l1, jnp.minimum(z, 0.0) - t


def _split_hi_lo(x):
    hi = lax.bitcast_convert_type(lax.bitcast_convert_type(x, jnp.uint32) & jnp.uint32(0xFFFF0000), F32)
    return jnp.concatenate([hi, x - hi], axis=1)


def _cumsum_operand(keep):
    half = jnp.concatenate([keep.astype(F32), jnp.ones((CHUNK, CHUNK), F32)], axis=1)
    return jnp.concatenate([half, half], axis=0)


Q_BLOCKS_PER_STEP = 4


def _q_blocks_per_step(nq):
    return next(n for n in (Q_BLOCKS_PER_STEP, 2, 1) if nq % n == 0)


def _attn_fwd(qkv, *, name):
    s = qkv.shape[0]
    nq = s // CHUNK
    per_step = _q_blocks_per_step(nq)

    def body(q_ref, k_ref, v_ref, o_ref, t_ref):
        masks = _head_masks(LANES)
        row = lax.broadcasted_iota(jnp.int32, (CHUNK, CHUNK), 0)
        col = lax.broadcasted_iota(jnp.int32, (CHUNK, CHUNK), 1)
        after_op = _cumsum_operand(row > col)
        cmr = col - row
        zc = jnp.zeros((CHUNK, LANES), F32)

        def q_block(sub, _):
            qi = pl.program_id(1) * per_step + sub
            q_rows = pl.ds(pl.multiple_of(sub * CHUNK, CHUNK), CHUNK)
            q = q_ref[q_rows, :] * SB_SCALE
            zero = jnp.zeros_like(q)
            qs = [jnp.where(m, q, zero) for m in masks]

            def blocks(js, carry):
                o, c0, c1 = carry
                kvs, valids = [], []
                for j in js:
                    k0 = pl.multiple_of(jnp.maximum(j, 0) * CHUNK, CHUNK)
                    kvs.append((k_ref[pl.ds(k0, CHUNK), :], v_ref[pl.ds(k0, CHUNK), :]))
                    valids.append(cmr < jnp.where(j >= 0, (qi - j) * CHUNK, -CHUNK))
                units = [(h, b) for b in range(len(js)) for h in range(2)]
                zs = [_dot(qs[h], kvs[b][0], _NT) for h, b in units]
                logits = [_sb_logits(z, valids[b]) for z, (h, b) in zip(zs, units)]
                sums = [_dot(_split_hi_lo(l1), after_op, _NN) for l1, _ in logits]
                cs = [c0, c1]
                probs = []
                for (h, b), (_, lb), sm in zip(units, logits, sums):
                    probs.append(jnp.where(valids[b], jnp.exp(lb + sm[:, :CHUNK] + cs[h]), 0.0))
                    cs[h] = cs[h] + sm[:, CHUNK:]
                for (h, b), a in zip(units, probs):
                    o = o + _dot(a.astype(BF16), jnp.where(masks[h], kvs[b][1], zero), _NN)
                return o, cs[0], cs[1]

            n_four = (qi + 1) // 4
            carry = lax.fori_loop(0, n_four, lambda jj, c: blocks([qi - 4 * jj - i for i in range(4)], c), (zc,) * 3)
            top = qi - 4 * n_four
            o, c0, c1 = lax.fori_loop(0, (top + 2) // 2, lambda jj, c: blocks([top - 2 * jj, top - 2 * jj - 1], c), carry)
            o_ref[q_rows, :] = o
            t_ref[q_rows, 0:LANES] = c0
            t_ref[q_rows, LANES:2 * LANES] = c1
            return 0

        lax.fori_loop(0, per_step, q_block, 0)

    rows = per_step * CHUNK
    return pl.pallas_call(
        body,
        name=name,
        grid=(N_PAIRS, nq // per_step),
        in_specs=[pl.BlockSpec((rows, LANES), lambda p, i: (i, p)),
                  pl.BlockSpec((s, LANES), lambda p, i: (0, N_PAIRS + p)),
                  pl.BlockSpec((s, LANES), lambda p, i: (0, 2 * N_PAIRS + p))],
        out_specs=[pl.BlockSpec((rows, LANES), lambda p, i: (i, p)),
                   pl.BlockSpec((rows, 2 * LANES), lambda p, i: (i, p))],
        out_shape=[jax.ShapeDtypeStruct((s, D_SB), F32), jax.ShapeDtypeStruct((s, 2 * D_SB), F32)],
        compiler_params=_cparams(("parallel", "parallel")),
    )(qkv, qkv, qkv)


def _attn_bwd(qkv, t_tot, do, *, name):
    s = qkv.shape[0]
    nq = s // CHUNK
    per_step = _q_blocks_per_step(nq)

    def body(q_ref, k_ref, v_ref, t_ref, do_ref, dq_ref, dk_ref, dv_ref):
        @pl.when(pl.program_id(1) == 0)
        def _():
            dk_ref[...] = jnp.zeros_like(dk_ref)
            dv_ref[...] = jnp.zeros_like(dv_ref)

        masks = _head_masks(LANES)
        row = lax.broadcasted_iota(jnp.int32, (CHUNK, CHUNK), 0)
        col = lax.broadcasted_iota(jnp.int32, (CHUNK, CHUNK), 1)
        upto_op = _cumsum_operand(row <= col)
        before_op = _cumsum_operand(row < col)
        cmr = col - row
        zc = jnp.zeros((CHUNK, LANES), F32)

        def q_block(sub, _):
            qi = pl.program_id(1) * per_step + sub
            q_rows = pl.ds(pl.multiple_of(sub * CHUNK, CHUNK), CHUNK)
            q = q_ref[q_rows, :] * SB_SCALE
            dob = do_ref[q_rows, :].astype(BF16)
            zero = jnp.zeros_like(q)
            qs = [jnp.where(m, q, zero) for m in masks]
            dos = [jnp.where(m, dob, zero) for m in masks]
            tots = [t_ref[q_rows, 0:LANES], t_ref[q_rows, LANES:2 * LANES]]

            def blocks(js, carry):
                dq, cl0, cl1, cp0, cp1 = carry
                starts = [pl.multiple_of(jnp.minimum(j, nq - 1) * CHUNK, CHUNK) for j in js]
                valids = [cmr < (qi - j) * CHUNK for j in js]
                kvs = [(k_ref[pl.ds(k0, CHUNK), :], v_ref[pl.ds(k0, CHUNK), :]) for k0 in starts]
                units = [(h, b) for b in range(len(js)) for h in range(2)]
                zs = [_dot(qs[h], kvs[b][0], _NT) for h, b in units]
                das = [_dot(dos[h], kvs[b][1], _NT) for h, b in units]
                logits = [_sb_logits(z, valids[b]) for z, (h, b) in zip(zs, units)]
                sums = [_dot(_split_hi_lo(l1), upto_op, _NN) for l1, _ in logits]
                cls, cps = [cl0, cl1], [cp0, cp1]
                probs, gs = [], []
                for (h, b), (_, lb), sm, da in zip(units, logits, sums, das):
                    a = jnp.where(valids[b], jnp.exp(lb + (tots[h] - cls[h] - sm[:, :CHUNK])), 0.0)
                    probs.append(a)
                    gs.append(a * da)
                    cls[h] = cls[h] + sm[:, CHUNK:]
                sums_g = [_dot(_split_hi_lo(g), before_op, _NN) for g in gs]
                dzs = []
                for (h, b), (_, lb), g, sg in zip(units, logits, gs, sums_g):
                    dz = g - (g + sg[:, :CHUNK] + cps[h]) * jnp.exp(lb)
                    dzs.append(jnp.where(valids[b], dz, 0.0).astype(BF16))
                    cps[h] = cps[h] + sg[:, CHUNK:]
                for (h, b), dzb in zip(units, dzs):
                    dq = dq + _dot(dzb, jnp.where(masks[h], kvs[b][0], zero), _NN)
                for b, k0 in enumerate(starts):
                    dk_ref[pl.ds(k0, CHUNK), :] += _dot(dzs[2 * b], qs[0], _TN) + _dot(dzs[2 * b + 1], qs[1], _TN)
                    dv_ref[pl.ds(k0, CHUNK), :] += (_dot(probs[2 * b].astype(BF16), dos[0], _TN)
                                                    + _dot(probs[2 * b + 1].astype(BF16), dos[1], _TN))
                return dq, cls[0], cls[1], cps[0], cps[1]

            n_four = (qi + 1) // 4
            carry = lax.fori_loop(0, n_four, lambda jj, c: blocks([4 * jj + i for i in range(4)], c), (zc,) * 5)
            base = 4 * n_four
            carry = lax.fori_loop(0, (qi - base + 2) // 2, lambda jj, c: blocks([base + 2 * jj, base + 2 * jj + 1], c), carry)
            dq_ref[q_rows, :] = (carry[0] * SB_SCALE).astype(BF16)
            return 0

        lax.fori_loop(0, per_step, q_block, 0)

    rows = per_step * CHUNK
    blk = pl.BlockSpec((rows, LANES), lambda p, i: (i, p))
    col_blk = pl.BlockSpec((s, LANES), lambda p, i: (0, p))
    out = jax.ShapeDtypeStruct((s, D_SB), F32)
    return pl.pallas_call(
        body,
        name=name,
        grid=(N_PAIRS, nq // per_step),
        in_specs=[blk,
                  pl.BlockSpec((s, LANES), lambda p, i: (0, N_PAIRS + p)),
                  pl.BlockSpec((s, LANES), lambda p, i: (0, 2 * N_PAIRS + p)),
                  pl.BlockSpec((rows, 2 * LANES), lambda p, i: (i, p)),
                  blk],
        out_specs=[blk, col_blk, col_blk],
        out_shape=[jax.ShapeDtypeStruct((s, D_SB), BF16), out, out],
        compiler_params=_cparams(("parallel", "arbitrary")),
    )(qkv, qkv, qkv, t_tot, do)


FFN_TILE = 256
FFN_COLS = 256
N_FF_BLOCKS = D_FF // FFN_COLS


def _ffn_act_fwd(up0, conv_w, conv_b, *, name):
    s = up0.shape[0]
    nt = s // FFN_TILE

    def body(xg_ref, xv_ref, wg_ref, wv_ref, bg_ref, bv_ref, act_ref, pg_ref, pv_ref):
        pg_ref[0:FFN_HALO, :] = jnp.zeros((FFN_HALO, FFN_COLS), F32)
        pv_ref[0:FFN_HALO, :] = jnp.zeros((FFN_HALO, FFN_COLS), F32)
        pg_ref[FFN_HALO:, :] = xg_ref[...].astype(F32)
        pv_ref[FFN_HALO:, :] = xv_ref[...].astype(F32)

        def tile(i, c):
            t0 = pl.multiple_of(i * FFN_TILE, FFN_TILE)
            outs = []
            for p_ref, w_ref, b_ref in ((pg_ref, wg_ref, bg_ref), (pv_ref, wv_ref, bv_ref)):
                window = p_ref[pl.ds(t0, FFN_TILE + FFN_HALO), :]
                acc = b_ref[...] + w_ref[2:3, :] * window[FFN_HALO:, :]
                for j in range(1, FFN_K):
                    acc = acc + w_ref[FFN_K - 1 - j:FFN_K - j, :] * _shift_down(window, j, FFN_HALO)
                outs.append(acc)
            gate, val = outs
            act_ref[pl.ds(t0, FFN_TILE), :] = (gate * _sigmoid(gate) * val).astype(BF16)
            return c

        lax.fori_loop(0, nt, tile, 0)

    gcol = lambda rows: pl.BlockSpec((rows, FFN_COLS), lambda j: (0, j))
    vcol = lambda rows: pl.BlockSpec((rows, FFN_COLS), lambda j: (0, j + N_FF_BLOCKS))
    return pl.pallas_call(
        body,
        name=name,
        grid=(N_FF_BLOCKS,),
        in_specs=[gcol(s), vcol(s), gcol(FFN_K), vcol(FFN_K), gcol(1), vcol(1)],
        out_specs=gcol(s),
        out_shape=jax.ShapeDtypeStruct((s, D_FF), BF16),
        scratch_shapes=[pltpu.VMEM((s + FFN_HALO, FFN_COLS), F32), pltpu.VMEM((s + FFN_HALO, FFN_COLS), F32)],
        compiler_params=_cparams(("parallel",)),
    )(up0, up0, conv_w, conv_w, conv_b, conv_b)


def _ffn_act_bwd(up0, dact, conv_w, conv_b, *, name):
    s = up0.shape[0]
    nt = s // FFN_TILE

    def body(xg_ref, xv_ref, da_ref, wg_ref, wv_ref, bg_ref, bv_ref, dxg_ref, dxv_ref, dwg_ref, dwv_ref, dbg_ref, dbv_ref,
             pg_ref, pv_ref, dg_ref, dv_ref):
        zeros = jnp.zeros((FFN_HALO, FFN_COLS), F32)
        for p_ref, x_ref in ((pg_ref, xg_ref), (pv_ref, xv_ref)):
            p_ref[0:FFN_HALO, :] = zeros
            p_ref[FFN_HALO:, :] = x_ref[...].astype(F32)
        dg_ref[s:s + FFN_HALO, :] = zeros
        dv_ref[s:s + FFN_HALO, :] = zeros
        for ref in (dwg_ref, dwv_ref, dbg_ref, dbv_ref):
            ref[...] = jnp.zeros_like(ref)

        def conv(p_ref, w_ref, b_ref, t0):
            window = p_ref[pl.ds(t0, FFN_TILE + FFN_HALO), :]
            taps = [_shift_down(window, j, FFN_HALO) for j in range(FFN_K)]
            out = b_ref[...]
            for j in range(FFN_K):
                out = out + w_ref[FFN_K - 1 - j:FFN_K - j, :] * taps[j]
            return out, taps

        def tile(i, c):
            t0 = pl.multiple_of(i * FFN_TILE, FFN_TILE)
            gate, taps_g = conv(pg_ref, wg_ref, bg_ref, t0)
            val, taps_v = conv(pv_ref, wv_ref, bv_ref, t0)
            da = da_ref[pl.ds(t0, FFN_TILE), :].astype(F32)
            sg = lax.logistic(gate)
            dgate = da * val * (sg * (1.0 + gate * (1.0 - sg)))
            dval = da * gate * sg
            dg_ref[pl.ds(t0, FFN_TILE), :] = dgate
            dv_ref[pl.ds(t0, FFN_TILE), :] = dval
            dbg_ref[...] += jnp.sum(dgate, axis=0, keepdims=True)
            dbv_ref[...] += jnp.sum(dval, axis=0, keepdims=True)
            for j in range(FFN_K):
                dwg_ref[FFN_K - 1 - j:FFN_K - j, :] += jnp.sum(dgate * taps_g[j], axis=0, keepdims=True)
                dwv_ref[FFN_K - 1 - j:FFN_K - j, :] += jnp.sum(dval * taps_v[j], axis=0, keepdims=True)
            return c

        lax.fori_loop(0, nt, tile, 0)

        def tile_dx(i, c):
            t0 = pl.multiple_of(i * FFN_TILE, FFN_TILE)
            for d_ref, w_ref, dx_ref in ((dg_ref, wg_ref, dxg_ref), (dv_ref, wv_ref, dxv_ref)):
                window = d_ref[pl.ds(t0, FFN_TILE + FFN_HALO), :]
                dx = w_ref[FFN_K - 1:FFN_K, :] * window[:FFN_TILE, :]
                for j in range(1, FFN_K):
                    dx = dx + w_ref[FFN_K - 1 - j:FFN_K - j, :] * _shift_up(window, j, FFN_TILE)
                dx_ref[pl.ds(t0, FFN_TILE), :] = dx.astype(BF16)
            return c

        lax.fori_loop(0, nt, tile_dx, 0)

    gcol = lambda rows: pl.BlockSpec((rows, FFN_COLS), lambda j: (0, j))
    vcol = lambda rows: pl.BlockSpec((rows, FFN_COLS), lambda j: (0, j + N_FF_BLOCKS))
    half = lambda rows, dtype: jax.ShapeDtypeStruct((rows, D_FF), dtype)
    padded = pltpu.VMEM((s + FFN_HALO, FFN_COLS), F32)
    return pl.pallas_call(
        body,
        name=name,
        grid=(N_FF_BLOCKS,),
        in_specs=[gcol(s), vcol(s), gcol(s), gcol(FFN_K), vcol(FFN_K), gcol(1), vcol(1)],
        out_specs=[gcol(s), gcol(s), gcol(FFN_K), gcol(FFN_K), gcol(1), gcol(1)],
        out_shape=[half(s, BF16), half(s, BF16), half(FFN_K, F32), half(FFN_K, F32), half(1, F32), half(1, F32)],
        scratch_shapes=[padded, padded, padded, padded],
        compiler_params=_cparams(("parallel",)),
    )(up0, up0, dact, conv_w, conv_w, conv_b, conv_b)


MESH = pl.DeviceIdType.MESH


def _position():
    x, y, c = lax.axis_index("x"), lax.axis_index("y"), lax.axis_index("c")
    return x, y, c, 4 * x + 2 * y + c


def _peer(k):
    x, y, c, _ = _position()
    px = 1 - x if k & 4 else x
    py = 1 - y if k & 2 else y
    pc = 1 - c if k & 1 else c
    return (px, py, pc), 4 * px + 2 * py + pc


_HBM = pl.BlockSpec(memory_space=pltpu.HBM)
_SEM = pl.BlockSpec(memory_space=pltpu.SEMAPHORE)
_DATAFLOW = pltpu.SideEffectType.DATAFLOW_SIDE_EFFECTING
N_PEERS = N_DEV - 1


class _SplitExchange:
    def __init__(self, src, *, kind, name):
        self.kind, self.name, self.dtype = kind, name, src.dtype
        scatter = kind.startswith("scatter")
        by_blocks = kind.endswith("blocks")
        self.scatter, self.by_blocks = scatter, by_blocks
        if by_blocks:
            self.r, self.cols, self.land_shape = None, None, src.shape if scatter else (N_DEV,) + src.shape
        else:
            self.r = src.shape[0] // N_DEV if scatter else src.shape[0]
            self.cols = src.shape[1]
            self.land_shape = (N_DEV, self.r, self.cols) if scatter else (N_DEV * self.r, self.cols)
        r = self.r

        def copies(src_ref, land_ref, send_sems, recv_sems, local_sem):
            me = _position()[3]

            def rows(ref, idx):
                return ref.at[pl.ds(pl.multiple_of(idx * r, r), r), :]

            if not scatter:
                outgoing = lambda idx: src_ref
            else:
                outgoing = (lambda idx: src_ref.at[idx]) if by_blocks else (lambda idx: rows(src_ref, idx))
            slot = (lambda idx: land_ref.at[idx]) if (scatter or by_blocks) else (lambda idx: rows(land_ref, idx))
            sends, recvs = [], []
            for k in range(1, N_DEV):
                peer, pidx = _peer(k)
                sems = dict(send_sem=send_sems[k - 1], recv_sem=recv_sems[k - 1], device_id=peer, device_id_type=MESH)
                sends.append(pltpu.make_async_remote_copy(src_ref=outgoing(pidx), dst_ref=slot(me), **sems))
                recvs.append(pltpu.make_async_remote_copy(src_ref=outgoing(pidx), dst_ref=slot(pidx), **sems))
            return sends, recvs, pltpu.make_async_copy(outgoing(me), slot(me), local_sem)

        self._copies = copies
        self.src = src

    @staticmethod
    def start(exchanges, name):
        n = len(exchanges)
        per = 2 * N_PEERS + 1

        def start_body(*refs):
            outs = refs[2 * n:]
            for i, ex in enumerate(exchanges):
                sems = outs[per * i:per * (i + 1)]
                sends, _, local = ex._copies(refs[2 * i], refs[2 * i + 1], sems[:N_PEERS], sems[N_PEERS:-1], sems[-1])
                for cp in sends + [local]:
                    cp.start()
            outs[-1][...] = jnp.zeros_like(outs[-1])

        sem = pltpu.SemaphoreType.DMA(())
        operands, thru_shapes = [], []
        for ex in exchanges:
            operands += [pltpu.with_memory_space_constraint(ex.src, pltpu.HBM),
                         pltpu.with_memory_space_constraint(lax.empty(ex.land_shape, ex.dtype), pltpu.HBM)]
            thru_shapes += [pltpu.HBM(ex.src.shape, ex.dtype), pltpu.HBM(ex.land_shape, ex.dtype)]
        out = pl.pallas_call(
            start_body,
            name=name,
            in_specs=(_HBM,) * (2 * n),
            out_specs=(_SEM,) * (per * n) + (_HBM,) * (2 * n) + (pl.BlockSpec(memory_space=pltpu.VMEM),),
            out_shape=(sem,) * (per * n) + tuple(thru_shapes) + (jax.ShapeDtypeStruct((8, LANES), F32),),
            input_output_aliases={i: per * n + i for i in range(2 * n)},
            compiler_params=pltpu.CompilerParams(has_side_effects=_DATAFLOW),
        )(*operands)
        for i, ex in enumerate(exchanges):
            ex.sems = out[per * i:per * (i + 1)]
            ex.src_thru, ex.land_thru = out[per * n + 2 * i], out[per * n + 2 * i + 1]
        return out[-1][0, 0]

    def finish(self, after):
        copies = self._copies

        def wait_body(src_ref, land_ref, *rest):
            sends, recvs, local = copies(src_ref, land_ref, rest[:N_PEERS], rest[N_PEERS:2 * N_PEERS], rest[2 * N_PEERS])
            for cp in sends:
                cp.wait_send()
            for cp in recvs:
                cp.wait_recv()
            local.wait()

        return pl.pallas_call(
            wait_body,
            name=f"{self.name}_wait",
            in_specs=(_HBM, _HBM) + (_SEM,) * (2 * N_PEERS + 1) + (pl.BlockSpec(memory_space=pl.ANY),),
            out_specs=(_HBM, _HBM),
            out_shape=(pltpu.HBM(self.src_thru.shape, self.dtype), pltpu.HBM(self.land_shape, self.dtype)),
            input_output_aliases={0: 0, 1: 1},
            compiler_params=pltpu.CompilerParams(has_side_effects=_DATAFLOW),
        )(self.src_thru, self.land_thru, *self.sems, after)[1]


def _row_tile(rows):
    return _tile(rows, (256, 128, 64, 32, 16, 8))


def _layer_parts_specs(n_layers, n_parts, tr, cols):
    return [pl.BlockSpec((n_parts, tr, cols), lambda l, i, j=j: (0, jnp.where(l == j, i, 0), 0)) for j in range(n_layers)]


def _select_layer_sum(p_refs):
    l = pl.program_id(0)
    g = None
    for j, p_ref in enumerate(p_refs):
        gj = p_ref[0].astype(F32)
        for k in range(1, p_ref.shape[0]):
            gj = gj + p_ref[k].astype(F32)
        g = gj if g is None else jnp.where(l == j, gj, g)
    return g


def _adamw(parts, w, m, v, *, name):
    n_layers, rows, cols = w.shape
    tr = _row_tile(rows)

    def body(*refs):
        w_ref, m_ref, v_ref, g_ref, d_ref, m2_ref, v2_ref = refs[n_layers:]
        g = _select_layer_sum(refs[:n_layers])
        m2 = ADAM_B1 * m_ref[...] + (1.0 - ADAM_B1) * g
        v2 = ADAM_B2 * v_ref[...] + (1.0 - ADAM_B2) * (g * g)
        m_hat = m2 / (1.0 - ADAM_B1 ** ADAM_STEP)
        v_hat = v2 / (1.0 - ADAM_B2 ** ADAM_STEP)
        g_ref[...] = g
        d_ref[...] = -ADAM_LR * (m_hat / (jnp.sqrt(v_hat) + ADAM_EPS) + ADAM_WD * w_ref[...])
        m2_ref[...] = m2
        v2_ref[...] = v2

    slab = pl.BlockSpec((None, tr, cols), lambda l, i: (l, i, 0))
    out = jax.ShapeDtypeStruct((n_layers, rows, cols), F32)
    p_specs = _layer_parts_specs(n_layers, parts[0].shape[0], tr, cols)
    return pl.pallas_call(
        body,
        name=name,
        grid=(n_layers, rows // tr),
        in_specs=p_specs + [slab, slab, slab],
        out_specs=[slab, slab, slab, slab],
        out_shape=[out, out, out, out],
        compiler_params=_cparams(("arbitrary", "arbitrary")),
    )(*parts, w, m, v)


SLAB_ROWS = 256
_SMALL_SHARDED = (("conv_w", (2, 31, 32)), ("ffn_conv_w", (2, 3, 704)))
_REPLICATED = (("g_mix", (2, 1024)), ("conv_b", (2, 256)), ("conv_ln_g", (2, 256)), ("conv_ln_b", (2, 256)),
               ("sgu_ln_g", (2, 256)), ("sgu_ln_b", (2, 256)), ("sgu_w", (2, 4, 128, 128)), ("sgu_b", (2, 4, 128)),
               ("g_out", (2, 1024)), ("g_ffn", (2, 1024)), ("ffn_conv_b", (2, 5632)), ("g_final", (1024,)))


def _seg_rows(n_elems):
    return -(-n_elems // LANES)


def _pack(arrays, lead=()):
    segs = []
    for a in arrays:
        flat = a.reshape(lead + (-1,)).astype(F32)
        pad = _seg_rows(flat.shape[-1]) * LANES - flat.shape[-1]
        if pad:
            flat = jnp.pad(flat, [(0, 0)] * len(lead) + [(0, pad)])
        segs.append(flat)
    flat = jnp.concatenate(segs, axis=-1)
    rows = flat.shape[-1] // LANES
    pad_rows = -rows % SLAB_ROWS
    if pad_rows:
        flat = jnp.pad(flat, [(0, 0)] * len(lead) + [(0, pad_rows * LANES)])
    return flat.reshape(lead + (rows + pad_rows, LANES))


def _unpack(slab, shapes, lead=()):
    flat = slab.reshape(lead + (-1,))
    out, off = [], 0
    for shape in shapes:
        n = math.prod(shape)
        out.append(flat[..., off:off + n].reshape(lead + tuple(shape)))
        off += _seg_rows(n) * LANES
    return out


def _split_last(full):
    split = full.shape[:-1] + (N_DEV, full.shape[-1] // N_DEV)
    return jnp.moveaxis(full.reshape(split), -2, 0)


def _join_last(blocks):
    moved = jnp.moveaxis(blocks, 0, -2)
    return moved.reshape(moved.shape[:-2] + (moved.shape[-2] * moved.shape[-1],))


def _gathered(wt, n, l, after):
    if isinstance(wt[n][l], _SplitExchange):
        wt[n][l] = wt[n][l].finish(after)
    return wt[n][l]


def _layer_fwd(l, x, wt, small):
    tag = f"l{l}"
    h = _rmsnorm_fwd(x, small["g_mix"][l][None], name=f"{tag}_norm_mix")
    w_in_t = _gathered(wt, "w_in_t", l, h)
    p_ab = _matmul(h, w_in_t, "nt", name=f"{tag}_proj_ab", n=D_AB)
    qkv = _matmul(h, w_in_t, "nt", name=f"{tag}_proj_qkv", n=D_QKV, b_n0=D_AB, out_dtype=BF16)
    ya = _mixer_a_fwd(p_ab, wt["conv_w"][l], small["conv_b"][l][None], small["conv_ln_g"][l][None],
                      small["conv_ln_b"][l][None], name=f"{tag}_mixer_a")
    bias = jnp.repeat(small["sgu_b"][l].T, HEAD_DIM, axis=1)
    yb = _sgu_fwd(p_ab, small["sgu_ln_g"][l][None], small["sgu_ln_b"][l][None], small["sgu_w"][l], bias,
                  name=f"{tag}_sgu")
    yc, t_tot = _attn_fwd(qkv, name=f"{tag}_attn")
    y = _combine_fwd(ya, yb, yc, small["g_out"][l][None], name=f"{tag}_combine")
    x1 = _matmul(y, _gathered(wt, "w_out", l, y), "nn", name=f"{tag}_out_proj", residual=x)
    h2 = _rmsnorm_fwd(x1, small["g_ffn"][l][None], name=f"{tag}_norm_ffn")
    up0 = _matmul(h2, _gathered(wt, "w_up_t", l, h2), "nt", name=f"{tag}_up", out_dtype=BF16)
    act = _ffn_act_fwd(up0, wt["ffn_conv_w"][l], small["ffn_conv_b"][l][None], name=f"{tag}_ffn_act")
    x2 = _matmul(act, _gathered(wt, "w_down", l, act), "nn", name=f"{tag}_down", residual=x1)
    saved = dict(x=x, h=h, p_ab=p_ab, qkv=qkv, ya=ya, yb=yb, yc=yc, t_tot=t_tot, y=y, x1=x1, h2=h2, up0=up0,
                 act=act, bias=bias)
    return x2, saved


def _layer_bwd(l, dres, sv, wt, small, scattering, token):
    tag = f"l{l}b"
    g = {}

    def scatter(n, partial):
        scattering[n][l] = _SplitExchange(partial, kind="scatter_rows", name=f"scatter_{n}_l{l}")
        return _SplitExchange.start([scattering[n][l]], name=f"scatter_{n}_l{l}_start")

    dx2, dx2_b = dres
    dact = _matmul(dx2_b, wt["w_down"][l], "nt", name=f"{tag}_dact", out_dtype=BF16)
    tok = scatter("w_down", _matmul(sv["act"], dx2_b, "tn", name=f"{tag}_dw_down", out_dtype=BF16))
    dup_g, dup_v, dwg, dwv, dbg, dbv = _ffn_act_bwd(sv["up0"], dact, wt["ffn_conv_w"][l], small["ffn_conv_b"][l][None] + tok + token,
                                                    name=f"{tag}_ffn_act")
    g["ffn_conv_w"] = jnp.concatenate([dwg, dwv], axis=1)
    g["ffn_conv_b"] = jnp.concatenate([dbg[0], dbv[0]])
    dh2 = _matmul(dup_g, wt["w_up_t"][l], "nn", name=f"{tag}_dh2_gate")
    dh2 = _matmul(dup_v, wt["w_up_t"][l], "nn", name=f"{tag}_dh2_val", b_k0=D_FF, residual=dh2)
    dw_up = _matmul(dup_g, sv["h2"], "tn", name=f"{tag}_dw_up_gate", out_dtype=BF16, rows=(2 * D_FF, 0))
    dw_up = _matmul(dup_v, sv["h2"], "tn", name=f"{tag}_dw_up_val", out_dtype=BF16, rows=(2 * D_FF, D_FF), into=dw_up)
    tok = scatter("w_up_t", dw_up)
    dx1, dx1_b, dg = _rmsnorm_bwd(sv["x1"], small["g_ffn"][l][None] + tok, dh2, dx2, name=f"{tag}_norm_ffn")
    g["g_ffn"] = dg[0]
    dy = _matmul(dx1_b, wt["w_out"][l], "nt", name=f"{tag}_dy")
    tok = scatter("w_out", _matmul(sv["y"], dx1_b, "tn", name=f"{tag}_dw_out", out_dtype=BF16))
    dya, dyb, dyc, dg = _combine_bwd(dy, sv["ya"], sv["yb"], sv["yc"], small["g_out"][l][None] + tok,
                                     name=f"{tag}_combine")
    g["g_out"] = dg[0]
    dq, dk, dv = _attn_bwd(sv["qkv"], sv["t_tot"], dyc, name=f"{tag}_attn")
    dp_b, g["sgu_w"], db, dg, dbeta = _sgu_bwd(sv["p_ab"], dyb, small["sgu_ln_g"][l][None], small["sgu_ln_b"][l][None],
                                               small["sgu_w"][l], sv["bias"], name=f"{tag}_sgu")
    g["sgu_b"] = db[:, :N_SGU_HEADS].T
    g["sgu_ln_g"], g["sgu_ln_b"] = dg[0], dbeta[0]
    dp_a, g["conv_w"], dcb, dg, dbeta = _mixer_a_bwd(sv["p_ab"], dya, wt["conv_w"][l], small["conv_b"][l][None],
                                                     small["conv_ln_g"][l][None], small["conv_ln_b"][l][None],
                                                     name=f"{tag}_mixer_a")
    g["conv_b"], g["conv_ln_g"], g["conv_ln_b"] = dcb[0], dg[0], dbeta[0]
    dp = jnp.concatenate([dp_a, dp_b, dq, dk.astype(BF16), dv.astype(BF16)], axis=1)
    tok = scatter("w_in_t", _matmul(dp, sv["h"], "tn", name=f"{tag}_dw_in", out_dtype=BF16))
    dh = _matmul(dp, wt["w_in_t"][l], "nn", name=f"{tag}_dh")
    dx, dx_b, dg = _rmsnorm_bwd(sv["x"], small["g_mix"][l][None] + tok, dh, dx1, name=f"{tag}_norm_mix")
    g["g_mix"] = dg[0]
    return (dx, dx_b), g


_BIG = ("w_in_t", "w_out", "w_up_t", "w_down")


def kernel(x, g_mix, w_in, conv_w, conv_b, conv_ln_g, conv_ln_b, sgu_ln_g, sgu_ln_b, sgu_w, sgu_b, g_out, w_out, g_ffn, w_up, ffn_conv_w, ffn_conv_b, w_down, g_final, loss_target, m_g_mix, m_w_in, m_conv_w, m_conv_b, m_conv_ln_g, m_conv_ln_b, m_sgu_ln_g, m_sgu_ln_b, m_sgu_w, m_sgu_b, m_g_out, m_w_out, m_g_ffn, m_w_up, m_ffn_conv_w, m_ffn_conv_b, m_w_down, m_g_final, v_g_mix, v_w_in, v_conv_w, v_conv_b, v_conv_ln_g, v_conv_ln_b, v_sgu_ln_g, v_sgu_ln_b, v_sgu_w, v_sgu_b, v_g_out, v_w_out, v_g_ffn, v_w_up, v_ffn_conv_w, v_ffn_conv_b, v_w_down, v_g_final):
    given = dict(locals())
    n_layers = g_mix.shape[0]
    layers = range(n_layers)
    small_sharded = [n for n, _ in _SMALL_SHARDED]
    replicated = [n for n, _ in _REPLICATED]
    small = {n: given[n] for n in replicated}

    filters = _SplitExchange(_pack([given[n] for n in small_sharded]), kind="gather_blocks", name="gather_filters")
    wt = {n: [None] * n_layers for n in _BIG}
    wt["w_in_t"][0] = _SplitExchange(w_in[0].T.astype(BF16), kind="gather_rows", name="gather_w_in_t_l0")
    tok = _SplitExchange.start([filters, wt["w_in_t"][0]], name="gather_first_start")
    w_in, w_out, w_up, w_down, tok = lax.optimization_barrier((w_in, w_out, w_up, w_down, tok))
    shard = {"w_in_t": [w_in[l].T.astype(BF16) for l in layers], "w_out": [w_out[l].astype(BF16) for l in layers],
             "w_up_t": [w_up[l].T.astype(BF16) for l in layers], "w_down": [w_down[l].astype(BF16) for l in layers]}
    later = [(n, l) for l in layers for n in _BIG if (n, l) != ("w_in_t", 0)]
    for n, l in later:
        wt[n][l] = _SplitExchange(shard[n][l], kind="gather_rows", name=f"gather_{n}_l{l}")
    small["g_mix"] = g_mix + tok + _SplitExchange.start([wt[n][l] for n, l in later], name="gather_weights_start")
    gathered_filters = filters.finish(small["g_mix"])
    for n, blocks in zip(small_sharded, _unpack(gathered_filters, [s for _, s in _SMALL_SHARDED], lead=(N_DEV,))):
        wt[n] = _join_last(blocks)

    xs = x[0]
    saved = []
    for l in layers:
        xs, sv = _layer_fwd(l, xs, wt, small)
        saved.append(sv)
    loss_tile, dx, dx_b, dgf = _loss_head(xs, g_final[None], loss_target[0], name="loss_head")
    dres = (dx, dx_b)
    scattering = {n: [None] * n_layers for n in _BIG}
    layered = [n for n in replicated if n != "g_final"]
    slabs = [None] * n_layers
    tok = 0.0
    for l in reversed(layers):
        dres, g = _layer_bwd(l, dres, saved[l], wt, small, scattering, tok)
        own = _pack([_split_last(g[n]) for n in small_sharded], lead=(N_DEV,))
        shared = _pack([g[n] for n in layered] + ([dgf[0]] if l == n_layers - 1 else []))
        slab = jnp.concatenate([own, jnp.broadcast_to(shared[None], (N_DEV,) + shared.shape)], axis=1)
        slabs[l] = _SplitExchange(slab, kind="scatter_blocks", name=f"scatter_small_grads_l{l}")
        tok = _SplitExchange.start([slabs[l]], name=f"scatter_small_grads_l{l}_start")
    n_own = own.shape[1]

    after_backward = jnp.full((8, LANES), tok)
    received = {n: [scattering[n][l].finish(after_backward) for l in layers] for n in _BIG}
    out = {}

    def update(n, parts, transposed=False):
        turn = (lambda a: jnp.swapaxes(a, 1, 2)) if transposed else (lambda a: a)
        results = _adamw(parts, turn(given[n]), turn(given["m_" + n]), turn(given["v_" + n]), name=f"adamw_{n}")
        for pre, res in zip(("grad_", "delta_", "new_m_", "new_v_"), results):
            out[pre + n] = turn(res)
        return results[0][0, :8, :LANES]

    update("w_out", received["w_out"])
    update("w_down", received["w_down"])
    update("w_in", received["w_in_t"], transposed=True)
    big_updated = update("w_up", received["w_up_t"], transposed=True)

    per_layer = {pre + n: [None] * n_layers for pre in ("grad_", "delta_", "new_m_", "new_v_") for n in small_sharded + layered}
    for l in reversed(layers):
        last = l == n_layers - 1
        stacks = [jnp.concatenate([_pack([given[pre + n][l] for n in small_sharded]),
                                   _pack([given[pre + n][l] for n in layered] + ([given[pre + "g_final"]] if last else []))])[None]
                  for pre in ("", "m_", "v_")]
        results = _adamw([slabs[l].finish(big_updated)], *stacks, name=f"adamw_small_l{l}")
        for pre, res in zip(("grad_", "delta_", "new_m_", "new_v_"), results):
            unpacked = (_unpack(res[0, :n_own], [s[1:] for _, s in _SMALL_SHARDED])
                        + _unpack(res[0, n_own:], [s[1:] for n, s in _REPLICATED if n != "g_final"] + ([g_final.shape] if last else [])))
            for n, a in zip(small_sharded + layered + (["g_final"] if last else []), unpacked):
                if n == "g_final":
                    out[pre + n] = a
                else:
                    per_layer[pre + n][l] = a
    for name, parts in per_layer.items():
        out[name] = jnp.stack(parts)

    loss = lax.psum(loss_tile[0, 0], ("x", "y", "c"))
    order = list(_WEIGHT_ORDER)
    return (loss, dres[0][None], *[out["grad_" + n] for n in order], *[out["delta_" + n] for n in order],
            *[out["new_m_" + n] for n in order], *[out["new_v_" + n] for n in order])


_WEIGHT_ORDER = ("g_mix", "w_in", "conv_w", "conv_b", "conv_ln_g", "conv_ln_b", "sgu_ln_g", "sgu_ln_b", "sgu_w", "sgu_b",
                 "g_out", "w_out", "g_ffn", "w_up", "ffn_conv_w", "ffn_conv_b", "w_down", "g_final")
```

```python
import math

import jax
import jax.numpy as jnp
from jax import lax
from jax.experimental import pallas as pl
from jax.experimental.pallas import tpu as pltpu

F32 = jnp.float32
BF16 = jnp.bfloat16

N_DEV = 8
D_MODEL = 1024
HEAD_DIM = 64
D_CONV = 256
D_SGU = 256
D_SB = 512
D_AB = 2 * D_CONV + 2 * D_SGU
D_QKV = 3 * D_SB
D_IN = D_AB + D_QKV
CONV_K = 31
CONV_HALO = 32
FFN_K = 3
FFN_HALO = 8
D_FF = 2816
CHUNK = 128
EPS = 1e-6
LANES = 128

ADAM_LR = 0.001
ADAM_B1 = 0.9
ADAM_B2 = 0.999
ADAM_EPS = 1e-08
ADAM_WD = 0.01
ADAM_STEP = 10

VMEM_LIMIT = 56 * 1024 * 1024


def _cparams(sem=None):
    return pltpu.CompilerParams(dimension_semantics=sem, vmem_limit_bytes=VMEM_LIMIT)


def _tile(n, prefs=(512, 256, 128)):
    for t in prefs:
        if n % t == 0:
            return t
    return n


def _sigmoid(x):
    return 1.0 / (1.0 + jnp.exp(-x))


_INV_SQRT2 = 1.0 / math.sqrt(2.0)
_INV_SQRT2PI = 1.0 / math.sqrt(2.0 * math.pi)


def _gelu(x):
    return 0.5 * x * (1.0 + lax.erf(x * _INV_SQRT2))


def _gelu_grad(x):
    return 0.5 * (1.0 + lax.erf(x * _INV_SQRT2)) + x * jnp.exp(-0.5 * x * x) * _INV_SQRT2PI


def _dot(a, b, dims):
    return lax.dot_general(a, b, (dims, ((), ())), preferred_element_type=F32)


_NN = ((1,), (0,))
_NT = ((1,), (1,))
_TN = ((0,), (0,))


def _split_bf16(x):
    hi = x.astype(BF16)
    lo = (x - hi.astype(F32)).astype(BF16)
    return jnp.concatenate([hi, lo], axis=1)


def _matmul(a, b, mode, *, name, out_dtype=F32, residual=None, n=None, b_n0=0, b_k0=0, rows=None, into=None):
    if mode == "nn":
        (m, k), n = a.shape, (n or b.shape[1])
    elif mode == "nt":
        (m, k), n = a.shape, (n or b.shape[0])
    else:
        (k, m), n = a.shape, b.shape[1]
    has_res = residual is not None
    tm, tn = _matmul_tiles(m, n, k, a.dtype.itemsize, b.dtype.itemsize, jnp.dtype(out_dtype).itemsize, has_res, b_n0)
    j0 = b_n0 // tn
    total_rows, first_row = rows or (m, 0)
    assert b_k0 % k == 0 and first_row % tm == 0
    kb, i0 = b_k0 // k, first_row // tm

    if mode == "nn":
        a_spec = pl.BlockSpec((tm, k), lambda i, j: (i, 0))
        b_spec = pl.BlockSpec((k, tn), lambda i, j: (kb, j + j0))
        dims = _NN
    elif mode == "nt":
        a_spec = pl.BlockSpec((tm, k), lambda i, j: (i, 0))
        b_spec = pl.BlockSpec((tn, k), lambda i, j: (j + j0, 0))
        dims = _NT
    else:
        a_spec = pl.BlockSpec((k, tm), lambda i, j: (0, i))
        b_spec = pl.BlockSpec((k, tn), lambda i, j: (0, j))
        dims = _TN
    o_spec = pl.BlockSpec((tm, tn), lambda i, j: (i + i0, j))
    r_spec = pl.BlockSpec((tm, tn), lambda i, j: (i, j))

    def body(*refs):
        a_ref, b_ref = refs[:2]
        acc = _dot(a_ref[...].astype(BF16), b_ref[...].astype(BF16), dims)
        if has_res:
            acc = acc + refs[2][...]
        refs[-1][...] = acc.astype(out_dtype)

    in_specs = [a_spec, b_spec] + ([r_spec] if has_res else [])
    args = (a, b) + ((residual,) if has_res else ())
    aliases = {}
    if into is not None:
        aliases = {len(args): 0}
        in_specs.append(pl.BlockSpec(memory_space=pl.ANY))
        args += (into,)

        def body(*refs, inner=body):
            inner(*refs[:len(args) - 1], refs[-1])

    return pl.pallas_call(
        body,
        name=name,
        grid=(m // tm, n // tn),
        in_specs=in_specs,
        out_specs=o_spec,
        out_shape=jax.ShapeDtypeStruct((total_rows, n), out_dtype),
        input_output_aliases=aliases,
        compiler_params=_cparams(("parallel", "parallel")),
    )(*args)


MATMUL_VMEM_BUDGET = 40 * 1024 * 1024


def _matmul_tiles(m, n, k, a_bytes, b_bytes, out_bytes, has_res, n_offset):
    def divisors(size, cap, also=0):
        return [t for t in range(cap, 0, -LANES) if size % t == 0 and also % t == 0] or [size]

    for tm in divisors(m, 1024):
        for tn in divisors(n, 1408, n_offset):
            blocks = tm * k * a_bytes + k * tn * b_bytes + tm * tn * (out_bytes + (4 if has_res else 0))
            if 2 * blocks <= MATMUL_VMEM_BUDGET:
                return tm, tn
    raise ValueError(f"no matmul tiling for {m} x {n} x {k}")


ROW_TILE = 256


def _rmsnorm_fwd(x, g, *, name):
    s, d = x.shape

    def body(x_ref, g_ref, h_ref):
        xv = x_ref[...]
        r = lax.rsqrt(jnp.mean(xv * xv, axis=-1, keepdims=True) + EPS)
        h_ref[...] = (xv * r * g_ref[...]).astype(BF16)

    return pl.pallas_call(
        body,
        name=name,
        grid=(s // ROW_TILE,),
        in_specs=[pl.BlockSpec((ROW_TILE, d), lambda i: (i, 0)), pl.BlockSpec((1, d), lambda i: (0, 0))],
        out_specs=pl.BlockSpec((ROW_TILE, d), lambda i: (i, 0)),
        out_shape=jax.ShapeDtypeStruct((s, d), BF16),
        compiler_params=_cparams(("parallel",)),
    )(x, g)


def _rmsnorm_bwd(x, g, dh, dres, *, name):
    s, d = x.shape

    def body(x_ref, g_ref, dh_ref, dres_ref, dx_ref, dxb_ref, dg_ref):
        xv = x_ref[...]
        r = lax.rsqrt(jnp.mean(xv * xv, axis=-1, keepdims=True) + EPS)
        xhat = xv * r
        dhv = dh_ref[...]
        dxhat = dhv * g_ref[...]
        dx = dres_ref[...] + r * (dxhat - xhat * jnp.mean(dxhat * xhat, axis=-1, keepdims=True))
        dx_ref[...] = dx
        dxb_ref[...] = dx.astype(BF16)
        part = jnp.sum(dhv * xhat, axis=0, keepdims=True)

        @pl.when(pl.program_id(0) == 0)
        def _():
            dg_ref[...] = part

        @pl.when(pl.program_id(0) > 0)
        def _():
            dg_ref[...] += part

    row = pl.BlockSpec((ROW_TILE, d), lambda i: (i, 0))
    vec = pl.BlockSpec((1, d), lambda i: (0, 0))
    return pl.pallas_call(
        body,
        name=name,
        grid=(s // ROW_TILE,),
        in_specs=[row, vec, row, row],
        out_specs=[row, row, vec],
        out_shape=[jax.ShapeDtypeStruct((s, d), F32), jax.ShapeDtypeStruct((s, d), BF16),
                   jax.ShapeDtypeStruct((1, d), F32)],
        compiler_params=_cparams(("arbitrary",)),
    )(x, g, dh, dres)


def _loss_head(x, g, target, *, name):
    s, d = x.shape

    def body(x_ref, g_ref, t_ref, loss_ref, dx_ref, dxb_ref, dg_ref):
        xv = x_ref[...]
        gv = g_ref[...]
        r = lax.rsqrt(jnp.mean(xv * xv, axis=-1, keepdims=True) + EPS)
        xhat = xv * r
        diff = xhat * gv - t_ref[...]
        dy = diff * (1.0 / d)
        dxhat = dy * gv
        dx = r * (dxhat - xhat * jnp.mean(dxhat * xhat, axis=-1, keepdims=True))
        dx_ref[...] = dx
        dxb_ref[...] = dx.astype(BF16)
        dg_part = jnp.sum(dy * xhat, axis=0, keepdims=True)
        row_loss = jnp.sum(diff * diff, axis=-1, keepdims=True)
        loss_part = jnp.sum(row_loss, axis=0, keepdims=True) * (0.5 / d)

        @pl.when(pl.program_id(0) == 0)
        def _():
            dg_ref[...] = dg_part
            loss_ref[...] = jnp.broadcast_to(loss_part, loss_ref.shape)

        @pl.when(pl.program_id(0) > 0)
        def _():
            dg_ref[...] += dg_part
            loss_ref[...] += jnp.broadcast_to(loss_part, loss_ref.shape)

    row = pl.BlockSpec((ROW_TILE, d), lambda i: (i, 0))
    vec = pl.BlockSpec((1, d), lambda i: (0, 0))
    tile = pl.BlockSpec((8, LANES), lambda i: (0, 0))
    return pl.pallas_call(
        body,
        name=name,
        grid=(s // ROW_TILE,),
        in_specs=[row, vec, row],
        out_specs=[tile, row, row, vec],
        out_shape=[jax.ShapeDtypeStruct((8, LANES), F32), jax.ShapeDtypeStruct((s, d), F32),
                   jax.ShapeDtypeStruct((s, d), BF16), jax.ShapeDtypeStruct((1, d), F32)],
        compiler_params=_cparams(("arbitrary",)),
    )(x, g, target)


_BRANCHES = ((0, D_CONV), (D_CONV, D_SGU), (D_CONV + D_SGU, D_SB))


def _combine_fwd(ya, yb, yc, g, *, name):
    s = ya.shape[0]

    def body(ya_ref, yb_ref, yc_ref, g_ref, y_ref):
        for ref, (off, w) in zip((ya_ref, yb_ref, yc_ref), _BRANCHES):
            v = ref[...]
            r = lax.rsqrt(jnp.mean(v * v, axis=-1, keepdims=True) + EPS)
            y_ref[:, off:off + w] = (v * r * g_ref[:, off:off + w]).astype(BF16)

    def row(w):
        return pl.BlockSpec((ROW_TILE, w), lambda i: (i, 0))

    return pl.pallas_call(
        body,
        name=name,
        grid=(s // ROW_TILE,),
        in_specs=[row(D_CONV), row(D_SGU), row(D_SB), pl.BlockSpec((1, D_MODEL), lambda i: (0, 0))],
        out_specs=row(D_MODEL),
        out_shape=jax.ShapeDtypeStruct((s, D_MODEL), BF16),
        compiler_params=_cparams(("parallel",)),
    )(ya, yb, yc, g)


def _combine_bwd(dy, ya, yb, yc, g, *, name):
    s = ya.shape[0]

    def body(dy_ref, ya_ref, yb_ref, yc_ref, g_ref, dya_ref, dyb_ref, dyc_ref, dg_ref):
        first = pl.program_id(0) == 0
        for ref, dref, (off, w) in zip((ya_ref, yb_ref, yc_ref), (dya_ref, dyb_ref, dyc_ref), _BRANCHES):
            v = ref[...]
            r = lax.rsqrt(jnp.mean(v * v, axis=-1, keepdims=True) + EPS)
            n = v * r
            dout = dy_ref[:, off:off + w]
            dn = dout * g_ref[:, off:off + w]
            dref[...] = r * (dn - n * jnp.mean(dn * n, axis=-1, keepdims=True))
            part = jnp.sum(dout * n, axis=0, keepdims=True)

            @pl.when(first)
            def _():
                dg_ref[:, off:off + w] = part

            @pl.when(jnp.logical_not(first))
            def _():
                dg_ref[:, off:off + w] += part

    def row(w):
        return pl.BlockSpec((ROW_TILE, w), lambda i: (i, 0))

    vec = pl.BlockSpec((1, D_MODEL), lambda i: (0, 0))
    return pl.pallas_call(
        body,
        name=name,
        grid=(s // ROW_TILE,),
        in_specs=[row(D_MODEL), row(D_CONV), row(D_SGU), row(D_SB), vec],
        out_specs=[row(D_CONV), row(D_SGU), row(D_SB), vec],
        out_shape=[jax.ShapeDtypeStruct((s, D_CONV), F32), jax.ShapeDtypeStruct((s, D_SGU), F32),
                   jax.ShapeDtypeStruct((s, D_SB), F32), jax.ShapeDtypeStruct((1, D_MODEL), F32)],
        compiler_params=_cparams(("arbitrary",)),
    )(dy, ya, yb, yc, g)


CONV_TILE = 128


def _shift_down(window, j, halo):
    return pltpu.roll(window, j, 0)[halo:, :] if j else window[halo:, :]


def _shift_up(window, j, n_out):
    n = window.shape[0]
    return pltpu.roll(window, n - j, 0)[:n_out, :] if j else window[:n_out, :]


def _mixer_a_fwd(p_ab, conv_w, conv_b, ln_g, ln_b, *, name):
    s = p_ab.shape[0]
    nt = s // CONV_TILE

    def body(p_ref, w_ref, b_ref, g_ref, beta_ref, y_ref, h_ref):
        h_ref[0:CONV_HALO, :] = jnp.zeros((CONV_HALO, D_CONV), F32)

        def glu(i, c):
            t0 = pl.multiple_of(i * CONV_TILE, CONV_TILE)
            a = p_ref[pl.ds(t0, CONV_TILE), 0:D_CONV].astype(F32)
            gate = p_ref[pl.ds(t0, CONV_TILE), D_CONV:2 * D_CONV].astype(F32)
            h_ref[pl.ds(t0 + CONV_HALO, CONV_TILE), :] = a * _sigmoid(gate)
            return c

        lax.fori_loop(0, nt, glu, 0)

        def conv(i, c):
            t0 = pl.multiple_of(i * CONV_TILE, CONV_TILE)
            window = h_ref[pl.ds(t0, CONV_TILE + CONV_HALO), :]
            acc = jnp.zeros((CONV_TILE, D_CONV), F32) + b_ref[...]
            for k in range(CONV_K):
                acc = acc + w_ref[k:k + 1, :] * _shift_down(window, CONV_K - 1 - k, CONV_HALO)
            mu = jnp.mean(acc, axis=-1, keepdims=True)
            xc = acc - mu
            rstd = lax.rsqrt(jnp.mean(xc * xc, axis=-1, keepdims=True) + EPS)
            z = xc * rstd * g_ref[...] + beta_ref[...]
            y_ref[pl.ds(t0, CONV_TILE), :] = z * _sigmoid(z)
            return c

        lax.fori_loop(0, nt, conv, 0)

    full = lambda shape: pl.BlockSpec(shape, lambda i: (0, 0))
    return pl.pallas_call(
        body,
        name=name,
        grid=(1,),
        in_specs=[full((s, 2 * D_CONV)), full((CONV_K, D_CONV)), full((1, D_CONV)), full((1, D_CONV)),
                  full((1, D_CONV))],
        out_specs=full((s, D_CONV)),
        out_shape=jax.ShapeDtypeStruct((s, D_CONV), F32),
        scratch_shapes=[pltpu.VMEM((s + CONV_HALO, D_CONV), F32)],
        compiler_params=_cparams(("arbitrary",)),
    )(p_ab, conv_w, conv_b, ln_g, ln_b)


def _mixer_a_bwd(p_ab, dya, conv_w, conv_b, ln_g, ln_b, *, name):
    s = p_ab.shape[0]
    nt = s // CONV_TILE

    def body(p_ref, dy_ref, w_ref, b_ref, g_ref, beta_ref, dp_ref, dw_ref, db_ref, dg_ref, dbeta_ref, h_ref, dc_ref):
        h_ref[0:CONV_HALO, :] = jnp.zeros((CONV_HALO, D_CONV), F32)
        dc_ref[s:s + CONV_HALO, :] = jnp.zeros((CONV_HALO, D_CONV), F32)
        dw_ref[...] = jnp.zeros_like(dw_ref)
        db_ref[...] = jnp.zeros_like(db_ref)
        dg_ref[...] = jnp.zeros_like(dg_ref)
        dbeta_ref[...] = jnp.zeros_like(dbeta_ref)

        def glu(i, c):
            t0 = pl.multiple_of(i * CONV_TILE, CONV_TILE)
            a = p_ref[pl.ds(t0, CONV_TILE), 0:D_CONV].astype(F32)
            gate = p_ref[pl.ds(t0, CONV_TILE), D_CONV:2 * D_CONV].astype(F32)
            h_ref[pl.ds(t0 + CONV_HALO, CONV_TILE), :] = a * _sigmoid(gate)
            return c

        lax.fori_loop(0, nt, glu, 0)

        def conv_bwd(i, c):
            t0 = pl.multiple_of(i * CONV_TILE, CONV_TILE)
            window = h_ref[pl.ds(t0, CONV_TILE + CONV_HALO), :]
            taps = [_shift_down(window, CONV_K - 1 - k, CONV_HALO) for k in range(CONV_K)]
            acc = jnp.zeros((CONV_TILE, D_CONV), F32) + b_ref[...]
            for k in range(CONV_K):
                acc = acc + w_ref[k:k + 1, :] * taps[k]
            mu = jnp.mean(acc, axis=-1, keepdims=True)
            xc = acc - mu
            rstd = lax.rsqrt(jnp.mean(xc * xc, axis=-1, keepdims=True) + EPS)
            xhat = xc * rstd
            z = xhat * g_ref[...] + beta_ref[...]
            sg = _sigmoid(z)
            dz = dy_ref[pl.ds(t0, CONV_TILE), :] * (sg * (1.0 + z * (1.0 - sg)))
            dg_ref[...] += jnp.sum(dz * xhat, axis=0, keepdims=True)
            dbeta_ref[...] += jnp.sum(dz, axis=0, keepdims=True)
            dxhat = dz * g_ref[...]
            dc = rstd * (dxhat - jnp.mean(dxhat, axis=-1, keepdims=True)
                         - xhat * jnp.mean(dxhat * xhat, axis=-1, keepdims=True))
            dc_ref[pl.ds(t0, CONV_TILE), :] = dc
            db_ref[...] += jnp.sum(dc, axis=0, keepdims=True)
            for k in range(CONV_K):
                dw_ref[k:k + 1, :] += jnp.sum(dc * taps[k], axis=0, keepdims=True)
            return c

        lax.fori_loop(0, nt, conv_bwd, 0)

        def glu_bwd(i, c):
            t0 = pl.multiple_of(i * CONV_TILE, CONV_TILE)
            window = dc_ref[pl.ds(t0, CONV_TILE + CONV_HALO), :]
            dh = jnp.zeros((CONV_TILE, D_CONV), F32)
            for j in range(CONV_K):
                dh = dh + w_ref[CONV_K - 1 - j:CONV_K - j, :] * _shift_up(window, j, CONV_TILE)
            a = p_ref[pl.ds(t0, CONV_TILE), 0:D_CONV].astype(F32)
            sg = _sigmoid(p_ref[pl.ds(t0, CONV_TILE), D_CONV:2 * D_CONV].astype(F32))
            dp_ref[pl.ds(t0, CONV_TILE), 0:D_CONV] = (dh * sg).astype(BF16)
            dp_ref[pl.ds(t0, CONV_TILE), D_CONV:2 * D_CONV] = (dh * a * sg * (1.0 - sg)).astype(BF16)
            return c

        lax.fori_loop(0, nt, glu_bwd, 0)

    full = lambda shape: pl.BlockSpec(shape, lambda i: (0, 0))
    vec = jax.ShapeDtypeStruct((1, D_CONV), F32)
    return pl.pallas_call(
        body,
        name=name,
        grid=(1,),
        in_specs=[full((s, 2 * D_CONV)), full((s, D_CONV)), full((CONV_K, D_CONV)), full((1, D_CONV)),
                  full((1, D_CONV)), full((1, D_CONV))],
        out_specs=[full((s, 2 * D_CONV)), full((CONV_K, D_CONV)), full((1, D_CONV)), full((1, D_CONV)),
                   full((1, D_CONV))],
        out_shape=[jax.ShapeDtypeStruct((s, 2 * D_CONV), BF16), jax.ShapeDtypeStruct((CONV_K, D_CONV), F32),
                   vec, vec, vec],
        scratch_shapes=[pltpu.VMEM((s + CONV_HALO, D_CONV), F32), pltpu.VMEM((s + CONV_HALO, D_CONV), F32)],
        compiler_params=_cparams(("arbitrary",)),
    )(p_ab, dya, conv_w, conv_b, ln_g, ln_b)


N_SGU_HEADS = D_SGU // HEAD_DIM


def _head_masks(width):
    lane = lax.broadcasted_iota(jnp.int32, (1, width), 1)
    return [(lane >= h * HEAD_DIM) & (lane < (h + 1) * HEAD_DIM) for h in range(width // HEAD_DIM)]


def _tril_mask():
    r = lax.broadcasted_iota(jnp.int32, (CHUNK, CHUNK), 0)
    c = lax.broadcasted_iota(jnp.int32, (CHUNK, CHUNK), 1)
    return c <= r


def _sgu_norm(bv, g, beta):
    vg = _gelu(bv)
    mu = jnp.mean(vg, axis=-1, keepdims=True)
    xc = vg - mu
    rstd = lax.rsqrt(jnp.mean(xc * xc, axis=-1, keepdims=True) + EPS)
    xhat = xc * rstd
    return xhat, rstd, xhat * g + beta


def _sgu_fwd(p_ab, ln_g, ln_b, w_s, bias, *, name):
    s = p_ab.shape[0]

    def body(p_ref, g_ref, beta_ref, w_ref, bias_ref, y_ref):
        u = _gelu(p_ref[:, 0:D_SGU].astype(F32))
        _, _, vn = _sgu_norm(p_ref[:, D_SGU:2 * D_SGU].astype(F32), g_ref[...], beta_ref[...])
        vb = vn.astype(BF16)
        tril = _tril_mask()
        mixed = bias_ref[...]
        for h, m in enumerate(_head_masks(D_SGU)):
            wh = jnp.where(tril, w_ref[h], 0.0).astype(BF16)
            mixed = mixed + _dot(wh, jnp.where(m, vb, jnp.zeros_like(vb)), _NN)
        y_ref[...] = u * mixed

    return pl.pallas_call(
        body,
        name=name,
        grid=(s // CHUNK,),
        in_specs=[pl.BlockSpec((CHUNK, 2 * D_SGU), lambda i: (i, 1)),
                  pl.BlockSpec((1, D_SGU), lambda i: (0, 0)), pl.BlockSpec((1, D_SGU), lambda i: (0, 0)),
                  pl.BlockSpec((N_SGU_HEADS, CHUNK, CHUNK), lambda i: (0, 0, 0)),
                  pl.BlockSpec((CHUNK, D_SGU), lambda i: (0, 0))],
        out_specs=pl.BlockSpec((CHUNK, D_SGU), lambda i: (i, 0)),
        out_shape=jax.ShapeDtypeStruct((s, D_SGU), F32),
        compiler_params=_cparams(("parallel",)),
    )(p_ab, ln_g, ln_b, w_s, bias)


def _sgu_bwd(p_ab, dyb, ln_g, ln_b, w_s, bias, *, name):
    s = p_ab.shape[0]
    n_chunks = s // CHUNK

    def body(p_ref, dy_ref, g_ref, beta_ref, w_ref, bias_ref, dp_ref, dw_ref, db_ref, dg_ref, dbeta_ref, dbias_ref):
        @pl.when(pl.program_id(0) == 0)
        def _():
            dw_ref[...] = jnp.zeros_like(dw_ref)
            dbias_ref[...] = jnp.zeros_like(dbias_ref)
            dg_ref[...] = jnp.zeros_like(dg_ref)
            dbeta_ref[...] = jnp.zeros_like(dbeta_ref)

        bu = p_ref[:, 0:D_SGU].astype(F32)
        bv = p_ref[:, D_SGU:2 * D_SGU].astype(F32)
        u = _gelu(bu)
        gv = g_ref[...]
        xhat, rstd, vn = _sgu_norm(bv, gv, beta_ref[...])
        vb = vn.astype(BF16)
        tril = _tril_mask()
        masks = _head_masks(D_SGU)
        whs = [jnp.where(tril, w_ref[h], 0.0).astype(BF16) for h in range(N_SGU_HEADS)]
        mixed = bias_ref[...]
        for h, m in enumerate(masks):
            mixed = mixed + _dot(whs[h], jnp.where(m, vb, jnp.zeros_like(vb)), _NN)
        dy = dy_ref[...]
        dp_ref[:, 0:D_SGU] = (dy * mixed * _gelu_grad(bu)).astype(BF16)
        dmixed = dy * u
        dbias_ref[...] += dmixed
        dmb = dmixed.astype(BF16)
        dvn = jnp.zeros((CHUNK, D_SGU), F32)
        for h, m in enumerate(masks):
            dmh = jnp.where(m, dmb, jnp.zeros_like(dmb))
            dvn = dvn + _dot(whs[h], dmh, _TN)
            dw_ref[h] += jnp.where(tril, _dot(dmh, vb, _NT), 0.0)
        dg_ref[...] += jnp.sum(dvn * xhat, axis=0, keepdims=True)
        dbeta_ref[...] += jnp.sum(dvn, axis=0, keepdims=True)
        dxhat = dvn * gv
        dvg = rstd * (dxhat - jnp.mean(dxhat, axis=-1, keepdims=True)
                      - xhat * jnp.mean(dxhat * xhat, axis=-1, keepdims=True))
        dp_ref[:, D_SGU:2 * D_SGU] = (dvg * _gelu_grad(bv)).astype(BF16)

        @pl.when(pl.program_id(0) == n_chunks - 1)
        def _():
            chan = lax.broadcasted_iota(jnp.int32, (D_SGU, LANES), 0)
            head = lax.broadcasted_iota(jnp.int32, (D_SGU, LANES), 1)
            to_head = jnp.where(chan // HEAD_DIM == head, 1.0, 0.0).astype(BF16)
            db_ref[...] = _dot(_split_bf16(dbias_ref[...]), jnp.concatenate([to_head, to_head], axis=0), _NN)

    vec = pl.BlockSpec((1, D_SGU), lambda i: (0, 0))
    wspec = pl.BlockSpec((N_SGU_HEADS, CHUNK, CHUNK), lambda i: (0, 0, 0))
    bspec = pl.BlockSpec((CHUNK, D_SGU), lambda i: (0, 0))
    return pl.pallas_call(
        body,
        name=name,
        grid=(n_chunks,),
        in_specs=[pl.BlockSpec((CHUNK, 2 * D_SGU), lambda i: (i, 1)), pl.BlockSpec((CHUNK, D_SGU), lambda i: (i, 0)),
                  vec, vec, wspec, bspec],
        out_specs=[pl.BlockSpec((CHUNK, 2 * D_SGU), lambda i: (i, 0)), wspec,
                   pl.BlockSpec((CHUNK, LANES), lambda i: (0, 0)), vec, vec],
        out_shape=[jax.ShapeDtypeStruct((s, 2 * D_SGU), BF16),
                   jax.ShapeDtypeStruct((N_SGU_HEADS, CHUNK, CHUNK), F32),
                   jax.ShapeDtypeStruct((CHUNK, LANES), F32),
                   jax.ShapeDtypeStruct((1, D_SGU), F32), jax.ShapeDtypeStruct((1, D_SGU), F32)],
        scratch_shapes=[pltpu.VMEM((CHUNK, D_SGU), F32)],
        compiler_params=_cparams(("arbitrary",)),
    )(p_ab, dyb, ln_g, ln_b, w_s, bias)


N_PAIRS = D_SB // LANES
QKV_BLOCK0 = D_AB // LANES
SB_SCALE = HEAD_DIM ** -0.5


def _sb_logits(z, valid):
    nz = -z
    t = jnp.log(1.0 + jnp.exp(jnp.minimum(z, nz)))
    l1 = jnp.minimum(nz, 0.0) - t
    if valid is not None:
        l1 = jnp.where(valid, l1, 0.0)
    return l1, jnp.minimum(z, 0.0) - t


def _split_hi_lo(x):
    hi = lax.bitcast_convert_type(lax.bitcast_convert_type(x, jnp.uint32) & jnp.uint32(0xFFFF0000), F32)
    return jnp.concatenate([hi, x - hi], axis=1)


def _cumsum_operand(keep):
    half = jnp.concatenate([keep.astype(F32), jnp.ones((CHUNK, CHUNK), F32)], axis=1)
    return jnp.concatenate([half, half], axis=0)


Q_BLOCKS_PER_STEP = 4


def _q_blocks_per_step(nq):
    return next(n for n in (Q_BLOCKS_PER_STEP, 2, 1) if nq % n == 0)


def _attn_fwd(qkv, *, name):
    s = qkv.shape[0]
    nq = s // CHUNK
    per_step = _q_blocks_per_step(nq)

    def body(q_ref, k_ref, v_ref, o_ref, t_ref):
        masks = _head_masks(LANES)
        row = lax.broadcasted_iota(jnp.int32, (CHUNK, CHUNK), 0)
        col = lax.broadcasted_iota(jnp.int32, (CHUNK, CHUNK), 1)
        after_op = _cumsum_operand(row > col)
        cmr = col - row
        zc = jnp.zeros((CHUNK, LANES), F32)

        def q_block(sub, _):
            qi = pl.program_id(1) * per_step + sub
            q_rows = pl.ds(pl.multiple_of(sub * CHUNK, CHUNK), CHUNK)
            q = q_ref[q_rows, :] * SB_SCALE
            zero = jnp.zeros_like(q)
            qs = [jnp.where(m, q, zero) for m in masks]

            def blocks(js, carry):
                o, c0, c1 = carry
                kvs, valids = [], []
                for j in js:
                    k0 = pl.multiple_of(jnp.maximum(j, 0) * CHUNK, CHUNK)
                    kvs.append((k_ref[pl.ds(k0, CHUNK), :], v_ref[pl.ds(k0, CHUNK), :]))
                    valids.append(cmr < jnp.where(j >= 0, (qi - j) * CHUNK, -CHUNK))
                units = [(h, b) for b in range(len(js)) for h in range(2)]
                zs = [_dot(qs[h], kvs[b][0], _NT) for h, b in units]
                logits = [_sb_logits(z, valids[b]) for z, (h, b) in zip(zs, units)]
                sums = [_dot(_split_hi_lo(l1), after_op, _NN) for l1, _ in logits]
                cs = [c0, c1]
                probs = []
                for (h, b), (_, lb), sm in zip(units, logits, sums):
                    probs.append(jnp.where(valids[b], jnp.exp(lb + sm[:, :CHUNK] + cs[h]), 0.0))
                    cs[h] = cs[h] + sm[:, CHUNK:]
                for (h, b), a in zip(units, probs):
                    o = o + _dot(a.astype(BF16), jnp.where(masks[h], kvs[b][1], zero), _NN)
                return o, cs[0], cs[1]

            n_four = (qi + 1) // 4
            carry = lax.fori_loop(0, n_four, lambda jj, c: blocks([qi - 4 * jj - i for i in range(4)], c), (zc,) * 3)
            top = qi - 4 * n_four
            o, c0, c1 = lax.fori_loop(0, (top + 2) // 2, lambda jj, c: blocks([top - 2 * jj, top - 2 * jj - 1], c), carry)
            o_ref[q_rows, :] = o
            t_ref[q_rows, 0:LANES] = c0
            t_ref[q_rows, LANES:2 * LANES] = c1
            return 0

        lax.fori_loop(0, per_step, q_block, 0)

    rows = per_step * CHUNK
    return pl.pallas_call(
        body,
        name=name,
        grid=(N_PAIRS, nq // per_step),
        in_specs=[pl.BlockSpec((rows, LANES), lambda p, i: (i, QKV_BLOCK0 + p)),
                  pl.BlockSpec((s, LANES), lambda p, i: (0, QKV_BLOCK0 + N_PAIRS + p)),
                  pl.BlockSpec((s, LANES), lambda p, i: (0, QKV_BLOCK0 + 2 * N_PAIRS + p))],
        out_specs=[pl.BlockSpec((rows, LANES), lambda p, i: (i, p)),
                   pl.BlockSpec((rows, 2 * LANES), lambda p, i: (i, p))],
        out_shape=[jax.ShapeDtypeStruct((s, D_SB), F32), jax.ShapeDtypeStruct((s, 2 * D_SB), F32)],
        compiler_params=_cparams(("parallel", "parallel")),
    )(qkv, qkv, qkv)


def _attn_bwd(qkv, t_tot, do, *, name):
    s = qkv.shape[0]
    nq = s // CHUNK
    per_step = _q_blocks_per_step(nq)

    def body(q_ref, k_ref, v_ref, t_ref, do_ref, dq_ref, dk_ref, dv_ref):
        @pl.when(pl.program_id(1) == 0)
        def _():
            dk_ref[...] = jnp.zeros_like(dk_ref)
            dv_ref[...] = jnp.zeros_like(dv_ref)

        masks = _head_masks(LANES)
        row = lax.broadcasted_iota(jnp.int32, (CHUNK, CHUNK), 0)
        col = lax.broadcasted_iota(jnp.int32, (CHUNK, CHUNK), 1)
        upto_op = _cumsum_operand(row <= col)
        before_op = _cumsum_operand(row < col)
        cmr = col - row
        zc = jnp.zeros((CHUNK, LANES), F32)

        def q_block(sub, _):
            qi = pl.program_id(1) * per_step + sub
            q_rows = pl.ds(pl.multiple_of(sub * CHUNK, CHUNK), CHUNK)
            q = q_ref[q_rows, :] * SB_SCALE
            dob = do_ref[q_rows, :].astype(BF16)
            zero = jnp.zeros_like(q)
            qs = [jnp.where(m, q, zero) for m in masks]
            dos = [jnp.where(m, dob, zero) for m in masks]
            tots = [t_ref[q_rows, 0:LANES], t_ref[q_rows, LANES:2 * LANES]]

            def blocks(js, carry):
                dq, cl0, cl1, cp0, cp1 = carry
                starts = [pl.multiple_of(jnp.minimum(j, nq - 1) * CHUNK, CHUNK) for j in js]
                valids = [cmr < (qi - j) * CHUNK for j in js]
                kvs = [(k_ref[pl.ds(k0, CHUNK), :], v_ref[pl.ds(k0, CHUNK), :]) for k0 in starts]
                units = [(h, b) for b in range(len(js)) for h in range(2)]
                zs = [_dot(qs[h], kvs[b][0], _NT) for h, b in units]
                das = [_dot(dos[h], kvs[b][1], _NT) for h, b in units]
                logits = [_sb_logits(z, valids[b]) for z, (h, b) in zip(zs, units)]
                sums = [_dot(_split_hi_lo(l1), upto_op, _NN) for l1, _ in logits]
                cls, cps = [cl0, cl1], [cp0, cp1]
                probs, gs = [], []
                for (h, b), (_, lb), sm, da in zip(units, logits, sums, das):
                    a = jnp.where(valids[b], jnp.exp(lb + (tots[h] - cls[h] - sm[:, :CHUNK])), 0.0)
                    probs.append(a)
                    gs.append(a * da)
                    cls[h] = cls[h] + sm[:, CHUNK:]
                sums_g = [_dot(_split_hi_lo(g), before_op, _NN) for g in gs]
                dzs = []
                for (h, b), (_, lb), g, sg in zip(units, logits, gs, sums_g):
                    dz = g - (g + sg[:, :CHUNK] + cps[h]) * jnp.exp(lb)
                    dzs.append(jnp.where(valids[b], dz, 0.0).astype(BF16))
                    cps[h] = cps[h] + sg[:, CHUNK:]
                for (h, b), dzb in zip(units, dzs):
                    dq = dq + _dot(dzb, jnp.where(masks[h], kvs[b][0], zero), _NN)
                for b, k0 in enumerate(starts):
                    dk_ref[pl.ds(k0, CHUNK), :] += _dot(dzs[2 * b], qs[0], _TN) + _dot(dzs[2 * b + 1], qs[1], _TN)
                    dv_ref[pl.ds(k0, CHUNK), :] += (_dot(probs[2 * b].astype(BF16), dos[0], _TN)
                                                    + _dot(probs[2 * b + 1].astype(BF16), dos[1], _TN))
                return dq, cls[0], cls[1], cps[0], cps[1]

            n_four = (qi + 1) // 4
            carry = lax.fori_loop(0, n_four, lambda jj, c: blocks([4 * jj + i for i in range(4)], c), (zc,) * 5)
            base = 4 * n_four
            carry = lax.fori_loop(0, (qi - base + 2) // 2, lambda jj, c: blocks([base + 2 * jj, base + 2 * jj + 1], c), carry)
            dq_ref[q_rows, :] = (carry[0] * SB_SCALE).astype(BF16)
            return 0

        lax.fori_loop(0, per_step, q_block, 0)

    rows = per_step * CHUNK
    blk = pl.BlockSpec((rows, LANES), lambda p, i: (i, p))
    col_blk = pl.BlockSpec((s, LANES), lambda p, i: (0, p))
    out = jax.ShapeDtypeStruct((s, D_SB), F32)
    return pl.pallas_call(
        body,
        name=name,
        grid=(N_PAIRS, nq // per_step),
        in_specs=[pl.BlockSpec((rows, LANES), lambda p, i: (i, QKV_BLOCK0 + p)),
                  pl.BlockSpec((s, LANES), lambda p, i: (0, QKV_BLOCK0 + N_PAIRS + p)),
                  pl.BlockSpec((s, LANES), lambda p, i: (0, QKV_BLOCK0 + 2 * N_PAIRS + p)),
                  pl.BlockSpec((rows, 2 * LANES), lambda p, i: (i, p)),
                  blk],
        out_specs=[blk, col_blk, col_blk],
        out_shape=[jax.ShapeDtypeStruct((s, D_SB), BF16), out, out],
        compiler_params=_cparams(("parallel", "arbitrary")),
    )(qkv, qkv, qkv, t_tot, do)


FFN_TILE = 256
FFN_COLS = 256
N_FF_BLOCKS = D_FF // FFN_COLS


def _ffn_act_fwd(up0, conv_w, conv_b, *, name):
    s = up0.shape[0]
    nt = s // FFN_TILE

    def body(xg_ref, xv_ref, wg_ref, wv_ref, bg_ref, bv_ref, act_ref, pg_ref, pv_ref):
        pg_ref[0:FFN_HALO, :] = jnp.zeros((FFN_HALO, FFN_COLS), F32)
        pv_ref[0:FFN_HALO, :] = jnp.zeros((FFN_HALO, FFN_COLS), F32)
        pg_ref[FFN_HALO:, :] = xg_ref[...].astype(F32)
        pv_ref[FFN_HALO:, :] = xv_ref[...].astype(F32)

        def tile(i, c):
            t0 = pl.multiple_of(i * FFN_TILE, FFN_TILE)
            outs = []
            for p_ref, w_ref, b_ref in ((pg_ref, wg_ref, bg_ref), (pv_ref, wv_ref, bv_ref)):
                window = p_ref[pl.ds(t0, FFN_TILE + FFN_HALO), :]
                acc = b_ref[...] + w_ref[2:3, :] * window[FFN_HALO:, :]
                for j in range(1, FFN_K):
                    acc = acc + w_ref[FFN_K - 1 - j:FFN_K - j, :] * _shift_down(window, j, FFN_HALO)
                outs.append(acc)
            gate, val = outs
            act_ref[pl.ds(t0, FFN_TILE), :] = (gate * _sigmoid(gate) * val).astype(BF16)
            return c

        lax.fori_loop(0, nt, tile, 0)

    gcol = lambda rows: pl.BlockSpec((rows, FFN_COLS), lambda j: (0, j))
    vcol = lambda rows: pl.BlockSpec((rows, FFN_COLS), lambda j: (0, j + N_FF_BLOCKS))
    return pl.pallas_call(
        body,
        name=name,
        grid=(N_FF_BLOCKS,),
        in_specs=[gcol(s), vcol(s), gcol(FFN_K), vcol(FFN_K), gcol(1), vcol(1)],
        out_specs=gcol(s),
        out_shape=jax.ShapeDtypeStruct((s, D_FF), BF16),
        scratch_shapes=[pltpu.VMEM((s + FFN_HALO, FFN_COLS), F32), pltpu.VMEM((s + FFN_HALO, FFN_COLS), F32)],
        compiler_params=_cparams(("parallel",)),
    )(up0, up0, conv_w, conv_w, conv_b, conv_b)


def _ffn_act_bwd(up0, dact, conv_w, conv_b, *, name):
    s = up0.shape[0]
    nt = s // FFN_TILE

    def body(xg_ref, xv_ref, da_ref, wg_ref, wv_ref, bg_ref, bv_ref, dxg_ref, dxv_ref, dwg_ref, dwv_ref, dbg_ref, dbv_ref,
             pg_ref, pv_ref, dg_ref, dv_ref):
        zeros = jnp.zeros((FFN_HALO, FFN_COLS), F32)
        for p_ref, x_ref in ((pg_ref, xg_ref), (pv_ref, xv_ref)):
            p_ref[0:FFN_HALO, :] = zeros
            p_ref[FFN_HALO:, :] = x_ref[...].astype(F32)
        dg_ref[s:s + FFN_HALO, :] = zeros
        dv_ref[s:s + FFN_HALO, :] = zeros
        for ref in (dwg_ref, dwv_ref, dbg_ref, dbv_ref):
            ref[...] = jnp.zeros_like(ref)

        def conv(p_ref, w_ref, b_ref, t0):
            window = p_ref[pl.ds(t0, FFN_TILE + FFN_HALO), :]
            taps = [_shift_down(window, j, FFN_HALO) for j in range(FFN_K)]
            out = b_ref[...]
            for j in range(FFN_K):
                out = out + w_ref[FFN_K - 1 - j:FFN_K - j, :] * taps[j]
            return out, taps

        def tile(i, c):
            t0 = pl.multiple_of(i * FFN_TILE, FFN_TILE)
            gate, taps_g = conv(pg_ref, wg_ref, bg_ref, t0)
            val, taps_v = conv(pv_ref, wv_ref, bv_ref, t0)
            da = da_ref[pl.ds(t0, FFN_TILE), :].astype(F32)
            sg = lax.logistic(gate)
            dgate = da * val * (sg * (1.0 + gate * (1.0 - sg)))
            dval = da * gate * sg
            dg_ref[pl.ds(t0, FFN_TILE), :] = dgate
            dv_ref[pl.ds(t0, FFN_TILE), :] = dval
            dbg_ref[...] += jnp.sum(dgate, axis=0, keepdims=True)
            dbv_ref[...] += jnp.sum(dval, axis=0, keepdims=True)
            for j in range(FFN_K):
                dwg_ref[FFN_K - 1 - j:FFN_K - j, :] += jnp.sum(dgate * taps_g[j], axis=0, keepdims=True)
                dwv_ref[FFN_K - 1 - j:FFN_K - j, :] += jnp.sum(dval * taps_v[j], axis=0, keepdims=True)
            return c

        lax.fori_loop(0, nt, tile, 0)

        def tile_dx(i, c):
            t0 = pl.multiple_of(i * FFN_TILE, FFN_TILE)
            for d_ref, w_ref, dx_ref in ((dg_ref, wg_ref, dxg_ref), (dv_ref, wv_ref, dxv_ref)):
                window = d_ref[pl.ds(t0, FFN_TILE + FFN_HALO), :]
                dx = w_ref[FFN_K - 1:FFN_K, :] * window[:FFN_TILE, :]
                for j in range(1, FFN_K):
                    dx = dx + w_ref[FFN_K - 1 - j:FFN_K - j, :] * _shift_up(window, j, FFN_TILE)
                dx_ref[pl.ds(t0, FFN_TILE), :] = dx.astype(BF16)
            return c

        lax.fori_loop(0, nt, tile_dx, 0)

    gcol = lambda rows: pl.BlockSpec((rows, FFN_COLS), lambda j: (0, j))
    vcol = lambda rows: pl.BlockSpec((rows, FFN_COLS), lambda j: (0, j + N_FF_BLOCKS))
    half = lambda rows, dtype: jax.ShapeDtypeStruct((rows, D_FF), dtype)
    padded = pltpu.VMEM((s + FFN_HALO, FFN_COLS), F32)
    return pl.pallas_call(
        body,
        name=name,
        grid=(N_FF_BLOCKS,),
        in_specs=[gcol(s), vcol(s), gcol(s), gcol(FFN_K), vcol(FFN_K), gcol(1), vcol(1)],
        out_specs=[gcol(s), gcol(s), gcol(FFN_K), gcol(FFN_K), gcol(1), gcol(1)],
        out_shape=[half(s, BF16), half(s, BF16), half(FFN_K, F32), half(FFN_K, F32), half(1, F32), half(1, F32)],
        scratch_shapes=[padded, padded, padded, padded],
        compiler_params=_cparams(("parallel",)),
    )(up0, up0, dact, conv_w, conv_w, conv_b, conv_b)


MESH = pl.DeviceIdType.MESH


def _position():
    x, y, c = lax.axis_index("x"), lax.axis_index("y"), lax.axis_index("c")
    return x, y, c, 4 * x + 2 * y + c


def _peer(k):
    x, y, c, _ = _position()
    px = 1 - x if k & 4 else x
    py = 1 - y if k & 2 else y
    pc = 1 - c if k & 1 else c
    return (px, py, pc), 4 * px + 2 * py + pc


_HBM = pl.BlockSpec(memory_space=pltpu.HBM)
_SEM = pl.BlockSpec(memory_space=pltpu.SEMAPHORE)
_DATAFLOW = pltpu.SideEffectType.DATAFLOW_SIDE_EFFECTING
N_PEERS = N_DEV - 1


class _SplitExchange:
    def __init__(self, src, *, kind, name):
        self.kind, self.name, self.dtype = kind, name, src.dtype
        scatter = kind.startswith("scatter")
        by_blocks = kind.endswith("blocks")
        self.scatter, self.by_blocks = scatter, by_blocks
        if by_blocks:
            self.r, self.cols, self.land_shape = None, None, src.shape if scatter else (N_DEV,) + src.shape
        else:
            self.r = src.shape[0] // N_DEV if scatter else src.shape[0]
            self.cols = src.shape[1]
            self.land_shape = (N_DEV, self.r, self.cols) if scatter else (N_DEV * self.r, self.cols)
        r = self.r

        def copies(src_ref, land_ref, send_sems, recv_sems, local_sem):
            me = _position()[3]

            def rows(ref, idx):
                return ref.at[pl.ds(pl.multiple_of(idx * r, r), r), :]

            if not scatter:
                outgoing = lambda idx: src_ref
            else:
                outgoing = (lambda idx: src_ref.at[idx]) if by_blocks else (lambda idx: rows(src_ref, idx))
            slot = (lambda idx: land_ref.at[idx]) if (scatter or by_blocks) else (lambda idx: rows(land_ref, idx))
            sends, recvs = [], []
            for k in range(1, N_DEV):
                peer, pidx = _peer(k)
                sems = dict(send_sem=send_sems[k - 1], recv_sem=recv_sems[k - 1], device_id=peer, device_id_type=MESH)
                sends.append(pltpu.make_async_remote_copy(src_ref=outgoing(pidx), dst_ref=slot(me), **sems))
                recvs.append(pltpu.make_async_remote_copy(src_ref=outgoing(pidx), dst_ref=slot(pidx), **sems))
            return sends, recvs, pltpu.make_async_copy(outgoing(me), slot(me), local_sem)

        self._copies = copies
        self.src = src

    @staticmethod
    def start(exchanges, name):
        n = len(exchanges)
        per = 2 * N_PEERS + 1

        def start_body(*refs):
            outs = refs[2 * n:]
            for i, ex in enumerate(exchanges):
                sems = outs[per * i:per * (i + 1)]
                sends, _, local = ex._copies(refs[2 * i], refs[2 * i + 1], sems[:N_PEERS], sems[N_PEERS:-1], sems[-1])
                for cp in sends + [local]:
                    cp.start()
            outs[-1][...] = jnp.zeros_like(outs[-1])

        sem = pltpu.SemaphoreType.DMA(())
        operands, thru_shapes = [], []
        for ex in exchanges:
            operands += [pltpu.with_memory_space_constraint(ex.src, pltpu.HBM),
                         pltpu.with_memory_space_constraint(lax.empty(ex.land_shape, ex.dtype), pltpu.HBM)]
            thru_shapes += [pltpu.HBM(ex.src.shape, ex.dtype), pltpu.HBM(ex.land_shape, ex.dtype)]
        out = pl.pallas_call(
            start_body,
            name=name,
            in_specs=(_HBM,) * (2 * n),
            out_specs=(_SEM,) * (per * n) + (_HBM,) * (2 * n) + (pl.BlockSpec(memory_space=pltpu.VMEM),),
            out_shape=(sem,) * (per * n) + tuple(thru_shapes) + (jax.ShapeDtypeStruct((8, LANES), F32),),
            input_output_aliases={i: per * n + i for i in range(2 * n)},
            compiler_params=pltpu.CompilerParams(has_side_effects=_DATAFLOW),
        )(*operands)
        for i, ex in enumerate(exchanges):
            ex.sems = out[per * i:per * (i + 1)]
            ex.src_thru, ex.land_thru = out[per * n + 2 * i], out[per * n + 2 * i + 1]
        return out[-1][0, 0]

    def finish(self, after):
        copies = self._copies

        def wait_body(src_ref, land_ref, *rest):
            sends, recvs, local = copies(src_ref, land_ref, rest[:N_PEERS], rest[N_PEERS:2 * N_PEERS], rest[2 * N_PEERS])
            for cp in sends:
                cp.wait_send()
            for cp in recvs:
                cp.wait_recv()
            local.wait()

        return pl.pallas_call(
            wait_body,
            name=f"{self.name}_wait",
            in_specs=(_HBM, _HBM) + (_SEM,) * (2 * N_PEERS + 1) + (pl.BlockSpec(memory_space=pl.ANY),),
            out_specs=(_HBM, _HBM),
            out_shape=(pltpu.HBM(self.src_thru.shape, self.dtype), pltpu.HBM(self.land_shape, self.dtype)),
            input_output_aliases={0: 0, 1: 1},
            compiler_params=pltpu.CompilerParams(has_side_effects=_DATAFLOW),
        )(self.src_thru, self.land_thru, *self.sems, after)[1]


def _row_tile(rows):
    return _tile(rows, (256, 128, 64, 32, 16, 8))


def _layer_parts_specs(n_layers, n_parts, tr, cols):
    return [pl.BlockSpec((n_parts, tr, cols), lambda l, i, j=j: (0, jnp.where(l == j, i, 0), 0)) for j in range(n_layers)]


def _select_layer_sum(p_refs):
    l = pl.program_id(0)
    g = None
    for j, p_ref in enumerate(p_refs):
        gj = p_ref[0].astype(F32)
        for k in range(1, p_ref.shape[0]):
            gj = gj + p_ref[k].astype(F32)
        g = gj if g is None else jnp.where(l == j, gj, g)
    return g


def _adamw(parts, w, m, v, *, name):
    n_layers, rows, cols = w.shape
    tr = _row_tile(rows)

    def body(*refs):
        w_ref, m_ref, v_ref, g_ref, d_ref, m2_ref, v2_ref = refs[n_layers:]
        g = _select_layer_sum(refs[:n_layers])
        m2 = ADAM_B1 * m_ref[...] + (1.0 - ADAM_B1) * g
        v2 = ADAM_B2 * v_ref[...] + (1.0 - ADAM_B2) * (g * g)
        m_hat = m2 / (1.0 - ADAM_B1 ** ADAM_STEP)
        v_hat = v2 / (1.0 - ADAM_B2 ** ADAM_STEP)
        g_ref[...] = g
        d_ref[...] = -ADAM_LR * (m_hat / (jnp.sqrt(v_hat) + ADAM_EPS) + ADAM_WD * w_ref[...])
        m2_ref[...] = m2
        v2_ref[...] = v2

    slab = pl.BlockSpec((None, tr, cols), lambda l, i: (l, i, 0))
    out = jax.ShapeDtypeStruct((n_layers, rows, cols), F32)
    p_specs = _layer_parts_specs(n_layers, parts[0].shape[0], tr, cols)
    return pl.pallas_call(
        body,
        name=name,
        grid=(n_layers, rows // tr),
        in_specs=p_specs + [slab, slab, slab],
        out_specs=[slab, slab, slab, slab],
        out_shape=[out, out, out, out],
        compiler_params=_cparams(("arbitrary", "arbitrary")),
    )(*parts, w, m, v)


SLAB_ROWS = 32
_SMALL_SHARDED = (("conv_w", (2, 31, 32)), ("ffn_conv_w", (2, 3, 704)))
_REPLICATED = (("g_mix", (2, 1024)), ("conv_b", (2, 256)), ("conv_ln_g", (2, 256)), ("conv_ln_b", (2, 256)),
               ("sgu_ln_g", (2, 256)), ("sgu_ln_b", (2, 256)), ("sgu_w", (2, 4, 128, 128)), ("sgu_b", (2, 4, 128)),
               ("g_out", (2, 1024)), ("g_ffn", (2, 1024)), ("ffn_conv_b", (2, 5632)), ("g_final", (1024,)))


def _seg_rows(n_elems):
    return -(-n_elems // LANES)


def _pack(arrays, lead=()):
    segs = []
    for a in arrays:
        flat = a.reshape(lead + (-1,)).astype(F32)
        pad = _seg_rows(flat.shape[-1]) * LANES - flat.shape[-1]
        if pad:
            flat = jnp.pad(flat, [(0, 0)] * len(lead) + [(0, pad)])
        segs.append(flat)
    flat = jnp.concatenate(segs, axis=-1)
    rows = flat.shape[-1] // LANES
    pad_rows = -rows % SLAB_ROWS
    if pad_rows:
        flat = jnp.pad(flat, [(0, 0)] * len(lead) + [(0, pad_rows * LANES)])
    return flat.reshape(lead + (rows + pad_rows, LANES))


def _unpack(slab, shapes, lead=()):
    flat = slab.reshape(lead + (-1,))
    out, off = [], 0
    for shape in shapes:
        n = math.prod(shape)
        out.append(flat[..., off:off + n].reshape(lead + tuple(shape)))
        off += _seg_rows(n) * LANES
    return out


def _split_last(full):
    split = full.shape[:-1] + (N_DEV, full.shape[-1] // N_DEV)
    return jnp.moveaxis(full.reshape(split), -2, 0)


def _join_last(blocks):
    moved = jnp.moveaxis(blocks, 0, -2)
    return moved.reshape(moved.shape[:-2] + (moved.shape[-2] * moved.shape[-1],))


def _gathered(wt, n, l, after):
    if isinstance(wt[n][l], _SplitExchange):
        wt[n][l] = wt[n][l].finish(after)
    return wt[n][l]


def _layer_fwd(l, x, wt, small):
    tag = f"l{l}"
    h = _rmsnorm_fwd(x, small["g_mix"][l][None], name=f"{tag}_norm_mix")
    w_in_t = _gathered(wt, "w_in_t", l, h)
    p = _matmul(h, w_in_t, "nt", name=f"{tag}_proj", out_dtype=BF16)
    p_ab = qkv = p
    ya = _mixer_a_fwd(p_ab, wt["conv_w"][l], small["conv_b"][l][None], small["conv_ln_g"][l][None],
                      small["conv_ln_b"][l][None], name=f"{tag}_mixer_a")
    bias = jnp.repeat(small["sgu_b"][l].T, HEAD_DIM, axis=1)
    yb = _sgu_fwd(p_ab, small["sgu_ln_g"][l][None], small["sgu_ln_b"][l][None], small["sgu_w"][l], bias,
                  name=f"{tag}_sgu")
    yc, t_tot = _attn_fwd(qkv, name=f"{tag}_attn")
    y = _combine_fwd(ya, yb, yc, small["g_out"][l][None], name=f"{tag}_combine")
    x1 = _matmul(y, _gathered(wt, "w_out", l, y), "nn", name=f"{tag}_out_proj", residual=x)
    h2 = _rmsnorm_fwd(x1, small["g_ffn"][l][None], name=f"{tag}_norm_ffn")
    up0 = _matmul(h2, _gathered(wt, "w_up_t", l, h2), "nt", name=f"{tag}_up", out_dtype=BF16)
    act = _ffn_act_fwd(up0, wt["ffn_conv_w"][l], small["ffn_conv_b"][l][None], name=f"{tag}_ffn_act")
    x2 = _matmul(act, _gathered(wt, "w_down", l, act), "nn", name=f"{tag}_down", residual=x1)
    saved = dict(x=x, h=h, p_ab=p, qkv=p, ya=ya, yb=yb, yc=yc, t_tot=t_tot, y=y, x1=x1, h2=h2, up0=up0,
                 act=act, bias=bias)
    return x2, saved


def _layer_bwd(l, dres, sv, wt, small, scattering, token):
    tag = f"l{l}b"
    g = {}

    def scatter(n, partial):
        scattering[n][l] = _SplitExchange(partial, kind="scatter_rows", name=f"scatter_{n}_l{l}")
        return _SplitExchange.start([scattering[n][l]], name=f"scatter_{n}_l{l}_start")

    dx2, dx2_b = dres
    dact = _matmul(dx2_b, wt["w_down"][l], "nt", name=f"{tag}_dact", out_dtype=BF16)
    tok = scatter("w_down", _matmul(sv["act"], dx2_b, "tn", name=f"{tag}_dw_down", out_dtype=BF16))
    dup_g, dup_v, dwg, dwv, dbg, dbv = _ffn_act_bwd(sv["up0"], dact, wt["ffn_conv_w"][l], small["ffn_conv_b"][l][None] + tok + token,
                                                    name=f"{tag}_ffn_act")
    g["ffn_conv_w"] = jnp.concatenate([dwg, dwv], axis=1)
    g["ffn_conv_b"] = jnp.concatenate([dbg[0], dbv[0]])
    dh2 = _matmul(dup_g, wt["w_up_t"][l], "nn", name=f"{tag}_dh2_gate")
    dh2 = _matmul(dup_v, wt["w_up_t"][l], "nn", name=f"{tag}_dh2_val", b_k0=D_FF, residual=dh2)
    dw_up = _matmul(dup_g, sv["h2"], "tn", name=f"{tag}_dw_up_gate", out_dtype=BF16, rows=(2 * D_FF, 0))
    dw_up = _matmul(dup_v, sv["h2"], "tn", name=f"{tag}_dw_up_val", out_dtype=BF16, rows=(2 * D_FF, D_FF), into=dw_up)
    tok = scatter("w_up_t", dw_up)
    dx1, dx1_b, dg = _rmsnorm_bwd(sv["x1"], small["g_ffn"][l][None] + tok, dh2, dx2, name=f"{tag}_norm_ffn")
    g["g_ffn"] = dg[0]
    dy = _matmul(dx1_b, wt["w_out"][l], "nt", name=f"{tag}_dy")
    tok = scatter("w_out", _matmul(sv["y"], dx1_b, "tn", name=f"{tag}_dw_out", out_dtype=BF16))
    dya, dyb, dyc, dg = _combine_bwd(dy, sv["ya"], sv["yb"], sv["yc"], small["g_out"][l][None] + tok,
                                     name=f"{tag}_combine")
    g["g_out"] = dg[0]
    dq, dk, dv = _attn_bwd(sv["qkv"], sv["t_tot"], dyc, name=f"{tag}_attn")
    dp_b, g["sgu_w"], db, dg, dbeta = _sgu_bwd(sv["p_ab"], dyb, small["sgu_ln_g"][l][None], small["sgu_ln_b"][l][None],
                                               small["sgu_w"][l], sv["bias"], name=f"{tag}_sgu")
    g["sgu_b"] = db[:, :N_SGU_HEADS].T
    g["sgu_ln_g"], g["sgu_ln_b"] = dg[0], dbeta[0]
    dp_a, g["conv_w"], dcb, dg, dbeta = _mixer_a_bwd(sv["p_ab"], dya, wt["conv_w"][l], small["conv_b"][l][None],
                                                     small["conv_ln_g"][l][None], small["conv_ln_b"][l][None],
                                                     name=f"{tag}_mixer_a")
    g["conv_b"], g["conv_ln_g"], g["conv_ln_b"] = dcb[0], dg[0], dbeta[0]
    dp = jnp.concatenate([dp_a, dp_b, dq, dk.astype(BF16), dv.astype(BF16)], axis=1)
    tok = scatter("w_in_t", _matmul(dp, sv["h"], "tn", name=f"{tag}_dw_in", out_dtype=BF16))
    dh = _matmul(dp, wt["w_in_t"][l], "nn", name=f"{tag}_dh")
    dx, dx_b, dg = _rmsnorm_bwd(sv["x"], small["g_mix"][l][None] + tok, dh, dx1, name=f"{tag}_norm_mix")
    g["g_mix"] = dg[0]
    return (dx, dx_b), g


_BIG = ("w_in_t", "w_out", "w_up_t", "w_down")


def kernel(x, g_mix, w_in, conv_w, conv_b, conv_ln_g, conv_ln_b, sgu_ln_g, sgu_ln_b, sgu_w, sgu_b, g_out, w_out, g_ffn, w_up, ffn_conv_w, ffn_conv_b, w_down, g_final, loss_target, m_g_mix, m_w_in, m_conv_w, m_conv_b, m_conv_ln_g, m_conv_ln_b, m_sgu_ln_g, m_sgu_ln_b, m_sgu_w, m_sgu_b, m_g_out, m_w_out, m_g_ffn, m_w_up, m_ffn_conv_w, m_ffn_conv_b, m_w_down, m_g_final, v_g_mix, v_w_in, v_conv_w, v_conv_b, v_conv_ln_g, v_conv_ln_b, v_sgu_ln_g, v_sgu_ln_b, v_sgu_w, v_sgu_b, v_g_out, v_w_out, v_g_ffn, v_w_up, v_ffn_conv_w, v_ffn_conv_b, v_w_down, v_g_final):
    given = dict(locals())
    n_layers = g_mix.shape[0]
    layers = range(n_layers)
    small_sharded = [n for n, _ in _SMALL_SHARDED]
    replicated = [n for n, _ in _REPLICATED]
    small = {n: given[n] for n in replicated}

    filters = _SplitExchange(_pack([given[n] for n in small_sharded]), kind="gather_blocks", name="gather_filters")
    wt = {n: [None] * n_layers for n in _BIG}
    wt["w_in_t"][0] = _SplitExchange(w_in[0].T.astype(BF16), kind="gather_rows", name="gather_w_in_t_l0")
    tok = _SplitExchange.start([filters, wt["w_in_t"][0]], name="gather_first_start")
    w_in, w_out, w_up, w_down, tok = lax.optimization_barrier((w_in, w_out, w_up, w_down, tok))
    shard = {"w_in_t": [w_in[l].T.astype(BF16) for l in layers], "w_out": [w_out[l].astype(BF16) for l in layers],
             "w_up_t": [w_up[l].T.astype(BF16) for l in layers], "w_down": [w_down[l].astype(BF16) for l in layers]}
    later = [(n, l) for l in layers for n in _BIG if (n, l) != ("w_in_t", 0)]
    for n, l in later:
        wt[n][l] = _SplitExchange(shard[n][l], kind="gather_rows", name=f"gather_{n}_l{l}")
    small["g_mix"] = g_mix + tok + _SplitExchange.start([wt[n][l] for n, l in later], name="gather_weights_start")
    gathered_filters = filters.finish(small["g_mix"])
    for n, blocks in zip(small_sharded, _unpack(gathered_filters, [s for _, s in _SMALL_SHARDED], lead=(N_DEV,))):
        wt[n] = _join_last(blocks)

    xs = x[0]
    saved = []
    for l in layers:
        xs, sv = _layer_fwd(l, xs, wt, small)
        saved.append(sv)
    loss_tile, dx, dx_b, dgf = _loss_head(xs, g_final[None], loss_target[0], name="loss_head")
    dres = (dx, dx_b)
    scattering = {n: [None] * n_layers for n in _BIG}
    layered = [n for n in replicated if n not in ("g_final", "sgu_w")]
    slabs, sgu_w_parts = [None] * n_layers, [None] * n_layers
    tok = 0.0
    for l in reversed(layers):
        dres, g = _layer_bwd(l, dres, saved[l], wt, small, scattering, tok)
        own = _pack([_split_last(g[n]) for n in small_sharded], lead=(N_DEV,))
        shared = _pack([g[n] for n in layered] + ([dgf[0]] if l == n_layers - 1 else []))
        slab = jnp.concatenate([own, jnp.broadcast_to(shared[None], (N_DEV,) + shared.shape)], axis=1)
        slabs[l] = _SplitExchange(slab, kind="scatter_blocks", name=f"scatter_small_grads_l{l}")
        sgu_w_parts[l] = _SplitExchange(g["sgu_w"].reshape(-1, LANES).astype(BF16), kind="gather_blocks",
                                        name=f"gather_sgu_w_grads_l{l}")
        tok = _SplitExchange.start([slabs[l], sgu_w_parts[l]], name=f"small_grads_l{l}_start")
    n_own = own.shape[1]

    after_backward = jnp.full((8, LANES), tok)
    received = {n: [scattering[n][l].finish(after_backward) for l in layers] for n in _BIG}
    out = {}

    def update(n, parts, transposed=False):
        turn = (lambda a: jnp.swapaxes(a, 1, 2)) if transposed else (lambda a: a)
        results = _adamw(parts, turn(given[n]), turn(given["m_" + n]), turn(given["v_" + n]), name=f"adamw_{n}")
        for pre, res in zip(("grad_", "delta_", "new_m_", "new_v_"), results):
            out[pre + n] = turn(res)
        return results[0][0, :8, :LANES]

    update("w_out", received["w_out"])
    update("w_down", received["w_down"])
    update("w_in", received["w_in_t"], transposed=True)
    big_updated = update("w_up", received["w_up_t"], transposed=True)

    as_rows = lambda a: a.reshape(n_layers, -1, LANES)
    results = _adamw([ex.finish(big_updated) for ex in sgu_w_parts], as_rows(sgu_w), as_rows(m_sgu_w), as_rows(v_sgu_w),
                     name="adamw_sgu_w")
    for pre, res in zip(("grad_", "delta_", "new_m_", "new_v_"), results):
        out[pre + "sgu_w"] = res.reshape(sgu_w.shape)

    per_layer = {pre + n: [None] * n_layers for pre in ("grad_", "delta_", "new_m_", "new_v_") for n in small_sharded + layered}
    for l in reversed(layers):
        last = l == n_layers - 1
        stacks = [jnp.concatenate([_pack([given[pre + n][l] for n in small_sharded]),
                                   _pack([given[pre + n][l] for n in layered] + ([given[pre + "g_final"]] if last else []))])[None]
                  for pre in ("", "m_", "v_")]
        results = _adamw([slabs[l].finish(big_updated)], *stacks, name=f"adamw_small_l{l}")
        for pre, res in zip(("grad_", "delta_", "new_m_", "new_v_"), results):
            unpacked = (_unpack(res[0, :n_own], [s[1:] for _, s in _SMALL_SHARDED])
                        + _unpack(res[0, n_own:], [s[1:] for n, s in _REPLICATED if n in layered] + ([g_final.shape] if last else [])))
            for n, a in zip(small_sharded + layered + (["g_final"] if last else []), unpacked):
                if n == "g_final":
                    out[pre + n] = a
                else:
                    per_layer[pre + n][l] = a
    for name, parts in per_layer.items():
        out[name] = jnp.stack(parts)

    loss = lax.psum(loss_tile[0, 0], ("x", "y", "c"))
    order = list(_WEIGHT_ORDER)
    return (loss, dres[0][None], *[out["grad_" + n] for n in order], *[out["delta_" + n] for n in order],
            *[out["new_m_" + n] for n in order], *[out["new_v_" + n] for n in order])


_WEIGHT_ORDER = ("g_mix", "w_in", "conv_w", "conv_b", "conv_ln_g", "conv_ln_b", "sgu_ln_g", "sgu_ln_b", "sgu_w", "sgu_b",
                 "g_out", "w_out", "g_ffn", "w_up", "ffn_conv_w", "ffn_conv_b", "w_down", "g_final")
```

```python
import math

import jax
import jax.numpy as jnp
from jax import lax
from jax.experimental import pallas as pl
from jax.experimental.pallas import tpu as pltpu

F32 = jnp.float32
BF16 = jnp.bfloat16

N_DEV = 8
D_MODEL = 1024
HEAD_DIM = 64
D_CONV = 256
D_SGU = 256
D_SB = 512
D_AB = 2 * D_CONV + 2 * D_SGU
D_QKV = 3 * D_SB
D_IN = D_AB + D_QKV
CONV_K = 31
CONV_HALO = 32
FFN_K = 3
FFN_HALO = 8
D_FF = 2816
CHUNK = 128
EPS = 1e-6
LANES = 128

ADAM_LR = 0.001
ADAM_B1 = 0.9
ADAM_B2 = 0.999
ADAM_EPS = 1e-08
ADAM_WD = 0.01
ADAM_STEP = 10

VMEM_LIMIT = 56 * 1024 * 1024


def _cparams(sem=None):
    return pltpu.CompilerParams(dimension_semantics=sem, vmem_limit_bytes=VMEM_LIMIT)


def _tile(n, prefs=(512, 256, 128)):
    for t in prefs:
        if n % t == 0:
            return t
    return n


def _sigmoid(x):
    return 1.0 / (1.0 + jnp.exp(-x))


_INV_SQRT2 = 1.0 / math.sqrt(2.0)
_INV_SQRT2PI = 1.0 / math.sqrt(2.0 * math.pi)


def _gelu(x):
    return 0.5 * x * (1.0 + lax.erf(x * _INV_SQRT2))


def _gelu_grad(x):
    return 0.5 * (1.0 + lax.erf(x * _INV_SQRT2)) + x * jnp.exp(-0.5 * x * x) * _INV_SQRT2PI


def _dot(a, b, dims):
    return lax.dot_general(a, b, (dims, ((), ())), preferred_element_type=F32)


_NN = ((1,), (0,))
_NT = ((1,), (1,))
_TN = ((0,), (0,))


def _split_bf16(x):
    hi = x.astype(BF16)
    lo = (x - hi.astype(F32)).astype(BF16)
    return jnp.concatenate([hi, lo], axis=1)


def _matmul(a, b, mode, *, name, out_dtype=F32, residual=None, n=None, b_n0=0, b_k0=0, rows=None, into=None):
    if mode == "nn":
        (m, k), n = a.shape, (n or b.shape[1])
    elif mode == "nt":
        (m, k), n = a.shape, (n or b.shape[0])
    else:
        (k, m), n = a.shape, b.shape[1]
    has_res = residual is not None
    tm, tn = _matmul_tiles(m, n, k, a.dtype.itemsize, b.dtype.itemsize, jnp.dtype(out_dtype).itemsize, has_res, b_n0)
    j0 = b_n0 // tn
    total_rows, first_row = rows or (m, 0)
    assert b_k0 % k == 0 and first_row % tm == 0
    kb, i0 = b_k0 // k, first_row // tm

    if mode == "nn":
        a_spec = pl.BlockSpec((tm, k), lambda i, j: (i, 0))
        b_spec = pl.BlockSpec((k, tn), lambda i, j: (kb, j + j0))
        dims = _NN
    elif mode == "nt":
        a_spec = pl.BlockSpec((tm, k), lambda i, j: (i, 0))
        b_spec = pl.BlockSpec((tn, k), lambda i, j: (j + j0, 0))
        dims = _NT
    else:
        a_spec = pl.BlockSpec((k, tm), lambda i, j: (0, i))
        b_spec = pl.BlockSpec((k, tn), lambda i, j: (0, j))
        dims = _TN
    o_spec = pl.BlockSpec((tm, tn), lambda i, j: (i + i0, j))
    r_spec = pl.BlockSpec((tm, tn), lambda i, j: (i, j))

    def body(*refs):
        a_ref, b_ref = refs[:2]
        acc = _dot(a_ref[...].astype(BF16), b_ref[...].astype(BF16), dims)
        if has_res:
            acc = acc + refs[2][...]
        refs[-1][...] = acc.astype(out_dtype)

    in_specs = [a_spec, b_spec] + ([r_spec] if has_res else [])
    args = (a, b) + ((residual,) if has_res else ())
    aliases = {}
    if into is not None:
        aliases = {len(args): 0}
        in_specs.append(pl.BlockSpec(memory_space=pl.ANY))
        args += (into,)

        def body(*refs, inner=body):
            inner(*refs[:len(args) - 1], refs[-1])

    return pl.pallas_call(
        body,
        name=name,
        grid=(m // tm, n // tn),
        in_specs=in_specs,
        out_specs=o_spec,
        out_shape=jax.ShapeDtypeStruct((total_rows, n), out_dtype),
        input_output_aliases=aliases,
        compiler_params=_cparams(("parallel", "parallel")),
    )(*args)


MATMUL_VMEM_BUDGET = 40 * 1024 * 1024
MATMUL_MIN_STEPS = 8


def _matmul_tiles(m, n, k, a_bytes, b_bytes, out_bytes, has_res, n_offset):
    def divisors(size, cap, also=0):
        return [t for t in range(cap, 0, -LANES) if size % t == 0 and also % t == 0] or [size]

    fitting = []
    for tm in divisors(m, 1024):
        for tn in divisors(n, 1408, n_offset):
            blocks = tm * k * a_bytes + k * tn * b_bytes + tm * tn * (out_bytes + (4 if has_res else 0))
            if 2 * blocks <= MATMUL_VMEM_BUDGET:
                fitting.append((tm, tn))
    if not fitting:
        raise ValueError(f"no matmul tiling for {m} x {n} x {k}")
    steps = lambda t: (m // t[0]) * (n // t[1])
    enough = [t for t in fitting if steps(t) >= MATMUL_MIN_STEPS]
    return max(enough, key=lambda t: (t[0] * t[1], t[1])) if enough else max(fitting, key=steps)


ROW_TILE = 256


def _rmsnorm_fwd(x, g, *, name):
    s, d = x.shape

    def body(x_ref, g_ref, h_ref):
        xv = x_ref[...]
        r = lax.rsqrt(jnp.mean(xv * xv, axis=-1, keepdims=True) + EPS)
        h_ref[...] = (xv * r * g_ref[...]).astype(BF16)

    return pl.pallas_call(
        body,
        name=name,
        grid=(s // ROW_TILE,),
        in_specs=[pl.BlockSpec((ROW_TILE, d), lambda i: (i, 0)), pl.BlockSpec((1, d), lambda i: (0, 0))],
        out_specs=pl.BlockSpec((ROW_TILE, d), lambda i: (i, 0)),
        out_shape=jax.ShapeDtypeStruct((s, d), BF16),
        compiler_params=_cparams(("parallel",)),
    )(x, g)


def _rmsnorm_bwd(x, g, dh, dres, *, name):
    s, d = x.shape

    def body(x_ref, g_ref, dh_ref, dres_ref, dx_ref, dxb_ref, dg_ref):
        xv = x_ref[...]
        r = lax.rsqrt(jnp.mean(xv * xv, axis=-1, keepdims=True) + EPS)
        xhat = xv * r
        dhv = dh_ref[...]
        dxhat = dhv * g_ref[...]
        dx = dres_ref[...] + r * (dxhat - xhat * jnp.mean(dxhat * xhat, axis=-1, keepdims=True))
        dx_ref[...] = dx
        dxb_ref[...] = dx.astype(BF16)
        part = jnp.sum(dhv * xhat, axis=0, keepdims=True)

        @pl.when(pl.program_id(0) == 0)
        def _():
            dg_ref[...] = part

        @pl.when(pl.program_id(0) > 0)
        def _():
            dg_ref[...] += part

    row = pl.BlockSpec((ROW_TILE, d), lambda i: (i, 0))
    vec = pl.BlockSpec((1, d), lambda i: (0, 0))
    return pl.pallas_call(
        body,
        name=name,
        grid=(s // ROW_TILE,),
        in_specs=[row, vec, row, row],
        out_specs=[row, row, vec],
        out_shape=[jax.ShapeDtypeStruct((s, d), F32), jax.ShapeDtypeStruct((s, d), BF16),
                   jax.ShapeDtypeStruct((1, d), F32)],
        compiler_params=_cparams(("arbitrary",)),
    )(x, g, dh, dres)


def _loss_head(x, g, target, *, name):
    s, d = x.shape

    def body(x_ref, g_ref, t_ref, loss_ref, dx_ref, dxb_ref, dg_ref):
        xv = x_ref[...]
        gv = g_ref[...]
        r = lax.rsqrt(jnp.mean(xv * xv, axis=-1, keepdims=True) + EPS)
        xhat = xv * r
        diff = xhat * gv - t_ref[...]
        dy = diff * (1.0 / d)
        dxhat = dy * gv
        dx = r * (dxhat - xhat * jnp.mean(dxhat * xhat, axis=-1, keepdims=True))
        dx_ref[...] = dx
        dxb_ref[...] = dx.astype(BF16)
        dg_part = jnp.sum(dy * xhat, axis=0, keepdims=True)
        row_loss = jnp.sum(diff * diff, axis=-1, keepdims=True)
        loss_part = jnp.sum(row_loss, axis=0, keepdims=True) * (0.5 / d)

        @pl.when(pl.program_id(0) == 0)
        def _():
            dg_ref[...] = dg_part
            loss_ref[...] = jnp.broadcast_to(loss_part, loss_ref.shape)

        @pl.when(pl.program_id(0) > 0)
        def _():
            dg_ref[...] += dg_part
            loss_ref[...] += jnp.broadcast_to(loss_part, loss_ref.shape)

    row = pl.BlockSpec((ROW_TILE, d), lambda i: (i, 0))
    vec = pl.BlockSpec((1, d), lambda i: (0, 0))
    tile = pl.BlockSpec((8, LANES), lambda i: (0, 0))
    return pl.pallas_call(
        body,
        name=name,
        grid=(s // ROW_TILE,),
        in_specs=[row, vec, row],
        out_specs=[tile, row, row, vec],
        out_shape=[jax.ShapeDtypeStruct((8, LANES), F32), jax.ShapeDtypeStruct((s, d), F32),
                   jax.ShapeDtypeStruct((s, d), BF16), jax.ShapeDtypeStruct((1, d), F32)],
        compiler_params=_cparams(("arbitrary",)),
    )(x, g, target)


_BRANCHES = ((0, D_CONV), (D_CONV, D_SGU), (D_CONV + D_SGU, D_SB))


def _combine_fwd(ya, yb, yc, g, *, name):
    s = ya.shape[0]

    def body(ya_ref, yb_ref, yc_ref, g_ref, y_ref):
        for ref, (off, w) in zip((ya_ref, yb_ref, yc_ref), _BRANCHES):
            v = ref[...]
            r = lax.rsqrt(jnp.mean(v * v, axis=-1, keepdims=True) + EPS)
            y_ref[:, off:off + w] = (v * r * g_ref[:, off:off + w]).astype(BF16)

    def row(w):
        return pl.BlockSpec((ROW_TILE, w), lambda i: (i, 0))

    return pl.pallas_call(
        body,
        name=name,
        grid=(s // ROW_TILE,),
        in_specs=[row(D_CONV), row(D_SGU), row(D_SB), pl.BlockSpec((1, D_MODEL), lambda i: (0, 0))],
        out_specs=row(D_MODEL),
        out_shape=jax.ShapeDtypeStruct((s, D_MODEL), BF16),
        compiler_params=_cparams(("parallel",)),
    )(ya, yb, yc, g)


def _combine_bwd(dy, ya, yb, yc, g, *, name):
    s = ya.shape[0]

    def body(dy_ref, ya_ref, yb_ref, yc_ref, g_ref, dya_ref, dyb_ref, dyc_ref, dg_ref):
        first = pl.program_id(0) == 0
        for ref, dref, (off, w) in zip((ya_ref, yb_ref, yc_ref), (dya_ref, dyb_ref, dyc_ref), _BRANCHES):
            v = ref[...]
            r = lax.rsqrt(jnp.mean(v * v, axis=-1, keepdims=True) + EPS)
            n = v * r
            dout = dy_ref[:, off:off + w]
            dn = dout * g_ref[:, off:off + w]
            dref[...] = r * (dn - n * jnp.mean(dn * n, axis=-1, keepdims=True))
            part = jnp.sum(dout * n, axis=0, keepdims=True)

            @pl.when(first)
            def _():
                dg_ref[:, off:off + w] = part

            @pl.when(jnp.logical_not(first))
            def _():
                dg_ref[:, off:off + w] += part

    def row(w):
        return pl.BlockSpec((ROW_TILE, w), lambda i: (i, 0))

    vec = pl.BlockSpec((1, D_MODEL), lambda i: (0, 0))
    return pl.pallas_call(
        body,
        name=name,
        grid=(s // ROW_TILE,),
        in_specs=[row(D_MODEL), row(D_CONV), row(D_SGU), row(D_SB), vec],
        out_specs=[row(D_CONV), row(D_SGU), row(D_SB), vec],
        out_shape=[jax.ShapeDtypeStruct((s, D_CONV), F32), jax.ShapeDtypeStruct((s, D_SGU), F32),
                   jax.ShapeDtypeStruct((s, D_SB), F32), jax.ShapeDtypeStruct((1, D_MODEL), F32)],
        compiler_params=_cparams(("arbitrary",)),
    )(dy, ya, yb, yc, g)


CONV_TILE = 128


def _shift_down(window, j, halo):
    return pltpu.roll(window, j, 0)[halo:, :] if j else window[halo:, :]


def _shift_up(window, j, n_out):
    n = window.shape[0]
    return pltpu.roll(window, n - j, 0)[:n_out, :] if j else window[:n_out, :]


def _mixer_a_fwd(p_ab, conv_w, conv_b, ln_g, ln_b, *, name):
    s = p_ab.shape[0]
    nt = s // CONV_TILE

    def body(p_ref, w_ref, b_ref, g_ref, beta_ref, y_ref, h_ref):
        h_ref[0:CONV_HALO, :] = jnp.zeros((CONV_HALO, D_CONV), F32)

        def glu(i, c):
            t0 = pl.multiple_of(i * CONV_TILE, CONV_TILE)
            a = p_ref[pl.ds(t0, CONV_TILE), 0:D_CONV].astype(F32)
            gate = p_ref[pl.ds(t0, CONV_TILE), D_CONV:2 * D_CONV].astype(F32)
            h_ref[pl.ds(t0 + CONV_HALO, CONV_TILE), :] = a * _sigmoid(gate)
            return c

        lax.fori_loop(0, nt, glu, 0)

        def conv(i, c):
            t0 = pl.multiple_of(i * CONV_TILE, CONV_TILE)
            window = h_ref[pl.ds(t0, CONV_TILE + CONV_HALO), :]
            acc = jnp.zeros((CONV_TILE, D_CONV), F32) + b_ref[...]
            for k in range(CONV_K):
                acc = acc + w_ref[k:k + 1, :] * _shift_down(window, CONV_K - 1 - k, CONV_HALO)
            mu = jnp.mean(acc, axis=-1, keepdims=True)
            xc = acc - mu
            rstd = lax.rsqrt(jnp.mean(xc * xc, axis=-1, keepdims=True) + EPS)
            z = xc * rstd * g_ref[...] + beta_ref[...]
            y_ref[pl.ds(t0, CONV_TILE), :] = z * _sigmoid(z)
            return c

        lax.fori_loop(0, nt, conv, 0)

    full = lambda shape: pl.BlockSpec(shape, lambda i: (0, 0))
    return pl.pallas_call(
        body,
        name=name,
        grid=(1,),
        in_specs=[full((s, 2 * D_CONV)), full((CONV_K, D_CONV)), full((1, D_CONV)), full((1, D_CONV)),
                  full((1, D_CONV))],
        out_specs=full((s, D_CONV)),
        out_shape=jax.ShapeDtypeStruct((s, D_CONV), F32),
        scratch_shapes=[pltpu.VMEM((s + CONV_HALO, D_CONV), F32)],
        compiler_params=_cparams(("arbitrary",)),
    )(p_ab, conv_w, conv_b, ln_g, ln_b)


def _mixer_a_bwd(p_ab, dya, conv_w, conv_b, ln_g, ln_b, *, name):
    s = p_ab.shape[0]
    nt = s // CONV_TILE

    def body(p_ref, dy_ref, w_ref, b_ref, g_ref, beta_ref, dp_ref, dw_ref, db_ref, dg_ref, dbeta_ref, h_ref, dc_ref):
        h_ref[0:CONV_HALO, :] = jnp.zeros((CONV_HALO, D_CONV), F32)
        dc_ref[s:s + CONV_HALO, :] = jnp.zeros((CONV_HALO, D_CONV), F32)
        dw_ref[...] = jnp.zeros_like(dw_ref)
        db_ref[...] = jnp.zeros_like(db_ref)
        dg_ref[...] = jnp.zeros_like(dg_ref)
        dbeta_ref[...] = jnp.zeros_like(dbeta_ref)

        def glu(i, c):
            t0 = pl.multiple_of(i * CONV_TILE, CONV_TILE)
            a = p_ref[pl.ds(t0, CONV_TILE), 0:D_CONV].astype(F32)
            gate = p_ref[pl.ds(t0, CONV_TILE), D_CONV:2 * D_CONV].astype(F32)
            h_ref[pl.ds(t0 + CONV_HALO, CONV_TILE), :] = a * _sigmoid(gate)
            return c

        lax.fori_loop(0, nt, glu, 0)

        def conv_bwd(i, c):
            t0 = pl.multiple_of(i * CONV_TILE, CONV_TILE)
            window = h_ref[pl.ds(t0, CONV_TILE + CONV_HALO), :]
            taps = [_shift_down(window, CONV_K - 1 - k, CONV_HALO) for k in range(CONV_K)]
            acc = jnp.zeros((CONV_TILE, D_CONV), F32) + b_ref[...]
            for k in range(CONV_K):
                acc = acc + w_ref[k:k + 1, :] * taps[k]
            mu = jnp.mean(acc, axis=-1, keepdims=True)
            xc = acc - mu
            rstd = lax.rsqrt(jnp.mean(xc * xc, axis=-1, keepdims=True) + EPS)
            xhat = xc * rstd
            z = xhat * g_ref[...] + beta_ref[...]
            sg = _sigmoid(z)
            dz = dy_ref[pl.ds(t0, CONV_TILE), :] * (sg * (1.0 + z * (1.0 - sg)))
            dg_ref[...] += jnp.sum(dz * xhat, axis=0, keepdims=True)
            dbeta_ref[...] += jnp.sum(dz, axis=0, keepdims=True)
            dxhat = dz * g_ref[...]
            dc = rstd * (dxhat - jnp.mean(dxhat, axis=-1, keepdims=True)
                         - xhat * jnp.mean(dxhat * xhat, axis=-1, keepdims=True))
            dc_ref[pl.ds(t0, CONV_TILE), :] = dc
            db_ref[...] += jnp.sum(dc, axis=0, keepdims=True)
            for k in range(CONV_K):
                dw_ref[k:k + 1, :] += jnp.sum(dc * taps[k], axis=0, keepdims=True)
            return c

        lax.fori_loop(0, nt, conv_bwd, 0)

        def glu_bwd(i, c):
            t0 = pl.multiple_of(i * CONV_TILE, CONV_TILE)
            window = dc_ref[pl.ds(t0, CONV_TILE + CONV_HALO), :]
            dh = jnp.zeros((CONV_TILE, D_CONV), F32)
            for j in range(CONV_K):
                dh = dh + w_ref[CONV_K - 1 - j:CONV_K - j, :] * _shift_up(window, j, CONV_TILE)
            a = p_ref[pl.ds(t0, CONV_TILE), 0:D_CONV].astype(F32)
            sg = _sigmoid(p_ref[pl.ds(t0, CONV_TILE), D_CONV:2 * D_CONV].astype(F32))
            dp_ref[pl.ds(t0, CONV_TILE), 0:D_CONV] = (dh * sg).astype(BF16)
            dp_ref[pl.ds(t0, CONV_TILE), D_CONV:2 * D_CONV] = (dh * a * sg * (1.0 - sg)).astype(BF16)
            return c

        lax.fori_loop(0, nt, glu_bwd, 0)

    full = lambda shape: pl.BlockSpec(shape, lambda i: (0, 0))
    vec = jax.ShapeDtypeStruct((1, D_CONV), F32)
    return pl.pallas_call(
        body,
        name=name,
        grid=(1,),
        in_specs=[full((s, 2 * D_CONV)), full((s, D_CONV)), full((CONV_K, D_CONV)), full((1, D_CONV)),
                  full((1, D_CONV)), full((1, D_CONV))],
        out_specs=[full((s, 2 * D_CONV)), full((CONV_K, D_CONV)), full((1, D_CONV)), full((1, D_CONV)),
                   full((1, D_CONV))],
        out_shape=[jax.ShapeDtypeStruct((s, 2 * D_CONV), BF16), jax.ShapeDtypeStruct((CONV_K, D_CONV), F32),
                   vec, vec, vec],
        scratch_shapes=[pltpu.VMEM((s + CONV_HALO, D_CONV), F32), pltpu.VMEM((s + CONV_HALO, D_CONV), F32)],
        compiler_params=_cparams(("arbitrary",)),
    )(p_ab, dya, conv_w, conv_b, ln_g, ln_b)


N_SGU_HEADS = D_SGU // HEAD_DIM


def _head_masks(width):
    lane = lax.broadcasted_iota(jnp.int32, (1, width), 1)
    return [(lane >= h * HEAD_DIM) & (lane < (h + 1) * HEAD_DIM) for h in range(width // HEAD_DIM)]


def _tril_mask():
    r = lax.broadcasted_iota(jnp.int32, (CHUNK, CHUNK), 0)
    c = lax.broadcasted_iota(jnp.int32, (CHUNK, CHUNK), 1)
    return c <= r


def _sgu_norm(bv, g, beta):
    vg = _gelu(bv)
    mu = jnp.mean(vg, axis=-1, keepdims=True)
    xc = vg - mu
    rstd = lax.rsqrt(jnp.mean(xc * xc, axis=-1, keepdims=True) + EPS)
    xhat = xc * rstd
    return xhat, rstd, xhat * g + beta


def _sgu_fwd(p_ab, ln_g, ln_b, w_s, bias, *, name):
    s = p_ab.shape[0]

    def body(p_ref, g_ref, beta_ref, w_ref, bias_ref, y_ref):
        u = _gelu(p_ref[:, 0:D_SGU].astype(F32))
        _, _, vn = _sgu_norm(p_ref[:, D_SGU:2 * D_SGU].astype(F32), g_ref[...], beta_ref[...])
        vb = vn.astype(BF16)
        tril = _tril_mask()
        mixed = bias_ref[...]
        for h, m in enumerate(_head_masks(D_SGU)):
            wh = jnp.where(tril, w_ref[h], 0.0).astype(BF16)
            mixed = mixed + _dot(wh, jnp.where(m, vb, jnp.zeros_like(vb)), _NN)
        y_ref[...] = u * mixed

    return pl.pallas_call(
        body,
        name=name,
        grid=(s // CHUNK,),
        in_specs=[pl.BlockSpec((CHUNK, 2 * D_SGU), lambda i: (i, 1)),
                  pl.BlockSpec((1, D_SGU), lambda i: (0, 0)), pl.BlockSpec((1, D_SGU), lambda i: (0, 0)),
                  pl.BlockSpec((N_SGU_HEADS, CHUNK, CHUNK), lambda i: (0, 0, 0)),
                  pl.BlockSpec((CHUNK, D_SGU), lambda i: (0, 0))],
        out_specs=pl.BlockSpec((CHUNK, D_SGU), lambda i: (i, 0)),
        out_shape=jax.ShapeDtypeStruct((s, D_SGU), F32),
        compiler_params=_cparams(("parallel",)),
    )(p_ab, ln_g, ln_b, w_s, bias)


def _sgu_bwd(p_ab, dyb, ln_g, ln_b, w_s, bias, *, name):
    s = p_ab.shape[0]
    n_chunks = s // CHUNK

    def body(p_ref, dy_ref, g_ref, beta_ref, w_ref, bias_ref, dp_ref, dw_ref, db_ref, dg_ref, dbeta_ref, dbias_ref):
        @pl.when(pl.program_id(0) == 0)
        def _():
            dw_ref[...] = jnp.zeros_like(dw_ref)
            dbias_ref[...] = jnp.zeros_like(dbias_ref)
            dg_ref[...] = jnp.zeros_like(dg_ref)
            dbeta_ref[...] = jnp.zeros_like(dbeta_ref)

        bu = p_ref[:, 0:D_SGU].astype(F32)
        bv = p_ref[:, D_SGU:2 * D_SGU].astype(F32)
        u = _gelu(bu)
        gv = g_ref[...]
        xhat, rstd, vn = _sgu_norm(bv, gv, beta_ref[...])
        vb = vn.astype(BF16)
        tril = _tril_mask()
        masks = _head_masks(D_SGU)
        whs = [jnp.where(tril, w_ref[h], 0.0).astype(BF16) for h in range(N_SGU_HEADS)]
        mixed = bias_ref[...]
        for h, m in enumerate(masks):
            mixed = mixed + _dot(whs[h], jnp.where(m, vb, jnp.zeros_like(vb)), _NN)
        dy = dy_ref[...]
        dp_ref[:, 0:D_SGU] = (dy * mixed * _gelu_grad(bu)).astype(BF16)
        dmixed = dy * u
        dbias_ref[...] += dmixed
        dmb = dmixed.astype(BF16)
        dvn = jnp.zeros((CHUNK, D_SGU), F32)
        for h, m in enumerate(masks):
            dmh = jnp.where(m, dmb, jnp.zeros_like(dmb))
            dvn = dvn + _dot(whs[h], dmh, _TN)
            dw_ref[h] += jnp.where(tril, _dot(dmh, vb, _NT), 0.0)
        dg_ref[...] += jnp.sum(dvn * xhat, axis=0, keepdims=True)
        dbeta_ref[...] += jnp.sum(dvn, axis=0, keepdims=True)
        dxhat = dvn * gv
        dvg = rstd * (dxhat - jnp.mean(dxhat, axis=-1, keepdims=True)
                      - xhat * jnp.mean(dxhat * xhat, axis=-1, keepdims=True))
        dp_ref[:, D_SGU:2 * D_SGU] = (dvg * _gelu_grad(bv)).astype(BF16)

        @pl.when(pl.program_id(0) == n_chunks - 1)
        def _():
            chan = lax.broadcasted_iota(jnp.int32, (D_SGU, LANES), 0)
            head = lax.broadcasted_iota(jnp.int32, (D_SGU, LANES), 1)
            to_head = jnp.where(chan // HEAD_DIM == head, 1.0, 0.0).astype(BF16)
            db_ref[...] = _dot(_split_bf16(dbias_ref[...]), jnp.concatenate([to_head, to_head], axis=0), _NN)

    vec = pl.BlockSpec((1, D_SGU), lambda i: (0, 0))
    wspec = pl.BlockSpec((N_SGU_HEADS, CHUNK, CHUNK), lambda i: (0, 0, 0))
    bspec = pl.BlockSpec((CHUNK, D_SGU), lambda i: (0, 0))
    return pl.pallas_call(
        body,
        name=name,
        grid=(n_chunks,),
        in_specs=[pl.BlockSpec((CHUNK, 2 * D_SGU), lambda i: (i, 1)), pl.BlockSpec((CHUNK, D_SGU), lambda i: (i, 0)),
                  vec, vec, wspec, bspec],
        out_specs=[pl.BlockSpec((CHUNK, 2 * D_SGU), lambda i: (i, 0)), wspec,
                   pl.BlockSpec((CHUNK, LANES), lambda i: (0, 0)), vec, vec],
        out_shape=[jax.ShapeDtypeStruct((s, 2 * D_SGU), BF16),
                   jax.ShapeDtypeStruct((N_SGU_HEADS, CHUNK, CHUNK), F32),
                   jax.ShapeDtypeStruct((CHUNK, LANES), F32),
                   jax.ShapeDtypeStruct((1, D_SGU), F32), jax.ShapeDtypeStruct((1, D_SGU), F32)],
        scratch_shapes=[pltpu.VMEM((CHUNK, D_SGU), F32)],
        compiler_params=_cparams(("arbitrary",)),
    )(p_ab, dyb, ln_g, ln_b, w_s, bias)


N_PAIRS = D_SB // LANES
QKV_BLOCK0 = D_AB // LANES
SB_SCALE = HEAD_DIM ** -0.5


def _sb_logits(z, valid):
    nz = -z
    t = jnp.log(1.0 + jnp.exp(jnp.minimum(z, nz)))
    l1 = jnp.minimum(nz, 0.0) - t
    if valid is not None:
        l1 = jnp.where(valid, l1, 0.0)
    return l1, jnp.minimum(z, 0.0) - t


def _split_hi_lo(x):
    hi = lax.bitcast_convert_type(lax.bitcast_convert_type(x, jnp.uint32) & jnp.uint32(0xFFFF0000), F32)
    return jnp.concatenate([hi, x - hi], axis=1)


def _cumsum_operand(keep):
    half = jnp.concatenate([keep.astype(F32), jnp.ones((CHUNK, CHUNK), F32)], axis=1)
    return jnp.concatenate([half, half], axis=0)


Q_BLOCKS_PER_STEP = 4


def _q_blocks_per_step(nq):
    return next(n for n in (Q_BLOCKS_PER_STEP, 2, 1) if nq % n == 0)


def _attn_fwd(qkv, *, name):
    s = qkv.shape[0]
    nq = s // CHUNK
    per_step = _q_blocks_per_step(nq)

    def body(q_ref, k_ref, v_ref, o_ref, t_ref):
        masks = _head_masks(LANES)
        row = lax.broadcasted_iota(jnp.int32, (CHUNK, CHUNK), 0)
        col = lax.broadcasted_iota(jnp.int32, (CHUNK, CHUNK), 1)
        after_op = _cumsum_operand(row > col)
        cmr = col - row
        zc = jnp.zeros((CHUNK, LANES), F32)

        def q_block(sub, _):
            qi = pl.program_id(1) * per_step + sub
            q_rows = pl.ds(pl.multiple_of(sub * CHUNK, CHUNK), CHUNK)
            q = q_ref[q_rows, :] * SB_SCALE
            zero = jnp.zeros_like(q)
            qs = [jnp.where(m, q, zero) for m in masks]

            def blocks(js, carry):
                o, c0, c1 = carry
                kvs, valids = [], []
                for j in js:
                    k0 = pl.multiple_of(jnp.maximum(j, 0) * CHUNK, CHUNK)
                    kvs.append((k_ref[pl.ds(k0, CHUNK), :], v_ref[pl.ds(k0, CHUNK), :]))
                    valids.append(cmr < jnp.where(j >= 0, (qi - j) * CHUNK, -CHUNK))
                units = [(h, b) for b in range(len(js)) for h in range(2)]
                zs = [_dot(qs[h], kvs[b][0], _NT) for h, b in units]
                logits = [_sb_logits(z, valids[b]) for z, (h, b) in zip(zs, units)]
                sums = [_dot(_split_hi_lo(l1), after_op, _NN) for l1, _ in logits]
                cs = [c0, c1]
                probs = []
                for (h, b), (_, lb), sm in zip(units, logits, sums):
                    probs.append(jnp.where(valids[b], jnp.exp(lb + sm[:, :CHUNK] + cs[h]), 0.0))
                    cs[h] = cs[h] + sm[:, CHUNK:]
                for (h, b), a in zip(units, probs):
                    o = o + _dot(a.astype(BF16), jnp.where(masks[h], kvs[b][1], zero), _NN)
                return o, cs[0], cs[1]

            n_four = (qi + 1) // 4
            carry = lax.fori_loop(0, n_four, lambda jj, c: blocks([qi - 4 * jj - i for i in range(4)], c), (zc,) * 3)
            top = qi - 4 * n_four
            o, c0, c1 = lax.fori_loop(0, (top + 2) // 2, lambda jj, c: blocks([top - 2 * jj, top - 2 * jj - 1], c), carry)
            o_ref[q_rows, :] = o
            t_ref[q_rows, 0:LANES] = c0
            t_ref[q_rows, LANES:2 * LANES] = c1
            return 0

        lax.fori_loop(0, per_step, q_block, 0)

    rows = per_step * CHUNK
    return pl.pallas_call(
        body,
        name=name,
        grid=(N_PAIRS, nq // per_step),
        in_specs=[pl.BlockSpec((rows, LANES), lambda p, i: (i, QKV_BLOCK0 + p)),
                  pl.BlockSpec((s, LANES), lambda p, i: (0, QKV_BLOCK0 + N_PAIRS + p)),
                  pl.BlockSpec((s, LANES), lambda p, i: (0, QKV_BLOCK0 + 2 * N_PAIRS + p))],
        out_specs=[pl.BlockSpec((rows, LANES), lambda p, i: (i, p)),
                   pl.BlockSpec((rows, 2 * LANES), lambda p, i: (i, p))],
        out_shape=[jax.ShapeDtypeStruct((s, D_SB), F32), jax.ShapeDtypeStruct((s, 2 * D_SB), F32)],
        compiler_params=_cparams(("parallel", "parallel")),
    )(qkv, qkv, qkv)


def _attn_bwd(qkv, t_tot, do, *, name):
    s = qkv.shape[0]
    nq = s // CHUNK
    per_step = _q_blocks_per_step(nq)

    def body(q_ref, k_ref, v_ref, t_ref, do_ref, dq_ref, dk_ref, dv_ref):
        @pl.when(pl.program_id(1) == 0)
        def _():
            dk_ref[...] = jnp.zeros_like(dk_ref)
            dv_ref[...] = jnp.zeros_like(dv_ref)

        masks = _head_masks(LANES)
        row = lax.broadcasted_iota(jnp.int32, (CHUNK, CHUNK), 0)
        col = lax.broadcasted_iota(jnp.int32, (CHUNK, CHUNK), 1)
        upto_op = _cumsum_operand(row <= col)
        before_op = _cumsum_operand(row < col)
        cmr = col - row
        zc = jnp.zeros((CHUNK, LANES), F32)

        def q_block(sub, _):
            qi = pl.program_id(1) * per_step + sub
            q_rows = pl.ds(pl.multiple_of(sub * CHUNK, CHUNK), CHUNK)
            q = q_ref[q_rows, :] * SB_SCALE
            dob = do_ref[q_rows, :].astype(BF16)
            zero = jnp.zeros_like(q)
            qs = [jnp.where(m, q, zero) for m in masks]
            dos = [jnp.where(m, dob, zero) for m in masks]
            tots = [t_ref[q_rows, 0:LANES], t_ref[q_rows, LANES:2 * LANES]]

            def blocks(js, carry):
                dq, cl0, cl1, cp0, cp1 = carry
                starts = [pl.multiple_of(jnp.minimum(j, nq - 1) * CHUNK, CHUNK) for j in js]
                valids = [cmr < (qi - j) * CHUNK for j in js]
                kvs = [(k_ref[pl.ds(k0, CHUNK), :], v_ref[pl.ds(k0, CHUNK), :]) for k0 in starts]
                units = [(h, b) for b in range(len(js)) for h in range(2)]
                zs = [_dot(qs[h], kvs[b][0], _NT) for h, b in units]
                das = [_dot(dos[h], kvs[b][1], _NT) for h, b in units]
                logits = [_sb_logits(z, valids[b]) for z, (h, b) in zip(zs, units)]
                sums = [_dot(_split_hi_lo(l1), upto_op, _NN) for l1, _ in logits]
                cls, cps = [cl0, cl1], [cp0, cp1]
                probs, gs = [], []
                for (h, b), (_, lb), sm, da in zip(units, logits, sums, das):
                    a = jnp.where(valids[b], jnp.exp(lb + (tots[h] - cls[h] - sm[:, :CHUNK])), 0.0)
                    probs.append(a)
                    gs.append(a * da)
                    cls[h] = cls[h] + sm[:, CHUNK:]
                sums_g = [_dot(_split_hi_lo(g), before_op, _NN) for g in gs]
                dzs = []
                for (h, b), (_, lb), g, sg in zip(units, logits, gs, sums_g):
                    dz = g - (g + sg[:, :CHUNK] + cps[h]) * jnp.exp(lb)
                    dzs.append(jnp.where(valids[b], dz, 0.0).astype(BF16))
                    cps[h] = cps[h] + sg[:, CHUNK:]
                for (h, b), dzb in zip(units, dzs):
                    dq = dq + _dot(dzb, jnp.where(masks[h], kvs[b][0], zero), _NN)
                for b, k0 in enumerate(starts):
                    dk_ref[pl.ds(k0, CHUNK), :] += _dot(dzs[2 * b], qs[0], _TN) + _dot(dzs[2 * b + 1], qs[1], _TN)
                    dv_ref[pl.ds(k0, CHUNK), :] += (_dot(probs[2 * b].astype(BF16), dos[0], _TN)
                                                    + _dot(probs[2 * b + 1].astype(BF16), dos[1], _TN))
                return dq, cls[0], cls[1], cps[0], cps[1]

            n_four = (qi + 1) // 4
            carry = lax.fori_loop(0, n_four, lambda jj, c: blocks([4 * jj + i for i in range(4)], c), (zc,) * 5)
            base = 4 * n_four
            carry = lax.fori_loop(0, (qi - base + 2) // 2, lambda jj, c: blocks([base + 2 * jj, base + 2 * jj + 1], c), carry)
            dq_ref[q_rows, :] = (carry[0] * SB_SCALE).astype(BF16)
            return 0

        lax.fori_loop(0, per_step, q_block, 0)

    rows = per_step * CHUNK
    blk = pl.BlockSpec((rows, LANES), lambda p, i: (i, p))
    col_blk = pl.BlockSpec((s, LANES), lambda p, i: (0, p))
    out = jax.ShapeDtypeStruct((s, D_SB), F32)
    return pl.pallas_call(
        body,
        name=name,
        grid=(N_PAIRS, nq // per_step),
        in_specs=[pl.BlockSpec((rows, LANES), lambda p, i: (i, QKV_BLOCK0 + p)),
                  pl.BlockSpec((s, LANES), lambda p, i: (0, QKV_BLOCK0 + N_PAIRS + p)),
                  pl.BlockSpec((s, LANES), lambda p, i: (0, QKV_BLOCK0 + 2 * N_PAIRS + p)),
                  pl.BlockSpec((rows, 2 * LANES), lambda p, i: (i, p)),
                  blk],
        out_specs=[blk, col_blk, col_blk],
        out_shape=[jax.ShapeDtypeStruct((s, D_SB), BF16), out, out],
        compiler_params=_cparams(("parallel", "arbitrary")),
    )(qkv, qkv, qkv, t_tot, do)


FFN_TILE = 256
FFN_COLS = 256
N_FF_BLOCKS = D_FF // FFN_COLS


def _ffn_act_fwd(up0, conv_w, conv_b, *, name):
    s = up0.shape[0]
    nt = s // FFN_TILE

    def body(xg_ref, xv_ref, wg_ref, wv_ref, bg_ref, bv_ref, act_ref, pg_ref, pv_ref):
        pg_ref[0:FFN_HALO, :] = jnp.zeros((FFN_HALO, FFN_COLS), F32)
        pv_ref[0:FFN_HALO, :] = jnp.zeros((FFN_HALO, FFN_COLS), F32)
        pg_ref[FFN_HALO:, :] = xg_ref[...].astype(F32)
        pv_ref[FFN_HALO:, :] = xv_ref[...].astype(F32)

        def tile(i, c):
            t0 = pl.multiple_of(i * FFN_TILE, FFN_TILE)
            outs = []
            for p_ref, w_ref, b_ref in ((pg_ref, wg_ref, bg_ref), (pv_ref, wv_ref, bv_ref)):
                window = p_ref[pl.ds(t0, FFN_TILE + FFN_HALO), :]
                acc = b_ref[...] + w_ref[2:3, :] * window[FFN_HALO:, :]
                for j in range(1, FFN_K):
                    acc = acc + w_ref[FFN_K - 1 - j:FFN_K - j, :] * _shift_down(window, j, FFN_HALO)
                outs.append(acc)
            gate, val = outs
            act_ref[pl.ds(t0, FFN_TILE), :] = (gate * _sigmoid(gate) * val).astype(BF16)
            return c

        lax.fori_loop(0, nt, tile, 0)

    gcol = lambda rows: pl.BlockSpec((rows, FFN_COLS), lambda j: (0, j))
    vcol = lambda rows: pl.BlockSpec((rows, FFN_COLS), lambda j: (0, j + N_FF_BLOCKS))
    return pl.pallas_call(
        body,
        name=name,
        grid=(N_FF_BLOCKS,),
        in_specs=[gcol(s), vcol(s), gcol(FFN_K), vcol(FFN_K), gcol(1), vcol(1)],
        out_specs=gcol(s),
        out_shape=jax.ShapeDtypeStruct((s, D_FF), BF16),
        scratch_shapes=[pltpu.VMEM((s + FFN_HALO, FFN_COLS), F32), pltpu.VMEM((s + FFN_HALO, FFN_COLS), F32)],
        compiler_params=_cparams(("parallel",)),
    )(up0, up0, conv_w, conv_w, conv_b, conv_b)


def _ffn_act_bwd(up0, dact, conv_w, conv_b, *, name):
    s = up0.shape[0]
    nt = s // FFN_TILE

    def body(xg_ref, xv_ref, da_ref, wg_ref, wv_ref, bg_ref, bv_ref, dxg_ref, dxv_ref, dwg_ref, dwv_ref, dbg_ref, dbv_ref,
             pg_ref, pv_ref, dg_ref, dv_ref):
        zeros = jnp.zeros((FFN_HALO, FFN_COLS), F32)
        for p_ref, x_ref in ((pg_ref, xg_ref), (pv_ref, xv_ref)):
            p_ref[0:FFN_HALO, :] = zeros
            p_ref[FFN_HALO:, :] = x_ref[...].astype(F32)
        dg_ref[s:s + FFN_HALO, :] = zeros
        dv_ref[s:s + FFN_HALO, :] = zeros
        for ref in (dwg_ref, dwv_ref, dbg_ref, dbv_ref):
            ref[...] = jnp.zeros_like(ref)

        def conv(p_ref, w_ref, b_ref, t0):
            window = p_ref[pl.ds(t0, FFN_TILE + FFN_HALO), :]
            taps = [_shift_down(window, j, FFN_HALO) for j in range(FFN_K)]
            out = b_ref[...]
            for j in range(FFN_K):
                out = out + w_ref[FFN_K - 1 - j:FFN_K - j, :] * taps[j]
            return out, taps

        def tile(i, c):
            t0 = pl.multiple_of(i * FFN_TILE, FFN_TILE)
            gate, taps_g = conv(pg_ref, wg_ref, bg_ref, t0)
            val, taps_v = conv(pv_ref, wv_ref, bv_ref, t0)
            da = da_ref[pl.ds(t0, FFN_TILE), :].astype(F32)
            sg = lax.logistic(gate)
            dgate = da * val * (sg * (1.0 + gate * (1.0 - sg)))
            dval = da * gate * sg
            dg_ref[pl.ds(t0, FFN_TILE), :] = dgate
            dv_ref[pl.ds(t0, FFN_TILE), :] = dval
            dbg_ref[...] += jnp.sum(dgate, axis=0, keepdims=True)
            dbv_ref[...] += jnp.sum(dval, axis=0, keepdims=True)
            for j in range(FFN_K):
                dwg_ref[FFN_K - 1 - j:FFN_K - j, :] += jnp.sum(dgate * taps_g[j], axis=0, keepdims=True)
                dwv_ref[FFN_K - 1 - j:FFN_K - j, :] += jnp.sum(dval * taps_v[j], axis=0, keepdims=True)
            return c

        lax.fori_loop(0, nt, tile, 0)

        def tile_dx(i, c):
            t0 = pl.multiple_of(i * FFN_TILE, FFN_TILE)
            for d_ref, w_ref, dx_ref in ((dg_ref, wg_ref, dxg_ref), (dv_ref, wv_ref, dxv_ref)):
                window = d_ref[pl.ds(t0, FFN_TILE + FFN_HALO), :]
                dx = w_ref[FFN_K - 1:FFN_K, :] * window[:FFN_TILE, :]
                for j in range(1, FFN_K):
                    dx = dx + w_ref[FFN_K - 1 - j:FFN_K - j, :] * _shift_up(window, j, FFN_TILE)
                dx_ref[pl.ds(t0, FFN_TILE), :] = dx.astype(BF16)
            return c

        lax.fori_loop(0, nt, tile_dx, 0)

    gcol = lambda rows: pl.BlockSpec((rows, FFN_COLS), lambda j: (0, j))
    vcol = lambda rows: pl.BlockSpec((rows, FFN_COLS), lambda j: (0, j + N_FF_BLOCKS))
    half = lambda rows, dtype: jax.ShapeDtypeStruct((rows, D_FF), dtype)
    padded = pltpu.VMEM((s + FFN_HALO, FFN_COLS), F32)
    return pl.pallas_call(
        body,
        name=name,
        grid=(N_FF_BLOCKS,),
        in_specs=[gcol(s), vcol(s), gcol(s), gcol(FFN_K), vcol(FFN_K), gcol(1), vcol(1)],
        out_specs=[gcol(s), gcol(s), gcol(FFN_K), gcol(FFN_K), gcol(1), gcol(1)],
        out_shape=[half(s, BF16), half(s, BF16), half(FFN_K, F32), half(FFN_K, F32), half(1, F32), half(1, F32)],
        scratch_shapes=[padded, padded, padded, padded],
        compiler_params=_cparams(("parallel",)),
    )(up0, up0, dact, conv_w, conv_w, conv_b, conv_b)


MESH = pl.DeviceIdType.MESH


def _position():
    x, y, c = lax.axis_index("x"), lax.axis_index("y"), lax.axis_index("c")
    return x, y, c, 4 * x + 2 * y + c


def _peer(k):
    x, y, c, _ = _position()
    px = 1 - x if k & 4 else x
    py = 1 - y if k & 2 else y
    pc = 1 - c if k & 1 else c
    return (px, py, pc), 4 * px + 2 * py + pc


_HBM = pl.BlockSpec(memory_space=pltpu.HBM)
_SEM = pl.BlockSpec(memory_space=pltpu.SEMAPHORE)
_DATAFLOW = pltpu.SideEffectType.DATAFLOW_SIDE_EFFECTING
N_PEERS = N_DEV - 1


class _SplitExchange:
    def __init__(self, src, *, kind, name):
        self.kind, self.name, self.dtype = kind, name, src.dtype
        scatter = kind.startswith("scatter")
        by_blocks = kind.endswith("blocks")
        self.scatter, self.by_blocks = scatter, by_blocks
        if by_blocks:
            self.r, self.cols, self.land_shape = None, None, src.shape if scatter else (N_DEV,) + src.shape
        else:
            self.r = src.shape[0] // N_DEV if scatter else src.shape[0]
            self.cols = src.shape[1]
            self.land_shape = (N_DEV, self.r, self.cols) if scatter else (N_DEV * self.r, self.cols)
        r = self.r

        def copies(src_ref, land_ref, send_sems, recv_sems, local_sem):
            me = _position()[3]

            def rows(ref, idx):
                return ref.at[pl.ds(pl.multiple_of(idx * r, r), r), :]

            if not scatter:
                outgoing = lambda idx: src_ref
            else:
                outgoing = (lambda idx: src_ref.at[idx]) if by_blocks else (lambda idx: rows(src_ref, idx))
            slot = (lambda idx: land_ref.at[idx]) if (scatter or by_blocks) else (lambda idx: rows(land_ref, idx))
            sends, recvs = [], []
            for k in range(1, N_DEV):
                peer, pidx = _peer(k)
                sems = dict(send_sem=send_sems[k - 1], recv_sem=recv_sems[k - 1], device_id=peer, device_id_type=MESH)
                sends.append(pltpu.make_async_remote_copy(src_ref=outgoing(pidx), dst_ref=slot(me), **sems))
                recvs.append(pltpu.make_async_remote_copy(src_ref=outgoing(pidx), dst_ref=slot(pidx), **sems))
            return sends, recvs, pltpu.make_async_copy(outgoing(me), slot(me), local_sem)

        self._copies = copies
        self.src = src

    @staticmethod
    def start(exchanges, name):
        n = len(exchanges)
        per = 2 * N_PEERS + 1

        def start_body(*refs):
            outs = refs[2 * n:]
            for i, ex in enumerate(exchanges):
                sems = outs[per * i:per * (i + 1)]
                sends, _, local = ex._copies(refs[2 * i], refs[2 * i + 1], sems[:N_PEERS], sems[N_PEERS:-1], sems[-1])
                for cp in sends + [local]:
                    cp.start()
            outs[-1][...] = jnp.zeros_like(outs[-1])

        sem = pltpu.SemaphoreType.DMA(())
        operands, thru_shapes = [], []
        for ex in exchanges:
            operands += [pltpu.with_memory_space_constraint(ex.src, pltpu.HBM),
                         pltpu.with_memory_space_constraint(lax.empty(ex.land_shape, ex.dtype), pltpu.HBM)]
            thru_shapes += [pltpu.HBM(ex.src.shape, ex.dtype), pltpu.HBM(ex.land_shape, ex.dtype)]
        out = pl.pallas_call(
            start_body,
            name=name,
            in_specs=(_HBM,) * (2 * n),
            out_specs=(_SEM,) * (per * n) + (_HBM,) * (2 * n) + (pl.BlockSpec(memory_space=pltpu.VMEM),),
            out_shape=(sem,) * (per * n) + tuple(thru_shapes) + (jax.ShapeDtypeStruct((8, LANES), F32),),
            input_output_aliases={i: per * n + i for i in range(2 * n)},
            compiler_params=pltpu.CompilerParams(has_side_effects=_DATAFLOW),
        )(*operands)
        for i, ex in enumerate(exchanges):
            ex.sems = out[per * i:per * (i + 1)]
            ex.src_thru, ex.land_thru = out[per * n + 2 * i], out[per * n + 2 * i + 1]
        return out[-1][0, 0]

    def finish(self, after):
        copies = self._copies

        def wait_body(src_ref, land_ref, *rest):
            sends, recvs, local = copies(src_ref, land_ref, rest[:N_PEERS], rest[N_PEERS:2 * N_PEERS], rest[2 * N_PEERS])
            for cp in sends:
                cp.wait_send()
            for cp in recvs:
                cp.wait_recv()
            local.wait()

        return pl.pallas_call(
            wait_body,
            name=f"{self.name}_wait",
            in_specs=(_HBM, _HBM) + (_SEM,) * (2 * N_PEERS + 1) + (pl.BlockSpec(memory_space=pl.ANY),),
            out_specs=(_HBM, _HBM),
            out_shape=(pltpu.HBM(self.src_thru.shape, self.dtype), pltpu.HBM(self.land_shape, self.dtype)),
            input_output_aliases={0: 0, 1: 1},
            compiler_params=pltpu.CompilerParams(has_side_effects=_DATAFLOW),
        )(self.src_thru, self.land_thru, *self.sems, after)[1]


def _row_tile(rows):
    return _tile(rows, (256, 128, 64, 32, 16, 8))


def _layer_parts_specs(n_layers, n_parts, tr, cols):
    return [pl.BlockSpec((n_parts, tr, cols), lambda l, i, j=j: (0, jnp.where(l == j, i, 0), 0)) for j in range(n_layers)]


def _select_layer_sum(p_refs):
    l = pl.program_id(0)
    g = None
    for j, p_ref in enumerate(p_refs):
        gj = p_ref[0].astype(F32)
        for k in range(1, p_ref.shape[0]):
            gj = gj + p_ref[k].astype(F32)
        g = gj if g is None else jnp.where(l == j, gj, g)
    return g


def _adamw(parts, w, m, v, *, name):
    n_layers, rows, cols = w.shape
    tr = _row_tile(rows)

    def body(*refs):
        w_ref, m_ref, v_ref, g_ref, d_ref, m2_ref, v2_ref = refs[n_layers:]
        g = _select_layer_sum(refs[:n_layers])
        m2 = ADAM_B1 * m_ref[...] + (1.0 - ADAM_B1) * g
        v2 = ADAM_B2 * v_ref[...] + (1.0 - ADAM_B2) * (g * g)
        m_hat = m2 / (1.0 - ADAM_B1 ** ADAM_STEP)
        v_hat = v2 / (1.0 - ADAM_B2 ** ADAM_STEP)
        g_ref[...] = g
        d_ref[...] = -ADAM_LR * (m_hat / (jnp.sqrt(v_hat) + ADAM_EPS) + ADAM_WD * w_ref[...])
        m2_ref[...] = m2
        v2_ref[...] = v2

    slab = pl.BlockSpec((None, tr, cols), lambda l, i: (l, i, 0))
    out = jax.ShapeDtypeStruct((n_layers, rows, cols), F32)
    p_specs = _layer_parts_specs(n_layers, parts[0].shape[0], tr, cols)
    return pl.pallas_call(
        body,
        name=name,
        grid=(n_layers, rows // tr),
        in_specs=p_specs + [slab, slab, slab],
        out_specs=[slab, slab, slab, slab],
        out_shape=[out, out, out, out],
        compiler_params=_cparams(("arbitrary", "arbitrary")),
    )(*parts, w, m, v)


SLAB_ROWS = 32
_SMALL_SHARDED = (("conv_w", (2, 31, 32)), ("ffn_conv_w", (2, 3, 704)))
_REPLICATED = (("g_mix", (2, 1024)), ("conv_b", (2, 256)), ("conv_ln_g", (2, 256)), ("conv_ln_b", (2, 256)),
               ("sgu_ln_g", (2, 256)), ("sgu_ln_b", (2, 256)), ("sgu_w", (2, 4, 128, 128)), ("sgu_b", (2, 4, 128)),
               ("g_out", (2, 1024)), ("g_ffn", (2, 1024)), ("ffn_conv_b", (2, 5632)), ("g_final", (1024,)))


def _seg_rows(n_elems):
    return -(-n_elems // LANES)


def _pack(arrays, lead=()):
    segs = []
    for a in arrays:
        flat = a.reshape(lead + (-1,)).astype(F32)
        pad = _seg_rows(flat.shape[-1]) * LANES - flat.shape[-1]
        if pad:
            flat = jnp.pad(flat, [(0, 0)] * len(lead) + [(0, pad)])
        segs.append(flat)
    flat = jnp.concatenate(segs, axis=-1)
    rows = flat.shape[-1] // LANES
    pad_rows = -rows % SLAB_ROWS
    if pad_rows:
        flat = jnp.pad(flat, [(0, 0)] * len(lead) + [(0, pad_rows * LANES)])
    return flat.reshape(lead + (rows + pad_rows, LANES))


def _unpack(slab, shapes, lead=()):
    flat = slab.reshape(lead + (-1,))
    out, off = [], 0
    for shape in shapes:
        n = math.prod(shape)
        out.append(flat[..., off:off + n].reshape(lead + tuple(shape)))
        off += _seg_rows(n) * LANES
    return out


def _split_last(full):
    split = full.shape[:-1] + (N_DEV, full.shape[-1] // N_DEV)
    return jnp.moveaxis(full.reshape(split), -2, 0)


def _join_last(blocks):
    moved = jnp.moveaxis(blocks, 0, -2)
    return moved.reshape(moved.shape[:-2] + (moved.shape[-2] * moved.shape[-1],))


def _gathered(wt, n, l, after):
    if isinstance(wt[n][l], _SplitExchange):
        wt[n][l] = wt[n][l].finish(after)
    return wt[n][l]


def _layer_fwd(l, x, wt, small):
    tag = f"l{l}"
    h = _rmsnorm_fwd(x, small["g_mix"][l][None], name=f"{tag}_norm_mix")
    w_in_t = _gathered(wt, "w_in_t", l, h)
    p = _matmul(h, w_in_t, "nt", name=f"{tag}_proj", out_dtype=BF16)
    p_ab = qkv = p
    ya = _mixer_a_fwd(p_ab, wt["conv_w"][l], small["conv_b"][l][None], small["conv_ln_g"][l][None],
                      small["conv_ln_b"][l][None], name=f"{tag}_mixer_a")
    bias = jnp.repeat(small["sgu_b"][l].T, HEAD_DIM, axis=1)
    yb = _sgu_fwd(p_ab, small["sgu_ln_g"][l][None], small["sgu_ln_b"][l][None], small["sgu_w"][l], bias,
                  name=f"{tag}_sgu")
    yc, t_tot = _attn_fwd(qkv, name=f"{tag}_attn")
    y = _combine_fwd(ya, yb, yc, small["g_out"][l][None], name=f"{tag}_combine")
    x1 = _matmul(y, _gathered(wt, "w_out", l, y), "nn", name=f"{tag}_out_proj", residual=x)
    h2 = _rmsnorm_fwd(x1, small["g_ffn"][l][None], name=f"{tag}_norm_ffn")
    up0 = _matmul(h2, _gathered(wt, "w_up_t", l, h2), "nt", name=f"{tag}_up", out_dtype=BF16)
    act = _ffn_act_fwd(up0, wt["ffn_conv_w"][l], small["ffn_conv_b"][l][None], name=f"{tag}_ffn_act")
    x2 = _matmul(act, _gathered(wt, "w_down", l, act), "nn", name=f"{tag}_down", residual=x1)
    saved = dict(x=x, h=h, p_ab=p, qkv=p, ya=ya, yb=yb, yc=yc, t_tot=t_tot, y=y, x1=x1, h2=h2, up0=up0,
                 act=act, bias=bias)
    return x2, saved


def _layer_bwd(l, dres, sv, wt, small, scattering, token):
    tag = f"l{l}b"
    g = {}

    def scatter(n, partial):
        scattering[n][l] = _SplitExchange(partial, kind="scatter_rows", name=f"scatter_{n}_l{l}")
        return _SplitExchange.start([scattering[n][l]], name=f"scatter_{n}_l{l}_start")

    dx2, dx2_b = dres
    dact = _matmul(dx2_b, wt["w_down"][l], "nt", name=f"{tag}_dact", out_dtype=BF16)
    tok = scatter("w_down", _matmul(sv["act"], dx2_b, "tn", name=f"{tag}_dw_down", out_dtype=BF16))
    dup_g, dup_v, dwg, dwv, dbg, dbv = _ffn_act_bwd(sv["up0"], dact, wt["ffn_conv_w"][l], small["ffn_conv_b"][l][None] + tok + token,
                                                    name=f"{tag}_ffn_act")
    g["ffn_conv_w"] = jnp.concatenate([dwg, dwv], axis=1)
    g["ffn_conv_b"] = jnp.concatenate([dbg[0], dbv[0]])
    dh2 = _matmul(dup_g, wt["w_up_t"][l], "nn", name=f"{tag}_dh2_gate")
    dh2 = _matmul(dup_v, wt["w_up_t"][l], "nn", name=f"{tag}_dh2_val", b_k0=D_FF, residual=dh2)
    dw_up = _matmul(dup_g, sv["h2"], "tn", name=f"{tag}_dw_up_gate", out_dtype=BF16, rows=(2 * D_FF, 0))
    dw_up = _matmul(dup_v, sv["h2"], "tn", name=f"{tag}_dw_up_val", out_dtype=BF16, rows=(2 * D_FF, D_FF), into=dw_up)
    tok = scatter("w_up_t", dw_up)
    dx1, dx1_b, dg = _rmsnorm_bwd(sv["x1"], small["g_ffn"][l][None] + tok, dh2, dx2, name=f"{tag}_norm_ffn")
    g["g_ffn"] = dg[0]
    dy = _matmul(dx1_b, wt["w_out"][l], "nt", name=f"{tag}_dy")
    tok = scatter("w_out", _matmul(sv["y"], dx1_b, "tn", name=f"{tag}_dw_out", out_dtype=BF16))
    dya, dyb, dyc, dg = _combine_bwd(dy, sv["ya"], sv["yb"], sv["yc"], small["g_out"][l][None] + tok,
                                     name=f"{tag}_combine")
    g["g_out"] = dg[0]
    dq, dk, dv = _attn_bwd(sv["qkv"], sv["t_tot"], dyc, name=f"{tag}_attn")
    dp_b, g["sgu_w"], db, dg, dbeta = _sgu_bwd(sv["p_ab"], dyb, small["sgu_ln_g"][l][None], small["sgu_ln_b"][l][None],
                                               small["sgu_w"][l], sv["bias"], name=f"{tag}_sgu")
    g["sgu_b"] = db[:, :N_SGU_HEADS].T
    g["sgu_ln_g"], g["sgu_ln_b"] = dg[0], dbeta[0]
    dp_a, g["conv_w"], dcb, dg, dbeta = _mixer_a_bwd(sv["p_ab"], dya, wt["conv_w"][l], small["conv_b"][l][None],
                                                     small["conv_ln_g"][l][None], small["conv_ln_b"][l][None],
                                                     name=f"{tag}_mixer_a")
    g["conv_b"], g["conv_ln_g"], g["conv_ln_b"] = dcb[0], dg[0], dbeta[0]
    dp = jnp.concatenate([dp_a, dp_b, dq, dk.astype(BF16), dv.astype(BF16)], axis=1)
    tok = scatter("w_in_t", _matmul(dp, sv["h"], "tn", name=f"{tag}_dw_in", out_dtype=BF16))
    dh = _matmul(dp, wt["w_in_t"][l], "nn", name=f"{tag}_dh")
    dx, dx_b, dg = _rmsnorm_bwd(sv["x"], small["g_mix"][l][None] + tok, dh, dx1, name=f"{tag}_norm_mix")
    g["g_mix"] = dg[0]
    return (dx, dx_b), g


_BIG = ("w_in_t", "w_out", "w_up_t", "w_down")


def kernel(x, g_mix, w_in, conv_w, conv_b, conv_ln_g, conv_ln_b, sgu_ln_g, sgu_ln_b, sgu_w, sgu_b, g_out, w_out, g_ffn, w_up, ffn_conv_w, ffn_conv_b, w_down, g_final, loss_target, m_g_mix, m_w_in, m_conv_w, m_conv_b, m_conv_ln_g, m_conv_ln_b, m_sgu_ln_g, m_sgu_ln_b, m_sgu_w, m_sgu_b, m_g_out, m_w_out, m_g_ffn, m_w_up, m_ffn_conv_w, m_ffn_conv_b, m_w_down, m_g_final, v_g_mix, v_w_in, v_conv_w, v_conv_b, v_conv_ln_g, v_conv_ln_b, v_sgu_ln_g, v_sgu_ln_b, v_sgu_w, v_sgu_b, v_g_out, v_w_out, v_g_ffn, v_w_up, v_ffn_conv_w, v_ffn_conv_b, v_w_down, v_g_final):
    given = dict(locals())
    n_layers = g_mix.shape[0]
    layers = range(n_layers)
    small_sharded = [n for n, _ in _SMALL_SHARDED]
    replicated = [n for n, _ in _REPLICATED]
    small = {n: given[n] for n in replicated}

    filters = _SplitExchange(_pack([given[n] for n in small_sharded]), kind="gather_blocks", name="gather_filters")
    wt = {n: [None] * n_layers for n in _BIG}
    wt["w_in_t"][0] = _SplitExchange(w_in[0].T.astype(BF16), kind="gather_rows", name="gather_w_in_t_l0")
    tok = _SplitExchange.start([filters, wt["w_in_t"][0]], name="gather_first_start")
    w_in, w_out, w_up, w_down, tok = lax.optimization_barrier((w_in, w_out, w_up, w_down, tok))
    shard = {"w_in_t": [w_in[l].T.astype(BF16) for l in layers], "w_out": [w_out[l].astype(BF16) for l in layers],
             "w_up_t": [w_up[l].T.astype(BF16) for l in layers], "w_down": [w_down[l].astype(BF16) for l in layers]}
    later = [(n, l) for l in layers for n in _BIG if (n, l) != ("w_in_t", 0)]
    for n, l in later:
        wt[n][l] = _SplitExchange(shard[n][l], kind="gather_rows", name=f"gather_{n}_l{l}")
    small["g_mix"] = g_mix + tok + _SplitExchange.start([wt[n][l] for n, l in later], name="gather_weights_start")
    gathered_filters = filters.finish(small["g_mix"])
    for n, blocks in zip(small_sharded, _unpack(gathered_filters, [s for _, s in _SMALL_SHARDED], lead=(N_DEV,))):
        wt[n] = _join_last(blocks)

    xs = x[0]
    saved = []
    for l in layers:
        xs, sv = _layer_fwd(l, xs, wt, small)
        saved.append(sv)
    loss_tile, dx, dx_b, dgf = _loss_head(xs, g_final[None], loss_target[0], name="loss_head")
    dres = (dx, dx_b)
    scattering = {n: [None] * n_layers for n in _BIG}
    layered = [n for n in replicated if n not in ("g_final", "sgu_w")]
    slabs, sgu_w_parts = [None] * n_layers, [None] * n_layers
    tok = 0.0
    for l in reversed(layers):
        dres, g = _layer_bwd(l, dres, saved[l], wt, small, scattering, tok)
        own = _pack([_split_last(g[n]) for n in small_sharded], lead=(N_DEV,))
        shared = _pack([g[n] for n in layered] + ([dgf[0]] if l == n_layers - 1 else []))
        slab = jnp.concatenate([own, jnp.broadcast_to(shared[None], (N_DEV,) + shared.shape)], axis=1)
        slabs[l] = _SplitExchange(slab, kind="scatter_blocks", name=f"scatter_small_grads_l{l}")
        sgu_w_parts[l] = _SplitExchange(g["sgu_w"].reshape(-1, LANES).astype(BF16), kind="gather_blocks",
                                        name=f"gather_sgu_w_grads_l{l}")
        tok = _SplitExchange.start([slabs[l], sgu_w_parts[l]], name=f"small_grads_l{l}_start")
    n_own = own.shape[1]

    after_backward = jnp.full((8, LANES), tok)
    received = {n: [scattering[n][l].finish(after_backward) for l in layers] for n in _BIG}
    out = {}

    def update(n, parts, transposed=False):
        turn = (lambda a: jnp.swapaxes(a, 1, 2)) if transposed else (lambda a: a)
        results = _adamw(parts, turn(given[n]), turn(given["m_" + n]), turn(given["v_" + n]), name=f"adamw_{n}")
        for pre, res in zip(("grad_", "delta_", "new_m_", "new_v_"), results):
            out[pre + n] = turn(res)
        return results[0][0, :8, :LANES]

    update("w_out", received["w_out"])
    update("w_down", received["w_down"])
    update("w_in", received["w_in_t"], transposed=True)
    big_updated = update("w_up", received["w_up_t"], transposed=True)

    as_rows = lambda a: a.reshape(n_layers, -1, LANES)
    results = _adamw([ex.finish(big_updated) for ex in sgu_w_parts], as_rows(sgu_w), as_rows(m_sgu_w), as_rows(v_sgu_w),
                     name="adamw_sgu_w")
    for pre, res in zip(("grad_", "delta_", "new_m_", "new_v_"), results):
        out[pre + "sgu_w"] = res.reshape(sgu_w.shape)

    per_layer = {pre + n: [None] * n_layers for pre in ("grad_", "delta_", "new_m_", "new_v_") for n in small_sharded + layered}
    for l in reversed(layers):
        last = l == n_layers - 1
        stacks = [jnp.concatenate([_pack([given[pre + n][l] for n in small_sharded]),
                                   _pack([given[pre + n][l] for n in layered] + ([given[pre + "g_final"]] if last else []))])[None]
                  for pre in ("", "m_", "v_")]
        results = _adamw([slabs[l].finish(big_updated)], *stacks, name=f"adamw_small_l{l}")
        for pre, res in zip(("grad_", "delta_", "new_m_", "new_v_"), results):
            unpacked = (_unpack(res[0, :n_own], [s[1:] for _, s in _SMALL_SHARDED])
                        + _unpack(res[0, n_own:], [s[1:] for n, s in _REPLICATED if n in layered] + ([g_final.shape] if last else [])))
            for n, a in zip(small_sharded + layered + (["g_final"] if last else []), unpacked):
                if n == "g_final":
                    out[pre + n] = a
                else:
                    per_layer[pre + n][l] = a
    for name, parts in per_layer.items():
        out[name] = jnp.stack(parts)

    loss = lax.psum(loss_tile[0, 0], ("x", "y", "c"))
    order = list(_WEIGHT_ORDER)
    return (loss, dres[0][None], *[out["grad_" + n] for n in order], *[out["delta_" + n] for n in order],
            *[out["new_m_" + n] for n in order], *[out["new_v_" + n] for n in order])


_WEIGHT_ORDER = ("g_mix", "w_in", "conv_w", "conv_b", "conv_ln_g", "conv_ln_b", "sgu_ln_g", "sgu_ln_b", "sgu_w", "sgu_b",
                 "g_out", "w_out", "g_ffn", "w_up", "ffn_conv_w", "ffn_conv_b", "w_down", "g_final")
```

```python
import math

import jax
import jax.numpy as jnp
from jax import lax
from jax.experimental import pallas as pl
from jax.experimental.pallas import tpu as pltpu

F32 = jnp.float32
BF16 = jnp.bfloat16

N_DEV = 8
D_MODEL = 1024
HEAD_DIM = 64
D_CONV = 256
D_SGU = 256
D_SB = 512
D_AB = 2 * D_CONV + 2 * D_SGU
D_QKV = 3 * D_SB
D_IN = D_AB + D_QKV
CONV_K = 31
CONV_HALO = 32
FFN_K = 3
FFN_HALO = 8
D_FF = 2816
CHUNK = 128
EPS = 1e-6
LANES = 128

ADAM_LR = 0.001
ADAM_B1 = 0.9
ADAM_B2 = 0.999
ADAM_EPS = 1e-08
ADAM_WD = 0.01
ADAM_STEP = 10

VMEM_LIMIT = 56 * 1024 * 1024


def _cparams(sem=None):
    return pltpu.CompilerParams(dimension_semantics=sem, vmem_limit_bytes=VMEM_LIMIT)


def _tile(n, prefs=(512, 256, 128)):
    for t in prefs:
        if n % t == 0:
            return t
    return n


def _sigmoid(x):
    return 1.0 / (1.0 + jnp.exp(-x))


_INV_SQRT2 = 1.0 / math.sqrt(2.0)
_INV_SQRT2PI = 1.0 / math.sqrt(2.0 * math.pi)


def _gelu(x):
    return 0.5 * x * (1.0 + lax.erf(x * _INV_SQRT2))


def _gelu_grad(x):
    return 0.5 * (1.0 + lax.erf(x * _INV_SQRT2)) + x * jnp.exp(-0.5 * x * x) * _INV_SQRT2PI


def _dot(a, b, dims):
    return lax.dot_general(a, b, (dims, ((), ())), preferred_element_type=F32)


_NN = ((1,), (0,))
_NT = ((1,), (1,))
_TN = ((0,), (0,))


def _split_bf16(x):
    hi = x.astype(BF16)
    lo = (x - hi.astype(F32)).astype(BF16)
    return jnp.concatenate([hi, lo], axis=1)


def _matmul(a, b, mode, *, name, out_dtype=F32, residual=None, n=None, b_n0=0, b_k0=0, rows=None, into=None):
    if mode == "nn":
        (m, k), n = a.shape, (n or b.shape[1])
    elif mode == "nt":
        (m, k), n = a.shape, (n or b.shape[0])
    else:
        (k, m), n = a.shape, b.shape[1]
    has_res = residual is not None
    tm, tn = _matmul_tiles(m, n, k, a.dtype.itemsize, b.dtype.itemsize, jnp.dtype(out_dtype).itemsize, has_res, b_n0)
    j0 = b_n0 // tn
    total_rows, first_row = rows or (m, 0)
    assert b_k0 % k == 0 and first_row % tm == 0
    kb, i0 = b_k0 // k, first_row // tm

    if mode == "nn":
        a_spec = pl.BlockSpec((tm, k), lambda i, j: (i, 0))
        b_spec = pl.BlockSpec((k, tn), lambda i, j: (kb, j + j0))
        dims = _NN
    elif mode == "nt":
        a_spec = pl.BlockSpec((tm, k), lambda i, j: (i, 0))
        b_spec = pl.BlockSpec((tn, k), lambda i, j: (j + j0, 0))
        dims = _NT
    else:
        a_spec = pl.BlockSpec((k, tm), lambda i, j: (0, i))
        b_spec = pl.BlockSpec((k, tn), lambda i, j: (0, j))
        dims = _TN
    o_spec = pl.BlockSpec((tm, tn), lambda i, j: (i + i0, j))
    r_spec = pl.BlockSpec((tm, tn), lambda i, j: (i, j))

    def body(*refs):
        a_ref, b_ref = refs[:2]
        acc = _dot(a_ref[...].astype(BF16), b_ref[...].astype(BF16), dims)
        if has_res:
            acc = acc + refs[2][...]
        refs[-1][...] = acc.astype(out_dtype)

    in_specs = [a_spec, b_spec] + ([r_spec] if has_res else [])
    args = (a, b) + ((residual,) if has_res else ())
    aliases = {}
    if into is not None:
        aliases = {len(args): 0}
        in_specs.append(pl.BlockSpec(memory_space=pl.ANY))
        args += (into,)

        def body(*refs, inner=body):
            inner(*refs[:len(args) - 1], refs[-1])

    return pl.pallas_call(
        body,
        name=name,
        grid=(m // tm, n // tn),
        in_specs=in_specs,
        out_specs=o_spec,
        out_shape=jax.ShapeDtypeStruct((total_rows, n), out_dtype),
        input_output_aliases=aliases,
        compiler_params=_cparams(("parallel", "parallel")),
    )(*args)


MATMUL_VMEM_BUDGET = 40 * 1024 * 1024


def _matmul_tiles(m, n, k, a_bytes, b_bytes, out_bytes, has_res, n_offset):
    def divisors(size, cap, also=0):
        return [t for t in range(cap, 0, -LANES) if size % t == 0 and also % t == 0] or [size]

    for tm in divisors(m, 1024):
        for tn in divisors(n, 1408, n_offset):
            blocks = tm * k * a_bytes + k * tn * b_bytes + tm * tn * (out_bytes + (4 if has_res else 0))
            if 2 * blocks <= MATMUL_VMEM_BUDGET:
                return tm, tn
    raise ValueError(f"no matmul tiling for {m} x {n} x {k}")


ROW_TILE = 512


def _rmsnorm_fwd(x, g, *, name):
    s, d = x.shape

    def body(x_ref, g_ref, h_ref):
        xv = x_ref[...]
        r = lax.rsqrt(jnp.mean(xv * xv, axis=-1, keepdims=True) + EPS)
        h_ref[...] = (xv * r * g_ref[...]).astype(BF16)

    return pl.pallas_call(
        body,
        name=name,
        grid=(s // ROW_TILE,),
        in_specs=[pl.BlockSpec((ROW_TILE, d), lambda i: (i, 0)), pl.BlockSpec((1, d), lambda i: (0, 0))],
        out_specs=pl.BlockSpec((ROW_TILE, d), lambda i: (i, 0)),
        out_shape=jax.ShapeDtypeStruct((s, d), BF16),
        compiler_params=_cparams(("parallel",)),
    )(x, g)


def _rmsnorm_bwd(x, g, dh, dres, *, name):
    s, d = x.shape

    def body(x_ref, g_ref, dh_ref, dres_ref, dx_ref, dxb_ref, dg_ref):
        xv = x_ref[...]
        r = lax.rsqrt(jnp.mean(xv * xv, axis=-1, keepdims=True) + EPS)
        xhat = xv * r
        dhv = dh_ref[...]
        dxhat = dhv * g_ref[...]
        dx = dres_ref[...] + r * (dxhat - xhat * jnp.mean(dxhat * xhat, axis=-1, keepdims=True))
        dx_ref[...] = dx
        dxb_ref[...] = dx.astype(BF16)
        part = jnp.sum(dhv * xhat, axis=0, keepdims=True)

        @pl.when(pl.program_id(0) == 0)
        def _():
            dg_ref[...] = part

        @pl.when(pl.program_id(0) > 0)
        def _():
            dg_ref[...] += part

    row = pl.BlockSpec((ROW_TILE, d), lambda i: (i, 0))
    vec = pl.BlockSpec((1, d), lambda i: (0, 0))
    return pl.pallas_call(
        body,
        name=name,
        grid=(s // ROW_TILE,),
        in_specs=[row, vec, row, row],
        out_specs=[row, row, vec],
        out_shape=[jax.ShapeDtypeStruct((s, d), F32), jax.ShapeDtypeStruct((s, d), BF16),
                   jax.ShapeDtypeStruct((1, d), F32)],
        compiler_params=_cparams(("arbitrary",)),
    )(x, g, dh, dres)


def _loss_head(x, g, target, *, name):
    s, d = x.shape

    def body(x_ref, g_ref, t_ref, loss_ref, dx_ref, dxb_ref, dg_ref):
        xv = x_ref[...]
        gv = g_ref[...]
        r = lax.rsqrt(jnp.mean(xv * xv, axis=-1, keepdims=True) + EPS)
        xhat = xv * r
        diff = xhat * gv - t_ref[...]
        dy = diff * (1.0 / d)
        dxhat = dy * gv
        dx = r * (dxhat - xhat * jnp.mean(dxhat * xhat, axis=-1, keepdims=True))
        dx_ref[...] = dx
        dxb_ref[...] = dx.astype(BF16)
        dg_part = jnp.sum(dy * xhat, axis=0, keepdims=True)
        row_loss = jnp.sum(diff * diff, axis=-1, keepdims=True)
        loss_part = jnp.sum(row_loss, axis=0, keepdims=True) * (0.5 / d)

        @pl.when(pl.program_id(0) == 0)
        def _():
            dg_ref[...] = dg_part
            loss_ref[...] = jnp.broadcast_to(loss_part, loss_ref.shape)

        @pl.when(pl.program_id(0) > 0)
        def _():
            dg_ref[...] += dg_part
            loss_ref[...] += jnp.broadcast_to(loss_part, loss_ref.shape)

    row = pl.BlockSpec((ROW_TILE, d), lambda i: (i, 0))
    vec = pl.BlockSpec((1, d), lambda i: (0, 0))
    tile = pl.BlockSpec((8, LANES), lambda i: (0, 0))
    return pl.pallas_call(
        body,
        name=name,
        grid=(s // ROW_TILE,),
        in_specs=[row, vec, row],
        out_specs=[tile, row, row, vec],
        out_shape=[jax.ShapeDtypeStruct((8, LANES), F32), jax.ShapeDtypeStruct((s, d), F32),
                   jax.ShapeDtypeStruct((s, d), BF16), jax.ShapeDtypeStruct((1, d), F32)],
        compiler_params=_cparams(("arbitrary",)),
    )(x, g, target)


_BRANCHES = ((0, D_CONV), (D_CONV, D_SGU), (D_CONV + D_SGU, D_SB))


def _combine_fwd(ya, yb, yc, g, *, name):
    s = ya.shape[0]

    def body(ya_ref, yb_ref, yc_ref, g_ref, y_ref):
        for ref, (off, w) in zip((ya_ref, yb_ref, yc_ref), _BRANCHES):
            v = ref[...]
            r = lax.rsqrt(jnp.mean(v * v, axis=-1, keepdims=True) + EPS)
            y_ref[:, off:off + w] = (v * r * g_ref[:, off:off + w]).astype(BF16)

    def row(w):
        return pl.BlockSpec((ROW_TILE, w), lambda i: (i, 0))

    return pl.pallas_call(
        body,
        name=name,
        grid=(s // ROW_TILE,),
        in_specs=[row(D_CONV), row(D_SGU), row(D_SB), pl.BlockSpec((1, D_MODEL), lambda i: (0, 0))],
        out_specs=row(D_MODEL),
        out_shape=jax.ShapeDtypeStruct((s, D_MODEL), BF16),
        compiler_params=_cparams(("parallel",)),
    )(ya, yb, yc, g)


def _combine_bwd(dy, ya, yb, yc, g, *, name):
    s = ya.shape[0]

    def body(dy_ref, ya_ref, yb_ref, yc_ref, g_ref, dya_ref, dyb_ref, dyc_ref, dg_ref):
        first = pl.program_id(0) == 0
        for ref, dref, (off, w) in zip((ya_ref, yb_ref, yc_ref), (dya_ref, dyb_ref, dyc_ref), _BRANCHES):
            v = ref[...]
            r = lax.rsqrt(jnp.mean(v * v, axis=-1, keepdims=True) + EPS)
            n = v * r
            dout = dy_ref[:, off:off + w]
            dn = dout * g_ref[:, off:off + w]
            dref[...] = r * (dn - n * jnp.mean(dn * n, axis=-1, keepdims=True))
            part = jnp.sum(dout * n, axis=0, keepdims=True)

            @pl.when(first)
            def _():
                dg_ref[:, off:off + w] = part

            @pl.when(jnp.logical_not(first))
            def _():
                dg_ref[:, off:off + w] += part

    def row(w):
        return pl.BlockSpec((ROW_TILE, w), lambda i: (i, 0))

    vec = pl.BlockSpec((1, D_MODEL), lambda i: (0, 0))
    return pl.pallas_call(
        body,
        name=name,
        grid=(s // ROW_TILE,),
        in_specs=[row(D_MODEL), row(D_CONV), row(D_SGU), row(D_SB), vec],
        out_specs=[row(D_CONV), row(D_SGU), row(D_SB), vec],
        out_shape=[jax.ShapeDtypeStruct((s, D_CONV), F32), jax.ShapeDtypeStruct((s, D_SGU), F32),
                   jax.ShapeDtypeStruct((s, D_SB), F32), jax.ShapeDtypeStruct((1, D_MODEL), F32)],
        compiler_params=_cparams(("arbitrary",)),
    )(dy, ya, yb, yc, g)


CONV_TILE = 128


def _shift_down(window, j, halo):
    return pltpu.roll(window, j, 0)[halo:, :] if j else window[halo:, :]


def _shift_up(window, j, n_out):
    n = window.shape[0]
    return pltpu.roll(window, n - j, 0)[:n_out, :] if j else window[:n_out, :]


def _mixer_a_fwd(p_ab, conv_w, conv_b, ln_g, ln_b, *, name):
    s = p_ab.shape[0]
    nt = s // CONV_TILE

    def body(p_ref, w_ref, b_ref, g_ref, beta_ref, y_ref, h_ref):
        h_ref[0:CONV_HALO, :] = jnp.zeros((CONV_HALO, D_CONV), F32)

        def glu(i, c):
            t0 = pl.multiple_of(i * CONV_TILE, CONV_TILE)
            a = p_ref[pl.ds(t0, CONV_TILE), 0:D_CONV].astype(F32)
            gate = p_ref[pl.ds(t0, CONV_TILE), D_CONV:2 * D_CONV].astype(F32)
            h_ref[pl.ds(t0 + CONV_HALO, CONV_TILE), :] = a * _sigmoid(gate)
            return c

        lax.fori_loop(0, nt, glu, 0)

        def conv(i, c):
            t0 = pl.multiple_of(i * CONV_TILE, CONV_TILE)
            window = h_ref[pl.ds(t0, CONV_TILE + CONV_HALO), :]
            acc = jnp.zeros((CONV_TILE, D_CONV), F32) + b_ref[...]
            for k in range(CONV_K):
                acc = acc + w_ref[k:k + 1, :] * _shift_down(window, CONV_K - 1 - k, CONV_HALO)
            mu = jnp.mean(acc, axis=-1, keepdims=True)
            xc = acc - mu
            rstd = lax.rsqrt(jnp.mean(xc * xc, axis=-1, keepdims=True) + EPS)
            z = xc * rstd * g_ref[...] + beta_ref[...]
            y_ref[pl.ds(t0, CONV_TILE), :] = z * _sigmoid(z)
            return c

        lax.fori_loop(0, nt, conv, 0)

    full = lambda shape: pl.BlockSpec(shape, lambda i: (0, 0))
    return pl.pallas_call(
        body,
        name=name,
        grid=(1,),
        in_specs=[full((s, 2 * D_CONV)), full((CONV_K, D_CONV)), full((1, D_CONV)), full((1, D_CONV)),
                  full((1, D_CONV))],
        out_specs=full((s, D_CONV)),
        out_shape=jax.ShapeDtypeStruct((s, D_CONV), F32),
        scratch_shapes=[pltpu.VMEM((s + CONV_HALO, D_CONV), F32)],
        compiler_params=_cparams(("arbitrary",)),
    )(p_ab, conv_w, conv_b, ln_g, ln_b)


def _mixer_a_bwd(p_ab, dya, conv_w, conv_b, ln_g, ln_b, *, name):
    s = p_ab.shape[0]
    nt = s // CONV_TILE

    def body(p_ref, dy_ref, w_ref, b_ref, g_ref, beta_ref, dp_ref, dw_ref, db_ref, dg_ref, dbeta_ref, h_ref, dc_ref):
        h_ref[0:CONV_HALO, :] = jnp.zeros((CONV_HALO, D_CONV), F32)
        dc_ref[s:s + CONV_HALO, :] = jnp.zeros((CONV_HALO, D_CONV), F32)
        dw_ref[...] = jnp.zeros_like(dw_ref)
        db_ref[...] = jnp.zeros_like(db_ref)
        dg_ref[...] = jnp.zeros_like(dg_ref)
        dbeta_ref[...] = jnp.zeros_like(dbeta_ref)

        def glu(i, c):
            t0 = pl.multiple_of(i * CONV_TILE, CONV_TILE)
            a = p_ref[pl.ds(t0, CONV_TILE), 0:D_CONV].astype(F32)
            gate = p_ref[pl.ds(t0, CONV_TILE), D_CONV:2 * D_CONV].astype(F32)
            h_ref[pl.ds(t0 + CONV_HALO, CONV_TILE), :] = a * _sigmoid(gate)
            return c

        lax.fori_loop(0, nt, glu, 0)

        def conv_bwd(i, c):
            t0 = pl.multiple_of(i * CONV_TILE, CONV_TILE)
            window = h_ref[pl.ds(t0, CONV_TILE + CONV_HALO), :]
            taps = [_shift_down(window, CONV_K - 1 - k, CONV_HALO) for k in range(CONV_K)]
            acc = jnp.zeros((CONV_TILE, D_CONV), F32) + b_ref[...]
            for k in range(CONV_K):
                acc = acc + w_ref[k:k + 1, :] * taps[k]
            mu = jnp.mean(acc, axis=-1, keepdims=True)
            xc = acc - mu
            rstd = lax.rsqrt(jnp.mean(xc * xc, axis=-1, keepdims=True) + EPS)
            xhat = xc * rstd
            z = xhat * g_ref[...] + beta_ref[...]
            sg = _sigmoid(z)
            dz = dy_ref[pl.ds(t0, CONV_TILE), :] * (sg * (1.0 + z * (1.0 - sg)))
            dg_ref[...] += jnp.sum(dz * xhat, axis=0, keepdims=True)
            dbeta_ref[...] += jnp.sum(dz, axis=0, keepdims=True)
            dxhat = dz * g_ref[...]
            dc = rstd * (dxhat - jnp.mean(dxhat, axis=-1, keepdims=True)
                         - xhat * jnp.mean(dxhat * xhat, axis=-1, keepdims=True))
            dc_ref[pl.ds(t0, CONV_TILE), :] = dc
            db_ref[...] += jnp.sum(dc, axis=0, keepdims=True)
            for k in range(CONV_K):
                dw_ref[k:k + 1, :] += jnp.sum(dc * taps[k], axis=0, keepdims=True)
            return c

        lax.fori_loop(0, nt, conv_bwd, 0)

        def glu_bwd(i, c):
            t0 = pl.multiple_of(i * CONV_TILE, CONV_TILE)
            window = dc_ref[pl.ds(t0, CONV_TILE + CONV_HALO), :]
            dh = jnp.zeros((CONV_TILE, D_CONV), F32)
            for j in range(CONV_K):
                dh = dh + w_ref[CONV_K - 1 - j:CONV_K - j, :] * _shift_up(window, j, CONV_TILE)
            a = p_ref[pl.ds(t0, CONV_TILE), 0:D_CONV].astype(F32)
            sg = _sigmoid(p_ref[pl.ds(t0, CONV_TILE), D_CONV:2 * D_CONV].astype(F32))
            dp_ref[pl.ds(t0, CONV_TILE), 0:D_CONV] = (dh * sg).astype(BF16)
            dp_ref[pl.ds(t0, CONV_TILE), D_CONV:2 * D_CONV] = (dh * a * sg * (1.0 - sg)).astype(BF16)
            return c

        lax.fori_loop(0, nt, glu_bwd, 0)

    full = lambda shape: pl.BlockSpec(shape, lambda i: (0, 0))
    vec = jax.ShapeDtypeStruct((1, D_CONV), F32)
    return pl.pallas_call(
        body,
        name=name,
        grid=(1,),
        in_specs=[full((s, 2 * D_CONV)), full((s, D_CONV)), full((CONV_K, D_CONV)), full((1, D_CONV)),
                  full((1, D_CONV)), full((1, D_CONV))],
        out_specs=[full((s, 2 * D_CONV)), full((CONV_K, D_CONV)), full((1, D_CONV)), full((1, D_CONV)),
                   full((1, D_CONV))],
        out_shape=[jax.ShapeDtypeStruct((s, 2 * D_CONV), BF16), jax.ShapeDtypeStruct((CONV_K, D_CONV), F32),
                   vec, vec, vec],
        scratch_shapes=[pltpu.VMEM((s + CONV_HALO, D_CONV), F32), pltpu.VMEM((s + CONV_HALO, D_CONV), F32)],
        compiler_params=_cparams(("arbitrary",)),
    )(p_ab, dya, conv_w, conv_b, ln_g, ln_b)


N_SGU_HEADS = D_SGU // HEAD_DIM


def _head_masks(width):
    lane = lax.broadcasted_iota(jnp.int32, (1, width), 1)
    return [(lane >= h * HEAD_DIM) & (lane < (h + 1) * HEAD_DIM) for h in range(width // HEAD_DIM)]


def _tril_mask():
    r = lax.broadcasted_iota(jnp.int32, (CHUNK, CHUNK), 0)
    c = lax.broadcasted_iota(jnp.int32, (CHUNK, CHUNK), 1)
    return c <= r


def _sgu_norm(bv, g, beta):
    vg = _gelu(bv)
    mu = jnp.mean(vg, axis=-1, keepdims=True)
    xc = vg - mu
    rstd = lax.rsqrt(jnp.mean(xc * xc, axis=-1, keepdims=True) + EPS)
    xhat = xc * rstd
    return xhat, rstd, xhat * g + beta


def _sgu_fwd(p_ab, ln_g, ln_b, w_s, bias, *, name):
    s = p_ab.shape[0]

    def body(p_ref, g_ref, beta_ref, w_ref, bias_ref, y_ref):
        u = _gelu(p_ref[:, 0:D_SGU].astype(F32))
        _, _, vn = _sgu_norm(p_ref[:, D_SGU:2 * D_SGU].astype(F32), g_ref[...], beta_ref[...])
        vb = vn.astype(BF16)
        tril = _tril_mask()
        mixed = bias_ref[...]
        for h, m in enumerate(_head_masks(D_SGU)):
            wh = jnp.where(tril, w_ref[h], 0.0).astype(BF16)
            mixed = mixed + _dot(wh, jnp.where(m, vb, jnp.zeros_like(vb)), _NN)
        y_ref[...] = u * mixed

    return pl.pallas_call(
        body,
        name=name,
        grid=(s // CHUNK,),
        in_specs=[pl.BlockSpec((CHUNK, 2 * D_SGU), lambda i: (i, 1)),
                  pl.BlockSpec((1, D_SGU), lambda i: (0, 0)), pl.BlockSpec((1, D_SGU), lambda i: (0, 0)),
                  pl.BlockSpec((N_SGU_HEADS, CHUNK, CHUNK), lambda i: (0, 0, 0)),
                  pl.BlockSpec((CHUNK, D_SGU), lambda i: (0, 0))],
        out_specs=pl.BlockSpec((CHUNK, D_SGU), lambda i: (i, 0)),
        out_shape=jax.ShapeDtypeStruct((s, D_SGU), F32),
        compiler_params=_cparams(("parallel",)),
    )(p_ab, ln_g, ln_b, w_s, bias)


def _sgu_bwd(p_ab, dyb, ln_g, ln_b, w_s, bias, *, name):
    s = p_ab.shape[0]
    n_chunks = s // CHUNK

    def body(p_ref, dy_ref, g_ref, beta_ref, w_ref, bias_ref, dp_ref, dw_ref, db_ref, dg_ref, dbeta_ref, dbias_ref):
        @pl.when(pl.program_id(0) == 0)
        def _():
            dw_ref[...] = jnp.zeros_like(dw_ref)
            dbias_ref[...] = jnp.zeros_like(dbias_ref)
            dg_ref[...] = jnp.zeros_like(dg_ref)
            dbeta_ref[...] = jnp.zeros_like(dbeta_ref)

        bu = p_ref[:, 0:D_SGU].astype(F32)
        bv = p_ref[:, D_SGU:2 * D_SGU].astype(F32)
        u = _gelu(bu)
        gv = g_ref[...]
        xhat, rstd, vn = _sgu_norm(bv, gv, beta_ref[...])
        vb = vn.astype(BF16)
        tril = _tril_mask()
        masks = _head_masks(D_SGU)
        whs = [jnp.where(tril, w_ref[h], 0.0).astype(BF16) for h in range(N_SGU_HEADS)]
        mixed = bias_ref[...]
        for h, m in enumerate(masks):
            mixed = mixed + _dot(whs[h], jnp.where(m, vb, jnp.zeros_like(vb)), _NN)
        dy = dy_ref[...]
        dp_ref[:, 0:D_SGU] = (dy * mixed * _gelu_grad(bu)).astype(BF16)
        dmixed = dy * u
        dbias_ref[...] += dmixed
        dmb = dmixed.astype(BF16)
        dvn = jnp.zeros((CHUNK, D_SGU), F32)
        for h, m in enumerate(masks):
            dmh = jnp.where(m, dmb, jnp.zeros_like(dmb))
            dvn = dvn + _dot(whs[h], dmh, _TN)
            dw_ref[h] += jnp.where(tril, _dot(dmh, vb, _NT), 0.0)
        dg_ref[...] += jnp.sum(dvn * xhat, axis=0, keepdims=True)
        dbeta_ref[...] += jnp.sum(dvn, axis=0, keepdims=True)
        dxhat = dvn * gv
        dvg = rstd * (dxhat - jnp.mean(dxhat, axis=-1, keepdims=True)
                      - xhat * jnp.mean(dxhat * xhat, axis=-1, keepdims=True))
        dp_ref[:, D_SGU:2 * D_SGU] = (dvg * _gelu_grad(bv)).astype(BF16)

        @pl.when(pl.program_id(0) == n_chunks - 1)
        def _():
            chan = lax.broadcasted_iota(jnp.int32, (D_SGU, LANES), 0)
            head = lax.broadcasted_iota(jnp.int32, (D_SGU, LANES), 1)
            to_head = jnp.where(chan // HEAD_DIM == head, 1.0, 0.0).astype(BF16)
            db_ref[...] = _dot(_split_bf16(dbias_ref[...]), jnp.concatenate([to_head, to_head], axis=0), _NN)

    vec = pl.BlockSpec((1, D_SGU), lambda i: (0, 0))
    wspec = pl.BlockSpec((N_SGU_HEADS, CHUNK, CHUNK), lambda i: (0, 0, 0))
    bspec = pl.BlockSpec((CHUNK, D_SGU), lambda i: (0, 0))
    return pl.pallas_call(
        body,
        name=name,
        grid=(n_chunks,),
        in_specs=[pl.BlockSpec((CHUNK, 2 * D_SGU), lambda i: (i, 1)), pl.BlockSpec((CHUNK, D_SGU), lambda i: (i, 0)),
                  vec, vec, wspec, bspec],
        out_specs=[pl.BlockSpec((CHUNK, 2 * D_SGU), lambda i: (i, 0)), wspec,
                   pl.BlockSpec((CHUNK, LANES), lambda i: (0, 0)), vec, vec],
        out_shape=[jax.ShapeDtypeStruct((s, 2 * D_SGU), BF16),
                   jax.ShapeDtypeStruct((N_SGU_HEADS, CHUNK, CHUNK), F32),
                   jax.ShapeDtypeStruct((CHUNK, LANES), F32),
                   jax.ShapeDtypeStruct((1, D_SGU), F32), jax.ShapeDtypeStruct((1, D_SGU), F32)],
        scratch_shapes=[pltpu.VMEM((CHUNK, D_SGU), F32)],
        compiler_params=_cparams(("arbitrary",)),
    )(p_ab, dyb, ln_g, ln_b, w_s, bias)


N_PAIRS = D_SB // LANES
QKV_BLOCK0 = D_AB // LANES
SB_SCALE = HEAD_DIM ** -0.5


def _sb_logits(z, valid):
    nz = -z
    t = jnp.log(1.0 + jnp.exp(jnp.minimum(z, nz)))
    l1 = jnp.minimum(nz, 0.0) - t
    if valid is not None:
        l1 = jnp.where(valid, l1, 0.0)
    return l1, jnp.minimum(z, 0.0) - t


def _split_hi_lo(x):
    hi = lax.bitcast_convert_type(lax.bitcast_convert_type(x, jnp.uint32) & jnp.uint32(0xFFFF0000), F32)
    return jnp.concatenate([hi, x - hi], axis=1)


def _cumsum_operand(keep):
    half = jnp.concatenate([keep.astype(F32), jnp.ones((CHUNK, CHUNK), F32)], axis=1)
    return jnp.concatenate([half, half], axis=0)


Q_BLOCKS_PER_STEP = 4


def _q_blocks_per_step(nq):
    return next(n for n in (Q_BLOCKS_PER_STEP, 2, 1) if nq % n == 0)


def _attn_fwd(qkv, *, name):
    s = qkv.shape[0]
    nq = s // CHUNK
    per_step = _q_blocks_per_step(nq)

    def body(q_ref, k_ref, v_ref, o_ref, t_ref):
        masks = _head_masks(LANES)
        row = lax.broadcasted_iota(jnp.int32, (CHUNK, CHUNK), 0)
        col = lax.broadcasted_iota(jnp.int32, (CHUNK, CHUNK), 1)
        after_op = _cumsum_operand(row > col)
        cmr = col - row
        zc = jnp.zeros((CHUNK, LANES), F32)

        def q_block(sub, _):
            qi = pl.program_id(1) * per_step + sub
            q_rows = pl.ds(pl.multiple_of(sub * CHUNK, CHUNK), CHUNK)
            q = q_ref[q_rows, :] * SB_SCALE
            zero = jnp.zeros_like(q)
            qs = [jnp.where(m, q, zero) for m in masks]

            def blocks(js, carry):
                o, c0, c1 = carry
                kvs, valids = [], []
                for j in js:
                    k0 = pl.multiple_of(jnp.maximum(j, 0) * CHUNK, CHUNK)
                    kvs.append((k_ref[pl.ds(k0, CHUNK), :], v_ref[pl.ds(k0, CHUNK), :]))
                    valids.append(cmr < jnp.where(j >= 0, (qi - j) * CHUNK, -CHUNK))
                units = [(h, b) for b in range(len(js)) for h in range(2)]
                zs = [_dot(qs[h], kvs[b][0], _NT) for h, b in units]
                logits = [_sb_logits(z, valids[b]) for z, (h, b) in zip(zs, units)]
                sums = [_dot(_split_hi_lo(l1), after_op, _NN) for l1, _ in logits]
                cs = [c0, c1]
                probs = []
                for (h, b), (_, lb), sm in zip(units, logits, sums):
                    probs.append(jnp.where(valids[b], jnp.exp(lb + sm[:, :CHUNK] + cs[h]), 0.0))
                    cs[h] = cs[h] + sm[:, CHUNK:]
                for (h, b), a in zip(units, probs):
                    o = o + _dot(a.astype(BF16), jnp.where(masks[h], kvs[b][1], zero), _NN)
                return o, cs[0], cs[1]

            n_four = (qi + 1) // 4
            carry = lax.fori_loop(0, n_four, lambda jj, c: blocks([qi - 4 * jj - i for i in range(4)], c), (zc,) * 3)
            top = qi - 4 * n_four
            o, c0, c1 = lax.fori_loop(0, (top + 2) // 2, lambda jj, c: blocks([top - 2 * jj, top - 2 * jj - 1], c), carry)
            o_ref[q_rows, :] = o
            t_ref[q_rows, 0:LANES] = c0
            t_ref[q_rows, LANES:2 * LANES] = c1
            return 0

        lax.fori_loop(0, per_step, q_block, 0)

    rows = per_step * CHUNK
    return pl.pallas_call(
        body,
        name=name,
        grid=(N_PAIRS, nq // per_step),
        in_specs=[pl.BlockSpec((rows, LANES), lambda p, i: (i, QKV_BLOCK0 + p)),
                  pl.BlockSpec((s, LANES), lambda p, i: (0, QKV_BLOCK0 + N_PAIRS + p)),
                  pl.BlockSpec((s, LANES), lambda p, i: (0, QKV_BLOCK0 + 2 * N_PAIRS + p))],
        out_specs=[pl.BlockSpec((rows, LANES), lambda p, i: (i, p)),
                   pl.BlockSpec((rows, 2 * LANES), lambda p, i: (i, p))],
        out_shape=[jax.ShapeDtypeStruct((s, D_SB), F32), jax.ShapeDtypeStruct((s, 2 * D_SB), F32)],
        compiler_params=_cparams(("parallel", "parallel")),
    )(qkv, qkv, qkv)


def _attn_bwd(qkv, t_tot, do, *, name):
    s = qkv.shape[0]
    nq = s // CHUNK
    per_step = _q_blocks_per_step(nq)

    def body(q_ref, k_ref, v_ref, t_ref, do_ref, dq_ref, dk_ref, dv_ref):
        @pl.when(pl.program_id(1) == 0)
        def _():
            dk_ref[...] = jnp.zeros_like(dk_ref)
            dv_ref[...] = jnp.zeros_like(dv_ref)

        masks = _head_masks(LANES)
        row = lax.broadcasted_iota(jnp.int32, (CHUNK, CHUNK), 0)
        col = lax.broadcasted_iota(jnp.int32, (CHUNK, CHUNK), 1)
        upto_op = _cumsum_operand(row <= col)
        before_op = _cumsum_operand(row < col)
        cmr = col - row
        zc = jnp.zeros((CHUNK, LANES), F32)

        def q_block(sub, _):
            qi = pl.program_id(1) * per_step + sub
            q_rows = pl.ds(pl.multiple_of(sub * CHUNK, CHUNK), CHUNK)
            q = q_ref[q_rows, :] * SB_SCALE
            dob = do_ref[q_rows, :].astype(BF16)
            zero = jnp.zeros_like(q)
            qs = [jnp.where(m, q, zero) for m in masks]
            dos = [jnp.where(m, dob, zero) for m in masks]
            tots = [t_ref[q_rows, 0:LANES], t_ref[q_rows, LANES:2 * LANES]]

            def blocks(js, carry):
                dq, cl0, cl1, cp0, cp1 = carry
                starts = [pl.multiple_of(jnp.minimum(j, nq - 1) * CHUNK, CHUNK) for j in js]
                valids = [cmr < (qi - j) * CHUNK for j in js]
                kvs = [(k_ref[pl.ds(k0, CHUNK), :], v_ref[pl.ds(k0, CHUNK), :]) for k0 in starts]
                units = [(h, b) for b in range(len(js)) for h in range(2)]
                zs = [_dot(qs[h], kvs[b][0], _NT) for h, b in units]
                das = [_dot(dos[h], kvs[b][1], _NT) for h, b in units]
                logits = [_sb_logits(z, valids[b]) for z, (h, b) in zip(zs, units)]
                sums = [_dot(_split_hi_lo(l1), upto_op, _NN) for l1, _ in logits]
                cls, cps = [cl0, cl1], [cp0, cp1]
                probs, gs = [], []
                for (h, b), (_, lb), sm, da in zip(units, logits, sums, das):
                    a = jnp.where(valids[b], jnp.exp(lb + (tots[h] - cls[h] - sm[:, :CHUNK])), 0.0)
                    probs.append(a)
                    gs.append(a * da)
                    cls[h] = cls[h] + sm[:, CHUNK:]
                sums_g = [_dot(_split_hi_lo(g), before_op, _NN) for g in gs]
                dzs = []
                for (h, b), (_, lb), g, sg in zip(units, logits, gs, sums_g):
                    dz = g - (g + sg[:, :CHUNK] + cps[h]) * jnp.exp(lb)
                    dzs.append(jnp.where(valids[b], dz, 0.0).astype(BF16))
                    cps[h] = cps[h] + sg[:, CHUNK:]
                for (h, b), dzb in zip(units, dzs):
                    dq = dq + _dot(dzb, jnp.where(masks[h], kvs[b][0], zero), _NN)
                for b, k0 in enumerate(starts):
                    dk_ref[pl.ds(k0, CHUNK), :] += _dot(dzs[2 * b], qs[0], _TN) + _dot(dzs[2 * b + 1], qs[1], _TN)
                    dv_ref[pl.ds(k0, CHUNK), :] += (_dot(probs[2 * b].astype(BF16), dos[0], _TN)
                                                    + _dot(probs[2 * b + 1].astype(BF16), dos[1], _TN))
                return dq, cls[0], cls[1], cps[0], cps[1]

            n_four = (qi + 1) // 4
            carry = lax.fori_loop(0, n_four, lambda jj, c: blocks([4 * jj + i for i in range(4)], c), (zc,) * 5)
            base = 4 * n_four
            carry = lax.fori_loop(0, (qi - base + 2) // 2, lambda jj, c: blocks([base + 2 * jj, base + 2 * jj + 1], c), carry)
            dq_ref[q_rows, :] = (carry[0] * SB_SCALE).astype(BF16)
            return 0

        lax.fori_loop(0, per_step, q_block, 0)

    rows = per_step * CHUNK
    blk = pl.BlockSpec((rows, LANES), lambda p, i: (i, p))
    col_blk = pl.BlockSpec((s, LANES), lambda p, i: (0, p))
    out = jax.ShapeDtypeStruct((s, D_SB), F32)
    return pl.pallas_call(
        body,
        name=name,
        grid=(N_PAIRS, nq // per_step),
        in_specs=[pl.BlockSpec((rows, LANES), lambda p, i: (i, QKV_BLOCK0 + p)),
                  pl.BlockSpec((s, LANES), lambda p, i: (0, QKV_BLOCK0 + N_PAIRS + p)),
                  pl.BlockSpec((s, LANES), lambda p, i: (0, QKV_BLOCK0 + 2 * N_PAIRS + p)),
                  pl.BlockSpec((rows, 2 * LANES), lambda p, i: (i, p)),
                  blk],
        out_specs=[blk, col_blk, col_blk],
        out_shape=[jax.ShapeDtypeStruct((s, D_SB), BF16), out, out],
        compiler_params=_cparams(("parallel", "arbitrary")),
    )(qkv, qkv, qkv, t_tot, do)


FFN_TILE = 256
FFN_COLS = 256
N_FF_BLOCKS = D_FF // FFN_COLS


def _ffn_act_fwd(up0, conv_w, conv_b, *, name):
    s = up0.shape[0]
    nt = s // FFN_TILE

    def body(xg_ref, xv_ref, wg_ref, wv_ref, bg_ref, bv_ref, act_ref, pg_ref, pv_ref):
        pg_ref[0:FFN_HALO, :] = jnp.zeros((FFN_HALO, FFN_COLS), F32)
        pv_ref[0:FFN_HALO, :] = jnp.zeros((FFN_HALO, FFN_COLS), F32)
        pg_ref[FFN_HALO:, :] = xg_ref[...].astype(F32)
        pv_ref[FFN_HALO:, :] = xv_ref[...].astype(F32)

        def tile(i, c):
            t0 = pl.multiple_of(i * FFN_TILE, FFN_TILE)
            outs = []
            for p_ref, w_ref, b_ref in ((pg_ref, wg_ref, bg_ref), (pv_ref, wv_ref, bv_ref)):
                window = p_ref[pl.ds(t0, FFN_TILE + FFN_HALO), :]
                acc = b_ref[...] + w_ref[2:3, :] * window[FFN_HALO:, :]
                for j in range(1, FFN_K):
                    acc = acc + w_ref[FFN_K - 1 - j:FFN_K - j, :] * _shift_down(window, j, FFN_HALO)
                outs.append(acc)
            gate, val = outs
            act_ref[pl.ds(t0, FFN_TILE), :] = (gate * _sigmoid(gate) * val).astype(BF16)
            return c

        lax.fori_loop(0, nt, tile, 0)

    gcol = lambda rows: pl.BlockSpec((rows, FFN_COLS), lambda j: (0, j))
    vcol = lambda rows: pl.BlockSpec((rows, FFN_COLS), lambda j: (0, j + N_FF_BLOCKS))
    return pl.pallas_call(
        body,
        name=name,
        grid=(N_FF_BLOCKS,),
        in_specs=[gcol(s), vcol(s), gcol(FFN_K), vcol(FFN_K), gcol(1), vcol(1)],
        out_specs=gcol(s),
        out_shape=jax.ShapeDtypeStruct((s, D_FF), BF16),
        scratch_shapes=[pltpu.VMEM((s + FFN_HALO, FFN_COLS), F32), pltpu.VMEM((s + FFN_HALO, FFN_COLS), F32)],
        compiler_params=_cparams(("parallel",)),
    )(up0, up0, conv_w, conv_w, conv_b, conv_b)


def _ffn_act_bwd(up0, dact, conv_w, conv_b, *, name):
    s = up0.shape[0]
    nt = s // FFN_TILE

    def body(xg_ref, xv_ref, da_ref, wg_ref, wv_ref, bg_ref, bv_ref, dxg_ref, dxv_ref, dwg_ref, dwv_ref, dbg_ref, dbv_ref,
             pg_ref, pv_ref, dg_ref, dv_ref):
        zeros = jnp.zeros((FFN_HALO, FFN_COLS), F32)
        for p_ref, x_ref in ((pg_ref, xg_ref), (pv_ref, xv_ref)):
            p_ref[0:FFN_HALO, :] = zeros
            p_ref[FFN_HALO:, :] = x_ref[...].astype(F32)
        dg_ref[s:s + FFN_HALO, :] = zeros
        dv_ref[s:s + FFN_HALO, :] = zeros
        for ref in (dwg_ref, dwv_ref, dbg_ref, dbv_ref):
            ref[...] = jnp.zeros_like(ref)

        def conv(p_ref, w_ref, b_ref, t0):
            window = p_ref[pl.ds(t0, FFN_TILE + FFN_HALO), :]
            taps = [_shift_down(window, j, FFN_HALO) for j in range(FFN_K)]
            out = b_ref[...]
            for j in range(FFN_K):
                out = out + w_ref[FFN_K - 1 - j:FFN_K - j, :] * taps[j]
            return out, taps

        def tile(i, c):
            t0 = pl.multiple_of(i * FFN_TILE, FFN_TILE)
            gate, taps_g = conv(pg_ref, wg_ref, bg_ref, t0)
            val, taps_v = conv(pv_ref, wv_ref, bv_ref, t0)
            da = da_ref[pl.ds(t0, FFN_TILE), :].astype(F32)
            sg = lax.logistic(gate)
            dgate = da * val * (sg * (1.0 + gate * (1.0 - sg)))
            dval = da * gate * sg
            dg_ref[pl.ds(t0, FFN_TILE), :] = dgate
            dv_ref[pl.ds(t0, FFN_TILE), :] = dval
            dbg_ref[...] += jnp.sum(dgate, axis=0, keepdims=True)
            dbv_ref[...] += jnp.sum(dval, axis=0, keepdims=True)
            for j in range(FFN_K):
                dwg_ref[FFN_K - 1 - j:FFN_K - j, :] += jnp.sum(dgate * taps_g[j], axis=0, keepdims=True)
                dwv_ref[FFN_K - 1 - j:FFN_K - j, :] += jnp.sum(dval * taps_v[j], axis=0, keepdims=True)
            return c

        lax.fori_loop(0, nt, tile, 0)

        def tile_dx(i, c):
            t0 = pl.multiple_of(i * FFN_TILE, FFN_TILE)
            for d_ref, w_ref, dx_ref in ((dg_ref, wg_ref, dxg_ref), (dv_ref, wv_ref, dxv_ref)):
                window = d_ref[pl.ds(t0, FFN_TILE + FFN_HALO), :]
                dx = w_ref[FFN_K - 1:FFN_K, :] * window[:FFN_TILE, :]
                for j in range(1, FFN_K):
                    dx = dx + w_ref[FFN_K - 1 - j:FFN_K - j, :] * _shift_up(window, j, FFN_TILE)
                dx_ref[pl.ds(t0, FFN_TILE), :] = dx.astype(BF16)
            return c

        lax.fori_loop(0, nt, tile_dx, 0)

    gcol = lambda rows: pl.BlockSpec((rows, FFN_COLS), lambda j: (0, j))
    vcol = lambda rows: pl.BlockSpec((rows, FFN_COLS), lambda j: (0, j + N_FF_BLOCKS))
    half = lambda rows, dtype: jax.ShapeDtypeStruct((rows, D_FF), dtype)
    padded = pltpu.VMEM((s + FFN_HALO, FFN_COLS), F32)
    return pl.pallas_call(
        body,
        name=name,
        grid=(N_FF_BLOCKS,),
        in_specs=[gcol(s), vcol(s), gcol(s), gcol(FFN_K), vcol(FFN_K), gcol(1), vcol(1)],
        out_specs=[gcol(s), gcol(s), gcol(FFN_K), gcol(FFN_K), gcol(1), gcol(1)],
        out_shape=[half(s, BF16), half(s, BF16), half(FFN_K, F32), half(FFN_K, F32), half(1, F32), half(1, F32)],
        scratch_shapes=[padded, padded, padded, padded],
        compiler_params=_cparams(("parallel",)),
    )(up0, up0, dact, conv_w, conv_w, conv_b, conv_b)


MESH = pl.DeviceIdType.MESH


def _position():
    x, y, c = lax.axis_index("x"), lax.axis_index("y"), lax.axis_index("c")
    return x, y, c, 4 * x + 2 * y + c


def _peer(k):
    x, y, c, _ = _position()
    px = 1 - x if k & 4 else x
    py = 1 - y if k & 2 else y
    pc = 1 - c if k & 1 else c
    return (px, py, pc), 4 * px + 2 * py + pc


_HBM = pl.BlockSpec(memory_space=pltpu.HBM)
_SEM = pl.BlockSpec(memory_space=pltpu.SEMAPHORE)
_DATAFLOW = pltpu.SideEffectType.DATAFLOW_SIDE_EFFECTING
N_PEERS = N_DEV - 1


class _SplitExchange:
    def __init__(self, src, *, kind, name):
        self.kind, self.name, self.dtype = kind, name, src.dtype
        scatter = kind.startswith("scatter")
        by_blocks = kind.endswith("blocks")
        self.scatter, self.by_blocks = scatter, by_blocks
        if by_blocks:
            self.r, self.cols, self.land_shape = None, None, src.shape if scatter else (N_DEV,) + src.shape
        else:
            self.r = src.shape[0] // N_DEV if scatter else src.shape[0]
            self.cols = src.shape[1]
            self.land_shape = (N_DEV, self.r, self.cols) if scatter else (N_DEV * self.r, self.cols)
        r = self.r

        def copies(src_ref, land_ref, send_sems, recv_sems, local_sem):
            me = _position()[3]

            def rows(ref, idx):
                return ref.at[pl.ds(pl.multiple_of(idx * r, r), r), :]

            if not scatter:
                outgoing = lambda idx: src_ref
            else:
                outgoing = (lambda idx: src_ref.at[idx]) if by_blocks else (lambda idx: rows(src_ref, idx))
            slot = (lambda idx: land_ref.at[idx]) if (scatter or by_blocks) else (lambda idx: rows(land_ref, idx))
            sends, recvs = [], []
            for k in range(1, N_DEV):
                peer, pidx = _peer(k)
                sems = dict(send_sem=send_sems[k - 1], recv_sem=recv_sems[k - 1], device_id=peer, device_id_type=MESH)
                sends.append(pltpu.make_async_remote_copy(src_ref=outgoing(pidx), dst_ref=slot(me), **sems))
                recvs.append(pltpu.make_async_remote_copy(src_ref=outgoing(pidx), dst_ref=slot(pidx), **sems))
            return sends, recvs, pltpu.make_async_copy(outgoing(me), slot(me), local_sem)

        self._copies = copies
        self.src = src

    @staticmethod
    def start(exchanges, name):
        n = len(exchanges)
        per = 2 * N_PEERS + 1

        def start_body(*refs):
            outs = refs[2 * n:]
            for i, ex in enumerate(exchanges):
                sems = outs[per * i:per * (i + 1)]
                sends, _, local = ex._copies(refs[2 * i], refs[2 * i + 1], sems[:N_PEERS], sems[N_PEERS:-1], sems[-1])
                for cp in sends + [local]:
                    cp.start()
            outs[-1][...] = jnp.zeros_like(outs[-1])

        sem = pltpu.SemaphoreType.DMA(())
        operands, thru_shapes = [], []
        for ex in exchanges:
            operands += [pltpu.with_memory_space_constraint(ex.src, pltpu.HBM),
                         pltpu.with_memory_space_constraint(lax.empty(ex.land_shape, ex.dtype), pltpu.HBM)]
            thru_shapes += [pltpu.HBM(ex.src.shape, ex.dtype), pltpu.HBM(ex.land_shape, ex.dtype)]
        out = pl.pallas_call(
            start_body,
            name=name,
            in_specs=(_HBM,) * (2 * n),
            out_specs=(_SEM,) * (per * n) + (_HBM,) * (2 * n) + (pl.BlockSpec(memory_space=pltpu.VMEM),),
            out_shape=(sem,) * (per * n) + tuple(thru_shapes) + (jax.ShapeDtypeStruct((8, LANES), F32),),
            input_output_aliases={i: per * n + i for i in range(2 * n)},
            compiler_params=pltpu.CompilerParams(has_side_effects=_DATAFLOW),
        )(*operands)
        for i, ex in enumerate(exchanges):
            ex.sems = out[per * i:per * (i + 1)]
            ex.src_thru, ex.land_thru = out[per * n + 2 * i], out[per * n + 2 * i + 1]
        return out[-1][0, 0]

    def finish(self, after):
        copies = self._copies

        def wait_body(src_ref, land_ref, *rest):
            sends, recvs, local = copies(src_ref, land_ref, rest[:N_PEERS], rest[N_PEERS:2 * N_PEERS], rest[2 * N_PEERS])
            for cp in sends:
                cp.wait_send()
            for cp in recvs:
                cp.wait_recv()
            local.wait()

        return pl.pallas_call(
            wait_body,
            name=f"{self.name}_wait",
            in_specs=(_HBM, _HBM) + (_SEM,) * (2 * N_PEERS + 1) + (pl.BlockSpec(memory_space=pl.ANY),),
            out_specs=(_HBM, _HBM),
            out_shape=(pltpu.HBM(self.src_thru.shape, self.dtype), pltpu.HBM(self.land_shape, self.dtype)),
            input_output_aliases={0: 0, 1: 1},
            compiler_params=pltpu.CompilerParams(has_side_effects=_DATAFLOW),
        )(self.src_thru, self.land_thru, *self.sems, after)[1]


def _row_tile(rows):
    return _tile(rows, (256, 128, 64, 32, 16, 8))


def _layer_parts_specs(n_layers, n_parts, tr, cols):
    return [pl.BlockSpec((n_parts, tr, cols), lambda l, i, j=j: (0, jnp.where(l == j, i, 0), 0)) for j in range(n_layers)]


def _select_layer_sum(p_refs):
    l = pl.program_id(0)
    g = None
    for j, p_ref in enumerate(p_refs):
        gj = p_ref[0].astype(F32)
        for k in range(1, p_ref.shape[0]):
            gj = gj + p_ref[k].astype(F32)
        g = gj if g is None else jnp.where(l == j, gj, g)
    return g


def _adamw(parts, w, m, v, *, name):
    n_layers, rows, cols = w.shape
    tr = _row_tile(rows)

    def body(*refs):
        w_ref, m_ref, v_ref, g_ref, d_ref, m2_ref, v2_ref = refs[n_layers:]
        g = _select_layer_sum(refs[:n_layers])
        m2 = ADAM_B1 * m_ref[...] + (1.0 - ADAM_B1) * g
        v2 = ADAM_B2 * v_ref[...] + (1.0 - ADAM_B2) * (g * g)
        m_hat = m2 / (1.0 - ADAM_B1 ** ADAM_STEP)
        v_hat = v2 / (1.0 - ADAM_B2 ** ADAM_STEP)
        g_ref[...] = g
        d_ref[...] = -ADAM_LR * (m_hat / (jnp.sqrt(v_hat) + ADAM_EPS) + ADAM_WD * w_ref[...])
        m2_ref[...] = m2
        v2_ref[...] = v2

    slab = pl.BlockSpec((None, tr, cols), lambda l, i: (l, i, 0))
    out = jax.ShapeDtypeStruct((n_layers, rows, cols), F32)
    p_specs = _layer_parts_specs(n_layers, parts[0].shape[0], tr, cols)
    return pl.pallas_call(
        body,
        name=name,
        grid=(n_layers, rows // tr),
        in_specs=p_specs + [slab, slab, slab],
        out_specs=[slab, slab, slab, slab],
        out_shape=[out, out, out, out],
        compiler_params=_cparams(("arbitrary", "arbitrary")),
    )(*parts, w, m, v)


SLAB_ROWS = 32
_SMALL_SHARDED = (("conv_w", (2, 31, 32)), ("ffn_conv_w", (2, 3, 704)))
_REPLICATED = (("g_mix", (2, 1024)), ("conv_b", (2, 256)), ("conv_ln_g", (2, 256)), ("conv_ln_b", (2, 256)),
               ("sgu_ln_g", (2, 256)), ("sgu_ln_b", (2, 256)), ("sgu_w", (2, 4, 128, 128)), ("sgu_b", (2, 4, 128)),
               ("g_out", (2, 1024)), ("g_ffn", (2, 1024)), ("ffn_conv_b", (2, 5632)), ("g_final", (1024,)))


def _seg_rows(n_elems):
    return -(-n_elems // LANES)


def _pack(arrays, lead=()):
    segs = []
    for a in arrays:
        flat = a.reshape(lead + (-1,)).astype(F32)
        pad = _seg_rows(flat.shape[-1]) * LANES - flat.shape[-1]
        if pad:
            flat = jnp.pad(flat, [(0, 0)] * len(lead) + [(0, pad)])
        segs.append(flat)
    flat = jnp.concatenate(segs, axis=-1)
    rows = flat.shape[-1] // LANES
    pad_rows = -rows % SLAB_ROWS
    if pad_rows:
        flat = jnp.pad(flat, [(0, 0)] * len(lead) + [(0, pad_rows * LANES)])
    return flat.reshape(lead + (rows + pad_rows, LANES))


def _unpack(slab, shapes, lead=()):
    flat = slab.reshape(lead + (-1,))
    out, off = [], 0
    for shape in shapes:
        n = math.prod(shape)
        out.append(flat[..., off:off + n].reshape(lead + tuple(shape)))
        off += _seg_rows(n) * LANES
    return out


def _split_last(full):
    split = full.shape[:-1] + (N_DEV, full.shape[-1] // N_DEV)
    return jnp.moveaxis(full.reshape(split), -2, 0)


def _join_last(blocks):
    moved = jnp.moveaxis(blocks, 0, -2)
    return moved.reshape(moved.shape[:-2] + (moved.shape[-2] * moved.shape[-1],))


def _gathered(wt, n, l, after):
    if isinstance(wt[n][l], _SplitExchange):
        wt[n][l] = wt[n][l].finish(after)
    return wt[n][l]


def _layer_fwd(l, x, wt, small):
    tag = f"l{l}"
    h = _rmsnorm_fwd(x, small["g_mix"][l][None], name=f"{tag}_norm_mix")
    w_in_t = _gathered(wt, "w_in_t", l, h)
    p = _matmul(h, w_in_t, "nt", name=f"{tag}_proj", out_dtype=BF16)
    p_ab = qkv = p
    ya = _mixer_a_fwd(p_ab, wt["conv_w"][l], small["conv_b"][l][None], small["conv_ln_g"][l][None],
                      small["conv_ln_b"][l][None], name=f"{tag}_mixer_a")
    bias = jnp.repeat(small["sgu_b"][l].T, HEAD_DIM, axis=1)
    yb = _sgu_fwd(p_ab, small["sgu_ln_g"][l][None], small["sgu_ln_b"][l][None], small["sgu_w"][l], bias,
                  name=f"{tag}_sgu")
    yc, t_tot = _attn_fwd(qkv, name=f"{tag}_attn")
    y = _combine_fwd(ya, yb, yc, small["g_out"][l][None], name=f"{tag}_combine")
    x1 = _matmul(y, _gathered(wt, "w_out", l, y), "nn", name=f"{tag}_out_proj", residual=x)
    h2 = _rmsnorm_fwd(x1, small["g_ffn"][l][None], name=f"{tag}_norm_ffn")
    up0 = _matmul(h2, _gathered(wt, "w_up_t", l, h2), "nt", name=f"{tag}_up", out_dtype=BF16)
    act = _ffn_act_fwd(up0, wt["ffn_conv_w"][l], small["ffn_conv_b"][l][None], name=f"{tag}_ffn_act")
    x2 = _matmul(act, _gathered(wt, "w_down", l, act), "nn", name=f"{tag}_down", residual=x1)
    saved = dict(x=x, h=h, p_ab=p, qkv=p, ya=ya, yb=yb, yc=yc, t_tot=t_tot, y=y, x1=x1, h2=h2, up0=up0,
                 act=act, bias=bias)
    return x2, saved


def _layer_bwd(l, dres, sv, wt, small, scattering, token):
    tag = f"l{l}b"
    g = {}

    def scatter(n, partial):
        scattering[n][l] = _SplitExchange(partial, kind="scatter_rows", name=f"scatter_{n}_l{l}")
        return _SplitExchange.start([scattering[n][l]], name=f"scatter_{n}_l{l}_start")

    dx2, dx2_b = dres
    dact = _matmul(dx2_b, wt["w_down"][l], "nt", name=f"{tag}_dact", out_dtype=BF16)
    tok = scatter("w_down", _matmul(sv["act"], dx2_b, "tn", name=f"{tag}_dw_down", out_dtype=BF16))
    dup_g, dup_v, dwg, dwv, dbg, dbv = _ffn_act_bwd(sv["up0"], dact, wt["ffn_conv_w"][l], small["ffn_conv_b"][l][None] + tok + token,
                                                    name=f"{tag}_ffn_act")
    g["ffn_conv_w"] = jnp.concatenate([dwg, dwv], axis=1)
    g["ffn_conv_b"] = jnp.concatenate([dbg[0], dbv[0]])
    dh2 = _matmul(dup_g, wt["w_up_t"][l], "nn", name=f"{tag}_dh2_gate")
    dh2 = _matmul(dup_v, wt["w_up_t"][l], "nn", name=f"{tag}_dh2_val", b_k0=D_FF, residual=dh2)
    dw_up = _matmul(dup_g, sv["h2"], "tn", name=f"{tag}_dw_up_gate", out_dtype=BF16, rows=(2 * D_FF, 0))
    dw_up = _matmul(dup_v, sv["h2"], "tn", name=f"{tag}_dw_up_val", out_dtype=BF16, rows=(2 * D_FF, D_FF), into=dw_up)
    tok = scatter("w_up_t", dw_up)
    dx1, dx1_b, dg = _rmsnorm_bwd(sv["x1"], small["g_ffn"][l][None] + tok, dh2, dx2, name=f"{tag}_norm_ffn")
    g["g_ffn"] = dg[0]
    dy = _matmul(dx1_b, wt["w_out"][l], "nt", name=f"{tag}_dy")
    tok = scatter("w_out", _matmul(sv["y"], dx1_b, "tn", name=f"{tag}_dw_out", out_dtype=BF16))
    dya, dyb, dyc, dg = _combine_bwd(dy, sv["ya"], sv["yb"], sv["yc"], small["g_out"][l][None] + tok,
                                     name=f"{tag}_combine")
    g["g_out"] = dg[0]
    dq, dk, dv = _attn_bwd(sv["qkv"], sv["t_tot"], dyc, name=f"{tag}_attn")
    dp_b, g["sgu_w"], db, dg, dbeta = _sgu_bwd(sv["p_ab"], dyb, small["sgu_ln_g"][l][None], small["sgu_ln_b"][l][None],
                                               small["sgu_w"][l], sv["bias"], name=f"{tag}_sgu")
    g["sgu_b"] = db[:, :N_SGU_HEADS].T
    g["sgu_ln_g"], g["sgu_ln_b"] = dg[0], dbeta[0]
    dp_a, g["conv_w"], dcb, dg, dbeta = _mixer_a_bwd(sv["p_ab"], dya, wt["conv_w"][l], small["conv_b"][l][None],
                                                     small["conv_ln_g"][l][None], small["conv_ln_b"][l][None],
                                                     name=f"{tag}_mixer_a")
    g["conv_b"], g["conv_ln_g"], g["conv_ln_b"] = dcb[0], dg[0], dbeta[0]
    dp = jnp.concatenate([dp_a, dp_b, dq, dk.astype(BF16), dv.astype(BF16)], axis=1)
    tok = scatter("w_in_t", _matmul(dp, sv["h"], "tn", name=f"{tag}_dw_in", out_dtype=BF16))
    dh = _matmul(dp, wt["w_in_t"][l], "nn", name=f"{tag}_dh")
    dx, dx_b, dg = _rmsnorm_bwd(sv["x"], small["g_mix"][l][None] + tok, dh, dx1, name=f"{tag}_norm_mix")
    g["g_mix"] = dg[0]
    return (dx, dx_b), g


_BIG = ("w_in_t", "w_out", "w_up_t", "w_down")


def kernel(x, g_mix, w_in, conv_w, conv_b, conv_ln_g, conv_ln_b, sgu_ln_g, sgu_ln_b, sgu_w, sgu_b, g_out, w_out, g_ffn, w_up, ffn_conv_w, ffn_conv_b, w_down, g_final, loss_target, m_g_mix, m_w_in, m_conv_w, m_conv_b, m_conv_ln_g, m_conv_ln_b, m_sgu_ln_g, m_sgu_ln_b, m_sgu_w, m_sgu_b, m_g_out, m_w_out, m_g_ffn, m_w_up, m_ffn_conv_w, m_ffn_conv_b, m_w_down, m_g_final, v_g_mix, v_w_in, v_conv_w, v_conv_b, v_conv_ln_g, v_conv_ln_b, v_sgu_ln_g, v_sgu_ln_b, v_sgu_w, v_sgu_b, v_g_out, v_w_out, v_g_ffn, v_w_up, v_ffn_conv_w, v_ffn_conv_b, v_w_down, v_g_final):
    given = dict(locals())
    n_layers = g_mix.shape[0]
    layers = range(n_layers)
    small_sharded = [n for n, _ in _SMALL_SHARDED]
    replicated = [n for n, _ in _REPLICATED]
    small = {n: given[n] for n in replicated}

    filters = _SplitExchange(_pack([given[n] for n in small_sharded]), kind="gather_blocks", name="gather_filters")
    wt = {n: [None] * n_layers for n in _BIG}
    wt["w_in_t"][0] = _SplitExchange(w_in[0].T.astype(BF16), kind="gather_rows", name="gather_w_in_t_l0")
    tok = _SplitExchange.start([filters, wt["w_in_t"][0]], name="gather_first_start")
    w_in, w_out, w_up, w_down, tok = lax.optimization_barrier((w_in, w_out, w_up, w_down, tok))
    shard = {"w_in_t": [w_in[l].T.astype(BF16) for l in layers], "w_out": [w_out[l].astype(BF16) for l in layers],
             "w_up_t": [w_up[l].T.astype(BF16) for l in layers], "w_down": [w_down[l].astype(BF16) for l in layers]}
    later = [(n, l) for l in layers for n in _BIG if (n, l) != ("w_in_t", 0)]
    for n, l in later:
        wt[n][l] = _SplitExchange(shard[n][l], kind="gather_rows", name=f"gather_{n}_l{l}")
    small["g_mix"] = g_mix + tok + _SplitExchange.start([wt[n][l] for n, l in later], name="gather_weights_start")
    gathered_filters = filters.finish(small["g_mix"])
    for n, blocks in zip(small_sharded, _unpack(gathered_filters, [s for _, s in _SMALL_SHARDED], lead=(N_DEV,))):
        wt[n] = _join_last(blocks)

    xs = x[0]
    saved = []
    for l in layers:
        xs, sv = _layer_fwd(l, xs, wt, small)
        saved.append(sv)
    loss_tile, dx, dx_b, dgf = _loss_head(xs, g_final[None], loss_target[0], name="loss_head")
    dres = (dx, dx_b)
    scattering = {n: [None] * n_layers for n in _BIG}
    layered = [n for n in replicated if n not in ("g_final", "sgu_w")]
    slabs, sgu_w_parts = [None] * n_layers, [None] * n_layers
    tok = 0.0
    for l in reversed(layers):
        dres, g = _layer_bwd(l, dres, saved[l], wt, small, scattering, tok)
        own = _pack([_split_last(g[n]) for n in small_sharded], lead=(N_DEV,))
        shared = _pack([g[n] for n in layered] + [dgf[0]])
        slab = jnp.concatenate([own, jnp.broadcast_to(shared[None], (N_DEV,) + shared.shape)], axis=1)
        slabs[l] = _SplitExchange(slab, kind="scatter_blocks", name=f"scatter_small_grads_l{l}")
        sgu_w_parts[l] = _SplitExchange(g["sgu_w"].reshape(-1, LANES).astype(BF16), kind="gather_blocks",
                                        name=f"gather_sgu_w_grads_l{l}")
        tok = _SplitExchange.start([slabs[l], sgu_w_parts[l]], name=f"small_grads_l{l}_start")
    n_own = own.shape[1]

    after_backward = jnp.full((8, LANES), tok)
    received = {n: [scattering[n][l].finish(after_backward) for l in layers] for n in _BIG}
    out = {}

    def update(n, parts, transposed=False):
        turn = (lambda a: jnp.swapaxes(a, 1, 2)) if transposed else (lambda a: a)
        results = _adamw(parts, turn(given[n]), turn(given["m_" + n]), turn(given["v_" + n]), name=f"adamw_{n}")
        for pre, res in zip(("grad_", "delta_", "new_m_", "new_v_"), results):
            out[pre + n] = turn(res)
        return results[0][0, :8, :LANES]

    update("w_out", received["w_out"])
    update("w_down", received["w_down"])
    update("w_in", received["w_in_t"], transposed=True)
    big_updated = update("w_up", received["w_up_t"], transposed=True)

    as_rows = lambda a: a.reshape(n_layers, -1, LANES)
    results = _adamw([ex.finish(big_updated) for ex in sgu_w_parts], as_rows(sgu_w), as_rows(m_sgu_w), as_rows(v_sgu_w),
                     name="adamw_sgu_w")
    for pre, res in zip(("grad_", "delta_", "new_m_", "new_v_"), results):
        out[pre + "sgu_w"] = res.reshape(sgu_w.shape)

    per_layer_g_final = {pre: jnp.broadcast_to(given[pre + "g_final"], (n_layers,) + g_final.shape) for pre in ("", "m_", "v_")}
    stacks = [jnp.concatenate([_pack([given[pre + n] for n in small_sharded], lead=(n_layers,)),
                               _pack([given[pre + n] for n in layered] + [per_layer_g_final[pre]], lead=(n_layers,))], axis=1)
              for pre in ("", "m_", "v_")]
    results = _adamw([slabs[l].finish(big_updated) for l in layers], *stacks, name="adamw_small")
    for pre, res in zip(("grad_", "delta_", "new_m_", "new_v_"), results):
        unpacked = (_unpack(res[:, :n_own], [s[1:] for _, s in _SMALL_SHARDED], lead=(n_layers,))
                    + _unpack(res[:, n_own:], [s[1:] for n, s in _REPLICATED if n in layered] + [g_final.shape], lead=(n_layers,)))
        for n, a in zip(small_sharded + layered + ["g_final"], unpacked):
            out[pre + n] = a[0] if n == "g_final" else a

    loss = lax.psum(loss_tile[0, 0], ("x", "y", "c"))
    order = list(_WEIGHT_ORDER)
    return (loss, dres[0][None], *[out["grad_" + n] for n in order], *[out["delta_" + n] for n in order],
            *[out["new_m_" + n] for n in order], *[out["new_v_" + n] for n in order])


_WEIGHT_ORDER = ("g_mix", "w_in", "conv_w", "conv_b", "conv_ln_g", "conv_ln_b", "sgu_ln_g", "sgu_ln_b", "sgu_w", "sgu_b",
                 "g_out", "w_out", "g_ffn", "w_up", "ffn_conv_w", "ffn_conv_b", "w_down", "g_final")
```

```python
import math

import jax
import jax.numpy as jnp
from jax import lax
from jax.experimental import pallas as pl
from jax.experimental.pallas import tpu as pltpu

F32 = jnp.float32
BF16 = jnp.bfloat16

N_DEV = 8
D_MODEL = 1024
HEAD_DIM = 64
D_CONV = 256
D_SGU = 256
D_SB = 512
D_AB = 2 * D_CONV + 2 * D_SGU
D_QKV = 3 * D_SB
D_IN = D_AB + D_QKV
CONV_K = 31
CONV_HALO = 32
FFN_K = 3
FFN_HALO = 8
D_FF = 2816
CHUNK = 128
EPS = 1e-6
LANES = 128

ADAM_LR = 0.001
ADAM_B1 = 0.9
ADAM_B2 = 0.999
ADAM_EPS = 1e-08
ADAM_WD = 0.01
ADAM_STEP = 10

VMEM_LIMIT = 56 * 1024 * 1024


def _cparams(sem=None):
    return pltpu.CompilerParams(dimension_semantics=sem, vmem_limit_bytes=VMEM_LIMIT)


def _tile(n, prefs=(512, 256, 128)):
    for t in prefs:
        if n % t == 0:
            return t
    return n


def _sigmoid(x):
    return 1.0 / (1.0 + jnp.exp(-x))


_INV_SQRT2 = 1.0 / math.sqrt(2.0)
_INV_SQRT2PI = 1.0 / math.sqrt(2.0 * math.pi)


def _gelu(x):
    return 0.5 * x * (1.0 + lax.erf(x * _INV_SQRT2))


def _gelu_grad(x):
    return 0.5 * (1.0 + lax.erf(x * _INV_SQRT2)) + x * jnp.exp(-0.5 * x * x) * _INV_SQRT2PI


def _dot(a, b, dims):
    return lax.dot_general(a, b, (dims, ((), ())), preferred_element_type=F32)


_NN = ((1,), (0,))
_NT = ((1,), (1,))
_TN = ((0,), (0,))


def _split_bf16(x):
    hi = x.astype(BF16)
    lo = (x - hi.astype(F32)).astype(BF16)
    return jnp.concatenate([hi, lo], axis=1)


def _matmul(a, b, mode, *, name, out_dtype=F32, residual=None, n=None, b_n0=0, b_k0=0, rows=None, into=None):
    if mode == "nn":
        (m, k), n = a.shape, (n or b.shape[1])
    elif mode == "nt":
        (m, k), n = a.shape, (n or b.shape[0])
    else:
        (k, m), n = a.shape, b.shape[1]
    has_res = residual is not None
    tm, tn = _matmul_tiles(m, n, k, a.dtype.itemsize, b.dtype.itemsize, jnp.dtype(out_dtype).itemsize, has_res, b_n0)
    j0 = b_n0 // tn
    total_rows, first_row = rows or (m, 0)
    assert b_k0 % k == 0 and first_row % tm == 0
    kb, i0 = b_k0 // k, first_row // tm

    if mode == "nn":
        a_spec = pl.BlockSpec((tm, k), lambda i, j: (i, 0))
        b_spec = pl.BlockSpec((k, tn), lambda i, j: (kb, j + j0))
        dims = _NN
    elif mode == "nt":
        a_spec = pl.BlockSpec((tm, k), lambda i, j: (i, 0))
        b_spec = pl.BlockSpec((tn, k), lambda i, j: (j + j0, 0))
        dims = _NT
    else:
        a_spec = pl.BlockSpec((k, tm), lambda i, j: (0, i))
        b_spec = pl.BlockSpec((k, tn), lambda i, j: (0, j))
        dims = _TN
    o_spec = pl.BlockSpec((tm, tn), lambda i, j: (i + i0, j))
    r_spec = pl.BlockSpec((tm, tn), lambda i, j: (i, j))

    def body(*refs):
        a_ref, b_ref = refs[:2]
        acc = _dot(a_ref[...].astype(BF16), b_ref[...].astype(BF16), dims)
        if has_res:
            acc = acc + refs[2][...]
        refs[-1][...] = acc.astype(out_dtype)

    in_specs = [a_spec, b_spec] + ([r_spec] if has_res else [])
    args = (a, b) + ((residual,) if has_res else ())
    aliases = {}
    if into is not None:
        aliases = {len(args): 0}
        in_specs.append(pl.BlockSpec(memory_space=pl.ANY))
        args += (into,)

        def body(*refs, inner=body):
            inner(*refs[:len(args) - 1], refs[-1])

    return pl.pallas_call(
        body,
        name=name,
        grid=(m // tm, n // tn),
        in_specs=in_specs,
        out_specs=o_spec,
        out_shape=jax.ShapeDtypeStruct((total_rows, n), out_dtype),
        input_output_aliases=aliases,
        compiler_params=_cparams(("parallel", "parallel")),
    )(*args)


MATMUL_VMEM_BUDGET = 40 * 1024 * 1024


def _matmul_tiles(m, n, k, a_bytes, b_bytes, out_bytes, has_res, n_offset):
    def divisors(size, cap, also=0):
        return [t for t in range(cap, 0, -LANES) if size % t == 0 and also % t == 0] or [size]

    for tm in divisors(m, 1024):
        for tn in divisors(n, 1408, n_offset):
            blocks = tm * k * a_bytes + k * tn * b_bytes + tm * tn * (out_bytes + (4 if has_res else 0))
            if 2 * blocks <= MATMUL_VMEM_BUDGET:
                return tm, tn
    raise ValueError(f"no matmul tiling for {m} x {n} x {k}")


ROW_TILE = 512


def _rmsnorm_fwd(x, g, *, name):
    s, d = x.shape

    def body(x_ref, g_ref, h_ref):
        xv = x_ref[...]
        r = lax.rsqrt(jnp.mean(xv * xv, axis=-1, keepdims=True) + EPS)
        h_ref[...] = (xv * r * g_ref[...]).astype(BF16)

    return pl.pallas_call(
        body,
        name=name,
        grid=(s // ROW_TILE,),
        in_specs=[pl.BlockSpec((ROW_TILE, d), lambda i: (i, 0)), pl.BlockSpec((1, d), lambda i: (0, 0))],
        out_specs=pl.BlockSpec((ROW_TILE, d), lambda i: (i, 0)),
        out_shape=jax.ShapeDtypeStruct((s, d), BF16),
        compiler_params=_cparams(("parallel",)),
    )(x, g)


def _rmsnorm_bwd(x, g, dh, dres, *, name):
    s, d = x.shape

    def body(x_ref, g_ref, dh_ref, dres_ref, dx_ref, dxb_ref, dg_ref):
        xv = x_ref[...]
        r = lax.rsqrt(jnp.mean(xv * xv, axis=-1, keepdims=True) + EPS)
        xhat = xv * r
        dhv = dh_ref[...]
        dxhat = dhv * g_ref[...]
        dx = dres_ref[...] + r * (dxhat - xhat * jnp.mean(dxhat * xhat, axis=-1, keepdims=True))
        dx_ref[...] = dx
        dxb_ref[...] = dx.astype(BF16)
        part = jnp.sum(dhv * xhat, axis=0, keepdims=True)

        @pl.when(pl.program_id(0) == 0)
        def _():
            dg_ref[...] = part

        @pl.when(pl.program_id(0) > 0)
        def _():
            dg_ref[...] += part

    row = pl.BlockSpec((ROW_TILE, d), lambda i: (i, 0))
    vec = pl.BlockSpec((1, d), lambda i: (0, 0))
    return pl.pallas_call(
        body,
        name=name,
        grid=(s // ROW_TILE,),
        in_specs=[row, vec, row, row],
        out_specs=[row, row, vec],
        out_shape=[jax.ShapeDtypeStruct((s, d), F32), jax.ShapeDtypeStruct((s, d), BF16),
                   jax.ShapeDtypeStruct((1, d), F32)],
        compiler_params=_cparams(("arbitrary",)),
    )(x, g, dh, dres)


def _loss_head(x, g, target, *, name):
    s, d = x.shape

    def body(x_ref, g_ref, t_ref, loss_ref, dx_ref, dxb_ref, dg_ref):
        xv = x_ref[...]
        gv = g_ref[...]
        r = lax.rsqrt(jnp.mean(xv * xv, axis=-1, keepdims=True) + EPS)
        xhat = xv * r
        diff = xhat * gv - t_ref[...]
        dy = diff * (1.0 / d)
        dxhat = dy * gv
        dx = r * (dxhat - xhat * jnp.mean(dxhat * xhat, axis=-1, keepdims=True))
        dx_ref[...] = dx
        dxb_ref[...] = dx.astype(BF16)
        dg_part = jnp.sum(dy * xhat, axis=0, keepdims=True)
        row_loss = jnp.sum(diff * diff, axis=-1, keepdims=True)
        loss_part = jnp.sum(row_loss, axis=0, keepdims=True) * (0.5 / d)

        @pl.when(pl.program_id(0) == 0)
        def _():
            dg_ref[...] = dg_part
            loss_ref[...] = jnp.broadcast_to(loss_part, loss_ref.shape)

        @pl.when(pl.program_id(0) > 0)
        def _():
            dg_ref[...] += dg_part
            loss_ref[...] += jnp.broadcast_to(loss_part, loss_ref.shape)

    row = pl.BlockSpec((ROW_TILE, d), lambda i: (i, 0))
    vec = pl.BlockSpec((1, d), lambda i: (0, 0))
    tile = pl.BlockSpec((8, LANES), lambda i: (0, 0))
    return pl.pallas_call(
        body,
        name=name,
        grid=(s // ROW_TILE,),
        in_specs=[row, vec, row],
        out_specs=[tile, row, row, vec],
        out_shape=[jax.ShapeDtypeStruct((8, LANES), F32), jax.ShapeDtypeStruct((s, d), F32),
                   jax.ShapeDtypeStruct((s, d), BF16), jax.ShapeDtypeStruct((1, d), F32)],
        compiler_params=_cparams(("arbitrary",)),
    )(x, g, target)


_BRANCHES = ((0, D_CONV), (D_CONV, D_SGU), (D_CONV + D_SGU, D_SB))


def _combine_fwd(ya, yb, yc, g, *, name):
    s = ya.shape[0]

    def body(ya_ref, yb_ref, yc_ref, g_ref, y_ref):
        for ref, (off, w) in zip((ya_ref, yb_ref, yc_ref), _BRANCHES):
            v = ref[...]
            r = lax.rsqrt(jnp.mean(v * v, axis=-1, keepdims=True) + EPS)
            y_ref[:, off:off + w] = (v * r * g_ref[:, off:off + w]).astype(BF16)

    def row(w):
        return pl.BlockSpec((ROW_TILE, w), lambda i: (i, 0))

    return pl.pallas_call(
        body,
        name=name,
        grid=(s // ROW_TILE,),
        in_specs=[row(D_CONV), row(D_SGU), row(D_SB), pl.BlockSpec((1, D_MODEL), lambda i: (0, 0))],
        out_specs=row(D_MODEL),
        out_shape=jax.ShapeDtypeStruct((s, D_MODEL), BF16),
        compiler_params=_cparams(("parallel",)),
    )(ya, yb, yc, g)


def _combine_bwd(dy, ya, yb, yc, g, *, name):
    s = ya.shape[0]

    def body(dy_ref, ya_ref, yb_ref, yc_ref, g_ref, dya_ref, dyb_ref, dyc_ref, dg_ref):
        first = pl.program_id(0) == 0
        for ref, dref, (off, w) in zip((ya_ref, yb_ref, yc_ref), (dya_ref, dyb_ref, dyc_ref), _BRANCHES):
            v = ref[...]
            r = lax.rsqrt(jnp.mean(v * v, axis=-1, keepdims=True) + EPS)
            n = v * r
            dout = dy_ref[:, off:off + w]
            dn = dout * g_ref[:, off:off + w]
            dref[...] = r * (dn - n * jnp.mean(dn * n, axis=-1, keepdims=True))
            part = jnp.sum(dout * n, axis=0, keepdims=True)

            @pl.when(first)
            def _():
                dg_ref[:, off:off + w] = part

            @pl.when(jnp.logical_not(first))
            def _():
                dg_ref[:, off:off + w] += part

    def row(w):
        return pl.BlockSpec((ROW_TILE, w), lambda i: (i, 0))

    vec = pl.BlockSpec((1, D_MODEL), lambda i: (0, 0))
    return pl.pallas_call(
        body,
        name=name,
        grid=(s // ROW_TILE,),
        in_specs=[row(D_MODEL), row(D_CONV), row(D_SGU), row(D_SB), vec],
        out_specs=[row(D_CONV), row(D_SGU), row(D_SB), vec],
        out_shape=[jax.ShapeDtypeStruct((s, D_CONV), F32), jax.ShapeDtypeStruct((s, D_SGU), F32),
                   jax.ShapeDtypeStruct((s, D_SB), F32), jax.ShapeDtypeStruct((1, D_MODEL), F32)],
        compiler_params=_cparams(("arbitrary",)),
    )(dy, ya, yb, yc, g)


CONV_TILE = 128


def _shift_down(window, j, halo):
    return pltpu.roll(window, j, 0)[halo:, :] if j else window[halo:, :]


def _shift_up(window, j, n_out):
    n = window.shape[0]
    return pltpu.roll(window, n - j, 0)[:n_out, :] if j else window[:n_out, :]


def _mixer_a_fwd(p_ab, conv_w, conv_b, ln_g, ln_b, *, name):
    s = p_ab.shape[0]
    nt = s // CONV_TILE

    def body(p_ref, w_ref, b_ref, g_ref, beta_ref, y_ref, h_ref):
        h_ref[0:CONV_HALO, :] = jnp.zeros((CONV_HALO, D_CONV), F32)

        def glu(i, c):
            t0 = pl.multiple_of(i * CONV_TILE, CONV_TILE)
            a = p_ref[pl.ds(t0, CONV_TILE), 0:D_CONV].astype(F32)
            gate = p_ref[pl.ds(t0, CONV_TILE), D_CONV:2 * D_CONV].astype(F32)
            h_ref[pl.ds(t0 + CONV_HALO, CONV_TILE), :] = a * _sigmoid(gate)
            return c

        lax.fori_loop(0, nt, glu, 0)

        def conv(i, c):
            t0 = pl.multiple_of(i * CONV_TILE, CONV_TILE)
            window = h_ref[pl.ds(t0, CONV_TILE + CONV_HALO), :]
            acc = jnp.zeros((CONV_TILE, D_CONV), F32) + b_ref[...]
            for k in range(CONV_K):
                acc = acc + w_ref[k:k + 1, :] * _shift_down(window, CONV_K - 1 - k, CONV_HALO)
            mu = jnp.mean(acc, axis=-1, keepdims=True)
            xc = acc - mu
            rstd = lax.rsqrt(jnp.mean(xc * xc, axis=-1, keepdims=True) + EPS)
            z = xc * rstd * g_ref[...] + beta_ref[...]
            y_ref[pl.ds(t0, CONV_TILE), :] = z * _sigmoid(z)
            return c

        lax.fori_loop(0, nt, conv, 0)

    full = lambda shape: pl.BlockSpec(shape, lambda i: (0, 0))
    return pl.pallas_call(
        body,
        name=name,
        grid=(1,),
        in_specs=[full((s, 2 * D_CONV)), full((CONV_K, D_CONV)), full((1, D_CONV)), full((1, D_CONV)),
                  full((1, D_CONV))],
        out_specs=full((s, D_CONV)),
        out_shape=jax.ShapeDtypeStruct((s, D_CONV), F32),
        scratch_shapes=[pltpu.VMEM((s + CONV_HALO, D_CONV), F32)],
        compiler_params=_cparams(("arbitrary",)),
    )(p_ab, conv_w, conv_b, ln_g, ln_b)


def _mixer_a_bwd(p_ab, dya, conv_w, conv_b, ln_g, ln_b, *, name):
    s = p_ab.shape[0]
    nt = s // CONV_TILE

    def body(p_ref, dy_ref, w_ref, b_ref, g_ref, beta_ref, dp_ref, dw_ref, db_ref, dg_ref, dbeta_ref, h_ref, dc_ref):
        h_ref[0:CONV_HALO, :] = jnp.zeros((CONV_HALO, D_CONV), F32)
        dc_ref[s:s + CONV_HALO, :] = jnp.zeros((CONV_HALO, D_CONV), F32)
        dw_ref[...] = jnp.zeros_like(dw_ref)
        db_ref[...] = jnp.zeros_like(db_ref)
        dg_ref[...] = jnp.zeros_like(dg_ref)
        dbeta_ref[...] = jnp.zeros_like(dbeta_ref)

        def glu(i, c):
            t0 = pl.multiple_of(i * CONV_TILE, CONV_TILE)
            a = p_ref[pl.ds(t0, CONV_TILE), 0:D_CONV].astype(F32)
            gate = p_ref[pl.ds(t0, CONV_TILE), D_CONV:2 * D_CONV].astype(F32)
            h_ref[pl.ds(t0 + CONV_HALO, CONV_TILE), :] = a * _sigmoid(gate)
            return c

        lax.fori_loop(0, nt, glu, 0)

        def conv_bwd(i, c):
            t0 = pl.multiple_of(i * CONV_TILE, CONV_TILE)
            window = h_ref[pl.ds(t0, CONV_TILE + CONV_HALO), :]
            taps = [_shift_down(window, CONV_K - 1 - k, CONV_HALO) for k in range(CONV_K)]
            acc = jnp.zeros((CONV_TILE, D_CONV), F32) + b_ref[...]
            for k in range(CONV_K):
                acc = acc + w_ref[k:k + 1, :] * taps[k]
            mu = jnp.mean(acc, axis=-1, keepdims=True)
            xc = acc - mu
            rstd = lax.rsqrt(jnp.mean(xc * xc, axis=-1, keepdims=True) + EPS)
            xhat = xc * rstd
            z = xhat * g_ref[...] + beta_ref[...]
            sg = _sigmoid(z)
            dz = dy_ref[pl.ds(t0, CONV_TILE), :] * (sg * (1.0 + z * (1.0 - sg)))
            dg_ref[...] += jnp.sum(dz * xhat, axis=0, keepdims=True)
            dbeta_ref[...] += jnp.sum(dz, axis=0, keepdims=True)
            dxhat = dz * g_ref[...]
            dc = rstd * (dxhat - jnp.mean(dxhat, axis=-1, keepdims=True)
                         - xhat * jnp.mean(dxhat * xhat, axis=-1, keepdims=True))
            dc_ref[pl.ds(t0, CONV_TILE), :] = dc
            db_ref[...] += jnp.sum(dc, axis=0, keepdims=True)
            for k in range(CONV_K):
                dw_ref[k:k + 1, :] += jnp.sum(dc * taps[k], axis=0, keepdims=True)
            return c

        lax.fori_loop(0, nt, conv_bwd, 0)

        def glu_bwd(i, c):
            t0 = pl.multiple_of(i * CONV_TILE, CONV_TILE)
            window = dc_ref[pl.ds(t0, CONV_TILE + CONV_HALO), :]
            dh = jnp.zeros((CONV_TILE, D_CONV), F32)
            for j in range(CONV_K):
                dh = dh + w_ref[CONV_K - 1 - j:CONV_K - j, :] * _shift_up(window, j, CONV_TILE)
            a = p_ref[pl.ds(t0, CONV_TILE), 0:D_CONV].astype(F32)
            sg = _sigmoid(p_ref[pl.ds(t0, CONV_TILE), D_CONV:2 * D_CONV].astype(F32))
            dp_ref[pl.ds(t0, CONV_TILE), 0:D_CONV] = (dh * sg).astype(BF16)
            dp_ref[pl.ds(t0, CONV_TILE), D_CONV:2 * D_CONV] = (dh * a * sg * (1.0 - sg)).astype(BF16)
            return c

        lax.fori_loop(0, nt, glu_bwd, 0)

    full = lambda shape: pl.BlockSpec(shape, lambda i: (0, 0))
    vec = jax.ShapeDtypeStruct((1, D_CONV), F32)
    return pl.pallas_call(
        body,
        name=name,
        grid=(1,),
        in_specs=[full((s, 2 * D_CONV)), full((s, D_CONV)), full((CONV_K, D_CONV)), full((1, D_CONV)),
                  full((1, D_CONV)), full((1, D_CONV))],
        out_specs=[full((s, 2 * D_CONV)), full((CONV_K, D_CONV)), full((1, D_CONV)), full((1, D_CONV)),
                   full((1, D_CONV))],
        out_shape=[jax.ShapeDtypeStruct((s, 2 * D_CONV), BF16), jax.ShapeDtypeStruct((CONV_K, D_CONV), F32),
                   vec, vec, vec],
        scratch_shapes=[pltpu.VMEM((s + CONV_HALO, D_CONV), F32), pltpu.VMEM((s + CONV_HALO, D_CONV), F32)],
        compiler_params=_cparams(("arbitrary",)),
    )(p_ab, dya, conv_w, conv_b, ln_g, ln_b)


N_SGU_HEADS = D_SGU // HEAD_DIM


def _head_masks(width):
    lane = lax.broadcasted_iota(jnp.int32, (1, width), 1)
    return [(lane >= h * HEAD_DIM) & (lane < (h + 1) * HEAD_DIM) for h in range(width // HEAD_DIM)]


def _tril_mask():
    r = lax.broadcasted_iota(jnp.int32, (CHUNK, CHUNK), 0)
    c = lax.broadcasted_iota(jnp.int32, (CHUNK, CHUNK), 1)
    return c <= r


def _sgu_norm(bv, g, beta):
    vg = _gelu(bv)
    mu = jnp.mean(vg, axis=-1, keepdims=True)
    xc = vg - mu
    rstd = lax.rsqrt(jnp.mean(xc * xc, axis=-1, keepdims=True) + EPS)
    xhat = xc * rstd
    return xhat, rstd, xhat * g + beta


def _sgu_fwd(p_ab, ln_g, ln_b, w_s, bias, *, name):
    s = p_ab.shape[0]

    def body(p_ref, g_ref, beta_ref, w_ref, bias_ref, y_ref):
        u = _gelu(p_ref[:, 0:D_SGU].astype(F32))
        _, _, vn = _sgu_norm(p_ref[:, D_SGU:2 * D_SGU].astype(F32), g_ref[...], beta_ref[...])
        vb = vn.astype(BF16)
        tril = _tril_mask()
        mixed = bias_ref[...]
        for h, m in enumerate(_head_masks(D_SGU)):
            wh = jnp.where(tril, w_ref[h], 0.0).astype(BF16)
            mixed = mixed + _dot(wh, jnp.where(m, vb, jnp.zeros_like(vb)), _NN)
        y_ref[...] = u * mixed

    return pl.pallas_call(
        body,
        name=name,
        grid=(s // CHUNK,),
        in_specs=[pl.BlockSpec((CHUNK, 2 * D_SGU), lambda i: (i, 1)),
                  pl.BlockSpec((1, D_SGU), lambda i: (0, 0)), pl.BlockSpec((1, D_SGU), lambda i: (0, 0)),
                  pl.BlockSpec((N_SGU_HEADS, CHUNK, CHUNK), lambda i: (0, 0, 0)),
                  pl.BlockSpec((CHUNK, D_SGU), lambda i: (0, 0))],
        out_specs=pl.BlockSpec((CHUNK, D_SGU), lambda i: (i, 0)),
        out_shape=jax.ShapeDtypeStruct((s, D_SGU), F32),
        compiler_params=_cparams(("parallel",)),
    )(p_ab, ln_g, ln_b, w_s, bias)


def _sgu_bwd(p_ab, dyb, ln_g, ln_b, w_s, bias, *, name):
    s = p_ab.shape[0]
    n_chunks = s // CHUNK

    def body(p_ref, dy_ref, g_ref, beta_ref, w_ref, bias_ref, dp_ref, dw_ref, db_ref, dg_ref, dbeta_ref, dbias_ref):
        @pl.when(pl.program_id(0) == 0)
        def _():
            dw_ref[...] = jnp.zeros_like(dw_ref)
            dbias_ref[...] = jnp.zeros_like(dbias_ref)
            dg_ref[...] = jnp.zeros_like(dg_ref)
            dbeta_ref[...] = jnp.zeros_like(dbeta_ref)

        bu = p_ref[:, 0:D_SGU].astype(F32)
        bv = p_ref[:, D_SGU:2 * D_SGU].astype(F32)
        u = _gelu(bu)
        gv = g_ref[...]
        xhat, rstd, vn = _sgu_norm(bv, gv, beta_ref[...])
        vb = vn.astype(BF16)
        tril = _tril_mask()
        masks = _head_masks(D_SGU)
        whs = [jnp.where(tril, w_ref[h], 0.0).astype(BF16) for h in range(N_SGU_HEADS)]
        mixed = bias_ref[...]
        for h, m in enumerate(masks):
            mixed = mixed + _dot(whs[h], jnp.where(m, vb, jnp.zeros_like(vb)), _NN)
        dy = dy_ref[...]
        dp_ref[:, 0:D_SGU] = (dy * mixed * _gelu_grad(bu)).astype(BF16)
        dmixed = dy * u
        dbias_ref[...] += dmixed
        dmb = dmixed.astype(BF16)
        dvn = jnp.zeros((CHUNK, D_SGU), F32)
        for h, m in enumerate(masks):
            dmh = jnp.where(m, dmb, jnp.zeros_like(dmb))
            dvn = dvn + _dot(whs[h], dmh, _TN)
            dw_ref[h] += jnp.where(tril, _dot(dmh, vb, _NT), 0.0)
        dg_ref[...] += jnp.sum(dvn * xhat, axis=0, keepdims=True)
        dbeta_ref[...] += jnp.sum(dvn, axis=0, keepdims=True)
        dxhat = dvn * gv
        dvg = rstd * (dxhat - jnp.mean(dxhat, axis=-1, keepdims=True)
                      - xhat * jnp.mean(dxhat * xhat, axis=-1, keepdims=True))
        dp_ref[:, D_SGU:2 * D_SGU] = (dvg * _gelu_grad(bv)).astype(BF16)

        @pl.when(pl.program_id(0) == n_chunks - 1)
        def _():
            chan = lax.broadcasted_iota(jnp.int32, (D_SGU, LANES), 0)
            head = lax.broadcasted_iota(jnp.int32, (D_SGU, LANES), 1)
            to_head = jnp.where(chan // HEAD_DIM == head, 1.0, 0.0).astype(BF16)
            db_ref[...] = _dot(_split_bf16(dbias_ref[...]), jnp.concatenate([to_head, to_head], axis=0), _NN)

    vec = pl.BlockSpec((1, D_SGU), lambda i: (0, 0))
    wspec = pl.BlockSpec((N_SGU_HEADS, CHUNK, CHUNK), lambda i: (0, 0, 0))
    bspec = pl.BlockSpec((CHUNK, D_SGU), lambda i: (0, 0))
    return pl.pallas_call(
        body,
        name=name,
        grid=(n_chunks,),
        in_specs=[pl.BlockSpec((CHUNK, 2 * D_SGU), lambda i: (i, 1)), pl.BlockSpec((CHUNK, D_SGU), lambda i: (i, 0)),
                  vec, vec, wspec, bspec],
        out_specs=[pl.BlockSpec((CHUNK, 2 * D_SGU), lambda i: (i, 0)), wspec,
                   pl.BlockSpec((CHUNK, LANES), lambda i: (0, 0)), vec, vec],
        out_shape=[jax.ShapeDtypeStruct((s, 2 * D_SGU), BF16),
                   jax.ShapeDtypeStruct((N_SGU_HEADS, CHUNK, CHUNK), F32),
                   jax.ShapeDtypeStruct((CHUNK, LANES), F32),
                   jax.ShapeDtypeStruct((1, D_SGU), F32), jax.ShapeDtypeStruct((1, D_SGU), F32)],
        scratch_shapes=[pltpu.VMEM((CHUNK, D_SGU), F32)],
        compiler_params=_cparams(("arbitrary",)),
    )(p_ab, dyb, ln_g, ln_b, w_s, bias)


N_PAIRS = D_SB // LANES
QKV_BLOCK0 = D_AB // LANES
SB_SCALE = HEAD_DIM ** -0.5


def _sb_logits(z, valid):
    nz = -z
    t = jnp.log(1.0 + jnp.exp(jnp.minimum(z, nz)))
    l1 = jnp.minimum(nz, 0.0) - t
    if valid is not None:
        l1 = jnp.where(valid, l1, 0.0)
    return l1, jnp.minimum(z, 0.0) - t


def _split_hi_lo(x):
    hi = lax.bitcast_convert_type(lax.bitcast_convert_type(x, jnp.uint32) & jnp.uint32(0xFFFF0000), F32)
    return jnp.concatenate([hi, x - hi], axis=1)


def _cumsum_operand(keep):
    half = jnp.concatenate([keep.astype(F32), jnp.ones((CHUNK, CHUNK), F32)], axis=1)
    return jnp.concatenate([half, half], axis=0)


Q_BLOCKS_PER_STEP = 4


def _q_blocks_per_step(nq):
    return next(n for n in (Q_BLOCKS_PER_STEP, 2, 1) if nq % n == 0)


def _attn_fwd(qkv, *, name):
    s = qkv.shape[0]
    nq = s // CHUNK
    per_step = _q_blocks_per_step(nq)

    def body(q_ref, k_ref, v_ref, o_ref, t_ref):
        masks = _head_masks(LANES)
        row = lax.broadcasted_iota(jnp.int32, (CHUNK, CHUNK), 0)
        col = lax.broadcasted_iota(jnp.int32, (CHUNK, CHUNK), 1)
        after_op = _cumsum_operand(row > col)
        cmr = col - row
        zc = jnp.zeros((CHUNK, LANES), F32)

        def q_block(sub, _):
            qi = pl.program_id(1) * per_step + sub
            q_rows = pl.ds(pl.multiple_of(sub * CHUNK, CHUNK), CHUNK)
            q = q_ref[q_rows, :] * SB_SCALE
            zero = jnp.zeros_like(q)
            qs = [jnp.where(m, q, zero) for m in masks]

            def blocks(js, carry):
                o, c0, c1 = carry
                kvs, valids = [], []
                for j in js:
                    k0 = pl.multiple_of(jnp.maximum(j, 0) * CHUNK, CHUNK)
                    kvs.append((k_ref[pl.ds(k0, CHUNK), :], v_ref[pl.ds(k0, CHUNK), :]))
                    valids.append(cmr < jnp.where(j >= 0, (qi - j) * CHUNK, -CHUNK))
                units = [(h, b) for b in range(len(js)) for h in range(2)]
                zs = [_dot(qs[h], kvs[b][0], _NT) for h, b in units]
                logits = [_sb_logits(z, valids[b]) for z, (h, b) in zip(zs, units)]
                sums = [_dot(_split_hi_lo(l1), after_op, _NN) for l1, _ in logits]
                cs = [c0, c1]
                probs = []
                for (h, b), (_, lb), sm in zip(units, logits, sums):
                    probs.append(jnp.where(valids[b], jnp.exp(lb + sm[:, :CHUNK] + cs[h]), 0.0))
                    cs[h] = cs[h] + sm[:, CHUNK:]
                for (h, b), a in zip(units, probs):
                    o = o + _dot(a.astype(BF16), jnp.where(masks[h], kvs[b][1], zero), _NN)
                return o, cs[0], cs[1]

            n_four = (qi + 1) // 4
            carry = lax.fori_loop(0, n_four, lambda jj, c: blocks([qi - 4 * jj - i for i in range(4)], c), (zc,) * 3)
            top = qi - 4 * n_four
            o, c0, c1 = lax.fori_loop(0, (top + 2) // 2, lambda jj, c: blocks([top - 2 * jj, top - 2 * jj - 1], c), carry)
            o_ref[q_rows, :] = o
            t_ref[q_rows, 0:LANES] = c0
            t_ref[q_rows, LANES:2 * LANES] = c1
            return 0

        lax.fori_loop(0, per_step, q_block, 0)

    rows = per_step * CHUNK
    return pl.pallas_call(
        body,
        name=name,
        grid=(N_PAIRS, nq // per_step),
        in_specs=[pl.BlockSpec((rows, LANES), lambda p, i: (i, QKV_BLOCK0 + p)),
                  pl.BlockSpec((s, LANES), lambda p, i: (0, QKV_BLOCK0 + N_PAIRS + p)),
                  pl.BlockSpec((s, LANES), lambda p, i: (0, QKV_BLOCK0 + 2 * N_PAIRS + p))],
        out_specs=[pl.BlockSpec((rows, LANES), lambda p, i: (i, p)),
                   pl.BlockSpec((rows, 2 * LANES), lambda p, i: (i, p))],
        out_shape=[jax.ShapeDtypeStruct((s, D_SB), F32), jax.ShapeDtypeStruct((s, 2 * D_SB), F32)],
        compiler_params=_cparams(("parallel", "parallel")),
    )(qkv, qkv, qkv)


def _attn_bwd(qkv, t_tot, do, *, name):
    s = qkv.shape[0]
    nq = s // CHUNK
    per_step = _q_blocks_per_step(nq)

    def body(q_ref, k_ref, v_ref, t_ref, do_ref, dq_ref, dk_ref, dv_ref):
        @pl.when(pl.program_id(1) == 0)
        def _():
            dk_ref[...] = jnp.zeros_like(dk_ref)
            dv_ref[...] = jnp.zeros_like(dv_ref)

        masks = _head_masks(LANES)
        row = lax.broadcasted_iota(jnp.int32, (CHUNK, CHUNK), 0)
        col = lax.broadcasted_iota(jnp.int32, (CHUNK, CHUNK), 1)
        upto_op = _cumsum_operand(row <= col)
        before_op = _cumsum_operand(row < col)
        cmr = col - row
        zc = jnp.zeros((CHUNK, LANES), F32)

        def q_block(sub, _):
            qi = pl.program_id(1) * per_step + sub
            q_rows = pl.ds(pl.multiple_of(sub * CHUNK, CHUNK), CHUNK)
            q = q_ref[q_rows, :] * SB_SCALE
            dob = do_ref[q_rows, :].astype(BF16)
            zero = jnp.zeros_like(q)
            qs = [jnp.where(m, q, zero) for m in masks]
            dos = [jnp.where(m, dob, zero) for m in masks]
            tots = [t_ref[q_rows, 0:LANES], t_ref[q_rows, LANES:2 * LANES]]

            def blocks(js, carry):
                dq, cl0, cl1, cp0, cp1 = carry
                starts = [pl.multiple_of(jnp.minimum(j, nq - 1) * CHUNK, CHUNK) for j in js]
                valids = [cmr < (qi - j) * CHUNK for j in js]
                kvs = [(k_ref[pl.ds(k0, CHUNK), :], v_ref[pl.ds(k0, CHUNK), :]) for k0 in starts]
                units = [(h, b) for b in range(len(js)) for h in range(2)]
                zs = [_dot(qs[h], kvs[b][0], _NT) for h, b in units]
                das = [_dot(dos[h], kvs[b][1], _NT) for h, b in units]
                logits = [_sb_logits(z, valids[b]) for z, (h, b) in zip(zs, units)]
                sums = [_dot(_split_hi_lo(l1), upto_op, _NN) for l1, _ in logits]
                cls, cps = [cl0, cl1], [cp0, cp1]
                probs, gs = [], []
                for (h, b), (_, lb), sm, da in zip(units, logits, sums, das):
                    a = jnp.where(valids[b], jnp.exp(lb + (tots[h] - cls[h] - sm[:, :CHUNK])), 0.0)
                    probs.append(a)
                    gs.append(a * da)
                    cls[h] = cls[h] + sm[:, CHUNK:]
                sums_g = [_dot(_split_hi_lo(g), before_op, _NN) for g in gs]
                dzs = []
                for (h, b), (_, lb), g, sg in zip(units, logits, gs, sums_g):
                    dz = g - (g + sg[:, :CHUNK] + cps[h]) * jnp.exp(lb)
                    dzs.append(jnp.where(valids[b], dz, 0.0).astype(BF16))
                    cps[h] = cps[h] + sg[:, CHUNK:]
                for (h, b), dzb in zip(units, dzs):
                    dq = dq + _dot(dzb, jnp.where(masks[h], kvs[b][0], zero), _NN)
                for b, k0 in enumerate(starts):
                    dk_ref[pl.ds(k0, CHUNK), :] += _dot(dzs[2 * b], qs[0], _TN) + _dot(dzs[2 * b + 1], qs[1], _TN)
                    dv_ref[pl.ds(k0, CHUNK), :] += (_dot(probs[2 * b].astype(BF16), dos[0], _TN)
                                                    + _dot(probs[2 * b + 1].astype(BF16), dos[1], _TN))
                return dq, cls[0], cls[1], cps[0], cps[1]

            n_four = (qi + 1) // 4
            carry = lax.fori_loop(0, n_four, lambda jj, c: blocks([4 * jj + i for i in range(4)], c), (zc,) * 5)
            base = 4 * n_four
            carry = lax.fori_loop(0, (qi - base + 2) // 2, lambda jj, c: blocks([base + 2 * jj, base + 2 * jj + 1], c), carry)
            dq_ref[q_rows, :] = (carry[0] * SB_SCALE).astype(BF16)
            return 0

        lax.fori_loop(0, per_step, q_block, 0)

    rows = per_step * CHUNK
    blk = pl.BlockSpec((rows, LANES), lambda p, i: (i, p))
    col_blk = pl.BlockSpec((s, LANES), lambda p, i: (0, p))
    out = jax.ShapeDtypeStruct((s, D_SB), F32)
    return pl.pallas_call(
        body,
        name=name,
        grid=(N_PAIRS, nq // per_step),
        in_specs=[pl.BlockSpec((rows, LANES), lambda p, i: (i, QKV_BLOCK0 + p)),
                  pl.BlockSpec((s, LANES), lambda p, i: (0, QKV_BLOCK0 + N_PAIRS + p)),
                  pl.BlockSpec((s, LANES), lambda p, i: (0, QKV_BLOCK0 + 2 * N_PAIRS + p)),
                  pl.BlockSpec((rows, 2 * LANES), lambda p, i: (i, p)),
                  blk],
        out_specs=[blk, col_blk, col_blk],
        out_shape=[jax.ShapeDtypeStruct((s, D_SB), BF16), out, out],
        compiler_params=_cparams(("parallel", "arbitrary")),
    )(qkv, qkv, qkv, t_tot, do)


FFN_TILE = 256
FFN_COLS = 256
N_FF_BLOCKS = D_FF // FFN_COLS


def _ffn_act_fwd(up0, conv_w, conv_b, *, name):
    s = up0.shape[0]
    nt = s // FFN_TILE

    def body(xg_ref, xv_ref, wg_ref, wv_ref, bg_ref, bv_ref, act_ref, pg_ref, pv_ref):
        pg_ref[0:FFN_HALO, :] = jnp.zeros((FFN_HALO, FFN_COLS), F32)
        pv_ref[0:FFN_HALO, :] = jnp.zeros((FFN_HALO, FFN_COLS), F32)
        pg_ref[FFN_HALO:, :] = xg_ref[...].astype(F32)
        pv_ref[FFN_HALO:, :] = xv_ref[...].astype(F32)

        def tile(i, c):
            t0 = pl.multiple_of(i * FFN_TILE, FFN_TILE)
            outs = []
            for p_ref, w_ref, b_ref in ((pg_ref, wg_ref, bg_ref), (pv_ref, wv_ref, bv_ref)):
                window = p_ref[pl.ds(t0, FFN_TILE + FFN_HALO), :]
                acc = b_ref[...] + w_ref[2:3, :] * window[FFN_HALO:, :]
                for j in range(1, FFN_K):
                    acc = acc + w_ref[FFN_K - 1 - j:FFN_K - j, :] * _shift_down(window, j, FFN_HALO)
                outs.append(acc)
            gate, val = outs
            act_ref[pl.ds(t0, FFN_TILE), :] = (gate * _sigmoid(gate) * val).astype(BF16)
            return c

        lax.fori_loop(0, nt, tile, 0)

    gcol = lambda rows: pl.BlockSpec((rows, FFN_COLS), lambda j: (0, j))
    vcol = lambda rows: pl.BlockSpec((rows, FFN_COLS), lambda j: (0, j + N_FF_BLOCKS))
    return pl.pallas_call(
        body,
        name=name,
        grid=(N_FF_BLOCKS,),
        in_specs=[gcol(s), vcol(s), gcol(FFN_K), vcol(FFN_K), gcol(1), vcol(1)],
        out_specs=gcol(s),
        out_shape=jax.ShapeDtypeStruct((s, D_FF), BF16),
        scratch_shapes=[pltpu.VMEM((s + FFN_HALO, FFN_COLS), F32), pltpu.VMEM((s + FFN_HALO, FFN_COLS), F32)],
        compiler_params=_cparams(("parallel",)),
    )(up0, up0, conv_w, conv_w, conv_b, conv_b)


def _ffn_act_bwd(up0, dact, conv_w, conv_b, *, name):
    s = up0.shape[0]
    nt = s // FFN_TILE

    def body(xg_ref, xv_ref, da_ref, wg_ref, wv_ref, bg_ref, bv_ref, dxg_ref, dxv_ref, dwg_ref, dwv_ref, dbg_ref, dbv_ref,
             pg_ref, pv_ref, dg_ref, dv_ref):
        zeros = jnp.zeros((FFN_HALO, FFN_COLS), F32)
        for p_ref, x_ref in ((pg_ref, xg_ref), (pv_ref, xv_ref)):
            p_ref[0:FFN_HALO, :] = zeros
            p_ref[FFN_HALO:, :] = x_ref[...].astype(F32)
        dg_ref[s:s + FFN_HALO, :] = zeros
        dv_ref[s:s + FFN_HALO, :] = zeros
        for ref in (dwg_ref, dwv_ref, dbg_ref, dbv_ref):
            ref[...] = jnp.zeros_like(ref)

        def conv(p_ref, w_ref, b_ref, t0):
            window = p_ref[pl.ds(t0, FFN_TILE + FFN_HALO), :]
            taps = [_shift_down(window, j, FFN_HALO) for j in range(FFN_K)]
            out = b_ref[...]
            for j in range(FFN_K):
                out = out + w_ref[FFN_K - 1 - j:FFN_K - j, :] * taps[j]
            return out, taps

        def tile(i, c):
            t0 = pl.multiple_of(i * FFN_TILE, FFN_TILE)
            gate, taps_g = conv(pg_ref, wg_ref, bg_ref, t0)
            val, taps_v = conv(pv_ref, wv_ref, bv_ref, t0)
            da = da_ref[pl.ds(t0, FFN_TILE), :].astype(F32)
            sg = lax.logistic(gate)
            dgate = da * val * (sg * (1.0 + gate * (1.0 - sg)))
            dval = da * gate * sg
            dg_ref[pl.ds(t0, FFN_TILE), :] = dgate
            dv_ref[pl.ds(t0, FFN_TILE), :] = dval
            dbg_ref[...] += jnp.sum(dgate, axis=0, keepdims=True)
            dbv_ref[...] += jnp.sum(dval, axis=0, keepdims=True)
            for j in range(FFN_K):
                dwg_ref[FFN_K - 1 - j:FFN_K - j, :] += jnp.sum(dgate * taps_g[j], axis=0, keepdims=True)
                dwv_ref[FFN_K - 1 - j:FFN_K - j, :] += jnp.sum(dval * taps_v[j], axis=0, keepdims=True)
            return c

        lax.fori_loop(0, nt, tile, 0)

        def tile_dx(i, c):
            t0 = pl.multiple_of(i * FFN_TILE, FFN_TILE)
            for d_ref, w_ref, dx_ref in ((dg_ref, wg_ref, dxg_ref), (dv_ref, wv_ref, dxv_ref)):
                window = d_ref[pl.ds(t0, FFN_TILE + FFN_HALO), :]
                dx = w_ref[FFN_K - 1:FFN_K, :] * window[:FFN_TILE, :]
                for j in range(1, FFN_K):
                    dx = dx + w_ref[FFN_K - 1 - j:FFN_K - j, :] * _shift_up(window, j, FFN_TILE)
                dx_ref[pl.ds(t0, FFN_TILE), :] = dx.astype(BF16)
            return c

        lax.fori_loop(0, nt, tile_dx, 0)

    gcol = lambda rows: pl.BlockSpec((rows, FFN_COLS), lambda j: (0, j))
    vcol = lambda rows: pl.BlockSpec((rows, FFN_COLS), lambda j: (0, j + N_FF_BLOCKS))
    half = lambda rows, dtype: jax.ShapeDtypeStruct((rows, D_FF), dtype)
    padded = pltpu.VMEM((s + FFN_HALO, FFN_COLS), F32)
    return pl.pallas_call(
        body,
        name=name,
        grid=(N_FF_BLOCKS,),
        in_specs=[gcol(s), vcol(s), gcol(s), gcol(FFN_K), vcol(FFN_K), gcol(1), vcol(1)],
        out_specs=[gcol(s), gcol(s), gcol(FFN_K), gcol(FFN_K), gcol(1), gcol(1)],
        out_shape=[half(s, BF16), half(s, BF16), half(FFN_K, F32), half(FFN_K, F32), half(1, F32), half(1, F32)],
        scratch_shapes=[padded, padded, padded, padded],
        compiler_params=_cparams(("parallel",)),
    )(up0, up0, dact, conv_w, conv_w, conv_b, conv_b)


MESH = pl.DeviceIdType.MESH


def _position():
    x, y, c = lax.axis_index("x"), lax.axis_index("y"), lax.axis_index("c")
    return x, y, c, 4 * x + 2 * y + c


def _peer(k):
    x, y, c, _ = _position()
    px = 1 - x if k & 4 else x
    py = 1 - y if k & 2 else y
    pc = 1 - c if k & 1 else c
    return (px, py, pc), 4 * px + 2 * py + pc


_HBM = pl.BlockSpec(memory_space=pltpu.HBM)
_SEM = pl.BlockSpec(memory_space=pltpu.SEMAPHORE)
_DATAFLOW = pltpu.SideEffectType.DATAFLOW_SIDE_EFFECTING
N_PEERS = N_DEV - 1


class _SplitExchange:
    def __init__(self, src, *, kind, name):
        self.kind, self.name, self.dtype = kind, name, src.dtype
        scatter = kind.startswith("scatter")
        by_blocks = kind.endswith("blocks")
        self.scatter, self.by_blocks = scatter, by_blocks
        if by_blocks:
            self.r, self.cols, self.land_shape = None, None, src.shape if scatter else (N_DEV,) + src.shape
        else:
            self.r = src.shape[0] // N_DEV if scatter else src.shape[0]
            self.cols = src.shape[1]
            self.land_shape = (N_DEV, self.r, self.cols) if scatter else (N_DEV * self.r, self.cols)
        r = self.r

        def copies(src_ref, land_ref, send_sems, recv_sems, local_sem):
            me = _position()[3]

            def rows(ref, idx):
                return ref.at[pl.ds(pl.multiple_of(idx * r, r), r), :]

            if not scatter:
                outgoing = lambda idx: src_ref
            else:
                outgoing = (lambda idx: src_ref.at[idx]) if by_blocks else (lambda idx: rows(src_ref, idx))
            slot = (lambda idx: land_ref.at[idx]) if (scatter or by_blocks) else (lambda idx: rows(land_ref, idx))
            sends, recvs = [], []
            for k in range(1, N_DEV):
                peer, pidx = _peer(k)
                sems = dict(send_sem=send_sems[k - 1], recv_sem=recv_sems[k - 1], device_id=peer, device_id_type=MESH)
                sends.append(pltpu.make_async_remote_copy(src_ref=outgoing(pidx), dst_ref=slot(me), **sems))
                recvs.append(pltpu.make_async_remote_copy(src_ref=outgoing(pidx), dst_ref=slot(pidx), **sems))
            return sends, recvs, pltpu.make_async_copy(outgoing(me), slot(me), local_sem)

        self._copies = copies
        self.src = src

    @staticmethod
    def start(exchanges, name):
        n = len(exchanges)
        per = 2 * N_PEERS + 1

        def start_body(*refs):
            outs = refs[2 * n:]
            for i, ex in enumerate(exchanges):
                sems = outs[per * i:per * (i + 1)]
                sends, _, local = ex._copies(refs[2 * i], refs[2 * i + 1], sems[:N_PEERS], sems[N_PEERS:-1], sems[-1])
                for cp in sends + [local]:
                    cp.start()
            outs[-1][...] = jnp.zeros_like(outs[-1])

        sem = pltpu.SemaphoreType.DMA(())
        operands, thru_shapes = [], []
        for ex in exchanges:
            operands += [pltpu.with_memory_space_constraint(ex.src, pltpu.HBM),
                         pltpu.with_memory_space_constraint(lax.empty(ex.land_shape, ex.dtype), pltpu.HBM)]
            thru_shapes += [pltpu.HBM(ex.src.shape, ex.dtype), pltpu.HBM(ex.land_shape, ex.dtype)]
        out = pl.pallas_call(
            start_body,
            name=name,
            in_specs=(_HBM,) * (2 * n),
            out_specs=(_SEM,) * (per * n) + (_HBM,) * (2 * n) + (pl.BlockSpec(memory_space=pltpu.VMEM),),
            out_shape=(sem,) * (per * n) + tuple(thru_shapes) + (jax.ShapeDtypeStruct((8, LANES), F32),),
            input_output_aliases={i: per * n + i for i in range(2 * n)},
            compiler_params=pltpu.CompilerParams(has_side_effects=_DATAFLOW),
        )(*operands)
        for i, ex in enumerate(exchanges):
            ex.sems = out[per * i:per * (i + 1)]
            ex.src_thru, ex.land_thru = out[per * n + 2 * i], out[per * n + 2 * i + 1]
        return out[-1][0, 0]

    def finish(self, after):
        copies = self._copies

        def wait_body(src_ref, land_ref, *rest):
            sends, recvs, local = copies(src_ref, land_ref, rest[:N_PEERS], rest[N_PEERS:2 * N_PEERS], rest[2 * N_PEERS])
            for cp in sends:
                cp.wait_send()
            for cp in recvs:
                cp.wait_recv()
            local.wait()

        return pl.pallas_call(
            wait_body,
            name=f"{self.name}_wait",
            in_specs=(_HBM, _HBM) + (_SEM,) * (2 * N_PEERS + 1) + (pl.BlockSpec(memory_space=pl.ANY),),
            out_specs=(_HBM, _HBM),
            out_shape=(pltpu.HBM(self.src_thru.shape, self.dtype), pltpu.HBM(self.land_shape, self.dtype)),
            input_output_aliases={0: 0, 1: 1},
            compiler_params=pltpu.CompilerParams(has_side_effects=_DATAFLOW),
        )(self.src_thru, self.land_thru, *self.sems, after)[1]


def _row_tile(rows):
    return _tile(rows, (256, 128, 64, 32, 16, 8))


def _layer_parts_specs(n_layers, n_parts, tr, cols):
    return [pl.BlockSpec((n_parts, tr, cols), lambda l, i, j=j: (0, jnp.where(l == j, i, 0), 0)) for j in range(n_layers)]


def _select_layer_sum(p_refs):
    l = pl.program_id(0)
    g = None
    for j, p_ref in enumerate(p_refs):
        gj = p_ref[0].astype(F32)
        for k in range(1, p_ref.shape[0]):
            gj = gj + p_ref[k].astype(F32)
        g = gj if g is None else jnp.where(l == j, gj, g)
    return g


def _adamw(parts, w, m, v, *, name):
    n_layers, rows, cols = w.shape
    tr = _row_tile(rows)

    def body(*refs):
        w_ref, m_ref, v_ref, g_ref, d_ref, m2_ref, v2_ref = refs[n_layers:]
        g = _select_layer_sum(refs[:n_layers])
        m2 = ADAM_B1 * m_ref[...] + (1.0 - ADAM_B1) * g
        v2 = ADAM_B2 * v_ref[...] + (1.0 - ADAM_B2) * (g * g)
        m_hat = m2 / (1.0 - ADAM_B1 ** ADAM_STEP)
        v_hat = v2 / (1.0 - ADAM_B2 ** ADAM_STEP)
        g_ref[...] = g
        d_ref[...] = -ADAM_LR * (m_hat / (jnp.sqrt(v_hat) + ADAM_EPS) + ADAM_WD * w_ref[...])
        m2_ref[...] = m2
        v2_ref[...] = v2

    slab = pl.BlockSpec((None, tr, cols), lambda l, i: (l, i, 0))
    out = jax.ShapeDtypeStruct((n_layers, rows, cols), F32)
    p_specs = _layer_parts_specs(n_layers, parts[0].shape[0], tr, cols)
    return pl.pallas_call(
        body,
        name=name,
        grid=(n_layers, rows // tr),
        in_specs=p_specs + [slab, slab, slab],
        out_specs=[slab, slab, slab, slab],
        out_shape=[out, out, out, out],
        compiler_params=_cparams(("arbitrary", "arbitrary")),
    )(*parts, w, m, v)


SLAB_ROWS = 32
_SMALL_SHARDED = (("conv_w", (2, 31, 32)), ("ffn_conv_w", (2, 3, 704)))
_REPLICATED = (("g_mix", (2, 1024)), ("conv_b", (2, 256)), ("conv_ln_g", (2, 256)), ("conv_ln_b", (2, 256)),
               ("sgu_ln_g", (2, 256)), ("sgu_ln_b", (2, 256)), ("sgu_w", (2, 4, 128, 128)), ("sgu_b", (2, 4, 128)),
               ("g_out", (2, 1024)), ("g_ffn", (2, 1024)), ("ffn_conv_b", (2, 5632)), ("g_final", (1024,)))


def _seg_rows(n_elems):
    return -(-n_elems // LANES)


def _pack(arrays, lead=()):
    segs = []
    for a in arrays:
        flat = a.reshape(lead + (-1,)).astype(F32)
        pad = _seg_rows(flat.shape[-1]) * LANES - flat.shape[-1]
        if pad:
            flat = jnp.pad(flat, [(0, 0)] * len(lead) + [(0, pad)])
        segs.append(flat)
    flat = jnp.concatenate(segs, axis=-1)
    rows = flat.shape[-1] // LANES
    pad_rows = -rows % SLAB_ROWS
    if pad_rows:
        flat = jnp.pad(flat, [(0, 0)] * len(lead) + [(0, pad_rows * LANES)])
    return flat.reshape(lead + (rows + pad_rows, LANES))


def _unpack(slab, shapes, lead=()):
    flat = slab.reshape(lead + (-1,))
    out, off = [], 0
    for shape in shapes:
        n = math.prod(shape)
        out.append(flat[..., off:off + n].reshape(lead + tuple(shape)))
        off += _seg_rows(n) * LANES
    return out


def _split_last(full):
    split = full.shape[:-1] + (N_DEV, full.shape[-1] // N_DEV)
    return jnp.moveaxis(full.reshape(split), -2, 0)


def _join_last(blocks):
    moved = jnp.moveaxis(blocks, 0, -2)
    return moved.reshape(moved.shape[:-2] + (moved.shape[-2] * moved.shape[-1],))


def _gathered(wt, n, l, after):
    if isinstance(wt[n][l], _SplitExchange):
        wt[n][l] = wt[n][l].finish(after)
    return wt[n][l]


def _layer_fwd(l, x, wt, small):
    tag = f"l{l}"
    h = _rmsnorm_fwd(x, small["g_mix"][l][None], name=f"{tag}_norm_mix")
    w_in_t = _gathered(wt, "w_in_t", l, h)
    p = _matmul(h, w_in_t, "nt", name=f"{tag}_proj", out_dtype=BF16)
    p_ab = qkv = p
    ya = _mixer_a_fwd(p_ab, wt["conv_w"][l], small["conv_b"][l][None], small["conv_ln_g"][l][None],
                      small["conv_ln_b"][l][None], name=f"{tag}_mixer_a")
    bias = jnp.repeat(small["sgu_b"][l].T, HEAD_DIM, axis=1)
    yb = _sgu_fwd(p_ab, small["sgu_ln_g"][l][None], small["sgu_ln_b"][l][None], small["sgu_w"][l], bias,
                  name=f"{tag}_sgu")
    yc, t_tot = _attn_fwd(qkv, name=f"{tag}_attn")
    y = _combine_fwd(ya, yb, yc, small["g_out"][l][None], name=f"{tag}_combine")
    x1 = _matmul(y, _gathered(wt, "w_out", l, y), "nn", name=f"{tag}_out_proj", residual=x)
    h2 = _rmsnorm_fwd(x1, small["g_ffn"][l][None], name=f"{tag}_norm_ffn")
    up0 = _matmul(h2, _gathered(wt, "w_up_t", l, h2), "nt", name=f"{tag}_up", out_dtype=BF16)
    act = _ffn_act_fwd(up0, wt["ffn_conv_w"][l], small["ffn_conv_b"][l][None], name=f"{tag}_ffn_act")
    x2 = _matmul(act, _gathered(wt, "w_down", l, act), "nn", name=f"{tag}_down", residual=x1)
    saved = dict(x=x, h=h, p_ab=p, qkv=p, ya=ya, yb=yb, yc=yc, t_tot=t_tot, y=y, x1=x1, h2=h2, up0=up0,
                 act=act, bias=bias)
    return x2, saved


def _layer_bwd(l, dres, sv, wt, small, scattering, token):
    tag = f"l{l}b"
    g = {}

    def scatter(n, partial):
        scattering[n][l] = _SplitExchange(partial, kind="scatter_rows", name=f"scatter_{n}_l{l}")
        return _SplitExchange.start([scattering[n][l]], name=f"scatter_{n}_l{l}_start")

    dx2, dx2_b = dres
    dact = _matmul(dx2_b, wt["w_down"][l], "nt", name=f"{tag}_dact", out_dtype=BF16)
    tok = scatter("w_down", _matmul(sv["act"], dx2_b, "tn", name=f"{tag}_dw_down", out_dtype=BF16))
    dup_g, dup_v, dwg, dwv, dbg, dbv = _ffn_act_bwd(sv["up0"], dact, wt["ffn_conv_w"][l], small["ffn_conv_b"][l][None] + tok + token,
                                                    name=f"{tag}_ffn_act")
    g["ffn_conv_w"] = jnp.concatenate([dwg, dwv], axis=1)
    g["ffn_conv_b"] = jnp.concatenate([dbg[0], dbv[0]])
    dh2 = _matmul(dup_g, wt["w_up_t"][l], "nn", name=f"{tag}_dh2_gate")
    dh2 = _matmul(dup_v, wt["w_up_t"][l], "nn", name=f"{tag}_dh2_val", b_k0=D_FF, residual=dh2)
    dw_up = _matmul(dup_g, sv["h2"], "tn", name=f"{tag}_dw_up_gate", out_dtype=BF16, rows=(2 * D_FF, 0))
    dw_up = _matmul(dup_v, sv["h2"], "tn", name=f"{tag}_dw_up_val", out_dtype=BF16, rows=(2 * D_FF, D_FF), into=dw_up)
    tok = scatter("w_up_t", dw_up)
    dx1, dx1_b, dg = _rmsnorm_bwd(sv["x1"], small["g_ffn"][l][None] + tok, dh2, dx2, name=f"{tag}_norm_ffn")
    g["g_ffn"] = dg[0]
    dy = _matmul(dx1_b, wt["w_out"][l], "nt", name=f"{tag}_dy")
    tok = scatter("w_out", _matmul(sv["y"], dx1_b, "tn", name=f"{tag}_dw_out", out_dtype=BF16))
    dya, dyb, dyc, dg = _combine_bwd(dy, sv["ya"], sv["yb"], sv["yc"], small["g_out"][l][None] + tok,
                                     name=f"{tag}_combine")
    g["g_out"] = dg[0]
    dq, dk, dv = _attn_bwd(sv["qkv"], sv["t_tot"], dyc, name=f"{tag}_attn")
    dp_b, g["sgu_w"], db, dg, dbeta = _sgu_bwd(sv["p_ab"], dyb, small["sgu_ln_g"][l][None], small["sgu_ln_b"][l][None],
                                               small["sgu_w"][l], sv["bias"], name=f"{tag}_sgu")
    g["sgu_b"] = db[:, :N_SGU_HEADS].T
    g["sgu_ln_g"], g["sgu_ln_b"] = dg[0], dbeta[0]
    dp_a, g["conv_w"], dcb, dg, dbeta = _mixer_a_bwd(sv["p_ab"], dya, wt["conv_w"][l], small["conv_b"][l][None],
                                                     small["conv_ln_g"][l][None], small["conv_ln_b"][l][None],
                                                     name=f"{tag}_mixer_a")
    g["conv_b"], g["conv_ln_g"], g["conv_ln_b"] = dcb[0], dg[0], dbeta[0]
    dp = jnp.concatenate([dp_a, dp_b, dq, dk.astype(BF16), dv.astype(BF16)], axis=1)
    tok = scatter("w_in_t", _matmul(dp, sv["h"], "tn", name=f"{tag}_dw_in", out_dtype=BF16))
    dh = _matmul(dp, wt["w_in_t"][l], "nn", name=f"{tag}_dh")
    dx, dx_b, dg = _rmsnorm_bwd(sv["x"], small["g_mix"][l][None] + tok, dh, dx1, name=f"{tag}_norm_mix")
    g["g_mix"] = dg[0]
    return (dx, dx_b), g


_BIG = ("w_in_t", "w_out", "w_up_t", "w_down")


def kernel(x, g_mix, w_in, conv_w, conv_b, conv_ln_g, conv_ln_b, sgu_ln_g, sgu_ln_b, sgu_w, sgu_b, g_out, w_out, g_ffn, w_up, ffn_conv_w, ffn_conv_b, w_down, g_final, loss_target, m_g_mix, m_w_in, m_conv_w, m_conv_b, m_conv_ln_g, m_conv_ln_b, m_sgu_ln_g, m_sgu_ln_b, m_sgu_w, m_sgu_b, m_g_out, m_w_out, m_g_ffn, m_w_up, m_ffn_conv_w, m_ffn_conv_b, m_w_down, m_g_final, v_g_mix, v_w_in, v_conv_w, v_conv_b, v_conv_ln_g, v_conv_ln_b, v_sgu_ln_g, v_sgu_ln_b, v_sgu_w, v_sgu_b, v_g_out, v_w_out, v_g_ffn, v_w_up, v_ffn_conv_w, v_ffn_conv_b, v_w_down, v_g_final):
    given = dict(locals())
    n_layers = g_mix.shape[0]
    layers = range(n_layers)
    small_sharded = [n for n, _ in _SMALL_SHARDED]
    replicated = [n for n, _ in _REPLICATED]
    small = {n: given[n] for n in replicated}

    filters = _SplitExchange(_pack([given[n] for n in small_sharded]), kind="gather_blocks", name="gather_filters")
    wt = {n: [None] * n_layers for n in _BIG}
    wt["w_in_t"][0] = _SplitExchange(w_in[0].T.astype(BF16), kind="gather_rows", name="gather_w_in_t_l0")
    tok = _SplitExchange.start([filters, wt["w_in_t"][0]], name="gather_first_start")
    w_in, w_out, w_up, w_down, tok = lax.optimization_barrier((w_in, w_out, w_up, w_down, tok))
    shard = {"w_in_t": [w_in[l].T.astype(BF16) for l in layers], "w_out": [w_out[l].astype(BF16) for l in layers],
             "w_up_t": [w_up[l].T.astype(BF16) for l in layers], "w_down": [w_down[l].astype(BF16) for l in layers]}
    later = [(n, l) for l in layers for n in _BIG if (n, l) != ("w_in_t", 0)]
    for n, l in later:
        wt[n][l] = _SplitExchange(shard[n][l], kind="gather_rows", name=f"gather_{n}_l{l}")
    small["g_mix"] = g_mix + tok + _SplitExchange.start([wt[n][l] for n, l in later], name="gather_weights_start")
    gathered_filters = filters.finish(small["g_mix"])
    for n, blocks in zip(small_sharded, _unpack(gathered_filters, [s for _, s in _SMALL_SHARDED], lead=(N_DEV,))):
        wt[n] = _join_last(blocks)

    xs = x[0]
    saved = []
    for l in layers:
        xs, sv = _layer_fwd(l, xs, wt, small)
        saved.append(sv)
    loss_tile, dx, dx_b, dgf = _loss_head(xs, g_final[None], loss_target[0], name="loss_head")
    dres = (dx, dx_b)
    scattering = {n: [None] * n_layers for n in _BIG}
    layered = [n for n in replicated if n not in ("g_final", "sgu_w")]
    slabs, sgu_w_parts = [None] * n_layers, [None] * n_layers
    tok = 0.0
    for l in reversed(layers):
        dres, g = _layer_bwd(l, dres, saved[l], wt, small, scattering, tok)
        own = _pack([_split_last(g[n]) for n in small_sharded], lead=(N_DEV,))
        shared = _pack([g[n] for n in layered] + [dgf[0], loss_tile[0]])
        slab = jnp.concatenate([own, jnp.broadcast_to(shared[None], (N_DEV,) + shared.shape)], axis=1)
        slabs[l] = _SplitExchange(slab, kind="scatter_blocks", name=f"scatter_small_grads_l{l}")
        sgu_w_parts[l] = _SplitExchange(g["sgu_w"].reshape(-1, LANES).astype(BF16), kind="gather_blocks",
                                        name=f"gather_sgu_w_grads_l{l}")
        tok = _SplitExchange.start([slabs[l], sgu_w_parts[l]], name=f"small_grads_l{l}_start")
    n_own = own.shape[1]

    after_backward = jnp.full((8, LANES), tok)
    received = {n: [scattering[n][l].finish(after_backward) for l in layers] for n in _BIG}
    out = {}

    def update(n, parts, transposed=False):
        turn = (lambda a: jnp.swapaxes(a, 1, 2)) if transposed else (lambda a: a)
        results = _adamw(parts, turn(given[n]), turn(given["m_" + n]), turn(given["v_" + n]), name=f"adamw_{n}")
        for pre, res in zip(("grad_", "delta_", "new_m_", "new_v_"), results):
            out[pre + n] = turn(res)
        return results[0][0, :8, :LANES]

    update("w_out", received["w_out"])
    update("w_down", received["w_down"])
    update("w_in", received["w_in_t"], transposed=True)
    big_updated = update("w_up", received["w_up_t"], transposed=True)

    as_rows = lambda a: a.reshape(n_layers, -1, LANES)
    results = _adamw([ex.finish(big_updated) for ex in sgu_w_parts], as_rows(sgu_w), as_rows(m_sgu_w), as_rows(v_sgu_w),
                     name="adamw_sgu_w")
    for pre, res in zip(("grad_", "delta_", "new_m_", "new_v_"), results):
        out[pre + "sgu_w"] = res.reshape(sgu_w.shape)

    per_layer_g_final = {pre: jnp.broadcast_to(given[pre + "g_final"], (n_layers,) + g_final.shape) for pre in ("", "m_", "v_")}
    loss_row = jnp.zeros((n_layers, LANES), F32)
    stacks = [jnp.concatenate([_pack([given[pre + n] for n in small_sharded], lead=(n_layers,)),
                               _pack([given[pre + n] for n in layered] + [per_layer_g_final[pre], loss_row], lead=(n_layers,))],
                              axis=1) for pre in ("", "m_", "v_")]
    results = _adamw([slabs[l].finish(big_updated) for l in layers], *stacks, name="adamw_small")
    for pre, res in zip(("grad_", "delta_", "new_m_", "new_v_"), results):
        unpacked = (_unpack(res[:, :n_own], [s[1:] for _, s in _SMALL_SHARDED], lead=(n_layers,))
                    + _unpack(res[:, n_own:], [s[1:] for n, s in _REPLICATED if n in layered] + [g_final.shape, (LANES,)],
                              lead=(n_layers,)))
        for n, a in zip(small_sharded + layered + ["g_final", "loss"], unpacked):
            out[pre + n] = a[0] if n in ("g_final", "loss") else a

    order = list(_WEIGHT_ORDER)
    return (out["grad_loss"][0], dres[0][None], *[out["grad_" + n] for n in order], *[out["delta_" + n] for n in order],
            *[out["new_m_" + n] for n in order], *[out["new_v_" + n] for n in order])


_WEIGHT_ORDER = ("g_mix", "w_in", "conv_w", "conv_b", "conv_ln_g", "conv_ln_b", "sgu_ln_g", "sgu_ln_b", "sgu_w", "sgu_b",
                 "g_out", "w_out", "g_ffn", "w_up", "ffn_conv_w", "ffn_conv_b", "w_down", "g_final")
```

```python
import math

import jax
import jax.numpy as jnp
from jax import lax
from jax.experimental import pallas as pl
from jax.experimental.pallas import tpu as pltpu

F32 = jnp.float32
BF16 = jnp.bfloat16

N_DEV = 8
D_MODEL = 1024
HEAD_DIM = 64
D_CONV = 256
D_SGU = 256
D_SB = 512
D_AB = 2 * D_CONV + 2 * D_SGU
CONV_K = 31
CONV_HALO = 32
FFN_K = 3
FFN_HALO = 8
D_FF = 2816
CHUNK = 128
EPS = 1e-6
LANES = 128

ADAM_LR = 0.001
ADAM_B1 = 0.9
ADAM_B2 = 0.999
ADAM_EPS = 1e-08
ADAM_WD = 0.01
ADAM_STEP = 10

VMEM_LIMIT = 56 * 1024 * 1024


def _cparams(sem=None):
    return pltpu.CompilerParams(dimension_semantics=sem, vmem_limit_bytes=VMEM_LIMIT)


def _sigmoid(x):
    return 1.0 / (1.0 + jnp.exp(-x))


_INV_SQRT2 = 1.0 / math.sqrt(2.0)
_INV_SQRT2PI = 1.0 / math.sqrt(2.0 * math.pi)


def _gelu(x):
    return 0.5 * x * (1.0 + lax.erf(x * _INV_SQRT2))


def _gelu_grad(x):
    return 0.5 * (1.0 + lax.erf(x * _INV_SQRT2)) + x * jnp.exp(-0.5 * x * x) * _INV_SQRT2PI


def _dot(a, b, dims):
    return lax.dot_general(a, b, (dims, ((), ())), preferred_element_type=F32)


_NN = ((1,), (0,))
_NT = ((1,), (1,))
_TN = ((0,), (0,))


def _split_bf16(x):
    hi = x.astype(BF16)
    lo = (x - hi.astype(F32)).astype(BF16)
    return jnp.concatenate([hi, lo], axis=1)


def _matmul(a, b, mode, *, name, out_dtype=F32, residual=None, n=None, b_n0=0, b_k0=0, rows=None, into=None):
    if mode == "nn":
        (m, k), n = a.shape, (n or b.shape[1])
    elif mode == "nt":
        (m, k), n = a.shape, (n or b.shape[0])
    else:
        (k, m), n = a.shape, b.shape[1]
    has_res = residual is not None
    tm, tn = _matmul_tiles(m, n, k, a.dtype.itemsize, b.dtype.itemsize, jnp.dtype(out_dtype).itemsize, has_res, b_n0)
    j0 = b_n0 // tn
    total_rows, first_row = rows or (m, 0)
    assert b_k0 % k == 0 and first_row % tm == 0
    kb, i0 = b_k0 // k, first_row // tm

    if mode == "nn":
        a_spec = pl.BlockSpec((tm, k), lambda i, j: (i, 0))
        b_spec = pl.BlockSpec((k, tn), lambda i, j: (kb, j + j0))
        dims = _NN
    elif mode == "nt":
        a_spec = pl.BlockSpec((tm, k), lambda i, j: (i, 0))
        b_spec = pl.BlockSpec((tn, k), lambda i, j: (j + j0, 0))
        dims = _NT
    else:
        a_spec = pl.BlockSpec((k, tm), lambda i, j: (0, i))
        b_spec = pl.BlockSpec((k, tn), lambda i, j: (0, j))
        dims = _TN
    o_spec = pl.BlockSpec((tm, tn), lambda i, j: (i + i0, j))
    r_spec = pl.BlockSpec((tm, tn), lambda i, j: (i, j))

    def body(*refs):
        a_ref, b_ref = refs[:2]
        acc = _dot(a_ref[...].astype(BF16), b_ref[...].astype(BF16), dims)
        if has_res:
            acc = acc + refs[2][...]
        refs[-1][...] = acc.astype(out_dtype)

    in_specs = [a_spec, b_spec] + ([r_spec] if has_res else [])
    args = (a, b) + ((residual,) if has_res else ())
    aliases = {}
    if into is not None:
        aliases = {len(args): 0}
        in_specs.append(pl.BlockSpec(memory_space=pl.ANY))
        args += (into,)

        def body(*refs, inner=body):
            inner(*refs[:len(args) - 1], refs[-1])

    return pl.pallas_call(
        body,
        name=name,
        grid=(m // tm, n // tn),
        in_specs=in_specs,
        out_specs=o_spec,
        out_shape=jax.ShapeDtypeStruct((total_rows, n), out_dtype),
        input_output_aliases=aliases,
        compiler_params=_cparams(("parallel", "parallel")),
    )(*args)


MATMUL_VMEM_BUDGET = 40 * 1024 * 1024


def _matmul_tiles(m, n, k, a_bytes, b_bytes, out_bytes, has_res, n_offset):
    def divisors(size, cap, also=0):
        return [t for t in range(cap, 0, -LANES) if size % t == 0 and also % t == 0] or [size]

    for tm in divisors(m, 1024):
        for tn in divisors(n, 1408, n_offset):
            blocks = tm * k * a_bytes + k * tn * b_bytes + tm * tn * (out_bytes + (4 if has_res else 0))
            if 2 * blocks <= MATMUL_VMEM_BUDGET:
                return tm, tn
    raise ValueError(f"no matmul tiling for {m} x {n} x {k}")


ROW_TILE = 512


def _rmsnorm_fwd(x, g, *, name):
    s, d = x.shape

    def body(x_ref, g_ref, h_ref):
        xv = x_ref[...]
        r = lax.rsqrt(jnp.mean(xv * xv, axis=-1, keepdims=True) + EPS)
        h_ref[...] = (xv * r * g_ref[...]).astype(BF16)

    return pl.pallas_call(
        body,
        name=name,
        grid=(s // ROW_TILE,),
        in_specs=[pl.BlockSpec((ROW_TILE, d), lambda i: (i, 0)), pl.BlockSpec((1, d), lambda i: (0, 0))],
        out_specs=pl.BlockSpec((ROW_TILE, d), lambda i: (i, 0)),
        out_shape=jax.ShapeDtypeStruct((s, d), BF16),
        compiler_params=_cparams(("parallel",)),
    )(x, g)


def _rmsnorm_bwd(x, g, dh, dres, *, name):
    s, d = x.shape

    def body(x_ref, g_ref, dh_ref, dres_ref, dx_ref, dxb_ref, dg_ref):
        xv = x_ref[...]
        r = lax.rsqrt(jnp.mean(xv * xv, axis=-1, keepdims=True) + EPS)
        xhat = xv * r
        dhv = dh_ref[...]
        dxhat = dhv * g_ref[...]
        dx = dres_ref[...] + r * (dxhat - xhat * jnp.mean(dxhat * xhat, axis=-1, keepdims=True))
        dx_ref[...] = dx
        dxb_ref[...] = dx.astype(BF16)
        part = jnp.sum(dhv * xhat, axis=0, keepdims=True)

        @pl.when(pl.program_id(0) == 0)
        def _():
            dg_ref[...] = part

        @pl.when(pl.program_id(0) > 0)
        def _():
            dg_ref[...] += part

    row = pl.BlockSpec((ROW_TILE, d), lambda i: (i, 0))
    vec = pl.BlockSpec((1, d), lambda i: (0, 0))
    return pl.pallas_call(
        body,
        name=name,
        grid=(s // ROW_TILE,),
        in_specs=[row, vec, row, row],
        out_specs=[row, row, vec],
        out_shape=[jax.ShapeDtypeStruct((s, d), F32), jax.ShapeDtypeStruct((s, d), BF16),
                   jax.ShapeDtypeStruct((1, d), F32)],
        compiler_params=_cparams(("arbitrary",)),
    )(x, g, dh, dres)


def _loss_head(x, g, target, *, name):
    s, d = x.shape

    def body(x_ref, g_ref, t_ref, loss_ref, dx_ref, dxb_ref, dg_ref):
        xv = x_ref[...]
        gv = g_ref[...]
        r = lax.rsqrt(jnp.mean(xv * xv, axis=-1, keepdims=True) + EPS)
        xhat = xv * r
        diff = xhat * gv - t_ref[...]
        dy = diff * (1.0 / d)
        dxhat = dy * gv
        dx = r * (dxhat - xhat * jnp.mean(dxhat * xhat, axis=-1, keepdims=True))
        dx_ref[...] = dx
        dxb_ref[...] = dx.astype(BF16)
        dg_part = jnp.sum(dy * xhat, axis=0, keepdims=True)
        row_loss = jnp.sum(diff * diff, axis=-1, keepdims=True)
        loss_part = jnp.sum(row_loss, axis=0, keepdims=True) * (0.5 / d)

        @pl.when(pl.program_id(0) == 0)
        def _():
            dg_ref[...] = dg_part
            loss_ref[...] = jnp.broadcast_to(loss_part, loss_ref.shape)

        @pl.when(pl.program_id(0) > 0)
        def _():
            dg_ref[...] += dg_part
            loss_ref[...] += jnp.broadcast_to(loss_part, loss_ref.shape)

    row = pl.BlockSpec((ROW_TILE, d), lambda i: (i, 0))
    vec = pl.BlockSpec((1, d), lambda i: (0, 0))
    tile = pl.BlockSpec((8, LANES), lambda i: (0, 0))
    return pl.pallas_call(
        body,
        name=name,
        grid=(s // ROW_TILE,),
        in_specs=[row, vec, row],
        out_specs=[tile, row, row, vec],
        out_shape=[jax.ShapeDtypeStruct((8, LANES), F32), jax.ShapeDtypeStruct((s, d), F32),
                   jax.ShapeDtypeStruct((s, d), BF16), jax.ShapeDtypeStruct((1, d), F32)],
        compiler_params=_cparams(("arbitrary",)),
    )(x, g, target)


_BRANCHES = ((0, D_CONV), (D_CONV, D_SGU), (D_CONV + D_SGU, D_SB))


def _combine_fwd(ya, yb, yc, g, *, name):
    s = ya.shape[0]

    def body(ya_ref, yb_ref, yc_ref, g_ref, y_ref):
        for ref, (off, w) in zip((ya_ref, yb_ref, yc_ref), _BRANCHES):
            v = ref[...]
            r = lax.rsqrt(jnp.mean(v * v, axis=-1, keepdims=True) + EPS)
            y_ref[:, off:off + w] = (v * r * g_ref[:, off:off + w]).astype(BF16)

    def row(w):
        return pl.BlockSpec((ROW_TILE, w), lambda i: (i, 0))

    return pl.pallas_call(
        body,
        name=name,
        grid=(s // ROW_TILE,),
        in_specs=[row(D_CONV), row(D_SGU), row(D_SB), pl.BlockSpec((1, D_MODEL), lambda i: (0, 0))],
        out_specs=row(D_MODEL),
        out_shape=jax.ShapeDtypeStruct((s, D_MODEL), BF16),
        compiler_params=_cparams(("parallel",)),
    )(ya, yb, yc, g)


def _combine_bwd(dy, ya, yb, yc, g, *, name):
    s = ya.shape[0]

    def body(dy_ref, ya_ref, yb_ref, yc_ref, g_ref, dya_ref, dyb_ref, dyc_ref, dg_ref):
        first = pl.program_id(0) == 0
        for ref, dref, (off, w) in zip((ya_ref, yb_ref, yc_ref), (dya_ref, dyb_ref, dyc_ref), _BRANCHES):
            v = ref[...]
            r = lax.rsqrt(jnp.mean(v * v, axis=-1, keepdims=True) + EPS)
            n = v * r
            dout = dy_ref[:, off:off + w]
            dn = dout * g_ref[:, off:off + w]
            dref[...] = r * (dn - n * jnp.mean(dn * n, axis=-1, keepdims=True))
            part = jnp.sum(dout * n, axis=0, keepdims=True)

            @pl.when(first)
            def _():
                dg_ref[:, off:off + w] = part

            @pl.when(jnp.logical_not(first))
            def _():
                dg_ref[:, off:off + w] += part

    def row(w):
        return pl.BlockSpec((ROW_TILE, w), lambda i: (i, 0))

    vec = pl.BlockSpec((1, D_MODEL), lambda i: (0, 0))
    return pl.pallas_call(
        body,
        name=name,
        grid=(s // ROW_TILE,),
        in_specs=[row(D_MODEL), row(D_CONV), row(D_SGU), row(D_SB), vec],
        out_specs=[row(D_CONV), row(D_SGU), row(D_SB), vec],
        out_shape=[jax.ShapeDtypeStruct((s, D_CONV), F32), jax.ShapeDtypeStruct((s, D_SGU), F32),
                   jax.ShapeDtypeStruct((s, D_SB), F32), jax.ShapeDtypeStruct((1, D_MODEL), F32)],
        compiler_params=_cparams(("arbitrary",)),
    )(dy, ya, yb, yc, g)


CONV_TILE = 128


def _shift_down(window, j, halo):
    return pltpu.roll(window, j, 0)[halo:, :] if j else window[halo:, :]


def _shift_up(window, j, n_out):
    n = window.shape[0]
    return pltpu.roll(window, n - j, 0)[:n_out, :] if j else window[:n_out, :]


def _mixer_a_fwd(p_ab, conv_w, conv_b, ln_g, ln_b, *, name):
    s = p_ab.shape[0]
    nt = s // CONV_TILE

    def body(p_ref, w_ref, b_ref, g_ref, beta_ref, y_ref, h_ref):
        h_ref[0:CONV_HALO, :] = jnp.zeros((CONV_HALO, D_CONV), F32)

        def glu(i, c):
            t0 = pl.multiple_of(i * CONV_TILE, CONV_TILE)
            a = p_ref[pl.ds(t0, CONV_TILE), 0:D_CONV].astype(F32)
            gate = p_ref[pl.ds(t0, CONV_TILE), D_CONV:2 * D_CONV].astype(F32)
            h_ref[pl.ds(t0 + CONV_HALO, CONV_TILE), :] = a * _sigmoid(gate)
            return c

        lax.fori_loop(0, nt, glu, 0)

        def conv(i, c):
            t0 = pl.multiple_of(i * CONV_TILE, CONV_TILE)
            window = h_ref[pl.ds(t0, CONV_TILE + CONV_HALO), :]
            acc = jnp.zeros((CONV_TILE, D_CONV), F32) + b_ref[...]
            for k in range(CONV_K):
                acc = acc + w_ref[k:k + 1, :] * _shift_down(window, CONV_K - 1 - k, CONV_HALO)
            mu = jnp.mean(acc, axis=-1, keepdims=True)
            xc = acc - mu
            rstd = lax.rsqrt(jnp.mean(xc * xc, axis=-1, keepdims=True) + EPS)
            z = xc * rstd * g_ref[...] + beta_ref[...]
            y_ref[pl.ds(t0, CONV_TILE), :] = z * _sigmoid(z)
            return c

        lax.fori_loop(0, nt, conv, 0)

    full = lambda shape: pl.BlockSpec(shape, lambda i: (0, 0))
    return pl.pallas_call(
        body,
        name=name,
        grid=(1,),
        in_specs=[full((s, 2 * D_CONV)), full((CONV_K, D_CONV)), full((1, D_CONV)), full((1, D_CONV)),
                  full((1, D_CONV))],
        out_specs=full((s, D_CONV)),
        out_shape=jax.ShapeDtypeStruct((s, D_CONV), F32),
        scratch_shapes=[pltpu.VMEM((s + CONV_HALO, D_CONV), F32)],
        compiler_params=_cparams(("arbitrary",)),
    )(p_ab, conv_w, conv_b, ln_g, ln_b)


def _mixer_a_bwd(p_ab, dya, conv_w, conv_b, ln_g, ln_b, *, name):
    s = p_ab.shape[0]
    nt = s // CONV_TILE

    def body(p_ref, dy_ref, w_ref, b_ref, g_ref, beta_ref, dp_ref, dw_ref, db_ref, dg_ref, dbeta_ref, h_ref, dc_ref):
        h_ref[0:CONV_HALO, :] = jnp.zeros((CONV_HALO, D_CONV), F32)
        dc_ref[s:s + CONV_HALO, :] = jnp.zeros((CONV_HALO, D_CONV), F32)
        dw_ref[...] = jnp.zeros_like(dw_ref)
        db_ref[...] = jnp.zeros_like(db_ref)
        dg_ref[...] = jnp.zeros_like(dg_ref)
        dbeta_ref[...] = jnp.zeros_like(dbeta_ref)

        def glu(i, c):
            t0 = pl.multiple_of(i * CONV_TILE, CONV_TILE)
            a = p_ref[pl.ds(t0, CONV_TILE), 0:D_CONV].astype(F32)
            gate = p_ref[pl.ds(t0, CONV_TILE), D_CONV:2 * D_CONV].astype(F32)
            h_ref[pl.ds(t0 + CONV_HALO, CONV_TILE), :] = a * _sigmoid(gate)
            return c

        lax.fori_loop(0, nt, glu, 0)

        def conv_bwd(i, c):
            t0 = pl.multiple_of(i * CONV_TILE, CONV_TILE)
            window = h_ref[pl.ds(t0, CONV_TILE + CONV_HALO), :]
            taps = [_shift_down(window, CONV_K - 1 - k, CONV_HALO) for k in range(CONV_K)]
            acc = jnp.zeros((CONV_TILE, D_CONV), F32) + b_ref[...]
            for k in range(CONV_K):
                acc = acc + w_ref[k:k + 1, :] * taps[k]
            mu = jnp.mean(acc, axis=-1, keepdims=True)
            xc = acc - mu
            rstd = lax.rsqrt(jnp.mean(xc * xc, axis=-1, keepdims=True) + EPS)
            xhat = xc * rstd
            z = xhat * g_ref[...] + beta_ref[...]
            sg = _sigmoid(z)
            dz = dy_ref[pl.ds(t0, CONV_TILE), :] * (sg * (1.0 + z * (1.0 - sg)))
            dg_ref[...] += jnp.sum(dz * xhat, axis=0, keepdims=True)
            dbeta_ref[...] += jnp.sum(dz, axis=0, keepdims=True)
            dxhat = dz * g_ref[...]
            dc = rstd * (dxhat - jnp.mean(dxhat, axis=-1, keepdims=True)
                         - xhat * jnp.mean(dxhat * xhat, axis=-1, keepdims=True))
            dc_ref[pl.ds(t0, CONV_TILE), :] = dc
            db_ref[...] += jnp.sum(dc, axis=0, keepdims=True)
            for k in range(CONV_K):
                dw_ref[k:k + 1, :] += jnp.sum(dc * taps[k], axis=0, keepdims=True)
            return c

        lax.fori_loop(0, nt, conv_bwd, 0)

        def glu_bwd(i, c):
            t0 = pl.multiple_of(i * CONV_TILE, CONV_TILE)
            window = dc_ref[pl.ds(t0, CONV_TILE + CONV_HALO), :]
            dh = jnp.zeros((CONV_TILE, D_CONV), F32)
            for j in range(CONV_K):
                dh = dh + w_ref[CONV_K - 1 - j:CONV_K - j, :] * _shift_up(window, j, CONV_TILE)
            a = p_ref[pl.ds(t0, CONV_TILE), 0:D_CONV].astype(F32)
            sg = _sigmoid(p_ref[pl.ds(t0, CONV_TILE), D_CONV:2 * D_CONV].astype(F32))
            dp_ref[pl.ds(t0, CONV_TILE), 0:D_CONV] = (dh * sg).astype(BF16)
            dp_ref[pl.ds(t0, CONV_TILE), D_CONV:2 * D_CONV] = (dh * a * sg * (1.0 - sg)).astype(BF16)
            return c

        lax.fori_loop(0, nt, glu_bwd, 0)

    full = lambda shape: pl.BlockSpec(shape, lambda i: (0, 0))
    vec = jax.ShapeDtypeStruct((1, D_CONV), F32)
    return pl.pallas_call(
        body,
        name=name,
        grid=(1,),
        in_specs=[full((s, 2 * D_CONV)), full((s, D_CONV)), full((CONV_K, D_CONV)), full((1, D_CONV)),
                  full((1, D_CONV)), full((1, D_CONV))],
        out_specs=[full((s, 2 * D_CONV)), full((CONV_K, D_CONV)), full((1, D_CONV)), full((1, D_CONV)),
                   full((1, D_CONV))],
        out_shape=[jax.ShapeDtypeStruct((s, 2 * D_CONV), BF16), jax.ShapeDtypeStruct((CONV_K, D_CONV), F32),
                   vec, vec, vec],
        scratch_shapes=[pltpu.VMEM((s + CONV_HALO, D_CONV), F32), pltpu.VMEM((s + CONV_HALO, D_CONV), F32)],
        compiler_params=_cparams(("arbitrary",)),
    )(p_ab, dya, conv_w, conv_b, ln_g, ln_b)


N_SGU_HEADS = D_SGU // HEAD_DIM


def _head_masks(width):
    lane = lax.broadcasted_iota(jnp.int32, (1, width), 1)
    return [(lane >= h * HEAD_DIM) & (lane < (h + 1) * HEAD_DIM) for h in range(width // HEAD_DIM)]


def _tril_mask():
    r = lax.broadcasted_iota(jnp.int32, (CHUNK, CHUNK), 0)
    c = lax.broadcasted_iota(jnp.int32, (CHUNK, CHUNK), 1)
    return c <= r


def _sgu_norm(bv, g, beta):
    vg = _gelu(bv)
    mu = jnp.mean(vg, axis=-1, keepdims=True)
    xc = vg - mu
    rstd = lax.rsqrt(jnp.mean(xc * xc, axis=-1, keepdims=True) + EPS)
    xhat = xc * rstd
    return xhat, rstd, xhat * g + beta


def _sgu_fwd(p_ab, ln_g, ln_b, w_s, bias, *, name):
    s = p_ab.shape[0]

    def body(p_ref, g_ref, beta_ref, w_ref, bias_ref, y_ref):
        u = _gelu(p_ref[:, 0:D_SGU].astype(F32))
        _, _, vn = _sgu_norm(p_ref[:, D_SGU:2 * D_SGU].astype(F32), g_ref[...], beta_ref[...])
        vb = vn.astype(BF16)
        tril = _tril_mask()
        mixed = bias_ref[...]
        for h, m in enumerate(_head_masks(D_SGU)):
            wh = jnp.where(tril, w_ref[h], 0.0).astype(BF16)
            mixed = mixed + _dot(wh, jnp.where(m, vb, jnp.zeros_like(vb)), _NN)
        y_ref[...] = u * mixed

    return pl.pallas_call(
        body,
        name=name,
        grid=(s // CHUNK,),
        in_specs=[pl.BlockSpec((CHUNK, 2 * D_SGU), lambda i: (i, 1)),
                  pl.BlockSpec((1, D_SGU), lambda i: (0, 0)), pl.BlockSpec((1, D_SGU), lambda i: (0, 0)),
                  pl.BlockSpec((N_SGU_HEADS, CHUNK, CHUNK), lambda i: (0, 0, 0)),
                  pl.BlockSpec((CHUNK, D_SGU), lambda i: (0, 0))],
        out_specs=pl.BlockSpec((CHUNK, D_SGU), lambda i: (i, 0)),
        out_shape=jax.ShapeDtypeStruct((s, D_SGU), F32),
        compiler_params=_cparams(("parallel",)),
    )(p_ab, ln_g, ln_b, w_s, bias)


def _sgu_bwd(p_ab, dyb, ln_g, ln_b, w_s, bias, *, name):
    s = p_ab.shape[0]
    n_chunks = s // CHUNK

    def body(p_ref, dy_ref, g_ref, beta_ref, w_ref, bias_ref, dp_ref, dw_ref, db_ref, dg_ref, dbeta_ref, dbias_ref):
        @pl.when(pl.program_id(0) == 0)
        def _():
            dw_ref[...] = jnp.zeros_like(dw_ref)
            dbias_ref[...] = jnp.zeros_like(dbias_ref)
            dg_ref[...] = jnp.zeros_like(dg_ref)
            dbeta_ref[...] = jnp.zeros_like(dbeta_ref)

        bu = p_ref[:, 0:D_SGU].astype(F32)
        bv = p_ref[:, D_SGU:2 * D_SGU].astype(F32)
        u = _gelu(bu)
        gv = g_ref[...]
        xhat, rstd, vn = _sgu_norm(bv, gv, beta_ref[...])
        vb = vn.astype(BF16)
        tril = _tril_mask()
        masks = _head_masks(D_SGU)
        whs = [jnp.where(tril, w_ref[h], 0.0).astype(BF16) for h in range(N_SGU_HEADS)]
        mixed = bias_ref[...]
        for h, m in enumerate(masks):
            mixed = mixed + _dot(whs[h], jnp.where(m, vb, jnp.zeros_like(vb)), _NN)
        dy = dy_ref[...]
        dp_ref[:, 0:D_SGU] = (dy * mixed * _gelu_grad(bu)).astype(BF16)
        dmixed = dy * u
        dbias_ref[...] += dmixed
        dmb = dmixed.astype(BF16)
        dvn = jnp.zeros((CHUNK, D_SGU), F32)
        for h, m in enumerate(masks):
            dmh = jnp.where(m, dmb, jnp.zeros_like(dmb))
            dvn = dvn + _dot(whs[h], dmh, _TN)
            dw_ref[h] += jnp.where(tril, _dot(dmh, vb, _NT), 0.0)
        dg_ref[...] += jnp.sum(dvn * xhat, axis=0, keepdims=True)
        dbeta_ref[...] += jnp.sum(dvn, axis=0, keepdims=True)
        dxhat = dvn * gv
        dvg = rstd * (dxhat - jnp.mean(dxhat, axis=-1, keepdims=True)
                      - xhat * jnp.mean(dxhat * xhat, axis=-1, keepdims=True))
        dp_ref[:, D_SGU:2 * D_SGU] = (dvg * _gelu_grad(bv)).astype(BF16)

        @pl.when(pl.program_id(0) == n_chunks - 1)
        def _():
            chan = lax.broadcasted_iota(jnp.int32, (D_SGU, LANES), 0)
            head = lax.broadcasted_iota(jnp.int32, (D_SGU, LANES), 1)
            to_head = jnp.where(chan // HEAD_DIM == head, 1.0, 0.0).astype(BF16)
            db_ref[...] = _dot(_split_bf16(dbias_ref[...]), jnp.concatenate([to_head, to_head], axis=0), _NN)

    vec = pl.BlockSpec((1, D_SGU), lambda i: (0, 0))
    wspec = pl.BlockSpec((N_SGU_HEADS, CHUNK, CHUNK), lambda i: (0, 0, 0))
    bspec = pl.BlockSpec((CHUNK, D_SGU), lambda i: (0, 0))
    return pl.pallas_call(
        body,
        name=name,
        grid=(n_chunks,),
        in_specs=[pl.BlockSpec((CHUNK, 2 * D_SGU), lambda i: (i, 1)), pl.BlockSpec((CHUNK, D_SGU), lambda i: (i, 0)),
                  vec, vec, wspec, bspec],
        out_specs=[pl.BlockSpec((CHUNK, 2 * D_SGU), lambda i: (i, 0)), wspec,
                   pl.BlockSpec((CHUNK, LANES), lambda i: (0, 0)), vec, vec],
        out_shape=[jax.ShapeDtypeStruct((s, 2 * D_SGU), BF16),
                   jax.ShapeDtypeStruct((N_SGU_HEADS, CHUNK, CHUNK), F32),
                   jax.ShapeDtypeStruct((CHUNK, LANES), F32),
                   jax.ShapeDtypeStruct((1, D_SGU), F32), jax.ShapeDtypeStruct((1, D_SGU), F32)],
        scratch_shapes=[pltpu.VMEM((CHUNK, D_SGU), F32)],
        compiler_params=_cparams(("arbitrary",)),
    )(p_ab, dyb, ln_g, ln_b, w_s, bias)


N_PAIRS = D_SB // LANES
QKV_BLOCK0 = D_AB // LANES
SB_SCALE = HEAD_DIM ** -0.5


def _sb_logits(z, valid):
    nz = -z
    t = jnp.log(1.0 + jnp.exp(jnp.minimum(z, nz)))
    l1 = jnp.minimum(nz, 0.0) - t
    if valid is not None:
        l1 = jnp.where(valid, l1, 0.0)
    return l1, jnp.minimum(z, 0.0) - t


def _split_hi_lo(x):
    hi = lax.bitcast_convert_type(lax.bitcast_convert_type(x, jnp.uint32) & jnp.uint32(0xFFFF0000), F32)
    return jnp.concatenate([hi, x - hi], axis=1)


def _cumsum_operand(keep):
    half = jnp.concatenate([keep.astype(F32), jnp.ones((CHUNK, CHUNK), F32)], axis=1)
    return jnp.concatenate([half, half], axis=0)


Q_BLOCKS_PER_STEP = 4


def _q_blocks_per_step(nq):
    return next(n for n in (Q_BLOCKS_PER_STEP, 2, 1) if nq % n == 0)


def _attn_fwd(qkv, *, name):
    s = qkv.shape[0]
    nq = s // CHUNK
    per_step = _q_blocks_per_step(nq)

    def body(q_ref, k_ref, v_ref, o_ref, t_ref):
        masks = _head_masks(LANES)
        row = lax.broadcasted_iota(jnp.int32, (CHUNK, CHUNK), 0)
        col = lax.broadcasted_iota(jnp.int32, (CHUNK, CHUNK), 1)
        after_op = _cumsum_operand(row > col)
        cmr = col - row
        zc = jnp.zeros((CHUNK, LANES), F32)

        def q_block(sub, _):
            qi = pl.program_id(1) * per_step + sub
            q_rows = pl.ds(pl.multiple_of(sub * CHUNK, CHUNK), CHUNK)
            q = q_ref[q_rows, :] * SB_SCALE
            zero = jnp.zeros_like(q)
            qs = [jnp.where(m, q, zero) for m in masks]

            def blocks(js, carry):
                o, c0, c1 = carry
                kvs, valids = [], []
                for j in js:
                    k0 = pl.multiple_of(jnp.maximum(j, 0) * CHUNK, CHUNK)
                    kvs.append((k_ref[pl.ds(k0, CHUNK), :], v_ref[pl.ds(k0, CHUNK), :]))
                    valids.append(cmr < jnp.where(j >= 0, (qi - j) * CHUNK, -CHUNK))
                units = [(h, b) for b in range(len(js)) for h in range(2)]
                zs = [_dot(qs[h], kvs[b][0], _NT) for h, b in units]
                logits = [_sb_logits(z, valids[b]) for z, (h, b) in zip(zs, units)]
                sums = [_dot(_split_hi_lo(l1), after_op, _NN) for l1, _ in logits]
                cs = [c0, c1]
                probs = []
                for (h, b), (_, lb), sm in zip(units, logits, sums):
                    probs.append(jnp.where(valids[b], jnp.exp(lb + sm[:, :CHUNK] + cs[h]), 0.0))
                    cs[h] = cs[h] + sm[:, CHUNK:]
                for (h, b), a in zip(units, probs):
                    o = o + _dot(a.astype(BF16), jnp.where(masks[h], kvs[b][1], zero), _NN)
                return o, cs[0], cs[1]

            n_four = (qi + 1) // 4
            carry = lax.fori_loop(0, n_four, lambda jj, c: blocks([qi - 4 * jj - i for i in range(4)], c), (zc,) * 3)
            top = qi - 4 * n_four
            o, c0, c1 = lax.fori_loop(0, (top + 2) // 2, lambda jj, c: blocks([top - 2 * jj, top - 2 * jj - 1], c), carry)
            o_ref[q_rows, :] = o
            t_ref[q_rows, 0:LANES] = c0
            t_ref[q_rows, LANES:2 * LANES] = c1
            return 0

        lax.fori_loop(0, per_step, q_block, 0)

    rows = per_step * CHUNK
    return pl.pallas_call(
        body,
        name=name,
        grid=(N_PAIRS, nq // per_step),
        in_specs=[pl.BlockSpec((rows, LANES), lambda p, i: (i, QKV_BLOCK0 + p)),
                  pl.BlockSpec((s, LANES), lambda p, i: (0, QKV_BLOCK0 + N_PAIRS + p)),
                  pl.BlockSpec((s, LANES), lambda p, i: (0, QKV_BLOCK0 + 2 * N_PAIRS + p))],
        out_specs=[pl.BlockSpec((rows, LANES), lambda p, i: (i, p)),
                   pl.BlockSpec((rows, 2 * LANES), lambda p, i: (i, p))],
        out_shape=[jax.ShapeDtypeStruct((s, D_SB), F32), jax.ShapeDtypeStruct((s, 2 * D_SB), F32)],
        compiler_params=_cparams(("parallel", "parallel")),
    )(qkv, qkv, qkv)


def _attn_bwd(qkv, t_tot, do, *, name):
    s = qkv.shape[0]
    nq = s // CHUNK
    per_step = _q_blocks_per_step(nq)

    def body(q_ref, k_ref, v_ref, t_ref, do_ref, dq_ref, dk_ref, dv_ref):
        @pl.when(pl.program_id(1) == 0)
        def _():
            dk_ref[...] = jnp.zeros_like(dk_ref)
            dv_ref[...] = jnp.zeros_like(dv_ref)

        masks = _head_masks(LANES)
        row = lax.broadcasted_iota(jnp.int32, (CHUNK, CHUNK), 0)
        col = lax.broadcasted_iota(jnp.int32, (CHUNK, CHUNK), 1)
        upto_op = _cumsum_operand(row <= col)
        before_op = _cumsum_operand(row < col)
        cmr = col - row
        zc = jnp.zeros((CHUNK, LANES), F32)

        def q_block(sub, _):
            qi = pl.program_id(1) * per_step + sub
            q_rows = pl.ds(pl.multiple_of(sub * CHUNK, CHUNK), CHUNK)
            q = q_ref[q_rows, :] * SB_SCALE
            dob = do_ref[q_rows, :].astype(BF16)
            zero = jnp.zeros_like(q)
            qs = [jnp.where(m, q, zero) for m in masks]
            dos = [jnp.where(m, dob, zero) for m in masks]
            tots = [t_ref[q_rows, 0:LANES], t_ref[q_rows, LANES:2 * LANES]]

            def blocks(js, carry):
                dq, cl0, cl1, cp0, cp1 = carry
                starts = [pl.multiple_of(jnp.minimum(j, nq - 1) * CHUNK, CHUNK) for j in js]
                valids = [cmr < (qi - j) * CHUNK for j in js]
                kvs = [(k_ref[pl.ds(k0, CHUNK), :], v_ref[pl.ds(k0, CHUNK), :]) for k0 in starts]
                units = [(h, b) for b in range(len(js)) for h in range(2)]
                zs = [_dot(qs[h], kvs[b][0], _NT) for h, b in units]
                das = [_dot(dos[h], kvs[b][1], _NT) for h, b in units]
                logits = [_sb_logits(z, valids[b]) for z, (h, b) in zip(zs, units)]
                sums = [_dot(_split_hi_lo(l1), upto_op, _NN) for l1, _ in logits]
                cls, cps = [cl0, cl1], [cp0, cp1]
                probs, gs = [], []
                for (h, b), (_, lb), sm, da in zip(units, logits, sums, das):
                    a = jnp.where(valids[b], jnp.exp(lb + (tots[h] - cls[h] - sm[:, :CHUNK])), 0.0)
                    probs.append(a)
                    gs.append(a * da)
                    cls[h] = cls[h] + sm[:, CHUNK:]
                sums_g = [_dot(_split_hi_lo(g), before_op, _NN) for g in gs]
                dzs = []
                for (h, b), (_, lb), g, sg in zip(units, logits, gs, sums_g):
                    dz = g - (g + sg[:, :CHUNK] + cps[h]) * jnp.exp(lb)
                    dzs.append(jnp.where(valids[b], dz, 0.0).astype(BF16))
                    cps[h] = cps[h] + sg[:, CHUNK:]
                for (h, b), dzb in zip(units, dzs):
                    dq = dq + _dot(dzb, jnp.where(masks[h], kvs[b][0], zero), _NN)
                for b, k0 in enumerate(starts):
                    dk_ref[pl.ds(k0, CHUNK), :] += _dot(dzs[2 * b], qs[0], _TN) + _dot(dzs[2 * b + 1], qs[1], _TN)
                    dv_ref[pl.ds(k0, CHUNK), :] += (_dot(probs[2 * b].astype(BF16), dos[0], _TN)
                                                    + _dot(probs[2 * b + 1].astype(BF16), dos[1], _TN))
                return dq, cls[0], cls[1], cps[0], cps[1]

            n_four = (qi + 1) // 4
            carry = lax.fori_loop(0, n_four, lambda jj, c: blocks([4 * jj + i for i in range(4)], c), (zc,) * 5)
            base = 4 * n_four
            carry = lax.fori_loop(0, (qi - base + 2) // 2, lambda jj, c: blocks([base + 2 * jj, base + 2 * jj + 1], c), carry)
            dq_ref[q_rows, :] = (carry[0] * SB_SCALE).astype(BF16)
            return 0

        lax.fori_loop(0, per_step, q_block, 0)

    rows = per_step * CHUNK
    blk = pl.BlockSpec((rows, LANES), lambda p, i: (i, p))
    col_blk = pl.BlockSpec((s, LANES), lambda p, i: (0, p))
    out = jax.ShapeDtypeStruct((s, D_SB), F32)
    return pl.pallas_call(
        body,
        name=name,
        grid=(N_PAIRS, nq // per_step),
        in_specs=[pl.BlockSpec((rows, LANES), lambda p, i: (i, QKV_BLOCK0 + p)),
                  pl.BlockSpec((s, LANES), lambda p, i: (0, QKV_BLOCK0 + N_PAIRS + p)),
                  pl.BlockSpec((s, LANES), lambda p, i: (0, QKV_BLOCK0 + 2 * N_PAIRS + p)),
                  pl.BlockSpec((rows, 2 * LANES), lambda p, i: (i, p)),
                  blk],
        out_specs=[blk, col_blk, col_blk],
        out_shape=[jax.ShapeDtypeStruct((s, D_SB), BF16), out, out],
        compiler_params=_cparams(("parallel", "arbitrary")),
    )(qkv, qkv, qkv, t_tot, do)


FFN_TILE = 256
FFN_COLS = 256
N_FF_BLOCKS = D_FF // FFN_COLS


def _ffn_act_fwd(up0, conv_w, conv_b, *, name):
    s = up0.shape[0]
    nt = s // FFN_TILE

    def body(xg_ref, xv_ref, wg_ref, wv_ref, bg_ref, bv_ref, act_ref, pg_ref, pv_ref):
        pg_ref[0:FFN_HALO, :] = jnp.zeros((FFN_HALO, FFN_COLS), F32)
        pv_ref[0:FFN_HALO, :] = jnp.zeros((FFN_HALO, FFN_COLS), F32)
        pg_ref[FFN_HALO:, :] = xg_ref[...].astype(F32)
        pv_ref[FFN_HALO:, :] = xv_ref[...].astype(F32)

        def tile(i, c):
            t0 = pl.multiple_of(i * FFN_TILE, FFN_TILE)
            outs = []
            for p_ref, w_ref, b_ref in ((pg_ref, wg_ref, bg_ref), (pv_ref, wv_ref, bv_ref)):
                window = p_ref[pl.ds(t0, FFN_TILE + FFN_HALO), :]
                acc = b_ref[...] + w_ref[2:3, :] * window[FFN_HALO:, :]
                for j in range(1, FFN_K):
                    acc = acc + w_ref[FFN_K - 1 - j:FFN_K - j, :] * _shift_down(window, j, FFN_HALO)
                outs.append(acc)
            gate, val = outs
            act_ref[pl.ds(t0, FFN_TILE), :] = (gate * _sigmoid(gate) * val).astype(BF16)
            return c

        lax.fori_loop(0, nt, tile, 0)

    gcol = lambda rows: pl.BlockSpec((rows, FFN_COLS), lambda j: (0, j))
    vcol = lambda rows: pl.BlockSpec((rows, FFN_COLS), lambda j: (0, j + N_FF_BLOCKS))
    return pl.pallas_call(
        body,
        name=name,
        grid=(N_FF_BLOCKS,),
        in_specs=[gcol(s), vcol(s), gcol(FFN_K), vcol(FFN_K), gcol(1), vcol(1)],
        out_specs=gcol(s),
        out_shape=jax.ShapeDtypeStruct((s, D_FF), BF16),
        scratch_shapes=[pltpu.VMEM((s + FFN_HALO, FFN_COLS), F32), pltpu.VMEM((s + FFN_HALO, FFN_COLS), F32)],
        compiler_params=_cparams(("parallel",)),
    )(up0, up0, conv_w, conv_w, conv_b, conv_b)


def _ffn_act_bwd(up0, dact, conv_w, conv_b, *, name):
    s = up0.shape[0]
    nt = s // FFN_TILE

    def body(xg_ref, xv_ref, da_ref, wg_ref, wv_ref, bg_ref, bv_ref, dxg_ref, dxv_ref, dwg_ref, dwv_ref, dbg_ref, dbv_ref,
             pg_ref, pv_ref, dg_ref, dv_ref):
        zeros = jnp.zeros((FFN_HALO, FFN_COLS), F32)
        for p_ref, x_ref in ((pg_ref, xg_ref), (pv_ref, xv_ref)):
            p_ref[0:FFN_HALO, :] = zeros
            p_ref[FFN_HALO:, :] = x_ref[...].astype(F32)
        dg_ref[s:s + FFN_HALO, :] = zeros
        dv_ref[s:s + FFN_HALO, :] = zeros
        for ref in (dwg_ref, dwv_ref, dbg_ref, dbv_ref):
            ref[...] = jnp.zeros_like(ref)

        def conv(p_ref, w_ref, b_ref, t0):
            window = p_ref[pl.ds(t0, FFN_TILE + FFN_HALO), :]
            taps = [_shift_down(window, j, FFN_HALO) for j in range(FFN_K)]
            out = b_ref[...]
            for j in range(FFN_K):
                out = out + w_ref[FFN_K - 1 - j:FFN_K - j, :] * taps[j]
            return out, taps

        def tile(i, c):
            t0 = pl.multiple_of(i * FFN_TILE, FFN_TILE)
            gate, taps_g = conv(pg_ref, wg_ref, bg_ref, t0)
            val, taps_v = conv(pv_ref, wv_ref, bv_ref, t0)
            da = da_ref[pl.ds(t0, FFN_TILE), :].astype(F32)
            sg = lax.logistic(gate)
            dgate = da * val * (sg * (1.0 + gate * (1.0 - sg)))
            dval = da * gate * sg
            dg_ref[pl.ds(t0, FFN_TILE), :] = dgate
            dv_ref[pl.ds(t0, FFN_TILE), :] = dval
            dbg_ref[...] += jnp.sum(dgate, axis=0, keepdims=True)
            dbv_ref[...] += jnp.sum(dval, axis=0, keepdims=True)
            for j in range(FFN_K):
                dwg_ref[FFN_K - 1 - j:FFN_K - j, :] += jnp.sum(dgate * taps_g[j], axis=0, keepdims=True)
                dwv_ref[FFN_K - 1 - j:FFN_K - j, :] += jnp.sum(dval * taps_v[j], axis=0, keepdims=True)
            return c

        lax.fori_loop(0, nt, tile, 0)

        def tile_dx(i, c):
            t0 = pl.multiple_of(i * FFN_TILE, FFN_TILE)
            for d_ref, w_ref, dx_ref in ((dg_ref, wg_ref, dxg_ref), (dv_ref, wv_ref, dxv_ref)):
                window = d_ref[pl.ds(t0, FFN_TILE + FFN_HALO), :]
                dx = w_ref[FFN_K - 1:FFN_K, :] * window[:FFN_TILE, :]
                for j in range(1, FFN_K):
                    dx = dx + w_ref[FFN_K - 1 - j:FFN_K - j, :] * _shift_up(window, j, FFN_TILE)
                dx_ref[pl.ds(t0, FFN_TILE), :] = dx.astype(BF16)
            return c

        lax.fori_loop(0, nt, tile_dx, 0)

    gcol = lambda rows: pl.BlockSpec((rows, FFN_COLS), lambda j: (0, j))
    vcol = lambda rows: pl.BlockSpec((rows, FFN_COLS), lambda j: (0, j + N_FF_BLOCKS))
    half = lambda rows, dtype: jax.ShapeDtypeStruct((rows, D_FF), dtype)
    padded = pltpu.VMEM((s + FFN_HALO, FFN_COLS), F32)
    return pl.pallas_call(
        body,
        name=name,
        grid=(N_FF_BLOCKS,),
        in_specs=[gcol(s), vcol(s), gcol(s), gcol(FFN_K), vcol(FFN_K), gcol(1), vcol(1)],
        out_specs=[gcol(s), gcol(s), gcol(FFN_K), gcol(FFN_K), gcol(1), gcol(1)],
        out_shape=[half(s, BF16), half(s, BF16), half(FFN_K, F32), half(FFN_K, F32), half(1, F32), half(1, F32)],
        scratch_shapes=[padded, padded, padded, padded],
        compiler_params=_cparams(("parallel",)),
    )(up0, up0, dact, conv_w, conv_w, conv_b, conv_b)


MESH = pl.DeviceIdType.MESH


def _position():
    x, y, c = lax.axis_index("x"), lax.axis_index("y"), lax.axis_index("c")
    return x, y, c, 4 * x + 2 * y + c


def _peer(k):
    x, y, c, _ = _position()
    px = 1 - x if k & 4 else x
    py = 1 - y if k & 2 else y
    pc = 1 - c if k & 1 else c
    return (px, py, pc), 4 * px + 2 * py + pc


_HBM = pl.BlockSpec(memory_space=pltpu.HBM)
_SEM = pl.BlockSpec(memory_space=pltpu.SEMAPHORE)
_DATAFLOW = pltpu.SideEffectType.DATAFLOW_SIDE_EFFECTING
N_PEERS = N_DEV - 1


class _SplitExchange:
    def __init__(self, src, *, kind, name):
        self.kind, self.name, self.dtype = kind, name, src.dtype
        scatter = kind.startswith("scatter")
        by_blocks = kind.endswith("blocks")
        self.scatter, self.by_blocks = scatter, by_blocks
        if by_blocks:
            self.r, self.cols, self.land_shape = None, None, src.shape if scatter else (N_DEV,) + src.shape
        else:
            self.r = src.shape[0] // N_DEV if scatter else src.shape[0]
            self.cols = src.shape[1]
            self.land_shape = (N_DEV, self.r, self.cols) if scatter else (N_DEV * self.r, self.cols)
        r = self.r

        def copies(src_ref, land_ref, send_sems, recv_sems, local_sem):
            me = _position()[3]

            def rows(ref, idx):
                return ref.at[pl.ds(pl.multiple_of(idx * r, r), r), :]

            if not scatter:
                outgoing = lambda idx: src_ref
            else:
                outgoing = (lambda idx: src_ref.at[idx]) if by_blocks else (lambda idx: rows(src_ref, idx))
            slot = (lambda idx: land_ref.at[idx]) if (scatter or by_blocks) else (lambda idx: rows(land_ref, idx))
            sends, recvs = [], []
            for k in range(1, N_DEV):
                peer, pidx = _peer(k)
                sems = dict(send_sem=send_sems[k - 1], recv_sem=recv_sems[k - 1], device_id=peer, device_id_type=MESH)
                sends.append(pltpu.make_async_remote_copy(src_ref=outgoing(pidx), dst_ref=slot(me), **sems))
                recvs.append(pltpu.make_async_remote_copy(src_ref=outgoing(pidx), dst_ref=slot(pidx), **sems))
            return sends, recvs, pltpu.make_async_copy(outgoing(me), slot(me), local_sem)

        self._copies = copies
        self.src = src

    @staticmethod
    def start(exchanges, name):
        n = len(exchanges)
        per = 2 * N_PEERS + 1

        def start_body(*refs):
            outs = refs[2 * n:]
            for i, ex in enumerate(exchanges):
                sems = outs[per * i:per * (i + 1)]
                sends, _, local = ex._copies(refs[2 * i], refs[2 * i + 1], sems[:N_PEERS], sems[N_PEERS:-1], sems[-1])
                for cp in sends + [local]:
                    cp.start()
            outs[-1][...] = jnp.zeros_like(outs[-1])

        sem = pltpu.SemaphoreType.DMA(())
        operands, thru_shapes = [], []
        for ex in exchanges:
            operands += [pltpu.with_memory_space_constraint(ex.src, pltpu.HBM),
                         pltpu.with_memory_space_constraint(lax.empty(ex.land_shape, ex.dtype), pltpu.HBM)]
            thru_shapes += [pltpu.HBM(ex.src.shape, ex.dtype), pltpu.HBM(ex.land_shape, ex.dtype)]
        out = pl.pallas_call(
            start_body,
            name=name,
            in_specs=(_HBM,) * (2 * n),
            out_specs=(_SEM,) * (per * n) + (_HBM,) * (2 * n) + (pl.BlockSpec(memory_space=pltpu.VMEM),),
            out_shape=(sem,) * (per * n) + tuple(thru_shapes) + (jax.ShapeDtypeStruct((8, LANES), F32),),
            input_output_aliases={i: per * n + i for i in range(2 * n)},
            compiler_params=pltpu.CompilerParams(has_side_effects=_DATAFLOW),
        )(*operands)
        for i, ex in enumerate(exchanges):
            ex.sems = out[per * i:per * (i + 1)]
            ex.src_thru, ex.land_thru = out[per * n + 2 * i], out[per * n + 2 * i + 1]
        return out[-1][0, 0]

    def finish(self, after):
        copies = self._copies

        def wait_body(src_ref, land_ref, *rest):
            sends, recvs, local = copies(src_ref, land_ref, rest[:N_PEERS], rest[N_PEERS:2 * N_PEERS], rest[2 * N_PEERS])
            for cp in sends:
                cp.wait_send()
            for cp in recvs:
                cp.wait_recv()
            local.wait()

        return pl.pallas_call(
            wait_body,
            name=f"{self.name}_wait",
            in_specs=(_HBM, _HBM) + (_SEM,) * (2 * N_PEERS + 1) + (pl.BlockSpec(memory_space=pl.ANY),),
            out_specs=(_HBM, _HBM),
            out_shape=(pltpu.HBM(self.src_thru.shape, self.dtype), pltpu.HBM(self.land_shape, self.dtype)),
            input_output_aliases={0: 0, 1: 1},
            compiler_params=pltpu.CompilerParams(has_side_effects=_DATAFLOW),
        )(self.src_thru, self.land_thru, *self.sems, after)[1]


ADAM_MAX_ROWS = 256
BF16_SUBLANES = 16


def _row_tile(rows):
    fitting = [t for t in range(BF16_SUBLANES, ADAM_MAX_ROWS + 1, BF16_SUBLANES) if rows % t == 0]
    return max(fitting) if fitting else rows


def _layer_parts_specs(n_layers, n_parts, tr, cols):
    return [pl.BlockSpec((n_parts, tr, cols), lambda l, i, j=j: (0, jnp.where(l == j, i, 0), 0)) for j in range(n_layers)]


def _select_layer_sum(p_refs):
    l = pl.program_id(0)
    g = None
    for j, p_ref in enumerate(p_refs):
        gj = p_ref[0].astype(F32)
        for k in range(1, p_ref.shape[0]):
            gj = gj + p_ref[k].astype(F32)
        g = gj if g is None else jnp.where(l == j, gj, g)
    return g


def _adamw(parts, w, m, v, *, name):
    n_layers, rows, cols = w.shape
    tr = _row_tile(rows)

    def body(*refs):
        w_ref, m_ref, v_ref, g_ref, d_ref, m2_ref, v2_ref = refs[n_layers:]
        g = _select_layer_sum(refs[:n_layers])
        m2 = ADAM_B1 * m_ref[...] + (1.0 - ADAM_B1) * g
        v2 = ADAM_B2 * v_ref[...] + (1.0 - ADAM_B2) * (g * g)
        m_hat = m2 / (1.0 - ADAM_B1 ** ADAM_STEP)
        v_hat = v2 / (1.0 - ADAM_B2 ** ADAM_STEP)
        g_ref[...] = g
        d_ref[...] = -ADAM_LR * (m_hat / (jnp.sqrt(v_hat) + ADAM_EPS) + ADAM_WD * w_ref[...])
        m2_ref[...] = m2
        v2_ref[...] = v2

    slab = pl.BlockSpec((None, tr, cols), lambda l, i: (l, i, 0))
    out = jax.ShapeDtypeStruct((n_layers, rows, cols), F32)
    p_specs = _layer_parts_specs(n_layers, parts[0].shape[0], tr, cols)
    return pl.pallas_call(
        body,
        name=name,
        grid=(n_layers, rows // tr),
        in_specs=p_specs + [slab, slab, slab],
        out_specs=[slab, slab, slab, slab],
        out_shape=[out, out, out, out],
        compiler_params=_cparams(("arbitrary", "arbitrary")),
    )(*parts, w, m, v)


SLAB_ROWS = 32
_SMALL_SHARDED = (("conv_w", (2, 31, 32)), ("ffn_conv_w", (2, 3, 704)))
_REPLICATED = (("g_mix", (2, 1024)), ("conv_b", (2, 256)), ("conv_ln_g", (2, 256)), ("conv_ln_b", (2, 256)),
               ("sgu_ln_g", (2, 256)), ("sgu_ln_b", (2, 256)), ("sgu_w", (2, 4, 128, 128)), ("sgu_b", (2, 4, 128)),
               ("g_out", (2, 1024)), ("g_ffn", (2, 1024)), ("ffn_conv_b", (2, 5632)), ("g_final", (1024,)))


def _seg_rows(n_elems):
    return -(-n_elems // LANES)


def _pack(arrays, lead=()):
    segs = []
    for a in arrays:
        flat = a.reshape(lead + (-1,)).astype(F32)
        pad = _seg_rows(flat.shape[-1]) * LANES - flat.shape[-1]
        if pad:
            flat = jnp.pad(flat, [(0, 0)] * len(lead) + [(0, pad)])
        segs.append(flat)
    flat = jnp.concatenate(segs, axis=-1)
    rows = flat.shape[-1] // LANES
    pad_rows = -rows % SLAB_ROWS
    if pad_rows:
        flat = jnp.pad(flat, [(0, 0)] * len(lead) + [(0, pad_rows * LANES)])
    return flat.reshape(lead + (rows + pad_rows, LANES))


def _unpack(slab, shapes, lead=()):
    flat = slab.reshape(lead + (-1,))
    out, off = [], 0
    for shape in shapes:
        n = math.prod(shape)
        out.append(flat[..., off:off + n].reshape(lead + tuple(shape)))
        off += _seg_rows(n) * LANES
    return out


def _split_last(full):
    split = full.shape[:-1] + (N_DEV, full.shape[-1] // N_DEV)
    return jnp.moveaxis(full.reshape(split), -2, 0)


def _join_last(blocks):
    moved = jnp.moveaxis(blocks, 0, -2)
    return moved.reshape(moved.shape[:-2] + (moved.shape[-2] * moved.shape[-1],))


def _gathered(wt, n, l, after):
    if isinstance(wt[n][l], _SplitExchange):
        wt[n][l] = wt[n][l].finish(after)
    return wt[n][l]


def _layer_fwd(l, x, wt, small):
    tag = f"l{l}"
    h = _rmsnorm_fwd(x, small["g_mix"][l][None], name=f"{tag}_norm_mix")
    w_in_t = _gathered(wt, "w_in_t", l, h)
    p = _matmul(h, w_in_t, "nt", name=f"{tag}_proj", out_dtype=BF16)
    p_ab = qkv = p
    ya = _mixer_a_fwd(p_ab, wt["conv_w"][l], small["conv_b"][l][None], small["conv_ln_g"][l][None],
                      small["conv_ln_b"][l][None], name=f"{tag}_mixer_a")
    bias = jnp.repeat(small["sgu_b"][l].T, HEAD_DIM, axis=1)
    yb = _sgu_fwd(p_ab, small["sgu_ln_g"][l][None], small["sgu_ln_b"][l][None], small["sgu_w"][l], bias,
                  name=f"{tag}_sgu")
    yc, t_tot = _attn_fwd(qkv, name=f"{tag}_attn")
    y = _combine_fwd(ya, yb, yc, small["g_out"][l][None], name=f"{tag}_combine")
    x1 = _matmul(y, _gathered(wt, "w_out", l, y), "nn", name=f"{tag}_out_proj", residual=x)
    h2 = _rmsnorm_fwd(x1, small["g_ffn"][l][None], name=f"{tag}_norm_ffn")
    up0 = _matmul(h2, _gathered(wt, "w_up_t", l, h2), "nt", name=f"{tag}_up", out_dtype=BF16)
    act = _ffn_act_fwd(up0, wt["ffn_conv_w"][l], small["ffn_conv_b"][l][None], name=f"{tag}_ffn_act")
    x2 = _matmul(act, _gathered(wt, "w_down", l, act), "nn", name=f"{tag}_down", residual=x1)
    saved = dict(x=x, h=h, p_ab=p, qkv=p, ya=ya, yb=yb, yc=yc, t_tot=t_tot, y=y, x1=x1, h2=h2, up0=up0,
                 act=act, bias=bias)
    return x2, saved


def _layer_bwd(l, dres, sv, wt, small, scattering, token):
    tag = f"l{l}b"
    g = {}

    def scatter(n, partial):
        scattering[n][l] = _SplitExchange(partial, kind="scatter_rows", name=f"scatter_{n}_l{l}")
        return _SplitExchange.start([scattering[n][l]], name=f"scatter_{n}_l{l}_start")

    dx2, dx2_b = dres
    dact = _matmul(dx2_b, wt["w_down"][l], "nt", name=f"{tag}_dact", out_dtype=BF16)
    tok = scatter("w_down", _matmul(sv["act"], dx2_b, "tn", name=f"{tag}_dw_down", out_dtype=BF16))
    dup_g, dup_v, dwg, dwv, dbg, dbv = _ffn_act_bwd(sv["up0"], dact, wt["ffn_conv_w"][l], small["ffn_conv_b"][l][None] + tok + token,
                                                    name=f"{tag}_ffn_act")
    g["ffn_conv_w"] = jnp.concatenate([dwg, dwv], axis=1)
    g["ffn_conv_b"] = jnp.concatenate([dbg[0], dbv[0]])
    dh2 = _matmul(dup_g, wt["w_up_t"][l], "nn", name=f"{tag}_dh2_gate")
    dh2 = _matmul(dup_v, wt["w_up_t"][l], "nn", name=f"{tag}_dh2_val", b_k0=D_FF, residual=dh2)
    dw_up = _matmul(dup_g, sv["h2"], "tn", name=f"{tag}_dw_up_gate", out_dtype=BF16, rows=(2 * D_FF, 0))
    dw_up = _matmul(dup_v, sv["h2"], "tn", name=f"{tag}_dw_up_val", out_dtype=BF16, rows=(2 * D_FF, D_FF), into=dw_up)
    tok = scatter("w_up_t", dw_up)
    dx1, dx1_b, dg = _rmsnorm_bwd(sv["x1"], small["g_ffn"][l][None] + tok, dh2, dx2, name=f"{tag}_norm_ffn")
    g["g_ffn"] = dg[0]
    dy = _matmul(dx1_b, wt["w_out"][l], "nt", name=f"{tag}_dy")
    tok = scatter("w_out", _matmul(sv["y"], dx1_b, "tn", name=f"{tag}_dw_out", out_dtype=BF16))
    dya, dyb, dyc, dg = _combine_bwd(dy, sv["ya"], sv["yb"], sv["yc"], small["g_out"][l][None] + tok,
                                     name=f"{tag}_combine")
    g["g_out"] = dg[0]
    dq, dk, dv = _attn_bwd(sv["qkv"], sv["t_tot"], dyc, name=f"{tag}_attn")
    dp_b, g["sgu_w"], db, dg, dbeta = _sgu_bwd(sv["p_ab"], dyb, small["sgu_ln_g"][l][None], small["sgu_ln_b"][l][None],
                                               small["sgu_w"][l], sv["bias"], name=f"{tag}_sgu")
    g["sgu_b"] = db[:, :N_SGU_HEADS].T
    g["sgu_ln_g"], g["sgu_ln_b"] = dg[0], dbeta[0]
    dp_a, g["conv_w"], dcb, dg, dbeta = _mixer_a_bwd(sv["p_ab"], dya, wt["conv_w"][l], small["conv_b"][l][None],
                                                     small["conv_ln_g"][l][None], small["conv_ln_b"][l][None],
                                                     name=f"{tag}_mixer_a")
    g["conv_b"], g["conv_ln_g"], g["conv_ln_b"] = dcb[0], dg[0], dbeta[0]
    dp = jnp.concatenate([dp_a, dp_b, dq, dk.astype(BF16), dv.astype(BF16)], axis=1)
    tok = scatter("w_in_t", _matmul(dp, sv["h"], "tn", name=f"{tag}_dw_in", out_dtype=BF16))
    dh = _matmul(dp, wt["w_in_t"][l], "nn", name=f"{tag}_dh")
    dx, dx_b, dg = _rmsnorm_bwd(sv["x"], small["g_mix"][l][None] + tok, dh, dx1, name=f"{tag}_norm_mix")
    g["g_mix"] = dg[0]
    return (dx, dx_b), g


_BIG = ("w_in_t", "w_out", "w_up_t", "w_down")


def kernel(x, g_mix, w_in, conv_w, conv_b, conv_ln_g, conv_ln_b, sgu_ln_g, sgu_ln_b, sgu_w, sgu_b, g_out, w_out, g_ffn, w_up, ffn_conv_w, ffn_conv_b, w_down, g_final, loss_target, m_g_mix, m_w_in, m_conv_w, m_conv_b, m_conv_ln_g, m_conv_ln_b, m_sgu_ln_g, m_sgu_ln_b, m_sgu_w, m_sgu_b, m_g_out, m_w_out, m_g_ffn, m_w_up, m_ffn_conv_w, m_ffn_conv_b, m_w_down, m_g_final, v_g_mix, v_w_in, v_conv_w, v_conv_b, v_conv_ln_g, v_conv_ln_b, v_sgu_ln_g, v_sgu_ln_b, v_sgu_w, v_sgu_b, v_g_out, v_w_out, v_g_ffn, v_w_up, v_ffn_conv_w, v_ffn_conv_b, v_w_down, v_g_final):
    given = dict(locals())
    n_layers = g_mix.shape[0]
    layers = range(n_layers)
    small_sharded = [n for n, _ in _SMALL_SHARDED]
    replicated = [n for n, _ in _REPLICATED]
    small = {n: given[n] for n in replicated}

    filters = _SplitExchange(_pack([given[n] for n in small_sharded]), kind="gather_blocks", name="gather_filters")
    wt = {n: [None] * n_layers for n in _BIG}
    wt["w_in_t"][0] = _SplitExchange(w_in[0].T.astype(BF16), kind="gather_rows", name="gather_w_in_t_l0")
    tok = _SplitExchange.start([filters, wt["w_in_t"][0]], name="gather_first_start")
    w_in, w_out, w_up, w_down, tok = lax.optimization_barrier((w_in, w_out, w_up, w_down, tok))
    shard = {"w_in_t": [w_in[l].T.astype(BF16) for l in layers], "w_out": [w_out[l].astype(BF16) for l in layers],
             "w_up_t": [w_up[l].T.astype(BF16) for l in layers], "w_down": [w_down[l].astype(BF16) for l in layers]}
    later = [(n, l) for l in layers for n in _BIG if (n, l) != ("w_in_t", 0)]
    for n, l in later:
        wt[n][l] = _SplitExchange(shard[n][l], kind="gather_rows", name=f"gather_{n}_l{l}")
    small["g_mix"] = g_mix + tok + _SplitExchange.start([wt[n][l] for n, l in later], name="gather_weights_start")
    gathered_filters = filters.finish(small["g_mix"])
    for n, blocks in zip(small_sharded, _unpack(gathered_filters, [s for _, s in _SMALL_SHARDED], lead=(N_DEV,))):
        wt[n] = _join_last(blocks)

    xs = x[0]
    saved = []
    for l in layers:
        xs, sv = _layer_fwd(l, xs, wt, small)
        saved.append(sv)
    loss_tile, dx, dx_b, dgf = _loss_head(xs, g_final[None], loss_target[0], name="loss_head")
    dres = (dx, dx_b)
    scattering = {n: [None] * n_layers for n in _BIG}
    layered = [n for n in replicated if n not in ("g_final", "sgu_w")]
    slabs, sgu_w_parts = [None] * n_layers, [None] * n_layers
    tok = 0.0
    for l in reversed(layers):
        dres, g = _layer_bwd(l, dres, saved[l], wt, small, scattering, tok)
        own = _pack([_split_last(g[n]) for n in small_sharded], lead=(N_DEV,))
        shared = _pack([g[n] for n in layered] + [dgf[0]])
        slab = jnp.concatenate([own, jnp.broadcast_to(shared[None], (N_DEV,) + shared.shape)], axis=1)
        slabs[l] = _SplitExchange(slab, kind="scatter_blocks", name=f"scatter_small_grads_l{l}")
        sgu_w_parts[l] = _SplitExchange(g["sgu_w"].reshape(-1, LANES).astype(BF16), kind="gather_blocks",
                                        name=f"gather_sgu_w_grads_l{l}")
        tok = _SplitExchange.start([slabs[l], sgu_w_parts[l]], name=f"small_grads_l{l}_start")
    n_own = own.shape[1]

    after_backward = jnp.full((8, LANES), tok)
    received = {n: [scattering[n][l].finish(after_backward) for l in layers] for n in _BIG}
    out = {}

    def update(n, parts, transposed=False):
        turn = (lambda a: jnp.swapaxes(a, 1, 2)) if transposed else (lambda a: a)
        results = _adamw(parts, turn(given[n]), turn(given["m_" + n]), turn(given["v_" + n]), name=f"adamw_{n}")
        for pre, res in zip(("grad_", "delta_", "new_m_", "new_v_"), results):
            out[pre + n] = turn(res)
        return results[0][0, :8, :LANES]

    update("w_out", received["w_out"])
    update("w_down", received["w_down"])
    update("w_in", received["w_in_t"], transposed=True)
    big_updated = update("w_up", received["w_up_t"], transposed=True)

    as_rows = lambda a: a.reshape(n_layers, -1, LANES)
    results = _adamw([ex.finish(big_updated) for ex in sgu_w_parts], as_rows(sgu_w), as_rows(m_sgu_w), as_rows(v_sgu_w),
                     name="adamw_sgu_w")
    for pre, res in zip(("grad_", "delta_", "new_m_", "new_v_"), results):
        out[pre + "sgu_w"] = res.reshape(sgu_w.shape)

    per_layer_g_final = {pre: jnp.broadcast_to(given[pre + "g_final"], (n_layers,) + g_final.shape) for pre in ("", "m_", "v_")}
    stacks = [jnp.concatenate([_pack([given[pre + n] for n in small_sharded], lead=(n_layers,)),
                               _pack([given[pre + n] for n in layered] + [per_layer_g_final[pre]], lead=(n_layers,))], axis=1)
              for pre in ("", "m_", "v_")]
    results = _adamw([slabs[l].finish(big_updated) for l in layers], *stacks, name="adamw_small")
    for pre, res in zip(("grad_", "delta_", "new_m_", "new_v_"), results):
        unpacked = (_unpack(res[:, :n_own], [s[1:] for _, s in _SMALL_SHARDED], lead=(n_layers,))
                    + _unpack(res[:, n_own:], [s[1:] for n, s in _REPLICATED if n in layered] + [g_final.shape], lead=(n_layers,)))
        for n, a in zip(small_sharded + layered + ["g_final"], unpacked):
            out[pre + n] = a[0] if n == "g_final" else a

    loss = lax.psum(loss_tile[0, 0], ("x", "y", "c"))
    order = list(_WEIGHT_ORDER)
    return (loss, dres[0][None], *[out["grad_" + n] for n in order], *[out["delta_" + n] for n in order],
            *[out["new_m_" + n] for n in order], *[out["new_v_" + n] for n in order])


_WEIGHT_ORDER = ("g_mix", "w_in", "conv_w", "conv_b", "conv_ln_g", "conv_ln_b", "sgu_ln_g", "sgu_ln_b", "sgu_w", "sgu_b",
                 "g_out", "w_out", "g_ffn", "w_up", "ffn_conv_w", "ffn_conv_b", "w_down", "g_final")
```

```python
import math

import jax
import jax.numpy as jnp
from jax import lax
from jax.experimental import pallas as pl
from jax.experimental.pallas import tpu as pltpu

F32 = jnp.float32
BF16 = jnp.bfloat16

N_DEV = 8
D_MODEL = 1024
HEAD_DIM = 64
D_CONV = 256
D_SGU = 256
D_SB = 512
D_AB = 2 * D_CONV + 2 * D_SGU
CONV_K = 31
CONV_HALO = 32
FFN_K = 3
FFN_HALO = 8
D_FF = 2816
CHUNK = 128
EPS = 1e-6
LANES = 128

ADAM_LR = 0.001
ADAM_B1 = 0.9
ADAM_B2 = 0.999
ADAM_EPS = 1e-08
ADAM_WD = 0.01
ADAM_STEP = 10

VMEM_LIMIT = 56 * 1024 * 1024


def _cparams(sem=None):
    return pltpu.CompilerParams(dimension_semantics=sem, vmem_limit_bytes=VMEM_LIMIT)


def _sigmoid(x):
    return 1.0 / (1.0 + jnp.exp(-x))


_INV_SQRT2 = 1.0 / math.sqrt(2.0)
_INV_SQRT2PI = 1.0 / math.sqrt(2.0 * math.pi)


def _gelu(x):
    return 0.5 * x * (1.0 + lax.erf(x * _INV_SQRT2))


def _gelu_grad(x):
    return 0.5 * (1.0 + lax.erf(x * _INV_SQRT2)) + x * jnp.exp(-0.5 * x * x) * _INV_SQRT2PI


def _dot(a, b, dims):
    return lax.dot_general(a, b, (dims, ((), ())), preferred_element_type=F32)


_NN = ((1,), (0,))
_NT = ((1,), (1,))
_TN = ((0,), (0,))


def _split_bf16(x):
    hi = x.astype(BF16)
    lo = (x - hi.astype(F32)).astype(BF16)
    return jnp.concatenate([hi, lo], axis=1)


def _matmul(a, b, mode, *, name, out_dtype=F32, residual=None, n=None, b_n0=0, b_k0=0, rows=None, into=None):
    if mode == "nn":
        (m, k), n = a.shape, (n or b.shape[1])
    elif mode == "nt":
        (m, k), n = a.shape, (n or b.shape[0])
    else:
        (k, m), n = a.shape, b.shape[1]
    has_res = residual is not None
    tm, tn = _matmul_tiles(m, n, k, a.dtype.itemsize, b.dtype.itemsize, jnp.dtype(out_dtype).itemsize, has_res, b_n0)
    j0 = b_n0 // tn
    total_rows, first_row = rows or (m, 0)
    assert b_k0 % k == 0 and first_row % tm == 0
    kb, i0 = b_k0 // k, first_row // tm

    if mode == "nn":
        a_spec = pl.BlockSpec((tm, k), lambda i, j: (i, 0))
        b_spec = pl.BlockSpec((k, tn), lambda i, j: (kb, j + j0))
        dims = _NN
    elif mode == "nt":
        a_spec = pl.BlockSpec((tm, k), lambda i, j: (i, 0))
        b_spec = pl.BlockSpec((tn, k), lambda i, j: (j + j0, 0))
        dims = _NT
    else:
        a_spec = pl.BlockSpec((k, tm), lambda i, j: (0, i))
        b_spec = pl.BlockSpec((k, tn), lambda i, j: (0, j))
        dims = _TN
    o_spec = pl.BlockSpec((tm, tn), lambda i, j: (i + i0, j))
    r_spec = pl.BlockSpec((tm, tn), lambda i, j: (i, j))

    def body(*refs):
        a_ref, b_ref = refs[:2]
        acc = _dot(a_ref[...].astype(BF16), b_ref[...].astype(BF16), dims)
        if has_res:
            acc = acc + refs[2][...]
        refs[-1][...] = acc.astype(out_dtype)

    in_specs = [a_spec, b_spec] + ([r_spec] if has_res else [])
    args = (a, b) + ((residual,) if has_res else ())
    aliases = {}
    if into is not None:
        aliases = {len(args): 0}
        in_specs.append(pl.BlockSpec(memory_space=pl.ANY))
        args += (into,)

        def body(*refs, inner=body):
            inner(*refs[:len(args) - 1], refs[-1])

    return pl.pallas_call(
        body,
        name=name,
        grid=(m // tm, n // tn),
        in_specs=in_specs,
        out_specs=o_spec,
        out_shape=jax.ShapeDtypeStruct((total_rows, n), out_dtype),
        input_output_aliases=aliases,
        compiler_params=_cparams(("parallel", "parallel")),
    )(*args)


MATMUL_VMEM_BUDGET = 40 * 1024 * 1024


def _matmul_tiles(m, n, k, a_bytes, b_bytes, out_bytes, has_res, n_offset):
    def divisors(size, cap, also=0):
        return [t for t in range(cap, 0, -LANES) if size % t == 0 and also % t == 0] or [size]

    for tm in divisors(m, 256 if (has_res and k > 2048) else 1024):
        for tn in divisors(n, 1408, n_offset):
            blocks = tm * k * a_bytes + k * tn * b_bytes + tm * tn * (out_bytes + (4 if has_res else 0))
            if 2 * blocks <= MATMUL_VMEM_BUDGET:
                return tm, tn
    raise ValueError(f"no matmul tiling for {m} x {n} x {k}")


ROW_TILE = 512


def _rmsnorm_fwd(x, g, *, name):
    s, d = x.shape

    def body(x_ref, g_ref, h_ref):
        xv = x_ref[...]
        r = lax.rsqrt(jnp.mean(xv * xv, axis=-1, keepdims=True) + EPS)
        h_ref[...] = (xv * r * g_ref[...]).astype(BF16)

    return pl.pallas_call(
        body,
        name=name,
        grid=(s // ROW_TILE,),
        in_specs=[pl.BlockSpec((ROW_TILE, d), lambda i: (i, 0)), pl.BlockSpec((1, d), lambda i: (0, 0))],
        out_specs=pl.BlockSpec((ROW_TILE, d), lambda i: (i, 0)),
        out_shape=jax.ShapeDtypeStruct((s, d), BF16),
        compiler_params=_cparams(("parallel",)),
    )(x, g)


def _rmsnorm_bwd(x, g, dh, dres, *, name):
    s, d = x.shape

    def body(x_ref, g_ref, dh_ref, dres_ref, dx_ref, dxb_ref, dg_ref):
        xv = x_ref[...]
        r = lax.rsqrt(jnp.mean(xv * xv, axis=-1, keepdims=True) + EPS)
        xhat = xv * r
        dhv = dh_ref[...]
        dxhat = dhv * g_ref[...]
        dx = dres_ref[...] + r * (dxhat - xhat * jnp.mean(dxhat * xhat, axis=-1, keepdims=True))
        dx_ref[...] = dx
        dxb_ref[...] = dx.astype(BF16)
        part = jnp.sum(dhv * xhat, axis=0, keepdims=True)

        @pl.when(pl.program_id(0) == 0)
        def _():
            dg_ref[...] = part

        @pl.when(pl.program_id(0) > 0)
        def _():
            dg_ref[...] += part

    row = pl.BlockSpec((ROW_TILE, d), lambda i: (i, 0))
    vec = pl.BlockSpec((1, d), lambda i: (0, 0))
    return pl.pallas_call(
        body,
        name=name,
        grid=(s // ROW_TILE,),
        in_specs=[row, vec, row, row],
        out_specs=[row, row, vec],
        out_shape=[jax.ShapeDtypeStruct((s, d), F32), jax.ShapeDtypeStruct((s, d), BF16),
                   jax.ShapeDtypeStruct((1, d), F32)],
        compiler_params=_cparams(("arbitrary",)),
    )(x, g, dh, dres)


def _loss_head(x, g, target, *, name):
    s, d = x.shape

    def body(x_ref, g_ref, t_ref, loss_ref, dx_ref, dxb_ref, dg_ref):
        xv = x_ref[...]
        gv = g_ref[...]
        r = lax.rsqrt(jnp.mean(xv * xv, axis=-1, keepdims=True) + EPS)
        xhat = xv * r
        diff = xhat * gv - t_ref[...]
        dy = diff * (1.0 / d)
        dxhat = dy * gv
        dx = r * (dxhat - xhat * jnp.mean(dxhat * xhat, axis=-1, keepdims=True))
        dx_ref[...] = dx
        dxb_ref[...] = dx.astype(BF16)
        dg_part = jnp.sum(dy * xhat, axis=0, keepdims=True)
        row_loss = jnp.sum(diff * diff, axis=-1, keepdims=True)
        loss_part = jnp.sum(row_loss, axis=0, keepdims=True) * (0.5 / d)

        @pl.when(pl.program_id(0) == 0)
        def _():
            dg_ref[...] = dg_part
            loss_ref[...] = jnp.broadcast_to(loss_part, loss_ref.shape)

        @pl.when(pl.program_id(0) > 0)
        def _():
            dg_ref[...] += dg_part
            loss_ref[...] += jnp.broadcast_to(loss_part, loss_ref.shape)

    row = pl.BlockSpec((ROW_TILE, d), lambda i: (i, 0))
    vec = pl.BlockSpec((1, d), lambda i: (0, 0))
    tile = pl.BlockSpec((8, LANES), lambda i: (0, 0))
    return pl.pallas_call(
        body,
        name=name,
        grid=(s // ROW_TILE,),
        in_specs=[row, vec, row],
        out_specs=[tile, row, row, vec],
        out_shape=[jax.ShapeDtypeStruct((8, LANES), F32), jax.ShapeDtypeStruct((s, d), F32),
                   jax.ShapeDtypeStruct((s, d), BF16), jax.ShapeDtypeStruct((1, d), F32)],
        compiler_params=_cparams(("arbitrary",)),
    )(x, g, target)


_BRANCHES = ((0, D_CONV), (D_CONV, D_SGU), (D_CONV + D_SGU, D_SB))


def _combine_fwd(ya, yb, yc, g, *, name):
    s = ya.shape[0]

    def body(ya_ref, yb_ref, yc_ref, g_ref, y_ref):
        for ref, (off, w) in zip((ya_ref, yb_ref, yc_ref), _BRANCHES):
            v = ref[...]
            r = lax.rsqrt(jnp.mean(v * v, axis=-1, keepdims=True) + EPS)
            y_ref[:, off:off + w] = (v * r * g_ref[:, off:off + w]).astype(BF16)

    def row(w):
        return pl.BlockSpec((ROW_TILE, w), lambda i: (i, 0))

    return pl.pallas_call(
        body,
        name=name,
        grid=(s // ROW_TILE,),
        in_specs=[row(D_CONV), row(D_SGU), row(D_SB), pl.BlockSpec((1, D_MODEL), lambda i: (0, 0))],
        out_specs=row(D_MODEL),
        out_shape=jax.ShapeDtypeStruct((s, D_MODEL), BF16),
        compiler_params=_cparams(("parallel",)),
    )(ya, yb, yc, g)


def _combine_bwd(dy, ya, yb, yc, g, *, name):
    s = ya.shape[0]

    def body(dy_ref, ya_ref, yb_ref, yc_ref, g_ref, dya_ref, dyb_ref, dyc_ref, dg_ref):
        first = pl.program_id(0) == 0
        for ref, dref, (off, w) in zip((ya_ref, yb_ref, yc_ref), (dya_ref, dyb_ref, dyc_ref), _BRANCHES):
            v = ref[...]
            r = lax.rsqrt(jnp.mean(v * v, axis=-1, keepdims=True) + EPS)
            n = v * r
            dout = dy_ref[:, off:off + w]
            dn = dout * g_ref[:, off:off + w]
            dref[...] = r * (dn - n * jnp.mean(dn * n, axis=-1, keepdims=True))
            part = jnp.sum(dout * n, axis=0, keepdims=True)

            @pl.when(first)
            def _():
                dg_ref[:, off:off + w] = part

            @pl.when(jnp.logical_not(first))
            def _():
                dg_ref[:, off:off + w] += part

    def row(w):
        return pl.BlockSpec((ROW_TILE, w), lambda i: (i, 0))

    vec = pl.BlockSpec((1, D_MODEL), lambda i: (0, 0))
    return pl.pallas_call(
        body,
        name=name,
        grid=(s // ROW_TILE,),
        in_specs=[row(D_MODEL), row(D_CONV), row(D_SGU), row(D_SB), vec],
        out_specs=[row(D_CONV), row(D_SGU), row(D_SB), vec],
        out_shape=[jax.ShapeDtypeStruct((s, D_CONV), F32), jax.ShapeDtypeStruct((s, D_SGU), F32),
                   jax.ShapeDtypeStruct((s, D_SB), F32), jax.ShapeDtypeStruct((1, D_MODEL), F32)],
        compiler_params=_cparams(("arbitrary",)),
    )(dy, ya, yb, yc, g)


CONV_TILE = 128


def _shift_down(window, j, halo):
    return pltpu.roll(window, j, 0)[halo:, :] if j else window[halo:, :]


def _shift_up(window, j, n_out):
    n = window.shape[0]
    return pltpu.roll(window, n - j, 0)[:n_out, :] if j else window[:n_out, :]


def _mixer_a_fwd(p_ab, conv_w, conv_b, ln_g, ln_b, *, name):
    s = p_ab.shape[0]
    nt = s // CONV_TILE

    def body(p_ref, w_ref, b_ref, g_ref, beta_ref, y_ref, h_ref):
        h_ref[0:CONV_HALO, :] = jnp.zeros((CONV_HALO, D_CONV), F32)

        def glu(i, c):
            t0 = pl.multiple_of(i * CONV_TILE, CONV_TILE)
            a = p_ref[pl.ds(t0, CONV_TILE), 0:D_CONV].astype(F32)
            gate = p_ref[pl.ds(t0, CONV_TILE), D_CONV:2 * D_CONV].astype(F32)
            h_ref[pl.ds(t0 + CONV_HALO, CONV_TILE), :] = a * _sigmoid(gate)
            return c

        lax.fori_loop(0, nt, glu, 0)

        def conv(i, c):
            t0 = pl.multiple_of(i * CONV_TILE, CONV_TILE)
            window = h_ref[pl.ds(t0, CONV_TILE + CONV_HALO), :]
            acc = jnp.zeros((CONV_TILE, D_CONV), F32) + b_ref[...]
            for k in range(CONV_K):
                acc = acc + w_ref[k:k + 1, :] * _shift_down(window, CONV_K - 1 - k, CONV_HALO)
            mu = jnp.mean(acc, axis=-1, keepdims=True)
            xc = acc - mu
            rstd = lax.rsqrt(jnp.mean(xc * xc, axis=-1, keepdims=True) + EPS)
            z = xc * rstd * g_ref[...] + beta_ref[...]
            y_ref[pl.ds(t0, CONV_TILE), :] = z * _sigmoid(z)
            return c

        lax.fori_loop(0, nt, conv, 0)

    full = lambda shape: pl.BlockSpec(shape, lambda i: (0, 0))
    return pl.pallas_call(
        body,
        name=name,
        grid=(1,),
        in_specs=[full((s, 2 * D_CONV)), full((CONV_K, D_CONV)), full((1, D_CONV)), full((1, D_CONV)),
                  full((1, D_CONV))],
        out_specs=full((s, D_CONV)),
        out_shape=jax.ShapeDtypeStruct((s, D_CONV), F32),
        scratch_shapes=[pltpu.VMEM((s + CONV_HALO, D_CONV), F32)],
        compiler_params=_cparams(("arbitrary",)),
    )(p_ab, conv_w, conv_b, ln_g, ln_b)


def _mixer_a_bwd(p_ab, dya, conv_w, conv_b, ln_g, ln_b, *, name):
    s = p_ab.shape[0]
    nt = s // CONV_TILE

    def body(p_ref, dy_ref, w_ref, b_ref, g_ref, beta_ref, dp_ref, dw_ref, db_ref, dg_ref, dbeta_ref, h_ref, dc_ref):
        h_ref[0:CONV_HALO, :] = jnp.zeros((CONV_HALO, D_CONV), F32)
        dc_ref[s:s + CONV_HALO, :] = jnp.zeros((CONV_HALO, D_CONV), F32)
        dw_ref[...] = jnp.zeros_like(dw_ref)
        db_ref[...] = jnp.zeros_like(db_ref)
        dg_ref[...] = jnp.zeros_like(dg_ref)
        dbeta_ref[...] = jnp.zeros_like(dbeta_ref)

        def glu(i, c):
            t0 = pl.multiple_of(i * CONV_TILE, CONV_TILE)
            a = p_ref[pl.ds(t0, CONV_TILE), 0:D_CONV].astype(F32)
            gate = p_ref[pl.ds(t0, CONV_TILE), D_CONV:2 * D_CONV].astype(F32)
            h_ref[pl.ds(t0 + CONV_HALO, CONV_TILE), :] = a * _sigmoid(gate)
            return c

        lax.fori_loop(0, nt, glu, 0)

        def conv_bwd(i, c):
            t0 = pl.multiple_of(i * CONV_TILE, CONV_TILE)
            window = h_ref[pl.ds(t0, CONV_TILE + CONV_HALO), :]
            taps = [_shift_down(window, CONV_K - 1 - k, CONV_HALO) for k in range(CONV_K)]
            acc = jnp.zeros((CONV_TILE, D_CONV), F32) + b_ref[...]
            for k in range(CONV_K):
                acc = acc + w_ref[k:k + 1, :] * taps[k]
            mu = jnp.mean(acc, axis=-1, keepdims=True)
            xc = acc - mu
            rstd = lax.rsqrt(jnp.mean(xc * xc, axis=-1, keepdims=True) + EPS)
            xhat = xc * rstd
            z = xhat * g_ref[...] + beta_ref[...]
            sg = _sigmoid(z)
            dz = dy_ref[pl.ds(t0, CONV_TILE), :] * (sg * (1.0 + z * (1.0 - sg)))
            dg_ref[...] += jnp.sum(dz * xhat, axis=0, keepdims=True)
            dbeta_ref[...] += jnp.sum(dz, axis=0, keepdims=True)
            dxhat = dz * g_ref[...]
            dc = rstd * (dxhat - jnp.mean(dxhat, axis=-1, keepdims=True)
                         - xhat * jnp.mean(dxhat * xhat, axis=-1, keepdims=True))
            dc_ref[pl.ds(t0, CONV_TILE), :] = dc
            db_ref[...] += jnp.sum(dc, axis=0, keepdims=True)
            for k in range(CONV_K):
                dw_ref[k:k + 1, :] += jnp.sum(dc * taps[k], axis=0, keepdims=True)
            return c

        lax.fori_loop(0, nt, conv_bwd, 0)

        def glu_bwd(i, c):
            t0 = pl.multiple_of(i * CONV_TILE, CONV_TILE)
            window = dc_ref[pl.ds(t0, CONV_TILE + CONV_HALO), :]
            dh = jnp.zeros((CONV_TILE, D_CONV), F32)
            for j in range(CONV_K):
                dh = dh + w_ref[CONV_K - 1 - j:CONV_K - j, :] * _shift_up(window, j, CONV_TILE)
            a = p_ref[pl.ds(t0, CONV_TILE), 0:D_CONV].astype(F32)
            sg = _sigmoid(p_ref[pl.ds(t0, CONV_TILE), D_CONV:2 * D_CONV].astype(F32))
            dp_ref[pl.ds(t0, CONV_TILE), 0:D_CONV] = (dh * sg).astype(BF16)
            dp_ref[pl.ds(t0, CONV_TILE), D_CONV:2 * D_CONV] = (dh * a * sg * (1.0 - sg)).astype(BF16)
            return c

        lax.fori_loop(0, nt, glu_bwd, 0)

    full = lambda shape: pl.BlockSpec(shape, lambda i: (0, 0))
    vec = jax.ShapeDtypeStruct((1, D_CONV), F32)
    return pl.pallas_call(
        body,
        name=name,
        grid=(1,),
        in_specs=[full((s, 2 * D_CONV)), full((s, D_CONV)), full((CONV_K, D_CONV)), full((1, D_CONV)),
                  full((1, D_CONV)), full((1, D_CONV))],
        out_specs=[full((s, 2 * D_CONV)), full((CONV_K, D_CONV)), full((1, D_CONV)), full((1, D_CONV)),
                   full((1, D_CONV))],
        out_shape=[jax.ShapeDtypeStruct((s, 2 * D_CONV), BF16), jax.ShapeDtypeStruct((CONV_K, D_CONV), F32),
                   vec, vec, vec],
        scratch_shapes=[pltpu.VMEM((s + CONV_HALO, D_CONV), F32), pltpu.VMEM((s + CONV_HALO, D_CONV), F32)],
        compiler_params=_cparams(("arbitrary",)),
    )(p_ab, dya, conv_w, conv_b, ln_g, ln_b)


N_SGU_HEADS = D_SGU // HEAD_DIM


def _head_masks(width):
    lane = lax.broadcasted_iota(jnp.int32, (1, width), 1)
    return [(lane >= h * HEAD_DIM) & (lane < (h + 1) * HEAD_DIM) for h in range(width // HEAD_DIM)]


def _tril_mask():
    r = lax.broadcasted_iota(jnp.int32, (CHUNK, CHUNK), 0)
    c = lax.broadcasted_iota(jnp.int32, (CHUNK, CHUNK), 1)
    return c <= r


def _sgu_norm(bv, g, beta):
    vg = _gelu(bv)
    mu = jnp.mean(vg, axis=-1, keepdims=True)
    xc = vg - mu
    rstd = lax.rsqrt(jnp.mean(xc * xc, axis=-1, keepdims=True) + EPS)
    xhat = xc * rstd
    return xhat, rstd, xhat * g + beta


def _sgu_fwd(p_ab, ln_g, ln_b, w_s, bias, *, name):
    s = p_ab.shape[0]

    def body(p_ref, g_ref, beta_ref, w_ref, bias_ref, y_ref):
        u = _gelu(p_ref[:, 0:D_SGU].astype(F32))
        _, _, vn = _sgu_norm(p_ref[:, D_SGU:2 * D_SGU].astype(F32), g_ref[...], beta_ref[...])
        vb = vn.astype(BF16)
        tril = _tril_mask()
        mixed = bias_ref[...]
        for h, m in enumerate(_head_masks(D_SGU)):
            wh = jnp.where(tril, w_ref[h], 0.0).astype(BF16)
            mixed = mixed + _dot(wh, jnp.where(m, vb, jnp.zeros_like(vb)), _NN)
        y_ref[...] = u * mixed

    return pl.pallas_call(
        body,
        name=name,
        grid=(s // CHUNK,),
        in_specs=[pl.BlockSpec((CHUNK, 2 * D_SGU), lambda i: (i, 1)),
                  pl.BlockSpec((1, D_SGU), lambda i: (0, 0)), pl.BlockSpec((1, D_SGU), lambda i: (0, 0)),
                  pl.BlockSpec((N_SGU_HEADS, CHUNK, CHUNK), lambda i: (0, 0, 0)),
                  pl.BlockSpec((CHUNK, D_SGU), lambda i: (0, 0))],
        out_specs=pl.BlockSpec((CHUNK, D_SGU), lambda i: (i, 0)),
        out_shape=jax.ShapeDtypeStruct((s, D_SGU), F32),
        compiler_params=_cparams(("parallel",)),
    )(p_ab, ln_g, ln_b, w_s, bias)


def _sgu_bwd(p_ab, dyb, ln_g, ln_b, w_s, bias, *, name):
    s = p_ab.shape[0]
    n_chunks = s // CHUNK

    def body(p_ref, dy_ref, g_ref, beta_ref, w_ref, bias_ref, dp_ref, dw_ref, db_ref, dg_ref, dbeta_ref, dbias_ref):
        @pl.when(pl.program_id(0) == 0)
        def _():
            dw_ref[...] = jnp.zeros_like(dw_ref)
            dbias_ref[...] = jnp.zeros_like(dbias_ref)
            dg_ref[...] = jnp.zeros_like(dg_ref)
            dbeta_ref[...] = jnp.zeros_like(dbeta_ref)

        bu = p_ref[:, 0:D_SGU].astype(F32)
        bv = p_ref[:, D_SGU:2 * D_SGU].astype(F32)
        u = _gelu(bu)
        gv = g_ref[...]
        xhat, rstd, vn = _sgu_norm(bv, gv, beta_ref[...])
        vb = vn.astype(BF16)
        tril = _tril_mask()
        masks = _head_masks(D_SGU)
        whs = [jnp.where(tril, w_ref[h], 0.0).astype(BF16) for h in range(N_SGU_HEADS)]
        mixed = bias_ref[...]
        for h, m in enumerate(masks):
            mixed = mixed + _dot(whs[h], jnp.where(m, vb, jnp.zeros_like(vb)), _NN)
        dy = dy_ref[...]
        dp_ref[:, 0:D_SGU] = (dy * mixed * _gelu_grad(bu)).astype(BF16)
        dmixed = dy * u
        dbias_ref[...] += dmixed
        dmb = dmixed.astype(BF16)
        dvn = jnp.zeros((CHUNK, D_SGU), F32)
        for h, m in enumerate(masks):
            dmh = jnp.where(m, dmb, jnp.zeros_like(dmb))
            dvn = dvn + _dot(whs[h], dmh, _TN)
            dw_ref[h] += jnp.where(tril, _dot(dmh, vb, _NT), 0.0)
        dg_ref[...] += jnp.sum(dvn * xhat, axis=0, keepdims=True)
        dbeta_ref[...] += jnp.sum(dvn, axis=0, keepdims=True)
        dxhat = dvn * gv
        dvg = rstd * (dxhat - jnp.mean(dxhat, axis=-1, keepdims=True)
                      - xhat * jnp.mean(dxhat * xhat, axis=-1, keepdims=True))
        dp_ref[:, D_SGU:2 * D_SGU] = (dvg * _gelu_grad(bv)).astype(BF16)

        @pl.when(pl.program_id(0) == n_chunks - 1)
        def _():
            chan = lax.broadcasted_iota(jnp.int32, (D_SGU, LANES), 0)
            head = lax.broadcasted_iota(jnp.int32, (D_SGU, LANES), 1)
            to_head = jnp.where(chan // HEAD_DIM == head, 1.0, 0.0).astype(BF16)
            db_ref[...] = _dot(_split_bf16(dbias_ref[...]), jnp.concatenate([to_head, to_head], axis=0), _NN)

    vec = pl.BlockSpec((1, D_SGU), lambda i: (0, 0))
    wspec = pl.BlockSpec((N_SGU_HEADS, CHUNK, CHUNK), lambda i: (0, 0, 0))
    bspec = pl.BlockSpec((CHUNK, D_SGU), lambda i: (0, 0))
    return pl.pallas_call(
        body,
        name=name,
        grid=(n_chunks,),
        in_specs=[pl.BlockSpec((CHUNK, 2 * D_SGU), lambda i: (i, 1)), pl.BlockSpec((CHUNK, D_SGU), lambda i: (i, 0)),
                  vec, vec, wspec, bspec],
        out_specs=[pl.BlockSpec((CHUNK, 2 * D_SGU), lambda i: (i, 0)), wspec,
                   pl.BlockSpec((CHUNK, LANES), lambda i: (0, 0)), vec, vec],
        out_shape=[jax.ShapeDtypeStruct((s, 2 * D_SGU), BF16),
                   jax.ShapeDtypeStruct((N_SGU_HEADS, CHUNK, CHUNK), F32),
                   jax.ShapeDtypeStruct((CHUNK, LANES), F32),
                   jax.ShapeDtypeStruct((1, D_SGU), F32), jax.ShapeDtypeStruct((1, D_SGU), F32)],
        scratch_shapes=[pltpu.VMEM((CHUNK, D_SGU), F32)],
        compiler_params=_cparams(("arbitrary",)),
    )(p_ab, dyb, ln_g, ln_b, w_s, bias)


N_PAIRS = D_SB // LANES
QKV_BLOCK0 = D_AB // LANES
SB_SCALE = HEAD_DIM ** -0.5


def _sb_logits(z, valid):
    nz = -z
    t = jnp.log(1.0 + jnp.exp(jnp.minimum(z, nz)))
    l1 = jnp.minimum(nz, 0.0) - t
    if valid is not None:
        l1 = jnp.where(valid, l1, 0.0)
    return l1, jnp.minimum(z, 0.0) - t


def _split_hi_lo(x):
    hi = lax.bitcast_convert_type(lax.bitcast_convert_type(x, jnp.uint32) & jnp.uint32(0xFFFF0000), F32)
    return jnp.concatenate([hi, x - hi], axis=1)


def _cumsum_operand(keep):
    half = jnp.concatenate([keep.astype(F32), jnp.ones((CHUNK, CHUNK), F32)], axis=1)
    return jnp.concatenate([half, half], axis=0)


Q_BLOCKS_PER_STEP = 4


def _q_blocks_per_step(nq):
    return next(n for n in (Q_BLOCKS_PER_STEP, 2, 1) if nq % n == 0)


def _attn_fwd(qkv, *, name):
    s = qkv.shape[0]
    nq = s // CHUNK
    per_step = _q_blocks_per_step(nq)

    def body(q_ref, k_ref, v_ref, o_ref, t_ref):
        masks = _head_masks(LANES)
        row = lax.broadcasted_iota(jnp.int32, (CHUNK, CHUNK), 0)
        col = lax.broadcasted_iota(jnp.int32, (CHUNK, CHUNK), 1)
        after_op = _cumsum_operand(row > col)
        cmr = col - row
        zc = jnp.zeros((CHUNK, LANES), F32)

        def q_block(sub, _):
            qi = pl.program_id(1) * per_step + sub
            q_rows = pl.ds(pl.multiple_of(sub * CHUNK, CHUNK), CHUNK)
            q = q_ref[q_rows, :] * SB_SCALE
            zero = jnp.zeros_like(q)
            qs = [jnp.where(m, q, zero) for m in masks]

            def blocks(js, carry):
                o, c0, c1 = carry
                kvs, valids = [], []
                for j in js:
                    k0 = pl.multiple_of(jnp.maximum(j, 0) * CHUNK, CHUNK)
                    kvs.append((k_ref[pl.ds(k0, CHUNK), :], v_ref[pl.ds(k0, CHUNK), :]))
                    valids.append(cmr < jnp.where(j >= 0, (qi - j) * CHUNK, -CHUNK))
                units = [(h, b) for b in range(len(js)) for h in range(2)]
                zs = [_dot(qs[h], kvs[b][0], _NT) for h, b in units]
                logits = [_sb_logits(z, valids[b]) for z, (h, b) in zip(zs, units)]
                sums = [_dot(_split_hi_lo(l1), after_op, _NN) for l1, _ in logits]
                cs = [c0, c1]
                probs = []
                for (h, b), (_, lb), sm in zip(units, logits, sums):
                    probs.append(jnp.where(valids[b], jnp.exp(lb + sm[:, :CHUNK] + cs[h]), 0.0))
                    cs[h] = cs[h] + sm[:, CHUNK:]
                for (h, b), a in zip(units, probs):
                    o = o + _dot(a.astype(BF16), jnp.where(masks[h], kvs[b][1], zero), _NN)
                return o, cs[0], cs[1]

            n_four = (qi + 1) // 4
            carry = lax.fori_loop(0, n_four, lambda jj, c: blocks([qi - 4 * jj - i for i in range(4)], c), (zc,) * 3)
            top = qi - 4 * n_four
            o, c0, c1 = lax.fori_loop(0, (top + 2) // 2, lambda jj, c: blocks([top - 2 * jj, top - 2 * jj - 1], c), carry)
            o_ref[q_rows, :] = o
            t_ref[q_rows, 0:LANES] = c0
            t_ref[q_rows, LANES:2 * LANES] = c1
            return 0

        lax.fori_loop(0, per_step, q_block, 0)

    rows = per_step * CHUNK
    return pl.pallas_call(
        body,
        name=name,
        grid=(N_PAIRS, nq // per_step),
        in_specs=[pl.BlockSpec((rows, LANES), lambda p, i: (i, QKV_BLOCK0 + p)),
                  pl.BlockSpec((s, LANES), lambda p, i: (0, QKV_BLOCK0 + N_PAIRS + p)),
                  pl.BlockSpec((s, LANES), lambda p, i: (0, QKV_BLOCK0 + 2 * N_PAIRS + p))],
        out_specs=[pl.BlockSpec((rows, LANES), lambda p, i: (i, p)),
                   pl.BlockSpec((rows, 2 * LANES), lambda p, i: (i, p))],
        out_shape=[jax.ShapeDtypeStruct((s, D_SB), F32), jax.ShapeDtypeStruct((s, 2 * D_SB), F32)],
        compiler_params=_cparams(("parallel", "parallel")),
    )(qkv, qkv, qkv)


def _attn_bwd(qkv, t_tot, do, *, name):
    s = qkv.shape[0]
    nq = s // CHUNK
    per_step = _q_blocks_per_step(nq)

    def body(q_ref, k_ref, v_ref, t_ref, do_ref, dq_ref, dk_ref, dv_ref):
        @pl.when(pl.program_id(1) == 0)
        def _():
            dk_ref[...] = jnp.zeros_like(dk_ref)
            dv_ref[...] = jnp.zeros_like(dv_ref)

        masks = _head_masks(LANES)
        row = lax.broadcasted_iota(jnp.int32, (CHUNK, CHUNK), 0)
        col = lax.broadcasted_iota(jnp.int32, (CHUNK, CHUNK), 1)
        upto_op = _cumsum_operand(row <= col)
        before_op = _cumsum_operand(row < col)
        cmr = col - row
        zc = jnp.zeros((CHUNK, LANES), F32)

        def q_block(sub, _):
            qi = pl.program_id(1) * per_step + sub
            q_rows = pl.ds(pl.multiple_of(sub * CHUNK, CHUNK), CHUNK)
            q = q_ref[q_rows, :] * SB_SCALE
            dob = do_ref[q_rows, :].astype(BF16)
            zero = jnp.zeros_like(q)
            qs = [jnp.where(m, q, zero) for m in masks]
            dos = [jnp.where(m, dob, zero) for m in masks]
            tots = [t_ref[q_rows, 0:LANES], t_ref[q_rows, LANES:2 * LANES]]

            def blocks(js, carry):
                dq, cl0, cl1, cp0, cp1 = carry
                starts = [pl.multiple_of(jnp.minimum(j, nq - 1) * CHUNK, CHUNK) for j in js]
                valids = [cmr < (qi - j) * CHUNK for j in js]
                kvs = [(k_ref[pl.ds(k0, CHUNK), :], v_ref[pl.ds(k0, CHUNK), :]) for k0 in starts]
                units = [(h, b) for b in range(len(js)) for h in range(2)]
                zs = [_dot(qs[h], kvs[b][0], _NT) for h, b in units]
                das = [_dot(dos[h], kvs[b][1], _NT) for h, b in units]
                logits = [_sb_logits(z, valids[b]) for z, (h, b) in zip(zs, units)]
                sums = [_dot(_split_hi_lo(l1), upto_op, _NN) for l1, _ in logits]
                cls, cps = [cl0, cl1], [cp0, cp1]
                probs, gs = [], []
                for (h, b), (_, lb), sm, da in zip(units, logits, sums, das):
                    a = jnp.where(valids[b], jnp.exp(lb + (tots[h] - cls[h] - sm[:, :CHUNK])), 0.0)
                    probs.append(a)
                    gs.append(a * da)
                    cls[h] = cls[h] + sm[:, CHUNK:]
                sums_g = [_dot(_split_hi_lo(g), before_op, _NN) for g in gs]
                dzs = []
                for (h, b), (_, lb), g, sg in zip(units, logits, gs, sums_g):
                    dz = g - (g + sg[:, :CHUNK] + cps[h]) * jnp.exp(lb)
                    dzs.append(jnp.where(valids[b], dz, 0.0).astype(BF16))
                    cps[h] = cps[h] + sg[:, CHUNK:]
                for (h, b), dzb in zip(units, dzs):
                    dq = dq + _dot(dzb, jnp.where(masks[h], kvs[b][0], zero), _NN)
                for b, k0 in enumerate(starts):
                    dk_ref[pl.ds(k0, CHUNK), :] += _dot(dzs[2 * b], qs[0], _TN) + _dot(dzs[2 * b + 1], qs[1], _TN)
                    dv_ref[pl.ds(k0, CHUNK), :] += (_dot(probs[2 * b].astype(BF16), dos[0], _TN)
                                                    + _dot(probs[2 * b + 1].astype(BF16), dos[1], _TN))
                return dq, cls[0], cls[1], cps[0], cps[1]

            n_four = (qi + 1) // 4
            carry = lax.fori_loop(0, n_four, lambda jj, c: blocks([4 * jj + i for i in range(4)], c), (zc,) * 5)
            base = 4 * n_four
            carry = lax.fori_loop(0, (qi - base + 2) // 2, lambda jj, c: blocks([base + 2 * jj, base + 2 * jj + 1], c), carry)
            dq_ref[q_rows, :] = (carry[0] * SB_SCALE).astype(BF16)
            return 0

        lax.fori_loop(0, per_step, q_block, 0)

    rows = per_step * CHUNK
    blk = pl.BlockSpec((rows, LANES), lambda p, i: (i, p))
    col_blk = pl.BlockSpec((s, LANES), lambda p, i: (0, p))
    out = jax.ShapeDtypeStruct((s, D_SB), F32)
    return pl.pallas_call(
        body,
        name=name,
        grid=(N_PAIRS, nq // per_step),
        in_specs=[pl.BlockSpec((rows, LANES), lambda p, i: (i, QKV_BLOCK0 + p)),
                  pl.BlockSpec((s, LANES), lambda p, i: (0, QKV_BLOCK0 + N_PAIRS + p)),
                  pl.BlockSpec((s, LANES), lambda p, i: (0, QKV_BLOCK0 + 2 * N_PAIRS + p)),
                  pl.BlockSpec((rows, 2 * LANES), lambda p, i: (i, p)),
                  blk],
        out_specs=[blk, col_blk, col_blk],
        out_shape=[jax.ShapeDtypeStruct((s, D_SB), BF16), out, out],
        compiler_params=_cparams(("parallel", "arbitrary")),
    )(qkv, qkv, qkv, t_tot, do)


FFN_TILE = 256
FFN_COLS = 256
N_FF_BLOCKS = D_FF // FFN_COLS


def _ffn_act_fwd(up0, conv_w, conv_b, *, name):
    s = up0.shape[0]
    nt = s // FFN_TILE

    def body(xg_ref, xv_ref, wg_ref, wv_ref, bg_ref, bv_ref, act_ref, pg_ref, pv_ref):
        pg_ref[0:FFN_HALO, :] = jnp.zeros((FFN_HALO, FFN_COLS), F32)
        pv_ref[0:FFN_HALO, :] = jnp.zeros((FFN_HALO, FFN_COLS), F32)
        pg_ref[FFN_HALO:, :] = xg_ref[...].astype(F32)
        pv_ref[FFN_HALO:, :] = xv_ref[...].astype(F32)

        def tile(i, c):
            t0 = pl.multiple_of(i * FFN_TILE, FFN_TILE)
            outs = []
            for p_ref, w_ref, b_ref in ((pg_ref, wg_ref, bg_ref), (pv_ref, wv_ref, bv_ref)):
                window = p_ref[pl.ds(t0, FFN_TILE + FFN_HALO), :]
                acc = b_ref[...] + w_ref[2:3, :] * window[FFN_HALO:, :]
                for j in range(1, FFN_K):
                    acc = acc + w_ref[FFN_K - 1 - j:FFN_K - j, :] * _shift_down(window, j, FFN_HALO)
                outs.append(acc)
            gate, val = outs
            act_ref[pl.ds(t0, FFN_TILE), :] = (gate * _sigmoid(gate) * val).astype(BF16)
            return c

        lax.fori_loop(0, nt, tile, 0)

    gcol = lambda rows: pl.BlockSpec((rows, FFN_COLS), lambda j: (0, j))
    vcol = lambda rows: pl.BlockSpec((rows, FFN_COLS), lambda j: (0, j + N_FF_BLOCKS))
    return pl.pallas_call(
        body,
        name=name,
        grid=(N_FF_BLOCKS,),
        in_specs=[gcol(s), vcol(s), gcol(FFN_K), vcol(FFN_K), gcol(1), vcol(1)],
        out_specs=gcol(s),
        out_shape=jax.ShapeDtypeStruct((s, D_FF), BF16),
        scratch_shapes=[pltpu.VMEM((s + FFN_HALO, FFN_COLS), F32), pltpu.VMEM((s + FFN_HALO, FFN_COLS), F32)],
        compiler_params=_cparams(("parallel",)),
    )(up0, up0, conv_w, conv_w, conv_b, conv_b)


def _ffn_act_bwd(up0, dact, conv_w, conv_b, *, name):
    s = up0.shape[0]
    nt = s // FFN_TILE

    def body(xg_ref, xv_ref, da_ref, wg_ref, wv_ref, bg_ref, bv_ref, dxg_ref, dxv_ref, dwg_ref, dwv_ref, dbg_ref, dbv_ref,
             pg_ref, pv_ref, dg_ref, dv_ref):
        zeros = jnp.zeros((FFN_HALO, FFN_COLS), F32)
        for p_ref, x_ref in ((pg_ref, xg_ref), (pv_ref, xv_ref)):
            p_ref[0:FFN_HALO, :] = zeros
            p_ref[FFN_HALO:, :] = x_ref[...].astype(F32)
        dg_ref[s:s + FFN_HALO, :] = zeros
        dv_ref[s:s + FFN_HALO, :] = zeros
        for ref in (dwg_ref, dwv_ref, dbg_ref, dbv_ref):
            ref[...] = jnp.zeros_like(ref)

        def conv(p_ref, w_ref, b_ref, t0):
            window = p_ref[pl.ds(t0, FFN_TILE + FFN_HALO), :]
            taps = [_shift_down(window, j, FFN_HALO) for j in range(FFN_K)]
            out = b_ref[...]
            for j in range(FFN_K):
                out = out + w_ref[FFN_K - 1 - j:FFN_K - j, :] * taps[j]
            return out, taps

        def tile(i, c):
            t0 = pl.multiple_of(i * FFN_TILE, FFN_TILE)
            gate, taps_g = conv(pg_ref, wg_ref, bg_ref, t0)
            val, taps_v = conv(pv_ref, wv_ref, bv_ref, t0)
            da = da_ref[pl.ds(t0, FFN_TILE), :].astype(F32)
            sg = lax.logistic(gate)
            dgate = da * val * (sg * (1.0 + gate * (1.0 - sg)))
            dval = da * gate * sg
            dg_ref[pl.ds(t0, FFN_TILE), :] = dgate
            dv_ref[pl.ds(t0, FFN_TILE), :] = dval
            dbg_ref[...] += jnp.sum(dgate, axis=0, keepdims=True)
            dbv_ref[...] += jnp.sum(dval, axis=0, keepdims=True)
            for j in range(FFN_K):
                dwg_ref[FFN_K - 1 - j:FFN_K - j, :] += jnp.sum(dgate * taps_g[j], axis=0, keepdims=True)
                dwv_ref[FFN_K - 1 - j:FFN_K - j, :] += jnp.sum(dval * taps_v[j], axis=0, keepdims=True)
            return c

        lax.fori_loop(0, nt, tile, 0)

        def tile_dx(i, c):
            t0 = pl.multiple_of(i * FFN_TILE, FFN_TILE)
            for d_ref, w_ref, dx_ref in ((dg_ref, wg_ref, dxg_ref), (dv_ref, wv_ref, dxv_ref)):
                window = d_ref[pl.ds(t0, FFN_TILE + FFN_HALO), :]
                dx = w_ref[FFN_K - 1:FFN_K, :] * window[:FFN_TILE, :]
                for j in range(1, FFN_K):
                    dx = dx + w_ref[FFN_K - 1 - j:FFN_K - j, :] * _shift_up(window, j, FFN_TILE)
                dx_ref[pl.ds(t0, FFN_TILE), :] = dx.astype(BF16)
            return c

        lax.fori_loop(0, nt, tile_dx, 0)

    gcol = lambda rows: pl.BlockSpec((rows, FFN_COLS), lambda j: (0, j))
    vcol = lambda rows: pl.BlockSpec((rows, FFN_COLS), lambda j: (0, j + N_FF_BLOCKS))
    half = lambda rows, dtype: jax.ShapeDtypeStruct((rows, D_FF), dtype)
    padded = pltpu.VMEM((s + FFN_HALO, FFN_COLS), F32)
    return pl.pallas_call(
        body,
        name=name,
        grid=(N_FF_BLOCKS,),
        in_specs=[gcol(s), vcol(s), gcol(s), gcol(FFN_K), vcol(FFN_K), gcol(1), vcol(1)],
        out_specs=[gcol(s), gcol(s), gcol(FFN_K), gcol(FFN_K), gcol(1), gcol(1)],
        out_shape=[half(s, BF16), half(s, BF16), half(FFN_K, F32), half(FFN_K, F32), half(1, F32), half(1, F32)],
        scratch_shapes=[padded, padded, padded, padded],
        compiler_params=_cparams(("parallel",)),
    )(up0, up0, dact, conv_w, conv_w, conv_b, conv_b)


MESH = pl.DeviceIdType.MESH


def _position():
    x, y, c = lax.axis_index("x"), lax.axis_index("y"), lax.axis_index("c")
    return x, y, c, 4 * x + 2 * y + c


def _peer(k):
    x, y, c, _ = _position()
    px = 1 - x if k & 4 else x
    py = 1 - y if k & 2 else y
    pc = 1 - c if k & 1 else c
    return (px, py, pc), 4 * px + 2 * py + pc


_HBM = pl.BlockSpec(memory_space=pltpu.HBM)
_SEM = pl.BlockSpec(memory_space=pltpu.SEMAPHORE)
_DATAFLOW = pltpu.SideEffectType.DATAFLOW_SIDE_EFFECTING
N_PEERS = N_DEV - 1


class _SplitExchange:
    def __init__(self, src, *, kind, name):
        self.kind, self.name, self.dtype = kind, name, src.dtype
        scatter = kind.startswith("scatter")
        by_blocks = kind.endswith("blocks")
        self.scatter, self.by_blocks = scatter, by_blocks
        if by_blocks:
            self.r, self.cols, self.land_shape = None, None, src.shape if scatter else (N_DEV,) + src.shape
        else:
            self.r = src.shape[0] // N_DEV if scatter else src.shape[0]
            self.cols = src.shape[1]
            self.land_shape = (N_DEV, self.r, self.cols) if scatter else (N_DEV * self.r, self.cols)
        r = self.r

        def copies(src_ref, land_ref, send_sems, recv_sems, local_sem):
            me = _position()[3]

            def rows(ref, idx):
                return ref.at[pl.ds(pl.multiple_of(idx * r, r), r), :]

            if not scatter:
                outgoing = lambda idx: src_ref
            else:
                outgoing = (lambda idx: src_ref.at[idx]) if by_blocks else (lambda idx: rows(src_ref, idx))
            slot = (lambda idx: land_ref.at[idx]) if (scatter or by_blocks) else (lambda idx: rows(land_ref, idx))
            sends, recvs = [], []
            for k in range(1, N_DEV):
                peer, pidx = _peer(k)
                sems = dict(send_sem=send_sems[k - 1], recv_sem=recv_sems[k - 1], device_id=peer, device_id_type=MESH)
                sends.append(pltpu.make_async_remote_copy(src_ref=outgoing(pidx), dst_ref=slot(me), **sems))
                recvs.append(pltpu.make_async_remote_copy(src_ref=outgoing(pidx), dst_ref=slot(pidx), **sems))
            return sends, recvs, pltpu.make_async_copy(outgoing(me), slot(me), local_sem)

        self._copies = copies
        self.src = src

    @staticmethod
    def start(exchanges, name):
        n = len(exchanges)
        per = 2 * N_PEERS + 1

        def start_body(*refs):
            outs = refs[2 * n:]
            for i, ex in enumerate(exchanges):
                sems = outs[per * i:per * (i + 1)]
                sends, _, local = ex._copies(refs[2 * i], refs[2 * i + 1], sems[:N_PEERS], sems[N_PEERS:-1], sems[-1])
                for cp in sends + [local]:
                    cp.start()
            outs[-1][...] = jnp.zeros_like(outs[-1])

        sem = pltpu.SemaphoreType.DMA(())
        operands, thru_shapes = [], []
        for ex in exchanges:
            operands += [pltpu.with_memory_space_constraint(ex.src, pltpu.HBM),
                         pltpu.with_memory_space_constraint(lax.empty(ex.land_shape, ex.dtype), pltpu.HBM)]
            thru_shapes += [pltpu.HBM(ex.src.shape, ex.dtype), pltpu.HBM(ex.land_shape, ex.dtype)]
        out = pl.pallas_call(
            start_body,
            name=name,
            in_specs=(_HBM,) * (2 * n),
            out_specs=(_SEM,) * (per * n) + (_HBM,) * (2 * n) + (pl.BlockSpec(memory_space=pltpu.VMEM),),
            out_shape=(sem,) * (per * n) + tuple(thru_shapes) + (jax.ShapeDtypeStruct((8, LANES), F32),),
            input_output_aliases={i: per * n + i for i in range(2 * n)},
            compiler_params=pltpu.CompilerParams(has_side_effects=_DATAFLOW),
        )(*operands)
        for i, ex in enumerate(exchanges):
            ex.sems = out[per * i:per * (i + 1)]
            ex.src_thru, ex.land_thru = out[per * n + 2 * i], out[per * n + 2 * i + 1]
        return out[-1][0, 0]

    def finish(self, after):
        copies = self._copies

        def wait_body(src_ref, land_ref, *rest):
            sends, recvs, local = copies(src_ref, land_ref, rest[:N_PEERS], rest[N_PEERS:2 * N_PEERS], rest[2 * N_PEERS])
            for cp in sends:
                cp.wait_send()
            for cp in recvs:
                cp.wait_recv()
            local.wait()

        return pl.pallas_call(
            wait_body,
            name=f"{self.name}_wait",
            in_specs=(_HBM, _HBM) + (_SEM,) * (2 * N_PEERS + 1) + (pl.BlockSpec(memory_space=pl.ANY),),
            out_specs=(_HBM, _HBM),
            out_shape=(pltpu.HBM(self.src_thru.shape, self.dtype), pltpu.HBM(self.land_shape, self.dtype)),
            input_output_aliases={0: 0, 1: 1},
            compiler_params=pltpu.CompilerParams(has_side_effects=_DATAFLOW),
        )(self.src_thru, self.land_thru, *self.sems, after)[1]


ADAM_MAX_ROWS = 352
BF16_SUBLANES = 16


def _row_tile(rows):
    fitting = [t for t in range(BF16_SUBLANES, ADAM_MAX_ROWS + 1, BF16_SUBLANES) if rows % t == 0]
    return max(fitting) if fitting else rows


def _layer_parts_specs(n_layers, n_parts, tr, cols):
    return [pl.BlockSpec((n_parts, tr, cols), lambda l, i, j=j: (0, jnp.where(l == j, i, 0), 0)) for j in range(n_layers)]


def _select_layer_sum(p_refs):
    l = pl.program_id(0)
    g = None
    for j, p_ref in enumerate(p_refs):
        gj = p_ref[0].astype(F32)
        for k in range(1, p_ref.shape[0]):
            gj = gj + p_ref[k].astype(F32)
        g = gj if g is None else jnp.where(l == j, gj, g)
    return g


def _adamw(parts, w, m, v, *, name):
    n_layers, rows, cols = w.shape
    tr = _row_tile(rows)

    def body(*refs):
        w_ref, m_ref, v_ref, g_ref, d_ref, m2_ref, v2_ref = refs[n_layers:]
        g = _select_layer_sum(refs[:n_layers])
        m2 = ADAM_B1 * m_ref[...] + (1.0 - ADAM_B1) * g
        v2 = ADAM_B2 * v_ref[...] + (1.0 - ADAM_B2) * (g * g)
        m_hat = m2 / (1.0 - ADAM_B1 ** ADAM_STEP)
        v_hat = v2 / (1.0 - ADAM_B2 ** ADAM_STEP)
        g_ref[...] = g
        d_ref[...] = -ADAM_LR * (m_hat / (jnp.sqrt(v_hat) + ADAM_EPS) + ADAM_WD * w_ref[...])
        m2_ref[...] = m2
        v2_ref[...] = v2

    slab = pl.BlockSpec((None, tr, cols), lambda l, i: (l, i, 0))
    out = jax.ShapeDtypeStruct((n_layers, rows, cols), F32)
    p_specs = _layer_parts_specs(n_layers, parts[0].shape[0], tr, cols)
    return pl.pallas_call(
        body,
        name=name,
        grid=(n_layers, rows // tr),
        in_specs=p_specs + [slab, slab, slab],
        out_specs=[slab, slab, slab, slab],
        out_shape=[out, out, out, out],
        compiler_params=_cparams(("arbitrary", "arbitrary")),
    )(*parts, w, m, v)


SLAB_ROWS = 32
_SMALL_SHARDED = (("conv_w", (2, 31, 32)), ("ffn_conv_w", (2, 3, 704)))
_REPLICATED = (("g_mix", (2, 1024)), ("conv_b", (2, 256)), ("conv_ln_g", (2, 256)), ("conv_ln_b", (2, 256)),
               ("sgu_ln_g", (2, 256)), ("sgu_ln_b", (2, 256)), ("sgu_w", (2, 4, 128, 128)), ("sgu_b", (2, 4, 128)),
               ("g_out", (2, 1024)), ("g_ffn", (2, 1024)), ("ffn_conv_b", (2, 5632)), ("g_final", (1024,)))


def _seg_rows(n_elems):
    return -(-n_elems // LANES)


def _pack(arrays, lead=()):
    segs = []
    for a in arrays:
        flat = a.reshape(lead + (-1,)).astype(F32)
        pad = _seg_rows(flat.shape[-1]) * LANES - flat.shape[-1]
        if pad:
            flat = jnp.pad(flat, [(0, 0)] * len(lead) + [(0, pad)])
        segs.append(flat)
    flat = jnp.concatenate(segs, axis=-1)
    rows = flat.shape[-1] // LANES
    pad_rows = -rows % SLAB_ROWS
    if pad_rows:
        flat = jnp.pad(flat, [(0, 0)] * len(lead) + [(0, pad_rows * LANES)])
    return flat.reshape(lead + (rows + pad_rows, LANES))


def _unpack(slab, shapes, lead=()):
    flat = slab.reshape(lead + (-1,))
    out, off = [], 0
    for shape in shapes:
        n = math.prod(shape)
        out.append(flat[..., off:off + n].reshape(lead + tuple(shape)))
        off += _seg_rows(n) * LANES
    return out


def _split_last(full):
    split = full.shape[:-1] + (N_DEV, full.shape[-1] // N_DEV)
    return jnp.moveaxis(full.reshape(split), -2, 0)


def _join_last(blocks):
    moved = jnp.moveaxis(blocks, 0, -2)
    return moved.reshape(moved.shape[:-2] + (moved.shape[-2] * moved.shape[-1],))


def _gathered(wt, n, l, after):
    if isinstance(wt[n][l], _SplitExchange):
        wt[n][l] = wt[n][l].finish(after)
    return wt[n][l]


def _layer_fwd(l, x, wt, small):
    tag = f"l{l}"
    h = _rmsnorm_fwd(x, small["g_mix"][l][None], name=f"{tag}_norm_mix")
    w_in_t = _gathered(wt, "w_in_t", l, h)
    p = _matmul(h, w_in_t, "nt", name=f"{tag}_proj", out_dtype=BF16)
    p_ab = qkv = p
    ya = _mixer_a_fwd(p_ab, wt["conv_w"][l], small["conv_b"][l][None], small["conv_ln_g"][l][None],
                      small["conv_ln_b"][l][None], name=f"{tag}_mixer_a")
    bias = jnp.repeat(small["sgu_b"][l].T, HEAD_DIM, axis=1)
    yb = _sgu_fwd(p_ab, small["sgu_ln_g"][l][None], small["sgu_ln_b"][l][None], small["sgu_w"][l], bias,
                  name=f"{tag}_sgu")
    yc, t_tot = _attn_fwd(qkv, name=f"{tag}_attn")
    y = _combine_fwd(ya, yb, yc, small["g_out"][l][None], name=f"{tag}_combine")
    x1 = _matmul(y, _gathered(wt, "w_out", l, y), "nn", name=f"{tag}_out_proj", residual=x)
    h2 = _rmsnorm_fwd(x1, small["g_ffn"][l][None], name=f"{tag}_norm_ffn")
    up0 = _matmul(h2, _gathered(wt, "w_up_t", l, h2), "nt", name=f"{tag}_up", out_dtype=BF16)
    act = _ffn_act_fwd(up0, wt["ffn_conv_w"][l], small["ffn_conv_b"][l][None], name=f"{tag}_ffn_act")
    x2 = _matmul(act, _gathered(wt, "w_down", l, act), "nn", name=f"{tag}_down", residual=x1)
    saved = dict(x=x, h=h, p_ab=p, qkv=p, ya=ya, yb=yb, yc=yc, t_tot=t_tot, y=y, x1=x1, h2=h2, up0=up0,
                 act=act, bias=bias)
    return x2, saved


def _layer_bwd(l, dres, sv, wt, small, scattering, token):
    tag = f"l{l}b"
    g = {}

    def scatter(n, partial):
        scattering[n][l] = _SplitExchange(partial, kind="scatter_rows", name=f"scatter_{n}_l{l}")
        return _SplitExchange.start([scattering[n][l]], name=f"scatter_{n}_l{l}_start")

    dx2, dx2_b = dres
    dact = _matmul(dx2_b, wt["w_down"][l], "nt", name=f"{tag}_dact", out_dtype=BF16)
    tok = scatter("w_down", _matmul(sv["act"], dx2_b, "tn", name=f"{tag}_dw_down", out_dtype=BF16))
    dup_g, dup_v, dwg, dwv, dbg, dbv = _ffn_act_bwd(sv["up0"], dact, wt["ffn_conv_w"][l], small["ffn_conv_b"][l][None] + tok + token,
                                                    name=f"{tag}_ffn_act")
    g["ffn_conv_w"] = jnp.concatenate([dwg, dwv], axis=1)
    g["ffn_conv_b"] = jnp.concatenate([dbg[0], dbv[0]])
    dh2 = _matmul(dup_g, wt["w_up_t"][l], "nn", name=f"{tag}_dh2_gate")
    dh2 = _matmul(dup_v, wt["w_up_t"][l], "nn", name=f"{tag}_dh2_val", b_k0=D_FF, residual=dh2)
    dw_up = _matmul(dup_g, sv["h2"], "tn", name=f"{tag}_dw_up_gate", out_dtype=BF16, rows=(2 * D_FF, 0))
    dw_up = _matmul(dup_v, sv["h2"], "tn", name=f"{tag}_dw_up_val", out_dtype=BF16, rows=(2 * D_FF, D_FF), into=dw_up)
    tok = scatter("w_up_t", dw_up)
    dx1, dx1_b, dg = _rmsnorm_bwd(sv["x1"], small["g_ffn"][l][None] + tok, dh2, dx2, name=f"{tag}_norm_ffn")
    g["g_ffn"] = dg[0]
    dy = _matmul(dx1_b, wt["w_out"][l], "nt", name=f"{tag}_dy")
    tok = scatter("w_out", _matmul(sv["y"], dx1_b, "tn", name=f"{tag}_dw_out", out_dtype=BF16))
    dya, dyb, dyc, dg = _combine_bwd(dy, sv["ya"], sv["yb"], sv["yc"], small["g_out"][l][None] + tok,
                                     name=f"{tag}_combine")
    g["g_out"] = dg[0]
    dq, dk, dv = _attn_bwd(sv["qkv"], sv["t_tot"], dyc, name=f"{tag}_attn")
    dp_b, g["sgu_w"], db, dg, dbeta = _sgu_bwd(sv["p_ab"], dyb, small["sgu_ln_g"][l][None], small["sgu_ln_b"][l][None],
                                               small["sgu_w"][l], sv["bias"], name=f"{tag}_sgu")
    g["sgu_b"] = db[:, :N_SGU_HEADS].T
    g["sgu_ln_g"], g["sgu_ln_b"] = dg[0], dbeta[0]
    dp_a, g["conv_w"], dcb, dg, dbeta = _mixer_a_bwd(sv["p_ab"], dya, wt["conv_w"][l], small["conv_b"][l][None],
                                                     small["conv_ln_g"][l][None], small["conv_ln_b"][l][None],
                                                     name=f"{tag}_mixer_a")
    g["conv_b"], g["conv_ln_g"], g["conv_ln_b"] = dcb[0], dg[0], dbeta[0]
    dp = jnp.concatenate([dp_a, dp_b, dq, dk.astype(BF16), dv.astype(BF16)], axis=1)
    tok = scatter("w_in_t", _matmul(dp, sv["h"], "tn", name=f"{tag}_dw_in", out_dtype=BF16))
    dh = _matmul(dp, wt["w_in_t"][l], "nn", name=f"{tag}_dh")
    dx, dx_b, dg = _rmsnorm_bwd(sv["x"], small["g_mix"][l][None] + tok, dh, dx1, name=f"{tag}_norm_mix")
    g["g_mix"] = dg[0]
    return (dx, dx_b), g


_BIG = ("w_in_t", "w_out", "w_up_t", "w_down")


def kernel(x, g_mix, w_in, conv_w, conv_b, conv_ln_g, conv_ln_b, sgu_ln_g, sgu_ln_b, sgu_w, sgu_b, g_out, w_out, g_ffn, w_up, ffn_conv_w, ffn_conv_b, w_down, g_final, loss_target, m_g_mix, m_w_in, m_conv_w, m_conv_b, m_conv_ln_g, m_conv_ln_b, m_sgu_ln_g, m_sgu_ln_b, m_sgu_w, m_sgu_b, m_g_out, m_w_out, m_g_ffn, m_w_up, m_ffn_conv_w, m_ffn_conv_b, m_w_down, m_g_final, v_g_mix, v_w_in, v_conv_w, v_conv_b, v_conv_ln_g, v_conv_ln_b, v_sgu_ln_g, v_sgu_ln_b, v_sgu_w, v_sgu_b, v_g_out, v_w_out, v_g_ffn, v_w_up, v_ffn_conv_w, v_ffn_conv_b, v_w_down, v_g_final):
    given = dict(locals())
    n_layers = g_mix.shape[0]
    layers = range(n_layers)
    small_sharded = [n for n, _ in _SMALL_SHARDED]
    replicated = [n for n, _ in _REPLICATED]
    small = {n: given[n] for n in replicated}

    filters = _SplitExchange(_pack([given[n] for n in small_sharded]), kind="gather_blocks", name="gather_filters")
    wt = {n: [None] * n_layers for n in _BIG}
    wt["w_in_t"][0] = _SplitExchange(w_in[0].T.astype(BF16), kind="gather_rows", name="gather_w_in_t_l0")
    tok = _SplitExchange.start([filters, wt["w_in_t"][0]], name="gather_first_start")
    w_in, w_out, w_up, w_down, tok = lax.optimization_barrier((w_in, w_out, w_up, w_down, tok))
    shard = {"w_in_t": [w_in[l].T.astype(BF16) for l in layers], "w_out": [w_out[l].astype(BF16) for l in layers],
             "w_up_t": [w_up[l].T.astype(BF16) for l in layers], "w_down": [w_down[l].astype(BF16) for l in layers]}
    later = [(n, l) for l in layers for n in _BIG if (n, l) != ("w_in_t", 0)]
    for n, l in later:
        wt[n][l] = _SplitExchange(shard[n][l], kind="gather_rows", name=f"gather_{n}_l{l}")
    small["g_mix"] = g_mix + tok + _SplitExchange.start([wt[n][l] for n, l in later], name="gather_weights_start")
    gathered_filters = filters.finish(small["g_mix"])
    for n, blocks in zip(small_sharded, _unpack(gathered_filters, [s for _, s in _SMALL_SHARDED], lead=(N_DEV,))):
        wt[n] = _join_last(blocks)

    xs = x[0]
    saved = []
    for l in layers:
        xs, sv = _layer_fwd(l, xs, wt, small)
        saved.append(sv)
    loss_tile, dx, dx_b, dgf = _loss_head(xs, g_final[None], loss_target[0], name="loss_head")
    dres = (dx, dx_b)
    scattering = {n: [None] * n_layers for n in _BIG}
    layered = [n for n in replicated if n not in ("g_final", "sgu_w")]
    slabs, sgu_w_parts = [None] * n_layers, [None] * n_layers
    tok = 0.0
    for l in reversed(layers):
        dres, g = _layer_bwd(l, dres, saved[l], wt, small, scattering, tok)
        own = _pack([_split_last(g[n]) for n in small_sharded], lead=(N_DEV,))
        shared = _pack([g[n] for n in layered] + [dgf[0]])
        slab = jnp.concatenate([own, jnp.broadcast_to(shared[None], (N_DEV,) + shared.shape)], axis=1)
        slabs[l] = _SplitExchange(slab, kind="scatter_blocks", name=f"scatter_small_grads_l{l}")
        sgu_w_parts[l] = _SplitExchange(g["sgu_w"].reshape(-1, LANES).astype(BF16), kind="gather_blocks",
                                        name=f"gather_sgu_w_grads_l{l}")
        tok = _SplitExchange.start([slabs[l], sgu_w_parts[l]], name=f"small_grads_l{l}_start")
    n_own = own.shape[1]

    after_backward = jnp.full((8, LANES), tok)
    received = {n: [scattering[n][l].finish(after_backward) for l in layers] for n in _BIG}
    out = {}

    def update(n, parts, transposed=False):
        turn = (lambda a: jnp.swapaxes(a, 1, 2)) if transposed else (lambda a: a)
        results = _adamw(parts, turn(given[n]), turn(given["m_" + n]), turn(given["v_" + n]), name=f"adamw_{n}")
        for pre, res in zip(("grad_", "delta_", "new_m_", "new_v_"), results):
            out[pre + n] = turn(res)
        return results[0][0, :8, :LANES]

    update("w_out", received["w_out"])
    update("w_down", received["w_down"])
    update("w_in", received["w_in_t"], transposed=True)
    big_updated = update("w_up", received["w_up_t"], transposed=True)

    as_rows = lambda a: a.reshape(n_layers, -1, LANES)
    results = _adamw([ex.finish(big_updated) for ex in sgu_w_parts], as_rows(sgu_w), as_rows(m_sgu_w), as_rows(v_sgu_w),
                     name="adamw_sgu_w")
    for pre, res in zip(("grad_", "delta_", "new_m_", "new_v_"), results):
        out[pre + "sgu_w"] = res.reshape(sgu_w.shape)

    per_layer_g_final = {pre: jnp.broadcast_to(given[pre + "g_final"], (n_layers,) + g_final.shape) for pre in ("", "m_", "v_")}
    stacks = [jnp.concatenate([_pack([given[pre + n] for n in small_sharded], lead=(n_layers,)),
                               _pack([given[pre + n] for n in layered] + [per_layer_g_final[pre]], lead=(n_layers,))], axis=1)
              for pre in ("", "m_", "v_")]
    results = _adamw([slabs[l].finish(big_updated) for l in layers], *stacks, name="adamw_small")
    for pre, res in zip(("grad_", "delta_", "new_m_", "new_v_"), results):
        unpacked = (_unpack(res[:, :n_own], [s[1:] for _, s in _SMALL_SHARDED], lead=(n_layers,))
                    + _unpack(res[:, n_own:], [s[1:] for n, s in _REPLICATED if n in layered] + [g_final.shape], lead=(n_layers,)))
        for n, a in zip(small_sharded + layered + ["g_final"], unpacked):
            out[pre + n] = a[0] if n == "g_final" else a

    loss = lax.psum(loss_tile[0, 0], ("x", "y", "c"))
    order = list(_WEIGHT_ORDER)
    return (loss, dres[0][None], *[out["grad_" + n] for n in order], *[out["delta_" + n] for n in order],
            *[out["new_m_" + n] for n in order], *[out["new_v_" + n] for n in order])


_WEIGHT_ORDER = ("g_mix", "w_in", "conv_w", "conv_b", "conv_ln_g", "conv_ln_b", "sgu_ln_g", "sgu_ln_b", "sgu_w", "sgu_b",
                 "g_out", "w_out", "g_ffn", "w_up", "ffn_conv_w", "ffn_conv_b", "w_down", "g_final")
```

```python
import math

import jax
import jax.numpy as jnp
from jax import lax
from jax.experimental import pallas as pl
from jax.experimental.pallas import tpu as pltpu

F32 = jnp.float32
BF16 = jnp.bfloat16

N_DEV = 8
D_MODEL = 1024
HEAD_DIM = 64
D_CONV = 256
D_SGU = 256
D_SB = 512
D_AB = 2 * D_CONV + 2 * D_SGU
CONV_K = 31
CONV_HALO = 32
FFN_K = 3
FFN_HALO = 8
D_FF = 2816
CHUNK = 128
EPS = 1e-6
LANES = 128

ADAM_LR = 0.001
ADAM_B1 = 0.9
ADAM_B2 = 0.999
ADAM_EPS = 1e-08
ADAM_WD = 0.01
ADAM_STEP = 10

VMEM_LIMIT = 56 * 1024 * 1024


def _cparams(sem=None):
    return pltpu.CompilerParams(dimension_semantics=sem, vmem_limit_bytes=VMEM_LIMIT)


def _sigmoid(x):
    return 1.0 / (1.0 + jnp.exp(-x))


_INV_SQRT2 = 1.0 / math.sqrt(2.0)
_INV_SQRT2PI = 1.0 / math.sqrt(2.0 * math.pi)


def _gelu(x):
    return 0.5 * x * (1.0 + lax.erf(x * _INV_SQRT2))


def _gelu_grad(x):
    return 0.5 * (1.0 + lax.erf(x * _INV_SQRT2)) + x * jnp.exp(-0.5 * x * x) * _INV_SQRT2PI


def _dot(a, b, dims):
    return lax.dot_general(a, b, (dims, ((), ())), preferred_element_type=F32)


_NN = ((1,), (0,))
_NT = ((1,), (1,))
_TN = ((0,), (0,))


def _split_bf16(x):
    hi = x.astype(BF16)
    lo = (x - hi.astype(F32)).astype(BF16)
    return jnp.concatenate([hi, lo], axis=1)


def _matmul(a, b, mode, *, name, out_dtype=F32, residual=None, n=None, b_n0=0, b_k0=0, rows=None, into=None):
    if mode == "nn":
        (m, k), n = a.shape, (n or b.shape[1])
    elif mode == "nt":
        (m, k), n = a.shape, (n or b.shape[0])
    else:
        (k, m), n = a.shape, b.shape[1]
    has_res = residual is not None
    tm, tn = _matmul_tiles(m, n, k, a.dtype.itemsize, b.dtype.itemsize, jnp.dtype(out_dtype).itemsize, has_res, b_n0)
    j0 = b_n0 // tn
    total_rows, first_row = rows or (m, 0)
    assert b_k0 % k == 0 and first_row % tm == 0
    kb, i0 = b_k0 // k, first_row // tm

    if mode == "nn":
        a_spec = pl.BlockSpec((tm, k), lambda i, j: (i, 0))
        b_spec = pl.BlockSpec((k, tn), lambda i, j: (kb, j + j0))
        dims = _NN
    elif mode == "nt":
        a_spec = pl.BlockSpec((tm, k), lambda i, j: (i, 0))
        b_spec = pl.BlockSpec((tn, k), lambda i, j: (j + j0, 0))
        dims = _NT
    else:
        a_spec = pl.BlockSpec((k, tm), lambda i, j: (0, i))
        b_spec = pl.BlockSpec((k, tn), lambda i, j: (0, j))
        dims = _TN
    o_spec = pl.BlockSpec((tm, tn), lambda i, j: (i + i0, j))
    r_spec = pl.BlockSpec((tm, tn), lambda i, j: (i, j))

    def body(*refs):
        a_ref, b_ref = refs[:2]
        acc = _dot(a_ref[...].astype(BF16), b_ref[...].astype(BF16), dims)
        if has_res:
            acc = acc + refs[2][...]
        refs[-1][...] = acc.astype(out_dtype)

    in_specs = [a_spec, b_spec] + ([r_spec] if has_res else [])
    args = (a, b) + ((residual,) if has_res else ())
    aliases = {}
    if into is not None:
        aliases = {len(args): 0}
        in_specs.append(pl.BlockSpec(memory_space=pl.ANY))
        args += (into,)

        def body(*refs, inner=body):
            inner(*refs[:len(args) - 1], refs[-1])

    return pl.pallas_call(
        body,
        name=name,
        grid=(m // tm, n // tn),
        in_specs=in_specs,
        out_specs=o_spec,
        out_shape=jax.ShapeDtypeStruct((total_rows, n), out_dtype),
        input_output_aliases=aliases,
        compiler_params=_cparams(("parallel", "parallel")),
    )(*args)


MATMUL_VMEM_BUDGET = 40 * 1024 * 1024


def _matmul_tiles(m, n, k, a_bytes, b_bytes, out_bytes, has_res, n_offset):
    def divisors(size, cap, also=0):
        return [t for t in range(cap, 0, -LANES) if size % t == 0 and also % t == 0] or [size]

    for tm in divisors(m, 256 if (has_res and k > 2048) else 1024):
        for tn in divisors(n, 1408, n_offset):
            blocks = tm * k * a_bytes + k * tn * b_bytes + tm * tn * (out_bytes + (4 if has_res else 0))
            if 2 * blocks <= MATMUL_VMEM_BUDGET:
                return tm, tn
    raise ValueError(f"no matmul tiling for {m} x {n} x {k}")


ROW_TILE = 512


def _rmsnorm_fwd(x, g, *, name):
    s, d = x.shape

    def body(x_ref, g_ref, h_ref):
        xv = x_ref[...]
        r = lax.rsqrt(jnp.mean(xv * xv, axis=-1, keepdims=True) + EPS)
        h_ref[...] = (xv * r * g_ref[...]).astype(BF16)

    return pl.pallas_call(
        body,
        name=name,
        grid=(s // ROW_TILE,),
        in_specs=[pl.BlockSpec((ROW_TILE, d), lambda i: (i, 0)), pl.BlockSpec((1, d), lambda i: (0, 0))],
        out_specs=pl.BlockSpec((ROW_TILE, d), lambda i: (i, 0)),
        out_shape=jax.ShapeDtypeStruct((s, d), BF16),
        compiler_params=_cparams(("parallel",)),
    )(x, g)


def _rmsnorm_bwd(x, g, dh, dres, *, name):
    s, d = x.shape

    def body(x_ref, g_ref, dh_ref, dres_ref, dx_ref, dxb_ref, dg_ref):
        xv = x_ref[...]
        r = lax.rsqrt(jnp.mean(xv * xv, axis=-1, keepdims=True) + EPS)
        xhat = xv * r
        dhv = dh_ref[...]
        dxhat = dhv * g_ref[...]
        dx = dres_ref[...] + r * (dxhat - xhat * jnp.mean(dxhat * xhat, axis=-1, keepdims=True))
        dx_ref[...] = dx
        dxb_ref[...] = dx.astype(BF16)
        part = jnp.sum(dhv * xhat, axis=0, keepdims=True)

        @pl.when(pl.program_id(0) == 0)
        def _():
            dg_ref[...] = part

        @pl.when(pl.program_id(0) > 0)
        def _():
            dg_ref[...] += part

    row = pl.BlockSpec((ROW_TILE, d), lambda i: (i, 0))
    vec = pl.BlockSpec((1, d), lambda i: (0, 0))
    return pl.pallas_call(
        body,
        name=name,
        grid=(s // ROW_TILE,),
        in_specs=[row, vec, row, row],
        out_specs=[row, row, vec],
        out_shape=[jax.ShapeDtypeStruct((s, d), F32), jax.ShapeDtypeStruct((s, d), BF16),
                   jax.ShapeDtypeStruct((1, d), F32)],
        compiler_params=_cparams(("arbitrary",)),
    )(x, g, dh, dres)


def _loss_head(x, g, target, *, name):
    s, d = x.shape

    def body(x_ref, g_ref, t_ref, loss_ref, dx_ref, dxb_ref, dg_ref):
        xv = x_ref[...]
        gv = g_ref[...]
        r = lax.rsqrt(jnp.mean(xv * xv, axis=-1, keepdims=True) + EPS)
        xhat = xv * r
        diff = xhat * gv - t_ref[...]
        dy = diff * (1.0 / d)
        dxhat = dy * gv
        dx = r * (dxhat - xhat * jnp.mean(dxhat * xhat, axis=-1, keepdims=True))
        dx_ref[...] = dx
        dxb_ref[...] = dx.astype(BF16)
        dg_part = jnp.sum(dy * xhat, axis=0, keepdims=True)
        row_loss = jnp.sum(diff * diff, axis=-1, keepdims=True)
        loss_part = jnp.sum(row_loss, axis=0, keepdims=True) * (0.5 / d)

        @pl.when(pl.program_id(0) == 0)
        def _():
            dg_ref[...] = dg_part
            loss_ref[...] = jnp.broadcast_to(loss_part, loss_ref.shape)

        @pl.when(pl.program_id(0) > 0)
        def _():
            dg_ref[...] += dg_part
            loss_ref[...] += jnp.broadcast_to(loss_part, loss_ref.shape)

    row = pl.BlockSpec((ROW_TILE, d), lambda i: (i, 0))
    vec = pl.BlockSpec((1, d), lambda i: (0, 0))
    tile = pl.BlockSpec((8, LANES), lambda i: (0, 0))
    return pl.pallas_call(
        body,
        name=name,
        grid=(s // ROW_TILE,),
        in_specs=[row, vec, row],
        out_specs=[tile, row, row, vec],
        out_shape=[jax.ShapeDtypeStruct((8, LANES), F32), jax.ShapeDtypeStruct((s, d), F32),
                   jax.ShapeDtypeStruct((s, d), BF16), jax.ShapeDtypeStruct((1, d), F32)],
        compiler_params=_cparams(("arbitrary",)),
    )(x, g, target)


_BRANCHES = ((0, D_CONV), (D_CONV, D_SGU), (D_CONV + D_SGU, D_SB))


def _combine_fwd(ya, yb, yc, g, *, name):
    s = ya.shape[0]

    def body(ya_ref, yb_ref, yc_ref, g_ref, y_ref):
        for ref, (off, w) in zip((ya_ref, yb_ref, yc_ref), _BRANCHES):
            v = ref[...]
            r = lax.rsqrt(jnp.mean(v * v, axis=-1, keepdims=True) + EPS)
            y_ref[:, off:off + w] = (v * r * g_ref[:, off:off + w]).astype(BF16)

    def row(w):
        return pl.BlockSpec((ROW_TILE, w), lambda i: (i, 0))

    return pl.pallas_call(
        body,
        name=name,
        grid=(s // ROW_TILE,),
        in_specs=[row(D_CONV), row(D_SGU), row(D_SB), pl.BlockSpec((1, D_MODEL), lambda i: (0, 0))],
        out_specs=row(D_MODEL),
        out_shape=jax.ShapeDtypeStruct((s, D_MODEL), BF16),
        compiler_params=_cparams(("parallel",)),
    )(ya, yb, yc, g)


def _combine_bwd(dy, ya, yb, yc, g, *, name):
    s = ya.shape[0]

    def body(dy_ref, ya_ref, yb_ref, yc_ref, g_ref, dya_ref, dyb_ref, dyc_ref, dg_ref):
        first = pl.program_id(0) == 0
        for ref, dref, (off, w) in zip((ya_ref, yb_ref, yc_ref), (dya_ref, dyb_ref, dyc_ref), _BRANCHES):
            v = ref[...]
            r = lax.rsqrt(jnp.mean(v * v, axis=-1, keepdims=True) + EPS)
            n = v * r
            dout = dy_ref[:, off:off + w]
            dn = dout * g_ref[:, off:off + w]
            dref[...] = r * (dn - n * jnp.mean(dn * n, axis=-1, keepdims=True))
            part = jnp.sum(dout * n, axis=0, keepdims=True)

            @pl.when(first)
            def _():
                dg_ref[:, off:off + w] = part

            @pl.when(jnp.logical_not(first))
            def _():
                dg_ref[:, off:off + w] += part

    def row(w):
        return pl.BlockSpec((ROW_TILE, w), lambda i: (i, 0))

    vec = pl.BlockSpec((1, D_MODEL), lambda i: (0, 0))
    return pl.pallas_call(
        body,
        name=name,
        grid=(s // ROW_TILE,),
        in_specs=[row(D_MODEL), row(D_CONV), row(D_SGU), row(D_SB), vec],
        out_specs=[row(D_CONV), row(D_SGU), row(D_SB), vec],
        out_shape=[jax.ShapeDtypeStruct((s, D_CONV), F32), jax.ShapeDtypeStruct((s, D_SGU), F32),
                   jax.ShapeDtypeStruct((s, D_SB), F32), jax.ShapeDtypeStruct((1, D_MODEL), F32)],
        compiler_params=_cparams(("arbitrary",)),
    )(dy, ya, yb, yc, g)


CONV_TILE = 128


def _shift_down(window, j, halo):
    return pltpu.roll(window, j, 0)[halo:, :] if j else window[halo:, :]


def _shift_up(window, j, n_out):
    n = window.shape[0]
    return pltpu.roll(window, n - j, 0)[:n_out, :] if j else window[:n_out, :]


def _mixer_a_fwd(p_ab, conv_w, conv_b, ln_g, ln_b, *, name):
    s = p_ab.shape[0]
    nt = s // CONV_TILE

    def body(p_ref, w_ref, b_ref, g_ref, beta_ref, y_ref, h_ref):
        h_ref[0:CONV_HALO, :] = jnp.zeros((CONV_HALO, D_CONV), F32)

        def glu(i, c):
            t0 = pl.multiple_of(i * CONV_TILE, CONV_TILE)
            a = p_ref[pl.ds(t0, CONV_TILE), 0:D_CONV].astype(F32)
            gate = p_ref[pl.ds(t0, CONV_TILE), D_CONV:2 * D_CONV].astype(F32)
            h_ref[pl.ds(t0 + CONV_HALO, CONV_TILE), :] = a * _sigmoid(gate)
            return c

        lax.fori_loop(0, nt, glu, 0)

        def conv(i, c):
            t0 = pl.multiple_of(i * CONV_TILE, CONV_TILE)
            window = h_ref[pl.ds(t0, CONV_TILE + CONV_HALO), :]
            acc = jnp.zeros((CONV_TILE, D_CONV), F32) + b_ref[...]
            for k in range(CONV_K):
                acc = acc + w_ref[k:k + 1, :] * _shift_down(window, CONV_K - 1 - k, CONV_HALO)
            mu = jnp.mean(acc, axis=-1, keepdims=True)
            xc = acc - mu
            rstd = lax.rsqrt(jnp.mean(xc * xc, axis=-1, keepdims=True) + EPS)
            z = xc * rstd * g_ref[...] + beta_ref[...]
            y_ref[pl.ds(t0, CONV_TILE), :] = z * _sigmoid(z)
            return c

        lax.fori_loop(0, nt, conv, 0)

    full = lambda shape: pl.BlockSpec(shape, lambda i: (0, 0))
    return pl.pallas_call(
        body,
        name=name,
        grid=(1,),
        in_specs=[full((s, 2 * D_CONV)), full((CONV_K, D_CONV)), full((1, D_CONV)), full((1, D_CONV)),
                  full((1, D_CONV))],
        out_specs=full((s, D_CONV)),
        out_shape=jax.ShapeDtypeStruct((s, D_CONV), F32),
        scratch_shapes=[pltpu.VMEM((s + CONV_HALO, D_CONV), F32)],
        compiler_params=_cparams(("arbitrary",)),
    )(p_ab, conv_w, conv_b, ln_g, ln_b)


def _mixer_a_bwd(p_ab, dya, conv_w, conv_b, ln_g, ln_b, *, name):
    s = p_ab.shape[0]
    nt = s // CONV_TILE

    def body(p_ref, dy_ref, w_ref, b_ref, g_ref, beta_ref, dp_ref, dw_ref, db_ref, dg_ref, dbeta_ref, h_ref, dc_ref):
        h_ref[0:CONV_HALO, :] = jnp.zeros((CONV_HALO, D_CONV), F32)
        dc_ref[s:s + CONV_HALO, :] = jnp.zeros((CONV_HALO, D_CONV), F32)
        dw_ref[...] = jnp.zeros_like(dw_ref)
        db_ref[...] = jnp.zeros_like(db_ref)
        dg_ref[...] = jnp.zeros_like(dg_ref)
        dbeta_ref[...] = jnp.zeros_like(dbeta_ref)

        def glu(i, c):
            t0 = pl.multiple_of(i * CONV_TILE, CONV_TILE)
            a = p_ref[pl.ds(t0, CONV_TILE), 0:D_CONV].astype(F32)
            gate = p_ref[pl.ds(t0, CONV_TILE), D_CONV:2 * D_CONV].astype(F32)
            h_ref[pl.ds(t0 + CONV_HALO, CONV_TILE), :] = a * _sigmoid(gate)
            return c

        lax.fori_loop(0, nt, glu, 0)

        def conv_bwd(i, c):
            t0 = pl.multiple_of(i * CONV_TILE, CONV_TILE)
            window = h_ref[pl.ds(t0, CONV_TILE + CONV_HALO), :]
            taps = [_shift_down(window, CONV_K - 1 - k, CONV_HALO) for k in range(CONV_K)]
            acc = jnp.zeros((CONV_TILE, D_CONV), F32) + b_ref[...]
            for k in range(CONV_K):
                acc = acc + w_ref[k:k + 1, :] * taps[k]
            mu = jnp.mean(acc, axis=-1, keepdims=True)
            xc = acc - mu
            rstd = lax.rsqrt(jnp.mean(xc * xc, axis=-1, keepdims=True) + EPS)
            xhat = xc * rstd
            z = xhat * g_ref[...] + beta_ref[...]
            sg = _sigmoid(z)
            dz = dy_ref[pl.ds(t0, CONV_TILE), :] * (sg * (1.0 + z * (1.0 - sg)))
            dg_ref[...] += jnp.sum(dz * xhat, axis=0, keepdims=True)
            dbeta_ref[...] += jnp.sum(dz, axis=0, keepdims=True)
            dxhat = dz * g_ref[...]
            dc = rstd * (dxhat - jnp.mean(dxhat, axis=-1, keepdims=True)
                         - xhat * jnp.mean(dxhat * xhat, axis=-1, keepdims=True))
            dc_ref[pl.ds(t0, CONV_TILE), :] = dc
            db_ref[...] += jnp.sum(dc, axis=0, keepdims=True)
            for k in range(CONV_K):
                dw_ref[k:k + 1, :] += jnp.sum(dc * taps[k], axis=0, keepdims=True)
            return c

        lax.fori_loop(0, nt, conv_bwd, 0)

        def glu_bwd(i, c):
            t0 = pl.multiple_of(i * CONV_TILE, CONV_TILE)
            window = dc_ref[pl.ds(t0, CONV_TILE + CONV_HALO), :]
            dh = jnp.zeros((CONV_TILE, D_CONV), F32)
            for j in range(CONV_K):
                dh = dh + w_ref[CONV_K - 1 - j:CONV_K - j, :] * _shift_up(window, j, CONV_TILE)
            a = p_ref[pl.ds(t0, CONV_TILE), 0:D_CONV].astype(F32)
            sg = _sigmoid(p_ref[pl.ds(t0, CONV_TILE), D_CONV:2 * D_CONV].astype(F32))
            dp_ref[pl.ds(t0, CONV_TILE), 0:D_CONV] = (dh * sg).astype(BF16)
            dp_ref[pl.ds(t0, CONV_TILE), D_CONV:2 * D_CONV] = (dh * a * sg * (1.0 - sg)).astype(BF16)
            return c

        lax.fori_loop(0, nt, glu_bwd, 0)

    full = lambda shape: pl.BlockSpec(shape, lambda i: (0, 0))
    vec = jax.ShapeDtypeStruct((1, D_CONV), F32)
    return pl.pallas_call(
        body,
        name=name,
        grid=(1,),
        in_specs=[full((s, 2 * D_CONV)), full((s, D_CONV)), full((CONV_K, D_CONV)), full((1, D_CONV)),
                  full((1, D_CONV)), full((1, D_CONV))],
        out_specs=[full((s, 2 * D_CONV)), full((CONV_K, D_CONV)), full((1, D_CONV)), full((1, D_CONV)),
                   full((1, D_CONV))],
        out_shape=[jax.ShapeDtypeStruct((s, 2 * D_CONV), BF16), jax.ShapeDtypeStruct((CONV_K, D_CONV), F32),
                   vec, vec, vec],
        scratch_shapes=[pltpu.VMEM((s + CONV_HALO, D_CONV), F32), pltpu.VMEM((s + CONV_HALO, D_CONV), F32)],
        compiler_params=_cparams(("arbitrary",)),
    )(p_ab, dya, conv_w, conv_b, ln_g, ln_b)


N_SGU_HEADS = D_SGU // HEAD_DIM


def _head_masks(width):
    lane = lax.broadcasted_iota(jnp.int32, (1, width), 1)
    return [(lane >= h * HEAD_DIM) & (lane < (h + 1) * HEAD_DIM) for h in range(width // HEAD_DIM)]


def _tril_mask():
    r = lax.broadcasted_iota(jnp.int32, (CHUNK, CHUNK), 0)
    c = lax.broadcasted_iota(jnp.int32, (CHUNK, CHUNK), 1)
    return c <= r


def _sgu_norm(bv, g, beta):
    vg = _gelu(bv)
    mu = jnp.mean(vg, axis=-1, keepdims=True)
    xc = vg - mu
    rstd = lax.rsqrt(jnp.mean(xc * xc, axis=-1, keepdims=True) + EPS)
    xhat = xc * rstd
    return xhat, rstd, xhat * g + beta


def _sgu_fwd(p_ab, ln_g, ln_b, w_s, bias, *, name):
    s = p_ab.shape[0]

    def body(p_ref, g_ref, beta_ref, w_ref, bias_ref, y_ref):
        u = _gelu(p_ref[:, 0:D_SGU].astype(F32))
        _, _, vn = _sgu_norm(p_ref[:, D_SGU:2 * D_SGU].astype(F32), g_ref[...], beta_ref[...])
        vb = vn.astype(BF16)
        tril = _tril_mask()
        mixed = bias_ref[...]
        for h, m in enumerate(_head_masks(D_SGU)):
            wh = jnp.where(tril, w_ref[h], 0.0).astype(BF16)
            mixed = mixed + _dot(wh, jnp.where(m, vb, jnp.zeros_like(vb)), _NN)
        y_ref[...] = u * mixed

    return pl.pallas_call(
        body,
        name=name,
        grid=(s // CHUNK,),
        in_specs=[pl.BlockSpec((CHUNK, 2 * D_SGU), lambda i: (i, 1)),
                  pl.BlockSpec((1, D_SGU), lambda i: (0, 0)), pl.BlockSpec((1, D_SGU), lambda i: (0, 0)),
                  pl.BlockSpec((N_SGU_HEADS, CHUNK, CHUNK), lambda i: (0, 0, 0)),
                  pl.BlockSpec((CHUNK, D_SGU), lambda i: (0, 0))],
        out_specs=pl.BlockSpec((CHUNK, D_SGU), lambda i: (i, 0)),
        out_shape=jax.ShapeDtypeStruct((s, D_SGU), F32),
        compiler_params=_cparams(("parallel",)),
    )(p_ab, ln_g, ln_b, w_s, bias)


def _sgu_bwd(p_ab, dyb, ln_g, ln_b, w_s, bias, *, name):
    s = p_ab.shape[0]
    n_chunks = s // CHUNK

    def body(p_ref, dy_ref, g_ref, beta_ref, w_ref, bias_ref, dp_ref, dw_ref, db_ref, dg_ref, dbeta_ref, dbias_ref):
        @pl.when(pl.program_id(0) == 0)
        def _():
            dw_ref[...] = jnp.zeros_like(dw_ref)
            dbias_ref[...] = jnp.zeros_like(dbias_ref)
            dg_ref[...] = jnp.zeros_like(dg_ref)
            dbeta_ref[...] = jnp.zeros_like(dbeta_ref)

        bu = p_ref[:, 0:D_SGU].astype(F32)
        bv = p_ref[:, D_SGU:2 * D_SGU].astype(F32)
        u = _gelu(bu)
        gv = g_ref[...]
        xhat, rstd, vn = _sgu_norm(bv, gv, beta_ref[...])
        vb = vn.astype(BF16)
        tril = _tril_mask()
        masks = _head_masks(D_SGU)
        whs = [jnp.where(tril, w_ref[h], 0.0).astype(BF16) for h in range(N_SGU_HEADS)]
        mixed = bias_ref[...]
        for h, m in enumerate(masks):
            mixed = mixed + _dot(whs[h], jnp.where(m, vb, jnp.zeros_like(vb)), _NN)
        dy = dy_ref[...]
        dp_ref[:, 0:D_SGU] = (dy * mixed * _gelu_grad(bu)).astype(BF16)
        dmixed = dy * u
        dbias_ref[...] += dmixed
        dmb = dmixed.astype(BF16)
        dvn = jnp.zeros((CHUNK, D_SGU), F32)
        for h, m in enumerate(masks):
            dmh = jnp.where(m, dmb, jnp.zeros_like(dmb))
            dvn = dvn + _dot(whs[h], dmh, _TN)
            dw_ref[h] += jnp.where(tril, _dot(dmh, vb, _NT), 0.0)
        dg_ref[...] += jnp.sum(dvn * xhat, axis=0, keepdims=True)
        dbeta_ref[...] += jnp.sum(dvn, axis=0, keepdims=True)
        dxhat = dvn * gv
        dvg = rstd * (dxhat - jnp.mean(dxhat, axis=-1, keepdims=True)
                      - xhat * jnp.mean(dxhat * xhat, axis=-1, keepdims=True))
        dp_ref[:, D_SGU:2 * D_SGU] = (dvg * _gelu_grad(bv)).astype(BF16)

        @pl.when(pl.program_id(0) == n_chunks - 1)
        def _():
            chan = lax.broadcasted_iota(jnp.int32, (D_SGU, LANES), 0)
            head = lax.broadcasted_iota(jnp.int32, (D_SGU, LANES), 1)
            to_head = jnp.where(chan // HEAD_DIM == head, 1.0, 0.0).astype(BF16)
            db_ref[...] = _dot(_split_bf16(dbias_ref[...]), jnp.concatenate([to_head, to_head], axis=0), _NN)

    vec = pl.BlockSpec((1, D_SGU), lambda i: (0, 0))
    wspec = pl.BlockSpec((N_SGU_HEADS, CHUNK, CHUNK), lambda i: (0, 0, 0))
    bspec = pl.BlockSpec((CHUNK, D_SGU), lambda i: (0, 0))
    return pl.pallas_call(
        body,
        name=name,
        grid=(n_chunks,),
        in_specs=[pl.BlockSpec((CHUNK, 2 * D_SGU), lambda i: (i, 1)), pl.BlockSpec((CHUNK, D_SGU), lambda i: (i, 0)),
                  vec, vec, wspec, bspec],
        out_specs=[pl.BlockSpec((CHUNK, 2 * D_SGU), lambda i: (i, 0)), wspec,
                   pl.BlockSpec((CHUNK, LANES), lambda i: (0, 0)), vec, vec],
        out_shape=[jax.ShapeDtypeStruct((s, 2 * D_SGU), BF16),
                   jax.ShapeDtypeStruct((N_SGU_HEADS, CHUNK, CHUNK), F32),
                   jax.ShapeDtypeStruct((CHUNK, LANES), F32),
                   jax.ShapeDtypeStruct((1, D_SGU), F32), jax.ShapeDtypeStruct((1, D_SGU), F32)],
        scratch_shapes=[pltpu.VMEM((CHUNK, D_SGU), F32)],
        compiler_params=_cparams(("arbitrary",)),
    )(p_ab, dyb, ln_g, ln_b, w_s, bias)


N_PAIRS = D_SB // LANES
QKV_BLOCK0 = D_AB // LANES
SB_SCALE = HEAD_DIM ** -0.5


def _sb_logits(z, valid):
    nz = -z
    t = jnp.log(1.0 + jnp.exp(jnp.minimum(z, nz)))
    l1 = jnp.minimum(nz, 0.0) - t
    if valid is not None:
        l1 = jnp.where(valid, l1, 0.0)
    return l1, jnp.minimum(z, 0.0) - t


def _split_hi_lo(x):
    hi = lax.bitcast_convert_type(lax.bitcast_convert_type(x, jnp.uint32) & jnp.uint32(0xFFFF0000), F32)
    return jnp.concatenate([hi, x - hi], axis=1)


def _cumsum_operand(keep):
    half = jnp.concatenate([keep.astype(F32), jnp.ones((CHUNK, CHUNK), F32)], axis=1)
    return jnp.concatenate([half, half], axis=0)


Q_BLOCKS_PER_STEP = 4


def _q_blocks_per_step(nq):
    return next(n for n in (Q_BLOCKS_PER_STEP, 2, 1) if nq % n == 0)


def _attn_fwd(qkv, *, name):
    s = qkv.shape[0]
    nq = s // CHUNK
    per_step = _q_blocks_per_step(nq)

    def body(q_ref, k_ref, v_ref, o_ref, t_ref):
        masks = _head_masks(LANES)
        row = lax.broadcasted_iota(jnp.int32, (CHUNK, CHUNK), 0)
        col = lax.broadcasted_iota(jnp.int32, (CHUNK, CHUNK), 1)
        after_op = _cumsum_operand(row > col)
        cmr = col - row
        zc = jnp.zeros((CHUNK, LANES), F32)

        def q_block(sub, _):
            qi = pl.program_id(1) * per_step + sub
            q_rows = pl.ds(pl.multiple_of(sub * CHUNK, CHUNK), CHUNK)
            q = q_ref[q_rows, :] * SB_SCALE
            zero = jnp.zeros_like(q)
            qs = [jnp.where(m, q, zero) for m in masks]

            def blocks(js, carry):
                o, c0, c1 = carry
                starts = [pl.multiple_of(jnp.maximum(j, 0) * CHUNK, CHUNK) for j in js]
                valids = [cmr < jnp.where(j >= 0, (qi - j) * CHUNK, -CHUNK) for j in js]
                units = [(h, b) for b in range(len(js)) for h in range(2)]
                zs = [_dot(qs[h], k_ref[pl.ds(starts[b], CHUNK), :], _NT) for h, b in units]
                logits = [_sb_logits(z, valids[b]) for z, (h, b) in zip(zs, units)]
                sums = [_dot(_split_hi_lo(l1), after_op, _NN) for l1, _ in logits]
                cs = [c0, c1]
                probs = []
                for (h, b), (_, lb), sm in zip(units, logits, sums):
                    probs.append(jnp.where(valids[b], jnp.exp(lb + sm[:, :CHUNK] + cs[h]), 0.0))
                    cs[h] = cs[h] + sm[:, CHUNK:]
                for (h, b), a in zip(units, probs):
                    o = o + _dot(a.astype(BF16), jnp.where(masks[h], v_ref[pl.ds(starts[b], CHUNK), :], zero), _NN)
                return o, cs[0], cs[1]

            n_four = (qi + 1) // 4
            carry = lax.fori_loop(0, n_four, lambda jj, c: blocks([qi - 4 * jj - i for i in range(4)], c), (zc,) * 3)
            top = qi - 4 * n_four
            o, c0, c1 = lax.fori_loop(0, (top + 2) // 2, lambda jj, c: blocks([top - 2 * jj, top - 2 * jj - 1], c), carry)
            o_ref[q_rows, :] = o
            t_ref[q_rows, 0:LANES] = c0
            t_ref[q_rows, LANES:2 * LANES] = c1
            return 0

        lax.fori_loop(0, per_step, q_block, 0)

    rows = per_step * CHUNK
    return pl.pallas_call(
        body,
        name=name,
        grid=(N_PAIRS, nq // per_step),
        in_specs=[pl.BlockSpec((rows, LANES), lambda p, i: (i, QKV_BLOCK0 + p)),
                  pl.BlockSpec((s, LANES), lambda p, i: (0, QKV_BLOCK0 + N_PAIRS + p)),
                  pl.BlockSpec((s, LANES), lambda p, i: (0, QKV_BLOCK0 + 2 * N_PAIRS + p))],
        out_specs=[pl.BlockSpec((rows, LANES), lambda p, i: (i, p)),
                   pl.BlockSpec((rows, 2 * LANES), lambda p, i: (i, p))],
        out_shape=[jax.ShapeDtypeStruct((s, D_SB), F32), jax.ShapeDtypeStruct((s, 2 * D_SB), F32)],
        compiler_params=_cparams(("parallel", "parallel")),
    )(qkv, qkv, qkv)


def _attn_bwd(qkv, t_tot, do, *, name):
    s = qkv.shape[0]
    nq = s // CHUNK
    per_step = _q_blocks_per_step(nq)

    def body(q_ref, k_ref, v_ref, t_ref, do_ref, dq_ref, dk_ref, dv_ref):
        @pl.when(pl.program_id(1) == 0)
        def _():
            dk_ref[...] = jnp.zeros_like(dk_ref)
            dv_ref[...] = jnp.zeros_like(dv_ref)

        masks = _head_masks(LANES)
        row = lax.broadcasted_iota(jnp.int32, (CHUNK, CHUNK), 0)
        col = lax.broadcasted_iota(jnp.int32, (CHUNK, CHUNK), 1)
        upto_op = _cumsum_operand(row <= col)
        before_op = _cumsum_operand(row < col)
        cmr = col - row
        zc = jnp.zeros((CHUNK, LANES), F32)

        def q_block(sub, _):
            qi = pl.program_id(1) * per_step + sub
            q_rows = pl.ds(pl.multiple_of(sub * CHUNK, CHUNK), CHUNK)
            q = q_ref[q_rows, :] * SB_SCALE
            dob = do_ref[q_rows, :].astype(BF16)
            zero = jnp.zeros_like(q)
            qs = [jnp.where(m, q, zero) for m in masks]
            dos = [jnp.where(m, dob, zero) for m in masks]
            tots = [t_ref[q_rows, 0:LANES], t_ref[q_rows, LANES:2 * LANES]]

            def blocks(js, carry):
                dq, cl0, cl1, cp0, cp1 = carry
                starts = [pl.multiple_of(jnp.minimum(j, nq - 1) * CHUNK, CHUNK) for j in js]
                valids = [cmr < (qi - j) * CHUNK for j in js]
                units = [(h, b) for b in range(len(js)) for h in range(2)]
                zs = [_dot(qs[h], k_ref[pl.ds(starts[b], CHUNK), :], _NT) for h, b in units]
                da_dots = lambda: [_dot(dos[h], v_ref[pl.ds(starts[b], CHUNK), :], _NT) for h, b in units]
                das = da_dots() if len(js) < 4 else None
                logits = [_sb_logits(z, valids[b]) for z, (h, b) in zip(zs, units)]
                sums = [_dot(_split_hi_lo(l1), upto_op, _NN) for l1, _ in logits]
                das = das or da_dots()
                cls, cps = [cl0, cl1], [cp0, cp1]
                probs, gs = [], []
                for (h, b), (_, lb), sm, da in zip(units, logits, sums, das):
                    a = jnp.where(valids[b], jnp.exp(lb + (tots[h] - cls[h] - sm[:, :CHUNK])), 0.0)
                    probs.append(a)
                    gs.append(a * da)
                    cls[h] = cls[h] + sm[:, CHUNK:]
                sums_g = [_dot(_split_hi_lo(g), before_op, _NN) for g in gs]
                dzs = []
                for (h, b), (_, lb), g, sg in zip(units, logits, gs, sums_g):
                    dz = g - (g + sg[:, :CHUNK] + cps[h]) * jnp.exp(lb)
                    dzs.append(jnp.where(valids[b], dz, 0.0).astype(BF16))
                    cps[h] = cps[h] + sg[:, CHUNK:]
                for (h, b), dzb in zip(units, dzs):
                    dq = dq + _dot(dzb, jnp.where(masks[h], k_ref[pl.ds(starts[b], CHUNK), :], zero), _NN)
                for b, k0 in enumerate(starts):
                    dk_ref[pl.ds(k0, CHUNK), :] += _dot(dzs[2 * b], qs[0], _TN) + _dot(dzs[2 * b + 1], qs[1], _TN)
                    dv_ref[pl.ds(k0, CHUNK), :] += (_dot(probs[2 * b].astype(BF16), dos[0], _TN)
                                                    + _dot(probs[2 * b + 1].astype(BF16), dos[1], _TN))
                return dq, cls[0], cls[1], cps[0], cps[1]

            n_four = (qi + 1) // 4
            carry = lax.fori_loop(0, n_four, lambda jj, c: blocks([4 * jj + i for i in range(4)], c), (zc,) * 5)
            base = 4 * n_four
            carry = lax.fori_loop(0, (qi - base + 2) // 2, lambda jj, c: blocks([base + 2 * jj, base + 2 * jj + 1], c), carry)
            dq_ref[q_rows, :] = (carry[0] * SB_SCALE).astype(BF16)
            return 0

        lax.fori_loop(0, per_step, q_block, 0)

    rows = per_step * CHUNK
    blk = pl.BlockSpec((rows, LANES), lambda p, i: (i, p))
    col_blk = pl.BlockSpec((s, LANES), lambda p, i: (0, p))
    out = jax.ShapeDtypeStruct((s, D_SB), F32)
    return pl.pallas_call(
        body,
        name=name,
        grid=(N_PAIRS, nq // per_step),
        in_specs=[pl.BlockSpec((rows, LANES), lambda p, i: (i, QKV_BLOCK0 + p)),
                  pl.BlockSpec((s, LANES), lambda p, i: (0, QKV_BLOCK0 + N_PAIRS + p)),
                  pl.BlockSpec((s, LANES), lambda p, i: (0, QKV_BLOCK0 + 2 * N_PAIRS + p)),
                  pl.BlockSpec((rows, 2 * LANES), lambda p, i: (i, p)),
                  blk],
        out_specs=[blk, col_blk, col_blk],
        out_shape=[jax.ShapeDtypeStruct((s, D_SB), BF16), out, out],
        compiler_params=_cparams(("parallel", "arbitrary")),
    )(qkv, qkv, qkv, t_tot, do)


FFN_TILE = 256
FFN_COLS = 256
N_FF_BLOCKS = D_FF // FFN_COLS


def _ffn_act_fwd(up0, conv_w, conv_b, *, name):
    s = up0.shape[0]
    nt = s // FFN_TILE

    def body(xg_ref, xv_ref, wg_ref, wv_ref, bg_ref, bv_ref, act_ref, pg_ref, pv_ref):
        pg_ref[0:FFN_HALO, :] = jnp.zeros((FFN_HALO, FFN_COLS), F32)
        pv_ref[0:FFN_HALO, :] = jnp.zeros((FFN_HALO, FFN_COLS), F32)
        pg_ref[FFN_HALO:, :] = xg_ref[...].astype(F32)
        pv_ref[FFN_HALO:, :] = xv_ref[...].astype(F32)

        def tile(i, c):
            t0 = pl.multiple_of(i * FFN_TILE, FFN_TILE)
            outs = []
            for p_ref, w_ref, b_ref in ((pg_ref, wg_ref, bg_ref), (pv_ref, wv_ref, bv_ref)):
                window = p_ref[pl.ds(t0, FFN_TILE + FFN_HALO), :]
                acc = b_ref[...] + w_ref[2:3, :] * window[FFN_HALO:, :]
                for j in range(1, FFN_K):
                    acc = acc + w_ref[FFN_K - 1 - j:FFN_K - j, :] * _shift_down(window, j, FFN_HALO)
                outs.append(acc)
            gate, val = outs
            act_ref[pl.ds(t0, FFN_TILE), :] = (gate * _sigmoid(gate) * val).astype(BF16)
            return c

        lax.fori_loop(0, nt, tile, 0)

    gcol = lambda rows: pl.BlockSpec((rows, FFN_COLS), lambda j: (0, j))
    vcol = lambda rows: pl.BlockSpec((rows, FFN_COLS), lambda j: (0, j + N_FF_BLOCKS))
    return pl.pallas_call(
        body,
        name=name,
        grid=(N_FF_BLOCKS,),
        in_specs=[gcol(s), vcol(s), gcol(FFN_K), vcol(FFN_K), gcol(1), vcol(1)],
        out_specs=gcol(s),
        out_shape=jax.ShapeDtypeStruct((s, D_FF), BF16),
        scratch_shapes=[pltpu.VMEM((s + FFN_HALO, FFN_COLS), F32), pltpu.VMEM((s + FFN_HALO, FFN_COLS), F32)],
        compiler_params=_cparams(("parallel",)),
    )(up0, up0, conv_w, conv_w, conv_b, conv_b)


def _ffn_act_bwd(up0, dact, conv_w, conv_b, *, name):
    s = up0.shape[0]
    nt = s // FFN_TILE

    def body(xg_ref, xv_ref, da_ref, wg_ref, wv_ref, bg_ref, bv_ref, dxg_ref, dxv_ref, dwg_ref, dwv_ref, dbg_ref, dbv_ref,
             pg_ref, pv_ref, dg_ref, dv_ref):
        zeros = jnp.zeros((FFN_HALO, FFN_COLS), F32)
        for p_ref, x_ref in ((pg_ref, xg_ref), (pv_ref, xv_ref)):
            p_ref[0:FFN_HALO, :] = zeros
            p_ref[FFN_HALO:, :] = x_ref[...].astype(F32)
        dg_ref[s:s + FFN_HALO, :] = zeros
        dv_ref[s:s + FFN_HALO, :] = zeros
        for ref in (dwg_ref, dwv_ref, dbg_ref, dbv_ref):
            ref[...] = jnp.zeros_like(ref)

        def conv(p_ref, w_ref, b_ref, t0):
            window = p_ref[pl.ds(t0, FFN_TILE + FFN_HALO), :]
            taps = [_shift_down(window, j, FFN_HALO) for j in range(FFN_K)]
            out = b_ref[...]
            for j in range(FFN_K):
                out = out + w_ref[FFN_K - 1 - j:FFN_K - j, :] * taps[j]
            return out, taps

        def tile(i, c):
            t0 = pl.multiple_of(i * FFN_TILE, FFN_TILE)
            gate, taps_g = conv(pg_ref, wg_ref, bg_ref, t0)
            val, taps_v = conv(pv_ref, wv_ref, bv_ref, t0)
            da = da_ref[pl.ds(t0, FFN_TILE), :].astype(F32)
            sg = lax.logistic(gate)
            dgate = da * val * (sg * (1.0 + gate * (1.0 - sg)))
            dval = da * gate * sg
            dg_ref[pl.ds(t0, FFN_TILE), :] = dgate
            dv_ref[pl.ds(t0, FFN_TILE), :] = dval
            dbg_ref[...] += jnp.sum(dgate, axis=0, keepdims=True)
            dbv_ref[...] += jnp.sum(dval, axis=0, keepdims=True)
            for j in range(FFN_K):
                dwg_ref[FFN_K - 1 - j:FFN_K - j, :] += jnp.sum(dgate * taps_g[j], axis=0, keepdims=True)
                dwv_ref[FFN_K - 1 - j:FFN_K - j, :] += jnp.sum(dval * taps_v[j], axis=0, keepdims=True)
            return c

        lax.fori_loop(0, nt, tile, 0)

        def tile_dx(i, c):
            t0 = pl.multiple_of(i * FFN_TILE, FFN_TILE)
            for d_ref, w_ref, dx_ref in ((dg_ref, wg_ref, dxg_ref), (dv_ref, wv_ref, dxv_ref)):
                window = d_ref[pl.ds(t0, FFN_TILE + FFN_HALO), :]
                dx = w_ref[FFN_K - 1:FFN_K, :] * window[:FFN_TILE, :]
                for j in range(1, FFN_K):
                    dx = dx + w_ref[FFN_K - 1 - j:FFN_K - j, :] * _shift_up(window, j, FFN_TILE)
                dx_ref[pl.ds(t0, FFN_TILE), :] = dx.astype(BF16)
            return c

        lax.fori_loop(0, nt, tile_dx, 0)

    gcol = lambda rows: pl.BlockSpec((rows, FFN_COLS), lambda j: (0, j))
    vcol = lambda rows: pl.BlockSpec((rows, FFN_COLS), lambda j: (0, j + N_FF_BLOCKS))
    half = lambda rows, dtype: jax.ShapeDtypeStruct((rows, D_FF), dtype)
    padded = pltpu.VMEM((s + FFN_HALO, FFN_COLS), F32)
    return pl.pallas_call(
        body,
        name=name,
        grid=(N_FF_BLOCKS,),
        in_specs=[gcol(s), vcol(s), gcol(s), gcol(FFN_K), vcol(FFN_K), gcol(1), vcol(1)],
        out_specs=[gcol(s), gcol(s), gcol(FFN_K), gcol(FFN_K), gcol(1), gcol(1)],
        out_shape=[half(s, BF16), half(s, BF16), half(FFN_K, F32), half(FFN_K, F32), half(1, F32), half(1, F32)],
        scratch_shapes=[padded, padded, padded, padded],
        compiler_params=_cparams(("parallel",)),
    )(up0, up0, dact, conv_w, conv_w, conv_b, conv_b)


MESH = pl.DeviceIdType.MESH


def _position():
    x, y, c = lax.axis_index("x"), lax.axis_index("y"), lax.axis_index("c")
    return x, y, c, 4 * x + 2 * y + c


def _peer(k):
    x, y, c, _ = _position()
    px = 1 - x if k & 4 else x
    py = 1 - y if k & 2 else y
    pc = 1 - c if k & 1 else c
    return (px, py, pc), 4 * px + 2 * py + pc


_HBM = pl.BlockSpec(memory_space=pltpu.HBM)
_SEM = pl.BlockSpec(memory_space=pltpu.SEMAPHORE)
_DATAFLOW = pltpu.SideEffectType.DATAFLOW_SIDE_EFFECTING
N_PEERS = N_DEV - 1


class _SplitExchange:
    def __init__(self, src, *, kind, name):
        self.kind, self.name, self.dtype = kind, name, src.dtype
        scatter = kind.startswith("scatter")
        by_blocks = kind.endswith("blocks")
        self.scatter, self.by_blocks = scatter, by_blocks
        if by_blocks:
            self.r, self.cols, self.land_shape = None, None, src.shape if scatter else (N_DEV,) + src.shape
        else:
            self.r = src.shape[0] // N_DEV if scatter else src.shape[0]
            self.cols = src.shape[1]
            self.land_shape = (N_DEV, self.r, self.cols) if scatter else (N_DEV * self.r, self.cols)
        r = self.r

        def copies(src_ref, land_ref, send_sems, recv_sems, local_sem):
            me = _position()[3]

            def rows(ref, idx):
                return ref.at[pl.ds(pl.multiple_of(idx * r, r), r), :]

            if not scatter:
                outgoing = lambda idx: src_ref
            else:
                outgoing = (lambda idx: src_ref.at[idx]) if by_blocks else (lambda idx: rows(src_ref, idx))
            slot = (lambda idx: land_ref.at[idx]) if (scatter or by_blocks) else (lambda idx: rows(land_ref, idx))
            sends, recvs = [], []
            for k in range(1, N_DEV):
                peer, pidx = _peer(k)
                sems = dict(send_sem=send_sems[k - 1], recv_sem=recv_sems[k - 1], device_id=peer, device_id_type=MESH)
                sends.append(pltpu.make_async_remote_copy(src_ref=outgoing(pidx), dst_ref=slot(me), **sems))
                recvs.append(pltpu.make_async_remote_copy(src_ref=outgoing(pidx), dst_ref=slot(pidx), **sems))
            return sends, recvs, pltpu.make_async_copy(outgoing(me), slot(me), local_sem)

        self._copies = copies
        self.src = src

    @staticmethod
    def start(exchanges, name):
        n = len(exchanges)
        per = 2 * N_PEERS + 1

        def start_body(*refs):
            outs = refs[2 * n:]
            for i, ex in enumerate(exchanges):
                sems = outs[per * i:per * (i + 1)]
                sends, _, local = ex._copies(refs[2 * i], refs[2 * i + 1], sems[:N_PEERS], sems[N_PEERS:-1], sems[-1])
                for cp in sends + [local]:
                    cp.start()
            outs[-1][...] = jnp.zeros_like(outs[-1])

        sem = pltpu.SemaphoreType.DMA(())
        operands, thru_shapes = [], []
        for ex in exchanges:
            operands += [pltpu.with_memory_space_constraint(ex.src, pltpu.HBM),
                         pltpu.with_memory_space_constraint(lax.empty(ex.land_shape, ex.dtype), pltpu.HBM)]
            thru_shapes += [pltpu.HBM(ex.src.shape, ex.dtype), pltpu.HBM(ex.land_shape, ex.dtype)]
        out = pl.pallas_call(
            start_body,
            name=name,
            in_specs=(_HBM,) * (2 * n),
            out_specs=(_SEM,) * (per * n) + (_HBM,) * (2 * n) + (pl.BlockSpec(memory_space=pltpu.VMEM),),
            out_shape=(sem,) * (per * n) + tuple(thru_shapes) + (jax.ShapeDtypeStruct((8, LANES), F32),),
            input_output_aliases={i: per * n + i for i in range(2 * n)},
            compiler_params=pltpu.CompilerParams(has_side_effects=_DATAFLOW),
        )(*operands)
        for i, ex in enumerate(exchanges):
            ex.sems = out[per * i:per * (i + 1)]
            ex.src_thru, ex.land_thru = out[per * n + 2 * i], out[per * n + 2 * i + 1]
        return out[-1][0, 0]

    def finish(self, after):
        copies = self._copies

        def wait_body(src_ref, land_ref, *rest):
            sends, recvs, local = copies(src_ref, land_ref, rest[:N_PEERS], rest[N_PEERS:2 * N_PEERS], rest[2 * N_PEERS])
            for cp in sends:
                cp.wait_send()
            for cp in recvs:
                cp.wait_recv()
            local.wait()

        return pl.pallas_call(
            wait_body,
            name=f"{self.name}_wait",
            in_specs=(_HBM, _HBM) + (_SEM,) * (2 * N_PEERS + 1) + (pl.BlockSpec(memory_space=pl.ANY),),
            out_specs=(_HBM, _HBM),
            out_shape=(pltpu.HBM(self.src_thru.shape, self.dtype), pltpu.HBM(self.land_shape, self.dtype)),
            input_output_aliases={0: 0, 1: 1},
            compiler_params=pltpu.CompilerParams(has_side_effects=_DATAFLOW),
        )(self.src_thru, self.land_thru, *self.sems, after)[1]


ADAM_MAX_ROWS = 352
BF16_SUBLANES = 16


def _row_tile(rows):
    fitting = [t for t in range(BF16_SUBLANES, ADAM_MAX_ROWS + 1, BF16_SUBLANES) if rows % t == 0]
    return max(fitting) if fitting else rows


def _layer_parts_specs(n_layers, n_parts, tr, cols):
    return [pl.BlockSpec((n_parts, tr, cols), lambda l, i, j=j: (0, jnp.where(l == j, i, 0), 0)) for j in range(n_layers)]


def _select_layer_sum(p_refs):
    l = pl.program_id(0)
    g = None
    for j, p_ref in enumerate(p_refs):
        gj = p_ref[0].astype(F32)
        for k in range(1, p_ref.shape[0]):
            gj = gj + p_ref[k].astype(F32)
        g = gj if g is None else jnp.where(l == j, gj, g)
    return g


def _adamw(parts, w, m, v, *, name):
    n_layers, rows, cols = w.shape
    tr = _row_tile(rows)

    def body(*refs):
        w_ref, m_ref, v_ref, g_ref, d_ref, m2_ref, v2_ref = refs[n_layers:]
        g = _select_layer_sum(refs[:n_layers])
        m2 = ADAM_B1 * m_ref[...] + (1.0 - ADAM_B1) * g
        v2 = ADAM_B2 * v_ref[...] + (1.0 - ADAM_B2) * (g * g)
        m_hat = m2 / (1.0 - ADAM_B1 ** ADAM_STEP)
        v_hat = v2 / (1.0 - ADAM_B2 ** ADAM_STEP)
        g_ref[...] = g
        d_ref[...] = -ADAM_LR * (m_hat / (jnp.sqrt(v_hat) + ADAM_EPS) + ADAM_WD * w_ref[...])
        m2_ref[...] = m2
        v2_ref[...] = v2

    slab = pl.BlockSpec((None, tr, cols), lambda l, i: (l, i, 0))
    out = jax.ShapeDtypeStruct((n_layers, rows, cols), F32)
    p_specs = _layer_parts_specs(n_layers, parts[0].shape[0], tr, cols)
    return pl.pallas_call(
        body,
        name=name,
        grid=(n_layers, rows // tr),
        in_specs=p_specs + [slab, slab, slab],
        out_specs=[slab, slab, slab, slab],
        out_shape=[out, out, out, out],
        compiler_params=_cparams(("arbitrary", "arbitrary")),
    )(*parts, w, m, v)


SLAB_ROWS = 32
_SMALL_SHARDED = (("conv_w", (2, 31, 32)), ("ffn_conv_w", (2, 3, 704)))
_REPLICATED = (("g_mix", (2, 1024)), ("conv_b", (2, 256)), ("conv_ln_g", (2, 256)), ("conv_ln_b", (2, 256)),
               ("sgu_ln_g", (2, 256)), ("sgu_ln_b", (2, 256)), ("sgu_w", (2, 4, 128, 128)), ("sgu_b", (2, 4, 128)),
               ("g_out", (2, 1024)), ("g_ffn", (2, 1024)), ("ffn_conv_b", (2, 5632)), ("g_final", (1024,)))


def _seg_rows(n_elems):
    return -(-n_elems // LANES)


def _pack(arrays, lead=()):
    segs = []
    for a in arrays:
        flat = a.reshape(lead + (-1,)).astype(F32)
        pad = _seg_rows(flat.shape[-1]) * LANES - flat.shape[-1]
        if pad:
            flat = jnp.pad(flat, [(0, 0)] * len(lead) + [(0, pad)])
        segs.append(flat)
    flat = jnp.concatenate(segs, axis=-1)
    rows = flat.shape[-1] // LANES
    pad_rows = -rows % SLAB_ROWS
    if pad_rows:
        flat = jnp.pad(flat, [(0, 0)] * len(lead) + [(0, pad_rows * LANES)])
    return flat.reshape(lead + (rows + pad_rows, LANES))


def _unpack(slab, shapes, lead=()):
    flat = slab.reshape(lead + (-1,))
    out, off = [], 0
    for shape in shapes:
        n = math.prod(shape)
        out.append(flat[..., off:off + n].reshape(lead + tuple(shape)))
        off += _seg_rows(n) * LANES
    return out


def _split_last(full):
    split = full.shape[:-1] + (N_DEV, full.shape[-1] // N_DEV)
    return jnp.moveaxis(full.reshape(split), -2, 0)


def _join_last(blocks):
    moved = jnp.moveaxis(blocks, 0, -2)
    return moved.reshape(moved.shape[:-2] + (moved.shape[-2] * moved.shape[-1],))


def _gathered(wt, n, l, after):
    if isinstance(wt[n][l], _SplitExchange):
        wt[n][l] = wt[n][l].finish(after)
    return wt[n][l]


def _layer_fwd(l, x, wt, small):
    tag = f"l{l}"
    h = _rmsnorm_fwd(x, small["g_mix"][l][None], name=f"{tag}_norm_mix")
    w_in_t = _gathered(wt, "w_in_t", l, h)
    p = _matmul(h, w_in_t, "nt", name=f"{tag}_proj", out_dtype=BF16)
    p_ab = qkv = p
    ya = _mixer_a_fwd(p_ab, wt["conv_w"][l], small["conv_b"][l][None], small["conv_ln_g"][l][None],
                      small["conv_ln_b"][l][None], name=f"{tag}_mixer_a")
    bias = jnp.repeat(small["sgu_b"][l].T, HEAD_DIM, axis=1)
    yb = _sgu_fwd(p_ab, small["sgu_ln_g"][l][None], small["sgu_ln_b"][l][None], small["sgu_w"][l], bias,
                  name=f"{tag}_sgu")
    yc, t_tot = _attn_fwd(qkv, name=f"{tag}_attn")
    y = _combine_fwd(ya, yb, yc, small["g_out"][l][None], name=f"{tag}_combine")
    x1 = _matmul(y, _gathered(wt, "w_out", l, y), "nn", name=f"{tag}_out_proj", residual=x)
    h2 = _rmsnorm_fwd(x1, small["g_ffn"][l][None], name=f"{tag}_norm_ffn")
    up0 = _matmul(h2, _gathered(wt, "w_up_t", l, h2), "nt", name=f"{tag}_up", out_dtype=BF16)
    act = _ffn_act_fwd(up0, wt["ffn_conv_w"][l], small["ffn_conv_b"][l][None], name=f"{tag}_ffn_act")
    x2 = _matmul(act, _gathered(wt, "w_down", l, act), "nn", name=f"{tag}_down", residual=x1)
    saved = dict(x=x, h=h, p_ab=p, qkv=p, ya=ya, yb=yb, yc=yc, t_tot=t_tot, y=y, x1=x1, h2=h2, up0=up0,
                 act=act, bias=bias)
    return x2, saved


def _layer_bwd(l, dres, sv, wt, small, scattering, token):
    tag = f"l{l}b"
    g = {}

    def scatter(n, partial):
        scattering[n][l] = _SplitExchange(partial, kind="scatter_rows", name=f"scatter_{n}_l{l}")
        return _SplitExchange.start([scattering[n][l]], name=f"scatter_{n}_l{l}_start")

    dx2, dx2_b = dres
    dact = _matmul(dx2_b, wt["w_down"][l], "nt", name=f"{tag}_dact", out_dtype=BF16)
    tok = scatter("w_down", _matmul(sv["act"], dx2_b, "tn", name=f"{tag}_dw_down", out_dtype=BF16))
    dup_g, dup_v, dwg, dwv, dbg, dbv = _ffn_act_bwd(sv["up0"], dact, wt["ffn_conv_w"][l], small["ffn_conv_b"][l][None] + tok + token,
                                                    name=f"{tag}_ffn_act")
    g["ffn_conv_w"] = jnp.concatenate([dwg, dwv], axis=1)
    g["ffn_conv_b"] = jnp.concatenate([dbg[0], dbv[0]])
    dh2 = _matmul(dup_g, wt["w_up_t"][l], "nn", name=f"{tag}_dh2_gate")
    dh2 = _matmul(dup_v, wt["w_up_t"][l], "nn", name=f"{tag}_dh2_val", b_k0=D_FF, residual=dh2)
    dw_up = _matmul(dup_g, sv["h2"], "tn", name=f"{tag}_dw_up_gate", out_dtype=BF16, rows=(2 * D_FF, 0))
    dw_up = _matmul(dup_v, sv["h2"], "tn", name=f"{tag}_dw_up_val", out_dtype=BF16, rows=(2 * D_FF, D_FF), into=dw_up)
    tok = scatter("w_up_t", dw_up)
    dx1, dx1_b, dg = _rmsnorm_bwd(sv["x1"], small["g_ffn"][l][None] + tok, dh2, dx2, name=f"{tag}_norm_ffn")
    g["g_ffn"] = dg[0]
    dy = _matmul(dx1_b, wt["w_out"][l], "nt", name=f"{tag}_dy")
    tok = scatter("w_out", _matmul(sv["y"], dx1_b, "tn", name=f"{tag}_dw_out", out_dtype=BF16))
    dya, dyb, dyc, dg = _combine_bwd(dy, sv["ya"], sv["yb"], sv["yc"], small["g_out"][l][None] + tok,
                                     name=f"{tag}_combine")
    g["g_out"] = dg[0]
    dq, dk, dv = _attn_bwd(sv["qkv"], sv["t_tot"], dyc, name=f"{tag}_attn")
    dp_b, g["sgu_w"], db, dg, dbeta = _sgu_bwd(sv["p_ab"], dyb, small["sgu_ln_g"][l][None], small["sgu_ln_b"][l][None],
                                               small["sgu_w"][l], sv["bias"], name=f"{tag}_sgu")
    g["sgu_b"] = db[:, :N_SGU_HEADS].T
    g["sgu_ln_g"], g["sgu_ln_b"] = dg[0], dbeta[0]
    dp_a, g["conv_w"], dcb, dg, dbeta = _mixer_a_bwd(sv["p_ab"], dya, wt["conv_w"][l], small["conv_b"][l][None],
                                                     small["conv_ln_g"][l][None], small["conv_ln_b"][l][None],
                                                     name=f"{tag}_mixer_a")
    g["conv_b"], g["conv_ln_g"], g["conv_ln_b"] = dcb[0], dg[0], dbeta[0]
    dp = jnp.concatenate([dp_a, dp_b, dq, dk.astype(BF16), dv.astype(BF16)], axis=1)
    tok = scatter("w_in_t", _matmul(dp, sv["h"], "tn", name=f"{tag}_dw_in", out_dtype=BF16))
    dh = _matmul(dp, wt["w_in_t"][l], "nn", name=f"{tag}_dh")
    dx, dx_b, dg = _rmsnorm_bwd(sv["x"], small["g_mix"][l][None] + tok, dh, dx1, name=f"{tag}_norm_mix")
    g["g_mix"] = dg[0]
    return (dx, dx_b), g


_BIG = ("w_in_t", "w_out", "w_up_t", "w_down")


def kernel(x, g_mix, w_in, conv_w, conv_b, conv_ln_g, conv_ln_b, sgu_ln_g, sgu_ln_b, sgu_w, sgu_b, g_out, w_out, g_ffn, w_up, ffn_conv_w, ffn_conv_b, w_down, g_final, loss_target, m_g_mix, m_w_in, m_conv_w, m_conv_b, m_conv_ln_g, m_conv_ln_b, m_sgu_ln_g, m_sgu_ln_b, m_sgu_w, m_sgu_b, m_g_out, m_w_out, m_g_ffn, m_w_up, m_ffn_conv_w, m_ffn_conv_b, m_w_down, m_g_final, v_g_mix, v_w_in, v_conv_w, v_conv_b, v_conv_ln_g, v_conv_ln_b, v_sgu_ln_g, v_sgu_ln_b, v_sgu_w, v_sgu_b, v_g_out, v_w_out, v_g_ffn, v_w_up, v_ffn_conv_w, v_ffn_conv_b, v_w_down, v_g_final):
    given = dict(locals())
    n_layers = g_mix.shape[0]
    layers = range(n_layers)
    small_sharded = [n for n, _ in _SMALL_SHARDED]
    replicated = [n for n, _ in _REPLICATED]
    small = {n: given[n] for n in replicated}

    filters = _SplitExchange(_pack([given[n] for n in small_sharded]), kind="gather_blocks", name="gather_filters")
    wt = {n: [None] * n_layers for n in _BIG}
    wt["w_in_t"][0] = _SplitExchange(w_in[0].T.astype(BF16), kind="gather_rows", name="gather_w_in_t_l0")
    tok = _SplitExchange.start([filters, wt["w_in_t"][0]], name="gather_first_start")
    w_in, w_out, w_up, w_down, tok = lax.optimization_barrier((w_in, w_out, w_up, w_down, tok))
    shard = {"w_in_t": [w_in[l].T.astype(BF16) for l in layers], "w_out": [w_out[l].astype(BF16) for l in layers],
             "w_up_t": [w_up[l].T.astype(BF16) for l in layers], "w_down": [w_down[l].astype(BF16) for l in layers]}
    later = [(n, l) for l in layers for n in _BIG if (n, l) != ("w_in_t", 0)]
    for n, l in later:
        wt[n][l] = _SplitExchange(shard[n][l], kind="gather_rows", name=f"gather_{n}_l{l}")
    small["g_mix"] = g_mix + tok + _SplitExchange.start([wt[n][l] for n, l in later], name="gather_weights_start")
    gathered_filters = filters.finish(small["g_mix"])
    for n, blocks in zip(small_sharded, _unpack(gathered_filters, [s for _, s in _SMALL_SHARDED], lead=(N_DEV,))):
        wt[n] = _join_last(blocks)

    xs = x[0]
    saved = []
    for l in layers:
        xs, sv = _layer_fwd(l, xs, wt, small)
        saved.append(sv)
    loss_tile, dx, dx_b, dgf = _loss_head(xs, g_final[None], loss_target[0], name="loss_head")
    dres = (dx, dx_b)
    scattering = {n: [None] * n_layers for n in _BIG}
    layered = [n for n in replicated if n not in ("g_final", "sgu_w")]
    slabs, sgu_w_parts = [None] * n_layers, [None] * n_layers
    tok = 0.0
    for l in reversed(layers):
        dres, g = _layer_bwd(l, dres, saved[l], wt, small, scattering, tok)
        own = _pack([_split_last(g[n]) for n in small_sharded], lead=(N_DEV,))
        shared = _pack([g[n] for n in layered] + [dgf[0]])
        slab = jnp.concatenate([own, jnp.broadcast_to(shared[None], (N_DEV,) + shared.shape)], axis=1)
        slabs[l] = _SplitExchange(slab, kind="scatter_blocks", name=f"scatter_small_grads_l{l}")
        sgu_w_parts[l] = _SplitExchange(g["sgu_w"].reshape(-1, LANES).astype(BF16), kind="gather_blocks",
                                        name=f"gather_sgu_w_grads_l{l}")
        tok = _SplitExchange.start([slabs[l], sgu_w_parts[l]], name=f"small_grads_l{l}_start")
    n_own = own.shape[1]

    after_backward = jnp.full((8, LANES), tok)
    received = {n: [scattering[n][l].finish(after_backward) for l in layers] for n in _BIG}
    out = {}

    def update(n, parts, transposed=False):
        turn = (lambda a: jnp.swapaxes(a, 1, 2)) if transposed else (lambda a: a)
        results = _adamw(parts, turn(given[n]), turn(given["m_" + n]), turn(given["v_" + n]), name=f"adamw_{n}")
        for pre, res in zip(("grad_", "delta_", "new_m_", "new_v_"), results):
            out[pre + n] = turn(res)
        return results[0][0, :8, :LANES]

    update("w_out", received["w_out"])
    update("w_down", received["w_down"])
    update("w_in", received["w_in_t"], transposed=True)
    big_updated = update("w_up", received["w_up_t"], transposed=True)

    as_rows = lambda a: a.reshape(n_layers, -1, LANES)
    results = _adamw([ex.finish(big_updated) for ex in sgu_w_parts], as_rows(sgu_w), as_rows(m_sgu_w), as_rows(v_sgu_w),
                     name="adamw_sgu_w")
    for pre, res in zip(("grad_", "delta_", "new_m_", "new_v_"), results):
        out[pre + "sgu_w"] = res.reshape(sgu_w.shape)

    per_layer_g_final = {pre: jnp.broadcast_to(given[pre + "g_final"], (n_layers,) + g_final.shape) for pre in ("", "m_", "v_")}
    stacks = [jnp.concatenate([_pack([given[pre + n] for n in small_sharded], lead=(n_layers,)),
                               _pack([given[pre + n] for n in layered] + [per_layer_g_final[pre]], lead=(n_layers,))], axis=1)
              for pre in ("", "m_", "v_")]
    results = _adamw([slabs[l].finish(big_updated) for l in layers], *stacks, name="adamw_small")
    for pre, res in zip(("grad_", "delta_", "new_m_", "new_v_"), results):
        unpacked = (_unpack(res[:, :n_own], [s[1:] for _, s in _SMALL_SHARDED], lead=(n_layers,))
                    + _unpack(res[:, n_own:], [s[1:] for n, s in _REPLICATED if n in layered] + [g_final.shape], lead=(n_layers,)))
        for n, a in zip(small_sharded + layered + ["g_final"], unpacked):
            out[pre + n] = a[0] if n == "g_final" else a

    loss = lax.psum(loss_tile[0, 0], ("x", "y", "c"))
    order = list(_WEIGHT_ORDER)
    return (loss, dres[0][None], *[out["grad_" + n] for n in order], *[out["delta_" + n] for n in order],
            *[out["new_m_" + n] for n in order], *[out["new_v_" + n] for n in order])


_WEIGHT_ORDER = ("g_mix", "w_in", "conv_w", "conv_b", "conv_ln_g", "conv_ln_b", "sgu_ln_g", "sgu_ln_b", "sgu_w", "sgu_b",
                 "g_out", "w_out", "g_ffn", "w_up", "ffn_conv_w", "ffn_conv_b", "w_down", "g_final")
```

```python
import math

import jax
import jax.numpy as jnp
from jax import lax
from jax.experimental import pallas as pl
from jax.experimental.pallas import tpu as pltpu

F32 = jnp.float32
BF16 = jnp.bfloat16

N_DEV = 8
D_MODEL = 1024
HEAD_DIM = 64
D_CONV = 256
D_SGU = 256
D_SB = 512
D_AB = 2 * D_CONV + 2 * D_SGU
CONV_K = 31
CONV_HALO = 32
FFN_K = 3
FFN_HALO = 8
D_FF = 2816
CHUNK = 128
EPS = 1e-6
LANES = 128

ADAM_LR = 0.001
ADAM_B1 = 0.9
ADAM_B2 = 0.999
ADAM_EPS = 1e-08
ADAM_WD = 0.01
ADAM_STEP = 10

VMEM_LIMIT = 56 * 1024 * 1024


def _cparams(sem=None):
    return pltpu.CompilerParams(dimension_semantics=sem, vmem_limit_bytes=VMEM_LIMIT)


def _sigmoid(x):
    return 1.0 / (1.0 + jnp.exp(-x))


_INV_SQRT2 = 1.0 / math.sqrt(2.0)
_INV_SQRT2PI = 1.0 / math.sqrt(2.0 * math.pi)


def _gelu(x):
    return 0.5 * x * (1.0 + lax.erf(x * _INV_SQRT2))


def _gelu_grad(x):
    return 0.5 * (1.0 + lax.erf(x * _INV_SQRT2)) + x * jnp.exp(-0.5 * x * x) * _INV_SQRT2PI


def _dot(a, b, dims):
    return lax.dot_general(a, b, (dims, ((), ())), preferred_element_type=F32)


_NN = ((1,), (0,))
_NT = ((1,), (1,))
_TN = ((0,), (0,))


def _split_bf16(x):
    hi = x.astype(BF16)
    lo = (x - hi.astype(F32)).astype(BF16)
    return jnp.concatenate([hi, lo], axis=1)


def _matmul(a, b, mode, *, name, out_dtype=F32, residual=None, n=None, b_n0=0, b_k0=0, rows=None, into=None):
    if mode == "nn":
        (m, k), n = a.shape, (n or b.shape[1])
    elif mode == "nt":
        (m, k), n = a.shape, (n or b.shape[0])
    else:
        (k, m), n = a.shape, b.shape[1]
    has_res = residual is not None
    tm, tn = _matmul_tiles(m, n, k, a.dtype.itemsize, b.dtype.itemsize, jnp.dtype(out_dtype).itemsize, has_res, b_n0)
    j0 = b_n0 // tn
    total_rows, first_row = rows or (m, 0)
    assert b_k0 % k == 0 and first_row % tm == 0
    kb, i0 = b_k0 // k, first_row // tm

    if mode == "nn":
        a_spec = pl.BlockSpec((tm, k), lambda i, j: (i, 0))
        b_spec = pl.BlockSpec((k, tn), lambda i, j: (kb, j + j0))
        dims = _NN
    elif mode == "nt":
        a_spec = pl.BlockSpec((tm, k), lambda i, j: (i, 0))
        b_spec = pl.BlockSpec((tn, k), lambda i, j: (j + j0, 0))
        dims = _NT
    else:
        a_spec = pl.BlockSpec((k, tm), lambda i, j: (0, i))
        b_spec = pl.BlockSpec((k, tn), lambda i, j: (0, j))
        dims = _TN
    o_spec = pl.BlockSpec((tm, tn), lambda i, j: (i + i0, j))
    r_spec = pl.BlockSpec((tm, tn), lambda i, j: (i, j))

    def body(*refs):
        a_ref, b_ref = refs[:2]
        acc = _dot(a_ref[...].astype(BF16), b_ref[...].astype(BF16), dims)
        if has_res:
            acc = acc + refs[2][...]
        refs[-1][...] = acc.astype(out_dtype)

    in_specs = [a_spec, b_spec] + ([r_spec] if has_res else [])
    args = (a, b) + ((residual,) if has_res else ())
    aliases = {}
    if into is not None:
        aliases = {len(args): 0}
        in_specs.append(pl.BlockSpec(memory_space=pl.ANY))
        args += (into,)

        def body(*refs, inner=body):
            inner(*refs[:len(args) - 1], refs[-1])

    return pl.pallas_call(
        body,
        name=name,
        grid=(m // tm, n // tn),
        in_specs=in_specs,
        out_specs=o_spec,
        out_shape=jax.ShapeDtypeStruct((total_rows, n), out_dtype),
        input_output_aliases=aliases,
        compiler_params=_cparams(("parallel", "parallel")),
    )(*args)


MATMUL_VMEM_BUDGET = 40 * 1024 * 1024


def _matmul_tiles(m, n, k, a_bytes, b_bytes, out_bytes, has_res, n_offset):
    def divisors(size, cap, also=0):
        return [t for t in range(cap, 0, -LANES) if size % t == 0 and also % t == 0] or [size]

    for tm in divisors(m, 256 if (has_res and k > 2048) else 1024):
        for tn in divisors(n, 1408, n_offset):
            blocks = tm * k * a_bytes + k * tn * b_bytes + tm * tn * (out_bytes + (4 if has_res else 0))
            if 2 * blocks <= MATMUL_VMEM_BUDGET:
                return tm, tn
    raise ValueError(f"no matmul tiling for {m} x {n} x {k}")


ROW_TILE = 512


def _rmsnorm_fwd(x, g, *, name):
    s, d = x.shape

    def body(x_ref, g_ref, h_ref):
        xv = x_ref[...]
        r = lax.rsqrt(jnp.mean(xv * xv, axis=-1, keepdims=True) + EPS)
        h_ref[...] = (xv * r * g_ref[...]).astype(BF16)

    return pl.pallas_call(
        body,
        name=name,
        grid=(s // ROW_TILE,),
        in_specs=[pl.BlockSpec((ROW_TILE, d), lambda i: (i, 0)), pl.BlockSpec((1, d), lambda i: (0, 0))],
        out_specs=pl.BlockSpec((ROW_TILE, d), lambda i: (i, 0)),
        out_shape=jax.ShapeDtypeStruct((s, d), BF16),
        compiler_params=_cparams(("parallel",)),
    )(x, g)


def _rmsnorm_bwd(x, g, dh, dres, *, name):
    s, d = x.shape

    def body(x_ref, g_ref, dh_ref, dres_ref, dx_ref, dxb_ref, dg_ref):
        xv = x_ref[...]
        r = lax.rsqrt(jnp.mean(xv * xv, axis=-1, keepdims=True) + EPS)
        xhat = xv * r
        dhv = dh_ref[...]
        dxhat = dhv * g_ref[...]
        dx = dres_ref[...] + r * (dxhat - xhat * jnp.mean(dxhat * xhat, axis=-1, keepdims=True))
        dx_ref[...] = dx
        dxb_ref[...] = dx.astype(BF16)
        part = jnp.sum(dhv * xhat, axis=0, keepdims=True)

        @pl.when(pl.program_id(0) == 0)
        def _():
            dg_ref[...] = part

        @pl.when(pl.program_id(0) > 0)
        def _():
            dg_ref[...] += part

    row = pl.BlockSpec((ROW_TILE, d), lambda i: (i, 0))
    vec = pl.BlockSpec((1, d), lambda i: (0, 0))
    return pl.pallas_call(
        body,
        name=name,
        grid=(s // ROW_TILE,),
        in_specs=[row, vec, row, row],
        out_specs=[row, row, vec],
        out_shape=[jax.ShapeDtypeStruct((s, d), F32), jax.ShapeDtypeStruct((s, d), BF16),
                   jax.ShapeDtypeStruct((1, d), F32)],
        compiler_params=_cparams(("arbitrary",)),
    )(x, g, dh, dres)


def _loss_head(x, g, target, *, name):
    s, d = x.shape

    def body(x_ref, g_ref, t_ref, loss_ref, dx_ref, dxb_ref, dg_ref):
        xv = x_ref[...]
        gv = g_ref[...]
        r = lax.rsqrt(jnp.mean(xv * xv, axis=-1, keepdims=True) + EPS)
        xhat = xv * r
        diff = xhat * gv - t_ref[...]
        dy = diff * (1.0 / d)
        dxhat = dy * gv
        dx = r * (dxhat - xhat * jnp.mean(dxhat * xhat, axis=-1, keepdims=True))
        dx_ref[...] = dx
        dxb_ref[...] = dx.astype(BF16)
        dg_part = jnp.sum(dy * xhat, axis=0, keepdims=True)
        row_loss = jnp.sum(diff * diff, axis=-1, keepdims=True)
        loss_part = jnp.sum(row_loss, axis=0, keepdims=True) * (0.5 / d)

        @pl.when(pl.program_id(0) == 0)
        def _():
            dg_ref[...] = dg_part
            loss_ref[...] = jnp.broadcast_to(loss_part, loss_ref.shape)

        @pl.when(pl.program_id(0) > 0)
        def _():
            dg_ref[...] += dg_part
            loss_ref[...] += jnp.broadcast_to(loss_part, loss_ref.shape)

    row = pl.BlockSpec((ROW_TILE, d), lambda i: (i, 0))
    vec = pl.BlockSpec((1, d), lambda i: (0, 0))
    tile = pl.BlockSpec((8, LANES), lambda i: (0, 0))
    return pl.pallas_call(
        body,
        name=name,
        grid=(s // ROW_TILE,),
        in_specs=[row, vec, row],
        out_specs=[tile, row, row, vec],
        out_shape=[jax.ShapeDtypeStruct((8, LANES), F32), jax.ShapeDtypeStruct((s, d), F32),
                   jax.ShapeDtypeStruct((s, d), BF16), jax.ShapeDtypeStruct((1, d), F32)],
        compiler_params=_cparams(("arbitrary",)),
    )(x, g, target)


_BRANCHES = ((0, D_CONV), (D_CONV, D_SGU), (D_CONV + D_SGU, D_SB))


def _combine_fwd(ya, yb, yc, g, *, name):
    s = ya.shape[0]

    def body(ya_ref, yb_ref, yc_ref, g_ref, y_ref):
        for ref, (off, w) in zip((ya_ref, yb_ref, yc_ref), _BRANCHES):
            v = ref[...]
            r = lax.rsqrt(jnp.mean(v * v, axis=-1, keepdims=True) + EPS)
            y_ref[:, off:off + w] = (v * r * g_ref[:, off:off + w]).astype(BF16)

    def row(w):
        return pl.BlockSpec((ROW_TILE, w), lambda i: (i, 0))

    return pl.pallas_call(
        body,
        name=name,
        grid=(s // ROW_TILE,),
        in_specs=[row(D_CONV), row(D_SGU), row(D_SB), pl.BlockSpec((1, D_MODEL), lambda i: (0, 0))],
        out_specs=row(D_MODEL),
        out_shape=jax.ShapeDtypeStruct((s, D_MODEL), BF16),
        compiler_params=_cparams(("parallel",)),
    )(ya, yb, yc, g)


def _combine_bwd(dy, ya, yb, yc, g, *, name):
    s = ya.shape[0]

    def body(dy_ref, ya_ref, yb_ref, yc_ref, g_ref, dya_ref, dyb_ref, dyc_ref, dg_ref):
        first = pl.program_id(0) == 0
        for ref, dref, (off, w) in zip((ya_ref, yb_ref, yc_ref), (dya_ref, dyb_ref, dyc_ref), _BRANCHES):
            v = ref[...]
            r = lax.rsqrt(jnp.mean(v * v, axis=-1, keepdims=True) + EPS)
            n = v * r
            dout = dy_ref[:, off:off + w]
            dn = dout * g_ref[:, off:off + w]
            dref[...] = r * (dn - n * jnp.mean(dn * n, axis=-1, keepdims=True))
            part = jnp.sum(dout * n, axis=0, keepdims=True)

            @pl.when(first)
            def _():
                dg_ref[:, off:off + w] = part

            @pl.when(jnp.logical_not(first))
            def _():
                dg_ref[:, off:off + w] += part

    def row(w):
        return pl.BlockSpec((ROW_TILE, w), lambda i: (i, 0))

    vec = pl.BlockSpec((1, D_MODEL), lambda i: (0, 0))
    return pl.pallas_call(
        body,
        name=name,
        grid=(s // ROW_TILE,),
        in_specs=[row(D_MODEL), row(D_CONV), row(D_SGU), row(D_SB), vec],
        out_specs=[row(D_CONV), row(D_SGU), row(D_SB), vec],
        out_shape=[jax.ShapeDtypeStruct((s, D_CONV), F32), jax.ShapeDtypeStruct((s, D_SGU), F32),
                   jax.ShapeDtypeStruct((s, D_SB), F32), jax.ShapeDtypeStruct((1, D_MODEL), F32)],
        compiler_params=_cparams(("arbitrary",)),
    )(dy, ya, yb, yc, g)


CONV_TILE = 128


def _shift_down(window, j, halo):
    return pltpu.roll(window, j, 0)[halo:, :] if j else window[halo:, :]


def _shift_up(window, j, n_out):
    n = window.shape[0]
    return pltpu.roll(window, n - j, 0)[:n_out, :] if j else window[:n_out, :]


def _mixer_a_fwd(p_ab, conv_w, conv_b, ln_g, ln_b, *, name):
    s = p_ab.shape[0]
    nt = s // CONV_TILE

    def body(p_ref, w_ref, b_ref, g_ref, beta_ref, y_ref, h_ref):
        h_ref[0:CONV_HALO, :] = jnp.zeros((CONV_HALO, D_CONV), F32)

        def glu(i, c):
            t0 = pl.multiple_of(i * CONV_TILE, CONV_TILE)
            a = p_ref[pl.ds(t0, CONV_TILE), 0:D_CONV].astype(F32)
            gate = p_ref[pl.ds(t0, CONV_TILE), D_CONV:2 * D_CONV].astype(F32)
            h_ref[pl.ds(t0 + CONV_HALO, CONV_TILE), :] = a * _sigmoid(gate)
            return c

        lax.fori_loop(0, nt, glu, 0)

        def conv(i, c):
            t0 = pl.multiple_of(i * CONV_TILE, CONV_TILE)
            window = h_ref[pl.ds(t0, CONV_TILE + CONV_HALO), :]
            acc = jnp.zeros((CONV_TILE, D_CONV), F32) + b_ref[...]
            for k in range(CONV_K):
                acc = acc + w_ref[k:k + 1, :] * _shift_down(window, CONV_K - 1 - k, CONV_HALO)
            mu = jnp.mean(acc, axis=-1, keepdims=True)
            xc = acc - mu
            rstd = lax.rsqrt(jnp.mean(xc * xc, axis=-1, keepdims=True) + EPS)
            z = xc * rstd * g_ref[...] + beta_ref[...]
            y_ref[pl.ds(t0, CONV_TILE), :] = z * _sigmoid(z)
            return c

        lax.fori_loop(0, nt, conv, 0)

    full = lambda shape: pl.BlockSpec(shape, lambda i: (0, 0))
    return pl.pallas_call(
        body,
        name=name,
        grid=(1,),
        in_specs=[full((s, 2 * D_CONV)), full((CONV_K, D_CONV)), full((1, D_CONV)), full((1, D_CONV)),
                  full((1, D_CONV))],
        out_specs=full((s, D_CONV)),
        out_shape=jax.ShapeDtypeStruct((s, D_CONV), F32),
        scratch_shapes=[pltpu.VMEM((s + CONV_HALO, D_CONV), F32)],
        compiler_params=_cparams(("arbitrary",)),
    )(p_ab, conv_w, conv_b, ln_g, ln_b)


def _mixer_a_bwd(p_ab, dya, conv_w, conv_b, ln_g, ln_b, *, name):
    s = p_ab.shape[0]
    nt = s // CONV_TILE

    def body(p_ref, dy_ref, w_ref, b_ref, g_ref, beta_ref, dp_ref, dw_ref, db_ref, dg_ref, dbeta_ref, h_ref, dc_ref):
        h_ref[0:CONV_HALO, :] = jnp.zeros((CONV_HALO, D_CONV), F32)
        dc_ref[s:s + CONV_HALO, :] = jnp.zeros((CONV_HALO, D_CONV), F32)
        dw_ref[...] = jnp.zeros_like(dw_ref)
        db_ref[...] = jnp.zeros_like(db_ref)
        dg_ref[...] = jnp.zeros_like(dg_ref)
        dbeta_ref[...] = jnp.zeros_like(dbeta_ref)

        def glu(i, c):
            t0 = pl.multiple_of(i * CONV_TILE, CONV_TILE)
            a = p_ref[pl.ds(t0, CONV_TILE), 0:D_CONV].astype(F32)
            gate = p_ref[pl.ds(t0, CONV_TILE), D_CONV:2 * D_CONV].astype(F32)
            h_ref[pl.ds(t0 + CONV_HALO, CONV_TILE), :] = a * _sigmoid(gate)
            return c

        lax.fori_loop(0, nt, glu, 0)

        def conv_bwd(i, c):
            t0 = pl.multiple_of(i * CONV_TILE, CONV_TILE)
            window = h_ref[pl.ds(t0, CONV_TILE + CONV_HALO), :]
            taps = [_shift_down(window, CONV_K - 1 - k, CONV_HALO) for k in range(CONV_K)]
            acc = jnp.zeros((CONV_TILE, D_CONV), F32) + b_ref[...]
            for k in range(CONV_K):
                acc = acc + w_ref[k:k + 1, :] * taps[k]
            mu = jnp.mean(acc, axis=-1, keepdims=True)
            xc = acc - mu
            rstd = lax.rsqrt(jnp.mean(xc * xc, axis=-1, keepdims=True) + EPS)
            xhat = xc * rstd
            z = xhat * g_ref[...] + beta_ref[...]
            sg = _sigmoid(z)
            dz = dy_ref[pl.ds(t0, CONV_TILE), :] * (sg * (1.0 + z * (1.0 - sg)))
            dg_ref[...] += jnp.sum(dz * xhat, axis=0, keepdims=True)
            dbeta_ref[...] += jnp.sum(dz, axis=0, keepdims=True)
            dxhat = dz * g_ref[...]
            dc = rstd * (dxhat - jnp.mean(dxhat, axis=-1, keepdims=True)
                         - xhat * jnp.mean(dxhat * xhat, axis=-1, keepdims=True))
            dc_ref[pl.ds(t0, CONV_TILE), :] = dc
            db_ref[...] += jnp.sum(dc, axis=0, keepdims=True)
            for k in range(CONV_K):
                dw_ref[k:k + 1, :] += jnp.sum(dc * taps[k], axis=0, keepdims=True)
            return c

        lax.fori_loop(0, nt, conv_bwd, 0)

        def glu_bwd(i, c):
            t0 = pl.multiple_of(i * CONV_TILE, CONV_TILE)
            window = dc_ref[pl.ds(t0, CONV_TILE + CONV_HALO), :]
            dh = jnp.zeros((CONV_TILE, D_CONV), F32)
            for j in range(CONV_K):
                dh = dh + w_ref[CONV_K - 1 - j:CONV_K - j, :] * _shift_up(window, j, CONV_TILE)
            a = p_ref[pl.ds(t0, CONV_TILE), 0:D_CONV].astype(F32)
            sg = _sigmoid(p_ref[pl.ds(t0, CONV_TILE), D_CONV:2 * D_CONV].astype(F32))
            dp_ref[pl.ds(t0, CONV_TILE), 0:D_CONV] = (dh * sg).astype(BF16)
            dp_ref[pl.ds(t0, CONV_TILE), D_CONV:2 * D_CONV] = (dh * a * sg * (1.0 - sg)).astype(BF16)
            return c

        lax.fori_loop(0, nt, glu_bwd, 0)

    full = lambda shape: pl.BlockSpec(shape, lambda i: (0, 0))
    vec = jax.ShapeDtypeStruct((1, D_CONV), F32)
    return pl.pallas_call(
        body,
        name=name,
        grid=(1,),
        in_specs=[full((s, 2 * D_CONV)), full((s, D_CONV)), full((CONV_K, D_CONV)), full((1, D_CONV)),
                  full((1, D_CONV)), full((1, D_CONV))],
        out_specs=[full((s, 2 * D_CONV)), full((CONV_K, D_CONV)), full((1, D_CONV)), full((1, D_CONV)),
                   full((1, D_CONV))],
        out_shape=[jax.ShapeDtypeStruct((s, 2 * D_CONV), BF16), jax.ShapeDtypeStruct((CONV_K, D_CONV), F32),
                   vec, vec, vec],
        scratch_shapes=[pltpu.VMEM((s + CONV_HALO, D_CONV), F32), pltpu.VMEM((s + CONV_HALO, D_CONV), F32)],
        compiler_params=_cparams(("arbitrary",)),
    )(p_ab, dya, conv_w, conv_b, ln_g, ln_b)


N_SGU_HEADS = D_SGU // HEAD_DIM


def _head_masks(width):
    lane = lax.broadcasted_iota(jnp.int32, (1, width), 1)
    return [(lane >= h * HEAD_DIM) & (lane < (h + 1) * HEAD_DIM) for h in range(width // HEAD_DIM)]


def _tril_mask():
    r = lax.broadcasted_iota(jnp.int32, (CHUNK, CHUNK), 0)
    c = lax.broadcasted_iota(jnp.int32, (CHUNK, CHUNK), 1)
    return c <= r


def _sgu_norm(bv, g, beta):
    vg = _gelu(bv)
    mu = jnp.mean(vg, axis=-1, keepdims=True)
    xc = vg - mu
    rstd = lax.rsqrt(jnp.mean(xc * xc, axis=-1, keepdims=True) + EPS)
    xhat = xc * rstd
    return xhat, rstd, xhat * g + beta


def _sgu_fwd(p_ab, ln_g, ln_b, w_s, bias, *, name):
    s = p_ab.shape[0]

    def body(p_ref, g_ref, beta_ref, w_ref, bias_ref, y_ref):
        u = _gelu(p_ref[:, 0:D_SGU].astype(F32))
        _, _, vn = _sgu_norm(p_ref[:, D_SGU:2 * D_SGU].astype(F32), g_ref[...], beta_ref[...])
        vb = vn.astype(BF16)
        tril = _tril_mask()
        mixed = bias_ref[...]
        for h, m in enumerate(_head_masks(D_SGU)):
            wh = jnp.where(tril, w_ref[h], 0.0).astype(BF16)
            mixed = mixed + _dot(wh, jnp.where(m, vb, jnp.zeros_like(vb)), _NN)
        y_ref[...] = u * mixed

    return pl.pallas_call(
        body,
        name=name,
        grid=(s // CHUNK,),
        in_specs=[pl.BlockSpec((CHUNK, 2 * D_SGU), lambda i: (i, 1)),
                  pl.BlockSpec((1, D_SGU), lambda i: (0, 0)), pl.BlockSpec((1, D_SGU), lambda i: (0, 0)),
                  pl.BlockSpec((N_SGU_HEADS, CHUNK, CHUNK), lambda i: (0, 0, 0)),
                  pl.BlockSpec((CHUNK, D_SGU), lambda i: (0, 0))],
        out_specs=pl.BlockSpec((CHUNK, D_SGU), lambda i: (i, 0)),
        out_shape=jax.ShapeDtypeStruct((s, D_SGU), F32),
        compiler_params=_cparams(("parallel",)),
    )(p_ab, ln_g, ln_b, w_s, bias)


def _sgu_bwd(p_ab, dyb, ln_g, ln_b, w_s, bias, *, name):
    s = p_ab.shape[0]
    n_chunks = s // CHUNK

    def body(p_ref, dy_ref, g_ref, beta_ref, w_ref, bias_ref, dp_ref, dw_ref, db_ref, dg_ref, dbeta_ref, dbias_ref):
        @pl.when(pl.program_id(0) == 0)
        def _():
            dw_ref[...] = jnp.zeros_like(dw_ref)
            dbias_ref[...] = jnp.zeros_like(dbias_ref)
            dg_ref[...] = jnp.zeros_like(dg_ref)
            dbeta_ref[...] = jnp.zeros_like(dbeta_ref)

        bu = p_ref[:, 0:D_SGU].astype(F32)
        bv = p_ref[:, D_SGU:2 * D_SGU].astype(F32)
        u = _gelu(bu)
        gv = g_ref[...]
        xhat, rstd, vn = _sgu_norm(bv, gv, beta_ref[...])
        vb = vn.astype(BF16)
        tril = _tril_mask()
        masks = _head_masks(D_SGU)
        whs = [jnp.where(tril, w_ref[h], 0.0).astype(BF16) for h in range(N_SGU_HEADS)]
        mixed = bias_ref[...]
        for h, m in enumerate(masks):
            mixed = mixed + _dot(whs[h], jnp.where(m, vb, jnp.zeros_like(vb)), _NN)
        dy = dy_ref[...]
        dp_ref[:, 0:D_SGU] = (dy * mixed * _gelu_grad(bu)).astype(BF16)
        dmixed = dy * u
        dbias_ref[...] += dmixed
        dmb = dmixed.astype(BF16)
        dvn = jnp.zeros((CHUNK, D_SGU), F32)
        for h, m in enumerate(masks):
            dmh = jnp.where(m, dmb, jnp.zeros_like(dmb))
            dvn = dvn + _dot(whs[h], dmh, _TN)
            dw_ref[h] += jnp.where(tril, _dot(dmh, vb, _NT), 0.0)
        dg_ref[...] += jnp.sum(dvn * xhat, axis=0, keepdims=True)
        dbeta_ref[...] += jnp.sum(dvn, axis=0, keepdims=True)
        dxhat = dvn * gv
        dvg = rstd * (dxhat - jnp.mean(dxhat, axis=-1, keepdims=True)
                      - xhat * jnp.mean(dxhat * xhat, axis=-1, keepdims=True))
        dp_ref[:, D_SGU:2 * D_SGU] = (dvg * _gelu_grad(bv)).astype(BF16)

        @pl.when(pl.program_id(0) == n_chunks - 1)
        def _():
            chan = lax.broadcasted_iota(jnp.int32, (D_SGU, LANES), 0)
            head = lax.broadcasted_iota(jnp.int32, (D_SGU, LANES), 1)
            to_head = jnp.where(chan // HEAD_DIM == head, 1.0, 0.0).astype(BF16)
            db_ref[...] = _dot(_split_bf16(dbias_ref[...]), jnp.concatenate([to_head, to_head], axis=0), _NN)

    vec = pl.BlockSpec((1, D_SGU), lambda i: (0, 0))
    wspec = pl.BlockSpec((N_SGU_HEADS, CHUNK, CHUNK), lambda i: (0, 0, 0))
    bspec = pl.BlockSpec((CHUNK, D_SGU), lambda i: (0, 0))
    return pl.pallas_call(
        body,
        name=name,
        grid=(n_chunks,),
        in_specs=[pl.BlockSpec((CHUNK, 2 * D_SGU), lambda i: (i, 1)), pl.BlockSpec((CHUNK, D_SGU), lambda i: (i, 0)),
                  vec, vec, wspec, bspec],
        out_specs=[pl.BlockSpec((CHUNK, 2 * D_SGU), lambda i: (i, 0)), wspec,
                   pl.BlockSpec((CHUNK, LANES), lambda i: (0, 0)), vec, vec],
        out_shape=[jax.ShapeDtypeStruct((s, 2 * D_SGU), BF16),
                   jax.ShapeDtypeStruct((N_SGU_HEADS, CHUNK, CHUNK), F32),
                   jax.ShapeDtypeStruct((CHUNK, LANES), F32),
                   jax.ShapeDtypeStruct((1, D_SGU), F32), jax.ShapeDtypeStruct((1, D_SGU), F32)],
        scratch_shapes=[pltpu.VMEM((CHUNK, D_SGU), F32)],
        compiler_params=_cparams(("arbitrary",)),
    )(p_ab, dyb, ln_g, ln_b, w_s, bias)


N_PAIRS = D_SB // LANES
QKV_BLOCK0 = D_AB // LANES
SB_SCALE = HEAD_DIM ** -0.5


def _sb_logits(z, valid):
    nz = -z
    t = jnp.log(1.0 + jnp.exp(jnp.minimum(z, nz)))
    l1 = jnp.minimum(nz, 0.0) - t
    if valid is not None:
        l1 = jnp.where(valid, l1, 0.0)
    return l1, jnp.minimum(z, 0.0) - t


def _split_hi_lo(x):
    hi = lax.bitcast_convert_type(lax.bitcast_convert_type(x, jnp.uint32) & jnp.uint32(0xFFFF0000), F32)
    return jnp.concatenate([hi, x - hi], axis=1)


def _cumsum_operand(keep):
    half = jnp.concatenate([keep.astype(F32), jnp.ones((CHUNK, CHUNK), F32)], axis=1)
    return jnp.concatenate([half, half], axis=0)


Q_BLOCKS_PER_STEP = 4


def _q_blocks_per_step(nq):
    return next(n for n in (Q_BLOCKS_PER_STEP, 2, 1) if nq % n == 0)


def _attn_fwd(qkv, *, name):
    s = qkv.shape[0]
    nq = s // CHUNK
    per_step = _q_blocks_per_step(nq)

    def body(q_ref, k_ref, v_ref, o_ref, t_ref):
        masks = _head_masks(LANES)
        row = lax.broadcasted_iota(jnp.int32, (CHUNK, CHUNK), 0)
        col = lax.broadcasted_iota(jnp.int32, (CHUNK, CHUNK), 1)
        after_op = _cumsum_operand(row > col)
        cmr = col - row
        zc = jnp.zeros((CHUNK, LANES), F32)

        def q_block(sub, _):
            qi = pl.program_id(1) * per_step + sub
            q_rows = pl.ds(pl.multiple_of(sub * CHUNK, CHUNK), CHUNK)
            q = q_ref[q_rows, :] * SB_SCALE
            zero = jnp.zeros_like(q)
            qs = [jnp.where(m, q, zero) for m in masks]

            def blocks(js, carry):
                o, c0, c1 = carry
                starts = [pl.multiple_of(jnp.maximum(j, 0) * CHUNK, CHUNK) for j in js]
                valids = [cmr < jnp.where(j >= 0, (qi - j) * CHUNK, -CHUNK) for j in js]
                units = [(h, b) for b in range(len(js)) for h in range(2)]
                zs = [_dot(qs[h], k_ref[pl.ds(starts[b], CHUNK), :], _NT) for h, b in units]
                logits = [_sb_logits(z, valids[b]) for z, (h, b) in zip(zs, units)]
                sums = [_dot(_split_hi_lo(l1), after_op, _NN) for l1, _ in logits]
                cs = [c0, c1]
                probs = []
                for (h, b), (_, lb), sm in zip(units, logits, sums):
                    probs.append(jnp.where(valids[b], jnp.exp(lb + sm[:, :CHUNK] + cs[h]), 0.0))
                    cs[h] = cs[h] + sm[:, CHUNK:]
                for (h, b), a in zip(units, probs):
                    o = o + _dot(a.astype(BF16), jnp.where(masks[h], v_ref[pl.ds(starts[b], CHUNK), :], zero), _NN)
                return o, cs[0], cs[1]

            n_four = (qi + 1) // 4
            carry = lax.fori_loop(0, n_four, lambda jj, c: blocks([qi - 4 * jj - i for i in range(4)], c), (zc,) * 3)
            top = qi - 4 * n_four
            o, c0, c1 = lax.fori_loop(0, (top + 2) // 2, lambda jj, c: blocks([top - 2 * jj, top - 2 * jj - 1], c), carry)
            o_ref[q_rows, :] = o
            t_ref[q_rows, 0:LANES] = c0
            t_ref[q_rows, LANES:2 * LANES] = c1
            return 0

        lax.fori_loop(0, per_step, q_block, 0)

    rows = per_step * CHUNK
    return pl.pallas_call(
        body,
        name=name,
        grid=(N_PAIRS, nq // per_step),
        in_specs=[pl.BlockSpec((rows, LANES), lambda p, i: (i, QKV_BLOCK0 + p)),
                  pl.BlockSpec((s, LANES), lambda p, i: (0, QKV_BLOCK0 + N_PAIRS + p)),
                  pl.BlockSpec((s, LANES), lambda p, i: (0, QKV_BLOCK0 + 2 * N_PAIRS + p))],
        out_specs=[pl.BlockSpec((rows, LANES), lambda p, i: (i, p)),
                   pl.BlockSpec((rows, 2 * LANES), lambda p, i: (i, p))],
        out_shape=[jax.ShapeDtypeStruct((s, D_SB), F32), jax.ShapeDtypeStruct((s, 2 * D_SB), F32)],
        compiler_params=_cparams(("parallel", "parallel")),
    )(qkv, qkv, qkv)


def _attn_bwd(qkv, t_tot, do, *, name):
    s = qkv.shape[0]
    nq = s // CHUNK
    per_step = _q_blocks_per_step(nq)

    def body(q_ref, k_ref, v_ref, t_ref, do_ref, dq_ref, dk_ref, dv_ref):
        @pl.when(pl.program_id(1) == 0)
        def _():
            dk_ref[...] = jnp.zeros_like(dk_ref)
            dv_ref[...] = jnp.zeros_like(dv_ref)

        masks = _head_masks(LANES)
        row = lax.broadcasted_iota(jnp.int32, (CHUNK, CHUNK), 0)
        col = lax.broadcasted_iota(jnp.int32, (CHUNK, CHUNK), 1)
        upto_op = _cumsum_operand(row <= col)
        before_op = _cumsum_operand(row < col)
        cmr = col - row
        zc = jnp.zeros((CHUNK, LANES), F32)

        def q_block(sub, _):
            qi = pl.program_id(1) * per_step + sub
            q_rows = pl.ds(pl.multiple_of(sub * CHUNK, CHUNK), CHUNK)
            q = q_ref[q_rows, :] * SB_SCALE
            dob = do_ref[q_rows, :].astype(BF16)
            zero = jnp.zeros_like(q)
            qs = [jnp.where(m, q, zero) for m in masks]
            dos = [jnp.where(m, dob, zero) for m in masks]
            tots = [t_ref[q_rows, 0:LANES], t_ref[q_rows, LANES:2 * LANES]]

            def blocks(js, carry):
                dq, cl0, cl1, cp0, cp1 = carry
                starts = [pl.multiple_of(jnp.minimum(j, nq - 1) * CHUNK, CHUNK) for j in js]
                valids = [cmr < (qi - j) * CHUNK for j in js]
                units = [(h, b) for b in range(len(js)) for h in range(2)]
                zs = [_dot(qs[h], k_ref[pl.ds(starts[b], CHUNK), :], _NT) for h, b in units]
                da_dots = lambda: [_dot(dos[h], v_ref[pl.ds(starts[b], CHUNK), :], _NT) for h, b in units]
                das = da_dots() if len(js) < 4 else None
                logits = [_sb_logits(z, valids[b]) for z, (h, b) in zip(zs, units)]
                sums = [_dot(_split_hi_lo(l1), upto_op, _NN) for l1, _ in logits]
                das = das or da_dots()
                cls, cps = [cl0, cl1], [cp0, cp1]
                probs, gs = [], []
                for (h, b), (_, lb), sm, da in zip(units, logits, sums, das):
                    a = jnp.where(valids[b], jnp.exp(lb + (tots[h] - cls[h] - sm[:, :CHUNK])), 0.0)
                    probs.append(a)
                    gs.append(a * da)
                    cls[h] = cls[h] + sm[:, CHUNK:]
                sums_g = [_dot(_split_hi_lo(g), before_op, _NN) for g in gs]
                if len(js) == 4:
                    for b, k0 in enumerate(starts):
                        dv_ref[pl.ds(k0, CHUNK), :] += (_dot(probs[2 * b].astype(BF16), dos[0], _TN)
                                                        + _dot(probs[2 * b + 1].astype(BF16), dos[1], _TN))
                dzs = []
                for (h, b), (_, lb), g, sg in zip(units, logits, gs, sums_g):
                    dz = g - (g + sg[:, :CHUNK] + cps[h]) * jnp.exp(lb)
                    dzs.append(jnp.where(valids[b], dz, 0.0).astype(BF16))
                    cps[h] = cps[h] + sg[:, CHUNK:]
                for (h, b), dzb in zip(units, dzs):
                    dq = dq + _dot(dzb, jnp.where(masks[h], k_ref[pl.ds(starts[b], CHUNK), :], zero), _NN)
                for b, k0 in enumerate(starts):
                    dk_ref[pl.ds(k0, CHUNK), :] += _dot(dzs[2 * b], qs[0], _TN) + _dot(dzs[2 * b + 1], qs[1], _TN)
                    if len(js) < 4:
                        dv_ref[pl.ds(k0, CHUNK), :] += (_dot(probs[2 * b].astype(BF16), dos[0], _TN)
                                                        + _dot(probs[2 * b + 1].astype(BF16), dos[1], _TN))
                return dq, cls[0], cls[1], cps[0], cps[1]

            n_four = (qi + 1) // 4
            carry = lax.fori_loop(0, n_four, lambda jj, c: blocks([4 * jj + i for i in range(4)], c), (zc,) * 5)
            base = 4 * n_four
            carry = lax.fori_loop(0, (qi - base + 2) // 2, lambda jj, c: blocks([base + 2 * jj, base + 2 * jj + 1], c), carry)
            dq_ref[q_rows, :] = (carry[0] * SB_SCALE).astype(BF16)
            return 0

        lax.fori_loop(0, per_step, q_block, 0)

    rows = per_step * CHUNK
    blk = pl.BlockSpec((rows, LANES), lambda p, i: (i, p))
    col_blk = pl.BlockSpec((s, LANES), lambda p, i: (0, p))
    out = jax.ShapeDtypeStruct((s, D_SB), F32)
    return pl.pallas_call(
        body,
        name=name,
        grid=(N_PAIRS, nq // per_step),
        in_specs=[pl.BlockSpec((rows, LANES), lambda p, i: (i, QKV_BLOCK0 + p)),
                  pl.BlockSpec((s, LANES), lambda p, i: (0, QKV_BLOCK0 + N_PAIRS + p)),
                  pl.BlockSpec((s, LANES), lambda p, i: (0, QKV_BLOCK0 + 2 * N_PAIRS + p)),
                  pl.BlockSpec((rows, 2 * LANES), lambda p, i: (i, p)),
                  blk],
        out_specs=[blk, col_blk, col_blk],
        out_shape=[jax.ShapeDtypeStruct((s, D_SB), BF16), out, out],
        compiler_params=_cparams(("parallel", "arbitrary")),
    )(qkv, qkv, qkv, t_tot, do)


FFN_TILE = 256
FFN_COLS = 256
N_FF_BLOCKS = D_FF // FFN_COLS


def _ffn_act_fwd(up0, conv_w, conv_b, *, name):
    s = up0.shape[0]
    nt = s // FFN_TILE

    def body(xg_ref, xv_ref, wg_ref, wv_ref, bg_ref, bv_ref, act_ref, pg_ref, pv_ref):
        pg_ref[0:FFN_HALO, :] = jnp.zeros((FFN_HALO, FFN_COLS), F32)
        pv_ref[0:FFN_HALO, :] = jnp.zeros((FFN_HALO, FFN_COLS), F32)
        pg_ref[FFN_HALO:, :] = xg_ref[...].astype(F32)
        pv_ref[FFN_HALO:, :] = xv_ref[...].astype(F32)

        def tile(i, c):
            t0 = pl.multiple_of(i * FFN_TILE, FFN_TILE)
            outs = []
            for p_ref, w_ref, b_ref in ((pg_ref, wg_ref, bg_ref), (pv_ref, wv_ref, bv_ref)):
                window = p_ref[pl.ds(t0, FFN_TILE + FFN_HALO), :]
                acc = b_ref[...] + w_ref[2:3, :] * window[FFN_HALO:, :]
                for j in range(1, FFN_K):
                    acc = acc + w_ref[FFN_K - 1 - j:FFN_K - j, :] * _shift_down(window, j, FFN_HALO)
                outs.append(acc)
            gate, val = outs
            act_ref[pl.ds(t0, FFN_TILE), :] = (gate * _sigmoid(gate) * val).astype(BF16)
            return c

        lax.fori_loop(0, nt, tile, 0)

    gcol = lambda rows: pl.BlockSpec((rows, FFN_COLS), lambda j: (0, j))
    vcol = lambda rows: pl.BlockSpec((rows, FFN_COLS), lambda j: (0, j + N_FF_BLOCKS))
    return pl.pallas_call(
        body,
        name=name,
        grid=(N_FF_BLOCKS,),
        in_specs=[gcol(s), vcol(s), gcol(FFN_K), vcol(FFN_K), gcol(1), vcol(1)],
        out_specs=gcol(s),
        out_shape=jax.ShapeDtypeStruct((s, D_FF), BF16),
        scratch_shapes=[pltpu.VMEM((s + FFN_HALO, FFN_COLS), F32), pltpu.VMEM((s + FFN_HALO, FFN_COLS), F32)],
        compiler_params=_cparams(("parallel",)),
    )(up0, up0, conv_w, conv_w, conv_b, conv_b)


def _ffn_act_bwd(up0, dact, conv_w, conv_b, *, name):
    s = up0.shape[0]
    nt = s // FFN_TILE

    def body(xg_ref, xv_ref, da_ref, wg_ref, wv_ref, bg_ref, bv_ref, dxg_ref, dxv_ref, dwg_ref, dwv_ref, dbg_ref, dbv_ref,
             pg_ref, pv_ref, dg_ref, dv_ref):
        zeros = jnp.zeros((FFN_HALO, FFN_COLS), F32)
        for p_ref, x_ref in ((pg_ref, xg_ref), (pv_ref, xv_ref)):
            p_ref[0:FFN_HALO, :] = zeros
            p_ref[FFN_HALO:, :] = x_ref[...].astype(F32)
        dg_ref[s:s + FFN_HALO, :] = zeros
        dv_ref[s:s + FFN_HALO, :] = zeros
        for ref in (dwg_ref, dwv_ref, dbg_ref, dbv_ref):
            ref[...] = jnp.zeros_like(ref)

        def conv(p_ref, w_ref, b_ref, t0):
            window = p_ref[pl.ds(t0, FFN_TILE + FFN_HALO), :]
            taps = [_shift_down(window, j, FFN_HALO) for j in range(FFN_K)]
            out = b_ref[...]
            for j in range(FFN_K):
                out = out + w_ref[FFN_K - 1 - j:FFN_K - j, :] * taps[j]
            return out, taps

        def tile(i, c):
            t0 = pl.multiple_of(i * FFN_TILE, FFN_TILE)
            gate, taps_g = conv(pg_ref, wg_ref, bg_ref, t0)
            val, taps_v = conv(pv_ref, wv_ref, bv_ref, t0)
            da = da_ref[pl.ds(t0, FFN_TILE), :].astype(F32)
            sg = lax.logistic(gate)
            dgate = da * val * (sg * (1.0 + gate * (1.0 - sg)))
            dval = da * gate * sg
            dg_ref[pl.ds(t0, FFN_TILE), :] = dgate
            dv_ref[pl.ds(t0, FFN_TILE), :] = dval
            dbg_ref[...] += jnp.sum(dgate, axis=0, keepdims=True)
            dbv_ref[...] += jnp.sum(dval, axis=0, keepdims=True)
            for j in range(FFN_K):
                dwg_ref[FFN_K - 1 - j:FFN_K - j, :] += jnp.sum(dgate * taps_g[j], axis=0, keepdims=True)
                dwv_ref[FFN_K - 1 - j:FFN_K - j, :] += jnp.sum(dval * taps_v[j], axis=0, keepdims=True)
            return c

        lax.fori_loop(0, nt, tile, 0)

        def tile_dx(i, c):
            t0 = pl.multiple_of(i * FFN_TILE, FFN_TILE)
            for d_ref, w_ref, dx_ref in ((dg_ref, wg_ref, dxg_ref), (dv_ref, wv_ref, dxv_ref)):
                window = d_ref[pl.ds(t0, FFN_TILE + FFN_HALO), :]
                dx = w_ref[FFN_K - 1:FFN_K, :] * window[:FFN_TILE, :]
                for j in range(1, FFN_K):
                    dx = dx + w_ref[FFN_K - 1 - j:FFN_K - j, :] * _shift_up(window, j, FFN_TILE)
                dx_ref[pl.ds(t0, FFN_TILE), :] = dx.astype(BF16)
            return c

        lax.fori_loop(0, nt, tile_dx, 0)

    gcol = lambda rows: pl.BlockSpec((rows, FFN_COLS), lambda j: (0, j))
    vcol = lambda rows: pl.BlockSpec((rows, FFN_COLS), lambda j: (0, j + N_FF_BLOCKS))
    half = lambda rows, dtype: jax.ShapeDtypeStruct((rows, D_FF), dtype)
    padded = pltpu.VMEM((s + FFN_HALO, FFN_COLS), F32)
    return pl.pallas_call(
        body,
        name=name,
        grid=(N_FF_BLOCKS,),
        in_specs=[gcol(s), vcol(s), gcol(s), gcol(FFN_K), vcol(FFN_K), gcol(1), vcol(1)],
        out_specs=[gcol(s), gcol(s), gcol(FFN_K), gcol(FFN_K), gcol(1), gcol(1)],
        out_shape=[half(s, BF16), half(s, BF16), half(FFN_K, F32), half(FFN_K, F32), half(1, F32), half(1, F32)],
        scratch_shapes=[padded, padded, padded, padded],
        compiler_params=_cparams(("parallel",)),
    )(up0, up0, dact, conv_w, conv_w, conv_b, conv_b)


MESH = pl.DeviceIdType.MESH


def _position():
    x, y, c = lax.axis_index("x"), lax.axis_index("y"), lax.axis_index("c")
    return x, y, c, 4 * x + 2 * y + c


def _peer(k):
    x, y, c, _ = _position()
    px = 1 - x if k & 4 else x
    py = 1 - y if k & 2 else y
    pc = 1 - c if k & 1 else c
    return (px, py, pc), 4 * px + 2 * py + pc


_HBM = pl.BlockSpec(memory_space=pltpu.HBM)
_SEM = pl.BlockSpec(memory_space=pltpu.SEMAPHORE)
_DATAFLOW = pltpu.SideEffectType.DATAFLOW_SIDE_EFFECTING
N_PEERS = N_DEV - 1


class _SplitExchange:
    def __init__(self, src, *, kind, name):
        self.kind, self.name, self.dtype = kind, name, src.dtype
        scatter = kind.startswith("scatter")
        by_blocks = kind.endswith("blocks")
        self.scatter, self.by_blocks = scatter, by_blocks
        if by_blocks:
            self.r, self.cols, self.land_shape = None, None, src.shape if scatter else (N_DEV,) + src.shape
        else:
            self.r = src.shape[0] // N_DEV if scatter else src.shape[0]
            self.cols = src.shape[1]
            self.land_shape = (N_DEV, self.r, self.cols) if scatter else (N_DEV * self.r, self.cols)
        r = self.r

        def copies(src_ref, land_ref, send_sems, recv_sems, local_sem):
            me = _position()[3]

            def rows(ref, idx):
                return ref.at[pl.ds(pl.multiple_of(idx * r, r), r), :]

            if not scatter:
                outgoing = lambda idx: src_ref
            else:
                outgoing = (lambda idx: src_ref.at[idx]) if by_blocks else (lambda idx: rows(src_ref, idx))
            slot = (lambda idx: land_ref.at[idx]) if (scatter or by_blocks) else (lambda idx: rows(land_ref, idx))
            sends, recvs = [], []
            for k in range(1, N_DEV):
                peer, pidx = _peer(k)
                sems = dict(send_sem=send_sems[k - 1], recv_sem=recv_sems[k - 1], device_id=peer, device_id_type=MESH)
                sends.append(pltpu.make_async_remote_copy(src_ref=outgoing(pidx), dst_ref=slot(me), **sems))
                recvs.append(pltpu.make_async_remote_copy(src_ref=outgoing(pidx), dst_ref=slot(pidx), **sems))
            return sends, recvs, pltpu.make_async_copy(outgoing(me), slot(me), local_sem)

        self._copies = copies
        self.src = src

    @staticmethod
    def start(exchanges, name):
        n = len(exchanges)
        per = 2 * N_PEERS + 1

        def start_body(*refs):
            outs = refs[2 * n:]
            for i, ex in enumerate(exchanges):
                sems = outs[per * i:per * (i + 1)]
                sends, _, local = ex._copies(refs[2 * i], refs[2 * i + 1], sems[:N_PEERS], sems[N_PEERS:-1], sems[-1])
                for cp in sends + [local]:
                    cp.start()
            outs[-1][...] = jnp.zeros_like(outs[-1])

        sem = pltpu.SemaphoreType.DMA(())
        operands, thru_shapes = [], []
        for ex in exchanges:
            operands += [pltpu.with_memory_space_constraint(ex.src, pltpu.HBM),
                         pltpu.with_memory_space_constraint(lax.empty(ex.land_shape, ex.dtype), pltpu.HBM)]
            thru_shapes += [pltpu.HBM(ex.src.shape, ex.dtype), pltpu.HBM(ex.land_shape, ex.dtype)]
        out = pl.pallas_call(
            start_body,
            name=name,
            in_specs=(_HBM,) * (2 * n),
            out_specs=(_SEM,) * (per * n) + (_HBM,) * (2 * n) + (pl.BlockSpec(memory_space=pltpu.VMEM),),
            out_shape=(sem,) * (per * n) + tuple(thru_shapes) + (jax.ShapeDtypeStruct((8, LANES), F32),),
            input_output_aliases={i: per * n + i for i in range(2 * n)},
            compiler_params=pltpu.CompilerParams(has_side_effects=_DATAFLOW),
        )(*operands)
        for i, ex in enumerate(exchanges):
            ex.sems = out[per * i:per * (i + 1)]
            ex.src_thru, ex.land_thru = out[per * n + 2 * i], out[per * n + 2 * i + 1]
        return out[-1][0, 0]

    def finish(self, after):
        copies = self._copies

        def wait_body(src_ref, land_ref, *rest):
            sends, recvs, local = copies(src_ref, land_ref, rest[:N_PEERS], rest[N_PEERS:2 * N_PEERS], rest[2 * N_PEERS])
            for cp in sends:
                cp.wait_send()
            for cp in recvs:
                cp.wait_recv()
            local.wait()

        return pl.pallas_call(
            wait_body,
            name=f"{self.name}_wait",
            in_specs=(_HBM, _HBM) + (_SEM,) * (2 * N_PEERS + 1) + (pl.BlockSpec(memory_space=pl.ANY),),
            out_specs=(_HBM, _HBM),
            out_shape=(pltpu.HBM(self.src_thru.shape, self.dtype), pltpu.HBM(self.land_shape, self.dtype)),
            input_output_aliases={0: 0, 1: 1},
            compiler_params=pltpu.CompilerParams(has_side_effects=_DATAFLOW),
        )(self.src_thru, self.land_thru, *self.sems, after)[1]


ADAM_MAX_ROWS = 352
BF16_SUBLANES = 16


def _row_tile(rows):
    fitting = [t for t in range(BF16_SUBLANES, ADAM_MAX_ROWS + 1, BF16_SUBLANES) if rows % t == 0]
    return max(fitting) if fitting else rows


def _layer_parts_specs(n_layers, n_parts, tr, cols):
    return [pl.BlockSpec((n_parts, tr, cols), lambda l, i, j=j: (0, jnp.where(l == j, i, 0), 0)) for j in range(n_layers)]


def _select_layer_sum(p_refs):
    l = pl.program_id(0)
    g = None
    for j, p_ref in enumerate(p_refs):
        gj = p_ref[0].astype(F32)
        for k in range(1, p_ref.shape[0]):
            gj = gj + p_ref[k].astype(F32)
        g = gj if g is None else jnp.where(l == j, gj, g)
    return g


def _adamw(parts, w, m, v, *, name):
    n_layers, rows, cols = w.shape
    tr = _row_tile(rows)

    def body(*refs):
        w_ref, m_ref, v_ref, g_ref, d_ref, m2_ref, v2_ref = refs[n_layers:]
        g = _select_layer_sum(refs[:n_layers])
        m2 = ADAM_B1 * m_ref[...] + (1.0 - ADAM_B1) * g
        v2 = ADAM_B2 * v_ref[...] + (1.0 - ADAM_B2) * (g * g)
        m_hat = m2 / (1.0 - ADAM_B1 ** ADAM_STEP)
        v_hat = v2 / (1.0 - ADAM_B2 ** ADAM_STEP)
        g_ref[...] = g
        d_ref[...] = -ADAM_LR * (m_hat / (jnp.sqrt(v_hat) + ADAM_EPS) + ADAM_WD * w_ref[...])
        m2_ref[...] = m2
        v2_ref[...] = v2

    slab = pl.BlockSpec((None, tr, cols), lambda l, i: (l, i, 0))
    out = jax.ShapeDtypeStruct((n_layers, rows, cols), F32)
    p_specs = _layer_parts_specs(n_layers, parts[0].shape[0], tr, cols)
    return pl.pallas_call(
        body,
        name=name,
        grid=(n_layers, rows // tr),
        in_specs=p_specs + [slab, slab, slab],
        out_specs=[slab, slab, slab, slab],
        out_shape=[out, out, out, out],
        compiler_params=_cparams(("arbitrary", "arbitrary")),
    )(*parts, w, m, v)


SLAB_ROWS = 32
_SMALL_SHARDED = (("conv_w", (2, 31, 32)), ("ffn_conv_w", (2, 3, 704)))
_REPLICATED = (("g_mix", (2, 1024)), ("conv_b", (2, 256)), ("conv_ln_g", (2, 256)), ("conv_ln_b", (2, 256)),
               ("sgu_ln_g", (2, 256)), ("sgu_ln_b", (2, 256)), ("sgu_w", (2, 4, 128, 128)), ("sgu_b", (2, 4, 128)),
               ("g_out", (2, 1024)), ("g_ffn", (2, 1024)), ("ffn_conv_b", (2, 5632)), ("g_final", (1024,)))


def _seg_rows(n_elems):
    return -(-n_elems // LANES)


def _pack(arrays, lead=()):
    segs = []
    for a in arrays:
        flat = a.reshape(lead + (-1,)).astype(F32)
        pad = _seg_rows(flat.shape[-1]) * LANES - flat.shape[-1]
        if pad:
            flat = jnp.pad(flat, [(0, 0)] * len(lead) + [(0, pad)])
        segs.append(flat)
    flat = jnp.concatenate(segs, axis=-1)
    rows = flat.shape[-1] // LANES
    pad_rows = -rows % SLAB_ROWS
    if pad_rows:
        flat = jnp.pad(flat, [(0, 0)] * len(lead) + [(0, pad_rows * LANES)])
    return flat.reshape(lead + (rows + pad_rows, LANES))


def _unpack(slab, shapes, lead=()):
    flat = slab.reshape(lead + (-1,))
    out, off = [], 0
    for shape in shapes:
        n = math.prod(shape)
        out.append(flat[..., off:off + n].reshape(lead + tuple(shape)))
        off += _seg_rows(n) * LANES
    return out


def _split_last(full):
    split = full.shape[:-1] + (N_DEV, full.shape[-1] // N_DEV)
    return jnp.moveaxis(full.reshape(split), -2, 0)


def _join_last(blocks):
    moved = jnp.moveaxis(blocks, 0, -2)
    return moved.reshape(moved.shape[:-2] + (moved.shape[-2] * moved.shape[-1],))


def _gathered(wt, n, l, after):
    if isinstance(wt[n][l], _SplitExchange):
        wt[n][l] = wt[n][l].finish(after)
    return wt[n][l]


def _layer_fwd(l, x, wt, small):
    tag = f"l{l}"
    h = _rmsnorm_fwd(x, small["g_mix"][l][None], name=f"{tag}_norm_mix")
    w_in_t = _gathered(wt, "w_in_t", l, h)
    p = _matmul(h, w_in_t, "nt", name=f"{tag}_proj", out_dtype=BF16)
    p_ab = qkv = p
    ya = _mixer_a_fwd(p_ab, wt["conv_w"][l], small["conv_b"][l][None], small["conv_ln_g"][l][None],
                      small["conv_ln_b"][l][None], name=f"{tag}_mixer_a")
    bias = jnp.repeat(small["sgu_b"][l].T, HEAD_DIM, axis=1)
    yb = _sgu_fwd(p_ab, small["sgu_ln_g"][l][None], small["sgu_ln_b"][l][None], small["sgu_w"][l], bias,
                  name=f"{tag}_sgu")
    yc, t_tot = _attn_fwd(qkv, name=f"{tag}_attn")
    y = _combine_fwd(ya, yb, yc, small["g_out"][l][None], name=f"{tag}_combine")
    x1 = _matmul(y, _gathered(wt, "w_out", l, y), "nn", name=f"{tag}_out_proj", residual=x)
    h2 = _rmsnorm_fwd(x1, small["g_ffn"][l][None], name=f"{tag}_norm_ffn")
    up0 = _matmul(h2, _gathered(wt, "w_up_t", l, h2), "nt", name=f"{tag}_up", out_dtype=BF16)
    act = _ffn_act_fwd(up0, wt["ffn_conv_w"][l], small["ffn_conv_b"][l][None], name=f"{tag}_ffn_act")
    x2 = _matmul(act, _gathered(wt, "w_down", l, act), "nn", name=f"{tag}_down", residual=x1)
    saved = dict(x=x, h=h, p_ab=p, qkv=p, ya=ya, yb=yb, yc=yc, t_tot=t_tot, y=y, x1=x1, h2=h2, up0=up0,
                 act=act, bias=bias)
    return x2, saved


def _layer_bwd(l, dres, sv, wt, small, scattering, token):
    tag = f"l{l}b"
    g = {}

    def scatter(n, partial):
        scattering[n][l] = _SplitExchange(partial, kind="scatter_rows", name=f"scatter_{n}_l{l}")
        return _SplitExchange.start([scattering[n][l]], name=f"scatter_{n}_l{l}_start")

    dx2, dx2_b = dres
    dact = _matmul(dx2_b, wt["w_down"][l], "nt", name=f"{tag}_dact", out_dtype=BF16)
    tok = scatter("w_down", _matmul(sv["act"], dx2_b, "tn", name=f"{tag}_dw_down", out_dtype=BF16))
    dup_g, dup_v, dwg, dwv, dbg, dbv = _ffn_act_bwd(sv["up0"], dact, wt["ffn_conv_w"][l], small["ffn_conv_b"][l][None] + tok + token,
                                                    name=f"{tag}_ffn_act")
    g["ffn_conv_w"] = jnp.concatenate([dwg, dwv], axis=1)
    g["ffn_conv_b"] = jnp.concatenate([dbg[0], dbv[0]])
    dh2 = _matmul(dup_g, wt["w_up_t"][l], "nn", name=f"{tag}_dh2_gate")
    dh2 = _matmul(dup_v, wt["w_up_t"][l], "nn", name=f"{tag}_dh2_val", b_k0=D_FF, residual=dh2)
    dw_up = _matmul(dup_g, sv["h2"], "tn", name=f"{tag}_dw_up_gate", out_dtype=BF16, rows=(2 * D_FF, 0))
    dw_up = _matmul(dup_v, sv["h2"], "tn", name=f"{tag}_dw_up_val", out_dtype=BF16, rows=(2 * D_FF, D_FF), into=dw_up)
    tok = scatter("w_up_t", dw_up)
    dx1, dx1_b, dg = _rmsnorm_bwd(sv["x1"], small["g_ffn"][l][None] + tok, dh2, dx2, name=f"{tag}_norm_ffn")
    g["g_ffn"] = dg[0]
    dy = _matmul(dx1_b, wt["w_out"][l], "nt", name=f"{tag}_dy")
    tok = scatter("w_out", _matmul(sv["y"], dx1_b, "tn", name=f"{tag}_dw_out", out_dtype=BF16))
    dya, dyb, dyc, dg = _combine_bwd(dy, sv["ya"], sv["yb"], sv["yc"], small["g_out"][l][None] + tok,
                                     name=f"{tag}_combine")
    g["g_out"] = dg[0]
    dq, dk, dv = _attn_bwd(sv["qkv"], sv["t_tot"], dyc, name=f"{tag}_attn")
    dp_b, g["sgu_w"], db, dg, dbeta = _sgu_bwd(sv["p_ab"], dyb, small["sgu_ln_g"][l][None], small["sgu_ln_b"][l][None],
                                               small["sgu_w"][l], sv["bias"], name=f"{tag}_sgu")
    g["sgu_b"] = db[:, :N_SGU_HEADS].T
    g["sgu_ln_g"], g["sgu_ln_b"] = dg[0], dbeta[0]
    dp_a, g["conv_w"], dcb, dg, dbeta = _mixer_a_bwd(sv["p_ab"], dya, wt["conv_w"][l], small["conv_b"][l][None],
                                                     small["conv_ln_g"][l][None], small["conv_ln_b"][l][None],
                                                     name=f"{tag}_mixer_a")
    g["conv_b"], g["conv_ln_g"], g["conv_ln_b"] = dcb[0], dg[0], dbeta[0]
    dp = jnp.concatenate([dp_a, dp_b, dq, dk.astype(BF16), dv.astype(BF16)], axis=1)
    tok = scatter("w_in_t", _matmul(dp, sv["h"], "tn", name=f"{tag}_dw_in", out_dtype=BF16))
    dh = _matmul(dp, wt["w_in_t"][l], "nn", name=f"{tag}_dh")
    dx, dx_b, dg = _rmsnorm_bwd(sv["x"], small["g_mix"][l][None] + tok, dh, dx1, name=f"{tag}_norm_mix")
    g["g_mix"] = dg[0]
    return (dx, dx_b), g


_BIG = ("w_in_t", "w_out", "w_up_t", "w_down")


def kernel(x, g_mix, w_in, conv_w, conv_b, conv_ln_g, conv_ln_b, sgu_ln_g, sgu_ln_b, sgu_w, sgu_b, g_out, w_out, g_ffn, w_up, ffn_conv_w, ffn_conv_b, w_down, g_final, loss_target, m_g_mix, m_w_in, m_conv_w, m_conv_b, m_conv_ln_g, m_conv_ln_b, m_sgu_ln_g, m_sgu_ln_b, m_sgu_w, m_sgu_b, m_g_out, m_w_out, m_g_ffn, m_w_up, m_ffn_conv_w, m_ffn_conv_b, m_w_down, m_g_final, v_g_mix, v_w_in, v_conv_w, v_conv_b, v_conv_ln_g, v_conv_ln_b, v_sgu_ln_g, v_sgu_ln_b, v_sgu_w, v_sgu_b, v_g_out, v_w_out, v_g_ffn, v_w_up, v_ffn_conv_w, v_ffn_conv_b, v_w_down, v_g_final):
    given = dict(locals())
    n_layers = g_mix.shape[0]
    layers = range(n_layers)
    small_sharded = [n for n, _ in _SMALL_SHARDED]
    replicated = [n for n, _ in _REPLICATED]
    small = {n: given[n] for n in replicated}

    filters = _SplitExchange(_pack([given[n] for n in small_sharded]), kind="gather_blocks", name="gather_filters")
    wt = {n: [None] * n_layers for n in _BIG}
    wt["w_in_t"][0] = _SplitExchange(w_in[0].T.astype(BF16), kind="gather_rows", name="gather_w_in_t_l0")
    tok = _SplitExchange.start([filters, wt["w_in_t"][0]], name="gather_first_start")
    w_in, w_out, w_up, w_down, tok = lax.optimization_barrier((w_in, w_out, w_up, w_down, tok))
    shard = {"w_in_t": [w_in[l].T.astype(BF16) for l in layers], "w_out": [w_out[l].astype(BF16) for l in layers],
             "w_up_t": [w_up[l].T.astype(BF16) for l in layers], "w_down": [w_down[l].astype(BF16) for l in layers]}
    later = [(n, l) for l in layers for n in _BIG if (n, l) != ("w_in_t", 0)]
    for n, l in later:
        wt[n][l] = _SplitExchange(shard[n][l], kind="gather_rows", name=f"gather_{n}_l{l}")
    small["g_mix"] = g_mix + tok + _SplitExchange.start([wt[n][l] for n, l in later], name="gather_weights_start")
    gathered_filters = filters.finish(small["g_mix"])
    for n, blocks in zip(small_sharded, _unpack(gathered_filters, [s for _, s in _SMALL_SHARDED], lead=(N_DEV,))):
        wt[n] = _join_last(blocks)

    xs = x[0]
    saved = []
    for l in layers:
        xs, sv = _layer_fwd(l, xs, wt, small)
        saved.append(sv)
    loss_tile, dx, dx_b, dgf = _loss_head(xs, g_final[None], loss_target[0], name="loss_head")
    dres = (dx, dx_b)
    scattering = {n: [None] * n_layers for n in _BIG}
    layered = [n for n in replicated if n not in ("g_final", "sgu_w")]
    slabs, sgu_w_parts = [None] * n_layers, [None] * n_layers
    tok = 0.0
    for l in reversed(layers):
        dres, g = _layer_bwd(l, dres, saved[l], wt, small, scattering, tok)
        own = _pack([_split_last(g[n]) for n in small_sharded], lead=(N_DEV,))
        shared = _pack([g[n] for n in layered] + [dgf[0]])
        slab = jnp.concatenate([own, jnp.broadcast_to(shared[None], (N_DEV,) + shared.shape)], axis=1)
        slabs[l] = _SplitExchange(slab, kind="scatter_blocks", name=f"scatter_small_grads_l{l}")
        sgu_w_parts[l] = _SplitExchange(g["sgu_w"].reshape(-1, LANES).astype(BF16), kind="gather_blocks",
                                        name=f"gather_sgu_w_grads_l{l}")
        tok = _SplitExchange.start([slabs[l], sgu_w_parts[l]], name=f"small_grads_l{l}_start")
    n_own = own.shape[1]

    after_backward = jnp.full((8, LANES), tok)
    received = {n: [scattering[n][l].finish(after_backward) for l in layers] for n in _BIG}
    out = {}

    def update(n, parts, transposed=False):
        turn = (lambda a: jnp.swapaxes(a, 1, 2)) if transposed else (lambda a: a)
        results = _adamw(parts, turn(given[n]), turn(given["m_" + n]), turn(given["v_" + n]), name=f"adamw_{n}")
        for pre, res in zip(("grad_", "delta_", "new_m_", "new_v_"), results):
            out[pre + n] = turn(res)
        return results[0][0, :8, :LANES]

    update("w_out", received["w_out"])
    update("w_down", received["w_down"])
    update("w_in", received["w_in_t"], transposed=True)
    big_updated = update("w_up", received["w_up_t"], transposed=True)

    as_rows = lambda a: a.reshape(n_layers, -1, LANES)
    results = _adamw([ex.finish(big_updated) for ex in sgu_w_parts], as_rows(sgu_w), as_rows(m_sgu_w), as_rows(v_sgu_w),
                     name="adamw_sgu_w")
    for pre, res in zip(("grad_", "delta_", "new_m_", "new_v_"), results):
        out[pre + "sgu_w"] = res.reshape(sgu_w.shape)

    per_layer_g_final = {pre: jnp.broadcast_to(given[pre + "g_final"], (n_layers,) + g_final.shape) for pre in ("", "m_", "v_")}
    stacks = [jnp.concatenate([_pack([given[pre + n] for n in small_sharded], lead=(n_layers,)),
                               _pack([given[pre + n] for n in layered] + [per_layer_g_final[pre]], lead=(n_layers,))], axis=1)
              for pre in ("", "m_", "v_")]
    results = _adamw([slabs[l].finish(big_updated) for l in layers], *stacks, name="adamw_small")
    for pre, res in zip(("grad_", "delta_", "new_m_", "new_v_"), results):
        unpacked = (_unpack(res[:, :n_own], [s[1:] for _, s in _SMALL_SHARDED], lead=(n_layers,))
                    + _unpack(res[:, n_own:], [s[1:] for n, s in _REPLICATED if n in layered] + [g_final.shape], lead=(n_layers,)))
        for n, a in zip(small_sharded + layered + ["g_final"], unpacked):
            out[pre + n] = a[0] if n == "g_final" else a

    loss = lax.psum(loss_tile[0, 0], ("x", "y", "c"))
    order = list(_WEIGHT_ORDER)
    return (loss, dres[0][None], *[out["grad_" + n] for n in order], *[out["delta_" + n] for n in order],
            *[out["new_m_" + n] for n in order], *[out["new_v_" + n] for n in order])


_WEIGHT_ORDER = ("g_mix", "w_in", "conv_w", "conv_b", "conv_ln_g", "conv_ln_b", "sgu_ln_g", "sgu_ln_b", "sgu_w", "sgu_b",
                 "g_out", "w_out", "g_ffn", "w_up", "ffn_conv_w", "ffn_conv_b", "w_down", "g_final")
```
